```python
import math
import jax, jax.numpy as jnp
from jax import lax
import numpy as np

D_MODEL = 2048
BATCH = 8
SEQ = 8192
DEPTH = 2

N_EVEN = (DEPTH + 1) // 2
N_ODD = DEPTH // 2
RMS_EPS = 1e-6

HG_HEADS = 8
HG_KDIM = 128
HG_VDIM = D_MODEL // 2 // HG_HEADS
HG_CHUNK = 64
HG_K_TOTAL = HG_HEADS * HG_KDIM
HG_V_TOTAL = HG_HEADS * HG_VDIM

SB_HEADS = 8
SB_HEAD_DIM = D_MODEL // 2 // SB_HEADS
SB_BLOCK = 128
SB_TOTAL = SB_HEADS * SB_HEAD_DIM

AB_SIZES = (HG_K_TOTAL, HG_K_TOTAL, HG_V_TOTAL, HG_V_TOTAL, SB_TOTAL, SB_TOTAL, SB_TOTAL)
AB_IN_DIM = sum(AB_SIZES)
AB_SPLITS = tuple(int(v) for v in np.cumsum(AB_SIZES)[:-1])
MIX_WIDTH = HG_V_TOTAL + SB_TOTAL

POOL_WINDOWS = (2, 4, 8, 16)
N_POOL_GROUPS = len(POOL_WINDOWS)
POOL_GROUP = D_MODEL // N_POOL_GROUPS

FFN_HIDDEN = -(-8 * D_MODEL // (3 * 256)) * 256

kernel_name = 'hgrn2_stickbreak_pool_hybrid'


def rms_norm(x, gain):
    x32 = x.astype(jnp.float32)
    y = x32 * lax.rsqrt(jnp.mean(x32 * x32, axis=-1, keepdims=True) + RMS_EPS)
    return (y * gain.astype(jnp.float32)).astype(x.dtype)


def swiglu(h, w_gate, w_up, w_down):
    return (jax.nn.silu(h @ w_gate) * (h @ w_up)) @ w_down


def hgrn2_chunked(q, log_f, k, v):
    b, s, h, dk = q.shape
    dv = v.shape[-1]
    nc = s // HG_CHUNK

    def to_chunks(t):
        return t.reshape(b, nc, HG_CHUNK, h, t.shape[-1]).transpose(1, 0, 3, 2, 4)

    causal = jnp.tril(jnp.ones((HG_CHUNK, HG_CHUNK), dtype=bool))

    def step(state, inp):
        qc, gc, kc, vc = inp
        g_cum = jnp.cumsum(gc, axis=-2)
        o_inter = jnp.einsum('bhtk,bhkv->bhtv', qc * jnp.exp(g_cum), state)
        diff = g_cum[:, :, :, None, :] - g_cum[:, :, None, :, :]
        decay = jnp.exp(jnp.where(causal[:, :, None], diff, -jnp.inf))
        att = jnp.einsum('bhtk,bhtsk,bhsk->bhts', qc, decay, kc)
        o_intra = jnp.einsum('bhts,bhsv->bhtv', att, vc)
        g_last = g_cum[:, :, -1:, :]
        new_state = (jnp.exp(g_last[:, :, 0, :])[..., None] * state
                     + jnp.einsum('bhsk,bhsv->bhkv', kc * jnp.exp(g_last - g_cum), vc))
        return new_state, o_inter + o_intra

    s0 = jnp.zeros((b, h, dk, dv), jnp.float32)
    _, o = lax.scan(step, s0, (to_chunks(q), to_chunks(log_f), to_chunks(k), to_chunks(v)))
    return o.transpose(1, 0, 3, 2, 4).reshape(b, s, h, dv)


def stick_breaking_attention(q, k, v):
    b, h, s, d = q.shape
    nb = s // SB_BLOCK
    scale = 1.0 / math.sqrt(d)
    q_blocks = q.reshape(b, h, nb, SB_BLOCK, d).transpose(2, 0, 1, 3, 4)
    key_pos = jnp.arange(s)

    def block(args):
        qb, start = args
        z = jnp.einsum('bhqd,bhsd->bhqs', qb, k).astype(jnp.float32) * scale
        query_pos = start + jnp.arange(SB_BLOCK)
        mask = key_pos[None, :] < query_pos[:, None]
        log_keep = jnp.where(mask, jax.nn.log_sigmoid(-z), 0.0)
        later = lax.cumsum(log_keep, axis=3, reverse=True) - log_keep
        weights = jnp.where(mask, jnp.exp(jax.nn.log_sigmoid(z) + later), 0.0)
        return jnp.einsum('bhqs,bhsd->bhqd', weights.astype(v.dtype), v)

    starts = jnp.arange(nb, dtype=jnp.int32) * SB_BLOCK
    o = lax.map(block, (q_blocks, starts))
    return o.transpose(1, 2, 0, 3, 4).reshape(b, h, s, d)


def hybrid_ab_mixer(h, w_in, lower_bound, hg_out_norm, w_out):
    b, s, _ = h.shape
    proj = h @ w_in
    qa, fa, ia, ga, qb, kb, vb = jnp.split(proj, AB_SPLITS, axis=-1)

    f = lower_bound + (1.0 - lower_bound) * jax.nn.sigmoid(fa.astype(jnp.float32))
    heads_k = lambda t: t.reshape(b, s, HG_HEADS, HG_KDIM)
    o_a = hgrn2_chunked(heads_k(jax.nn.silu(qa.astype(jnp.float32))),
                        heads_k(jnp.log(f)),
                        heads_k(1.0 - f),
                        ia.astype(jnp.float32).reshape(b, s, HG_HEADS, HG_VDIM))
    o_a = rms_norm(o_a, hg_out_norm) * jax.nn.silu(ga.astype(jnp.float32)).reshape(b, s, HG_HEADS, HG_VDIM)
    o_a = o_a.reshape(b, s, HG_V_TOTAL).astype(h.dtype)

    heads_b = lambda t: t.reshape(b, s, SB_HEADS, SB_HEAD_DIM).transpose(0, 2, 1, 3)
    o_b = stick_breaking_attention(heads_b(qb), heads_b(kb), heads_b(vb))
    o_b = o_b.transpose(0, 2, 1, 3).reshape(b, s, SB_TOTAL).astype(h.dtype)

    return jnp.concatenate([o_a, o_b], axis=-1) @ w_out


def multiscale_pool_mixer(h, w_groups, scale):
    b, s, d = h.shape
    hg = h.astype(jnp.float32).reshape(b, s, N_POOL_GROUPS, POOL_GROUP)
    prefix = jnp.concatenate([jnp.zeros((b, 1, N_POOL_GROUPS, POOL_GROUP), jnp.float32),
                              jnp.cumsum(hg, axis=1)], axis=1)
    pos = jnp.arange(s)
    outs = []
    for gi, w in enumerate(POOL_WINDOWS):
        lo = jnp.maximum(pos + 1 - w, 0)
        window_sum = prefix[:, 1:, gi] - prefix[:, lo, gi]
        count = jnp.minimum(pos + 1, w).astype(jnp.float32)
        outs.append(window_sum / count[None, :, None] - hg[:, :, gi])
    pooled = jnp.stack(outs, axis=2).astype(h.dtype)
    mixed = jnp.einsum('bsgc,gcd->bsgd', pooled, w_groups)
    return mixed.reshape(b, s, d) * scale


def _fwd_setup_inputs(seed: int = 0) -> dict:
    key = jax.random.key(seed)
    ks = jax.random.split(key, 14)
    f32 = jnp.float32
    nrm = lambda k, shape, fan_in: jax.random.normal(k, shape, f32) * (fan_in ** -0.5)
    return {
        'x': jax.random.normal(ks[0], (BATCH, SEQ, D_MODEL), f32),
        'mix_norm': 1.0 + 0.02 * jax.random.normal(ks[1], (DEPTH, D_MODEL), f32),
        'ffn_norm': 1.0 + 0.02 * jax.random.normal(ks[2], (DEPTH, D_MODEL), f32),
        'final_norm': 1.0 + 0.02 * jax.random.normal(ks[3], (D_MODEL,), f32),
        'ab_w_in': nrm(ks[4], (N_EVEN, D_MODEL, AB_IN_DIM), D_MODEL),
        'lb_logits': 0.1 * jax.random.normal(ks[5], (N_EVEN + 1, HG_K_TOTAL), f32),
        'hg_out_norm': 1.0 + 0.02 * jax.random.normal(ks[6], (N_EVEN, HG_VDIM), f32),
        'ab_w_out': nrm(ks[7], (N_EVEN, MIX_WIDTH, D_MODEL), MIX_WIDTH),
        'pool_w': nrm(ks[8], (N_ODD, N_POOL_GROUPS, POOL_GROUP, POOL_GROUP), POOL_GROUP),
        'pool_scale': 1.0 + 0.02 * jax.random.normal(ks[9], (N_ODD, D_MODEL), f32),
        'ffn_w_gate': nrm(ks[10], (DEPTH, D_MODEL, FFN_HIDDEN), D_MODEL),
        'ffn_w_up': nrm(ks[11], (DEPTH, D_MODEL, FFN_HIDDEN), D_MODEL),
        'ffn_w_down': nrm(ks[12], (DEPTH, FFN_HIDDEN, D_MODEL), FFN_HIDDEN),
    }


def _fwd_reference(x, mix_norm, ffn_norm, final_norm, ab_w_in, lb_logits, hg_out_norm, ab_w_out,
              pool_w, pool_scale, ffn_w_gate, ffn_w_up, ffn_w_down):
    lower_bounds = jnp.cumsum(jax.nn.softmax(lb_logits.astype(jnp.float32), axis=0), axis=0)
    for layer in range(DEPTH):
        h = rms_norm(x, mix_norm[layer])
        i = layer // 2
        if layer % 2 == 0:
            mix = hybrid_ab_mixer(h, ab_w_in[i], lower_bounds[i], hg_out_norm[i], ab_w_out[i])
        else:
            mix = multiscale_pool_mixer(h, pool_w[i], pool_scale[i])
        x = x + mix.astype(x.dtype)
        h = rms_norm(x, ffn_norm[layer])
        x = x + swiglu(h, ffn_w_gate[layer], ffn_w_up[layer], ffn_w_down[layer]).astype(x.dtype)
    return rms_norm(x, final_norm)


import jax as _jax
import jax.numpy as _jnp

TWIN_FORMAT = 'train_step'
FWD_PARAMS = ['x', 'mix_norm', 'ffn_norm', 'final_norm', 'ab_w_in', 'lb_logits', 'hg_out_norm', 'ab_w_out', 'pool_w', 'pool_scale', 'ffn_w_gate', 'ffn_w_up', 'ffn_w_down']
TWIN_WEIGHTS = ['mix_norm', 'ffn_norm', 'final_norm', 'ab_w_in', 'lb_logits', 'hg_out_norm', 'ab_w_out', 'pool_w', 'pool_scale', 'ffn_w_gate', 'ffn_w_up', 'ffn_w_down']
TWIN_DIFF_INPUT = 'x'
TWIN_INPUTS = ['x', 'mix_norm', 'ffn_norm', 'final_norm', 'ab_w_in', 'lb_logits', 'hg_out_norm', 'ab_w_out', 'pool_w', 'pool_scale', 'ffn_w_gate', 'ffn_w_up', 'ffn_w_down', 'loss_target', 'm_mix_norm', 'm_ffn_norm', 'm_final_norm', 'm_ab_w_in', 'm_lb_logits', 'm_hg_out_norm', 'm_ab_w_out', 'm_pool_w', 'm_pool_scale', 'm_ffn_w_gate', 'm_ffn_w_up', 'm_ffn_w_down', 'v_mix_norm', 'v_ffn_norm', 'v_final_norm', 'v_ab_w_in', 'v_lb_logits', 'v_hg_out_norm', 'v_ab_w_out', 'v_pool_w', 'v_pool_scale', 'v_ffn_w_gate', 'v_ffn_w_up', 'v_ffn_w_down']
TWIN_OUTPUTS = ['loss', 'grad_x', 'grad_mix_norm', 'grad_ffn_norm', 'grad_final_norm', 'grad_ab_w_in', 'grad_lb_logits', 'grad_hg_out_norm', 'grad_ab_w_out', 'grad_pool_w', 'grad_pool_scale', 'grad_ffn_w_gate', 'grad_ffn_w_up', 'grad_ffn_w_down', 'delta_mix_norm', 'delta_ffn_norm', 'delta_final_norm', 'delta_ab_w_in', 'delta_lb_logits', 'delta_hg_out_norm', 'delta_ab_w_out', 'delta_pool_w', 'delta_pool_scale', 'delta_ffn_w_gate', 'delta_ffn_w_up', 'delta_ffn_w_down', 'new_m_mix_norm', 'new_m_ffn_norm', 'new_m_final_norm', 'new_m_ab_w_in', 'new_m_lb_logits', 'new_m_hg_out_norm', 'new_m_ab_w_out', 'new_m_pool_w', 'new_m_pool_scale', 'new_m_ffn_w_gate', 'new_m_ffn_w_up', 'new_m_ffn_w_down', 'new_v_mix_norm', 'new_v_ffn_norm', 'new_v_final_norm', 'new_v_ab_w_in', 'new_v_lb_logits', 'new_v_hg_out_norm', 'new_v_ab_w_out', 'new_v_pool_w', 'new_v_pool_scale', 'new_v_ffn_w_gate', 'new_v_ffn_w_up', 'new_v_ffn_w_down']
TWIN_LEAF_KINDS = {'loss': 'loss', 'grad_x': 'grad_x', 'grad_mix_norm': 'grad_w', 'grad_ffn_norm': 'grad_w', 'grad_final_norm': 'grad_w', 'grad_ab_w_in': 'grad_w', 'grad_lb_logits': 'grad_w', 'grad_hg_out_norm': 'grad_w', 'grad_ab_w_out': 'grad_w', 'grad_pool_w': 'grad_w', 'grad_pool_scale': 'grad_w', 'grad_ffn_w_gate': 'grad_w', 'grad_ffn_w_up': 'grad_w', 'grad_ffn_w_down': 'grad_w', 'delta_mix_norm': 'delta_w', 'delta_ffn_norm': 'delta_w', 'delta_final_norm': 'delta_w', 'delta_ab_w_in': 'delta_w', 'delta_lb_logits': 'delta_w', 'delta_hg_out_norm': 'delta_w', 'delta_ab_w_out': 'delta_w', 'delta_pool_w': 'delta_w', 'delta_pool_scale': 'delta_w', 'delta_ffn_w_gate': 'delta_w', 'delta_ffn_w_up': 'delta_w', 'delta_ffn_w_down': 'delta_w', 'new_m_mix_norm': 'new_m', 'new_m_ffn_norm': 'new_m', 'new_m_final_norm': 'new_m', 'new_m_ab_w_in': 'new_m', 'new_m_lb_logits': 'new_m', 'new_m_hg_out_norm': 'new_m', 'new_m_ab_w_out': 'new_m', 'new_m_pool_w': 'new_m', 'new_m_pool_scale': 'new_m', 'new_m_ffn_w_gate': 'new_m', 'new_m_ffn_w_up': 'new_m', 'new_m_ffn_w_down': 'new_m', 'new_v_mix_norm': 'new_v', 'new_v_ffn_norm': 'new_v', 'new_v_final_norm': 'new_v', 'new_v_ab_w_in': 'new_v', 'new_v_lb_logits': 'new_v', 'new_v_hg_out_norm': 'new_v', 'new_v_ab_w_out': 'new_v', 'new_v_pool_w': 'new_v', 'new_v_pool_scale': 'new_v', 'new_v_ffn_w_gate': 'new_v', 'new_v_ffn_w_up': 'new_v', 'new_v_ffn_w_down': 'new_v'}


def _forward(args):
    return _fwd_reference(*[args[k] for k in FWD_PARAMS])


def _output_shape():
    def fwd():
        inp = _fwd_setup_inputs(0)
        return _fwd_reference(*[inp[k] for k in FWD_PARAMS])
    out = _jax.eval_shape(fwd)
    return out.shape, out.dtype

N_MICROBATCH = 1
ADAM_LR = 0.001
ADAM_B1 = 0.9
ADAM_B2 = 0.999
ADAM_EPS = 1e-08
ADAM_WD = 0.01
ADAM_STEP = 10
PER_EXAMPLE_BATCH_AXIS = {'x': 0, 'loss_target': 0}
SHARED_INPUTS = []
_WEIGHT_DTYPES = {'mix_norm': _jnp.float32, 'ffn_norm': _jnp.float32, 'final_norm': _jnp.float32, 'ab_w_in': _jnp.float32, 'lb_logits': _jnp.float32, 'hg_out_norm': _jnp.float32, 'ab_w_out': _jnp.float32, 'pool_w': _jnp.float32, 'pool_scale': _jnp.float32, 'ffn_w_gate': _jnp.float32, 'ffn_w_up': _jnp.float32, 'ffn_w_down': _jnp.float32}
MOMENT_SCALE = {'mix_norm': 9.451026e-02, 'ffn_norm': 7.602384e-02, 'final_norm': 3.204288e+01, 'ab_w_in': 5.466756e-02, 'lb_logits': 6.989879e-03, 'hg_out_norm': 2.063747e-01, 'ab_w_out': 7.731737e-02, 'pool_w': 7.451480e-02, 'pool_scale': 2.554693e-01, 'ffn_w_gate': 3.301782e-02, 'ffn_w_up': 3.195655e-02, 'ffn_w_down': 5.296284e-02}


def _to_microbatches(a, axis):
    t = _jnp.moveaxis(a, axis, 0)
    t = t.reshape((N_MICROBATCH, t.shape[0] // N_MICROBATCH) + t.shape[1:])
    return _jnp.moveaxis(t, 1, axis + 1)


def setup_inputs(seed: int = 0) -> dict:
    inp = _fwd_setup_inputs(seed)
    key = _jax.random.fold_in(_jax.random.key(seed), 7919)
    shape, _ = _output_shape()
    out = dict(inp)
    out["loss_target"] = _jax.random.normal(_jax.random.fold_in(key, 0), shape, _jnp.float32)
    for i, name in enumerate(TWIN_WEIGHTS):
        w = inp[name].astype(_jnp.float32)
        if MOMENT_SCALE is None:
            s = _jnp.sqrt(_jnp.mean(_jnp.square(w)) + 1e-30)
        else:
            s = MOMENT_SCALE[name]
        km, kv = _jax.random.split(_jax.random.fold_in(key, i + 1))
        out[name] = w
        out["m_" + name] = s * _jax.random.normal(km, w.shape, _jnp.float32)
        out["v_" + name] = (s * s) * _jax.random.uniform(kv, w.shape, _jnp.float32, 0.5, 1.5)
    if N_MICROBATCH > 1:
        for name, axis in PER_EXAMPLE_BATCH_AXIS.items():
            out[name] = _to_microbatches(out[name], axis)
    return {'x': out['x'], 'mix_norm': out['mix_norm'], 'ffn_norm': out['ffn_norm'], 'final_norm': out['final_norm'], 'ab_w_in': out['ab_w_in'], 'lb_logits': out['lb_logits'], 'hg_out_norm': out['hg_out_norm'], 'ab_w_out': out['ab_w_out'], 'pool_w': out['pool_w'], 'pool_scale': out['pool_scale'], 'ffn_w_gate': out['ffn_w_gate'], 'ffn_w_up': out['ffn_w_up'], 'ffn_w_down': out['ffn_w_down'], 'loss_target': out['loss_target'], 'm_mix_norm': out['m_mix_norm'], 'm_ffn_norm': out['m_ffn_norm'], 'm_final_norm': out['m_final_norm'], 'm_ab_w_in': out['m_ab_w_in'], 'm_lb_logits': out['m_lb_logits'], 'm_hg_out_norm': out['m_hg_out_norm'], 'm_ab_w_out': out['m_ab_w_out'], 'm_pool_w': out['m_pool_w'], 'm_pool_scale': out['m_pool_scale'], 'm_ffn_w_gate': out['m_ffn_w_gate'], 'm_ffn_w_up': out['m_ffn_w_up'], 'm_ffn_w_down': out['m_ffn_w_down'], 'v_mix_norm': out['v_mix_norm'], 'v_ffn_norm': out['v_ffn_norm'], 'v_final_norm': out['v_final_norm'], 'v_ab_w_in': out['v_ab_w_in'], 'v_lb_logits': out['v_lb_logits'], 'v_hg_out_norm': out['v_hg_out_norm'], 'v_ab_w_out': out['v_ab_w_out'], 'v_pool_w': out['v_pool_w'], 'v_pool_scale': out['v_pool_scale'], 'v_ffn_w_gate': out['v_ffn_w_gate'], 'v_ffn_w_up': out['v_ffn_w_up'], 'v_ffn_w_down': out['v_ffn_w_down']}


def _loss(weights, diff, rest, loss_target):
    with _jax.named_scope("forward"):
        args = {**rest, TWIN_DIFF_INPUT: diff, **{k: w.astype(_WEIGHT_DTYPES[k]) for k, w in weights.items()}}
        y = _forward(args)
    with _jax.named_scope("loss_head"):
        err = _jnp.square(y.astype(_jnp.float32) - loss_target)
        return 0.5 * _jnp.sum(_jnp.mean(err, axis=-1)) if err.ndim else 0.5 * err


def _adamw(w, g, m, v):
    m = ADAM_B1 * m + (1.0 - ADAM_B1) * g
    v = ADAM_B2 * v + (1.0 - ADAM_B2) * _jnp.square(g)
    m_hat = m / (1.0 - ADAM_B1 ** ADAM_STEP)
    v_hat = v / (1.0 - ADAM_B2 ** ADAM_STEP)
    delta = -ADAM_LR * (m_hat / (_jnp.sqrt(v_hat) + ADAM_EPS) + ADAM_WD * w)
    return delta, m, v


def reference(x, mix_norm, ffn_norm, final_norm, ab_w_in, lb_logits, hg_out_norm, ab_w_out, pool_w, pool_scale, ffn_w_gate, ffn_w_up, ffn_w_down, loss_target, m_mix_norm, m_ffn_norm, m_final_norm, m_ab_w_in, m_lb_logits, m_hg_out_norm, m_ab_w_out, m_pool_w, m_pool_scale, m_ffn_w_gate, m_ffn_w_up, m_ffn_w_down, v_mix_norm, v_ffn_norm, v_final_norm, v_ab_w_in, v_lb_logits, v_hg_out_norm, v_ab_w_out, v_pool_w, v_pool_scale, v_ffn_w_gate, v_ffn_w_up, v_ffn_w_down):
    given = dict(x=x, mix_norm=mix_norm, ffn_norm=ffn_norm, final_norm=final_norm, ab_w_in=ab_w_in, lb_logits=lb_logits, hg_out_norm=hg_out_norm, ab_w_out=ab_w_out, pool_w=pool_w, pool_scale=pool_scale, ffn_w_gate=ffn_w_gate, ffn_w_up=ffn_w_up, ffn_w_down=ffn_w_down, loss_target=loss_target, m_mix_norm=m_mix_norm, m_ffn_norm=m_ffn_norm, m_final_norm=m_final_norm, m_ab_w_in=m_ab_w_in, m_lb_logits=m_lb_logits, m_hg_out_norm=m_hg_out_norm, m_ab_w_out=m_ab_w_out, m_pool_w=m_pool_w, m_pool_scale=m_pool_scale, m_ffn_w_gate=m_ffn_w_gate, m_ffn_w_up=m_ffn_w_up, m_ffn_w_down=m_ffn_w_down, v_mix_norm=v_mix_norm, v_ffn_norm=v_ffn_norm, v_final_norm=v_final_norm, v_ab_w_in=v_ab_w_in, v_lb_logits=v_lb_logits, v_hg_out_norm=v_hg_out_norm, v_ab_w_out=v_ab_w_out, v_pool_w=v_pool_w, v_pool_scale=v_pool_scale, v_ffn_w_gate=v_ffn_w_gate, v_ffn_w_up=v_ffn_w_up, v_ffn_w_down=v_ffn_w_down)
    weights = {n: given[n] for n in TWIN_WEIGHTS}
    shared = {n: given[n] for n in SHARED_INPUTS}
    per_example = {n: given[n] for n in ['x']}
    grad_fn = _jax.value_and_grad(_loss, argnums=(0, 1))

    def one_microbatch(ex, loss_target):
        ex = dict(ex)
        diff = ex.pop(TWIN_DIFF_INPUT)
        return grad_fn(weights, diff, {**shared, **ex}, loss_target)

    if N_MICROBATCH == 1:
        loss, (grad_w, grad_x) = one_microbatch(per_example, given["loss_target"])
    else:
        def body(carry, xs):
            loss_sum, grad_sum = carry
            l_k, (gw_k, gx_k) = one_microbatch(xs[0], xs[1])
            with _jax.named_scope("update"):
                return (loss_sum + l_k, _jax.tree.map(_jnp.add, grad_sum, gw_k)), gx_k

        init = (_jnp.zeros((), _jnp.float32), _jax.tree.map(_jnp.zeros_like, weights))
        (loss, grad_w), grad_x = _jax.lax.scan(body, init, (per_example, given["loss_target"]))
    with _jax.named_scope("update"):
        delta_w, new_m, new_v = {}, {}, {}
        for n in TWIN_WEIGHTS:
            delta_w[n], new_m[n], new_v[n] = _adamw(weights[n], grad_w[n], given["m_" + n], given["v_" + n])
    return (loss, grad_x, *[grad_w[n] for n in TWIN_WEIGHTS], *[delta_w[n] for n in TWIN_WEIGHTS],
            *[new_m[n] for n in TWIN_WEIGHTS], *[new_v[n] for n in TWIN_WEIGHTS])
```

```python
import functools
import math

import numpy as np
import jax
import jax.numpy as jnp
from jax import lax
from jax.experimental import pallas as pl
from jax.experimental.pallas import tpu as pltpu

F32 = jnp.float32
BF16 = jnp.bfloat16

N_DEV = 8
RMS_EPS = 1e-6
HEAD = 128
HG_CHUNK = 64
POOL_WINDOWS = (2, 4, 8, 16)
POOL_HALO = 16
ADAM_LR, ADAM_B1, ADAM_B2, ADAM_EPS, ADAM_WD, ADAM_STEP = 0.001, 0.9, 0.999, 1e-08, 0.01, 10
VMEM_LIMIT_BYTES = 60 * 1024 * 1024
MESH = pl.DeviceIdType.MESH


def _params(sem):
    return pltpu.CompilerParams(dimension_semantics=sem, vmem_limit_bytes=VMEM_LIMIT_BYTES)


def _sigmoid(x):
    return 1.0 / (1.0 + jnp.exp(-x))


def rms_fwd(x, gain, out_dtype, ts=512):
    S, D = x.shape

    def body(x_ref, g_ref, h_ref, r_ref):
        xv = x_ref[...]
        r = lax.rsqrt(jnp.mean(xv * xv, axis=-1, keepdims=True) + RMS_EPS)
        h_ref[...] = ((xv * r) * g_ref[...]).astype(h_ref.dtype)
        r_ref[...] = r

    return pl.pallas_call(
        body, grid=(S // ts,), name="rms_fwd",
        in_specs=[pl.BlockSpec((ts, D), lambda i: (i, 0)), pl.BlockSpec((1, D), lambda i: (0, 0))],
        out_specs=[pl.BlockSpec((ts, D), lambda i: (i, 0)), pl.BlockSpec((ts, 1), lambda i: (i, 0))],
        out_shape=[jax.ShapeDtypeStruct((S, D), out_dtype), jax.ShapeDtypeStruct((S, 1), F32)],
        compiler_params=_params(("arbitrary",)),
    )(x, gain)


def rms_bwd(dh, x, r, gain, dres, ts=512):
    S, D = x.shape

    def body(dh_ref, x_ref, r_ref, g_ref, dres_ref, dx_ref, dxb_ref, dg_ref):
        i = pl.program_id(0)
        rr = r_ref[...]
        xh = x_ref[...] * rr
        dhv = dh_ref[...]
        dxh = dhv * g_ref[...]
        dx = dres_ref[...] + rr * (dxh - xh * jnp.mean(dxh * xh, axis=-1, keepdims=True))
        dx_ref[...] = dx
        dxb_ref[...] = dx.astype(BF16)
        part = jnp.sum(dhv * xh, axis=0, keepdims=True)

        @pl.when(i == 0)
        def _():
            dg_ref[...] = part

        @pl.when(i > 0)
        def _():
            dg_ref[...] += part

    row = pl.BlockSpec((ts, D), lambda i: (i, 0))
    vec = pl.BlockSpec((1, D), lambda i: (0, 0))
    return pl.pallas_call(
        body, grid=(S // ts,), name="rms_bwd",
        in_specs=[row, row, pl.BlockSpec((ts, 1), lambda i: (i, 0)), vec, row],
        out_specs=[row, row, vec],
        out_shape=[jax.ShapeDtypeStruct((S, D), F32), jax.ShapeDtypeStruct((S, D), BF16),
                   jax.ShapeDtypeStruct((1, D), F32)],
        compiler_params=_params(("arbitrary",)),
    )(dh, x, r, gain, dres)


def loss_and_final_bwd(x, gain, target, ts=512):
    S, D = x.shape

    def body(x_ref, g_ref, t_ref, loss_ref, dx_ref, dxb_ref, dg_ref):
        i = pl.program_id(0)
        xv = x_ref[...]
        rr = lax.rsqrt(jnp.mean(xv * xv, axis=-1, keepdims=True) + RMS_EPS)
        xh = xv * rr
        err = xh * g_ref[...] - t_ref[...]
        part_loss = 0.5 * jnp.sum(jnp.mean(err * err, axis=-1, keepdims=True))
        dy = err / D
        dxh = dy * g_ref[...]
        dx = rr * (dxh - xh * jnp.mean(dxh * xh, axis=-1, keepdims=True))
        dx_ref[...] = dx
        dxb_ref[...] = dx.astype(BF16)
        part = jnp.sum(dy * xh, axis=0, keepdims=True)

        @pl.when(i == 0)
        def _():
            dg_ref[...] = part
            loss_ref[...] = jnp.zeros_like(loss_ref) + part_loss

        @pl.when(i > 0)
        def _():
            dg_ref[...] += part
            loss_ref[...] += part_loss

    row = pl.BlockSpec((ts, D), lambda i: (i, 0))
    vec = pl.BlockSpec((1, D), lambda i: (0, 0))
    return pl.pallas_call(
        body, grid=(S // ts,), name="loss_final",
        in_specs=[row, vec, row],
        out_specs=[pl.BlockSpec((8, 128), lambda i: (0, 0)), row, row, vec],
        out_shape=[jax.ShapeDtypeStruct((8, 128), F32), jax.ShapeDtypeStruct((S, D), F32),
                   jax.ShapeDtypeStruct((S, D), BF16), jax.ShapeDtypeStruct((1, D), F32)],
        compiler_params=_params(("arbitrary",)),
    )(x, gain, target)


def matmul(name, a_ops, b_ops, *, grid, a_spec, b_spec, out_spec, out_shape, out_dtypes, acc_shape,
           trans_a=False, trans_b=False, res=None, res_spec=None):
    n_pairs = len(a_ops)
    n_out = len(out_dtypes)
    nk = grid[-1]
    kaxis = len(grid) - 1
    dn = (((0,) if trans_a else (1,), (1,) if trans_b else (0,)), ((), ()))

    def body(*refs):
        a_refs = refs[:n_pairs]
        b_refs = refs[n_pairs:2 * n_pairs]
        pos = 2 * n_pairs
        res_ref = None
        if res is not None:
            res_ref = refs[pos]
            pos += 1
        out_refs = refs[pos:pos + n_out]
        acc_ref = refs[pos + n_out]
        k = pl.program_id(kaxis)
        part = None
        for ar, br in zip(a_refs, b_refs):
            d = lax.dot_general(ar[...].astype(BF16), br[...].astype(BF16), dn, preferred_element_type=F32)
            part = d if part is None else part + d

        def finish(val):
            if res_ref is not None:
                val = val + res_ref[...]
            for o in out_refs:
                o[...] = val.astype(o.dtype)

        if nk == 1:
            finish(part)
        else:
            @pl.when(k == 0)
            def _():
                acc_ref[...] = part

            @pl.when(k > 0)
            def _():
                acc_ref[...] += part

            @pl.when(k == nk - 1)
            def _():
                finish(acc_ref[...])

    in_specs = [a_spec] * n_pairs + [b_spec] * n_pairs
    operands = list(a_ops) + list(b_ops)
    if res is not None:
        in_specs.append(res_spec)
        operands.append(res)
    return pl.pallas_call(
        body, grid=grid, name=name, in_specs=in_specs,
        out_specs=[out_spec] * n_out,
        out_shape=[jax.ShapeDtypeStruct(out_shape, dt) for dt in out_dtypes],
        scratch_shapes=[pltpu.VMEM(acc_shape, F32)],
        compiler_params=_params(("arbitrary",) * len(grid)),
    )(*operands)


def ffn_gate_up(h, wg, wu, tm=512):
    S, D = h.shape
    nb = wg.shape[2]

    def body(h_ref, wg_ref, wu_ref, g_ref, u_ref, a_ref):
        hv = h_ref[...]
        g = jnp.dot(hv, wg_ref[...], preferred_element_type=F32)
        u = jnp.dot(hv, wu_ref[...], preferred_element_type=F32)
        g_ref[...] = g
        u_ref[...] = u
        a_ref[...] = (g * _sigmoid(g) * u).astype(BF16)

    wspec = pl.BlockSpec((None, D, nb), lambda j, i: (j, 0, 0))
    ospec = pl.BlockSpec((None, tm, nb), lambda j, i: (j, i, 0))
    return pl.pallas_call(
        body, grid=(N_DEV, S // tm), name="ffn_gate_up",
        in_specs=[pl.BlockSpec((tm, D), lambda j, i: (i, 0)), wspec, wspec],
        out_specs=[ospec, ospec, ospec],
        out_shape=[jax.ShapeDtypeStruct((N_DEV, S, nb), F32), jax.ShapeDtypeStruct((N_DEV, S, nb), F32),
                   jax.ShapeDtypeStruct((N_DEV, S, nb), BF16)],
        compiler_params=_params(("arbitrary", "arbitrary")),
    )(h, wg, wu)


def ffn_bwd_hidden(dy, wd, g, u, tm=512):
    S, D = dy.shape
    nb = wd.shape[1]

    def body(dy_ref, wd_ref, g_ref, u_ref, dg_ref, du_ref):
        da = lax.dot_general(dy_ref[...], wd_ref[...], (((1,), (1,)), ((), ())), preferred_element_type=F32)
        gv = g_ref[...]
        s = _sigmoid(gv)
        du_ref[...] = (da * (gv * s)).astype(BF16)
        dg_ref[...] = (da * u_ref[...] * (s * (1.0 + gv * (1.0 - s)))).astype(BF16)

    hspec = pl.BlockSpec((None, tm, nb), lambda j, i: (j, i, 0))
    return pl.pallas_call(
        body, grid=(N_DEV, S // tm), name="ffn_bwd_hidden",
        in_specs=[pl.BlockSpec((tm, D), lambda j, i: (i, 0)), pl.BlockSpec((None, nb, D), lambda j, i: (j, 0, 0)),
                  hspec, hspec],
        out_specs=[hspec, hspec],
        out_shape=[jax.ShapeDtypeStruct((N_DEV, S, nb), BF16), jax.ShapeDtypeStruct((N_DEV, S, nb), BF16)],
        compiler_params=_params(("arbitrary", "arbitrary")),
    )(dy, wd, g, u)


def ffn_forward(h, xres, wg, wu, wd, tm=512):
    S, D = h.shape
    nb = wg.shape[2]
    g, u, a = ffn_gate_up(h, wg, wu)
    (xo,) = matmul(
        "ffn_down", [a], [wd], grid=(S // tm, N_DEV),
        a_spec=pl.BlockSpec((None, tm, nb), lambda i, j: (j, i, 0)),
        b_spec=pl.BlockSpec((None, nb, D), lambda i, j: (j, 0, 0)),
        out_spec=pl.BlockSpec((tm, D), lambda i, j: (i, 0)), out_shape=(S, D), out_dtypes=[F32],
        acc_shape=(tm, D), res=xres, res_spec=pl.BlockSpec((tm, D), lambda i, j: (i, 0)))
    return xo, (g, u, a)


def ffn_backward(dy_b, h, saved, wg, wu, wd, tm=512, tk=512):
    S, D = h.shape
    nb = wg.shape[2]
    g, u, a = saved
    dg, du = ffn_bwd_hidden(dy_b, wd, g, u)
    (dh,) = matmul(
        "ffn_dh", [dg, du], [wg, wu], grid=(S // tm, N_DEV),
        a_spec=pl.BlockSpec((None, tm, nb), lambda i, j: (j, i, 0)),
        b_spec=pl.BlockSpec((None, D, nb), lambda i, j: (j, 0, 0)),
        out_spec=pl.BlockSpec((tm, D), lambda i, j: (i, 0)), out_shape=(S, D), out_dtypes=[F32],
        acc_shape=(tm, D), trans_b=True)

    def wgrad_in(name, dhid):
        (dw,) = matmul(
            name, [h], [dhid], grid=(N_DEV, S // tk),
            a_spec=pl.BlockSpec((tk, D), lambda j, k: (k, 0)),
            b_spec=pl.BlockSpec((None, tk, nb), lambda j, k: (j, k, 0)),
            out_spec=pl.BlockSpec((None, D, nb), lambda j, k: (j, 0, 0)), out_shape=(N_DEV, D, nb),
            out_dtypes=[BF16], acc_shape=(D, nb), trans_a=True)
        return dw

    dwg = wgrad_in("ffn_dwg", dg)
    dwu = wgrad_in("ffn_dwu", du)
    (dwd,) = matmul(
        "ffn_dwd", [a], [dy_b], grid=(N_DEV, S // tk),
        a_spec=pl.BlockSpec((None, tk, nb), lambda j, k: (j, k, 0)),
        b_spec=pl.BlockSpec((tk, D), lambda j, k: (k, 0)),
        out_spec=pl.BlockSpec((None, nb, D), lambda j, k: (j, 0, 0)), out_shape=(N_DEV, nb, D),
        out_dtypes=[BF16], acc_shape=(nb, D), trans_a=True)
    return dh, dwg, dwu, dwd


def _pool_counts(row0, n, w):
    pos = row0 + lax.broadcasted_iota(jnp.int32, (n, 1), 0)
    return jnp.minimum(pos + 1, w).astype(F32)


def pool_forward(h, xres, w, scale, ts=256):
    S, D = h.shape
    G = len(POOL_WINDOWS)
    P = D // G
    hb = ts // POOL_HALO

    def body(h_ref, halo_ref, x_ref, w_ref, s_ref, xo_ref, p_ref):
        i = pl.program_id(0)
        for gi, win in enumerate(POOL_WINDOWS):
            cols = slice(gi * P, (gi + 1) * P)
            cur = h_ref[:, cols]
            halo = jnp.where(i > 0, halo_ref[:, cols], 0.0)
            acc = jnp.concatenate([halo, cur], axis=0)
            step = 1
            while step < win:
                acc = acc + pltpu.roll(acc, step, 0)
                step *= 2
            wsum = acc[POOL_HALO:, :]
            pooled = wsum / _pool_counts(i * ts, ts, win) - cur
            pb = pooled.astype(BF16)
            p_ref[:, cols] = pb
            mixed = jnp.dot(pb, w_ref[gi], preferred_element_type=F32)
            xo_ref[:, cols] = x_ref[:, cols] + mixed * s_ref[:, cols]

    row = pl.BlockSpec((ts, D), lambda i: (i, 0))
    return pl.pallas_call(
        body, grid=(S // ts,), name="pool_fwd",
        in_specs=[row, pl.BlockSpec((POOL_HALO, D), lambda i: (jnp.maximum(i * hb - 1, 0), 0)), row,
                  pl.BlockSpec((G, P, P), lambda i: (0, 0, 0)), pl.BlockSpec((1, D), lambda i: (0, 0))],
        out_specs=[row, row],
        out_shape=[jax.ShapeDtypeStruct((S, D), F32), jax.ShapeDtypeStruct((S, D), BF16)],
        compiler_params=_params(("arbitrary",)),
    )(h, h, xres, w, scale)


def pool_backward_mix(dx, pooled, w, scale, ts=256):
    S, D = dx.shape
    G = len(POOL_WINDOWS)
    P = D // G

    def body(dx_ref, p_ref, w_ref, s_ref, dm_ref, dp_ref, ds_ref):
        i = pl.program_id(0)
        parts = []
        for gi in range(G):
            cols = slice(gi * P, (gi + 1) * P)
            dxv = dx_ref[:, cols]
            dmb = (dxv * s_ref[:, cols]).astype(BF16)
            dm_ref[:, cols] = dmb
            dp_ref[:, cols] = lax.dot_general(dmb, w_ref[gi], (((1,), (1,)), ((), ())),
                                              preferred_element_type=F32)
            mixed = jnp.dot(p_ref[:, cols], w_ref[gi], preferred_element_type=F32)
            parts.append(jnp.sum(dxv * mixed, axis=0, keepdims=True))
        part = jnp.concatenate(parts, axis=1)

        @pl.when(i == 0)
        def _():
            ds_ref[...] = part

        @pl.when(i > 0)
        def _():
            ds_ref[...] += part

    row = pl.BlockSpec((ts, D), lambda i: (i, 0))
    vec = pl.BlockSpec((1, D), lambda i: (0, 0))
    return pl.pallas_call(
        body, grid=(S // ts,), name="pool_bwd_mix",
        in_specs=[row, row, pl.BlockSpec((G, P, P), lambda i: (0, 0, 0)), vec],
        out_specs=[row, row, vec],
        out_shape=[jax.ShapeDtypeStruct((S, D), BF16), jax.ShapeDtypeStruct((S, D), F32),
                   jax.ShapeDtypeStruct((1, D), F32)],
        compiler_params=_params(("arbitrary",)),
    )(dx, pooled, w, scale)


def pool_backward_window(dp, ts=256):
    S, D = dp.shape
    G = len(POOL_WINDOWS)
    P = D // G
    hb = ts // POOL_HALO
    n_i = S // ts
    n_rows = ts + POOL_HALO

    def body(dp_ref, halo_ref, dh_ref):
        i = pl.program_id(0)
        for gi, win in enumerate(POOL_WINDOWS):
            cols = slice(gi * P, (gi + 1) * P)
            cur = dp_ref[:, cols]
            halo = jnp.where(i < n_i - 1, halo_ref[:, cols], 0.0)
            acc = jnp.concatenate([cur / _pool_counts(i * ts, ts, win),
                                   halo / _pool_counts((i + 1) * ts, POOL_HALO, win)], axis=0)
            step = 1
            while step < win:
                acc = acc + pltpu.roll(acc, n_rows - step, 0)
                step *= 2
            dh_ref[:, cols] = acc[:ts, :] - cur

    row = pl.BlockSpec((ts, D), lambda i: (i, 0))
    return pl.pallas_call(
        body, grid=(n_i,), name="pool_bwd_window",
        in_specs=[row, pl.BlockSpec((POOL_HALO, D), lambda i: (jnp.minimum((i + 1) * hb, S // POOL_HALO - 1), 0))],
        out_specs=row,
        out_shape=jax.ShapeDtypeStruct((S, D), F32),
        compiler_params=_params(("arbitrary",)),
    )(dp, dp)


_HG_LEVELS = (32, 16, 8, 4, 2, 1)
_N_LEV = len(_HG_LEVELS) + 1


def _hgrn_constants():
    C = HG_CHUNK
    t = np.arange(C)
    tri = (t[None, :] <= t[:, None]).astype(np.float32)
    blocks = [tri]
    masks, upq, upk = [], [], []
    for m in _HG_LEVELS:
        p = (t // (2 * m)) * 2 * m + m - 1
        blocks.append(tri[p])
        masks.append(((t[:, None] // (2 * m)) == (t[None, :] // (2 * m))).astype(np.float32))
        upper = (t % (2 * m)) >= m
        upq.append(np.repeat(upper[:, None], HEAD, 1).astype(np.float32))
        upk.append(np.repeat(~upper[:, None], HEAD, 1).astype(np.float32))
    blocks.append(tri)
    masks.append(np.eye(C, dtype=np.float32))
    upq.append(np.ones((C, HEAD), np.float32))
    upk.append(np.ones((C, HEAD), np.float32))
    mstack = np.concatenate(blocks, axis=0)
    mstack3 = np.concatenate([mstack] * 3, axis=1)
    trirev3 = np.concatenate([tri.T] * 3, axis=1)
    return (jnp.asarray(mstack3, BF16), jnp.asarray(np.stack(masks)), jnp.asarray(np.stack(upq)),
            jnp.asarray(np.stack(upk)), jnp.asarray(trirev3, BF16))


def _split3(x):
    hi = x.astype(BF16)
    r1 = x - hi.astype(F32)
    mid = r1.astype(BF16)
    lo = (r1 - mid.astype(F32)).astype(BF16)
    return jnp.concatenate([hi, mid, lo], axis=0)


def _hgrn_chunk_common(qa, fa, lbv, mstack3, upq, upk):
    sq = _sigmoid(qa)
    q = qa * sq
    sf = _sigmoid(fa)
    f = lbv + (1.0 - lbv) * sf
    g = jnp.log(f)
    k = 1.0 - f
    gall = jnp.dot(mstack3, _split3(g), preferred_element_type=F32).reshape(_N_LEV + 1, HG_CHUNK, HEAD)
    G = gall[0]
    eq_exp = G[None] - gall[1:]
    eq = jnp.exp(jnp.minimum(eq_exp, 0.0)) * upq
    ek = jnp.exp(jnp.minimum(-eq_exp, 0.0)) * upk
    Qs = (q[None] * eq).astype(BF16)
    Ks = (k[None] * ek).astype(BF16)
    return sq, q, sf, f, k, G, eq, ek, Qs, Ks


def hgrn_forward(proj, lb, hg_norm, ts=512):
    S = proj.shape[0]
    nh = lb.shape[1] // HEAD
    C = HG_CHUNK
    ncs = ts // C
    mstack3, masks, upq, upk, _ = _hgrn_constants()

    def body(qa_ref, fa_ref, ia_ref, ga_ref, lb_ref, gn_ref, ms_ref, mk_ref, uq_ref, uk_ref,
             oa_ref, oraw_ref, st_ref, state):
        tt = pl.program_id(1)

        @pl.when(tt == 0)
        def _():
            state[...] = jnp.zeros_like(state)

        lbv = lb_ref[...]
        gn = gn_ref[...]

        def chunk(c, carry):
            sl = pl.ds(pl.multiple_of(c * C, C), C)
            qa, fa, v, ga = qa_ref[sl, :], fa_ref[sl, :], ia_ref[sl, :], ga_ref[sl, :]
            _, q, _, _, k, G, _, _, Qs, Ks = _hgrn_chunk_common(qa, fa, lbv, ms_ref[...], uq_ref[...], uk_ref[...])
            att7 = lax.dot_general(Qs, Ks, (((2,), (2,)), ((0,), (0,))), preferred_element_type=F32)
            att = jnp.sum(att7 * mk_ref[...], axis=0)
            st = state[...]
            st_ref[c] = st
            vb = v.astype(BF16)
            qg = (q * jnp.exp(G)).astype(BF16)
            o = jnp.dot(att.astype(BF16), vb, preferred_element_type=F32)
            o = o + lax.dot_general(qg, st.astype(BF16), (((1,), (1,)), ((), ())), preferred_element_type=F32)
            g_last = G[C - 1:C, :]
            kh = (k * jnp.exp(g_last - G)).astype(BF16)
            state[...] = st * jnp.exp(g_last) + lax.dot_general(vb, kh, (((0,), (0,)), ((), ())),
                                                                preferred_element_type=F32)
            oraw_ref[sl, :] = o
            r = lax.rsqrt(jnp.mean(o * o, axis=-1, keepdims=True) + RMS_EPS)
            oa_ref[sl, :] = (((o * r) * gn) * (ga * _sigmoid(ga))).astype(BF16)
            return carry

        lax.fori_loop(0, ncs, chunk, 0)

    def col(m0):
        return pl.BlockSpec((ts, HEAD), lambda h, t: (t, m0 + h))

    const3 = lambda shape: pl.BlockSpec(shape, lambda h, t: (0, 0, 0))
    return pl.pallas_call(
        body, grid=(nh, S // ts), name="hgrn_fwd",
        in_specs=[col(0), col(nh), col(2 * nh), col(3 * nh),
                  pl.BlockSpec((1, HEAD), lambda h, t: (0, h)), pl.BlockSpec((1, HEAD), lambda h, t: (0, 0)),
                  pl.BlockSpec(mstack3.shape, lambda h, t: (0, 0)), const3(masks.shape), const3(upq.shape),
                  const3(upk.shape)],
        out_specs=[pl.BlockSpec((ts, HEAD), lambda h, t: (t, h)), pl.BlockSpec((ts, HEAD), lambda h, t: (t, h)),
                   pl.BlockSpec((None, ncs, HEAD, HEAD), lambda h, t: (h, t, 0, 0))],
        out_shape=[jax.ShapeDtypeStruct((S, nh * HEAD), BF16), jax.ShapeDtypeStruct((S, nh * HEAD), F32),
                   jax.ShapeDtypeStruct((nh, S // C, HEAD, HEAD), F32)],
        scratch_shapes=[pltpu.VMEM((HEAD, HEAD), F32)],
        compiler_params=_params(("arbitrary", "arbitrary")),
    )(proj, proj, proj, proj, lb, hg_norm, mstack3, masks, upq, upk)


def hgrn_backward(dcat, proj, oraw, states, lb, hg_norm, ts=512):
    S = proj.shape[0]
    nh = lb.shape[1] // HEAD
    C = HG_CHUNK
    ncs = ts // C
    nt = S // ts
    mstack3, masks, upq, upk, trirev3 = _hgrn_constants()

    def body(do_ref, qa_ref, fa_ref, ia_ref, ga_ref, or_ref, st_ref, lb_ref, gn_ref, ms_ref, mk_ref, uq_ref,
             uk_ref, tr_ref, dqa_ref, dfa_ref, dia_ref, dga_ref, dlb_ref, dgn_ref, dstate):
        tt = pl.program_id(1)

        @pl.when(tt == 0)
        def _():
            dstate[...] = jnp.zeros_like(dstate)
            dlb_ref[...] = jnp.zeros_like(dlb_ref)
            dgn_ref[...] = jnp.zeros_like(dgn_ref)

        lbv = lb_ref[...]
        gn = gn_ref[...]

        def chunk(cc, carry):
            c = ncs - 1 - cc
            sl = pl.ds(pl.multiple_of(c * C, C), C)
            qa, fa, v, ga = qa_ref[sl, :], fa_ref[sl, :], ia_ref[sl, :], ga_ref[sl, :]
            sq, q, sf, f, k, G, eq, ek, Qs, Ks = _hgrn_chunk_common(qa, fa, lbv, ms_ref[...], uq_ref[...],
                                                                    uk_ref[...])
            mk = mk_ref[...]
            att7 = lax.dot_general(Qs, Ks, (((2,), (2,)), ((0,), (0,))), preferred_element_type=F32)
            att = jnp.sum(att7 * mk, axis=0)
            o = or_ref[sl, :]
            dO = do_ref[sl, :]
            sg = _sigmoid(ga)
            r = lax.rsqrt(jnp.mean(o * o, axis=-1, keepdims=True) + RMS_EPS)
            xh = o * r
            dga_ref[sl, :] = (dO * (xh * gn) * (sg * (1.0 + ga * (1.0 - sg)))).astype(BF16)
            don = dO * (ga * sg)
            dgn_ref[...] += jnp.sum(don * xh, axis=0, keepdims=True)
            dxh = don * gn
            do = r * (dxh - xh * jnp.mean(dxh * xh, axis=-1, keepdims=True))
            dob = do.astype(BF16)
            st = st_ref[c]
            dst = dstate[...]
            dstb = dst.astype(BF16)
            vb = v.astype(BF16)
            eG = jnp.exp(G)
            g_last = G[C - 1:C, :]
            e_last = jnp.exp(g_last)
            e_tail = jnp.exp(g_last - G)
            qg = (q * eG).astype(BF16)
            kh = (k * e_tail).astype(BF16)
            dq_inter = jnp.dot(dob, st.astype(BF16), preferred_element_type=F32) * eG
            dk_inter = jnp.dot(vb, dstb, preferred_element_type=F32) * e_tail
            dv = lax.dot_general(kh, dstb, (((1,), (1,)), ((), ())), preferred_element_type=F32)
            dv = dv + lax.dot_general(att.astype(BF16), dob, (((0,), (0,)), ((), ())), preferred_element_type=F32)
            dA = lax.dot_general(dob, vb, (((1,), (1,)), ((), ())), preferred_element_type=F32)
            dA7 = (dA[None] * mk).astype(BF16)
            dAT7 = (dA.T[None] * mk).astype(BF16)
            dQs = lax.dot_general(dA7, Ks, (((2,), (1,)), ((0,), (0,))), preferred_element_type=F32)
            dKs = lax.dot_general(dAT7, Qs, (((2,), (1,)), ((0,), (0,))), preferred_element_type=F32)
            dq = dq_inter + jnp.sum(dQs * eq, axis=0)
            dk = dk_inter + jnp.sum(dKs * ek, axis=0)
            dG = (jnp.sum(Qs.astype(F32) * dQs - Ks.astype(F32) * dKs, axis=0)
                  + q * dq_inter - k * dk_inter)
            last_extra = (jnp.sum(k * dk_inter, axis=0, keepdims=True)
                          + e_last * jnp.sum(dst * st, axis=0, keepdims=True))
            is_last = lax.broadcasted_iota(jnp.int32, (C, 1), 0) == C - 1
            dG = dG + jnp.where(is_last, last_extra, 0.0)
            dg = jnp.dot(tr_ref[...], _split3(dG), preferred_element_type=F32)
            df = dg / f - dk
            dfa_ref[sl, :] = (df * (1.0 - lbv) * (sf * (1.0 - sf))).astype(BF16)
            dlb_ref[...] += jnp.sum(df * (1.0 - sf), axis=0, keepdims=True)
            dqa_ref[sl, :] = (dq * (sq * (1.0 + qa * (1.0 - sq)))).astype(BF16)
            dia_ref[sl, :] = dv.astype(BF16)
            dstate[...] = dst * e_last + lax.dot_general(dob, qg, (((0,), (0,)), ((), ())),
                                                         preferred_element_type=F32)
            return carry

        lax.fori_loop(0, ncs, chunk, 0)

    def col(m0):
        return pl.BlockSpec((ts, HEAD), lambda h, t: (nt - 1 - t, m0 + h))

    const3 = lambda shape: pl.BlockSpec(shape, lambda h, t: (0, 0, 0))
    const2 = lambda shape: pl.BlockSpec(shape, lambda h, t: (0, 0))
    ocol = pl.BlockSpec((ts, HEAD), lambda h, t: (nt - 1 - t, h))
    half = nh * HEAD
    return pl.pallas_call(
        body, grid=(nh, nt), name="hgrn_bwd",
        in_specs=[col(0), col(0), col(nh), col(2 * nh), col(3 * nh), col(0),
                  pl.BlockSpec((None, ncs, HEAD, HEAD), lambda h, t: (h, nt - 1 - t, 0, 0)),
                  pl.BlockSpec((1, HEAD), lambda h, t: (0, h)), const2((1, HEAD)),
                  const2(mstack3.shape), const3(masks.shape), const3(upq.shape), const3(upk.shape),
                  const2(trirev3.shape)],
        out_specs=[ocol, ocol, ocol, ocol, pl.BlockSpec((1, HEAD), lambda h, t: (0, h)),
                   pl.BlockSpec((None, 1, HEAD), lambda h, t: (h, 0, 0))],
        out_shape=[jax.ShapeDtypeStruct((S, half), BF16)] * 4
                  + [jax.ShapeDtypeStruct((1, half), F32), jax.ShapeDtypeStruct((nh, 1, HEAD), F32)],
        scratch_shapes=[pltpu.VMEM((HEAD, HEAD), F32)],
        compiler_params=_params(("arbitrary", "arbitrary")),
    )(dcat, proj, proj, proj, proj, oraw, states, lb, hg_norm, mstack3, masks, upq, upk, trirev3)


def _split2(x):
    hi = x.astype(BF16)
    lo = (x - hi.astype(F32)).astype(BF16)
    return jnp.concatenate([hi, lo], axis=1)


def _sb_constants(tk):
    j = np.arange(tk)
    after = (j[:, None] > j[None, :]).astype(np.float32)
    before = (j[:, None] < j[None, :]).astype(np.float32)
    return (jnp.asarray(np.concatenate([after, after], axis=0), BF16),
            jnp.asarray(np.concatenate([before, before], axis=0), BF16))


def _sb_block(q, ks, scale, qrow0, kcol0, run, after2):
    tq, tk = q.shape[0], ks.shape[0]
    z = lax.dot_general(q, ks, (((1,), (1,)), ((), ())), preferred_element_type=F32) * scale
    qpos = qrow0 + lax.broadcasted_iota(jnp.int32, (tq, tk), 0)
    kpos = kcol0 + lax.broadcasted_iota(jnp.int32, (tq, tk), 1)
    mask = kpos < qpos
    softplus = jnp.maximum(z, 0.0) + jnp.log(1.0 + jnp.exp(-jnp.abs(z)))
    lk = jnp.where(mask, -softplus, 0.0)
    later = run + jnp.dot(_split2(lk), after2, preferred_element_type=F32)
    w = jnp.where(mask, jnp.exp(z + lk + later), 0.0)
    new_run = later[:, 0:1] + lk[:, 0:1]
    return z, mask, w, new_run


def sb_forward(projb, nh, m0, tq=128, tk=128):
    S = projb.shape[0]
    scale = 1.0 / math.sqrt(HEAD)
    after2, _ = _sb_constants(tk)
    ratio = tq // tk

    def body(q_ref, k_ref, v_ref, af_ref, o_ref):
        i = pl.program_id(1)
        q = q_ref[...]
        nkb = (i + 1) * ratio

        def step(n, carry):
            run, acc = carry
            jb = nkb - 1 - n
            ksl = pl.ds(pl.multiple_of(jb * tk, tk), tk)
            _, _, w, run = _sb_block(q, k_ref[ksl, :], scale, i * tq, jb * tk, run, af_ref[...])
            acc = acc + jnp.dot(w.astype(BF16), v_ref[ksl, :], preferred_element_type=F32)
            return run, acc

        _, acc = lax.fori_loop(0, nkb, step, (jnp.zeros((tq, 1), F32), jnp.zeros((tq, HEAD), F32)))
        o_ref[...] = acc.astype(BF16)

    return pl.pallas_call(
        body, grid=(nh, S // tq), name="sb_fwd",
        in_specs=[pl.BlockSpec((tq, HEAD), lambda h, i: (i, m0 + h)),
                  pl.BlockSpec((S, HEAD), lambda h, i: (0, m0 + nh + h)),
                  pl.BlockSpec((S, HEAD), lambda h, i: (0, m0 + 2 * nh + h)),
                  pl.BlockSpec(after2.shape, lambda h, i: (0, 0))],
        out_specs=pl.BlockSpec((tq, HEAD), lambda h, i: (i, h)),
        out_shape=jax.ShapeDtypeStruct((S, nh * HEAD), BF16),
        compiler_params=_params(("arbitrary", "arbitrary")),
    )(projb, projb, projb, after2)


def sb_backward(dcat, projb, nh, m0, tq=128, tk=128):
    S = projb.shape[0]
    scale = 1.0 / math.sqrt(HEAD)
    after2, before2 = _sb_constants(tk)
    ratio = tq // tk
    n_i = S // tq
    nkb_max = S // tk

    def body(do_ref, q_ref, k_ref, v_ref, af_ref, bf_ref, dq_ref, dk_ref, dv_ref, zbuf, dbuf, dk_acc, dv_acc):
        i = pl.program_id(1)

        @pl.when(i == 0)
        def _():
            dk_acc[...] = jnp.zeros_like(dk_acc)
            dv_acc[...] = jnp.zeros_like(dv_acc)

        q = q_ref[...]
        dob = do_ref[...].astype(BF16)
        nkb = (i + 1) * ratio

        def sweep_right_to_left(n, run):
            jb = nkb - 1 - n
            ksl = pl.ds(pl.multiple_of(jb * tk, tk), tk)
            z, _, w, run = _sb_block(q, k_ref[ksl, :], scale, i * tq, jb * tk, run, af_ref[...])
            dw = lax.dot_general(dob, v_ref[ksl, :], (((1,), (1,)), ((), ())), preferred_element_type=F32)
            zbuf[jb] = z
            dbuf[jb] = dw * w
            dv_acc[ksl, :] += lax.dot_general(w.astype(BF16), dob, (((0,), (0,)), ((), ())),
                                              preferred_element_type=F32)
            return run

        lax.fori_loop(0, nkb, sweep_right_to_left, jnp.zeros((tq, 1), F32))

        def sweep_left_to_right(jb, carry):
            run, dq = carry
            ksl = pl.ds(pl.multiple_of(jb * tk, tk), tk)
            z = zbuf[jb]
            d = dbuf[jb]
            qpos = i * tq + lax.broadcasted_iota(jnp.int32, (tq, tk), 0)
            kpos = jb * tk + lax.broadcasted_iota(jnp.int32, (tq, tk), 1)
            prefix = run + jnp.dot(_split2(d), bf_ref[...], preferred_element_type=F32)
            sig = _sigmoid(z)
            da = jnp.where(kpos < qpos, d * (1.0 - sig) - sig * prefix, 0.0)
            dab = (da * scale).astype(BF16)
            dq = dq + jnp.dot(dab, k_ref[ksl, :], preferred_element_type=F32)
            dk_acc[ksl, :] += lax.dot_general(dab, q, (((0,), (0,)), ((), ())), preferred_element_type=F32)
            run = prefix[:, tk - 1:tk] + d[:, tk - 1:tk]
            return run, dq

        _, dq = lax.fori_loop(0, nkb, sweep_left_to_right,
                              (jnp.zeros((tq, 1), F32), jnp.zeros((tq, HEAD), F32)))
        dq_ref[...] = dq.astype(BF16)

        @pl.when(i == n_i - 1)
        def _():
            dk_ref[...] = dk_acc[...].astype(BF16)
            dv_ref[...] = dv_acc[...].astype(BF16)

    half = nh * HEAD
    full = pl.BlockSpec((S, HEAD), lambda h, i: (0, h))
    return pl.pallas_call(
        body, grid=(nh, n_i), name="sb_bwd",
        in_specs=[pl.BlockSpec((tq, HEAD), lambda h, i: (i, nh + h)),
                  pl.BlockSpec((tq, HEAD), lambda h, i: (i, m0 + h)),
                  pl.BlockSpec((S, HEAD), lambda h, i: (0, m0 + nh + h)),
                  pl.BlockSpec((S, HEAD), lambda h, i: (0, m0 + 2 * nh + h)),
                  pl.BlockSpec(after2.shape, lambda h, i: (0, 0)), pl.BlockSpec(before2.shape, lambda h, i: (0, 0))],
        out_specs=[pl.BlockSpec((tq, HEAD), lambda h, i: (i, h)), full, full],
        out_shape=[jax.ShapeDtypeStruct((S, half), BF16)] * 3,
        scratch_shapes=[pltpu.VMEM((nkb_max, tq, tk), F32), pltpu.VMEM((nkb_max, tq, tk), F32),
                        pltpu.VMEM((S, HEAD), F32), pltpu.VMEM((S, HEAD), F32)],
        compiler_params=_params(("arbitrary", "arbitrary")),
    )(dcat, projb, projb, projb, after2, before2)


def local_step(x, target, mix_norm, ffn_norm, final_norm, lb_logits, hg_norm, w_in, w_out, pool_w, pool_scale,
               wg, wu, wd):
    S, D = x.shape
    half = D // 2
    nh = half // HEAD
    nbi = w_in.shape[2]
    tm = 512
    tk = 512
    row = lambda i, j: (i, 0)

    lb = jax.nn.softmax(lb_logits, axis=0)[0:1]

    h0, r0 = rms_fwd(x, mix_norm[0:1], BF16)
    proj, projb = matmul(
        "proj_in", [h0], [w_in], grid=(N_DEV, S // tm, 1),
        a_spec=pl.BlockSpec((tm, D), lambda j, i, k: (i, 0)),
        b_spec=pl.BlockSpec((None, D, nbi), lambda j, i, k: (j, 0, 0)),
        out_spec=pl.BlockSpec((tm, nbi), lambda j, i, k: (i, j)), out_shape=(S, N_DEV * nbi),
        out_dtypes=[F32, BF16], acc_shape=(8, 128))
    oa, oraw, states = hgrn_forward(proj, lb, hg_norm)
    ob = sb_forward(projb, nh, 4 * nh)
    cat = jnp.concatenate([oa, ob], axis=1)
    (x1,) = matmul(
        "mix_out", [cat], [w_out], grid=(S // tm, 1),
        a_spec=pl.BlockSpec((tm, D), row), b_spec=pl.BlockSpec((D, D), lambda i, k: (0, 0)),
        out_spec=pl.BlockSpec((tm, D), row), out_shape=(S, D), out_dtypes=[F32], acc_shape=(8, 128),
        res=x, res_spec=pl.BlockSpec((tm, D), row))
    h1, r1 = rms_fwd(x1, ffn_norm[0:1], BF16)
    x2, ffn0 = ffn_forward(h1, x1, wg[0], wu[0], wd[0])

    h2, r2 = rms_fwd(x2, mix_norm[1:2], F32)
    x3, pooled = pool_forward(h2, x2, pool_w, pool_scale)
    h3, r3 = rms_fwd(x3, ffn_norm[1:2], BF16)
    x4, ffn1 = ffn_forward(h3, x3, wg[1], wu[1], wd[1])

    loss_blk, dx4, dx4b, d_final = loss_and_final_bwd(x4, final_norm, target)

    dh3, dwg1, dwu1, dwd1 = ffn_backward(dx4b, h3, ffn1, wg[1], wu[1], wd[1])
    dx3, _, d_ffn1 = rms_bwd(dh3, x3, r3, ffn_norm[1:2], dx4)
    dmixed, dpooled, d_pscale = pool_backward_mix(dx3, pooled, pool_w, pool_scale)
    G = len(POOL_WINDOWS)
    P = D // G
    (d_pool_w,) = matmul(
        "pool_dw", [pooled], [dmixed], grid=(G, S // tk),
        a_spec=pl.BlockSpec((tk, P), lambda g, k: (k, g)), b_spec=pl.BlockSpec((tk, P), lambda g, k: (k, g)),
        out_spec=pl.BlockSpec((None, P, P), lambda g, k: (g, 0, 0)), out_shape=(G, P, P), out_dtypes=[BF16],
        acc_shape=(P, P), trans_a=True)
    dh2 = pool_backward_window(dpooled)
    dx2, dx2b, d_mix1 = rms_bwd(dh2, x2, r2, mix_norm[1:2], dx3)

    dh1, dwg0, dwu0, dwd0 = ffn_backward(dx2b, h1, ffn0, wg[0], wu[0], wd[0])
    dx1, dx1b, d_ffn0 = rms_bwd(dh1, x1, r1, ffn_norm[0:1], dx2)
    (dcat,) = matmul(
        "mix_out_dx", [dx1b], [w_out], grid=(S // tm, 1),
        a_spec=pl.BlockSpec((tm, D), row), b_spec=pl.BlockSpec((D, D), lambda i, k: (0, 0)),
        out_spec=pl.BlockSpec((tm, D), row), out_shape=(S, D), out_dtypes=[F32], acc_shape=(8, 128),
        trans_b=True)
    (d_w_out,) = matmul(
        "mix_out_dw", [cat], [dx1b], grid=(2, S // tk),
        a_spec=pl.BlockSpec((tk, half), lambda m, k: (k, m)), b_spec=pl.BlockSpec((tk, D), lambda m, k: (k, 0)),
        out_spec=pl.BlockSpec((half, D), lambda m, k: (m, 0)), out_shape=(D, D), out_dtypes=[BF16],
        acc_shape=(half, D), trans_a=True)
    dqa, dfa, dia, dga, d_lb, d_hg = hgrn_backward(dcat, proj, oraw, states, lb, hg_norm)
    dqb, dkb, dvb = sb_backward(dcat, projb, nh, 4 * nh)
    dproj = jnp.concatenate([dqa, dfa, dia, dga, dqb, dkb, dvb], axis=1)
    (dh0,) = matmul(
        "proj_in_dx", [dproj], [w_in], grid=(S // tm, N_DEV),
        a_spec=pl.BlockSpec((tm, nbi), lambda i, j: (i, j)),
        b_spec=pl.BlockSpec((None, D, nbi), lambda i, j: (j, 0, 0)),
        out_spec=pl.BlockSpec((tm, D), row), out_shape=(S, D), out_dtypes=[F32], acc_shape=(tm, D),
        trans_b=True)
    (d_w_in,) = matmul(
        "proj_in_dw", [h0], [dproj], grid=(N_DEV, S // tk),
        a_spec=pl.BlockSpec((tk, D), lambda j, k: (k, 0)), b_spec=pl.BlockSpec((tk, nbi), lambda j, k: (k, j)),
        out_spec=pl.BlockSpec((None, D, nbi), lambda j, k: (j, 0, 0)), out_shape=(N_DEV, D, nbi),
        out_dtypes=[BF16], acc_shape=(D, nbi), trans_a=True)
    dx0, _, d_mix0 = rms_bwd(dh0, x, r0, mix_norm[0:1], dx1)

    d_l0 = d_lb * lb * (1.0 - lb)
    small = dict(
        loss=loss_blk[0:1, 0:1],
        mix_norm=jnp.concatenate([d_mix0, d_mix1], axis=0),
        ffn_norm=jnp.concatenate([d_ffn0, d_ffn1], axis=0),
        final_norm=d_final,
        lb_logits=jnp.concatenate([d_l0, -d_l0], axis=0),
        hg_out_norm=jnp.sum(d_hg, axis=0),
        pool_scale=d_pscale,
    )
    big = dict(
        ab_w_in=d_w_in, ab_w_out=d_w_out, pool_w=d_pool_w,
        ffn_w_gate=[dwg0, dwg1], ffn_w_up=[dwu0, dwu1], ffn_w_down=[dwd0, dwd1],
    )
    return dx0, small, big


def _my_index():
    return 4 * lax.axis_index("x") + 2 * lax.axis_index("y") + lax.axis_index("c")


def _peer(r):
    x, y, c = lax.axis_index("x"), lax.axis_index("y"), lax.axis_index("c")
    px = 1 - x if (r >> 2) & 1 else x
    py = 1 - y if (r >> 1) & 1 else y
    pc = 1 - c if r & 1 else c
    return (px, py, pc), 4 * px + 2 * py + pc


def exchange(name, arrays, gather):
    n = len(arrays)
    n_peers = N_DEV - 1

    def body(*refs):
        ins, outs = refs[:n], refs[n:2 * n]
        send_sems, recv_sems, local_sems = refs[2 * n:]
        me = _my_index()
        local = []
        for a in range(n):
            src = ins[a] if gather else ins[a].at[me]
            cp = pltpu.make_async_copy(src, outs[a].at[me], local_sems.at[a])
            cp.start()
            local.append(cp)
        remote = []
        for a in range(n):
            for r in range(1, N_DEV):
                peer, pidx = _peer(r)
                src = ins[a] if gather else ins[a].at[pidx]
                cp = pltpu.make_async_remote_copy(
                    src_ref=src, dst_ref=outs[a].at[me], send_sem=send_sems.at[a * n_peers + r - 1],
                    recv_sem=recv_sems.at[a * n_peers + r - 1], device_id=peer, device_id_type=MESH)
                cp.start()
                remote.append((cp, a, r))
        for cp, a, r in remote:
            _, pidx = _peer(r)
            src = ins[a] if gather else ins[a].at[pidx]
            pltpu.make_async_remote_copy(
                src_ref=src, dst_ref=outs[a].at[pidx], send_sem=send_sems.at[a * n_peers + r - 1],
                recv_sem=recv_sems.at[a * n_peers + r - 1], device_id=_peer(r)[0], device_id_type=MESH).wait_recv()
        for cp, a, r in remote:
            cp.wait_send()
        for cp in local:
            cp.wait()

    out_shape = [jax.ShapeDtypeStruct(((N_DEV,) + a.shape) if gather else a.shape, a.dtype) for a in arrays]
    any_spec = pl.BlockSpec(memory_space=pl.ANY)
    return pl.pallas_call(
        body, name=name, in_specs=[any_spec] * n, out_specs=[any_spec] * n, out_shape=out_shape,
        scratch_shapes=[pltpu.SemaphoreType.DMA((n * n_peers,)), pltpu.SemaphoreType.DMA((n * n_peers,)),
                        pltpu.SemaphoreType.DMA((n,))],
    )(*arrays)


def _row_tile(rows, cap=256):
    best = None
    for t in range(16, min(rows, cap) + 1, 16):
        if rows % t == 0:
            best = t
    return best if best is not None else rows


def sum_slots(name, recv):
    n, R, C = recv.shape
    tr = _row_tile(R)

    def body(r_ref, o_ref):
        g = r_ref[0].astype(F32)
        for d in range(1, n):
            g = g + r_ref[d].astype(F32)
        o_ref[...] = g

    return pl.pallas_call(
        body, grid=(R // tr,), name=name,
        in_specs=[pl.BlockSpec((n, tr, C), lambda i: (0, i, 0))],
        out_specs=pl.BlockSpec((tr, C), lambda i: (i, 0)),
        out_shape=jax.ShapeDtypeStruct((R, C), F32),
        compiler_params=_params(("arbitrary",)),
    )(recv)


def adamw(name, recv, w, m, v):
    n, R, C = recv.shape
    tr = _row_tile(R)

    def body(r_ref, w_ref, m_ref, v_ref, g_ref, d_ref, nm_ref, nv_ref):
        g = r_ref[0].astype(F32)
        for d in range(1, n):
            g = g + r_ref[d].astype(F32)
        mm = ADAM_B1 * m_ref[...] + (1.0 - ADAM_B1) * g
        vv = ADAM_B2 * v_ref[...] + (1.0 - ADAM_B2) * (g * g)
        m_hat = mm / (1.0 - ADAM_B1 ** ADAM_STEP)
        v_hat = vv / (1.0 - ADAM_B2 ** ADAM_STEP)
        g_ref[...] = g
        d_ref[...] = -ADAM_LR * (m_hat / (jnp.sqrt(v_hat) + ADAM_EPS) + ADAM_WD * w_ref[...])
        nm_ref[...] = mm
        nv_ref[...] = vv

    row = pl.BlockSpec((tr, C), lambda i: (i, 0))
    return pl.pallas_call(
        body, grid=(R // tr,), name=name,
        in_specs=[pl.BlockSpec((n, tr, C), lambda i: (0, i, 0)), row, row, row],
        out_specs=[row] * 4,
        out_shape=[jax.ShapeDtypeStruct((R, C), F32)] * 4,
        compiler_params=_params(("arbitrary",)),
    )(recv, w, m, v)


def _adamw_nd(name, recv, w, m, v):
    shp = w.shape
    C = shp[-1]
    flat = lambda a: a.reshape(-1, C)
    outs = adamw(name, recv.reshape(recv.shape[0], -1, C), flat(w), flat(m), flat(v))
    return [o.reshape(shp) for o in outs]


_SMALL_NAMES = ("loss", "mix_norm", "ffn_norm", "final_norm", "lb_logits", "hg_out_norm", "pool_scale")
_LANES = 128


def _pack_small(parts):
    rows, layout = [], {}
    at = 0
    for name in parts:
        flat = parts[name].reshape(-1).astype(F32)
        n_rows = -(-flat.shape[0] // _LANES)
        flat = jnp.pad(flat, (0, n_rows * _LANES - flat.shape[0]))
        rows.append(flat.reshape(n_rows, _LANES))
        layout[name] = (at, parts[name].shape)
        at += n_rows
    pad = -at % 8
    if pad:
        rows.append(jnp.zeros((pad, _LANES), F32))
    return jnp.concatenate(rows, axis=0), layout


def _unpack_small(pack, layout):
    out = {}
    for name, (at, shape) in layout.items():
        size = int(np.prod(shape))
        n_rows = -(-size // _LANES)
        out[name] = pack[at:at + n_rows].reshape(-1)[:size].reshape(shape)
    return out


def kernel(x, mix_norm, ffn_norm, final_norm, ab_w_in, lb_logits, hg_out_norm, ab_w_out, pool_w, pool_scale, ffn_w_gate, ffn_w_up, ffn_w_down, loss_target, m_mix_norm, m_ffn_norm, m_final_norm, m_ab_w_in, m_lb_logits, m_hg_out_norm, m_ab_w_out, m_pool_w, m_pool_scale, m_ffn_w_gate, m_ffn_w_up, m_ffn_w_down, v_mix_norm, v_ffn_norm, v_final_norm, v_ab_w_in, v_lb_logits, v_hg_out_norm, v_ab_w_out, v_pool_w, v_pool_scale, v_ffn_w_gate, v_ffn_w_up, v_ffn_w_down):
    D = x.shape[-1]
    n_layers = ffn_w_gate.shape[0]
    G = pool_w.shape[1]
    P = pool_w.shape[3]
    me = _my_index()

    shards = [ab_w_in[0], ab_w_out[0], pool_w[0]]
    for l in range(n_layers):
        shards += [ffn_w_gate[l], ffn_w_up[l], ffn_w_down[l]]
    gathered = exchange("gather_weights", [s.astype(BF16) for s in shards] + [pool_scale], gather=True)
    pool_scale_g = gathered[-1].reshape(1, D)
    w_in_g = gathered[0]
    w_out_g = gathered[1].reshape(D, D)
    pool_g = gathered[2].transpose(1, 0, 2, 3).reshape(G, P, P)
    wg = [gathered[3 + 3 * l] for l in range(n_layers)]
    wu = [gathered[4 + 3 * l] for l in range(n_layers)]
    wd = [gathered[5 + 3 * l] for l in range(n_layers)]

    dx0, small, big = local_step(x[0], loss_target[0], mix_norm, ffn_norm, final_norm[None], lb_logits,
                                 hg_out_norm, w_in_g, w_out_g, pool_g, pool_scale_g, wg, wu, wd)

    d_pool_blocks = big["pool_w"].reshape(G, N_DEV, P // N_DEV, P).transpose(1, 0, 2, 3)
    sends = [big["ab_w_in"], big["ab_w_out"].reshape(N_DEV, D // N_DEV, D), d_pool_blocks]
    for l in range(n_layers):
        sends += [big["ffn_w_gate"][l], big["ffn_w_up"][l], big["ffn_w_down"][l]]
    recv = exchange("exchange_grads", sends, gather=False)
    small_pack, layout = _pack_small({k: small[k] for k in _SMALL_NAMES})
    (small_all,) = exchange("gather_small", [small_pack], gather=True)
    tot = _unpack_small(sum_slots("sum_small", small_all), layout)

    res = {}
    res["ab_w_in"] = _adamw_nd("adamw_w_in", recv[0], ab_w_in, m_ab_w_in, v_ab_w_in)
    res["ab_w_out"] = _adamw_nd("adamw_w_out", recv[1], ab_w_out, m_ab_w_out, v_ab_w_out)
    res["pool_w"] = _adamw_nd("adamw_pool_w", recv[2], pool_w, m_pool_w, v_pool_w)
    ffn_in = {"ffn_w_gate": (ffn_w_gate, m_ffn_w_gate, v_ffn_w_gate, 3),
              "ffn_w_up": (ffn_w_up, m_ffn_w_up, v_ffn_w_up, 4),
              "ffn_w_down": (ffn_w_down, m_ffn_w_down, v_ffn_w_down, 5)}
    for name, (w, m, v, at) in ffn_in.items():
        per_layer = [_adamw_nd("adamw_" + name, recv[at + 3 * l], w[l], m[l], v[l]) for l in range(n_layers)]
        res[name] = [jnp.stack([per_layer[l][o] for l in range(n_layers)]) for o in range(4)]

    n_ps = pool_scale.shape[1]
    small_g = dict(tot)
    small_g["pool_scale"] = lax.dynamic_slice(tot["pool_scale"], (0, me * n_ps), (1, n_ps))
    small_w = dict(mix_norm=(mix_norm, m_mix_norm, v_mix_norm), ffn_norm=(ffn_norm, m_ffn_norm, v_ffn_norm),
                   final_norm=(final_norm, m_final_norm, v_final_norm),
                   lb_logits=(lb_logits, m_lb_logits, v_lb_logits),
                   hg_out_norm=(hg_out_norm, m_hg_out_norm, v_hg_out_norm),
                   pool_scale=(pool_scale, m_pool_scale, v_pool_scale))
    g_pack, lay2 = _pack_small({k: small_g[k].reshape(small_w[k][0].shape) for k in small_w})
    w_pack, _ = _pack_small({k: small_w[k][0] for k in small_w})
    m_pack, _ = _pack_small({k: small_w[k][1] for k in small_w})
    v_pack, _ = _pack_small({k: small_w[k][2] for k in small_w})
    small_out = [_unpack_small(o, lay2) for o in adamw("adamw_small", g_pack[None], w_pack, m_pack, v_pack)]
    for k in small_w:
        res[k] = [small_out[o][k] for o in range(4)]

    order = ("mix_norm", "ffn_norm", "final_norm", "ab_w_in", "lb_logits", "hg_out_norm", "ab_w_out", "pool_w",
             "pool_scale", "ffn_w_gate", "ffn_w_up", "ffn_w_down")
    outs = [tot["loss"].reshape(()), dx0[None]]
    for o in range(4):
        outs += [res[k][o] for k in order]
    return tuple(outs)
```

```python
import functools
import math

import numpy as np
import jax
import jax.numpy as jnp
from jax import lax
from jax.experimental import pallas as pl
from jax.experimental.pallas import tpu as pltpu

F32 = jnp.float32
BF16 = jnp.bfloat16

N_DEV = 8
RMS_EPS = 1e-6
HEAD = 128
HG_CHUNK = 64
POOL_WINDOWS = (2, 4, 8, 16)
POOL_HALO = 16
ADAM_LR, ADAM_B1, ADAM_B2, ADAM_EPS, ADAM_WD, ADAM_STEP = 0.001, 0.9, 0.999, 1e-08, 0.01, 10
VMEM_LIMIT_BYTES = 60 * 1024 * 1024
MESH = pl.DeviceIdType.MESH


def _params(sem):
    return pltpu.CompilerParams(dimension_semantics=sem, vmem_limit_bytes=VMEM_LIMIT_BYTES)


def _sigmoid(x):
    return 1.0 / (1.0 + jnp.exp(-x))


def rms_fwd(x, gain, out_dtype, ts=512):
    S, D = x.shape

    def body(x_ref, g_ref, h_ref, r_ref):
        xv = x_ref[...]
        r = lax.rsqrt(jnp.mean(xv * xv, axis=-1, keepdims=True) + RMS_EPS)
        h_ref[...] = ((xv * r) * g_ref[...]).astype(h_ref.dtype)
        r_ref[...] = r

    return pl.pallas_call(
        body, grid=(S // ts,), name="rms_fwd",
        in_specs=[pl.BlockSpec((ts, D), lambda i: (i, 0)), pl.BlockSpec((1, D), lambda i: (0, 0))],
        out_specs=[pl.BlockSpec((ts, D), lambda i: (i, 0)), pl.BlockSpec((ts, 1), lambda i: (i, 0))],
        out_shape=[jax.ShapeDtypeStruct((S, D), out_dtype), jax.ShapeDtypeStruct((S, 1), F32)],
        compiler_params=_params(("arbitrary",)),
    )(x, gain)


def rms_bwd(dh, x, r, gain, dres, ts=512):
    S, D = x.shape

    def body(dh_ref, x_ref, r_ref, g_ref, dres_ref, dx_ref, dxb_ref, dg_ref):
        i = pl.program_id(0)
        rr = r_ref[...]
        xh = x_ref[...] * rr
        dhv = dh_ref[...]
        dxh = dhv * g_ref[...]
        dx = dres_ref[...] + rr * (dxh - xh * jnp.mean(dxh * xh, axis=-1, keepdims=True))
        dx_ref[...] = dx
        dxb_ref[...] = dx.astype(BF16)
        part = jnp.sum(dhv * xh, axis=0, keepdims=True)

        @pl.when(i == 0)
        def _():
            dg_ref[...] = part

        @pl.when(i > 0)
        def _():
            dg_ref[...] += part

    row = pl.BlockSpec((ts, D), lambda i: (i, 0))
    vec = pl.BlockSpec((1, D), lambda i: (0, 0))
    return pl.pallas_call(
        body, grid=(S // ts,), name="rms_bwd",
        in_specs=[row, row, pl.BlockSpec((ts, 1), lambda i: (i, 0)), vec, row],
        out_specs=[row, row, vec],
        out_shape=[jax.ShapeDtypeStruct((S, D), F32), jax.ShapeDtypeStruct((S, D), BF16),
                   jax.ShapeDtypeStruct((1, D), F32)],
        compiler_params=_params(("arbitrary",)),
    )(dh, x, r, gain, dres)


def loss_and_final_bwd(x, gain, target, ts=512):
    S, D = x.shape

    def body(x_ref, g_ref, t_ref, loss_ref, dx_ref, dxb_ref, dg_ref):
        i = pl.program_id(0)
        xv = x_ref[...]
        rr = lax.rsqrt(jnp.mean(xv * xv, axis=-1, keepdims=True) + RMS_EPS)
        xh = xv * rr
        err = xh * g_ref[...] - t_ref[...]
        part_loss = 0.5 * jnp.sum(jnp.mean(err * err, axis=-1, keepdims=True))
        dy = err / D
        dxh = dy * g_ref[...]
        dx = rr * (dxh - xh * jnp.mean(dxh * xh, axis=-1, keepdims=True))
        dx_ref[...] = dx
        dxb_ref[...] = dx.astype(BF16)
        part = jnp.sum(dy * xh, axis=0, keepdims=True)

        @pl.when(i == 0)
        def _():
            dg_ref[...] = part
            loss_ref[...] = jnp.zeros_like(loss_ref) + part_loss

        @pl.when(i > 0)
        def _():
            dg_ref[...] += part
            loss_ref[...] += part_loss

    row = pl.BlockSpec((ts, D), lambda i: (i, 0))
    vec = pl.BlockSpec((1, D), lambda i: (0, 0))
    return pl.pallas_call(
        body, grid=(S // ts,), name="loss_final",
        in_specs=[row, vec, row],
        out_specs=[pl.BlockSpec((8, 128), lambda i: (0, 0)), row, row, vec],
        out_shape=[jax.ShapeDtypeStruct((8, 128), F32), jax.ShapeDtypeStruct((S, D), F32),
                   jax.ShapeDtypeStruct((S, D), BF16), jax.ShapeDtypeStruct((1, D), F32)],
        compiler_params=_params(("arbitrary",)),
    )(x, gain, target)


def matmul(name, a_ops, b_ops, *, grid, a_spec, b_spec, out_spec, out_shape, out_dtypes, acc_shape,
           trans_a=False, trans_b=False, res=None, res_spec=None):
    n_pairs = len(a_ops)
    n_out = len(out_dtypes)
    nk = grid[-1]
    kaxis = len(grid) - 1
    dn = (((0,) if trans_a else (1,), (1,) if trans_b else (0,)), ((), ()))

    def body(*refs):
        a_refs = refs[:n_pairs]
        b_refs = refs[n_pairs:2 * n_pairs]
        pos = 2 * n_pairs
        res_ref = None
        if res is not None:
            res_ref = refs[pos]
            pos += 1
        out_refs = refs[pos:pos + n_out]
        acc_ref = refs[pos + n_out]
        k = pl.program_id(kaxis)
        part = None
        for ar, br in zip(a_refs, b_refs):
            d = lax.dot_general(ar[...].astype(BF16), br[...].astype(BF16), dn, preferred_element_type=F32)
            part = d if part is None else part + d

        def finish(val):
            if res_ref is not None:
                val = val + res_ref[...]
            for o in out_refs:
                o[...] = val.astype(o.dtype)

        if nk == 1:
            finish(part)
        else:
            @pl.when(k == 0)
            def _():
                acc_ref[...] = part

            @pl.when(k > 0)
            def _():
                acc_ref[...] += part

            @pl.when(k == nk - 1)
            def _():
                finish(acc_ref[...])

    in_specs = [a_spec] * n_pairs + [b_spec] * n_pairs
    operands = list(a_ops) + list(b_ops)
    if res is not None:
        in_specs.append(res_spec)
        operands.append(res)
    return pl.pallas_call(
        body, grid=grid, name=name, in_specs=in_specs,
        out_specs=[out_spec] * n_out,
        out_shape=[jax.ShapeDtypeStruct(out_shape, dt) for dt in out_dtypes],
        scratch_shapes=[pltpu.VMEM(acc_shape, F32)],
        compiler_params=_params(("arbitrary",) * len(grid)),
    )(*operands)


def ffn_gate_up(h, wg, wu, tm=512):
    S, D = h.shape
    nb = wg.shape[2]

    def body(h_ref, wg_ref, wu_ref, g_ref, u_ref, a_ref):
        hv = h_ref[...]
        g = jnp.dot(hv, wg_ref[...], preferred_element_type=F32)
        u = jnp.dot(hv, wu_ref[...], preferred_element_type=F32)
        g_ref[...] = g
        u_ref[...] = u
        a_ref[...] = (g * _sigmoid(g) * u).astype(BF16)

    wspec = pl.BlockSpec((None, D, nb), lambda j, i: (j, 0, 0))
    ospec = pl.BlockSpec((None, tm, nb), lambda j, i: (j, i, 0))
    return pl.pallas_call(
        body, grid=(N_DEV, S // tm), name="ffn_gate_up",
        in_specs=[pl.BlockSpec((tm, D), lambda j, i: (i, 0)), wspec, wspec],
        out_specs=[ospec, ospec, ospec],
        out_shape=[jax.ShapeDtypeStruct((N_DEV, S, nb), F32), jax.ShapeDtypeStruct((N_DEV, S, nb), F32),
                   jax.ShapeDtypeStruct((N_DEV, S, nb), BF16)],
        compiler_params=_params(("arbitrary", "arbitrary")),
    )(h, wg, wu)


def ffn_bwd_hidden(dy, wd, g, u, tm=512):
    S, D = dy.shape
    nb = wd.shape[1]

    def body(dy_ref, wd_ref, g_ref, u_ref, dg_ref, du_ref):
        da = lax.dot_general(dy_ref[...], wd_ref[...], (((1,), (1,)), ((), ())), preferred_element_type=F32)
        gv = g_ref[...]
        s = _sigmoid(gv)
        du_ref[...] = (da * (gv * s)).astype(BF16)
        dg_ref[...] = (da * u_ref[...] * (s * (1.0 + gv * (1.0 - s)))).astype(BF16)

    hspec = pl.BlockSpec((None, tm, nb), lambda j, i: (j, i, 0))
    return pl.pallas_call(
        body, grid=(N_DEV, S // tm), name="ffn_bwd_hidden",
        in_specs=[pl.BlockSpec((tm, D), lambda j, i: (i, 0)), pl.BlockSpec((None, nb, D), lambda j, i: (j, 0, 0)),
                  hspec, hspec],
        out_specs=[hspec, hspec],
        out_shape=[jax.ShapeDtypeStruct((N_DEV, S, nb), BF16), jax.ShapeDtypeStruct((N_DEV, S, nb), BF16)],
        compiler_params=_params(("arbitrary", "arbitrary")),
    )(dy, wd, g, u)


def ffn_forward(h, xres, wg, wu, wd, tm=512):
    S, D = h.shape
    nb = wg.shape[2]
    g, u, a = ffn_gate_up(h, wg, wu)
    (xo,) = matmul(
        "ffn_down", [a], [wd], grid=(S // tm, N_DEV),
        a_spec=pl.BlockSpec((None, tm, nb), lambda i, j: (j, i, 0)),
        b_spec=pl.BlockSpec((None, nb, D), lambda i, j: (j, 0, 0)),
        out_spec=pl.BlockSpec((tm, D), lambda i, j: (i, 0)), out_shape=(S, D), out_dtypes=[F32],
        acc_shape=(tm, D), res=xres, res_spec=pl.BlockSpec((tm, D), lambda i, j: (i, 0)))
    return xo, (g, u, a)


def ffn_backward(dy_b, h, saved, wg, wu, wd, tm=512, tk=512):
    S, D = h.shape
    nb = wg.shape[2]
    g, u, a = saved
    dg, du = ffn_bwd_hidden(dy_b, wd, g, u)
    (dh,) = matmul(
        "ffn_dh", [dg, du], [wg, wu], grid=(S // tm, N_DEV),
        a_spec=pl.BlockSpec((None, tm, nb), lambda i, j: (j, i, 0)),
        b_spec=pl.BlockSpec((None, D, nb), lambda i, j: (j, 0, 0)),
        out_spec=pl.BlockSpec((tm, D), lambda i, j: (i, 0)), out_shape=(S, D), out_dtypes=[F32],
        acc_shape=(tm, D), trans_b=True)

    def wgrad_in(name, dhid):
        (dw,) = matmul(
            name, [h], [dhid], grid=(N_DEV, S // tk),
            a_spec=pl.BlockSpec((tk, D), lambda j, k: (k, 0)),
            b_spec=pl.BlockSpec((None, tk, nb), lambda j, k: (j, k, 0)),
            out_spec=pl.BlockSpec((None, D, nb), lambda j, k: (j, 0, 0)), out_shape=(N_DEV, D, nb),
            out_dtypes=[BF16], acc_shape=(D, nb), trans_a=True)
        return dw

    dwg = wgrad_in("ffn_dwg", dg)
    dwu = wgrad_in("ffn_dwu", du)
    (dwd,) = matmul(
        "ffn_dwd", [a], [dy_b], grid=(N_DEV, S // tk),
        a_spec=pl.BlockSpec((None, tk, nb), lambda j, k: (j, k, 0)),
        b_spec=pl.BlockSpec((tk, D), lambda j, k: (k, 0)),
        out_spec=pl.BlockSpec((None, nb, D), lambda j, k: (j, 0, 0)), out_shape=(N_DEV, nb, D),
        out_dtypes=[BF16], acc_shape=(nb, D), trans_a=True)
    return dh, dwg, dwu, dwd


def _pool_counts(row0, n, w):
    pos = row0 + lax.broadcasted_iota(jnp.int32, (n, 1), 0)
    return jnp.minimum(pos + 1, w).astype(F32)


def pool_forward(h, xres, w, scale, ts=256):
    S, D = h.shape
    G = len(POOL_WINDOWS)
    P = D // G
    hb = ts // POOL_HALO

    def body(h_ref, halo_ref, x_ref, w_ref, s_ref, xo_ref, p_ref):
        i = pl.program_id(0)
        for gi, win in enumerate(POOL_WINDOWS):
            cols = slice(gi * P, (gi + 1) * P)
            cur = h_ref[:, cols]
            halo = jnp.where(i > 0, halo_ref[:, cols], 0.0)
            acc = jnp.concatenate([halo, cur], axis=0)
            step = 1
            while step < win:
                acc = acc + pltpu.roll(acc, step, 0)
                step *= 2
            wsum = acc[POOL_HALO:, :]
            pooled = wsum / _pool_counts(i * ts, ts, win) - cur
            pb = pooled.astype(BF16)
            p_ref[:, cols] = pb
            mixed = jnp.dot(pb, w_ref[gi], preferred_element_type=F32)
            xo_ref[:, cols] = x_ref[:, cols] + mixed * s_ref[:, cols]

    row = pl.BlockSpec((ts, D), lambda i: (i, 0))
    return pl.pallas_call(
        body, grid=(S // ts,), name="pool_fwd",
        in_specs=[row, pl.BlockSpec((POOL_HALO, D), lambda i: (jnp.maximum(i * hb - 1, 0), 0)), row,
                  pl.BlockSpec((G, P, P), lambda i: (0, 0, 0)), pl.BlockSpec((1, D), lambda i: (0, 0))],
        out_specs=[row, row],
        out_shape=[jax.ShapeDtypeStruct((S, D), F32), jax.ShapeDtypeStruct((S, D), BF16)],
        compiler_params=_params(("arbitrary",)),
    )(h, h, xres, w, scale)


def pool_backward_mix(dx, pooled, w, scale, ts=256):
    S, D = dx.shape
    G = len(POOL_WINDOWS)
    P = D // G

    def body(dx_ref, p_ref, w_ref, s_ref, dm_ref, dp_ref, ds_ref):
        i = pl.program_id(0)
        parts = []
        for gi in range(G):
            cols = slice(gi * P, (gi + 1) * P)
            dxv = dx_ref[:, cols]
            dmb = (dxv * s_ref[:, cols]).astype(BF16)
            dm_ref[:, cols] = dmb
            dp_ref[:, cols] = lax.dot_general(dmb, w_ref[gi], (((1,), (1,)), ((), ())),
                                              preferred_element_type=F32)
            mixed = jnp.dot(p_ref[:, cols], w_ref[gi], preferred_element_type=F32)
            parts.append(jnp.sum(dxv * mixed, axis=0, keepdims=True))
        part = jnp.concatenate(parts, axis=1)

        @pl.when(i == 0)
        def _():
            ds_ref[...] = part

        @pl.when(i > 0)
        def _():
            ds_ref[...] += part

    row = pl.BlockSpec((ts, D), lambda i: (i, 0))
    vec = pl.BlockSpec((1, D), lambda i: (0, 0))
    return pl.pallas_call(
        body, grid=(S // ts,), name="pool_bwd_mix",
        in_specs=[row, row, pl.BlockSpec((G, P, P), lambda i: (0, 0, 0)), vec],
        out_specs=[row, row, vec],
        out_shape=[jax.ShapeDtypeStruct((S, D), BF16), jax.ShapeDtypeStruct((S, D), F32),
                   jax.ShapeDtypeStruct((1, D), F32)],
        compiler_params=_params(("arbitrary",)),
    )(dx, pooled, w, scale)


def pool_backward_window(dp, ts=256):
    S, D = dp.shape
    G = len(POOL_WINDOWS)
    P = D // G
    hb = ts // POOL_HALO
    n_i = S // ts
    n_rows = ts + POOL_HALO

    def body(dp_ref, halo_ref, dh_ref):
        i = pl.program_id(0)
        for gi, win in enumerate(POOL_WINDOWS):
            cols = slice(gi * P, (gi + 1) * P)
            cur = dp_ref[:, cols]
            halo = jnp.where(i < n_i - 1, halo_ref[:, cols], 0.0)
            acc = jnp.concatenate([cur / _pool_counts(i * ts, ts, win),
                                   halo / _pool_counts((i + 1) * ts, POOL_HALO, win)], axis=0)
            step = 1
            while step < win:
                acc = acc + pltpu.roll(acc, n_rows - step, 0)
                step *= 2
            dh_ref[:, cols] = acc[:ts, :] - cur

    row = pl.BlockSpec((ts, D), lambda i: (i, 0))
    return pl.pallas_call(
        body, grid=(n_i,), name="pool_bwd_window",
        in_specs=[row, pl.BlockSpec((POOL_HALO, D), lambda i: (jnp.minimum((i + 1) * hb, S // POOL_HALO - 1), 0))],
        out_specs=row,
        out_shape=jax.ShapeDtypeStruct((S, D), F32),
        compiler_params=_params(("arbitrary",)),
    )(dp, dp)


_HG_LEVELS = (32, 16, 8, 4, 2, 1)
_N_LEV = len(_HG_LEVELS) + 1


def _hgrn_constants():
    C = HG_CHUNK
    t = np.arange(C)
    tri = (t[None, :] <= t[:, None]).astype(np.float32)
    blocks = [tri]
    masks, upq, upk = [], [], []
    for m in _HG_LEVELS:
        p = (t // (2 * m)) * 2 * m + m - 1
        blocks.append(tri[p])
        masks.append(((t[:, None] // (2 * m)) == (t[None, :] // (2 * m))).astype(np.float32))
        upper = (t % (2 * m)) >= m
        upq.append(np.repeat(upper[:, None], HEAD, 1).astype(np.float32))
        upk.append(np.repeat(~upper[:, None], HEAD, 1).astype(np.float32))
    blocks.append(tri)
    masks.append(np.eye(C, dtype=np.float32))
    upq.append(np.ones((C, HEAD), np.float32))
    upk.append(np.ones((C, HEAD), np.float32))
    mstack = np.concatenate(blocks, axis=0)
    mstack3 = np.concatenate([mstack] * 3, axis=1)
    trirev3 = np.concatenate([tri.T] * 3, axis=1)
    return (jnp.asarray(mstack3, BF16), jnp.asarray(np.stack(masks)), jnp.asarray(np.stack(upq)),
            jnp.asarray(np.stack(upk)), jnp.asarray(trirev3, BF16))


def _split3(x):
    hi = x.astype(BF16)
    r1 = x - hi.astype(F32)
    mid = r1.astype(BF16)
    lo = (r1 - mid.astype(F32)).astype(BF16)
    return jnp.concatenate([hi, mid, lo], axis=0)


def _hgrn_chunk_common(qa, fa, lbv, mstack3, upq, upk):
    sq = _sigmoid(qa)
    q = qa * sq
    sf = _sigmoid(fa)
    f = lbv + (1.0 - lbv) * sf
    g = jnp.log(f)
    k = 1.0 - f
    gall = jnp.dot(mstack3, _split3(g), preferred_element_type=F32).reshape(_N_LEV + 1, HG_CHUNK, HEAD)
    G = gall[0]
    eq_exp = G[None] - gall[1:]
    eq = jnp.exp(jnp.minimum(eq_exp, 0.0)) * upq
    ek = jnp.exp(jnp.minimum(-eq_exp, 0.0)) * upk
    Qs = (q[None] * eq).astype(BF16)
    Ks = (k[None] * ek).astype(BF16)
    return sq, q, sf, f, k, G, eq, ek, Qs, Ks


def hgrn_forward(proj, lb, hg_norm, ts=512):
    S = proj.shape[0]
    nh = lb.shape[1] // HEAD
    C = HG_CHUNK
    ncs = ts // C
    mstack3, masks, upq, upk, _ = _hgrn_constants()

    def body(qa_ref, fa_ref, ia_ref, ga_ref, lb_ref, gn_ref, ms_ref, mk_ref, uq_ref, uk_ref,
             oa_ref, oraw_ref, st_ref, state):
        tt = pl.program_id(1)

        @pl.when(tt == 0)
        def _():
            state[...] = jnp.zeros_like(state)

        lbv = lb_ref[...]
        gn = gn_ref[...]

        def chunk(c, carry):
            sl = pl.ds(pl.multiple_of(c * C, C), C)
            qa, fa, v, ga = qa_ref[sl, :], fa_ref[sl, :], ia_ref[sl, :], ga_ref[sl, :]
            _, q, _, _, k, G, _, _, Qs, Ks = _hgrn_chunk_common(qa, fa, lbv, ms_ref[...], uq_ref[...], uk_ref[...])
            att7 = lax.dot_general(Qs, Ks, (((2,), (2,)), ((0,), (0,))), preferred_element_type=F32)
            att = jnp.sum(att7 * mk_ref[...], axis=0)
            st = state[...]
            st_ref[c] = st
            vb = v.astype(BF16)
            qg = (q * jnp.exp(G)).astype(BF16)
            o = jnp.dot(att.astype(BF16), vb, preferred_element_type=F32)
            o = o + lax.dot_general(qg, st.astype(BF16), (((1,), (1,)), ((), ())), preferred_element_type=F32)
            g_last = G[C - 1:C, :]
            kh = (k * jnp.exp(g_last - G)).astype(BF16)
            state[...] = st * jnp.exp(g_last) + lax.dot_general(vb, kh, (((0,), (0,)), ((), ())),
                                                                preferred_element_type=F32)
            oraw_ref[sl, :] = o
            r = lax.rsqrt(jnp.mean(o * o, axis=-1, keepdims=True) + RMS_EPS)
            oa_ref[sl, :] = (((o * r) * gn) * (ga * _sigmoid(ga))).astype(BF16)
            return carry

        lax.fori_loop(0, ncs, chunk, 0)

    def col(m0):
        return pl.BlockSpec((ts, HEAD), lambda h, t: (t, m0 + h))

    const3 = lambda shape: pl.BlockSpec(shape, lambda h, t: (0, 0, 0))
    return pl.pallas_call(
        body, grid=(nh, S // ts), name="hgrn_fwd",
        in_specs=[col(0), col(nh), col(2 * nh), col(3 * nh),
                  pl.BlockSpec((1, HEAD), lambda h, t: (0, h)), pl.BlockSpec((1, HEAD), lambda h, t: (0, 0)),
                  pl.BlockSpec(mstack3.shape, lambda h, t: (0, 0)), const3(masks.shape), const3(upq.shape),
                  const3(upk.shape)],
        out_specs=[pl.BlockSpec((ts, HEAD), lambda h, t: (t, h)), pl.BlockSpec((ts, HEAD), lambda h, t: (t, h)),
                   pl.BlockSpec((None, ncs, HEAD, HEAD), lambda h, t: (h, t, 0, 0))],
        out_shape=[jax.ShapeDtypeStruct((S, nh * HEAD), BF16), jax.ShapeDtypeStruct((S, nh * HEAD), F32),
                   jax.ShapeDtypeStruct((nh, S // C, HEAD, HEAD), F32)],
        scratch_shapes=[pltpu.VMEM((HEAD, HEAD), F32)],
        compiler_params=_params(("arbitrary", "arbitrary")),
    )(proj, proj, proj, proj, lb, hg_norm, mstack3, masks, upq, upk)


def hgrn_backward(dcat, proj, oraw, states, lb, hg_norm, ts=512):
    S = proj.shape[0]
    nh = lb.shape[1] // HEAD
    C = HG_CHUNK
    ncs = ts // C
    nt = S // ts
    mstack3, masks, upq, upk, trirev3 = _hgrn_constants()

    def body(do_ref, qa_ref, fa_ref, ia_ref, ga_ref, or_ref, st_ref, lb_ref, gn_ref, ms_ref, mk_ref, uq_ref,
             uk_ref, tr_ref, dqa_ref, dfa_ref, dia_ref, dga_ref, dlb_ref, dgn_ref, dstate):
        tt = pl.program_id(1)

        @pl.when(tt == 0)
        def _():
            dstate[...] = jnp.zeros_like(dstate)
            dlb_ref[...] = jnp.zeros_like(dlb_ref)
            dgn_ref[...] = jnp.zeros_like(dgn_ref)

        lbv = lb_ref[...]
        gn = gn_ref[...]

        def chunk(cc, carry):
            c = ncs - 1 - cc
            sl = pl.ds(pl.multiple_of(c * C, C), C)
            qa, fa, v, ga = qa_ref[sl, :], fa_ref[sl, :], ia_ref[sl, :], ga_ref[sl, :]
            sq, q, sf, f, k, G, eq, ek, Qs, Ks = _hgrn_chunk_common(qa, fa, lbv, ms_ref[...], uq_ref[...],
                                                                    uk_ref[...])
            mk = mk_ref[...]
            att7 = lax.dot_general(Qs, Ks, (((2,), (2,)), ((0,), (0,))), preferred_element_type=F32)
            att = jnp.sum(att7 * mk, axis=0)
            o = or_ref[sl, :]
            dO = do_ref[sl, :]
            sg = _sigmoid(ga)
            r = lax.rsqrt(jnp.mean(o * o, axis=-1, keepdims=True) + RMS_EPS)
            xh = o * r
            dga_ref[sl, :] = (dO * (xh * gn) * (sg * (1.0 + ga * (1.0 - sg)))).astype(BF16)
            don = dO * (ga * sg)
            dgn_ref[...] += jnp.sum(don * xh, axis=0, keepdims=True)
            dxh = don * gn
            do = r * (dxh - xh * jnp.mean(dxh * xh, axis=-1, keepdims=True))
            dob = do.astype(BF16)
            st = st_ref[c]
            dst = dstate[...]
            dstb = dst.astype(BF16)
            vb = v.astype(BF16)
            eG = jnp.exp(G)
            g_last = G[C - 1:C, :]
            e_last = jnp.exp(g_last)
            e_tail = jnp.exp(g_last - G)
            qg = (q * eG).astype(BF16)
            kh = (k * e_tail).astype(BF16)
            dq_inter = jnp.dot(dob, st.astype(BF16), preferred_element_type=F32) * eG
            dk_inter = jnp.dot(vb, dstb, preferred_element_type=F32) * e_tail
            dv = lax.dot_general(kh, dstb, (((1,), (1,)), ((), ())), preferred_element_type=F32)
            dv = dv + lax.dot_general(att.astype(BF16), dob, (((0,), (0,)), ((), ())), preferred_element_type=F32)
            dA = lax.dot_general(dob, vb, (((1,), (1,)), ((), ())), preferred_element_type=F32)
            dA7 = (dA[None] * mk).astype(BF16)
            dAT7 = (dA.T[None] * mk).astype(BF16)
            dQs = lax.dot_general(dA7, Ks, (((2,), (1,)), ((0,), (0,))), preferred_element_type=F32)
            dKs = lax.dot_general(dAT7, Qs, (((2,), (1,)), ((0,), (0,))), preferred_element_type=F32)
            dq = dq_inter + jnp.sum(dQs * eq, axis=0)
            dk = dk_inter + jnp.sum(dKs * ek, axis=0)
            dG = (jnp.sum(Qs.astype(F32) * dQs - Ks.astype(F32) * dKs, axis=0)
                  + q * dq_inter - k * dk_inter)
            last_extra = (jnp.sum(k * dk_inter, axis=0, keepdims=True)
                          + e_last * jnp.sum(dst * st, axis=0, keepdims=True))
            is_last = lax.broadcasted_iota(jnp.int32, (C, 1), 0) == C - 1
            dG = dG + jnp.where(is_last, last_extra, 0.0)
            dg = jnp.dot(tr_ref[...], _split3(dG), preferred_element_type=F32)
            df = dg / f - dk
            dfa_ref[sl, :] = (df * (1.0 - lbv) * (sf * (1.0 - sf))).astype(BF16)
            dlb_ref[...] += jnp.sum(df * (1.0 - sf), axis=0, keepdims=True)
            dqa_ref[sl, :] = (dq * (sq * (1.0 + qa * (1.0 - sq)))).astype(BF16)
            dia_ref[sl, :] = dv.astype(BF16)
            dstate[...] = dst * e_last + lax.dot_general(dob, qg, (((0,), (0,)), ((), ())),
                                                         preferred_element_type=F32)
            return carry

        lax.fori_loop(0, ncs, chunk, 0)

    def col(m0):
        return pl.BlockSpec((ts, HEAD), lambda h, t: (nt - 1 - t, m0 + h))

    const3 = lambda shape: pl.BlockSpec(shape, lambda h, t: (0, 0, 0))
    const2 = lambda shape: pl.BlockSpec(shape, lambda h, t: (0, 0))
    ocol = pl.BlockSpec((ts, HEAD), lambda h, t: (nt - 1 - t, h))
    half = nh * HEAD
    return pl.pallas_call(
        body, grid=(nh, nt), name="hgrn_bwd",
        in_specs=[col(0), col(0), col(nh), col(2 * nh), col(3 * nh), col(0),
                  pl.BlockSpec((None, ncs, HEAD, HEAD), lambda h, t: (h, nt - 1 - t, 0, 0)),
                  pl.BlockSpec((1, HEAD), lambda h, t: (0, h)), const2((1, HEAD)),
                  const2(mstack3.shape), const3(masks.shape), const3(upq.shape), const3(upk.shape),
                  const2(trirev3.shape)],
        out_specs=[ocol, ocol, ocol, ocol, pl.BlockSpec((1, HEAD), lambda h, t: (0, h)),
                   pl.BlockSpec((None, 1, HEAD), lambda h, t: (h, 0, 0))],
        out_shape=[jax.ShapeDtypeStruct((S, half), BF16)] * 4
                  + [jax.ShapeDtypeStruct((1, half), F32), jax.ShapeDtypeStruct((nh, 1, HEAD), F32)],
        scratch_shapes=[pltpu.VMEM((HEAD, HEAD), F32)],
        compiler_params=_params(("arbitrary", "arbitrary")),
    )(dcat, proj, proj, proj, proj, oraw, states, lb, hg_norm, mstack3, masks, upq, upk, trirev3)


SB_SUB = 128


def _split2(x):
    hi = x.astype(BF16)
    lo = (x - hi.astype(F32)).astype(BF16)
    return jnp.concatenate([hi, lo], axis=1)


def _sb_constants():
    j = np.arange(SB_SUB)
    after = (j[:, None] > j[None, :]).astype(np.float32)
    before = (j[:, None] < j[None, :]).astype(np.float32)
    return (jnp.asarray(np.concatenate([after, after], axis=0), BF16),
            jnp.asarray(np.concatenate([before, before], axis=0), BF16))


def _sb_diag_mask(t):
    return lax.broadcasted_iota(jnp.int32, (t, t), 1) < lax.broadcasted_iota(jnp.int32, (t, t), 0)


def _sb_scores(q, k_ref, col0, t, scale):
    ks = k_ref[pl.ds(pl.multiple_of(col0, t), t), :]
    return lax.dot_general(q, ks, (((1,), (1,)), ((), ())), preferred_element_type=F32) * scale


def _sb_weights(z, mask, run, after2):
    nsub = z.shape[1] // SB_SUB
    nz = -z
    lk = jnp.minimum(nz, 0.0) - jnp.log(1.0 + jnp.exp(jnp.minimum(z, nz)))
    if mask is not None:
        lk = jnp.where(mask, lk, 0.0)
    locs, tots = [], []
    for b in range(nsub):
        lkb = lk[:, b * SB_SUB:(b + 1) * SB_SUB]
        loc = jnp.dot(_split2(lkb), after2, preferred_element_type=F32)
        locs.append(loc)
        tots.append(loc[:, 0:1] + lkb[:, 0:1])
    ws = [None] * nsub
    for b in reversed(range(nsub)):
        sl = slice(b * SB_SUB, (b + 1) * SB_SUB)
        ws[b] = jnp.exp(z[:, sl] + lk[:, sl] + (locs[b] + run))
        run = run + tots[b]
    w = jnp.concatenate(ws, axis=1)
    if mask is not None:
        w = jnp.where(mask, w, 0.0)
    return w, run


def sb_forward(projb, nh, m0, t=512):
    S = projb.shape[0]
    scale = 1.0 / math.sqrt(HEAD)
    after2, _ = _sb_constants()

    def body(q_ref, k_ref, v_ref, af_ref, o_ref):
        i = pl.program_id(1)
        q = q_ref[...]
        after = af_ref[...]

        def block(jb, run, mask):
            z = _sb_scores(q, k_ref, jb * t, t, scale)
            w, run = _sb_weights(z, mask, run, after)
            vs = v_ref[pl.ds(pl.multiple_of(jb * t, t), t), :]
            return run, jnp.dot(w.astype(BF16), vs, preferred_element_type=F32)

        run, acc = block(i, jnp.zeros((t, 1), F32), _sb_diag_mask(t))

        def step(n, carry):
            run, acc = carry
            run, part = block(i - 1 - n, run, None)
            return run, acc + part

        _, acc = lax.fori_loop(0, i, step, (run, acc))
        o_ref[...] = acc.astype(BF16)

    return pl.pallas_call(
        body, grid=(nh, S // t), name="sb_fwd",
        in_specs=[pl.BlockSpec((t, HEAD), lambda h, i: (i, m0 + h)),
                  pl.BlockSpec((S, HEAD), lambda h, i: (0, m0 + nh + h)),
                  pl.BlockSpec((S, HEAD), lambda h, i: (0, m0 + 2 * nh + h)),
                  pl.BlockSpec(after2.shape, lambda h, i: (0, 0))],
        out_specs=pl.BlockSpec((t, HEAD), lambda h, i: (i, h)),
        out_shape=jax.ShapeDtypeStruct((S, nh * HEAD), BF16),
        compiler_params=_params(("arbitrary", "arbitrary")),
    )(projb, projb, projb, after2)


def sb_backward(dcat, projb, nh, m0, t=512):
    S = projb.shape[0]
    scale = 1.0 / math.sqrt(HEAD)
    after2, before2 = _sb_constants()
    n_i = S // t
    nsub = t // SB_SUB

    def body(do_ref, q_ref, k_ref, v_ref, af_ref, bf_ref, dq_ref, dk_ref, dv_ref, dbuf, dk_acc, dv_acc):
        i = pl.program_id(1)

        @pl.when(i == 0)
        def _():
            dk_acc[...] = jnp.zeros_like(dk_acc)
            dv_acc[...] = jnp.zeros_like(dv_acc)

        q = q_ref[...]
        dob = do_ref[...].astype(BF16)
        after = af_ref[...]
        before = bf_ref[...]

        def right_to_left(jb, run, mask):
            ksl = pl.ds(pl.multiple_of(jb * t, t), t)
            z = _sb_scores(q, k_ref, jb * t, t, scale)
            w, run = _sb_weights(z, mask, run, after)
            dw = lax.dot_general(dob, v_ref[ksl, :], (((1,), (1,)), ((), ())), preferred_element_type=F32)
            dbuf[jb] = dw * w
            dv_acc[ksl, :] += lax.dot_general(w.astype(BF16), dob, (((0,), (0,)), ((), ())),
                                              preferred_element_type=F32)
            return run

        run = right_to_left(i, jnp.zeros((t, 1), F32), _sb_diag_mask(t))
        lax.fori_loop(0, i, lambda n, run: right_to_left(i - 1 - n, run, None), run)

        def left_to_right(jb, run, dq, mask):
            ksl = pl.ds(pl.multiple_of(jb * t, t), t)
            z = _sb_scores(q, k_ref, jb * t, t, scale)
            d = dbuf[jb]
            sig = 1.0 / (1.0 + jnp.exp(-z))
            das = []
            for b in range(nsub):
                db = d[:, b * SB_SUB:(b + 1) * SB_SUB]
                prefix = run + jnp.dot(_split2(db), before, preferred_element_type=F32)
                das.append(db - sig[:, b * SB_SUB:(b + 1) * SB_SUB] * (db + prefix))
                run = prefix[:, SB_SUB - 1:SB_SUB] + db[:, SB_SUB - 1:SB_SUB]
            da = jnp.concatenate(das, axis=1)
            if mask is not None:
                da = jnp.where(mask, da, 0.0)
            dab = (da * scale).astype(BF16)
            dq = dq + jnp.dot(dab, k_ref[ksl, :], preferred_element_type=F32)
            dk_acc[ksl, :] += lax.dot_general(dab, q, (((0,), (0,)), ((), ())), preferred_element_type=F32)
            return run, dq

        run, dq = lax.fori_loop(0, i, lambda jb, c: left_to_right(jb, c[0], c[1], None),
                                (jnp.zeros((t, 1), F32), jnp.zeros((t, HEAD), F32)))
        _, dq = left_to_right(i, run, dq, _sb_diag_mask(t))
        dq_ref[...] = dq.astype(BF16)

        @pl.when(i == n_i - 1)
        def _():
            dk_ref[...] = dk_acc[...].astype(BF16)
            dv_ref[...] = dv_acc[...].astype(BF16)

    half = nh * HEAD
    full = pl.BlockSpec((S, HEAD), lambda h, i: (0, h))
    return pl.pallas_call(
        body, grid=(nh, n_i), name="sb_bwd",
        in_specs=[pl.BlockSpec((t, HEAD), lambda h, i: (i, nh + h)),
                  pl.BlockSpec((t, HEAD), lambda h, i: (i, m0 + h)),
                  pl.BlockSpec((S, HEAD), lambda h, i: (0, m0 + nh + h)),
                  pl.BlockSpec((S, HEAD), lambda h, i: (0, m0 + 2 * nh + h)),
                  pl.BlockSpec(after2.shape, lambda h, i: (0, 0)), pl.BlockSpec(before2.shape, lambda h, i: (0, 0))],
        out_specs=[pl.BlockSpec((t, HEAD), lambda h, i: (i, h)), full, full],
        out_shape=[jax.ShapeDtypeStruct((S, half), BF16)] * 3,
        scratch_shapes=[pltpu.VMEM((n_i, t, t), F32), pltpu.VMEM((S, HEAD), F32), pltpu.VMEM((S, HEAD), F32)],
        compiler_params=_params(("arbitrary", "arbitrary")),
    )(dcat, projb, projb, projb, after2, before2)


def local_step(x, target, mix_norm, ffn_norm, final_norm, lb_logits, hg_norm, w_in, w_out, pool_w, pool_scale,
               wg, wu, wd):
    S, D = x.shape
    half = D // 2
    nh = half // HEAD
    nbi = w_in.shape[2]
    tm = 512
    tk = 512
    row = lambda i, j: (i, 0)

    lb = jax.nn.softmax(lb_logits, axis=0)[0:1]

    h0, r0 = rms_fwd(x, mix_norm[0:1], BF16)
    proj, projb = matmul(
        "proj_in", [h0], [w_in], grid=(N_DEV, S // tm, 1),
        a_spec=pl.BlockSpec((tm, D), lambda j, i, k: (i, 0)),
        b_spec=pl.BlockSpec((None, D, nbi), lambda j, i, k: (j, 0, 0)),
        out_spec=pl.BlockSpec((tm, nbi), lambda j, i, k: (i, j)), out_shape=(S, N_DEV * nbi),
        out_dtypes=[F32, BF16], acc_shape=(8, 128))
    oa, oraw, states = hgrn_forward(proj, lb, hg_norm)
    ob = sb_forward(projb, nh, 4 * nh)
    cat = jnp.concatenate([oa, ob], axis=1)
    (x1,) = matmul(
        "mix_out", [cat], [w_out], grid=(S // tm, 1),
        a_spec=pl.BlockSpec((tm, D), row), b_spec=pl.BlockSpec((D, D), lambda i, k: (0, 0)),
        out_spec=pl.BlockSpec((tm, D), row), out_shape=(S, D), out_dtypes=[F32], acc_shape=(8, 128),
        res=x, res_spec=pl.BlockSpec((tm, D), row))
    h1, r1 = rms_fwd(x1, ffn_norm[0:1], BF16)
    x2, ffn0 = ffn_forward(h1, x1, wg[0], wu[0], wd[0])

    h2, r2 = rms_fwd(x2, mix_norm[1:2], F32)
    x3, pooled = pool_forward(h2, x2, pool_w, pool_scale)
    h3, r3 = rms_fwd(x3, ffn_norm[1:2], BF16)
    x4, ffn1 = ffn_forward(h3, x3, wg[1], wu[1], wd[1])

    loss_blk, dx4, dx4b, d_final = loss_and_final_bwd(x4, final_norm, target)

    dh3, dwg1, dwu1, dwd1 = ffn_backward(dx4b, h3, ffn1, wg[1], wu[1], wd[1])
    dx3, _, d_ffn1 = rms_bwd(dh3, x3, r3, ffn_norm[1:2], dx4)
    dmixed, dpooled, d_pscale = pool_backward_mix(dx3, pooled, pool_w, pool_scale)
    G = len(POOL_WINDOWS)
    P = D // G
    (d_pool_w,) = matmul(
        "pool_dw", [pooled], [dmixed], grid=(G, S // tk),
        a_spec=pl.BlockSpec((tk, P), lambda g, k: (k, g)), b_spec=pl.BlockSpec((tk, P), lambda g, k: (k, g)),
        out_spec=pl.BlockSpec((None, P, P), lambda g, k: (g, 0, 0)), out_shape=(G, P, P), out_dtypes=[BF16],
        acc_shape=(P, P), trans_a=True)
    dh2 = pool_backward_window(dpooled)
    dx2, dx2b, d_mix1 = rms_bwd(dh2, x2, r2, mix_norm[1:2], dx3)

    dh1, dwg0, dwu0, dwd0 = ffn_backward(dx2b, h1, ffn0, wg[0], wu[0], wd[0])
    dx1, dx1b, d_ffn0 = rms_bwd(dh1, x1, r1, ffn_norm[0:1], dx2)
    (dcat,) = matmul(
        "mix_out_dx", [dx1b], [w_out], grid=(S // tm, 1),
        a_spec=pl.BlockSpec((tm, D), row), b_spec=pl.BlockSpec((D, D), lambda i, k: (0, 0)),
        out_spec=pl.BlockSpec((tm, D), row), out_shape=(S, D), out_dtypes=[F32], acc_shape=(8, 128),
        trans_b=True)
    (d_w_out,) = matmul(
        "mix_out_dw", [cat], [dx1b], grid=(2, S // tk),
        a_spec=pl.BlockSpec((tk, half), lambda m, k: (k, m)), b_spec=pl.BlockSpec((tk, D), lambda m, k: (k, 0)),
        out_spec=pl.BlockSpec((half, D), lambda m, k: (m, 0)), out_shape=(D, D), out_dtypes=[BF16],
        acc_shape=(half, D), trans_a=True)
    dqa, dfa, dia, dga, d_lb, d_hg = hgrn_backward(dcat, proj, oraw, states, lb, hg_norm)
    dqb, dkb, dvb = sb_backward(dcat, projb, nh, 4 * nh)
    dproj = jnp.concatenate([dqa, dfa, dia, dga, dqb, dkb, dvb], axis=1)
    (dh0,) = matmul(
        "proj_in_dx", [dproj], [w_in], grid=(S // tm, N_DEV),
        a_spec=pl.BlockSpec((tm, nbi), lambda i, j: (i, j)),
        b_spec=pl.BlockSpec((None, D, nbi), lambda i, j: (j, 0, 0)),
        out_spec=pl.BlockSpec((tm, D), row), out_shape=(S, D), out_dtypes=[F32], acc_shape=(tm, D),
        trans_b=True)
    (d_w_in,) = matmul(
        "proj_in_dw", [h0], [dproj], grid=(N_DEV, S // tk),
        a_spec=pl.BlockSpec((tk, D), lambda j, k: (k, 0)), b_spec=pl.BlockSpec((tk, nbi), lambda j, k: (k, j)),
        out_spec=pl.BlockSpec((None, D, nbi), lambda j, k: (j, 0, 0)), out_shape=(N_DEV, D, nbi),
        out_dtypes=[BF16], acc_shape=(D, nbi), trans_a=True)
    dx0, _, d_mix0 = rms_bwd(dh0, x, r0, mix_norm[0:1], dx1)

    d_l0 = d_lb * lb * (1.0 - lb)
    small = dict(
        loss=loss_blk[0:1, 0:1],
        mix_norm=jnp.concatenate([d_mix0, d_mix1], axis=0),
        ffn_norm=jnp.concatenate([d_ffn0, d_ffn1], axis=0),
        final_norm=d_final,
        lb_logits=jnp.concatenate([d_l0, -d_l0], axis=0),
        hg_out_norm=jnp.sum(d_hg, axis=0),
        pool_scale=d_pscale,
    )
    big = dict(
        ab_w_in=d_w_in, ab_w_out=d_w_out, pool_w=d_pool_w,
        ffn_w_gate=[dwg0, dwg1], ffn_w_up=[dwu0, dwu1], ffn_w_down=[dwd0, dwd1],
    )
    return dx0, small, big


def _my_index():
    return 4 * lax.axis_index("x") + 2 * lax.axis_index("y") + lax.axis_index("c")


def _peer(r):
    x, y, c = lax.axis_index("x"), lax.axis_index("y"), lax.axis_index("c")
    px = 1 - x if (r >> 2) & 1 else x
    py = 1 - y if (r >> 1) & 1 else y
    pc = 1 - c if r & 1 else c
    return (px, py, pc), 4 * px + 2 * py + pc


def exchange(name, arrays, gather):
    n = len(arrays)
    n_peers = N_DEV - 1

    def body(*refs):
        ins, outs = refs[:n], refs[n:2 * n]
        send_sems, recv_sems, local_sems = refs[2 * n:]
        me = _my_index()
        local = []
        for a in range(n):
            src = ins[a] if gather else ins[a].at[me]
            cp = pltpu.make_async_copy(src, outs[a].at[me], local_sems.at[a])
            cp.start()
            local.append(cp)
        remote = []
        for a in range(n):
            for r in range(1, N_DEV):
                peer, pidx = _peer(r)
                src = ins[a] if gather else ins[a].at[pidx]
                cp = pltpu.make_async_remote_copy(
                    src_ref=src, dst_ref=outs[a].at[me], send_sem=send_sems.at[a * n_peers + r - 1],
                    recv_sem=recv_sems.at[a * n_peers + r - 1], device_id=peer, device_id_type=MESH)
                cp.start()
                remote.append((cp, a, r))
        for cp, a, r in remote:
            _, pidx = _peer(r)
            src = ins[a] if gather else ins[a].at[pidx]
            pltpu.make_async_remote_copy(
                src_ref=src, dst_ref=outs[a].at[pidx], send_sem=send_sems.at[a * n_peers + r - 1],
                recv_sem=recv_sems.at[a * n_peers + r - 1], device_id=_peer(r)[0], device_id_type=MESH).wait_recv()
        for cp, a, r in remote:
            cp.wait_send()
        for cp in local:
            cp.wait()

    out_shape = [jax.ShapeDtypeStruct(((N_DEV,) + a.shape) if gather else a.shape, a.dtype) for a in arrays]
    any_spec = pl.BlockSpec(memory_space=pl.ANY)
    return pl.pallas_call(
        body, name=name, in_specs=[any_spec] * n, out_specs=[any_spec] * n, out_shape=out_shape,
        scratch_shapes=[pltpu.SemaphoreType.DMA((n * n_peers,)), pltpu.SemaphoreType.DMA((n * n_peers,)),
                        pltpu.SemaphoreType.DMA((n,))],
    )(*arrays)


def _row_tile(rows, cap=256):
    best = None
    for t in range(16, min(rows, cap) + 1, 16):
        if rows % t == 0:
            best = t
    return best if best is not None else rows


def sum_slots(name, recv):
    n, R, C = recv.shape
    tr = _row_tile(R)

    def body(r_ref, o_ref):
        g = r_ref[0].astype(F32)
        for d in range(1, n):
            g = g + r_ref[d].astype(F32)
        o_ref[...] = g

    return pl.pallas_call(
        body, grid=(R // tr,), name=name,
        in_specs=[pl.BlockSpec((n, tr, C), lambda i: (0, i, 0))],
        out_specs=pl.BlockSpec((tr, C), lambda i: (i, 0)),
        out_shape=jax.ShapeDtypeStruct((R, C), F32),
        compiler_params=_params(("arbitrary",)),
    )(recv)


def adamw(name, recv, w, m, v):
    n, R, C = recv.shape
    tr = _row_tile(R)

    def body(r_ref, w_ref, m_ref, v_ref, g_ref, d_ref, nm_ref, nv_ref):
        g = r_ref[0].astype(F32)
        for d in range(1, n):
            g = g + r_ref[d].astype(F32)
        mm = ADAM_B1 * m_ref[...] + (1.0 - ADAM_B1) * g
        vv = ADAM_B2 * v_ref[...] + (1.0 - ADAM_B2) * (g * g)
        m_hat = mm / (1.0 - ADAM_B1 ** ADAM_STEP)
        v_hat = vv / (1.0 - ADAM_B2 ** ADAM_STEP)
        g_ref[...] = g
        d_ref[...] = -ADAM_LR * (m_hat / (jnp.sqrt(v_hat) + ADAM_EPS) + ADAM_WD * w_ref[...])
        nm_ref[...] = mm
        nv_ref[...] = vv

    row = pl.BlockSpec((tr, C), lambda i: (i, 0))
    return pl.pallas_call(
        body, grid=(R // tr,), name=name,
        in_specs=[pl.BlockSpec((n, tr, C), lambda i: (0, i, 0)), row, row, row],
        out_specs=[row] * 4,
        out_shape=[jax.ShapeDtypeStruct((R, C), F32)] * 4,
        compiler_params=_params(("arbitrary",)),
    )(recv, w, m, v)


def _adamw_nd(name, recv, w, m, v):
    shp = w.shape
    C = shp[-1]
    flat = lambda a: a.reshape(-1, C)
    outs = adamw(name, recv.reshape(recv.shape[0], -1, C), flat(w), flat(m), flat(v))
    return [o.reshape(shp) for o in outs]


_SMALL_NAMES = ("loss", "mix_norm", "ffn_norm", "final_norm", "lb_logits", "hg_out_norm", "pool_scale")
_LANES = 128


def _pack_small(parts):
    rows, layout = [], {}
    at = 0
    for name in parts:
        flat = parts[name].reshape(-1).astype(F32)
        n_rows = -(-flat.shape[0] // _LANES)
        flat = jnp.pad(flat, (0, n_rows * _LANES - flat.shape[0]))
        rows.append(flat.reshape(n_rows, _LANES))
        layout[name] = (at, parts[name].shape)
        at += n_rows
    pad = -at % 8
    if pad:
        rows.append(jnp.zeros((pad, _LANES), F32))
    return jnp.concatenate(rows, axis=0), layout


def _unpack_small(pack, layout):
    out = {}
    for name, (at, shape) in layout.items():
        size = int(np.prod(shape))
        n_rows = -(-size // _LANES)
        out[name] = pack[at:at + n_rows].reshape(-1)[:size].reshape(shape)
    return out


def kernel(x, mix_norm, ffn_norm, final_norm, ab_w_in, lb_logits, hg_out_norm, ab_w_out, pool_w, pool_scale, ffn_w_gate, ffn_w_up, ffn_w_down, loss_target, m_mix_norm, m_ffn_norm, m_final_norm, m_ab_w_in, m_lb_logits, m_hg_out_norm, m_ab_w_out, m_pool_w, m_pool_scale, m_ffn_w_gate, m_ffn_w_up, m_ffn_w_down, v_mix_norm, v_ffn_norm, v_final_norm, v_ab_w_in, v_lb_logits, v_hg_out_norm, v_ab_w_out, v_pool_w, v_pool_scale, v_ffn_w_gate, v_ffn_w_up, v_ffn_w_down):
    D = x.shape[-1]
    n_layers = ffn_w_gate.shape[0]
    G = pool_w.shape[1]
    P = pool_w.shape[3]
    me = _my_index()

    shards = [ab_w_in[0], ab_w_out[0], pool_w[0]]
    for l in range(n_layers):
        shards += [ffn_w_gate[l], ffn_w_up[l], ffn_w_down[l]]
    gathered = exchange("gather_weights", [s.astype(BF16) for s in shards] + [pool_scale], gather=True)
    pool_scale_g = gathered[-1].reshape(1, D)
    w_in_g = gathered[0]
    w_out_g = gathered[1].reshape(D, D)
    pool_g = gathered[2].transpose(1, 0, 2, 3).reshape(G, P, P)
    wg = [gathered[3 + 3 * l] for l in range(n_layers)]
    wu = [gathered[4 + 3 * l] for l in range(n_layers)]
    wd = [gathered[5 + 3 * l] for l in range(n_layers)]

    dx0, small, big = local_step(x[0], loss_target[0], mix_norm, ffn_norm, final_norm[None], lb_logits,
                                 hg_out_norm, w_in_g, w_out_g, pool_g, pool_scale_g, wg, wu, wd)

    d_pool_blocks = big["pool_w"].reshape(G, N_DEV, P // N_DEV, P).transpose(1, 0, 2, 3)
    sends = [big["ab_w_in"], big["ab_w_out"].reshape(N_DEV, D // N_DEV, D), d_pool_blocks]
    for l in range(n_layers):
        sends += [big["ffn_w_gate"][l], big["ffn_w_up"][l], big["ffn_w_down"][l]]
    recv = exchange("exchange_grads", sends, gather=False)
    small_pack, layout = _pack_small({k: small[k] for k in _SMALL_NAMES})
    (small_all,) = exchange("gather_small", [small_pack], gather=True)
    tot = _unpack_small(sum_slots("sum_small", small_all), layout)

    res = {}
    res["ab_w_in"] = _adamw_nd("adamw_w_in", recv[0], ab_w_in, m_ab_w_in, v_ab_w_in)
    res["ab_w_out"] = _adamw_nd("adamw_w_out", recv[1], ab_w_out, m_ab_w_out, v_ab_w_out)
    res["pool_w"] = _adamw_nd("adamw_pool_w", recv[2], pool_w, m_pool_w, v_pool_w)
    ffn_in = {"ffn_w_gate": (ffn_w_gate, m_ffn_w_gate, v_ffn_w_gate, 3),
              "ffn_w_up": (ffn_w_up, m_ffn_w_up, v_ffn_w_up, 4),
              "ffn_w_down": (ffn_w_down, m_ffn_w_down, v_ffn_w_down, 5)}
    for name, (w, m, v, at) in ffn_in.items():
        per_layer = [_adamw_nd("adamw_" + name, recv[at + 3 * l], w[l], m[l], v[l]) for l in range(n_layers)]
        res[name] = [jnp.stack([per_layer[l][o] for l in range(n_layers)]) for o in range(4)]

    n_ps = pool_scale.shape[1]
    small_g = dict(tot)
    small_g["pool_scale"] = lax.dynamic_slice(tot["pool_scale"], (0, me * n_ps), (1, n_ps))
    small_w = dict(mix_norm=(mix_norm, m_mix_norm, v_mix_norm), ffn_norm=(ffn_norm, m_ffn_norm, v_ffn_norm),
                   final_norm=(final_norm, m_final_norm, v_final_norm),
                   lb_logits=(lb_logits, m_lb_logits, v_lb_logits),
                   hg_out_norm=(hg_out_norm, m_hg_out_norm, v_hg_out_norm),
                   pool_scale=(pool_scale, m_pool_scale, v_pool_scale))
    g_pack, lay2 = _pack_small({k: small_g[k].reshape(small_w[k][0].shape) for k in small_w})
    w_pack, _ = _pack_small({k: small_w[k][0] for k in small_w})
    m_pack, _ = _pack_small({k: small_w[k][1] for k in small_w})
    v_pack, _ = _pack_small({k: small_w[k][2] for k in small_w})
    small_out = [_unpack_small(o, lay2) for o in adamw("adamw_small", g_pack[None], w_pack, m_pack, v_pack)]
    for k in small_w:
        res[k] = [small_out[o][k] for o in range(4)]

    order = ("mix_norm", "ffn_norm", "final_norm", "ab_w_in", "lb_logits", "hg_out_norm", "ab_w_out", "pool_w",
             "pool_scale", "ffn_w_gate", "ffn_w_up", "ffn_w_down")
    outs = [tot["loss"].reshape(()), dx0[None]]
    for o in range(4):
        outs += [res[k][o] for k in order]
    return tuple(outs)
```

```python
import functools
import math

import numpy as np
import jax
import jax.numpy as jnp
from jax import lax
from jax.experimental import pallas as pl
from jax.experimental.pallas import tpu as pltpu

F32 = jnp.float32
BF16 = jnp.bfloat16

N_DEV = 8
RMS_EPS = 1e-6
HEAD = 128
HG_CHUNK = 64
POOL_WINDOWS = (2, 4, 8, 16)
POOL_HALO = 16
ADAM_LR, ADAM_B1, ADAM_B2, ADAM_EPS, ADAM_WD, ADAM_STEP = 0.001, 0.9, 0.999, 1e-08, 0.01, 10
VMEM_LIMIT_BYTES = 60 * 1024 * 1024
MESH = pl.DeviceIdType.MESH


def _params(sem):
    return pltpu.CompilerParams(dimension_semantics=sem, vmem_limit_bytes=VMEM_LIMIT_BYTES)


def _sigmoid(x):
    return 1.0 / (1.0 + jnp.exp(-x))


def rms_fwd(x, gain, out_dtype, ts=512):
    S, D = x.shape

    def body(x_ref, g_ref, h_ref, r_ref):
        xv = x_ref[...]
        r = lax.rsqrt(jnp.mean(xv * xv, axis=-1, keepdims=True) + RMS_EPS)
        h_ref[...] = ((xv * r) * g_ref[...]).astype(h_ref.dtype)
        r_ref[...] = r

    return pl.pallas_call(
        body, grid=(S // ts,), name="rms_fwd",
        in_specs=[pl.BlockSpec((ts, D), lambda i: (i, 0)), pl.BlockSpec((1, D), lambda i: (0, 0))],
        out_specs=[pl.BlockSpec((ts, D), lambda i: (i, 0)), pl.BlockSpec((ts, 1), lambda i: (i, 0))],
        out_shape=[jax.ShapeDtypeStruct((S, D), out_dtype), jax.ShapeDtypeStruct((S, 1), F32)],
        compiler_params=_params(("arbitrary",)),
    )(x, gain)


def rms_bwd(dh, x, r, gain, dres, ts=512):
    S, D = x.shape

    def body(dh_ref, x_ref, r_ref, g_ref, dres_ref, dx_ref, dxb_ref, dg_ref):
        i = pl.program_id(0)
        rr = r_ref[...]
        xh = x_ref[...] * rr
        dhv = dh_ref[...]
        dxh = dhv * g_ref[...]
        dx = dres_ref[...] + rr * (dxh - xh * jnp.mean(dxh * xh, axis=-1, keepdims=True))
        dx_ref[...] = dx
        dxb_ref[...] = dx.astype(BF16)
        part = jnp.sum(dhv * xh, axis=0, keepdims=True)

        @pl.when(i == 0)
        def _():
            dg_ref[...] = part

        @pl.when(i > 0)
        def _():
            dg_ref[...] += part

    row = pl.BlockSpec((ts, D), lambda i: (i, 0))
    vec = pl.BlockSpec((1, D), lambda i: (0, 0))
    return pl.pallas_call(
        body, grid=(S // ts,), name="rms_bwd",
        in_specs=[row, row, pl.BlockSpec((ts, 1), lambda i: (i, 0)), vec, row],
        out_specs=[row, row, vec],
        out_shape=[jax.ShapeDtypeStruct((S, D), F32), jax.ShapeDtypeStruct((S, D), BF16),
                   jax.ShapeDtypeStruct((1, D), F32)],
        compiler_params=_params(("arbitrary",)),
    )(dh, x, r, gain, dres)


def loss_and_final_bwd(x, gain, target, ts=512):
    S, D = x.shape

    def body(x_ref, g_ref, t_ref, loss_ref, dx_ref, dxb_ref, dg_ref):
        i = pl.program_id(0)
        xv = x_ref[...]
        rr = lax.rsqrt(jnp.mean(xv * xv, axis=-1, keepdims=True) + RMS_EPS)
        xh = xv * rr
        err = xh * g_ref[...] - t_ref[...]
        part_loss = 0.5 * jnp.sum(jnp.mean(err * err, axis=-1, keepdims=True))
        dy = err / D
        dxh = dy * g_ref[...]
        dx = rr * (dxh - xh * jnp.mean(dxh * xh, axis=-1, keepdims=True))
        dx_ref[...] = dx
        dxb_ref[...] = dx.astype(BF16)
        part = jnp.sum(dy * xh, axis=0, keepdims=True)

        @pl.when(i == 0)
        def _():
            dg_ref[...] = part
            loss_ref[...] = jnp.zeros_like(loss_ref) + part_loss

        @pl.when(i > 0)
        def _():
            dg_ref[...] += part
            loss_ref[...] += part_loss

    row = pl.BlockSpec((ts, D), lambda i: (i, 0))
    vec = pl.BlockSpec((1, D), lambda i: (0, 0))
    return pl.pallas_call(
        body, grid=(S // ts,), name="loss_final",
        in_specs=[row, vec, row],
        out_specs=[pl.BlockSpec((8, 128), lambda i: (0, 0)), row, row, vec],
        out_shape=[jax.ShapeDtypeStruct((8, 128), F32), jax.ShapeDtypeStruct((S, D), F32),
                   jax.ShapeDtypeStruct((S, D), BF16), jax.ShapeDtypeStruct((1, D), F32)],
        compiler_params=_params(("arbitrary",)),
    )(x, gain, target)


def matmul(name, a_ops, b_ops, *, grid, a_spec, b_spec, out_spec, out_shape, out_dtypes, acc_shape,
           trans_a=False, trans_b=False, res=None, res_spec=None):
    n_pairs = len(a_ops)
    n_out = len(out_dtypes)
    nk = grid[-1]
    kaxis = len(grid) - 1
    dn = (((0,) if trans_a else (1,), (1,) if trans_b else (0,)), ((), ()))

    def body(*refs):
        a_refs = refs[:n_pairs]
        b_refs = refs[n_pairs:2 * n_pairs]
        pos = 2 * n_pairs
        res_ref = None
        if res is not None:
            res_ref = refs[pos]
            pos += 1
        out_refs = refs[pos:pos + n_out]
        acc_ref = refs[pos + n_out]
        k = pl.program_id(kaxis)
        part = None
        for ar, br in zip(a_refs, b_refs):
            d = lax.dot_general(ar[...].astype(BF16), br[...].astype(BF16), dn, preferred_element_type=F32)
            part = d if part is None else part + d

        def finish(val):
            if res_ref is not None:
                val = val + res_ref[...]
            for o in out_refs:
                o[...] = val.astype(o.dtype)

        if nk == 1:
            finish(part)
        else:
            @pl.when(k == 0)
            def _():
                acc_ref[...] = part

            @pl.when(k > 0)
            def _():
                acc_ref[...] += part

            @pl.when(k == nk - 1)
            def _():
                finish(acc_ref[...])

    in_specs = [a_spec] * n_pairs + [b_spec] * n_pairs
    operands = list(a_ops) + list(b_ops)
    if res is not None:
        in_specs.append(res_spec)
        operands.append(res)
    return pl.pallas_call(
        body, grid=grid, name=name, in_specs=in_specs,
        out_specs=[out_spec] * n_out,
        out_shape=[jax.ShapeDtypeStruct(out_shape, dt) for dt in out_dtypes],
        scratch_shapes=[pltpu.VMEM(acc_shape, F32)],
        compiler_params=_params(("arbitrary",) * len(grid)),
    )(*operands)


def ffn_gate_up(h, wg, wu, tm=512):
    S, D = h.shape
    nb = wg.shape[2]

    def body(h_ref, wg_ref, wu_ref, g_ref, u_ref, a_ref):
        hv = h_ref[...]
        g = jnp.dot(hv, wg_ref[...], preferred_element_type=F32)
        u = jnp.dot(hv, wu_ref[...], preferred_element_type=F32)
        g_ref[...] = g
        u_ref[...] = u
        a_ref[...] = (g * _sigmoid(g) * u).astype(BF16)

    wspec = pl.BlockSpec((None, D, nb), lambda j, i: (j, 0, 0))
    ospec = pl.BlockSpec((None, tm, nb), lambda j, i: (j, i, 0))
    return pl.pallas_call(
        body, grid=(N_DEV, S // tm), name="ffn_gate_up",
        in_specs=[pl.BlockSpec((tm, D), lambda j, i: (i, 0)), wspec, wspec],
        out_specs=[ospec, ospec, ospec],
        out_shape=[jax.ShapeDtypeStruct((N_DEV, S, nb), F32), jax.ShapeDtypeStruct((N_DEV, S, nb), F32),
                   jax.ShapeDtypeStruct((N_DEV, S, nb), BF16)],
        compiler_params=_params(("arbitrary", "arbitrary")),
    )(h, wg, wu)


def ffn_bwd_hidden(dy, wd, g, u, tm=512):
    S, D = dy.shape
    nb = wd.shape[1]

    def body(dy_ref, wd_ref, g_ref, u_ref, dg_ref, du_ref):
        da = lax.dot_general(dy_ref[...], wd_ref[...], (((1,), (1,)), ((), ())), preferred_element_type=F32)
        gv = g_ref[...]
        s = _sigmoid(gv)
        du_ref[...] = (da * (gv * s)).astype(BF16)
        dg_ref[...] = (da * u_ref[...] * (s * (1.0 + gv * (1.0 - s)))).astype(BF16)

    hspec = pl.BlockSpec((None, tm, nb), lambda j, i: (j, i, 0))
    return pl.pallas_call(
        body, grid=(N_DEV, S // tm), name="ffn_bwd_hidden",
        in_specs=[pl.BlockSpec((tm, D), lambda j, i: (i, 0)), pl.BlockSpec((None, nb, D), lambda j, i: (j, 0, 0)),
                  hspec, hspec],
        out_specs=[hspec, hspec],
        out_shape=[jax.ShapeDtypeStruct((N_DEV, S, nb), BF16), jax.ShapeDtypeStruct((N_DEV, S, nb), BF16)],
        compiler_params=_params(("arbitrary", "arbitrary")),
    )(dy, wd, g, u)


def ffn_forward(h, xres, wg, wu, wd, tm=512):
    S, D = h.shape
    nb = wg.shape[2]
    g, u, a = ffn_gate_up(h, wg, wu)
    (xo,) = matmul(
        "ffn_down", [a], [wd], grid=(S // tm, N_DEV),
        a_spec=pl.BlockSpec((None, tm, nb), lambda i, j: (j, i, 0)),
        b_spec=pl.BlockSpec((None, nb, D), lambda i, j: (j, 0, 0)),
        out_spec=pl.BlockSpec((tm, D), lambda i, j: (i, 0)), out_shape=(S, D), out_dtypes=[F32],
        acc_shape=(tm, D), res=xres, res_spec=pl.BlockSpec((tm, D), lambda i, j: (i, 0)))
    return xo, (g, u, a)


def ffn_backward(dy_b, h, saved, wg, wu, wd, tm=512, tk=512):
    S, D = h.shape
    nb = wg.shape[2]
    g, u, a = saved
    dg, du = ffn_bwd_hidden(dy_b, wd, g, u)
    (dh,) = matmul(
        "ffn_dh", [dg, du], [wg, wu], grid=(S // tm, N_DEV),
        a_spec=pl.BlockSpec((None, tm, nb), lambda i, j: (j, i, 0)),
        b_spec=pl.BlockSpec((None, D, nb), lambda i, j: (j, 0, 0)),
        out_spec=pl.BlockSpec((tm, D), lambda i, j: (i, 0)), out_shape=(S, D), out_dtypes=[F32],
        acc_shape=(tm, D), trans_b=True)

    def wgrad_in(name, dhid):
        (dw,) = matmul(
            name, [h], [dhid], grid=(N_DEV, S // tk),
            a_spec=pl.BlockSpec((tk, D), lambda j, k: (k, 0)),
            b_spec=pl.BlockSpec((None, tk, nb), lambda j, k: (j, k, 0)),
            out_spec=pl.BlockSpec((None, D, nb), lambda j, k: (j, 0, 0)), out_shape=(N_DEV, D, nb),
            out_dtypes=[BF16], acc_shape=(D, nb), trans_a=True)
        return dw

    dwg = wgrad_in("ffn_dwg", dg)
    dwu = wgrad_in("ffn_dwu", du)
    (dwd,) = matmul(
        "ffn_dwd", [a], [dy_b], grid=(N_DEV, S // tk),
        a_spec=pl.BlockSpec((None, tk, nb), lambda j, k: (j, k, 0)),
        b_spec=pl.BlockSpec((tk, D), lambda j, k: (k, 0)),
        out_spec=pl.BlockSpec((None, nb, D), lambda j, k: (j, 0, 0)), out_shape=(N_DEV, nb, D),
        out_dtypes=[BF16], acc_shape=(nb, D), trans_a=True)
    return dh, dwg, dwu, dwd


def _pool_counts(row0, n, w):
    pos = row0 + lax.broadcasted_iota(jnp.int32, (n, 1), 0)
    return jnp.minimum(pos + 1, w).astype(F32)


def pool_forward(h, xres, w, scale, ts=256):
    S, D = h.shape
    G = len(POOL_WINDOWS)
    P = D // G
    hb = ts // POOL_HALO

    def body(h_ref, halo_ref, x_ref, w_ref, s_ref, xo_ref, p_ref):
        i = pl.program_id(0)
        for gi, win in enumerate(POOL_WINDOWS):
            cols = slice(gi * P, (gi + 1) * P)
            cur = h_ref[:, cols]
            halo = jnp.where(i > 0, halo_ref[:, cols], 0.0)
            acc = jnp.concatenate([halo, cur], axis=0)
            step = 1
            while step < win:
                acc = acc + pltpu.roll(acc, step, 0)
                step *= 2
            wsum = acc[POOL_HALO:, :]
            pooled = wsum / _pool_counts(i * ts, ts, win) - cur
            pb = pooled.astype(BF16)
            p_ref[:, cols] = pb
            mixed = jnp.dot(pb, w_ref[gi], preferred_element_type=F32)
            xo_ref[:, cols] = x_ref[:, cols] + mixed * s_ref[:, cols]

    row = pl.BlockSpec((ts, D), lambda i: (i, 0))
    return pl.pallas_call(
        body, grid=(S // ts,), name="pool_fwd",
        in_specs=[row, pl.BlockSpec((POOL_HALO, D), lambda i: (jnp.maximum(i * hb - 1, 0), 0)), row,
                  pl.BlockSpec((G, P, P), lambda i: (0, 0, 0)), pl.BlockSpec((1, D), lambda i: (0, 0))],
        out_specs=[row, row],
        out_shape=[jax.ShapeDtypeStruct((S, D), F32), jax.ShapeDtypeStruct((S, D), BF16)],
        compiler_params=_params(("arbitrary",)),
    )(h, h, xres, w, scale)


def pool_backward_mix(dx, pooled, w, scale, ts=256):
    S, D = dx.shape
    G = len(POOL_WINDOWS)
    P = D // G

    def body(dx_ref, p_ref, w_ref, s_ref, dm_ref, dp_ref, ds_ref):
        i = pl.program_id(0)
        parts = []
        for gi in range(G):
            cols = slice(gi * P, (gi + 1) * P)
            dxv = dx_ref[:, cols]
            dmb = (dxv * s_ref[:, cols]).astype(BF16)
            dm_ref[:, cols] = dmb
            dp_ref[:, cols] = lax.dot_general(dmb, w_ref[gi], (((1,), (1,)), ((), ())),
                                              preferred_element_type=F32)
            mixed = jnp.dot(p_ref[:, cols], w_ref[gi], preferred_element_type=F32)
            parts.append(jnp.sum(dxv * mixed, axis=0, keepdims=True))
        part = jnp.concatenate(parts, axis=1)

        @pl.when(i == 0)
        def _():
            ds_ref[...] = part

        @pl.when(i > 0)
        def _():
            ds_ref[...] += part

    row = pl.BlockSpec((ts, D), lambda i: (i, 0))
    vec = pl.BlockSpec((1, D), lambda i: (0, 0))
    return pl.pallas_call(
        body, grid=(S // ts,), name="pool_bwd_mix",
        in_specs=[row, row, pl.BlockSpec((G, P, P), lambda i: (0, 0, 0)), vec],
        out_specs=[row, row, vec],
        out_shape=[jax.ShapeDtypeStruct((S, D), BF16), jax.ShapeDtypeStruct((S, D), F32),
                   jax.ShapeDtypeStruct((1, D), F32)],
        compiler_params=_params(("arbitrary",)),
    )(dx, pooled, w, scale)


def pool_backward_window(dp, ts=256):
    S, D = dp.shape
    G = len(POOL_WINDOWS)
    P = D // G
    hb = ts // POOL_HALO
    n_i = S // ts
    n_rows = ts + POOL_HALO

    def body(dp_ref, halo_ref, dh_ref):
        i = pl.program_id(0)
        for gi, win in enumerate(POOL_WINDOWS):
            cols = slice(gi * P, (gi + 1) * P)
            cur = dp_ref[:, cols]
            halo = jnp.where(i < n_i - 1, halo_ref[:, cols], 0.0)
            acc = jnp.concatenate([cur / _pool_counts(i * ts, ts, win),
                                   halo / _pool_counts((i + 1) * ts, POOL_HALO, win)], axis=0)
            step = 1
            while step < win:
                acc = acc + pltpu.roll(acc, n_rows - step, 0)
                step *= 2
            dh_ref[:, cols] = acc[:ts, :] - cur

    row = pl.BlockSpec((ts, D), lambda i: (i, 0))
    return pl.pallas_call(
        body, grid=(n_i,), name="pool_bwd_window",
        in_specs=[row, pl.BlockSpec((POOL_HALO, D), lambda i: (jnp.minimum((i + 1) * hb, S // POOL_HALO - 1), 0))],
        out_specs=row,
        out_shape=jax.ShapeDtypeStruct((S, D), F32),
        compiler_params=_params(("arbitrary",)),
    )(dp, dp)


_HG_LEVELS = (32, 16, 8, 4, 2, 1)
_N_LEV = len(_HG_LEVELS) + 1


def _hgrn_constants():
    C = HG_CHUNK
    t = np.arange(C)
    tri = (t[None, :] <= t[:, None]).astype(np.float32)
    blocks = [tri]
    masks, upq, upk = [], [], []
    for m in _HG_LEVELS:
        p = (t // (2 * m)) * 2 * m + m - 1
        blocks.append(tri[p])
        masks.append(((t[:, None] // (2 * m)) == (t[None, :] // (2 * m))).astype(np.float32))
        upper = (t % (2 * m)) >= m
        upq.append(np.repeat(upper[:, None], HEAD, 1).astype(np.float32))
        upk.append(np.repeat(~upper[:, None], HEAD, 1).astype(np.float32))
    blocks.append(tri)
    masks.append(np.eye(C, dtype=np.float32))
    upq.append(np.ones((C, HEAD), np.float32))
    upk.append(np.ones((C, HEAD), np.float32))
    mstack = np.concatenate(blocks, axis=0)
    mstack3 = np.concatenate([mstack] * 3, axis=1)
    trirev3 = np.concatenate([tri.T] * 3, axis=1)
    return (jnp.asarray(mstack3, BF16), jnp.asarray(np.stack(masks)), jnp.asarray(np.stack(upq)),
            jnp.asarray(np.stack(upk)), jnp.asarray(trirev3, BF16))


def _split3(x):
    hi = x.astype(BF16)
    r1 = x - hi.astype(F32)
    mid = r1.astype(BF16)
    lo = (r1 - mid.astype(F32)).astype(BF16)
    return jnp.concatenate([hi, mid, lo], axis=0)


def _hgrn_chunk_common(qa, fa, lbv, mstack3, upq, upk):
    sq = _sigmoid(qa)
    q = qa * sq
    sf = _sigmoid(fa)
    f = lbv + (1.0 - lbv) * sf
    g = jnp.log(f)
    k = 1.0 - f
    gall = jnp.dot(mstack3, _split3(g), preferred_element_type=F32).reshape(_N_LEV + 1, HG_CHUNK, HEAD)
    G = gall[0]
    eq_exp = G[None] - gall[1:]
    eq = jnp.exp(jnp.minimum(eq_exp, 0.0)) * upq
    ek = jnp.exp(jnp.minimum(-eq_exp, 0.0)) * upk
    Qs = (q[None] * eq).astype(BF16)
    Ks = (k[None] * ek).astype(BF16)
    return sq, q, sf, f, k, G, eq, ek, Qs, Ks


def hgrn_forward(proj, lb, hg_norm, ts=512):
    S = proj.shape[0]
    nh = lb.shape[1] // HEAD
    C = HG_CHUNK
    ncs = ts // C
    mstack3, masks, upq, upk, _ = _hgrn_constants()

    def body(qa_ref, fa_ref, ia_ref, ga_ref, lb_ref, gn_ref, ms_ref, mk_ref, uq_ref, uk_ref,
             oa_ref, oraw_ref, st_ref, state):
        tt = pl.program_id(1)

        @pl.when(tt == 0)
        def _():
            state[...] = jnp.zeros_like(state)

        lbv = lb_ref[...]
        gn = gn_ref[...]

        def chunk(c, carry):
            sl = pl.ds(pl.multiple_of(c * C, C), C)
            qa, fa, v, ga = qa_ref[sl, :], fa_ref[sl, :], ia_ref[sl, :], ga_ref[sl, :]
            _, q, _, _, k, G, _, _, Qs, Ks = _hgrn_chunk_common(qa, fa, lbv, ms_ref[...], uq_ref[...], uk_ref[...])
            att7 = lax.dot_general(Qs, Ks, (((2,), (2,)), ((0,), (0,))), preferred_element_type=F32)
            att = jnp.sum(att7 * mk_ref[...], axis=0)
            st = state[...]
            st_ref[c] = st
            vb = v.astype(BF16)
            qg = (q * jnp.exp(G)).astype(BF16)
            o = jnp.dot(att.astype(BF16), vb, preferred_element_type=F32)
            o = o + lax.dot_general(qg, st.astype(BF16), (((1,), (1,)), ((), ())), preferred_element_type=F32)
            g_last = G[C - 1:C, :]
            kh = (k * jnp.exp(g_last - G)).astype(BF16)
            state[...] = st * jnp.exp(g_last) + lax.dot_general(vb, kh, (((0,), (0,)), ((), ())),
                                                                preferred_element_type=F32)
            oraw_ref[sl, :] = o
            r = lax.rsqrt(jnp.mean(o * o, axis=-1, keepdims=True) + RMS_EPS)
            oa_ref[sl, :] = (((o * r) * gn) * (ga * _sigmoid(ga))).astype(BF16)
            return carry

        lax.fori_loop(0, ncs, chunk, 0)

    def col(m0):
        return pl.BlockSpec((ts, HEAD), lambda h, t: (t, m0 + h))

    const3 = lambda shape: pl.BlockSpec(shape, lambda h, t: (0, 0, 0))
    return pl.pallas_call(
        body, grid=(nh, S // ts), name="hgrn_fwd",
        in_specs=[col(0), col(nh), col(2 * nh), col(3 * nh),
                  pl.BlockSpec((1, HEAD), lambda h, t: (0, h)), pl.BlockSpec((1, HEAD), lambda h, t: (0, 0)),
                  pl.BlockSpec(mstack3.shape, lambda h, t: (0, 0)), const3(masks.shape), const3(upq.shape),
                  const3(upk.shape)],
        out_specs=[pl.BlockSpec((ts, HEAD), lambda h, t: (t, h)), pl.BlockSpec((ts, HEAD), lambda h, t: (t, h)),
                   pl.BlockSpec((None, ncs, HEAD, HEAD), lambda h, t: (h, t, 0, 0))],
        out_shape=[jax.ShapeDtypeStruct((S, nh * HEAD), BF16), jax.ShapeDtypeStruct((S, nh * HEAD), F32),
                   jax.ShapeDtypeStruct((nh, S // C, HEAD, HEAD), F32)],
        scratch_shapes=[pltpu.VMEM((HEAD, HEAD), F32)],
        compiler_params=_params(("arbitrary", "arbitrary")),
    )(proj, proj, proj, proj, lb, hg_norm, mstack3, masks, upq, upk)


def hgrn_backward(dcat, proj, oraw, states, lb, hg_norm, ts=512):
    S = proj.shape[0]
    nh = lb.shape[1] // HEAD
    C = HG_CHUNK
    ncs = ts // C
    nt = S // ts
    mstack3, masks, upq, upk, trirev3 = _hgrn_constants()

    def body(do_ref, qa_ref, fa_ref, ia_ref, ga_ref, or_ref, st_ref, lb_ref, gn_ref, ms_ref, mk_ref, uq_ref,
             uk_ref, tr_ref, dqa_ref, dfa_ref, dia_ref, dga_ref, dlb_ref, dgn_ref, dstate):
        tt = pl.program_id(1)

        @pl.when(tt == 0)
        def _():
            dstate[...] = jnp.zeros_like(dstate)
            dlb_ref[...] = jnp.zeros_like(dlb_ref)
            dgn_ref[...] = jnp.zeros_like(dgn_ref)

        lbv = lb_ref[...]
        gn = gn_ref[...]

        def chunk(cc, carry):
            c = ncs - 1 - cc
            sl = pl.ds(pl.multiple_of(c * C, C), C)
            qa, fa, v, ga = qa_ref[sl, :], fa_ref[sl, :], ia_ref[sl, :], ga_ref[sl, :]
            sq, q, sf, f, k, G, eq, ek, Qs, Ks = _hgrn_chunk_common(qa, fa, lbv, ms_ref[...], uq_ref[...],
                                                                    uk_ref[...])
            mk = mk_ref[...]
            att7 = lax.dot_general(Qs, Ks, (((2,), (2,)), ((0,), (0,))), preferred_element_type=F32)
            att = jnp.sum(att7 * mk, axis=0)
            o = or_ref[sl, :]
            dO = do_ref[sl, :]
            sg = _sigmoid(ga)
            r = lax.rsqrt(jnp.mean(o * o, axis=-1, keepdims=True) + RMS_EPS)
            xh = o * r
            dga_ref[sl, :] = (dO * (xh * gn) * (sg * (1.0 + ga * (1.0 - sg)))).astype(BF16)
            don = dO * (ga * sg)
            dgn_ref[...] += jnp.sum(don * xh, axis=0, keepdims=True)
            dxh = don * gn
            do = r * (dxh - xh * jnp.mean(dxh * xh, axis=-1, keepdims=True))
            dob = do.astype(BF16)
            st = st_ref[c]
            dst = dstate[...]
            dstb = dst.astype(BF16)
            vb = v.astype(BF16)
            eG = jnp.exp(G)
            g_last = G[C - 1:C, :]
            e_last = jnp.exp(g_last)
            e_tail = jnp.exp(g_last - G)
            qg = (q * eG).astype(BF16)
            kh = (k * e_tail).astype(BF16)
            dq_inter = jnp.dot(dob, st.astype(BF16), preferred_element_type=F32) * eG
            dk_inter = jnp.dot(vb, dstb, preferred_element_type=F32) * e_tail
            dv = lax.dot_general(kh, dstb, (((1,), (1,)), ((), ())), preferred_element_type=F32)
            dv = dv + lax.dot_general(att.astype(BF16), dob, (((0,), (0,)), ((), ())), preferred_element_type=F32)
            dA = lax.dot_general(dob, vb, (((1,), (1,)), ((), ())), preferred_element_type=F32)
            dA7 = (dA[None] * mk).astype(BF16)
            dAT7 = (dA.T[None] * mk).astype(BF16)
            dQs = lax.dot_general(dA7, Ks, (((2,), (1,)), ((0,), (0,))), preferred_element_type=F32)
            dKs = lax.dot_general(dAT7, Qs, (((2,), (1,)), ((0,), (0,))), preferred_element_type=F32)
            dq = dq_inter + jnp.sum(dQs * eq, axis=0)
            dk = dk_inter + jnp.sum(dKs * ek, axis=0)
            dG = (jnp.sum(Qs.astype(F32) * dQs - Ks.astype(F32) * dKs, axis=0)
                  + q * dq_inter - k * dk_inter)
            last_extra = (jnp.sum(k * dk_inter, axis=0, keepdims=True)
                          + e_last * jnp.sum(dst * st, axis=0, keepdims=True))
            is_last = lax.broadcasted_iota(jnp.int32, (C, 1), 0) == C - 1
            dG = dG + jnp.where(is_last, last_extra, 0.0)
            dg = jnp.dot(tr_ref[...], _split3(dG), preferred_element_type=F32)
            df = dg / f - dk
            dfa_ref[sl, :] = (df * (1.0 - lbv) * (sf * (1.0 - sf))).astype(BF16)
            dlb_ref[...] += jnp.sum(df * (1.0 - sf), axis=0, keepdims=True)
            dqa_ref[sl, :] = (dq * (sq * (1.0 + qa * (1.0 - sq)))).astype(BF16)
            dia_ref[sl, :] = dv.astype(BF16)
            dstate[...] = dst * e_last + lax.dot_general(dob, qg, (((0,), (0,)), ((), ())),
                                                         preferred_element_type=F32)
            return carry

        lax.fori_loop(0, ncs, chunk, 0)

    def col(m0):
        return pl.BlockSpec((ts, HEAD), lambda h, t: (nt - 1 - t, m0 + h))

    const3 = lambda shape: pl.BlockSpec(shape, lambda h, t: (0, 0, 0))
    const2 = lambda shape: pl.BlockSpec(shape, lambda h, t: (0, 0))
    ocol = pl.BlockSpec((ts, HEAD), lambda h, t: (nt - 1 - t, h))
    half = nh * HEAD
    return pl.pallas_call(
        body, grid=(nh, nt), name="hgrn_bwd",
        in_specs=[col(0), col(0), col(nh), col(2 * nh), col(3 * nh), col(0),
                  pl.BlockSpec((None, ncs, HEAD, HEAD), lambda h, t: (h, nt - 1 - t, 0, 0)),
                  pl.BlockSpec((1, HEAD), lambda h, t: (0, h)), const2((1, HEAD)),
                  const2(mstack3.shape), const3(masks.shape), const3(upq.shape), const3(upk.shape),
                  const2(trirev3.shape)],
        out_specs=[ocol, ocol, ocol, ocol, pl.BlockSpec((1, HEAD), lambda h, t: (0, h)),
                   pl.BlockSpec((None, 1, HEAD), lambda h, t: (h, 0, 0))],
        out_shape=[jax.ShapeDtypeStruct((S, half), BF16)] * 4
                  + [jax.ShapeDtypeStruct((1, half), F32), jax.ShapeDtypeStruct((nh, 1, HEAD), F32)],
        scratch_shapes=[pltpu.VMEM((HEAD, HEAD), F32)],
        compiler_params=_params(("arbitrary", "arbitrary")),
    )(dcat, proj, proj, proj, proj, oraw, states, lb, hg_norm, mstack3, masks, upq, upk, trirev3)


SB_SUB = 128


def _split2(x):
    hi = x.astype(BF16)
    lo = (x - hi.astype(F32)).astype(BF16)
    return jnp.concatenate([hi, lo], axis=1)


def _sb_constants():
    j = np.arange(SB_SUB)
    after = (j[:, None] > j[None, :]).astype(np.float32)
    before = (j[:, None] < j[None, :]).astype(np.float32)
    return (jnp.asarray(np.concatenate([after, after], axis=0), BF16),
            jnp.asarray(np.concatenate([before, before], axis=0), BF16))


def _sb_diag_mask(t):
    return lax.broadcasted_iota(jnp.int32, (t, t), 1) < lax.broadcasted_iota(jnp.int32, (t, t), 0)


def _sb_scores(q, k_ref, col0, t, scale):
    ks = k_ref[pl.ds(pl.multiple_of(col0, t), t), :]
    return lax.dot_general(q, ks, (((1,), (1,)), ((), ())), preferred_element_type=F32) * scale


def _sb_weights(z, mask, run, after2):
    nsub = z.shape[1] // SB_SUB
    nz = -z
    lk = jnp.minimum(nz, 0.0) - jnp.log(1.0 + jnp.exp(jnp.minimum(z, nz)))
    if mask is not None:
        lk = jnp.where(mask, lk, 0.0)
    locs, tots = [], []
    for b in range(nsub):
        lkb = lk[:, b * SB_SUB:(b + 1) * SB_SUB]
        loc = jnp.dot(_split2(lkb), after2, preferred_element_type=F32)
        locs.append(loc)
        tots.append(loc[:, 0:1] + lkb[:, 0:1])
    ws = [None] * nsub
    for b in reversed(range(nsub)):
        sl = slice(b * SB_SUB, (b + 1) * SB_SUB)
        ws[b] = jnp.exp(z[:, sl] + lk[:, sl] + (locs[b] + run))
        run = run + tots[b]
    w = jnp.concatenate(ws, axis=1)
    if mask is not None:
        w = jnp.where(mask, w, 0.0)
    return w, run


def sb_forward(projb, nh, m0, t=512):
    S = projb.shape[0]
    scale = 1.0 / math.sqrt(HEAD)
    after2, _ = _sb_constants()

    def body(q_ref, k_ref, v_ref, af_ref, o_ref):
        i = pl.program_id(1)
        q = q_ref[...]
        after = af_ref[...]

        def block(jb, run, mask):
            z = _sb_scores(q, k_ref, jb * t, t, scale)
            w, run = _sb_weights(z, mask, run, after)
            vs = v_ref[pl.ds(pl.multiple_of(jb * t, t), t), :]
            return run, jnp.dot(w.astype(BF16), vs, preferred_element_type=F32)

        run, acc = block(i, jnp.zeros((t, 1), F32), _sb_diag_mask(t))

        def step(n, carry):
            run, acc = carry
            run, part = block(i - 1 - n, run, None)
            return run, acc + part

        _, acc = lax.fori_loop(0, i, step, (run, acc))
        o_ref[...] = acc.astype(BF16)

    return pl.pallas_call(
        body, grid=(nh, S // t), name="sb_fwd",
        in_specs=[pl.BlockSpec((t, HEAD), lambda h, i: (i, m0 + h)),
                  pl.BlockSpec((S, HEAD), lambda h, i: (0, m0 + nh + h)),
                  pl.BlockSpec((S, HEAD), lambda h, i: (0, m0 + 2 * nh + h)),
                  pl.BlockSpec(after2.shape, lambda h, i: (0, 0))],
        out_specs=pl.BlockSpec((t, HEAD), lambda h, i: (i, h)),
        out_shape=jax.ShapeDtypeStruct((S, nh * HEAD), BF16),
        compiler_params=_params(("arbitrary", "arbitrary")),
    )(projb, projb, projb, after2)


def sb_backward(dcat, projb, nh, m0, t=512):
    S = projb.shape[0]
    scale = 1.0 / math.sqrt(HEAD)
    after2, before2 = _sb_constants()
    n_i = S // t
    nsub = t // SB_SUB

    def body(do_ref, q_ref, k_ref, v_ref, af_ref, bf_ref, dq_ref, dk_ref, dv_ref, dbuf, dk_acc, dv_acc):
        i = pl.program_id(1)

        @pl.when(i == 0)
        def _():
            dk_acc[...] = jnp.zeros_like(dk_acc)
            dv_acc[...] = jnp.zeros_like(dv_acc)

        q = q_ref[...]
        dob = do_ref[...].astype(BF16)
        after = af_ref[...]
        before = bf_ref[...]

        def right_to_left(jb, run, mask):
            ksl = pl.ds(pl.multiple_of(jb * t, t), t)
            z = _sb_scores(q, k_ref, jb * t, t, scale)
            w, run = _sb_weights(z, mask, run, after)
            dw = lax.dot_general(dob, v_ref[ksl, :], (((1,), (1,)), ((), ())), preferred_element_type=F32)
            dbuf[jb] = dw * w
            dv_acc[ksl, :] += lax.dot_general(w.astype(BF16), dob, (((0,), (0,)), ((), ())),
                                              preferred_element_type=F32)
            return run

        run = right_to_left(i, jnp.zeros((t, 1), F32), _sb_diag_mask(t))
        lax.fori_loop(0, i, lambda n, run: right_to_left(i - 1 - n, run, None), run)

        def left_to_right(jb, run, dq, mask):
            ksl = pl.ds(pl.multiple_of(jb * t, t), t)
            z = _sb_scores(q, k_ref, jb * t, t, scale)
            d = dbuf[jb]
            sig = 1.0 / (1.0 + jnp.exp(-z))
            das = []
            for b in range(nsub):
                db = d[:, b * SB_SUB:(b + 1) * SB_SUB]
                prefix = run + jnp.dot(_split2(db), before, preferred_element_type=F32)
                das.append(db - sig[:, b * SB_SUB:(b + 1) * SB_SUB] * (db + prefix))
                run = prefix[:, SB_SUB - 1:SB_SUB] + db[:, SB_SUB - 1:SB_SUB]
            da = jnp.concatenate(das, axis=1)
            if mask is not None:
                da = jnp.where(mask, da, 0.0)
            dab = (da * scale).astype(BF16)
            dq = dq + jnp.dot(dab, k_ref[ksl, :], preferred_element_type=F32)
            dk_acc[ksl, :] += lax.dot_general(dab, q, (((0,), (0,)), ((), ())), preferred_element_type=F32)
            return run, dq

        run, dq = lax.fori_loop(0, i, lambda jb, c: left_to_right(jb, c[0], c[1], None),
                                (jnp.zeros((t, 1), F32), jnp.zeros((t, HEAD), F32)))
        _, dq = left_to_right(i, run, dq, _sb_diag_mask(t))
        dq_ref[...] = dq.astype(BF16)

        @pl.when(i == n_i - 1)
        def _():
            dk_ref[...] = dk_acc[...].astype(BF16)
            dv_ref[...] = dv_acc[...].astype(BF16)

    half = nh * HEAD
    full = pl.BlockSpec((S, HEAD), lambda h, i: (0, h))
    return pl.pallas_call(
        body, grid=(nh, n_i), name="sb_bwd",
        in_specs=[pl.BlockSpec((t, HEAD), lambda h, i: (i, nh + h)),
                  pl.BlockSpec((t, HEAD), lambda h, i: (i, m0 + h)),
                  pl.BlockSpec((S, HEAD), lambda h, i: (0, m0 + nh + h)),
                  pl.BlockSpec((S, HEAD), lambda h, i: (0, m0 + 2 * nh + h)),
                  pl.BlockSpec(after2.shape, lambda h, i: (0, 0)), pl.BlockSpec(before2.shape, lambda h, i: (0, 0))],
        out_specs=[pl.BlockSpec((t, HEAD), lambda h, i: (i, h)), full, full],
        out_shape=[jax.ShapeDtypeStruct((S, half), BF16)] * 3,
        scratch_shapes=[pltpu.VMEM((n_i, t, t), F32), pltpu.VMEM((S, HEAD), F32), pltpu.VMEM((S, HEAD), F32)],
        compiler_params=_params(("arbitrary", "arbitrary")),
    )(dcat, projb, projb, projb, after2, before2)


def local_step(x, target, mix_norm, ffn_norm, final_norm, lb_logits, hg_norm, get_w_in, get_w_rest, send):
    S, D = x.shape
    half = D // 2
    nh = half // HEAD
    tm = 512
    tk = 512
    row = lambda i, j: (i, 0)

    lb = jax.nn.softmax(lb_logits, axis=0)[0:1]

    h0, r0 = rms_fwd(x, mix_norm[0:1], BF16)
    w_in = get_w_in(h0)
    nbi = w_in.shape[2]
    proj, projb = matmul(
        "proj_in", [h0], [w_in], grid=(N_DEV, S // tm, 1),
        a_spec=pl.BlockSpec((tm, D), lambda j, i, k: (i, 0)),
        b_spec=pl.BlockSpec((None, D, nbi), lambda j, i, k: (j, 0, 0)),
        out_spec=pl.BlockSpec((tm, nbi), lambda j, i, k: (i, j)), out_shape=(S, N_DEV * nbi),
        out_dtypes=[F32, BF16], acc_shape=(8, 128))
    oa, oraw, states = hgrn_forward(proj, lb, hg_norm)
    ob = sb_forward(projb, nh, 4 * nh)
    cat = jnp.concatenate([oa, ob], axis=1)
    w_out, pool_w, pool_scale, wg, wu, wd = get_w_rest(cat)
    (x1,) = matmul(
        "mix_out", [cat], [w_out], grid=(S // tm, 1),
        a_spec=pl.BlockSpec((tm, D), row), b_spec=pl.BlockSpec((D, D), lambda i, k: (0, 0)),
        out_spec=pl.BlockSpec((tm, D), row), out_shape=(S, D), out_dtypes=[F32], acc_shape=(8, 128),
        res=x, res_spec=pl.BlockSpec((tm, D), row))
    h1, r1 = rms_fwd(x1, ffn_norm[0:1], BF16)
    x2, ffn0 = ffn_forward(h1, x1, wg[0], wu[0], wd[0])

    h2, r2 = rms_fwd(x2, mix_norm[1:2], F32)
    x3, pooled = pool_forward(h2, x2, pool_w, pool_scale)
    h3, r3 = rms_fwd(x3, ffn_norm[1:2], BF16)
    x4, ffn1 = ffn_forward(h3, x3, wg[1], wu[1], wd[1])

    loss_blk, dx4, dx4b, d_final = loss_and_final_bwd(x4, final_norm, target)

    dh3, dwg1, dwu1, dwd1 = ffn_backward(dx4b, h3, ffn1, wg[1], wu[1], wd[1])
    sent = send("ffn1", dict(ffn_w_gate_1=dwg1, ffn_w_up_1=dwu1, ffn_w_down_1=dwd1))
    dx3, _, d_ffn1 = rms_bwd(dh3, x3, r3, ffn_norm[1:2] + sent, dx4)
    dmixed, dpooled, d_pscale = pool_backward_mix(dx3, pooled, pool_w, pool_scale)
    G = len(POOL_WINDOWS)
    P = D // G
    (d_pool_w,) = matmul(
        "pool_dw", [pooled], [dmixed], grid=(G, S // tk),
        a_spec=pl.BlockSpec((tk, P), lambda g, k: (k, g)), b_spec=pl.BlockSpec((tk, P), lambda g, k: (k, g)),
        out_spec=pl.BlockSpec((None, P, P), lambda g, k: (g, 0, 0)), out_shape=(G, P, P), out_dtypes=[BF16],
        acc_shape=(P, P), trans_a=True)
    dh2 = pool_backward_window(dpooled)
    dx2, dx2b, d_mix1 = rms_bwd(dh2, x2, r2, mix_norm[1:2], dx3)

    dh1, dwg0, dwu0, dwd0 = ffn_backward(dx2b, h1, ffn0, wg[0], wu[0], wd[0])
    dx1, dx1b, d_ffn0 = rms_bwd(dh1, x1, r1, ffn_norm[0:1], dx2)
    (dcat,) = matmul(
        "mix_out_dx", [dx1b], [w_out], grid=(S // tm, 1),
        a_spec=pl.BlockSpec((tm, D), row), b_spec=pl.BlockSpec((D, D), lambda i, k: (0, 0)),
        out_spec=pl.BlockSpec((tm, D), row), out_shape=(S, D), out_dtypes=[F32], acc_shape=(8, 128),
        trans_b=True)
    (d_w_out,) = matmul(
        "mix_out_dw", [cat], [dx1b], grid=(2, S // tk),
        a_spec=pl.BlockSpec((tk, half), lambda m, k: (k, m)), b_spec=pl.BlockSpec((tk, D), lambda m, k: (k, 0)),
        out_spec=pl.BlockSpec((half, D), lambda m, k: (m, 0)), out_shape=(D, D), out_dtypes=[BF16],
        acc_shape=(half, D), trans_a=True)
    sent = send("layer0", dict(ffn_w_gate_0=dwg0, ffn_w_up_0=dwu0, ffn_w_down_0=dwd0, pool_w=d_pool_w,
                               ab_w_out=d_w_out))
    dqa, dfa, dia, dga, d_lb, d_hg = hgrn_backward(dcat, proj, oraw, states, lb, hg_norm + sent)
    dqb, dkb, dvb = sb_backward(dcat, projb, nh, 4 * nh)
    dproj = jnp.concatenate([dqa, dfa, dia, dga, dqb, dkb, dvb], axis=1)
    (d_w_in,) = matmul(
        "proj_in_dw", [h0], [dproj], grid=(N_DEV, S // tk),
        a_spec=pl.BlockSpec((tk, D), lambda j, k: (k, 0)), b_spec=pl.BlockSpec((tk, nbi), lambda j, k: (k, j)),
        out_spec=pl.BlockSpec((None, D, nbi), lambda j, k: (j, 0, 0)), out_shape=(N_DEV, D, nbi),
        out_dtypes=[BF16], acc_shape=(D, nbi), trans_a=True)
    sent = send("w_in", dict(ab_w_in=d_w_in))
    (dh0,) = matmul(
        "proj_in_dx", [dproj], [w_in], grid=(S // tm, N_DEV),
        a_spec=pl.BlockSpec((tm, nbi), lambda i, j: (i, j)),
        b_spec=pl.BlockSpec((None, D, nbi), lambda i, j: (j, 0, 0)),
        out_spec=pl.BlockSpec((tm, D), row), out_shape=(S, D), out_dtypes=[F32], acc_shape=(tm, D),
        trans_b=True)
    dx0, _, d_mix0 = rms_bwd(dh0, x, r0, mix_norm[0:1] + sent, dx1)

    d_l0 = d_lb * lb * (1.0 - lb)
    small = dict(
        loss=loss_blk[0:1, 0:1],
        mix_norm=jnp.concatenate([d_mix0, d_mix1], axis=0),
        ffn_norm=jnp.concatenate([d_ffn0, d_ffn1], axis=0),
        final_norm=d_final,
        lb_logits=jnp.concatenate([d_l0, -d_l0], axis=0),
        hg_out_norm=jnp.sum(d_hg, axis=0),
        pool_scale=d_pscale,
    )
    return dx0, small


def _my_index():
    return 4 * lax.axis_index("x") + 2 * lax.axis_index("y") + lax.axis_index("c")


def _peer(r):
    x, y, c = lax.axis_index("x"), lax.axis_index("y"), lax.axis_index("c")
    px = 1 - x if (r >> 2) & 1 else x
    py = 1 - y if (r >> 1) & 1 else y
    pc = 1 - c if r & 1 else c
    return (px, py, pc), 4 * px + 2 * py + pc


def exchange(name, arrays, gather):
    n = len(arrays)
    n_peers = N_DEV - 1

    def body(*refs):
        ins, outs = refs[:n], refs[n:2 * n]
        send_sems, recv_sems, local_sems = refs[2 * n:]
        me = _my_index()
        local = []
        for a in range(n):
            src = ins[a] if gather else ins[a].at[me]
            cp = pltpu.make_async_copy(src, outs[a].at[me], local_sems.at[a])
            cp.start()
            local.append(cp)
        remote = []
        for a in range(n):
            for r in range(1, N_DEV):
                peer, pidx = _peer(r)
                src = ins[a] if gather else ins[a].at[pidx]
                cp = pltpu.make_async_remote_copy(
                    src_ref=src, dst_ref=outs[a].at[me], send_sem=send_sems.at[a * n_peers + r - 1],
                    recv_sem=recv_sems.at[a * n_peers + r - 1], device_id=peer, device_id_type=MESH)
                cp.start()
                remote.append((cp, a, r))
        for cp, a, r in remote:
            _, pidx = _peer(r)
            src = ins[a] if gather else ins[a].at[pidx]
            pltpu.make_async_remote_copy(
                src_ref=src, dst_ref=outs[a].at[pidx], send_sem=send_sems.at[a * n_peers + r - 1],
                recv_sem=recv_sems.at[a * n_peers + r - 1], device_id=_peer(r)[0], device_id_type=MESH).wait_recv()
        for cp, a, r in remote:
            cp.wait_send()
        for cp in local:
            cp.wait()

    out_shape = [jax.ShapeDtypeStruct(((N_DEV,) + a.shape) if gather else a.shape, a.dtype) for a in arrays]
    any_spec = pl.BlockSpec(memory_space=pl.ANY)
    return pl.pallas_call(
        body, name=name, in_specs=[any_spec] * n, out_specs=[any_spec] * n, out_shape=out_shape,
        scratch_shapes=[pltpu.SemaphoreType.DMA((n * n_peers,)), pltpu.SemaphoreType.DMA((n * n_peers,)),
                        pltpu.SemaphoreType.DMA((n,))],
    )(*arrays)


_HBM = pl.BlockSpec(memory_space=pltpu.HBM)
_SEM = pl.BlockSpec(memory_space=pltpu.SEMAPHORE)
_EFFECT = pltpu.SideEffectType.DATAFLOW_SIDE_EFFECTING


def _landing(arrays, gather):
    me = _my_index()
    lands = []
    for a in arrays:
        own = a[None] if gather else lax.dynamic_slice_in_dim(a, me, 1, axis=0)
        shape = ((N_DEV,) + a.shape) if gather else a.shape
        lands.append(lax.dynamic_update_slice_in_dim(lax.empty(shape, a.dtype), own, me, axis=0))
    return lands


def exchange_start(name, arrays, gather):
    n = len(arrays)
    n_peers = N_DEV - 1
    lands = _landing(arrays, gather)

    def body(*refs):
        src, land = refs[:n], refs[n:2 * n]
        send_sems, recv_sems = refs[2 * n], refs[2 * n + 1]
        token = refs[-1]
        me = _my_index()
        for a in range(n):
            for r in range(1, N_DEV):
                peer, pidx = _peer(r)
                pltpu.make_async_remote_copy(
                    src_ref=src[a] if gather else src[a].at[pidx], dst_ref=land[a].at[me],
                    send_sem=send_sems.at[a * n_peers + r - 1], recv_sem=recv_sems.at[a * n_peers + r - 1],
                    device_id=peer, device_id_type=MESH).start()
        token[...] = jnp.zeros_like(token)

    thru = [pltpu.HBM(a.shape, a.dtype) for a in arrays] + [pltpu.HBM(l.shape, l.dtype) for l in lands]
    outs = pl.pallas_call(
        body, name=name,
        out_shape=(pltpu.SemaphoreType.DMA((n * n_peers,)), pltpu.SemaphoreType.DMA((n * n_peers,)), *thru,
                   jax.ShapeDtypeStruct((8, 128), F32)),
        in_specs=[_HBM] * (2 * n),
        out_specs=(_SEM, _SEM, *([_HBM] * (2 * n)), pl.BlockSpec(memory_space=pltpu.VMEM)),
        input_output_aliases={i: 2 + i for i in range(2 * n)},
        compiler_params=pltpu.CompilerParams(has_side_effects=_EFFECT),
    )(*[pltpu.with_memory_space_constraint(a, pltpu.HBM) for a in list(arrays) + lands])
    handle = (outs[0], outs[1], list(outs[2:2 + n]), list(outs[2 + n:2 + 2 * n]), gather)
    return handle, outs[-1][0:1, 0:1]


def exchange_wait(name, handle, after):
    send_sems, recv_sems, srcs, lands, gather = handle
    n = len(srcs)
    n_peers = N_DEV - 1

    def body(*refs):
        src, land = refs[:n], refs[n:2 * n]
        send_s, recv_s = refs[2 * n], refs[2 * n + 1]
        for a in range(n):
            for r in range(1, N_DEV):
                peer, pidx = _peer(r)
                cp = pltpu.make_async_remote_copy(
                    src_ref=src[a] if gather else src[a].at[pidx], dst_ref=land[a].at[pidx],
                    send_sem=send_s.at[a * n_peers + r - 1], recv_sem=recv_s.at[a * n_peers + r - 1],
                    device_id=peer, device_id_type=MESH)
                cp.wait_send()
                cp.wait_recv()

    shapes = [pltpu.HBM(a.shape, a.dtype) for a in srcs] + [pltpu.HBM(l.shape, l.dtype) for l in lands]
    outs = pl.pallas_call(
        body, name=name, out_shape=tuple(shapes),
        in_specs=[_HBM] * (2 * n) + [_SEM, _SEM, pl.BlockSpec(memory_space=pl.ANY)],
        out_specs=tuple([_HBM] * (2 * n)),
        input_output_aliases={i: i for i in range(2 * n)},
        compiler_params=pltpu.CompilerParams(has_side_effects=_EFFECT),
    )(*srcs, *lands, send_sems, recv_sems, after)
    return list(outs[n:])


def _row_tile(rows, cap=256):
    best = None
    for t in range(16, min(rows, cap) + 1, 16):
        if rows % t == 0:
            best = t
    return best if best is not None else rows


def sum_slots(name, recv):
    n, R, C = recv.shape
    tr = _row_tile(R)

    def body(r_ref, o_ref):
        g = r_ref[0].astype(F32)
        for d in range(1, n):
            g = g + r_ref[d].astype(F32)
        o_ref[...] = g

    return pl.pallas_call(
        body, grid=(R // tr,), name=name,
        in_specs=[pl.BlockSpec((n, tr, C), lambda i: (0, i, 0))],
        out_specs=pl.BlockSpec((tr, C), lambda i: (i, 0)),
        out_shape=jax.ShapeDtypeStruct((R, C), F32),
        compiler_params=_params(("arbitrary",)),
    )(recv)


def adamw(name, recv, w, m, v):
    n, R, C = recv.shape
    tr = _row_tile(R)

    def body(r_ref, w_ref, m_ref, v_ref, g_ref, d_ref, nm_ref, nv_ref):
        g = r_ref[0].astype(F32)
        for d in range(1, n):
            g = g + r_ref[d].astype(F32)
        mm = ADAM_B1 * m_ref[...] + (1.0 - ADAM_B1) * g
        vv = ADAM_B2 * v_ref[...] + (1.0 - ADAM_B2) * (g * g)
        m_hat = mm / (1.0 - ADAM_B1 ** ADAM_STEP)
        v_hat = vv / (1.0 - ADAM_B2 ** ADAM_STEP)
        g_ref[...] = g
        d_ref[...] = -ADAM_LR * (m_hat / (jnp.sqrt(v_hat) + ADAM_EPS) + ADAM_WD * w_ref[...])
        nm_ref[...] = mm
        nv_ref[...] = vv

    row = pl.BlockSpec((tr, C), lambda i: (i, 0))
    return pl.pallas_call(
        body, grid=(R // tr,), name=name,
        in_specs=[pl.BlockSpec((n, tr, C), lambda i: (0, i, 0)), row, row, row],
        out_specs=[row] * 4,
        out_shape=[jax.ShapeDtypeStruct((R, C), F32)] * 4,
        compiler_params=_params(("arbitrary",)),
    )(recv, w, m, v)


def _adamw_nd(name, recv, w, m, v):
    shp = w.shape
    C = shp[-1]
    flat = lambda a: a.reshape(-1, C)
    outs = adamw(name, recv.reshape(recv.shape[0], -1, C), flat(w), flat(m), flat(v))
    return [o.reshape(shp) for o in outs]


_SMALL_NAMES = ("loss", "mix_norm", "ffn_norm", "final_norm", "lb_logits", "hg_out_norm", "pool_scale")
_LANES = 128


def _pack_small(parts):
    rows, layout = [], {}
    at = 0
    for name in parts:
        flat = parts[name].reshape(-1).astype(F32)
        n_rows = -(-flat.shape[0] // (8 * _LANES)) * 8
        flat = jnp.pad(flat, (0, n_rows * _LANES - flat.shape[0]))
        rows.append(flat.reshape(n_rows, _LANES))
        layout[name] = (at, parts[name].shape)
        at += n_rows
    return jnp.concatenate(rows, axis=0), layout


def _unpack_small(pack, layout):
    out = {}
    for name, (at, shape) in layout.items():
        size = int(np.prod(shape))
        n_rows = -(-size // _LANES)
        out[name] = pack[at:at + n_rows].reshape(-1)[:size].reshape(shape)
    return out


def kernel(x, mix_norm, ffn_norm, final_norm, ab_w_in, lb_logits, hg_out_norm, ab_w_out, pool_w, pool_scale, ffn_w_gate, ffn_w_up, ffn_w_down, loss_target, m_mix_norm, m_ffn_norm, m_final_norm, m_ab_w_in, m_lb_logits, m_hg_out_norm, m_ab_w_out, m_pool_w, m_pool_scale, m_ffn_w_gate, m_ffn_w_up, m_ffn_w_down, v_mix_norm, v_ffn_norm, v_final_norm, v_ab_w_in, v_lb_logits, v_hg_out_norm, v_ab_w_out, v_pool_w, v_pool_scale, v_ffn_w_gate, v_ffn_w_up, v_ffn_w_down):
    D = x.shape[-1]
    n_layers = ffn_w_gate.shape[0]
    G = pool_w.shape[1]
    P = pool_w.shape[3]
    me = _my_index()

    in_handle, started_in = exchange_start("gather_w_in_start", [ab_w_in[0].astype(BF16)], gather=True)
    rest = [ab_w_out[0], pool_w[0]]
    for l in range(n_layers):
        rest += [ffn_w_gate[l], ffn_w_up[l], ffn_w_down[l]]
    rest_handle, started_rest = exchange_start("gather_rest_start", [s.astype(BF16) for s in rest] + [pool_scale],
                                               gather=True)

    def get_w_in(after):
        return exchange_wait("gather_w_in_wait", in_handle, after)[0]

    def get_w_rest(after):
        got = exchange_wait("gather_rest_wait", rest_handle, after)
        w_out_g = got[0].reshape(D, D)
        pool_g = got[1].transpose(1, 0, 2, 3).reshape(G, P, P)
        wg = [got[2 + 3 * l] for l in range(n_layers)]
        wu = [got[3 + 3 * l] for l in range(n_layers)]
        wd = [got[4 + 3 * l] for l in range(n_layers)]
        return w_out_g, pool_g, got[-1].reshape(1, D), wg, wu, wd

    in_flight = []

    def send(tag, grads):
        if "pool_w" in grads:
            grads = dict(grads, pool_w=grads["pool_w"].reshape(G, N_DEV, P // N_DEV, P).transpose(1, 0, 2, 3))
        if "ab_w_out" in grads:
            grads = dict(grads, ab_w_out=grads["ab_w_out"].reshape(N_DEV, D // N_DEV, D))
        handle, started = exchange_start("grads_" + tag + "_start", list(grads.values()), gather=False)
        in_flight.append((tag, list(grads.keys()), handle))
        return started

    dx0, small = local_step(x[0], loss_target[0], mix_norm + (started_in + started_rest), ffn_norm, final_norm[None],
                            lb_logits, hg_out_norm, get_w_in, get_w_rest, send)

    recv = {}
    for tag, names, handle in in_flight:
        recv.update(zip(names, exchange_wait("grads_" + tag + "_wait", handle, dx0)))
    small_pack, layout = _pack_small({k: small[k] for k in _SMALL_NAMES})
    (small_all,) = exchange("gather_small", [small_pack], gather=True)
    tot = _unpack_small(sum_slots("sum_small", small_all), layout)

    res = {}
    res["ab_w_in"] = _adamw_nd("adamw_w_in", recv["ab_w_in"], ab_w_in, m_ab_w_in, v_ab_w_in)
    res["ab_w_out"] = _adamw_nd("adamw_w_out", recv["ab_w_out"], ab_w_out, m_ab_w_out, v_ab_w_out)
    res["pool_w"] = _adamw_nd("adamw_pool_w", recv["pool_w"], pool_w, m_pool_w, v_pool_w)
    ffn_in = {"ffn_w_gate": (ffn_w_gate, m_ffn_w_gate, v_ffn_w_gate),
              "ffn_w_up": (ffn_w_up, m_ffn_w_up, v_ffn_w_up),
              "ffn_w_down": (ffn_w_down, m_ffn_w_down, v_ffn_w_down)}
    for name, (w, m, v) in ffn_in.items():
        per_layer = [_adamw_nd("adamw_" + name, recv[name + "_" + str(l)], w[l], m[l], v[l])
                     for l in range(n_layers)]
        res[name] = [jnp.stack([per_layer[l][o] for l in range(n_layers)]) for o in range(4)]

    n_ps = pool_scale.shape[1]
    small_g = dict(tot)
    small_g["pool_scale"] = lax.dynamic_slice(tot["pool_scale"], (0, me * n_ps), (1, n_ps))
    small_w = dict(mix_norm=(mix_norm, m_mix_norm, v_mix_norm), ffn_norm=(ffn_norm, m_ffn_norm, v_ffn_norm),
                   final_norm=(final_norm, m_final_norm, v_final_norm),
                   lb_logits=(lb_logits, m_lb_logits, v_lb_logits),
                   hg_out_norm=(hg_out_norm, m_hg_out_norm, v_hg_out_norm),
                   pool_scale=(pool_scale, m_pool_scale, v_pool_scale))
    g_pack, lay2 = _pack_small({k: small_g[k].reshape(small_w[k][0].shape) for k in small_w})
    w_pack, _ = _pack_small({k: small_w[k][0] for k in small_w})
    m_pack, _ = _pack_small({k: small_w[k][1] for k in small_w})
    v_pack, _ = _pack_small({k: small_w[k][2] for k in small_w})
    small_out = [_unpack_small(o, lay2) for o in adamw("adamw_small", g_pack[None], w_pack, m_pack, v_pack)]
    for k in small_w:
        res[k] = [small_out[o][k] for o in range(4)]

    order = ("mix_norm", "ffn_norm", "final_norm", "ab_w_in", "lb_logits", "hg_out_norm", "ab_w_out", "pool_w",
             "pool_scale", "ffn_w_gate", "ffn_w_up", "ffn_w_down")
    outs = [tot["loss"].reshape(()), dx0[None]]
    for o in range(4):
        outs += [res[k][o] for k in order]
    return tuple(outs)
```

```python
import functools
import math

import numpy as np
import jax
import jax.numpy as jnp
from jax import lax
from jax.experimental import pallas as pl
from jax.experimental.pallas import tpu as pltpu

F32 = jnp.float32
BF16 = jnp.bfloat16

N_DEV = 8
RMS_EPS = 1e-6
HEAD = 128
HG_CHUNK = 64
POOL_WINDOWS = (2, 4, 8, 16)
POOL_HALO = 16
ADAM_LR, ADAM_B1, ADAM_B2, ADAM_EPS, ADAM_WD, ADAM_STEP = 0.001, 0.9, 0.999, 1e-08, 0.01, 10
VMEM_LIMIT_BYTES = 60 * 1024 * 1024
MESH = pl.DeviceIdType.MESH


def _params(sem):
    return pltpu.CompilerParams(dimension_semantics=sem, vmem_limit_bytes=VMEM_LIMIT_BYTES)


def _sigmoid(x):
    return 1.0 / (1.0 + jnp.exp(-x))


def rms_fwd(x, gain, out_dtype, ts=512):
    S, D = x.shape

    def body(x_ref, g_ref, h_ref, r_ref):
        xv = x_ref[...]
        r = lax.rsqrt(jnp.mean(xv * xv, axis=-1, keepdims=True) + RMS_EPS)
        h_ref[...] = ((xv * r) * g_ref[...]).astype(h_ref.dtype)
        r_ref[...] = r

    return pl.pallas_call(
        body, grid=(S // ts,), name="rms_fwd",
        in_specs=[pl.BlockSpec((ts, D), lambda i: (i, 0)), pl.BlockSpec((1, D), lambda i: (0, 0))],
        out_specs=[pl.BlockSpec((ts, D), lambda i: (i, 0)), pl.BlockSpec((ts, 1), lambda i: (i, 0))],
        out_shape=[jax.ShapeDtypeStruct((S, D), out_dtype), jax.ShapeDtypeStruct((S, 1), F32)],
        compiler_params=_params(("arbitrary",)),
    )(x, gain)


def rms_bwd(dh, x, r, gain, dres, ts=512):
    S, D = x.shape

    def body(dh_ref, x_ref, r_ref, g_ref, dres_ref, dx_ref, dxb_ref, dg_ref):
        i = pl.program_id(0)
        rr = r_ref[...]
        xh = x_ref[...] * rr
        dhv = dh_ref[...]
        dxh = dhv * g_ref[...]
        dx = dres_ref[...] + rr * (dxh - xh * jnp.mean(dxh * xh, axis=-1, keepdims=True))
        dx_ref[...] = dx
        dxb_ref[...] = dx.astype(BF16)
        part = jnp.sum(dhv * xh, axis=0, keepdims=True)

        @pl.when(i == 0)
        def _():
            dg_ref[...] = part

        @pl.when(i > 0)
        def _():
            dg_ref[...] += part

    row = pl.BlockSpec((ts, D), lambda i: (i, 0))
    vec = pl.BlockSpec((1, D), lambda i: (0, 0))
    return pl.pallas_call(
        body, grid=(S // ts,), name="rms_bwd",
        in_specs=[row, row, pl.BlockSpec((ts, 1), lambda i: (i, 0)), vec, row],
        out_specs=[row, row, vec],
        out_shape=[jax.ShapeDtypeStruct((S, D), F32), jax.ShapeDtypeStruct((S, D), BF16),
                   jax.ShapeDtypeStruct((1, D), F32)],
        compiler_params=_params(("arbitrary",)),
    )(dh, x, r, gain, dres)


def loss_and_final_bwd(x, gain, target, ts=512):
    S, D = x.shape

    def body(x_ref, g_ref, t_ref, loss_ref, dx_ref, dxb_ref, dg_ref):
        i = pl.program_id(0)
        xv = x_ref[...]
        rr = lax.rsqrt(jnp.mean(xv * xv, axis=-1, keepdims=True) + RMS_EPS)
        xh = xv * rr
        err = xh * g_ref[...] - t_ref[...]
        part_loss = 0.5 * jnp.sum(jnp.mean(err * err, axis=-1, keepdims=True))
        dy = err / D
        dxh = dy * g_ref[...]
        dx = rr * (dxh - xh * jnp.mean(dxh * xh, axis=-1, keepdims=True))
        dx_ref[...] = dx
        dxb_ref[...] = dx.astype(BF16)
        part = jnp.sum(dy * xh, axis=0, keepdims=True)

        @pl.when(i == 0)
        def _():
            dg_ref[...] = part
            loss_ref[...] = jnp.zeros_like(loss_ref) + part_loss

        @pl.when(i > 0)
        def _():
            dg_ref[...] += part
            loss_ref[...] += part_loss

    row = pl.BlockSpec((ts, D), lambda i: (i, 0))
    vec = pl.BlockSpec((1, D), lambda i: (0, 0))
    return pl.pallas_call(
        body, grid=(S // ts,), name="loss_final",
        in_specs=[row, vec, row],
        out_specs=[pl.BlockSpec((8, 128), lambda i: (0, 0)), row, row, vec],
        out_shape=[jax.ShapeDtypeStruct((8, 128), F32), jax.ShapeDtypeStruct((S, D), F32),
                   jax.ShapeDtypeStruct((S, D), BF16), jax.ShapeDtypeStruct((1, D), F32)],
        compiler_params=_params(("arbitrary",)),
    )(x, gain, target)


def matmul(name, a_ops, b_ops, *, grid, a_spec, b_spec, out_spec, out_shape, out_dtypes, acc_shape,
           trans_a=False, trans_b=False, res=None, res_spec=None):
    n_pairs = len(a_ops)
    n_out = len(out_dtypes)
    nk = grid[-1]
    kaxis = len(grid) - 1
    dn = (((0,) if trans_a else (1,), (1,) if trans_b else (0,)), ((), ()))

    def body(*refs):
        a_refs = refs[:n_pairs]
        b_refs = refs[n_pairs:2 * n_pairs]
        pos = 2 * n_pairs
        res_ref = None
        if res is not None:
            res_ref = refs[pos]
            pos += 1
        out_refs = refs[pos:pos + n_out]
        acc_ref = refs[pos + n_out]
        k = pl.program_id(kaxis)
        in_place = n_out == 1 and out_dtypes[0] == F32
        target = out_refs[0] if in_place else acc_ref

        def finish(val):
            if res_ref is not None:
                val = val + res_ref[...]
            for o in out_refs:
                o[...] = val.astype(o.dtype)

        if nk > 1:
            @pl.when(k == 0)
            def _():
                if in_place and res_ref is not None:
                    target[...] = res_ref[...]
                else:
                    target[...] = jnp.zeros_like(target)

        part = None
        for ar, br in zip(a_refs, b_refs):
            d = lax.dot_general(ar[...].astype(BF16), br[...].astype(BF16), dn, preferred_element_type=F32)
            part = d if part is None else part + d

        if nk == 1:
            finish(part)
        else:
            target[...] += part
            if not in_place:
                @pl.when(k == nk - 1)
                def _():
                    finish(acc_ref[...])

    in_specs = [a_spec] * n_pairs + [b_spec] * n_pairs
    operands = list(a_ops) + list(b_ops)
    if res is not None:
        in_specs.append(res_spec)
        operands.append(res)
    return pl.pallas_call(
        body, grid=grid, name=name, in_specs=in_specs,
        out_specs=[out_spec] * n_out,
        out_shape=[jax.ShapeDtypeStruct(out_shape, dt) for dt in out_dtypes],
        scratch_shapes=[pltpu.VMEM(acc_shape, F32)],
        compiler_params=_params(("arbitrary",) * len(grid)),
    )(*operands)


def ffn_gate_up(h, wg, wu, tm=512):
    S, D = h.shape
    nb = wg.shape[2]

    def body(h_ref, wg_ref, wu_ref, g_ref, u_ref, a_ref):
        for c in range(2):
            rows = slice(c * (tm // 2), (c + 1) * (tm // 2))
            hv = h_ref[rows, :]
            g = jnp.dot(hv, wg_ref[...], preferred_element_type=F32)
            u = jnp.dot(hv, wu_ref[...], preferred_element_type=F32)
            g_ref[rows, :] = g
            u_ref[rows, :] = u
            a_ref[rows, :] = (g * _sigmoid(g) * u).astype(BF16)

    wspec = pl.BlockSpec((None, D, nb), lambda j, i: (j, 0, 0))
    ospec = pl.BlockSpec((None, tm, nb), lambda j, i: (j, i, 0))
    return pl.pallas_call(
        body, grid=(N_DEV, S // tm), name="ffn_gate_up",
        in_specs=[pl.BlockSpec((tm, D), lambda j, i: (i, 0)), wspec, wspec],
        out_specs=[ospec, ospec, ospec],
        out_shape=[jax.ShapeDtypeStruct((N_DEV, S, nb), F32), jax.ShapeDtypeStruct((N_DEV, S, nb), F32),
                   jax.ShapeDtypeStruct((N_DEV, S, nb), BF16)],
        compiler_params=_params(("arbitrary", "arbitrary")),
    )(h, wg, wu)


def ffn_bwd_hidden(dy, wd, g, u, tm=512):
    S, D = dy.shape
    nb = wd.shape[1]

    def body(dy_ref, wd_ref, g_ref, u_ref, dg_ref, du_ref):
        for c in range(2):
            rows = slice(c * (tm // 2), (c + 1) * (tm // 2))
            da = lax.dot_general(dy_ref[rows, :], wd_ref[...], (((1,), (1,)), ((), ())),
                                 preferred_element_type=F32)
            gv = g_ref[rows, :]
            s = _sigmoid(gv)
            du_ref[rows, :] = (da * (gv * s)).astype(BF16)
            dg_ref[rows, :] = (da * u_ref[rows, :] * (s * (1.0 + gv * (1.0 - s)))).astype(BF16)

    hspec = pl.BlockSpec((None, tm, nb), lambda j, i: (j, i, 0))
    return pl.pallas_call(
        body, grid=(N_DEV, S // tm), name="ffn_bwd_hidden",
        in_specs=[pl.BlockSpec((tm, D), lambda j, i: (i, 0)), pl.BlockSpec((None, nb, D), lambda j, i: (j, 0, 0)),
                  hspec, hspec],
        out_specs=[hspec, hspec],
        out_shape=[jax.ShapeDtypeStruct((N_DEV, S, nb), BF16), jax.ShapeDtypeStruct((N_DEV, S, nb), BF16)],
        compiler_params=_params(("arbitrary", "arbitrary")),
    )(dy, wd, g, u)


def ffn_forward(h, xres, wg, wu, wd, tm=512):
    S, D = h.shape
    nb = wg.shape[2]
    g, u, a = ffn_gate_up(h, wg, wu)
    (xo,) = matmul(
        "ffn_down", [a], [wd], grid=(S // tm, N_DEV),
        a_spec=pl.BlockSpec((None, tm, nb), lambda i, j: (j, i, 0)),
        b_spec=pl.BlockSpec((None, nb, D), lambda i, j: (j, 0, 0)),
        out_spec=pl.BlockSpec((tm, D), lambda i, j: (i, 0)), out_shape=(S, D), out_dtypes=[F32],
        acc_shape=(tm, D), res=xres, res_spec=pl.BlockSpec((tm, D), lambda i, j: (i, 0)))
    return xo, (g, u, a)


def ffn_backward(dy_b, h, saved, wg, wu, wd, tm=512, tk=1024):
    S, D = h.shape
    nb = wg.shape[2]
    g, u, a = saved
    dg, du = ffn_bwd_hidden(dy_b, wd, g, u)
    (dh,) = matmul(
        "ffn_dh", [dg, du], [wg, wu], grid=(S // tm, N_DEV),
        a_spec=pl.BlockSpec((None, tm, nb), lambda i, j: (j, i, 0)),
        b_spec=pl.BlockSpec((None, D, nb), lambda i, j: (j, 0, 0)),
        out_spec=pl.BlockSpec((tm, D), lambda i, j: (i, 0)), out_shape=(S, D), out_dtypes=[F32],
        acc_shape=(tm, D), trans_b=True)

    def wgrad_in(name, dhid):
        (dw,) = matmul(
            name, [h], [dhid], grid=(N_DEV, S // tk),
            a_spec=pl.BlockSpec((tk, D), lambda j, k: (k, 0)),
            b_spec=pl.BlockSpec((None, tk, nb), lambda j, k: (j, k, 0)),
            out_spec=pl.BlockSpec((None, D, nb), lambda j, k: (j, 0, 0)), out_shape=(N_DEV, D, nb),
            out_dtypes=[BF16], acc_shape=(D, nb), trans_a=True)
        return dw

    dwg = wgrad_in("ffn_dwg", dg)
    dwu = wgrad_in("ffn_dwu", du)
    (dwd,) = matmul(
        "ffn_dwd", [a], [dy_b], grid=(N_DEV, S // tk),
        a_spec=pl.BlockSpec((None, tk, nb), lambda j, k: (j, k, 0)),
        b_spec=pl.BlockSpec((tk, D), lambda j, k: (k, 0)),
        out_spec=pl.BlockSpec((None, nb, D), lambda j, k: (j, 0, 0)), out_shape=(N_DEV, nb, D),
        out_dtypes=[BF16], acc_shape=(nb, D), trans_a=True)
    return dh, dwg, dwu, dwd


def _pool_counts(row0, n, w):
    pos = row0 + lax.broadcasted_iota(jnp.int32, (n, 1), 0)
    return jnp.minimum(pos + 1, w).astype(F32)


def pool_forward(h, xres, w, scale, ts=256):
    S, D = h.shape
    G = len(POOL_WINDOWS)
    P = D // G
    hb = ts // POOL_HALO

    def body(h_ref, halo_ref, x_ref, w_ref, s_ref, xo_ref, p_ref):
        i = pl.program_id(0)
        for gi, win in enumerate(POOL_WINDOWS):
            cols = slice(gi * P, (gi + 1) * P)
            cur = h_ref[:, cols]
            halo = jnp.where(i > 0, halo_ref[:, cols], 0.0)
            acc = jnp.concatenate([halo, cur], axis=0)
            step = 1
            while step < win:
                acc = acc + pltpu.roll(acc, step, 0)
                step *= 2
            wsum = acc[POOL_HALO:, :]
            pooled = wsum / _pool_counts(i * ts, ts, win) - cur
            pb = pooled.astype(BF16)
            p_ref[:, cols] = pb
            mixed = jnp.dot(pb, w_ref[gi], preferred_element_type=F32)
            xo_ref[:, cols] = x_ref[:, cols] + mixed * s_ref[:, cols]

    row = pl.BlockSpec((ts, D), lambda i: (i, 0))
    return pl.pallas_call(
        body, grid=(S // ts,), name="pool_fwd",
        in_specs=[row, pl.BlockSpec((POOL_HALO, D), lambda i: (jnp.maximum(i * hb - 1, 0), 0)), row,
                  pl.BlockSpec((G, P, P), lambda i: (0, 0, 0)), pl.BlockSpec((1, D), lambda i: (0, 0))],
        out_specs=[row, row],
        out_shape=[jax.ShapeDtypeStruct((S, D), F32), jax.ShapeDtypeStruct((S, D), BF16)],
        compiler_params=_params(("arbitrary",)),
    )(h, h, xres, w, scale)


def pool_backward_mix(dx, pooled, w, scale, ts=256):
    S, D = dx.shape
    G = len(POOL_WINDOWS)
    P = D // G

    def body(dx_ref, p_ref, w_ref, s_ref, dm_ref, dp_ref, ds_ref):
        i = pl.program_id(0)
        parts = []
        for gi in range(G):
            cols = slice(gi * P, (gi + 1) * P)
            dxv = dx_ref[:, cols]
            dmb = (dxv * s_ref[:, cols]).astype(BF16)
            dm_ref[:, cols] = dmb
            dp_ref[:, cols] = lax.dot_general(dmb, w_ref[gi], (((1,), (1,)), ((), ())),
                                              preferred_element_type=F32)
            mixed = jnp.dot(p_ref[:, cols], w_ref[gi], preferred_element_type=F32)
            parts.append(jnp.sum(dxv * mixed, axis=0, keepdims=True))
        part = jnp.concatenate(parts, axis=1)

        @pl.when(i == 0)
        def _():
            ds_ref[...] = part

        @pl.when(i > 0)
        def _():
            ds_ref[...] += part

    row = pl.BlockSpec((ts, D), lambda i: (i, 0))
    vec = pl.BlockSpec((1, D), lambda i: (0, 0))
    return pl.pallas_call(
        body, grid=(S // ts,), name="pool_bwd_mix",
        in_specs=[row, row, pl.BlockSpec((G, P, P), lambda i: (0, 0, 0)), vec],
        out_specs=[row, row, vec],
        out_shape=[jax.ShapeDtypeStruct((S, D), BF16), jax.ShapeDtypeStruct((S, D), F32),
                   jax.ShapeDtypeStruct((1, D), F32)],
        compiler_params=_params(("arbitrary",)),
    )(dx, pooled, w, scale)


def pool_backward_window(dp, ts=256):
    S, D = dp.shape
    G = len(POOL_WINDOWS)
    P = D // G
    hb = ts // POOL_HALO
    n_i = S // ts
    n_rows = ts + POOL_HALO

    def body(dp_ref, halo_ref, dh_ref):
        i = pl.program_id(0)
        for gi, win in enumerate(POOL_WINDOWS):
            cols = slice(gi * P, (gi + 1) * P)
            cur = dp_ref[:, cols]
            halo = jnp.where(i < n_i - 1, halo_ref[:, cols], 0.0)
            acc = jnp.concatenate([cur / _pool_counts(i * ts, ts, win),
                                   halo / _pool_counts((i + 1) * ts, POOL_HALO, win)], axis=0)
            step = 1
            while step < win:
                acc = acc + pltpu.roll(acc, n_rows - step, 0)
                step *= 2
            dh_ref[:, cols] = acc[:ts, :] - cur

    row = pl.BlockSpec((ts, D), lambda i: (i, 0))
    return pl.pallas_call(
        body, grid=(n_i,), name="pool_bwd_window",
        in_specs=[row, pl.BlockSpec((POOL_HALO, D), lambda i: (jnp.minimum((i + 1) * hb, S // POOL_HALO - 1), 0))],
        out_specs=row,
        out_shape=jax.ShapeDtypeStruct((S, D), F32),
        compiler_params=_params(("arbitrary",)),
    )(dp, dp)


_HG_LEVELS = (32, 16, 8, 4, 2, 1)
_N_LEV = len(_HG_LEVELS) + 1


def _hgrn_constants():
    C = HG_CHUNK
    t = np.arange(C)
    tri = (t[None, :] <= t[:, None]).astype(np.float32)
    blocks = [tri]
    masks, upq, upk = [], [], []
    for m in _HG_LEVELS:
        p = (t // (2 * m)) * 2 * m + m - 1
        blocks.append(tri[p])
        masks.append(((t[:, None] // (2 * m)) == (t[None, :] // (2 * m))).astype(np.float32))
        upper = (t % (2 * m)) >= m
        upq.append(np.repeat(upper[:, None], HEAD, 1).astype(np.float32))
        upk.append(np.repeat(~upper[:, None], HEAD, 1).astype(np.float32))
    blocks.append(tri)
    masks.append(np.eye(C, dtype=np.float32))
    upq.append(np.ones((C, HEAD), np.float32))
    upk.append(np.ones((C, HEAD), np.float32))
    mstack = np.concatenate(blocks, axis=0)
    mstack3 = np.concatenate([mstack] * 3, axis=1)
    trirev3 = np.concatenate([tri.T] * 3, axis=1)
    return (jnp.asarray(mstack3, BF16), jnp.asarray(np.stack(masks)), jnp.asarray(np.stack(upq)),
            jnp.asarray(np.stack(upk)), jnp.asarray(trirev3, BF16))


def _split3(x):
    hi = x.astype(BF16)
    r1 = x - hi.astype(F32)
    mid = r1.astype(BF16)
    lo = (r1 - mid.astype(F32)).astype(BF16)
    return jnp.concatenate([hi, mid, lo], axis=0)


def _hgrn_chunk_common(qa, fa, lbv, mstack3, upq, upk):
    sq = _sigmoid(qa)
    q = qa * sq
    sf = _sigmoid(fa)
    f = lbv + (1.0 - lbv) * sf
    g = jnp.log(f)
    k = 1.0 - f
    gall = jnp.dot(mstack3, _split3(g), preferred_element_type=F32).reshape(_N_LEV + 1, HG_CHUNK, HEAD)
    G = gall[0]
    eq_exp = G[None] - gall[1:]
    eq = jnp.exp(jnp.minimum(eq_exp, 0.0)) * upq
    ek = jnp.exp(jnp.minimum(-eq_exp, 0.0)) * upk
    Qs = (q[None] * eq).astype(BF16)
    Ks = (k[None] * ek).astype(BF16)
    return sq, q, sf, f, k, G, eq, ek, Qs, Ks


def hgrn_forward(proj, lb, hg_norm, ts=512):
    S = proj.shape[0]
    nh = lb.shape[1] // HEAD
    C = HG_CHUNK
    ncs = ts // C
    mstack3, masks, upq, upk, _ = _hgrn_constants()

    def body(qa_ref, fa_ref, ia_ref, ga_ref, lb_ref, gn_ref, ms_ref, mk_ref, uq_ref, uk_ref,
             oa_ref, oraw_ref, st_ref, state):
        tt = pl.program_id(1)

        @pl.when(tt == 0)
        def _():
            state[...] = jnp.zeros_like(state)

        lbv = lb_ref[...]
        gn = gn_ref[...]

        def chunk(c, carry):
            sl = pl.ds(pl.multiple_of(c * C, C), C)
            qa, fa, v, ga = qa_ref[sl, :], fa_ref[sl, :], ia_ref[sl, :], ga_ref[sl, :]
            _, q, _, _, k, G, _, _, Qs, Ks = _hgrn_chunk_common(qa, fa, lbv, ms_ref[...], uq_ref[...], uk_ref[...])
            att7 = lax.dot_general(Qs, Ks, (((2,), (2,)), ((0,), (0,))), preferred_element_type=F32)
            att = jnp.sum(att7 * mk_ref[...], axis=0)
            st = state[...]
            st_ref[c] = st
            vb = v.astype(BF16)
            qg = (q * jnp.exp(G)).astype(BF16)
            o = jnp.dot(att.astype(BF16), vb, preferred_element_type=F32)
            o = o + lax.dot_general(qg, st.astype(BF16), (((1,), (1,)), ((), ())), preferred_element_type=F32)
            g_last = G[C - 1:C, :]
            kh = (k * jnp.exp(g_last - G)).astype(BF16)
            state[...] = st * jnp.exp(g_last) + lax.dot_general(vb, kh, (((0,), (0,)), ((), ())),
                                                                preferred_element_type=F32)
            oraw_ref[sl, :] = o
            r = lax.rsqrt(jnp.mean(o * o, axis=-1, keepdims=True) + RMS_EPS)
            oa_ref[sl, :] = (((o * r) * gn) * (ga * _sigmoid(ga))).astype(BF16)
            return carry

        lax.fori_loop(0, ncs, chunk, 0)

    def col(m0):
        return pl.BlockSpec((ts, HEAD), lambda h, t: (t, m0 + h))

    const3 = lambda shape: pl.BlockSpec(shape, lambda h, t: (0, 0, 0))
    return pl.pallas_call(
        body, grid=(nh, S // ts), name="hgrn_fwd",
        in_specs=[col(0), col(nh), col(2 * nh), col(3 * nh),
                  pl.BlockSpec((1, HEAD), lambda h, t: (0, h)), pl.BlockSpec((1, HEAD), lambda h, t: (0, 0)),
                  pl.BlockSpec(mstack3.shape, lambda h, t: (0, 0)), const3(masks.shape), const3(upq.shape),
                  const3(upk.shape)],
        out_specs=[pl.BlockSpec((ts, HEAD), lambda h, t: (t, h)), pl.BlockSpec((ts, HEAD), lambda h, t: (t, h)),
                   pl.BlockSpec((None, ncs, HEAD, HEAD), lambda h, t: (h, t, 0, 0))],
        out_shape=[jax.ShapeDtypeStruct((S, nh * HEAD), BF16), jax.ShapeDtypeStruct((S, nh * HEAD), F32),
                   jax.ShapeDtypeStruct((nh, S // C, HEAD, HEAD), F32)],
        scratch_shapes=[pltpu.VMEM((HEAD, HEAD), F32)],
        compiler_params=_params(("arbitrary", "arbitrary")),
    )(proj, proj, proj, proj, lb, hg_norm, mstack3, masks, upq, upk)


def hgrn_backward(dcat, proj, oraw, states, lb, hg_norm, ts=512):
    S = proj.shape[0]
    nh = lb.shape[1] // HEAD
    C = HG_CHUNK
    ncs = ts // C
    nt = S // ts
    mstack3, masks, upq, upk, trirev3 = _hgrn_constants()

    def body(do_ref, qa_ref, fa_ref, ia_ref, ga_ref, or_ref, st_ref, lb_ref, gn_ref, ms_ref, mk_ref, uq_ref,
             uk_ref, tr_ref, dqa_ref, dfa_ref, dia_ref, dga_ref, dlb_ref, dgn_ref, dstate):
        tt = pl.program_id(1)

        @pl.when(tt == 0)
        def _():
            dstate[...] = jnp.zeros_like(dstate)
            dlb_ref[...] = jnp.zeros_like(dlb_ref)
            dgn_ref[...] = jnp.zeros_like(dgn_ref)

        lbv = lb_ref[...]
        gn = gn_ref[...]

        def chunk(cc, carry):
            c = ncs - 1 - cc
            sl = pl.ds(pl.multiple_of(c * C, C), C)
            qa, fa, v, ga = qa_ref[sl, :], fa_ref[sl, :], ia_ref[sl, :], ga_ref[sl, :]
            sq, q, sf, f, k, G, eq, ek, Qs, Ks = _hgrn_chunk_common(qa, fa, lbv, ms_ref[...], uq_ref[...],
                                                                    uk_ref[...])
            mk = mk_ref[...]
            att7 = lax.dot_general(Qs, Ks, (((2,), (2,)), ((0,), (0,))), preferred_element_type=F32)
            att = jnp.sum(att7 * mk, axis=0)
            o = or_ref[sl, :]
            dO = do_ref[sl, :]
            sg = _sigmoid(ga)
            r = lax.rsqrt(jnp.mean(o * o, axis=-1, keepdims=True) + RMS_EPS)
            xh = o * r
            dga_ref[sl, :] = (dO * (xh * gn) * (sg * (1.0 + ga * (1.0 - sg)))).astype(BF16)
            don = dO * (ga * sg)
            dgn_ref[...] += jnp.sum(don * xh, axis=0, keepdims=True)
            dxh = don * gn
            do = r * (dxh - xh * jnp.mean(dxh * xh, axis=-1, keepdims=True))
            dob = do.astype(BF16)
            st = st_ref[c]
            dst = dstate[...]
            dstb = dst.astype(BF16)
            vb = v.astype(BF16)
            eG = jnp.exp(G)
            g_last = G[C - 1:C, :]
            e_last = jnp.exp(g_last)
            e_tail = jnp.exp(g_last - G)
            qg = (q * eG).astype(BF16)
            kh = (k * e_tail).astype(BF16)
            dq_inter = jnp.dot(dob, st.astype(BF16), preferred_element_type=F32) * eG
            dk_inter = jnp.dot(vb, dstb, preferred_element_type=F32) * e_tail
            dv = lax.dot_general(kh, dstb, (((1,), (1,)), ((), ())), preferred_element_type=F32)
            dv = dv + lax.dot_general(att.astype(BF16), dob, (((0,), (0,)), ((), ())), preferred_element_type=F32)
            dA = lax.dot_general(dob, vb, (((1,), (1,)), ((), ())), preferred_element_type=F32)
            dA7 = (dA[None] * mk).astype(BF16)
            dAT7 = (dA.T[None] * mk).astype(BF16)
            dQs = lax.dot_general(dA7, Ks, (((2,), (1,)), ((0,), (0,))), preferred_element_type=F32)
            dKs = lax.dot_general(dAT7, Qs, (((2,), (1,)), ((0,), (0,))), preferred_element_type=F32)
            dq = dq_inter + jnp.sum(dQs * eq, axis=0)
            dk = dk_inter + jnp.sum(dKs * ek, axis=0)
            dG = (jnp.sum(Qs.astype(F32) * dQs - Ks.astype(F32) * dKs, axis=0)
                  + q * dq_inter - k * dk_inter)
            last_extra = (jnp.sum(k * dk_inter, axis=0, keepdims=True)
                          + e_last * jnp.sum(dst * st, axis=0, keepdims=True))
            is_last = lax.broadcasted_iota(jnp.int32, (C, 1), 0) == C - 1
            dG = dG + jnp.where(is_last, last_extra, 0.0)
            dg = jnp.dot(tr_ref[...], _split3(dG), preferred_element_type=F32)
            df = dg / f - dk
            dfa_ref[sl, :] = (df * (1.0 - lbv) * (sf * (1.0 - sf))).astype(BF16)
            dlb_ref[...] += jnp.sum(df * (1.0 - sf), axis=0, keepdims=True)
            dqa_ref[sl, :] = (dq * (sq * (1.0 + qa * (1.0 - sq)))).astype(BF16)
            dia_ref[sl, :] = dv.astype(BF16)
            dstate[...] = dst * e_last + lax.dot_general(dob, qg, (((0,), (0,)), ((), ())),
                                                         preferred_element_type=F32)
            return carry

        lax.fori_loop(0, ncs, chunk, 0)

    def col(m0):
        return pl.BlockSpec((ts, HEAD), lambda h, t: (nt - 1 - t, m0 + h))

    const3 = lambda shape: pl.BlockSpec(shape, lambda h, t: (0, 0, 0))
    const2 = lambda shape: pl.BlockSpec(shape, lambda h, t: (0, 0))
    ocol = pl.BlockSpec((ts, HEAD), lambda h, t: (nt - 1 - t, h))
    half = nh * HEAD
    return pl.pallas_call(
        body, grid=(nh, nt), name="hgrn_bwd",
        in_specs=[col(0), col(0), col(nh), col(2 * nh), col(3 * nh), col(0),
                  pl.BlockSpec((None, ncs, HEAD, HEAD), lambda h, t: (h, nt - 1 - t, 0, 0)),
                  pl.BlockSpec((1, HEAD), lambda h, t: (0, h)), const2((1, HEAD)),
                  const2(mstack3.shape), const3(masks.shape), const3(upq.shape), const3(upk.shape),
                  const2(trirev3.shape)],
        out_specs=[ocol, ocol, ocol, ocol, pl.BlockSpec((1, HEAD), lambda h, t: (0, h)),
                   pl.BlockSpec((None, 1, HEAD), lambda h, t: (h, 0, 0))],
        out_shape=[jax.ShapeDtypeStruct((S, half), BF16)] * 4
                  + [jax.ShapeDtypeStruct((1, half), F32), jax.ShapeDtypeStruct((nh, 1, HEAD), F32)],
        scratch_shapes=[pltpu.VMEM((HEAD, HEAD), F32)],
        compiler_params=_params(("arbitrary", "arbitrary")),
    )(dcat, proj, proj, proj, proj, oraw, states, lb, hg_norm, mstack3, masks, upq, upk, trirev3)


SB_SUB = 128


def _split2(x):
    hi = x.astype(BF16)
    lo = (x - hi.astype(F32)).astype(BF16)
    return jnp.concatenate([hi, lo], axis=1)


def _sb_constants():
    j = np.arange(SB_SUB)
    after = (j[:, None] > j[None, :]).astype(np.float32)
    before = (j[:, None] < j[None, :]).astype(np.float32)
    return (jnp.asarray(np.concatenate([after, after], axis=0), BF16),
            jnp.asarray(np.concatenate([before, before], axis=0), BF16))


def _sb_diag_mask(t):
    return lax.broadcasted_iota(jnp.int32, (t, t), 1) < lax.broadcasted_iota(jnp.int32, (t, t), 0)


def _sb_scores(q, k_ref, col0, t, scale):
    ks = k_ref[pl.ds(pl.multiple_of(col0, t), t), :]
    return lax.dot_general(q, ks, (((1,), (1,)), ((), ())), preferred_element_type=F32) * scale


def _sb_weights(z, mask, run, after2):
    nsub = z.shape[1] // SB_SUB
    nz = -z
    lk = jnp.minimum(nz, 0.0) - jnp.log(1.0 + jnp.exp(jnp.minimum(z, nz)))
    if mask is not None:
        lk = jnp.where(mask, lk, 0.0)
    locs, tots = [], []
    for b in range(nsub):
        lkb = lk[:, b * SB_SUB:(b + 1) * SB_SUB]
        loc = jnp.dot(_split2(lkb), after2, preferred_element_type=F32)
        locs.append(loc)
        tots.append(loc[:, 0:1] + lkb[:, 0:1])
    ws = [None] * nsub
    for b in reversed(range(nsub)):
        sl = slice(b * SB_SUB, (b + 1) * SB_SUB)
        ws[b] = jnp.exp(z[:, sl] + lk[:, sl] + (locs[b] + run))
        run = run + tots[b]
    w = jnp.concatenate(ws, axis=1)
    if mask is not None:
        w = jnp.where(mask, w, 0.0)
    return w, run


def sb_forward(projb, nh, m0, t=512):
    S = projb.shape[0]
    scale = 1.0 / math.sqrt(HEAD)
    after2, _ = _sb_constants()

    def body(q_ref, k_ref, v_ref, af_ref, o_ref):
        i = pl.program_id(1)
        q = q_ref[...]
        after = af_ref[...]

        def block(jb, run, mask):
            z = _sb_scores(q, k_ref, jb * t, t, scale)
            w, run = _sb_weights(z, mask, run, after)
            vs = v_ref[pl.ds(pl.multiple_of(jb * t, t), t), :]
            return run, jnp.dot(w.astype(BF16), vs, preferred_element_type=F32)

        run, acc = block(i, jnp.zeros((t, 1), F32), _sb_diag_mask(t))

        def step(n, carry):
            run, acc = carry
            run, part = block(i - 1 - n, run, None)
            return run, acc + part

        _, acc = lax.fori_loop(0, i, step, (run, acc))
        o_ref[...] = acc.astype(BF16)

    return pl.pallas_call(
        body, grid=(nh, S // t), name="sb_fwd",
        in_specs=[pl.BlockSpec((t, HEAD), lambda h, i: (i, m0 + h)),
                  pl.BlockSpec((S, HEAD), lambda h, i: (0, m0 + nh + h)),
                  pl.BlockSpec((S, HEAD), lambda h, i: (0, m0 + 2 * nh + h)),
                  pl.BlockSpec(after2.shape, lambda h, i: (0, 0))],
        out_specs=pl.BlockSpec((t, HEAD), lambda h, i: (i, h)),
        out_shape=jax.ShapeDtypeStruct((S, nh * HEAD), BF16),
        compiler_params=_params(("arbitrary", "arbitrary")),
    )(projb, projb, projb, after2)


def sb_backward(dcat, projb, nh, m0, t=512):
    S = projb.shape[0]
    scale = 1.0 / math.sqrt(HEAD)
    after2, before2 = _sb_constants()
    n_i = S // t
    nsub = t // SB_SUB

    def body(do_ref, q_ref, k_ref, v_ref, af_ref, bf_ref, dq_ref, dk_ref, dv_ref, dbuf, dk_acc, dv_acc):
        i = pl.program_id(1)

        @pl.when(i == 0)
        def _():
            dk_acc[...] = jnp.zeros_like(dk_acc)
            dv_acc[...] = jnp.zeros_like(dv_acc)

        q = q_ref[...]
        dob = do_ref[...].astype(BF16)
        after = af_ref[...]
        before = bf_ref[...]

        def right_to_left(jb, run, mask):
            ksl = pl.ds(pl.multiple_of(jb * t, t), t)
            z = _sb_scores(q, k_ref, jb * t, t, scale)
            w, run = _sb_weights(z, mask, run, after)
            dw = lax.dot_general(dob, v_ref[ksl, :], (((1,), (1,)), ((), ())), preferred_element_type=F32)
            dbuf[jb] = dw * w
            dv_acc[ksl, :] += lax.dot_general(w.astype(BF16), dob, (((0,), (0,)), ((), ())),
                                              preferred_element_type=F32)
            return run

        run = right_to_left(i, jnp.zeros((t, 1), F32), _sb_diag_mask(t))
        lax.fori_loop(0, i, lambda n, run: right_to_left(i - 1 - n, run, None), run)

        def left_to_right(jb, run, dq, mask):
            ksl = pl.ds(pl.multiple_of(jb * t, t), t)
            z = _sb_scores(q, k_ref, jb * t, t, scale)
            d = dbuf[jb]
            sig = 1.0 / (1.0 + jnp.exp(-z))
            das = []
            for b in range(nsub):
                db = d[:, b * SB_SUB:(b + 1) * SB_SUB]
                prefix = run + jnp.dot(_split2(db), before, preferred_element_type=F32)
                das.append(db - sig[:, b * SB_SUB:(b + 1) * SB_SUB] * (db + prefix))
                run = prefix[:, SB_SUB - 1:SB_SUB] + db[:, SB_SUB - 1:SB_SUB]
            da = jnp.concatenate(das, axis=1)
            if mask is not None:
                da = jnp.where(mask, da, 0.0)
            dab = (da * scale).astype(BF16)
            dq = dq + jnp.dot(dab, k_ref[ksl, :], preferred_element_type=F32)
            dk_acc[ksl, :] += lax.dot_general(dab, q, (((0,), (0,)), ((), ())), preferred_element_type=F32)
            return run, dq

        run, dq = lax.fori_loop(0, i, lambda jb, c: left_to_right(jb, c[0], c[1], None),
                                (jnp.zeros((t, 1), F32), jnp.zeros((t, HEAD), F32)))
        _, dq = left_to_right(i, run, dq, _sb_diag_mask(t))
        dq_ref[...] = dq.astype(BF16)

        @pl.when(i == n_i - 1)
        def _():
            dk_ref[...] = dk_acc[...].astype(BF16)
            dv_ref[...] = dv_acc[...].astype(BF16)

    half = nh * HEAD
    full = pl.BlockSpec((S, HEAD), lambda h, i: (0, h))
    return pl.pallas_call(
        body, grid=(nh, n_i), name="sb_bwd",
        in_specs=[pl.BlockSpec((t, HEAD), lambda h, i: (i, nh + h)),
                  pl.BlockSpec((t, HEAD), lambda h, i: (i, m0 + h)),
                  pl.BlockSpec((S, HEAD), lambda h, i: (0, m0 + nh + h)),
                  pl.BlockSpec((S, HEAD), lambda h, i: (0, m0 + 2 * nh + h)),
                  pl.BlockSpec(after2.shape, lambda h, i: (0, 0)), pl.BlockSpec(before2.shape, lambda h, i: (0, 0))],
        out_specs=[pl.BlockSpec((t, HEAD), lambda h, i: (i, h)), full, full],
        out_shape=[jax.ShapeDtypeStruct((S, half), BF16)] * 3,
        scratch_shapes=[pltpu.VMEM((n_i, t, t), F32), pltpu.VMEM((S, HEAD), F32), pltpu.VMEM((S, HEAD), F32)],
        compiler_params=_params(("arbitrary", "arbitrary")),
    )(dcat, projb, projb, projb, after2, before2)


def local_step(x, target, mix_norm, ffn_norm, final_norm, lb_logits, hg_norm, get_w_in, get_w_rest, send):
    S, D = x.shape
    half = D // 2
    nh = half // HEAD
    tm = 512
    tk = 1024
    row = lambda i, j: (i, 0)

    lb = jax.nn.softmax(lb_logits, axis=0)[0:1]

    h0, r0 = rms_fwd(x, mix_norm[0:1], BF16)
    w_in = get_w_in(h0)
    nbi = w_in.shape[2]
    proj, projb = matmul(
        "proj_in", [h0], [w_in], grid=(N_DEV, S // tm, 1),
        a_spec=pl.BlockSpec((tm, D), lambda j, i, k: (i, 0)),
        b_spec=pl.BlockSpec((None, D, nbi), lambda j, i, k: (j, 0, 0)),
        out_spec=pl.BlockSpec((tm, nbi), lambda j, i, k: (i, j)), out_shape=(S, N_DEV * nbi),
        out_dtypes=[F32, BF16], acc_shape=(8, 128))
    oa, oraw, states = hgrn_forward(proj, lb, hg_norm)
    ob = sb_forward(projb, nh, 4 * nh)
    cat = jnp.concatenate([oa, ob], axis=1)
    w_out, pool_w, pool_scale, wg, wu, wd = get_w_rest(cat)
    (x1,) = matmul(
        "mix_out", [cat], [w_out], grid=(S // tm, 1),
        a_spec=pl.BlockSpec((tm, D), row), b_spec=pl.BlockSpec((D, D), lambda i, k: (0, 0)),
        out_spec=pl.BlockSpec((tm, D), row), out_shape=(S, D), out_dtypes=[F32], acc_shape=(8, 128),
        res=x, res_spec=pl.BlockSpec((tm, D), row))
    h1, r1 = rms_fwd(x1, ffn_norm[0:1], BF16)
    x2, ffn0 = ffn_forward(h1, x1, wg[0], wu[0], wd[0])

    h2, r2 = rms_fwd(x2, mix_norm[1:2], F32)
    x3, pooled = pool_forward(h2, x2, pool_w, pool_scale)
    h3, r3 = rms_fwd(x3, ffn_norm[1:2], BF16)
    x4, ffn1 = ffn_forward(h3, x3, wg[1], wu[1], wd[1])

    loss_blk, dx4, dx4b, d_final = loss_and_final_bwd(x4, final_norm, target)

    dh3, dwg1, dwu1, dwd1 = ffn_backward(dx4b, h3, ffn1, wg[1], wu[1], wd[1])
    sent = send("ffn1", dict(ffn_w_gate_1=dwg1, ffn_w_up_1=dwu1, ffn_w_down_1=dwd1))
    dx3, _, d_ffn1 = rms_bwd(dh3, x3, r3, ffn_norm[1:2] + sent, dx4)
    dmixed, dpooled, d_pscale = pool_backward_mix(dx3, pooled, pool_w, pool_scale)
    G = len(POOL_WINDOWS)
    P = D // G
    (d_pool_w,) = matmul(
        "pool_dw", [pooled], [dmixed], grid=(G, S // tk),
        a_spec=pl.BlockSpec((tk, P), lambda g, k: (k, g)), b_spec=pl.BlockSpec((tk, P), lambda g, k: (k, g)),
        out_spec=pl.BlockSpec((None, P, P), lambda g, k: (g, 0, 0)), out_shape=(G, P, P), out_dtypes=[BF16],
        acc_shape=(P, P), trans_a=True)
    dh2 = pool_backward_window(dpooled)
    dx2, dx2b, d_mix1 = rms_bwd(dh2, x2, r2, mix_norm[1:2], dx3)

    dh1, dwg0, dwu0, dwd0 = ffn_backward(dx2b, h1, ffn0, wg[0], wu[0], wd[0])
    dx1, dx1b, d_ffn0 = rms_bwd(dh1, x1, r1, ffn_norm[0:1], dx2)
    (dcat,) = matmul(
        "mix_out_dx", [dx1b], [w_out], grid=(S // tm, 1),
        a_spec=pl.BlockSpec((tm, D), row), b_spec=pl.BlockSpec((D, D), lambda i, k: (0, 0)),
        out_spec=pl.BlockSpec((tm, D), row), out_shape=(S, D), out_dtypes=[F32], acc_shape=(8, 128),
        trans_b=True)
    (d_w_out,) = matmul(
        "mix_out_dw", [cat], [dx1b], grid=(2, S // tk),
        a_spec=pl.BlockSpec((tk, half), lambda m, k: (k, m)), b_spec=pl.BlockSpec((tk, D), lambda m, k: (k, 0)),
        out_spec=pl.BlockSpec((half, D), lambda m, k: (m, 0)), out_shape=(D, D), out_dtypes=[BF16],
        acc_shape=(half, D), trans_a=True)
    sent = send("layer0", dict(ffn_w_gate_0=dwg0, ffn_w_up_0=dwu0, ffn_w_down_0=dwd0, pool_w=d_pool_w,
                               ab_w_out=d_w_out))
    dqa, dfa, dia, dga, d_lb, d_hg = hgrn_backward(dcat, proj, oraw, states, lb, hg_norm + sent)
    dqb, dkb, dvb = sb_backward(dcat, projb, nh, 4 * nh)
    dproj = jnp.concatenate([dqa, dfa, dia, dga, dqb, dkb, dvb], axis=1)
    (d_w_in,) = matmul(
        "proj_in_dw", [h0], [dproj], grid=(N_DEV, S // tk),
        a_spec=pl.BlockSpec((tk, D), lambda j, k: (k, 0)), b_spec=pl.BlockSpec((tk, nbi), lambda j, k: (k, j)),
        out_spec=pl.BlockSpec((None, D, nbi), lambda j, k: (j, 0, 0)), out_shape=(N_DEV, D, nbi),
        out_dtypes=[BF16], acc_shape=(D, nbi), trans_a=True)
    sent = send("w_in", dict(ab_w_in=d_w_in))
    (dh0,) = matmul(
        "proj_in_dx", [dproj], [w_in], grid=(S // tm, N_DEV),
        a_spec=pl.BlockSpec((tm, nbi), lambda i, j: (i, j)),
        b_spec=pl.BlockSpec((None, D, nbi), lambda i, j: (j, 0, 0)),
        out_spec=pl.BlockSpec((tm, D), row), out_shape=(S, D), out_dtypes=[F32], acc_shape=(tm, D),
        trans_b=True)
    dx0, _, d_mix0 = rms_bwd(dh0, x, r0, mix_norm[0:1] + sent, dx1)

    d_l0 = d_lb * lb * (1.0 - lb)
    small = dict(
        loss=loss_blk[0:1, 0:1],
        mix_norm=jnp.concatenate([d_mix0, d_mix1], axis=0),
        ffn_norm=jnp.concatenate([d_ffn0, d_ffn1], axis=0),
        final_norm=d_final,
        lb_logits=jnp.concatenate([d_l0, -d_l0], axis=0),
        hg_out_norm=jnp.sum(d_hg, axis=0),
        pool_scale=d_pscale,
    )
    return dx0, small


def _my_index():
    return 4 * lax.axis_index("x") + 2 * lax.axis_index("y") + lax.axis_index("c")


def _peer(r):
    x, y, c = lax.axis_index("x"), lax.axis_index("y"), lax.axis_index("c")
    px = 1 - x if (r >> 2) & 1 else x
    py = 1 - y if (r >> 1) & 1 else y
    pc = 1 - c if r & 1 else c
    return (px, py, pc), 4 * px + 2 * py + pc


def exchange(name, arrays, gather):
    n = len(arrays)
    n_peers = N_DEV - 1

    def body(*refs):
        ins, outs = refs[:n], refs[n:2 * n]
        send_sems, recv_sems, local_sems = refs[2 * n:]
        me = _my_index()
        local = []
        for a in range(n):
            src = ins[a] if gather else ins[a].at[me]
            cp = pltpu.make_async_copy(src, outs[a].at[me], local_sems.at[a])
            cp.start()
            local.append(cp)
        remote = []
        for a in range(n):
            for r in range(1, N_DEV):
                peer, pidx = _peer(r)
                src = ins[a] if gather else ins[a].at[pidx]
                cp = pltpu.make_async_remote_copy(
                    src_ref=src, dst_ref=outs[a].at[me], send_sem=send_sems.at[a * n_peers + r - 1],
                    recv_sem=recv_sems.at[a * n_peers + r - 1], device_id=peer, device_id_type=MESH)
                cp.start()
                remote.append((cp, a, r))
        for cp, a, r in remote:
            _, pidx = _peer(r)
            src = ins[a] if gather else ins[a].at[pidx]
            pltpu.make_async_remote_copy(
                src_ref=src, dst_ref=outs[a].at[pidx], send_sem=send_sems.at[a * n_peers + r - 1],
                recv_sem=recv_sems.at[a * n_peers + r - 1], device_id=_peer(r)[0], device_id_type=MESH).wait_recv()
        for cp, a, r in remote:
            cp.wait_send()
        for cp in local:
            cp.wait()

    out_shape = [jax.ShapeDtypeStruct(((N_DEV,) + a.shape) if gather else a.shape, a.dtype) for a in arrays]
    any_spec = pl.BlockSpec(memory_space=pl.ANY)
    return pl.pallas_call(
        body, name=name, in_specs=[any_spec] * n, out_specs=[any_spec] * n, out_shape=out_shape,
        scratch_shapes=[pltpu.SemaphoreType.DMA((n * n_peers,)), pltpu.SemaphoreType.DMA((n * n_peers,)),
                        pltpu.SemaphoreType.DMA((n,))],
    )(*arrays)


_HBM = pl.BlockSpec(memory_space=pltpu.HBM)
_SEM = pl.BlockSpec(memory_space=pltpu.SEMAPHORE)
_EFFECT = pltpu.SideEffectType.DATAFLOW_SIDE_EFFECTING


def _landing(arrays, gather):
    me = _my_index()
    lands = []
    for a in arrays:
        own = a[None] if gather else lax.dynamic_slice_in_dim(a, me, 1, axis=0)
        shape = ((N_DEV,) + a.shape) if gather else a.shape
        lands.append(lax.dynamic_update_slice_in_dim(lax.empty(shape, a.dtype), own, me, axis=0))
    return lands


def exchange_start(name, arrays, gather):
    n = len(arrays)
    n_peers = N_DEV - 1
    lands = _landing(arrays, gather)

    def body(*refs):
        src, land = refs[:n], refs[n:2 * n]
        send_sems, recv_sems = refs[2 * n], refs[2 * n + 1]
        token = refs[-1]
        me = _my_index()
        for a in range(n):
            for r in range(1, N_DEV):
                peer, pidx = _peer(r)
                pltpu.make_async_remote_copy(
                    src_ref=src[a] if gather else src[a].at[pidx], dst_ref=land[a].at[me],
                    send_sem=send_sems.at[a * n_peers + r - 1], recv_sem=recv_sems.at[a * n_peers + r - 1],
                    device_id=peer, device_id_type=MESH).start()
        token[...] = jnp.zeros_like(token)

    thru = [pltpu.HBM(a.shape, a.dtype) for a in arrays] + [pltpu.HBM(l.shape, l.dtype) for l in lands]
    outs = pl.pallas_call(
        body, name=name,
        out_shape=(pltpu.SemaphoreType.DMA((n * n_peers,)), pltpu.SemaphoreType.DMA((n * n_peers,)), *thru,
                   jax.ShapeDtypeStruct((8, 128), F32)),
        in_specs=[_HBM] * (2 * n),
        out_specs=(_SEM, _SEM, *([_HBM] * (2 * n)), pl.BlockSpec(memory_space=pltpu.VMEM)),
        input_output_aliases={i: 2 + i for i in range(2 * n)},
        compiler_params=pltpu.CompilerParams(has_side_effects=_EFFECT),
    )(*[pltpu.with_memory_space_constraint(a, pltpu.HBM) for a in list(arrays) + lands])
    handle = (outs[0], outs[1], list(outs[2:2 + n]), list(outs[2 + n:2 + 2 * n]), gather)
    return handle, outs[-1][0:1, 0:1]


def exchange_wait(name, handle, after):
    send_sems, recv_sems, srcs, lands, gather = handle
    n = len(srcs)
    n_peers = N_DEV - 1

    def body(*refs):
        src, land = refs[:n], refs[n:2 * n]
        send_s, recv_s = refs[2 * n], refs[2 * n + 1]
        for a in range(n):
            for r in range(1, N_DEV):
                peer, pidx = _peer(r)
                cp = pltpu.make_async_remote_copy(
                    src_ref=src[a] if gather else src[a].at[pidx], dst_ref=land[a].at[pidx],
                    send_sem=send_s.at[a * n_peers + r - 1], recv_sem=recv_s.at[a * n_peers + r - 1],
                    device_id=peer, device_id_type=MESH)
                cp.wait_send()
                cp.wait_recv()

    shapes = [pltpu.HBM(a.shape, a.dtype) for a in srcs] + [pltpu.HBM(l.shape, l.dtype) for l in lands]
    outs = pl.pallas_call(
        body, name=name, out_shape=tuple(shapes),
        in_specs=[_HBM] * (2 * n) + [_SEM, _SEM, pl.BlockSpec(memory_space=pl.ANY)],
        out_specs=tuple([_HBM] * (2 * n)),
        input_output_aliases={i: i for i in range(2 * n)},
        compiler_params=pltpu.CompilerParams(has_side_effects=_EFFECT),
    )(*srcs, *lands, send_sems, recv_sems, after)
    return list(outs[n:])


def _row_tile(rows, cap=256):
    best = None
    for t in range(16, min(rows, cap) + 1, 16):
        if rows % t == 0:
            best = t
    return best if best is not None else rows


def sum_slots(name, recv):
    n, R, C = recv.shape
    tr = _row_tile(R)

    def body(r_ref, o_ref):
        g = r_ref[0].astype(F32)
        for d in range(1, n):
            g = g + r_ref[d].astype(F32)
        o_ref[...] = g

    return pl.pallas_call(
        body, grid=(R // tr,), name=name,
        in_specs=[pl.BlockSpec((n, tr, C), lambda i: (0, i, 0))],
        out_specs=pl.BlockSpec((tr, C), lambda i: (i, 0)),
        out_shape=jax.ShapeDtypeStruct((R, C), F32),
        compiler_params=_params(("arbitrary",)),
    )(recv)


def adamw(name, recv, w, m, v):
    n, R, C = recv.shape
    tr = _row_tile(R)

    def body(r_ref, w_ref, m_ref, v_ref, g_ref, d_ref, nm_ref, nv_ref):
        g = r_ref[0].astype(F32)
        for d in range(1, n):
            g = g + r_ref[d].astype(F32)
        mm = ADAM_B1 * m_ref[...] + (1.0 - ADAM_B1) * g
        vv = ADAM_B2 * v_ref[...] + (1.0 - ADAM_B2) * (g * g)
        m_hat = mm / (1.0 - ADAM_B1 ** ADAM_STEP)
        v_hat = vv / (1.0 - ADAM_B2 ** ADAM_STEP)
        g_ref[...] = g
        d_ref[...] = -ADAM_LR * (m_hat / (jnp.sqrt(v_hat) + ADAM_EPS) + ADAM_WD * w_ref[...])
        nm_ref[...] = mm
        nv_ref[...] = vv

    row = pl.BlockSpec((tr, C), lambda i: (i, 0))
    return pl.pallas_call(
        body, grid=(R // tr,), name=name,
        in_specs=[pl.BlockSpec((n, tr, C), lambda i: (0, i, 0)), row, row, row],
        out_specs=[row] * 4,
        out_shape=[jax.ShapeDtypeStruct((R, C), F32)] * 4,
        compiler_params=_params(("arbitrary",)),
    )(recv, w, m, v)


def _adamw_nd(name, recv, w, m, v):
    shp = w.shape
    C = shp[-1]
    flat = lambda a: a.reshape(-1, C)
    outs = adamw(name, recv.reshape(recv.shape[0], -1, C), flat(w), flat(m), flat(v))
    return [o.reshape(shp) for o in outs]


_SMALL_NAMES = ("loss", "mix_norm", "ffn_norm", "final_norm", "lb_logits", "hg_out_norm", "pool_scale")
_LANES = 128


def _pack_small(parts):
    rows, layout = [], {}
    at = 0
    for name in parts:
        flat = parts[name].reshape(-1).astype(F32)
        n_rows = -(-flat.shape[0] // (8 * _LANES)) * 8
        flat = jnp.pad(flat, (0, n_rows * _LANES - flat.shape[0]))
        rows.append(flat.reshape(n_rows, _LANES))
        layout[name] = (at, parts[name].shape)
        at += n_rows
    return jnp.concatenate(rows, axis=0), layout


def _unpack_small(pack, layout):
    out = {}
    for name, (at, shape) in layout.items():
        size = int(np.prod(shape))
        n_rows = -(-size // _LANES)
        out[name] = pack[at:at + n_rows].reshape(-1)[:size].reshape(shape)
    return out


def kernel(x, mix_norm, ffn_norm, final_norm, ab_w_in, lb_logits, hg_out_norm, ab_w_out, pool_w, pool_scale, ffn_w_gate, ffn_w_up, ffn_w_down, loss_target, m_mix_norm, m_ffn_norm, m_final_norm, m_ab_w_in, m_lb_logits, m_hg_out_norm, m_ab_w_out, m_pool_w, m_pool_scale, m_ffn_w_gate, m_ffn_w_up, m_ffn_w_down, v_mix_norm, v_ffn_norm, v_final_norm, v_ab_w_in, v_lb_logits, v_hg_out_norm, v_ab_w_out, v_pool_w, v_pool_scale, v_ffn_w_gate, v_ffn_w_up, v_ffn_w_down):
    D = x.shape[-1]
    n_layers = ffn_w_gate.shape[0]
    G = pool_w.shape[1]
    P = pool_w.shape[3]
    me = _my_index()

    in_handle, started_in = exchange_start("gather_w_in_start", [ab_w_in[0].astype(BF16)], gather=True)
    rest = [ab_w_out[0], pool_w[0]]
    for l in range(n_layers):
        rest += [ffn_w_gate[l], ffn_w_up[l], ffn_w_down[l]]
    rest = [s.astype(BF16) for s in rest] + [pool_scale]
    rest_handle = []

    def get_w_in(after):
        w_in = exchange_wait("gather_w_in_wait", in_handle, after)[0]
        w_in, srcs = lax.optimization_barrier((w_in, rest))
        handle, started = exchange_start("gather_rest_start", srcs, gather=True)
        rest_handle.append(handle)
        return lax.optimization_barrier((w_in, started))[0]

    def get_w_rest(after):
        got = exchange_wait("gather_rest_wait", rest_handle[0], after)
        w_out_g = got[0].reshape(D, D)
        pool_g = got[1].transpose(1, 0, 2, 3).reshape(G, P, P)
        wg = [got[2 + 3 * l] for l in range(n_layers)]
        wu = [got[3 + 3 * l] for l in range(n_layers)]
        wd = [got[4 + 3 * l] for l in range(n_layers)]
        return w_out_g, pool_g, got[-1].reshape(1, D), wg, wu, wd

    in_flight = []

    def send(tag, grads):
        if "pool_w" in grads:
            grads = dict(grads, pool_w=grads["pool_w"].reshape(G, N_DEV, P // N_DEV, P).transpose(1, 0, 2, 3))
        if "ab_w_out" in grads:
            grads = dict(grads, ab_w_out=grads["ab_w_out"].reshape(N_DEV, D // N_DEV, D))
        handle, started = exchange_start("grads_" + tag + "_start", list(grads.values()), gather=False)
        in_flight.append((tag, list(grads.keys()), handle))
        return started

    dx0, small = local_step(x[0], loss_target[0], mix_norm + started_in, ffn_norm, final_norm[None],
                            lb_logits, hg_out_norm, get_w_in, get_w_rest, send)

    recv = {}
    for tag, names, handle in in_flight:
        recv.update(zip(names, exchange_wait("grads_" + tag + "_wait", handle, dx0)))
    small_pack, layout = _pack_small({k: small[k] for k in _SMALL_NAMES})
    (small_all,) = exchange("gather_small", [small_pack], gather=True)
    tot = _unpack_small(sum_slots("sum_small", small_all), layout)

    res = {}
    res["ab_w_in"] = _adamw_nd("adamw_w_in", recv["ab_w_in"], ab_w_in, m_ab_w_in, v_ab_w_in)
    res["ab_w_out"] = _adamw_nd("adamw_w_out", recv["ab_w_out"], ab_w_out, m_ab_w_out, v_ab_w_out)
    res["pool_w"] = _adamw_nd("adamw_pool_w", recv["pool_w"], pool_w, m_pool_w, v_pool_w)
    ffn_in = {"ffn_w_gate": (ffn_w_gate, m_ffn_w_gate, v_ffn_w_gate),
              "ffn_w_up": (ffn_w_up, m_ffn_w_up, v_ffn_w_up),
              "ffn_w_down": (ffn_w_down, m_ffn_w_down, v_ffn_w_down)}
    for name, (w, m, v) in ffn_in.items():
        per_layer = [_adamw_nd("adamw_" + name, recv[name + "_" + str(l)], w[l], m[l], v[l])
                     for l in range(n_layers)]
        res[name] = [jnp.stack([per_layer[l][o] for l in range(n_layers)]) for o in range(4)]

    n_ps = pool_scale.shape[1]
    small_g = dict(tot)
    small_g["pool_scale"] = lax.dynamic_slice(tot["pool_scale"], (0, me * n_ps), (1, n_ps))
    small_w = dict(mix_norm=(mix_norm, m_mix_norm, v_mix_norm), ffn_norm=(ffn_norm, m_ffn_norm, v_ffn_norm),
                   final_norm=(final_norm, m_final_norm, v_final_norm),
                   lb_logits=(lb_logits, m_lb_logits, v_lb_logits),
                   hg_out_norm=(hg_out_norm, m_hg_out_norm, v_hg_out_norm),
                   pool_scale=(pool_scale, m_pool_scale, v_pool_scale))
    g_pack, lay2 = _pack_small({k: small_g[k].reshape(small_w[k][0].shape) for k in small_w})
    w_pack, _ = _pack_small({k: small_w[k][0] for k in small_w})
    m_pack, _ = _pack_small({k: small_w[k][1] for k in small_w})
    v_pack, _ = _pack_small({k: small_w[k][2] for k in small_w})
    small_out = [_unpack_small(o, lay2) for o in adamw("adamw_small", g_pack[None], w_pack, m_pack, v_pack)]
    for k in small_w:
        res[k] = [small_out[o][k] for o in range(4)]

    order = ("mix_norm", "ffn_norm", "final_norm", "ab_w_in", "lb_logits", "hg_out_norm", "ab_w_out", "pool_w",
             "pool_scale", "ffn_w_gate", "ffn_w_up", "ffn_w_down")
    outs = [tot["loss"].reshape(()), dx0[None]]
    for o in range(4):
        outs += [res[k][o] for k in order]
    return tuple(outs)
```

```python
import functools
import math

import numpy as np
import jax
import jax.numpy as jnp
from jax import lax
from jax.experimental import pallas as pl
from jax.experimental.pallas import tpu as pltpu

F32 = jnp.float32
BF16 = jnp.bfloat16

N_DEV = 8
RMS_EPS = 1e-6
HEAD = 128
HG_CHUNK = 64
HG_HEADS_PER_BLOCK = 2
POOL_WINDOWS = (2, 4, 8, 16)
POOL_HALO = 16
ADAM_LR, ADAM_B1, ADAM_B2, ADAM_EPS, ADAM_WD, ADAM_STEP = 0.001, 0.9, 0.999, 1e-08, 0.01, 10
VMEM_LIMIT_BYTES = 60 * 1024 * 1024
MESH = pl.DeviceIdType.MESH


def _params(sem):
    return pltpu.CompilerParams(dimension_semantics=sem, vmem_limit_bytes=VMEM_LIMIT_BYTES)


def _sigmoid(x):
    return 1.0 / (1.0 + jnp.exp(-x))


def rms_fwd(x, gain, out_dtype, ts=512):
    S, D = x.shape

    def body(x_ref, g_ref, h_ref, r_ref):
        xv = x_ref[...]
        r = lax.rsqrt(jnp.mean(xv * xv, axis=-1, keepdims=True) + RMS_EPS)
        h_ref[...] = ((xv * r) * g_ref[...]).astype(h_ref.dtype)
        r_ref[...] = r

    return pl.pallas_call(
        body, grid=(S // ts,), name="rms_fwd",
        in_specs=[pl.BlockSpec((ts, D), lambda i: (i, 0)), pl.BlockSpec((1, D), lambda i: (0, 0))],
        out_specs=[pl.BlockSpec((ts, D), lambda i: (i, 0)), pl.BlockSpec((ts, 1), lambda i: (i, 0))],
        out_shape=[jax.ShapeDtypeStruct((S, D), out_dtype), jax.ShapeDtypeStruct((S, 1), F32)],
        compiler_params=_params(("arbitrary",)),
    )(x, gain)


def rms_bwd(dh, x, r, gain, dres, ts=512):
    S, D = x.shape

    def body(dh_ref, x_ref, r_ref, g_ref, dres_ref, dx_ref, dxb_ref, dg_ref):
        i = pl.program_id(0)
        rr = r_ref[...]
        xh = x_ref[...] * rr
        dhv = dh_ref[...]
        dxh = dhv * g_ref[...]
        dx = dres_ref[...] + rr * (dxh - xh * jnp.mean(dxh * xh, axis=-1, keepdims=True))
        dx_ref[...] = dx
        dxb_ref[...] = dx.astype(BF16)
        part = jnp.sum(dhv * xh, axis=0, keepdims=True)

        @pl.when(i == 0)
        def _():
            dg_ref[...] = part

        @pl.when(i > 0)
        def _():
            dg_ref[...] += part

    row = pl.BlockSpec((ts, D), lambda i: (i, 0))
    vec = pl.BlockSpec((1, D), lambda i: (0, 0))
    return pl.pallas_call(
        body, grid=(S // ts,), name="rms_bwd",
        in_specs=[row, row, pl.BlockSpec((ts, 1), lambda i: (i, 0)), vec, row],
        out_specs=[row, row, vec],
        out_shape=[jax.ShapeDtypeStruct((S, D), F32), jax.ShapeDtypeStruct((S, D), BF16),
                   jax.ShapeDtypeStruct((1, D), F32)],
        compiler_params=_params(("arbitrary",)),
    )(dh, x, r, gain, dres)


def loss_and_final_bwd(x, gain, target, ts=512):
    S, D = x.shape

    def body(x_ref, g_ref, t_ref, loss_ref, dx_ref, dxb_ref, dg_ref):
        i = pl.program_id(0)
        xv = x_ref[...]
        rr = lax.rsqrt(jnp.mean(xv * xv, axis=-1, keepdims=True) + RMS_EPS)
        xh = xv * rr
        err = xh * g_ref[...] - t_ref[...]
        part_loss = 0.5 * jnp.sum(jnp.mean(err * err, axis=-1, keepdims=True))
        dy = err / D
        dxh = dy * g_ref[...]
        dx = rr * (dxh - xh * jnp.mean(dxh * xh, axis=-1, keepdims=True))
        dx_ref[...] = dx
        dxb_ref[...] = dx.astype(BF16)
        part = jnp.sum(dy * xh, axis=0, keepdims=True)

        @pl.when(i == 0)
        def _():
            dg_ref[...] = part
            loss_ref[...] = jnp.zeros_like(loss_ref) + part_loss

        @pl.when(i > 0)
        def _():
            dg_ref[...] += part
            loss_ref[...] += part_loss

    row = pl.BlockSpec((ts, D), lambda i: (i, 0))
    vec = pl.BlockSpec((1, D), lambda i: (0, 0))
    return pl.pallas_call(
        body, grid=(S // ts,), name="loss_final",
        in_specs=[row, vec, row],
        out_specs=[pl.BlockSpec((8, 128), lambda i: (0, 0)), row, row, vec],
        out_shape=[jax.ShapeDtypeStruct((8, 128), F32), jax.ShapeDtypeStruct((S, D), F32),
                   jax.ShapeDtypeStruct((S, D), BF16), jax.ShapeDtypeStruct((1, D), F32)],
        compiler_params=_params(("arbitrary",)),
    )(x, gain, target)


def matmul(name, a_ops, b_ops, *, grid, a_spec, b_spec, out_spec, out_shape, out_dtypes, acc_shape,
           trans_a=False, trans_b=False, res=None, res_spec=None):
    n_pairs = len(a_ops)
    n_out = len(out_dtypes)
    nk = grid[-1]
    kaxis = len(grid) - 1
    dn = (((0,) if trans_a else (1,), (1,) if trans_b else (0,)), ((), ()))

    def body(*refs):
        a_refs = refs[:n_pairs]
        b_refs = refs[n_pairs:2 * n_pairs]
        pos = 2 * n_pairs
        res_ref = None
        if res is not None:
            res_ref = refs[pos]
            pos += 1
        out_refs = refs[pos:pos + n_out]
        acc_ref = refs[pos + n_out]
        k = pl.program_id(kaxis)
        in_place = n_out == 1 and out_dtypes[0] == F32
        target = out_refs[0] if in_place else acc_ref

        def finish(val):
            if res_ref is not None:
                val = val + res_ref[...]
            for o in out_refs:
                o[...] = val.astype(o.dtype)

        if nk > 1:
            @pl.when(k == 0)
            def _():
                if in_place and res_ref is not None:
                    target[...] = res_ref[...]
                else:
                    target[...] = jnp.zeros_like(target)

        part = None
        for ar, br in zip(a_refs, b_refs):
            d = lax.dot_general(ar[...].astype(BF16), br[...].astype(BF16), dn, preferred_element_type=F32)
            part = d if part is None else part + d

        if nk == 1:
            finish(part)
        else:
            target[...] += part
            if not in_place:
                @pl.when(k == nk - 1)
                def _():
                    finish(acc_ref[...])

    in_specs = [a_spec] * n_pairs + [b_spec] * n_pairs
    operands = list(a_ops) + list(b_ops)
    if res is not None:
        in_specs.append(res_spec)
        operands.append(res)
    return pl.pallas_call(
        body, grid=grid, name=name, in_specs=in_specs,
        out_specs=[out_spec] * n_out,
        out_shape=[jax.ShapeDtypeStruct(out_shape, dt) for dt in out_dtypes],
        scratch_shapes=[pltpu.VMEM(acc_shape, F32)],
        compiler_params=_params(("arbitrary",) * len(grid)),
    )(*operands)


def ffn_gate_up(h, wg, wu, tm=512):
    S, D = h.shape
    nb = wg.shape[2]

    def body(h_ref, wg_ref, wu_ref, g_ref, u_ref, a_ref):
        for c in range(2):
            rows = slice(c * (tm // 2), (c + 1) * (tm // 2))
            hv = h_ref[rows, :]
            g = jnp.dot(hv, wg_ref[...], preferred_element_type=F32)
            u = jnp.dot(hv, wu_ref[...], preferred_element_type=F32)
            g_ref[rows, :] = g
            u_ref[rows, :] = u
            a_ref[rows, :] = (g * _sigmoid(g) * u).astype(BF16)

    wspec = pl.BlockSpec((None, D, nb), lambda j, i: (j, 0, 0))
    ospec = pl.BlockSpec((None, tm, nb), lambda j, i: (j, i, 0))
    return pl.pallas_call(
        body, grid=(N_DEV, S // tm), name="ffn_gate_up",
        in_specs=[pl.BlockSpec((tm, D), lambda j, i: (i, 0)), wspec, wspec],
        out_specs=[ospec, ospec, ospec],
        out_shape=[jax.ShapeDtypeStruct((N_DEV, S, nb), F32), jax.ShapeDtypeStruct((N_DEV, S, nb), F32),
                   jax.ShapeDtypeStruct((N_DEV, S, nb), BF16)],
        compiler_params=_params(("arbitrary", "arbitrary")),
    )(h, wg, wu)


def ffn_bwd_hidden(dy, wd, g, u, tm=512):
    S, D = dy.shape
    nb = wd.shape[1]

    def body(dy_ref, wd_ref, g_ref, u_ref, dg_ref, du_ref):
        for c in range(2):
            rows = slice(c * (tm // 2), (c + 1) * (tm // 2))
            da = lax.dot_general(dy_ref[rows, :], wd_ref[...], (((1,), (1,)), ((), ())),
                                 preferred_element_type=F32)
            gv = g_ref[rows, :]
            s = _sigmoid(gv)
            du_ref[rows, :] = (da * (gv * s)).astype(BF16)
            dg_ref[rows, :] = (da * u_ref[rows, :] * (s * (1.0 + gv * (1.0 - s)))).astype(BF16)

    hspec = pl.BlockSpec((None, tm, nb), lambda j, i: (j, i, 0))
    return pl.pallas_call(
        body, grid=(N_DEV, S // tm), name="ffn_bwd_hidden",
        in_specs=[pl.BlockSpec((tm, D), lambda j, i: (i, 0)), pl.BlockSpec((None, nb, D), lambda j, i: (j, 0, 0)),
                  hspec, hspec],
        out_specs=[hspec, hspec],
        out_shape=[jax.ShapeDtypeStruct((N_DEV, S, nb), BF16), jax.ShapeDtypeStruct((N_DEV, S, nb), BF16)],
        compiler_params=_params(("arbitrary", "arbitrary")),
    )(dy, wd, g, u)


def ffn_forward(h, xres, wg, wu, wd, tm=512):
    S, D = h.shape
    nb = wg.shape[2]
    g, u, a = ffn_gate_up(h, wg, wu)
    (xo,) = matmul(
        "ffn_down", [a], [wd], grid=(S // tm, N_DEV),
        a_spec=pl.BlockSpec((None, tm, nb), lambda i, j: (j, i, 0)),
        b_spec=pl.BlockSpec((None, nb, D), lambda i, j: (j, 0, 0)),
        out_spec=pl.BlockSpec((tm, D), lambda i, j: (i, 0)), out_shape=(S, D), out_dtypes=[F32],
        acc_shape=(tm, D), res=xres, res_spec=pl.BlockSpec((tm, D), lambda i, j: (i, 0)))
    return xo, (g, u, a)


def ffn_backward(dy_b, h, saved, wg, wu, wd, tm=512, tk=1024):
    S, D = h.shape
    nb = wg.shape[2]
    g, u, a = saved
    dg, du = ffn_bwd_hidden(dy_b, wd, g, u)
    (dh,) = matmul(
        "ffn_dh", [dg, du], [wg, wu], grid=(S // tm, N_DEV),
        a_spec=pl.BlockSpec((None, tm, nb), lambda i, j: (j, i, 0)),
        b_spec=pl.BlockSpec((None, D, nb), lambda i, j: (j, 0, 0)),
        out_spec=pl.BlockSpec((tm, D), lambda i, j: (i, 0)), out_shape=(S, D), out_dtypes=[F32],
        acc_shape=(tm, D), trans_b=True)

    def wgrad_in(name, dhid):
        (dw,) = matmul(
            name, [h], [dhid], grid=(N_DEV, S // tk),
            a_spec=pl.BlockSpec((tk, D), lambda j, k: (k, 0)),
            b_spec=pl.BlockSpec((None, tk, nb), lambda j, k: (j, k, 0)),
            out_spec=pl.BlockSpec((None, D, nb), lambda j, k: (j, 0, 0)), out_shape=(N_DEV, D, nb),
            out_dtypes=[BF16], acc_shape=(D, nb), trans_a=True)
        return dw

    dwg = wgrad_in("ffn_dwg", dg)
    dwu = wgrad_in("ffn_dwu", du)
    (dwd,) = matmul(
        "ffn_dwd", [a], [dy_b], grid=(N_DEV, S // tk),
        a_spec=pl.BlockSpec((None, tk, nb), lambda j, k: (j, k, 0)),
        b_spec=pl.BlockSpec((tk, D), lambda j, k: (k, 0)),
        out_spec=pl.BlockSpec((None, nb, D), lambda j, k: (j, 0, 0)), out_shape=(N_DEV, nb, D),
        out_dtypes=[BF16], acc_shape=(nb, D), trans_a=True)
    return dh, dwg, dwu, dwd


def _pool_counts(row0, n, w):
    pos = row0 + lax.broadcasted_iota(jnp.int32, (n, 1), 0)
    return jnp.minimum(pos + 1, w).astype(F32)


def pool_forward(h, xres, w, scale, ts=256):
    S, D = h.shape
    G = len(POOL_WINDOWS)
    P = D // G
    hb = ts // POOL_HALO

    def body(h_ref, halo_ref, x_ref, w_ref, s_ref, xo_ref, p_ref):
        i = pl.program_id(0)
        for gi, win in enumerate(POOL_WINDOWS):
            cols = slice(gi * P, (gi + 1) * P)
            cur = h_ref[:, cols]
            halo = jnp.where(i > 0, halo_ref[:, cols], 0.0)
            acc = jnp.concatenate([halo, cur], axis=0)
            step = 1
            while step < win:
                acc = acc + pltpu.roll(acc, step, 0)
                step *= 2
            wsum = acc[POOL_HALO:, :]
            pooled = wsum / _pool_counts(i * ts, ts, win) - cur
            pb = pooled.astype(BF16)
            p_ref[:, cols] = pb
            mixed = jnp.dot(pb, w_ref[gi], preferred_element_type=F32)
            xo_ref[:, cols] = x_ref[:, cols] + mixed * s_ref[:, cols]

    row = pl.BlockSpec((ts, D), lambda i: (i, 0))
    return pl.pallas_call(
        body, grid=(S // ts,), name="pool_fwd",
        in_specs=[row, pl.BlockSpec((POOL_HALO, D), lambda i: (jnp.maximum(i * hb - 1, 0), 0)), row,
                  pl.BlockSpec((G, P, P), lambda i: (0, 0, 0)), pl.BlockSpec((1, D), lambda i: (0, 0))],
        out_specs=[row, row],
        out_shape=[jax.ShapeDtypeStruct((S, D), F32), jax.ShapeDtypeStruct((S, D), BF16)],
        compiler_params=_params(("arbitrary",)),
    )(h, h, xres, w, scale)


def pool_backward_mix(dx, pooled, w, scale, ts=256):
    S, D = dx.shape
    G = len(POOL_WINDOWS)
    P = D // G

    def body(dx_ref, p_ref, w_ref, s_ref, dm_ref, dp_ref, ds_ref):
        i = pl.program_id(0)
        parts = []
        for gi in range(G):
            cols = slice(gi * P, (gi + 1) * P)
            dxv = dx_ref[:, cols]
            dmb = (dxv * s_ref[:, cols]).astype(BF16)
            dm_ref[:, cols] = dmb
            dp_ref[:, cols] = lax.dot_general(dmb, w_ref[gi], (((1,), (1,)), ((), ())),
                                              preferred_element_type=F32)
            mixed = jnp.dot(p_ref[:, cols], w_ref[gi], preferred_element_type=F32)
            parts.append(jnp.sum(dxv * mixed, axis=0, keepdims=True))
        part = jnp.concatenate(parts, axis=1)

        @pl.when(i == 0)
        def _():
            ds_ref[...] = part

        @pl.when(i > 0)
        def _():
            ds_ref[...] += part

    row = pl.BlockSpec((ts, D), lambda i: (i, 0))
    vec = pl.BlockSpec((1, D), lambda i: (0, 0))
    return pl.pallas_call(
        body, grid=(S // ts,), name="pool_bwd_mix",
        in_specs=[row, row, pl.BlockSpec((G, P, P), lambda i: (0, 0, 0)), vec],
        out_specs=[row, row, vec],
        out_shape=[jax.ShapeDtypeStruct((S, D), BF16), jax.ShapeDtypeStruct((S, D), F32),
                   jax.ShapeDtypeStruct((1, D), F32)],
        compiler_params=_params(("arbitrary",)),
    )(dx, pooled, w, scale)


def pool_backward_window(dp, ts=256):
    S, D = dp.shape
    G = len(POOL_WINDOWS)
    P = D // G
    hb = ts // POOL_HALO
    n_i = S // ts
    n_rows = ts + POOL_HALO

    def body(dp_ref, halo_ref, dh_ref):
        i = pl.program_id(0)
        for gi, win in enumerate(POOL_WINDOWS):
            cols = slice(gi * P, (gi + 1) * P)
            cur = dp_ref[:, cols]
            halo = jnp.where(i < n_i - 1, halo_ref[:, cols], 0.0)
            acc = jnp.concatenate([cur / _pool_counts(i * ts, ts, win),
                                   halo / _pool_counts((i + 1) * ts, POOL_HALO, win)], axis=0)
            step = 1
            while step < win:
                acc = acc + pltpu.roll(acc, n_rows - step, 0)
                step *= 2
            dh_ref[:, cols] = acc[:ts, :] - cur

    row = pl.BlockSpec((ts, D), lambda i: (i, 0))
    return pl.pallas_call(
        body, grid=(n_i,), name="pool_bwd_window",
        in_specs=[row, pl.BlockSpec((POOL_HALO, D), lambda i: (jnp.minimum((i + 1) * hb, S // POOL_HALO - 1), 0))],
        out_specs=row,
        out_shape=jax.ShapeDtypeStruct((S, D), F32),
        compiler_params=_params(("arbitrary",)),
    )(dp, dp)


_HG_LEVELS = (32, 16, 8, 4, 2, 1)
_N_LEV = len(_HG_LEVELS) + 1


def _hgrn_constants():
    C = HG_CHUNK
    t = np.arange(C)
    tri = (t[None, :] <= t[:, None]).astype(np.float32)
    blocks = [tri]
    masks, upq, upk = [], [], []
    for m in _HG_LEVELS:
        p = (t // (2 * m)) * 2 * m + m - 1
        blocks.append(tri[p])
        masks.append(((t[:, None] // (2 * m)) == (t[None, :] // (2 * m))).astype(np.float32))
        upper = (t % (2 * m)) >= m
        upq.append(np.repeat(upper[:, None], HEAD, 1).astype(np.float32))
        upk.append(np.repeat(~upper[:, None], HEAD, 1).astype(np.float32))
    blocks.append(tri)
    masks.append(np.eye(C, dtype=np.float32))
    upq.append(np.ones((C, HEAD), np.float32))
    upk.append(np.ones((C, HEAD), np.float32))
    mstack = np.concatenate(blocks, axis=0)
    mstack3 = np.concatenate([mstack] * 3, axis=1)
    trirev3 = np.concatenate([tri.T] * 3, axis=1)
    return (jnp.asarray(mstack3, BF16), jnp.asarray(np.stack(masks)), jnp.asarray(np.stack(upq)),
            jnp.asarray(np.stack(upk)), jnp.asarray(trirev3, BF16))


def _split3(x):
    hi = x.astype(BF16)
    r1 = x - hi.astype(F32)
    mid = r1.astype(BF16)
    lo = (r1 - mid.astype(F32)).astype(BF16)
    return jnp.concatenate([hi, mid, lo], axis=0)


def _hgrn_chunk_common(qa, fa, lbv, mstack3, upq, upk):
    sq = _sigmoid(qa)
    q = qa * sq
    sf = _sigmoid(fa)
    f = lbv + (1.0 - lbv) * sf
    g = jnp.log(f)
    k = 1.0 - f
    gall = jnp.dot(mstack3, _split3(g), preferred_element_type=F32).reshape(_N_LEV + 1, HG_CHUNK, HEAD)
    G = gall[0]
    eq_exp = G[None] - gall[1:]
    eq = jnp.exp(jnp.minimum(eq_exp, 0.0)) * upq
    ek = jnp.exp(jnp.minimum(-eq_exp, 0.0)) * upk
    Qs = (q[None] * eq).astype(BF16)
    Ks = (k[None] * ek).astype(BF16)
    return sq, q, sf, f, k, G, eq, ek, Qs, Ks


def hgrn_forward(proj, lb, hg_norm, ts=512):
    S = proj.shape[0]
    nh = lb.shape[1] // HEAD
    C = HG_CHUNK
    ncs = ts // C
    mstack3, masks, upq, upk, _ = _hgrn_constants()

    def body(qa_ref, fa_ref, ia_ref, ga_ref, lb_ref, gn_ref, ms_ref, mk_ref, uq_ref, uk_ref,
             oa_ref, oraw_ref, st_ref, state):
        tt = pl.program_id(1)

        @pl.when(tt == 0)
        def _():
            state[...] = jnp.zeros_like(state)

        gn = gn_ref[...]

        def chunk(c, carry):
            sl = pl.ds(pl.multiple_of(c * C, C), C)
            for hh in range(HG_HEADS_PER_BLOCK):
                cols = slice(hh * HEAD, (hh + 1) * HEAD)
                qa, fa, v, ga = qa_ref[sl, cols], fa_ref[sl, cols], ia_ref[sl, cols], ga_ref[sl, cols]
                _, q, _, _, k, G, _, _, Qs, Ks = _hgrn_chunk_common(qa, fa, lb_ref[:, cols], ms_ref[...],
                                                                    uq_ref[...], uk_ref[...])
                att7 = lax.dot_general(Qs, Ks, (((2,), (2,)), ((0,), (0,))), preferred_element_type=F32)
                att = jnp.sum(att7 * mk_ref[...], axis=0)
                st = state[hh]
                st_ref[hh, c] = st
                vb = v.astype(BF16)
                qg = (q * jnp.exp(G)).astype(BF16)
                o = jnp.dot(att.astype(BF16), vb, preferred_element_type=F32)
                o = o + lax.dot_general(qg, st.astype(BF16), (((1,), (1,)), ((), ())),
                                        preferred_element_type=F32)
                g_last = G[C - 1:C, :]
                kh = (k * jnp.exp(g_last - G)).astype(BF16)
                state[hh] = st * jnp.exp(g_last) + lax.dot_general(vb, kh, (((0,), (0,)), ((), ())),
                                                                   preferred_element_type=F32)
                oraw_ref[sl, cols] = o
                r = lax.rsqrt(jnp.mean(o * o, axis=-1, keepdims=True) + RMS_EPS)
                oa_ref[sl, cols] = (((o * r) * gn) * (ga * _sigmoid(ga))).astype(BF16)
            return carry

        lax.fori_loop(0, ncs, chunk, 0)

    hpb = HG_HEADS_PER_BLOCK
    wide = hpb * HEAD

    def col(m0):
        return pl.BlockSpec((ts, wide), lambda h, t: (t, m0 // hpb + h))

    const3 = lambda shape: pl.BlockSpec(shape, lambda h, t: (0, 0, 0))
    return pl.pallas_call(
        body, grid=(nh // hpb, S // ts), name="hgrn_fwd",
        in_specs=[col(0), col(nh), col(2 * nh), col(3 * nh),
                  pl.BlockSpec((1, wide), lambda h, t: (0, h)), pl.BlockSpec((1, HEAD), lambda h, t: (0, 0)),
                  pl.BlockSpec(mstack3.shape, lambda h, t: (0, 0)), const3(masks.shape), const3(upq.shape),
                  const3(upk.shape)],
        out_specs=[pl.BlockSpec((ts, wide), lambda h, t: (t, h)), pl.BlockSpec((ts, wide), lambda h, t: (t, h)),
                   pl.BlockSpec((hpb, ncs, HEAD, HEAD), lambda h, t: (h, t, 0, 0))],
        out_shape=[jax.ShapeDtypeStruct((S, nh * HEAD), BF16), jax.ShapeDtypeStruct((S, nh * HEAD), F32),
                   jax.ShapeDtypeStruct((nh, S // C, HEAD, HEAD), F32)],
        scratch_shapes=[pltpu.VMEM((hpb, HEAD, HEAD), F32)],
        compiler_params=_params(("arbitrary", "arbitrary")),
    )(proj, proj, proj, proj, lb, hg_norm, mstack3, masks, upq, upk)


def hgrn_backward(dcat, proj, oraw, states, lb, hg_norm, ts=512):
    S = proj.shape[0]
    nh = lb.shape[1] // HEAD
    C = HG_CHUNK
    ncs = ts // C
    nt = S // ts
    mstack3, masks, upq, upk, trirev3 = _hgrn_constants()

    def body(do_ref, qa_ref, fa_ref, ia_ref, ga_ref, or_ref, st_ref, lb_ref, gn_ref, ms_ref, mk_ref, uq_ref,
             uk_ref, tr_ref, dqa_ref, dfa_ref, dia_ref, dga_ref, dlb_ref, dgn_ref, dstate):
        tt = pl.program_id(1)

        @pl.when(tt == 0)
        def _():
            dstate[...] = jnp.zeros_like(dstate)
            dlb_ref[...] = jnp.zeros_like(dlb_ref)
            dgn_ref[...] = jnp.zeros_like(dgn_ref)

        gn = gn_ref[...]

        def chunk(cc, carry):
            c = ncs - 1 - cc
            sl = pl.ds(pl.multiple_of(c * C, C), C)
            for hh in range(HG_HEADS_PER_BLOCK):
                cols = slice(hh * HEAD, (hh + 1) * HEAD)
                lbv = lb_ref[:, cols]
                qa, fa, v, ga = qa_ref[sl, cols], fa_ref[sl, cols], ia_ref[sl, cols], ga_ref[sl, cols]
                sq, q, sf, f, k, G, eq, ek, Qs, Ks = _hgrn_chunk_common(qa, fa, lbv, ms_ref[...], uq_ref[...],
                                                                        uk_ref[...])
                mk = mk_ref[...]
                att7 = lax.dot_general(Qs, Ks, (((2,), (2,)), ((0,), (0,))), preferred_element_type=F32)
                att = jnp.sum(att7 * mk, axis=0)
                o = or_ref[sl, cols]
                dO = do_ref[sl, cols]
                sg = _sigmoid(ga)
                r = lax.rsqrt(jnp.mean(o * o, axis=-1, keepdims=True) + RMS_EPS)
                xh = o * r
                dga_ref[sl, cols] = (dO * (xh * gn) * (sg * (1.0 + ga * (1.0 - sg)))).astype(BF16)
                don = dO * (ga * sg)
                dgn_ref[hh] += jnp.sum(don * xh, axis=0, keepdims=True)
                dxh = don * gn
                do = r * (dxh - xh * jnp.mean(dxh * xh, axis=-1, keepdims=True))
                dob = do.astype(BF16)
                st = st_ref[hh, c]
                dst = dstate[hh]
                dstb = dst.astype(BF16)
                vb = v.astype(BF16)
                eG = jnp.exp(G)
                g_last = G[C - 1:C, :]
                e_last = jnp.exp(g_last)
                e_tail = jnp.exp(g_last - G)
                qg = (q * eG).astype(BF16)
                kh = (k * e_tail).astype(BF16)
                dq_inter = jnp.dot(dob, st.astype(BF16), preferred_element_type=F32) * eG
                dk_inter = jnp.dot(vb, dstb, preferred_element_type=F32) * e_tail
                dv = lax.dot_general(kh, dstb, (((1,), (1,)), ((), ())), preferred_element_type=F32)
                dv = dv + lax.dot_general(att.astype(BF16), dob, (((0,), (0,)), ((), ())),
                                          preferred_element_type=F32)
                dA = lax.dot_general(dob, vb, (((1,), (1,)), ((), ())), preferred_element_type=F32)
                dA7 = (dA[None] * mk).astype(BF16)
                dAT7 = (dA.T[None] * mk).astype(BF16)
                dQs = lax.dot_general(dA7, Ks, (((2,), (1,)), ((0,), (0,))), preferred_element_type=F32)
                dKs = lax.dot_general(dAT7, Qs, (((2,), (1,)), ((0,), (0,))), preferred_element_type=F32)
                dq = dq_inter + jnp.sum(dQs * eq, axis=0)
                dk = dk_inter + jnp.sum(dKs * ek, axis=0)
                dG = (jnp.sum(Qs.astype(F32) * dQs - Ks.astype(F32) * dKs, axis=0)
                      + q * dq_inter - k * dk_inter)
                last_extra = (jnp.sum(k * dk_inter, axis=0, keepdims=True)
                              + e_last * jnp.sum(dst * st, axis=0, keepdims=True))
                is_last = lax.broadcasted_iota(jnp.int32, (C, 1), 0) == C - 1
                dG = dG + jnp.where(is_last, last_extra, 0.0)
                dg = jnp.dot(tr_ref[...], _split3(dG), preferred_element_type=F32)
                df = dg / f - dk
                dfa_ref[sl, cols] = (df * (1.0 - lbv) * (sf * (1.0 - sf))).astype(BF16)
                dlb_ref[:, cols] += jnp.sum(df * (1.0 - sf), axis=0, keepdims=True)
                dqa_ref[sl, cols] = (dq * (sq * (1.0 + qa * (1.0 - sq)))).astype(BF16)
                dia_ref[sl, cols] = dv.astype(BF16)
                dstate[hh] = dst * e_last + lax.dot_general(dob, qg, (((0,), (0,)), ((), ())),
                                                            preferred_element_type=F32)
            return carry

        lax.fori_loop(0, ncs, chunk, 0)

    hpb = HG_HEADS_PER_BLOCK
    wide = hpb * HEAD

    def col(m0):
        return pl.BlockSpec((ts, wide), lambda h, t: (nt - 1 - t, m0 // hpb + h))

    const3 = lambda shape: pl.BlockSpec(shape, lambda h, t: (0, 0, 0))
    const2 = lambda shape: pl.BlockSpec(shape, lambda h, t: (0, 0))
    ocol = pl.BlockSpec((ts, wide), lambda h, t: (nt - 1 - t, h))
    half = nh * HEAD
    return pl.pallas_call(
        body, grid=(nh // hpb, nt), name="hgrn_bwd",
        in_specs=[col(0), col(0), col(nh), col(2 * nh), col(3 * nh), col(0),
                  pl.BlockSpec((hpb, ncs, HEAD, HEAD), lambda h, t: (h, nt - 1 - t, 0, 0)),
                  pl.BlockSpec((1, wide), lambda h, t: (0, h)), const2((1, HEAD)),
                  const2(mstack3.shape), const3(masks.shape), const3(upq.shape), const3(upk.shape),
                  const2(trirev3.shape)],
        out_specs=[ocol, ocol, ocol, ocol, pl.BlockSpec((1, wide), lambda h, t: (0, h)),
                   pl.BlockSpec((hpb, 1, HEAD), lambda h, t: (h, 0, 0))],
        out_shape=[jax.ShapeDtypeStruct((S, half), BF16)] * 4
                  + [jax.ShapeDtypeStruct((1, half), F32), jax.ShapeDtypeStruct((nh, 1, HEAD), F32)],
        scratch_shapes=[pltpu.VMEM((hpb, HEAD, HEAD), F32)],
        compiler_params=_params(("arbitrary", "arbitrary")),
    )(dcat, proj, proj, proj, proj, oraw, states, lb, hg_norm, mstack3, masks, upq, upk, trirev3)


SB_SUB = 128


def _split2(x):
    hi = x.astype(BF16)
    lo = (x - hi.astype(F32)).astype(BF16)
    return jnp.concatenate([hi, lo], axis=1)


def _sb_constants():
    j = np.arange(SB_SUB)
    after = (j[:, None] > j[None, :]).astype(np.float32)
    before = (j[:, None] < j[None, :]).astype(np.float32)
    return (jnp.asarray(np.concatenate([after, after], axis=0), BF16),
            jnp.asarray(np.concatenate([before, before], axis=0), BF16))


def _sb_diag_mask(t):
    return lax.broadcasted_iota(jnp.int32, (t, t), 1) < lax.broadcasted_iota(jnp.int32, (t, t), 0)


def _sb_scores(q, k_ref, col0, t, scale):
    ks = k_ref[pl.ds(pl.multiple_of(col0, t), t), :]
    return lax.dot_general(q, ks, (((1,), (1,)), ((), ())), preferred_element_type=F32) * scale


def _sb_weights(z, mask, run, after2):
    nsub = z.shape[1] // SB_SUB
    nz = -z
    lk = jnp.minimum(nz, 0.0) - jnp.log(1.0 + jnp.exp(jnp.minimum(z, nz)))
    if mask is not None:
        lk = jnp.where(mask, lk, 0.0)
    locs, tots = [], []
    for b in range(nsub):
        lkb = lk[:, b * SB_SUB:(b + 1) * SB_SUB]
        loc = jnp.dot(_split2(lkb), after2, preferred_element_type=F32)
        locs.append(loc)
        tots.append(loc[:, 0:1] + lkb[:, 0:1])
    ws = [None] * nsub
    for b in reversed(range(nsub)):
        sl = slice(b * SB_SUB, (b + 1) * SB_SUB)
        ws[b] = jnp.exp(z[:, sl] + lk[:, sl] + (locs[b] + run))
        run = run + tots[b]
    w = jnp.concatenate(ws, axis=1)
    if mask is not None:
        w = jnp.where(mask, w, 0.0)
    return w, run


def sb_forward(projb, nh, m0, t=512):
    S = projb.shape[0]
    scale = 1.0 / math.sqrt(HEAD)
    after2, _ = _sb_constants()

    def body(q_ref, k_ref, v_ref, af_ref, o_ref):
        i = pl.program_id(1)
        q = q_ref[...]
        after = af_ref[...]

        def block(jb, run, mask):
            z = _sb_scores(q, k_ref, jb * t, t, scale)
            w, run = _sb_weights(z, mask, run, after)
            vs = v_ref[pl.ds(pl.multiple_of(jb * t, t), t), :]
            return run, jnp.dot(w.astype(BF16), vs, preferred_element_type=F32)

        run, acc = block(i, jnp.zeros((t, 1), F32), _sb_diag_mask(t))

        def step(n, carry):
            run, acc = carry
            run, part = block(i - 1 - n, run, None)
            return run, acc + part

        _, acc = lax.fori_loop(0, i, step, (run, acc))
        o_ref[...] = acc.astype(BF16)

    return pl.pallas_call(
        body, grid=(nh, S // t), name="sb_fwd",
        in_specs=[pl.BlockSpec((t, HEAD), lambda h, i: (i, m0 + h)),
                  pl.BlockSpec((S, HEAD), lambda h, i: (0, m0 + nh + h)),
                  pl.BlockSpec((S, HEAD), lambda h, i: (0, m0 + 2 * nh + h)),
                  pl.BlockSpec(after2.shape, lambda h, i: (0, 0))],
        out_specs=pl.BlockSpec((t, HEAD), lambda h, i: (i, h)),
        out_shape=jax.ShapeDtypeStruct((S, nh * HEAD), BF16),
        compiler_params=_params(("arbitrary", "arbitrary")),
    )(projb, projb, projb, after2)


def sb_backward(dcat, projb, nh, m0, t=512):
    S = projb.shape[0]
    scale = 1.0 / math.sqrt(HEAD)
    after2, before2 = _sb_constants()
    n_i = S // t
    nsub = t // SB_SUB

    def body(do_ref, q_ref, k_ref, v_ref, af_ref, bf_ref, dq_ref, dk_ref, dv_ref, dbuf, dk_acc, dv_acc):
        i = pl.program_id(1)

        @pl.when(i == 0)
        def _():
            dk_acc[...] = jnp.zeros_like(dk_acc)
            dv_acc[...] = jnp.zeros_like(dv_acc)

        q = q_ref[...]
        dob = do_ref[...].astype(BF16)
        after = af_ref[...]
        before = bf_ref[...]

        def right_to_left(jb, run, mask):
            ksl = pl.ds(pl.multiple_of(jb * t, t), t)
            z = _sb_scores(q, k_ref, jb * t, t, scale)
            w, run = _sb_weights(z, mask, run, after)
            dw = lax.dot_general(dob, v_ref[ksl, :], (((1,), (1,)), ((), ())), preferred_element_type=F32)
            dbuf[jb] = dw * w
            dv_acc[ksl, :] += lax.dot_general(w.astype(BF16), dob, (((0,), (0,)), ((), ())),
                                              preferred_element_type=F32)
            return run

        run = right_to_left(i, jnp.zeros((t, 1), F32), _sb_diag_mask(t))
        lax.fori_loop(0, i, lambda n, run: right_to_left(i - 1 - n, run, None), run)

        def left_to_right(jb, run, dq, mask):
            ksl = pl.ds(pl.multiple_of(jb * t, t), t)
            z = _sb_scores(q, k_ref, jb * t, t, scale)
            d = dbuf[jb]
            sig = 1.0 / (1.0 + jnp.exp(-z))
            das = []
            for b in range(nsub):
                db = d[:, b * SB_SUB:(b + 1) * SB_SUB]
                prefix = run + jnp.dot(_split2(db), before, preferred_element_type=F32)
                das.append(db - sig[:, b * SB_SUB:(b + 1) * SB_SUB] * (db + prefix))
                run = prefix[:, SB_SUB - 1:SB_SUB] + db[:, SB_SUB - 1:SB_SUB]
            da = jnp.concatenate(das, axis=1)
            if mask is not None:
                da = jnp.where(mask, da, 0.0)
            dab = (da * scale).astype(BF16)
            dq = dq + jnp.dot(dab, k_ref[ksl, :], preferred_element_type=F32)
            dk_acc[ksl, :] += lax.dot_general(dab, q, (((0,), (0,)), ((), ())), preferred_element_type=F32)
            return run, dq

        run, dq = lax.fori_loop(0, i, lambda jb, c: left_to_right(jb, c[0], c[1], None),
                                (jnp.zeros((t, 1), F32), jnp.zeros((t, HEAD), F32)))
        _, dq = left_to_right(i, run, dq, _sb_diag_mask(t))
        dq_ref[...] = dq.astype(BF16)

        @pl.when(i == n_i - 1)
        def _():
            dk_ref[...] = dk_acc[...].astype(BF16)
            dv_ref[...] = dv_acc[...].astype(BF16)

    half = nh * HEAD
    full = pl.BlockSpec((S, HEAD), lambda h, i: (0, h))
    return pl.pallas_call(
        body, grid=(nh, n_i), name="sb_bwd",
        in_specs=[pl.BlockSpec((t, HEAD), lambda h, i: (i, nh + h)),
                  pl.BlockSpec((t, HEAD), lambda h, i: (i, m0 + h)),
                  pl.BlockSpec((S, HEAD), lambda h, i: (0, m0 + nh + h)),
                  pl.BlockSpec((S, HEAD), lambda h, i: (0, m0 + 2 * nh + h)),
                  pl.BlockSpec(after2.shape, lambda h, i: (0, 0)), pl.BlockSpec(before2.shape, lambda h, i: (0, 0))],
        out_specs=[pl.BlockSpec((t, HEAD), lambda h, i: (i, h)), full, full],
        out_shape=[jax.ShapeDtypeStruct((S, half), BF16)] * 3,
        scratch_shapes=[pltpu.VMEM((n_i, t, t), F32), pltpu.VMEM((S, HEAD), F32), pltpu.VMEM((S, HEAD), F32)],
        compiler_params=_params(("arbitrary", "arbitrary")),
    )(dcat, projb, projb, projb, after2, before2)


def local_step(x, target, mix_norm, ffn_norm, final_norm, lb_logits, hg_norm, get_w_in, get_w_rest, send):
    S, D = x.shape
    half = D // 2
    nh = half // HEAD
    tm = 512
    tk = 1024
    row = lambda i, j: (i, 0)

    lb = jax.nn.softmax(lb_logits, axis=0)[0:1]

    h0, r0 = rms_fwd(x, mix_norm[0:1], BF16)
    w_in = get_w_in(h0)
    nbi = w_in.shape[2]
    proj, projb = matmul(
        "proj_in", [h0], [w_in], grid=(N_DEV, S // tm, 1),
        a_spec=pl.BlockSpec((tm, D), lambda j, i, k: (i, 0)),
        b_spec=pl.BlockSpec((None, D, nbi), lambda j, i, k: (j, 0, 0)),
        out_spec=pl.BlockSpec((tm, nbi), lambda j, i, k: (i, j)), out_shape=(S, N_DEV * nbi),
        out_dtypes=[F32, BF16], acc_shape=(8, 128))
    oa, oraw, states = hgrn_forward(proj, lb, hg_norm)
    ob = sb_forward(projb, nh, 4 * nh)
    cat = jnp.concatenate([oa, ob], axis=1)
    w_out, pool_w, pool_scale, wg, wu, wd = get_w_rest(cat)
    (x1,) = matmul(
        "mix_out", [cat], [w_out], grid=(S // tm, 1),
        a_spec=pl.BlockSpec((tm, D), row), b_spec=pl.BlockSpec((D, D), lambda i, k: (0, 0)),
        out_spec=pl.BlockSpec((tm, D), row), out_shape=(S, D), out_dtypes=[F32], acc_shape=(8, 128),
        res=x, res_spec=pl.BlockSpec((tm, D), row))
    h1, r1 = rms_fwd(x1, ffn_norm[0:1], BF16)
    x2, ffn0 = ffn_forward(h1, x1, wg[0], wu[0], wd[0])

    h2, r2 = rms_fwd(x2, mix_norm[1:2], F32)
    x3, pooled = pool_forward(h2, x2, pool_w, pool_scale)
    h3, r3 = rms_fwd(x3, ffn_norm[1:2], BF16)
    x4, ffn1 = ffn_forward(h3, x3, wg[1], wu[1], wd[1])

    loss_blk, dx4, dx4b, d_final = loss_and_final_bwd(x4, final_norm, target)

    dh3, dwg1, dwu1, dwd1 = ffn_backward(dx4b, h3, ffn1, wg[1], wu[1], wd[1])
    dh3 = send("ffn1", dict(ffn_w_gate_1=dwg1, ffn_w_up_1=dwu1, ffn_w_down_1=dwd1), dh3)
    dx3, _, d_ffn1 = rms_bwd(dh3, x3, r3, ffn_norm[1:2], dx4)
    dmixed, dpooled, d_pscale = pool_backward_mix(dx3, pooled, pool_w, pool_scale)
    G = len(POOL_WINDOWS)
    P = D // G
    (d_pool_w,) = matmul(
        "pool_dw", [pooled], [dmixed], grid=(G, S // tk),
        a_spec=pl.BlockSpec((tk, P), lambda g, k: (k, g)), b_spec=pl.BlockSpec((tk, P), lambda g, k: (k, g)),
        out_spec=pl.BlockSpec((None, P, P), lambda g, k: (g, 0, 0)), out_shape=(G, P, P), out_dtypes=[BF16],
        acc_shape=(P, P), trans_a=True)
    dh2 = pool_backward_window(dpooled)
    dx2, dx2b, d_mix1 = rms_bwd(dh2, x2, r2, mix_norm[1:2], dx3)

    dh1, dwg0, dwu0, dwd0 = ffn_backward(dx2b, h1, ffn0, wg[0], wu[0], wd[0])
    dx1, dx1b, d_ffn0 = rms_bwd(dh1, x1, r1, ffn_norm[0:1], dx2)
    (dcat,) = matmul(
        "mix_out_dx", [dx1b], [w_out], grid=(S // tm, 1),
        a_spec=pl.BlockSpec((tm, D), row), b_spec=pl.BlockSpec((D, D), lambda i, k: (0, 0)),
        out_spec=pl.BlockSpec((tm, D), row), out_shape=(S, D), out_dtypes=[F32], acc_shape=(8, 128),
        trans_b=True)
    (d_w_out,) = matmul(
        "mix_out_dw", [cat], [dx1b], grid=(2, S // tk),
        a_spec=pl.BlockSpec((tk, half), lambda m, k: (k, m)), b_spec=pl.BlockSpec((tk, D), lambda m, k: (k, 0)),
        out_spec=pl.BlockSpec((half, D), lambda m, k: (m, 0)), out_shape=(D, D), out_dtypes=[BF16],
        acc_shape=(half, D), trans_a=True)
    dcat = send("layer0", dict(ffn_w_gate_0=dwg0, ffn_w_up_0=dwu0, ffn_w_down_0=dwd0, pool_w=d_pool_w,
                               ab_w_out=d_w_out), dcat)
    dqa, dfa, dia, dga, d_lb, d_hg = hgrn_backward(dcat, proj, oraw, states, lb, hg_norm)
    dqb, dkb, dvb = sb_backward(dcat, projb, nh, 4 * nh)
    dproj = jnp.concatenate([dqa, dfa, dia, dga, dqb, dkb, dvb], axis=1)
    (d_w_in,) = matmul(
        "proj_in_dw", [h0], [dproj], grid=(N_DEV, S // tk),
        a_spec=pl.BlockSpec((tk, D), lambda j, k: (k, 0)), b_spec=pl.BlockSpec((tk, nbi), lambda j, k: (k, j)),
        out_spec=pl.BlockSpec((None, D, nbi), lambda j, k: (j, 0, 0)), out_shape=(N_DEV, D, nbi),
        out_dtypes=[BF16], acc_shape=(D, nbi), trans_a=True)
    dproj = send("w_in", dict(ab_w_in=d_w_in), dproj)
    (dh0,) = matmul(
        "proj_in_dx", [dproj], [w_in], grid=(S // tm, N_DEV),
        a_spec=pl.BlockSpec((tm, nbi), lambda i, j: (i, j)),
        b_spec=pl.BlockSpec((None, D, nbi), lambda i, j: (j, 0, 0)),
        out_spec=pl.BlockSpec((tm, D), row), out_shape=(S, D), out_dtypes=[F32], acc_shape=(tm, D),
        trans_b=True)
    dx0, _, d_mix0 = rms_bwd(dh0, x, r0, mix_norm[0:1], dx1)

    d_l0 = d_lb * lb * (1.0 - lb)
    small = dict(
        loss=loss_blk[0:1, 0:1],
        mix_norm=jnp.concatenate([d_mix0, d_mix1], axis=0),
        ffn_norm=jnp.concatenate([d_ffn0, d_ffn1], axis=0),
        final_norm=d_final,
        lb_logits=jnp.concatenate([d_l0, -d_l0], axis=0),
        hg_out_norm=jnp.sum(d_hg, axis=0),
        pool_scale=d_pscale,
    )
    return dx0, small


def _my_index():
    return 4 * lax.axis_index("x") + 2 * lax.axis_index("y") + lax.axis_index("c")


def _peer(r):
    x, y, c = lax.axis_index("x"), lax.axis_index("y"), lax.axis_index("c")
    px = 1 - x if (r >> 2) & 1 else x
    py = 1 - y if (r >> 1) & 1 else y
    pc = 1 - c if r & 1 else c
    return (px, py, pc), 4 * px + 2 * py + pc


def exchange(name, arrays, gather):
    n = len(arrays)
    n_peers = N_DEV - 1

    def body(*refs):
        ins, outs = refs[:n], refs[n:2 * n]
        send_sems, recv_sems, local_sems = refs[2 * n:]
        me = _my_index()
        local = []
        for a in range(n):
            src = ins[a] if gather else ins[a].at[me]
            cp = pltpu.make_async_copy(src, outs[a].at[me], local_sems.at[a])
            cp.start()
            local.append(cp)
        remote = []
        for a in range(n):
            for r in range(1, N_DEV):
                peer, pidx = _peer(r)
                src = ins[a] if gather else ins[a].at[pidx]
                cp = pltpu.make_async_remote_copy(
                    src_ref=src, dst_ref=outs[a].at[me], send_sem=send_sems.at[a * n_peers + r - 1],
                    recv_sem=recv_sems.at[a * n_peers + r - 1], device_id=peer, device_id_type=MESH)
                cp.start()
                remote.append((cp, a, r))
        for cp, a, r in remote:
            _, pidx = _peer(r)
            src = ins[a] if gather else ins[a].at[pidx]
            pltpu.make_async_remote_copy(
                src_ref=src, dst_ref=outs[a].at[pidx], send_sem=send_sems.at[a * n_peers + r - 1],
                recv_sem=recv_sems.at[a * n_peers + r - 1], device_id=_peer(r)[0], device_id_type=MESH).wait_recv()
        for cp, a, r in remote:
            cp.wait_send()
        for cp in local:
            cp.wait()

    out_shape = [jax.ShapeDtypeStruct(((N_DEV,) + a.shape) if gather else a.shape, a.dtype) for a in arrays]
    any_spec = pl.BlockSpec(memory_space=pl.ANY)
    return pl.pallas_call(
        body, name=name, in_specs=[any_spec] * n, out_specs=[any_spec] * n, out_shape=out_shape,
        scratch_shapes=[pltpu.SemaphoreType.DMA((n * n_peers,)), pltpu.SemaphoreType.DMA((n * n_peers,)),
                        pltpu.SemaphoreType.DMA((n,))],
    )(*arrays)


_HBM = pl.BlockSpec(memory_space=pltpu.HBM)
_SEM = pl.BlockSpec(memory_space=pltpu.SEMAPHORE)
_EFFECT = pltpu.SideEffectType.DATAFLOW_SIDE_EFFECTING


def _landing(arrays, gather):
    me = _my_index()
    lands = []
    for a in arrays:
        own = a[None] if gather else lax.dynamic_slice_in_dim(a, me, 1, axis=0)
        shape = ((N_DEV,) + a.shape) if gather else a.shape
        lands.append(lax.dynamic_update_slice_in_dim(lax.empty(shape, a.dtype), own, me, axis=0))
    return lands


def exchange_start(name, arrays, gather, carry):
    n = len(arrays)
    n_peers = N_DEV - 1
    lands = _landing(arrays, gather)
    n_thru = 2 * n + 1

    def body(*refs):
        src, land = refs[:n], refs[n:2 * n]
        send_sems, recv_sems = refs[n_thru], refs[n_thru + 1]
        token = refs[-1]
        me = _my_index()
        for a in range(n):
            for r in range(1, N_DEV):
                peer, pidx = _peer(r)
                pltpu.make_async_remote_copy(
                    src_ref=src[a] if gather else src[a].at[pidx], dst_ref=land[a].at[me],
                    send_sem=send_sems.at[a * n_peers + r - 1], recv_sem=recv_sems.at[a * n_peers + r - 1],
                    device_id=peer, device_id_type=MESH).start()
        token[...] = jnp.zeros_like(token)

    operands = list(arrays) + lands + [carry]
    outs = pl.pallas_call(
        body, name=name,
        out_shape=(pltpu.SemaphoreType.DMA((n * n_peers,)), pltpu.SemaphoreType.DMA((n * n_peers,)),
                   *[pltpu.HBM(a.shape, a.dtype) for a in operands], jax.ShapeDtypeStruct((8, 128), F32)),
        in_specs=[_HBM] * n_thru,
        out_specs=(_SEM, _SEM, *([_HBM] * n_thru), pl.BlockSpec(memory_space=pltpu.VMEM)),
        input_output_aliases={i: 2 + i for i in range(n_thru)},
        compiler_params=pltpu.CompilerParams(has_side_effects=_EFFECT),
    )(*[pltpu.with_memory_space_constraint(a, pltpu.HBM) for a in operands])
    handle = (outs[0], outs[1], list(outs[2:2 + n]), list(outs[2 + n:2 + 2 * n]), gather)
    return handle, outs[2 + 2 * n]


def exchange_wait(name, handle, after):
    send_sems, recv_sems, srcs, lands, gather = handle
    n = len(srcs)
    n_peers = N_DEV - 1

    def body(*refs):
        src, land = refs[:n], refs[n:2 * n]
        send_s, recv_s = refs[2 * n], refs[2 * n + 1]
        for a in range(n):
            for r in range(1, N_DEV):
                peer, pidx = _peer(r)
                cp = pltpu.make_async_remote_copy(
                    src_ref=src[a] if gather else src[a].at[pidx], dst_ref=land[a].at[pidx],
                    send_sem=send_s.at[a * n_peers + r - 1], recv_sem=recv_s.at[a * n_peers + r - 1],
                    device_id=peer, device_id_type=MESH)
                cp.wait_send()
                cp.wait_recv()

    shapes = [pltpu.HBM(a.shape, a.dtype) for a in srcs] + [pltpu.HBM(l.shape, l.dtype) for l in lands]
    outs = pl.pallas_call(
        body, name=name, out_shape=tuple(shapes),
        in_specs=[_HBM] * (2 * n) + [_SEM, _SEM, pl.BlockSpec(memory_space=pl.ANY)],
        out_specs=tuple([_HBM] * (2 * n)),
        input_output_aliases={i: i for i in range(2 * n)},
        compiler_params=pltpu.CompilerParams(has_side_effects=_EFFECT),
    )(*srcs, *lands, send_sems, recv_sems, after)
    return list(outs[n:])


def _row_tile(rows, cap=256):
    best = None
    for t in range(16, min(rows, cap) + 1, 16):
        if rows % t == 0:
            best = t
    return best if best is not None else rows


def sum_slots(name, recv):
    n, R, C = recv.shape
    tr = _row_tile(R)

    def body(r_ref, o_ref):
        g = r_ref[0].astype(F32)
        for d in range(1, n):
            g = g + r_ref[d].astype(F32)
        o_ref[...] = g

    return pl.pallas_call(
        body, grid=(R // tr,), name=name,
        in_specs=[pl.BlockSpec((n, tr, C), lambda i: (0, i, 0))],
        out_specs=pl.BlockSpec((tr, C), lambda i: (i, 0)),
        out_shape=jax.ShapeDtypeStruct((R, C), F32),
        compiler_params=_params(("arbitrary",)),
    )(recv)


def adamw(name, recv, w, m, v):
    n, R, C = recv.shape
    tr = _row_tile(R)

    def body(r_ref, w_ref, m_ref, v_ref, g_ref, d_ref, nm_ref, nv_ref):
        g = r_ref[0].astype(F32)
        for d in range(1, n):
            g = g + r_ref[d].astype(F32)
        mm = ADAM_B1 * m_ref[...] + (1.0 - ADAM_B1) * g
        vv = ADAM_B2 * v_ref[...] + (1.0 - ADAM_B2) * (g * g)
        m_hat = mm / (1.0 - ADAM_B1 ** ADAM_STEP)
        v_hat = vv / (1.0 - ADAM_B2 ** ADAM_STEP)
        g_ref[...] = g
        d_ref[...] = -ADAM_LR * (m_hat / (jnp.sqrt(v_hat) + ADAM_EPS) + ADAM_WD * w_ref[...])
        nm_ref[...] = mm
        nv_ref[...] = vv

    row = pl.BlockSpec((tr, C), lambda i: (i, 0))
    return pl.pallas_call(
        body, grid=(R // tr,), name=name,
        in_specs=[pl.BlockSpec((n, tr, C), lambda i: (0, i, 0)), row, row, row],
        out_specs=[row] * 4,
        out_shape=[jax.ShapeDtypeStruct((R, C), F32)] * 4,
        compiler_params=_params(("arbitrary",)),
    )(recv, w, m, v)


def _adamw_nd(name, recv, w, m, v):
    shp = w.shape
    C = shp[-1]
    flat = lambda a: a.reshape(-1, C)
    outs = adamw(name, recv.reshape(recv.shape[0], -1, C), flat(w), flat(m), flat(v))
    return [o.reshape(shp) for o in outs]


_SMALL_NAMES = ("loss", "mix_norm", "ffn_norm", "final_norm", "lb_logits", "hg_out_norm", "pool_scale")
_LANES = 128


def _pack_small(parts):
    rows, layout = [], {}
    at = 0
    for name in parts:
        flat = parts[name].reshape(-1).astype(F32)
        n_rows = -(-flat.shape[0] // (8 * _LANES)) * 8
        flat = jnp.pad(flat, (0, n_rows * _LANES - flat.shape[0]))
        rows.append(flat.reshape(n_rows, _LANES))
        layout[name] = (at, parts[name].shape)
        at += n_rows
    return jnp.concatenate(rows, axis=0), layout


def _unpack_small(pack, layout):
    out = {}
    for name, (at, shape) in layout.items():
        size = int(np.prod(shape))
        n_rows = -(-size // _LANES)
        out[name] = pack[at:at + n_rows].reshape(-1)[:size].reshape(shape)
    return out


def kernel(x, mix_norm, ffn_norm, final_norm, ab_w_in, lb_logits, hg_out_norm, ab_w_out, pool_w, pool_scale, ffn_w_gate, ffn_w_up, ffn_w_down, loss_target, m_mix_norm, m_ffn_norm, m_final_norm, m_ab_w_in, m_lb_logits, m_hg_out_norm, m_ab_w_out, m_pool_w, m_pool_scale, m_ffn_w_gate, m_ffn_w_up, m_ffn_w_down, v_mix_norm, v_ffn_norm, v_final_norm, v_ab_w_in, v_lb_logits, v_hg_out_norm, v_ab_w_out, v_pool_w, v_pool_scale, v_ffn_w_gate, v_ffn_w_up, v_ffn_w_down):
    D = x.shape[-1]
    n_layers = ffn_w_gate.shape[0]
    G = pool_w.shape[1]
    P = pool_w.shape[3]
    me = _my_index()

    in_handle, mix_norm_after = exchange_start("gather_w_in_start", [ab_w_in[0].astype(BF16)], True, mix_norm)
    rest = [ab_w_out[0], pool_w[0]]
    for l in range(n_layers):
        rest += [ffn_w_gate[l], ffn_w_up[l], ffn_w_down[l]]
    rest = [s.astype(BF16) for s in rest] + [pool_scale]
    rest_handle = []

    def get_w_in(after):
        w_in = exchange_wait("gather_w_in_wait", in_handle, after)[0]
        handle, w_in = exchange_start("gather_rest_start", rest, True, w_in)
        rest_handle.append(handle)
        return w_in

    def get_w_rest(after):
        got = exchange_wait("gather_rest_wait", rest_handle[0], after)
        w_out_g = got[0].reshape(D, D)
        pool_g = got[1].transpose(1, 0, 2, 3).reshape(G, P, P)
        wg = [got[2 + 3 * l] for l in range(n_layers)]
        wu = [got[3 + 3 * l] for l in range(n_layers)]
        wd = [got[4 + 3 * l] for l in range(n_layers)]
        return w_out_g, pool_g, got[-1].reshape(1, D), wg, wu, wd

    in_flight = []

    def send(tag, grads, carry):
        if "pool_w" in grads:
            grads = dict(grads, pool_w=grads["pool_w"].reshape(G, N_DEV, P // N_DEV, P).transpose(1, 0, 2, 3))
        if "ab_w_out" in grads:
            grads = dict(grads, ab_w_out=grads["ab_w_out"].reshape(N_DEV, D // N_DEV, D))
        handle, carry = exchange_start("grads_" + tag + "_start", list(grads.values()), False, carry)
        in_flight.append((tag, list(grads.keys()), handle))
        return carry

    dx0, small = local_step(x[0], loss_target[0], mix_norm_after, ffn_norm, final_norm[None],
                            lb_logits, hg_out_norm, get_w_in, get_w_rest, send)

    recv = {}
    for tag, names, handle in in_flight:
        recv.update(zip(names, exchange_wait("grads_" + tag + "_wait", handle, dx0)))
    small_pack, layout = _pack_small({k: small[k] for k in _SMALL_NAMES})
    (small_all,) = exchange("gather_small", [small_pack], gather=True)
    tot = _unpack_small(sum_slots("sum_small", small_all), layout)

    res = {}
    res["ab_w_in"] = _adamw_nd("adamw_w_in", recv["ab_w_in"], ab_w_in, m_ab_w_in, v_ab_w_in)
    res["ab_w_out"] = _adamw_nd("adamw_w_out", recv["ab_w_out"], ab_w_out, m_ab_w_out, v_ab_w_out)
    res["pool_w"] = _adamw_nd("adamw_pool_w", recv["pool_w"], pool_w, m_pool_w, v_pool_w)
    ffn_in = {"ffn_w_gate": (ffn_w_gate, m_ffn_w_gate, v_ffn_w_gate),
              "ffn_w_up": (ffn_w_up, m_ffn_w_up, v_ffn_w_up),
              "ffn_w_down": (ffn_w_down, m_ffn_w_down, v_ffn_w_down)}
    for name, (w, m, v) in ffn_in.items():
        per_layer = [_adamw_nd("adamw_" + name, recv[name + "_" + str(l)], w[l], m[l], v[l])
                     for l in range(n_layers)]
        res[name] = [jnp.stack([per_layer[l][o] for l in range(n_layers)]) for o in range(4)]

    n_ps = pool_scale.shape[1]
    small_g = dict(tot)
    small_g["pool_scale"] = lax.dynamic_slice(tot["pool_scale"], (0, me * n_ps), (1, n_ps))
    small_w = dict(mix_norm=(mix_norm, m_mix_norm, v_mix_norm), ffn_norm=(ffn_norm, m_ffn_norm, v_ffn_norm),
                   final_norm=(final_norm, m_final_norm, v_final_norm),
                   lb_logits=(lb_logits, m_lb_logits, v_lb_logits),
                   hg_out_norm=(hg_out_norm, m_hg_out_norm, v_hg_out_norm),
                   pool_scale=(pool_scale, m_pool_scale, v_pool_scale))
    g_pack, lay2 = _pack_small({k: small_g[k].reshape(small_w[k][0].shape) for k in small_w})
    w_pack, _ = _pack_small({k: small_w[k][0] for k in small_w})
    m_pack, _ = _pack_small({k: small_w[k][1] for k in small_w})
    v_pack, _ = _pack_small({k: small_w[k][2] for k in small_w})
    small_out = [_unpack_small(o, lay2) for o in adamw("adamw_small", g_pack[None], w_pack, m_pack, v_pack)]
    for k in small_w:
        res[k] = [small_out[o][k] for o in range(4)]

    order = ("mix_norm", "ffn_norm", "final_norm", "ab_w_in", "lb_logits", "hg_out_norm", "ab_w_out", "pool_w",
             "pool_scale", "ffn_w_gate", "ffn_w_up", "ffn_w_down")
    outs = [tot["loss"].reshape(()), dx0[None]]
    for o in range(4):
        outs += [res[k][o] for k in order]
    return tuple(outs)
```

```python
import functools
import math

import numpy as np
import jax
import jax.numpy as jnp
from jax import lax
from jax.experimental import pallas as pl
from jax.experimental.pallas import tpu as pltpu

F32 = jnp.float32
BF16 = jnp.bfloat16

N_DEV = 8
RMS_EPS = 1e-6
HEAD = 128
HG_CHUNK = 64
HG_HEADS_PER_BLOCK = 2
POOL_WINDOWS = (2, 4, 8, 16)
POOL_HALO = 16
ADAM_LR, ADAM_B1, ADAM_B2, ADAM_EPS, ADAM_WD, ADAM_STEP = 0.001, 0.9, 0.999, 1e-08, 0.01, 10
VMEM_LIMIT_BYTES = 60 * 1024 * 1024
MESH = pl.DeviceIdType.MESH


def _params(sem):
    return pltpu.CompilerParams(dimension_semantics=sem, vmem_limit_bytes=VMEM_LIMIT_BYTES)


def _sigmoid(x):
    return 1.0 / (1.0 + jnp.exp(-x))


def rms_fwd(x, gain, out_dtype, ts=512):
    S, D = x.shape

    def body(x_ref, g_ref, h_ref, r_ref):
        xv = x_ref[...]
        r = lax.rsqrt(jnp.mean(xv * xv, axis=-1, keepdims=True) + RMS_EPS)
        h_ref[...] = ((xv * r) * g_ref[...]).astype(h_ref.dtype)
        r_ref[...] = r

    return pl.pallas_call(
        body, grid=(S // ts,), name="rms_fwd",
        in_specs=[pl.BlockSpec((ts, D), lambda i: (i, 0)), pl.BlockSpec((1, D), lambda i: (0, 0))],
        out_specs=[pl.BlockSpec((ts, D), lambda i: (i, 0)), pl.BlockSpec((ts, 1), lambda i: (i, 0))],
        out_shape=[jax.ShapeDtypeStruct((S, D), out_dtype), jax.ShapeDtypeStruct((S, 1), F32)],
        compiler_params=_params(("arbitrary",)),
    )(x, gain)


def rms_bwd(dh, x, r, gain, dres, ts=512):
    S, D = x.shape

    def body(dh_ref, x_ref, r_ref, g_ref, dres_ref, dx_ref, dxb_ref, dg_ref):
        i = pl.program_id(0)
        rr = r_ref[...]
        xh = x_ref[...] * rr
        dhv = dh_ref[...]
        dxh = dhv * g_ref[...]
        dx = dres_ref[...] + rr * (dxh - xh * jnp.mean(dxh * xh, axis=-1, keepdims=True))
        dx_ref[...] = dx
        dxb_ref[...] = dx.astype(BF16)
        part = jnp.sum(dhv * xh, axis=0, keepdims=True)

        @pl.when(i == 0)
        def _():
            dg_ref[...] = part

        @pl.when(i > 0)
        def _():
            dg_ref[...] += part

    row = pl.BlockSpec((ts, D), lambda i: (i, 0))
    vec = pl.BlockSpec((1, D), lambda i: (0, 0))
    return pl.pallas_call(
        body, grid=(S // ts,), name="rms_bwd",
        in_specs=[row, row, pl.BlockSpec((ts, 1), lambda i: (i, 0)), vec, row],
        out_specs=[row, row, vec],
        out_shape=[jax.ShapeDtypeStruct((S, D), F32), jax.ShapeDtypeStruct((S, D), BF16),
                   jax.ShapeDtypeStruct((1, D), F32)],
        compiler_params=_params(("arbitrary",)),
    )(dh, x, r, gain, dres)


def loss_and_final_bwd(x, gain, target, ts=512):
    S, D = x.shape

    def body(x_ref, g_ref, t_ref, loss_ref, dx_ref, dxb_ref, dg_ref):
        i = pl.program_id(0)
        xv = x_ref[...]
        rr = lax.rsqrt(jnp.mean(xv * xv, axis=-1, keepdims=True) + RMS_EPS)
        xh = xv * rr
        err = xh * g_ref[...] - t_ref[...]
        part_loss = 0.5 * jnp.sum(jnp.mean(err * err, axis=-1, keepdims=True))
        dy = err / D
        dxh = dy * g_ref[...]
        dx = rr * (dxh - xh * jnp.mean(dxh * xh, axis=-1, keepdims=True))
        dx_ref[...] = dx
        dxb_ref[...] = dx.astype(BF16)
        part = jnp.sum(dy * xh, axis=0, keepdims=True)

        @pl.when(i == 0)
        def _():
            dg_ref[...] = part
            loss_ref[...] = jnp.zeros_like(loss_ref) + part_loss

        @pl.when(i > 0)
        def _():
            dg_ref[...] += part
            loss_ref[...] += part_loss

    row = pl.BlockSpec((ts, D), lambda i: (i, 0))
    vec = pl.BlockSpec((1, D), lambda i: (0, 0))
    return pl.pallas_call(
        body, grid=(S // ts,), name="loss_final",
        in_specs=[row, vec, row],
        out_specs=[pl.BlockSpec((8, 128), lambda i: (0, 0)), row, row, vec],
        out_shape=[jax.ShapeDtypeStruct((8, 128), F32), jax.ShapeDtypeStruct((S, D), F32),
                   jax.ShapeDtypeStruct((S, D), BF16), jax.ShapeDtypeStruct((1, D), F32)],
        compiler_params=_params(("arbitrary",)),
    )(x, gain, target)


def matmul(name, a_ops, b_ops, *, grid, a_spec, b_spec, out_spec, out_shape, out_dtypes, acc_shape,
           trans_a=False, trans_b=False, res=None, res_spec=None, bf16_scale=None, bf16_scale_spec=None):
    n_pairs = len(a_ops)
    n_out = len(out_dtypes)
    nk = grid[-1]
    kaxis = len(grid) - 1
    dn = (((0,) if trans_a else (1,), (1,) if trans_b else (0,)), ((), ()))

    def body(*refs):
        a_refs = refs[:n_pairs]
        b_refs = refs[n_pairs:2 * n_pairs]
        pos = 2 * n_pairs
        res_ref = None
        if res is not None:
            res_ref = refs[pos]
            pos += 1
        scale_ref = None
        if bf16_scale is not None:
            scale_ref = refs[pos]
            pos += 1
        out_refs = refs[pos:pos + n_out]
        acc_ref = refs[pos + n_out]
        k = pl.program_id(kaxis)
        in_place = n_out == 1 and out_dtypes[0] == F32
        target = out_refs[0] if in_place else acc_ref

        def finish(val):
            if res_ref is not None:
                val = val + res_ref[...]
            for o in out_refs:
                if scale_ref is not None and o.dtype == BF16:
                    o[...] = (val * scale_ref[...]).astype(BF16)
                else:
                    o[...] = val.astype(o.dtype)

        if nk > 1:
            @pl.when(k == 0)
            def _():
                if in_place and res_ref is not None:
                    target[...] = res_ref[...]
                else:
                    target[...] = jnp.zeros_like(target)

        part = None
        for ar, br in zip(a_refs, b_refs):
            d = lax.dot_general(ar[...].astype(BF16), br[...].astype(BF16), dn, preferred_element_type=F32)
            part = d if part is None else part + d

        if nk == 1:
            finish(part)
        else:
            target[...] += part
            if not in_place:
                @pl.when(k == nk - 1)
                def _():
                    finish(acc_ref[...])

    in_specs = [a_spec] * n_pairs + [b_spec] * n_pairs
    operands = list(a_ops) + list(b_ops)
    if res is not None:
        in_specs.append(res_spec)
        operands.append(res)
    if bf16_scale is not None:
        in_specs.append(bf16_scale_spec)
        operands.append(bf16_scale)
    return pl.pallas_call(
        body, grid=grid, name=name, in_specs=in_specs,
        out_specs=[out_spec] * n_out,
        out_shape=[jax.ShapeDtypeStruct(out_shape, dt) for dt in out_dtypes],
        scratch_shapes=[pltpu.VMEM(acc_shape, F32)],
        compiler_params=_params(("arbitrary",) * len(grid)),
    )(*operands)


def ffn_gate_up(h, wg, wu, tm=512):
    S, D = h.shape
    nb = wg.shape[2]

    def body(h_ref, wg_ref, wu_ref, g_ref, u_ref, a_ref):
        for c in range(2):
            rows = slice(c * (tm // 2), (c + 1) * (tm // 2))
            hv = h_ref[rows, :]
            g = jnp.dot(hv, wg_ref[...], preferred_element_type=F32)
            u = jnp.dot(hv, wu_ref[...], preferred_element_type=F32)
            g_ref[rows, :] = g
            u_ref[rows, :] = u
            a_ref[rows, :] = (g * _sigmoid(g) * u).astype(BF16)

    wspec = pl.BlockSpec((None, D, nb), lambda j, i: (j, 0, 0))
    ospec = pl.BlockSpec((None, tm, nb), lambda j, i: (j, i, 0))
    return pl.pallas_call(
        body, grid=(N_DEV, S // tm), name="ffn_gate_up",
        in_specs=[pl.BlockSpec((tm, D), lambda j, i: (i, 0)), wspec, wspec],
        out_specs=[ospec, ospec, ospec],
        out_shape=[jax.ShapeDtypeStruct((N_DEV, S, nb), F32), jax.ShapeDtypeStruct((N_DEV, S, nb), F32),
                   jax.ShapeDtypeStruct((N_DEV, S, nb), BF16)],
        compiler_params=_params(("arbitrary", "arbitrary")),
    )(h, wg, wu)


def ffn_bwd_hidden(dy, wd, g, u, tm=512):
    S, D = dy.shape
    nb = wd.shape[1]

    def body(dy_ref, wd_ref, g_ref, u_ref, dg_ref, du_ref):
        for c in range(2):
            rows = slice(c * (tm // 2), (c + 1) * (tm // 2))
            da = lax.dot_general(dy_ref[rows, :], wd_ref[...], (((1,), (1,)), ((), ())),
                                 preferred_element_type=F32)
            gv = g_ref[rows, :]
            s = _sigmoid(gv)
            du_ref[rows, :] = (da * (gv * s)).astype(BF16)
            dg_ref[rows, :] = (da * u_ref[rows, :] * (s * (1.0 + gv * (1.0 - s)))).astype(BF16)

    hspec = pl.BlockSpec((None, tm, nb), lambda j, i: (j, i, 0))
    return pl.pallas_call(
        body, grid=(N_DEV, S // tm), name="ffn_bwd_hidden",
        in_specs=[pl.BlockSpec((tm, D), lambda j, i: (i, 0)), pl.BlockSpec((None, nb, D), lambda j, i: (j, 0, 0)),
                  hspec, hspec],
        out_specs=[hspec, hspec],
        out_shape=[jax.ShapeDtypeStruct((N_DEV, S, nb), BF16), jax.ShapeDtypeStruct((N_DEV, S, nb), BF16)],
        compiler_params=_params(("arbitrary", "arbitrary")),
    )(dy, wd, g, u)


def ffn_forward(h, xres, wg, wu, wd, tm=512):
    S, D = h.shape
    nb = wg.shape[2]
    g, u, a = ffn_gate_up(h, wg, wu)
    (xo,) = matmul(
        "ffn_down", [a], [wd], grid=(S // tm, N_DEV),
        a_spec=pl.BlockSpec((None, tm, nb), lambda i, j: (j, i, 0)),
        b_spec=pl.BlockSpec((None, nb, D), lambda i, j: (j, 0, 0)),
        out_spec=pl.BlockSpec((tm, D), lambda i, j: (i, 0)), out_shape=(S, D), out_dtypes=[F32],
        acc_shape=(tm, D), res=xres, res_spec=pl.BlockSpec((tm, D), lambda i, j: (i, 0)))
    return xo, (g, u, a)


def ffn_backward(dy_b, h, saved, wg, wu, wd, tm=512, tk=1024):
    S, D = h.shape
    nb = wg.shape[2]
    g, u, a = saved
    dg, du = ffn_bwd_hidden(dy_b, wd, g, u)
    (dh,) = matmul(
        "ffn_dh", [dg, du], [wg, wu], grid=(S // tm, N_DEV),
        a_spec=pl.BlockSpec((None, tm, nb), lambda i, j: (j, i, 0)),
        b_spec=pl.BlockSpec((None, D, nb), lambda i, j: (j, 0, 0)),
        out_spec=pl.BlockSpec((tm, D), lambda i, j: (i, 0)), out_shape=(S, D), out_dtypes=[F32],
        acc_shape=(tm, D), trans_b=True)

    def wgrad_in(name, dhid):
        (dw,) = matmul(
            name, [h], [dhid], grid=(N_DEV, S // tk),
            a_spec=pl.BlockSpec((tk, D), lambda j, k: (k, 0)),
            b_spec=pl.BlockSpec((None, tk, nb), lambda j, k: (j, k, 0)),
            out_spec=pl.BlockSpec((None, D, nb), lambda j, k: (j, 0, 0)), out_shape=(N_DEV, D, nb),
            out_dtypes=[BF16], acc_shape=(D, nb), trans_a=True)
        return dw

    dwg = wgrad_in("ffn_dwg", dg)
    dwu = wgrad_in("ffn_dwu", du)
    (dwd,) = matmul(
        "ffn_dwd", [a], [dy_b], grid=(N_DEV, S // tk),
        a_spec=pl.BlockSpec((None, tk, nb), lambda j, k: (j, k, 0)),
        b_spec=pl.BlockSpec((tk, D), lambda j, k: (k, 0)),
        out_spec=pl.BlockSpec((None, nb, D), lambda j, k: (j, 0, 0)), out_shape=(N_DEV, nb, D),
        out_dtypes=[BF16], acc_shape=(nb, D), trans_a=True)
    return dh, dwg, dwu, dwd


def _pool_counts(row0, n, w):
    pos = row0 + lax.broadcasted_iota(jnp.int32, (n, 1), 0)
    return jnp.minimum(pos + 1, w).astype(F32)


def pool_forward(h, xres, w, scale, ts=256):
    S, D = h.shape
    G = len(POOL_WINDOWS)
    P = D // G
    hb = ts // POOL_HALO

    def body(h_ref, halo_ref, x_ref, w_ref, s_ref, xo_ref, p_ref):
        i = pl.program_id(0)
        for gi, win in enumerate(POOL_WINDOWS):
            cols = slice(gi * P, (gi + 1) * P)
            cur = h_ref[:, cols]
            halo = jnp.where(i > 0, halo_ref[:, cols], 0.0)
            acc = jnp.concatenate([halo, cur], axis=0)
            step = 1
            while step < win:
                acc = acc + pltpu.roll(acc, step, 0)
                step *= 2
            wsum = acc[POOL_HALO:, :]
            pooled = wsum / _pool_counts(i * ts, ts, win) - cur
            pb = pooled.astype(BF16)
            p_ref[:, cols] = pb
            mixed = jnp.dot(pb, w_ref[gi], preferred_element_type=F32)
            xo_ref[:, cols] = x_ref[:, cols] + mixed * s_ref[:, cols]

    row = pl.BlockSpec((ts, D), lambda i: (i, 0))
    return pl.pallas_call(
        body, grid=(S // ts,), name="pool_fwd",
        in_specs=[row, pl.BlockSpec((POOL_HALO, D), lambda i: (jnp.maximum(i * hb - 1, 0), 0)), row,
                  pl.BlockSpec((G, P, P), lambda i: (0, 0, 0)), pl.BlockSpec((1, D), lambda i: (0, 0))],
        out_specs=[row, row],
        out_shape=[jax.ShapeDtypeStruct((S, D), F32), jax.ShapeDtypeStruct((S, D), BF16)],
        compiler_params=_params(("arbitrary",)),
    )(h, h, xres, w, scale)


def pool_backward_mix(dx, pooled, w, scale, ts=256):
    S, D = dx.shape
    G = len(POOL_WINDOWS)
    P = D // G

    def body(dx_ref, p_ref, w_ref, s_ref, dm_ref, dp_ref, ds_ref):
        i = pl.program_id(0)
        parts = []
        for gi in range(G):
            cols = slice(gi * P, (gi + 1) * P)
            dxv = dx_ref[:, cols]
            dmb = (dxv * s_ref[:, cols]).astype(BF16)
            dm_ref[:, cols] = dmb
            dp_ref[:, cols] = lax.dot_general(dmb, w_ref[gi], (((1,), (1,)), ((), ())),
                                              preferred_element_type=F32)
            mixed = jnp.dot(p_ref[:, cols], w_ref[gi], preferred_element_type=F32)
            parts.append(jnp.sum(dxv * mixed, axis=0, keepdims=True))
        part = jnp.concatenate(parts, axis=1)

        @pl.when(i == 0)
        def _():
            ds_ref[...] = part

        @pl.when(i > 0)
        def _():
            ds_ref[...] += part

    row = pl.BlockSpec((ts, D), lambda i: (i, 0))
    vec = pl.BlockSpec((1, D), lambda i: (0, 0))
    return pl.pallas_call(
        body, grid=(S // ts,), name="pool_bwd_mix",
        in_specs=[row, row, pl.BlockSpec((G, P, P), lambda i: (0, 0, 0)), vec],
        out_specs=[row, row, vec],
        out_shape=[jax.ShapeDtypeStruct((S, D), BF16), jax.ShapeDtypeStruct((S, D), F32),
                   jax.ShapeDtypeStruct((1, D), F32)],
        compiler_params=_params(("arbitrary",)),
    )(dx, pooled, w, scale)


def pool_backward_window(dp, ts=256):
    S, D = dp.shape
    G = len(POOL_WINDOWS)
    P = D // G
    hb = ts // POOL_HALO
    n_i = S // ts
    n_rows = ts + POOL_HALO

    def body(dp_ref, halo_ref, dh_ref):
        i = pl.program_id(0)
        for gi, win in enumerate(POOL_WINDOWS):
            cols = slice(gi * P, (gi + 1) * P)
            cur = dp_ref[:, cols]
            halo = jnp.where(i < n_i - 1, halo_ref[:, cols], 0.0)
            acc = jnp.concatenate([cur / _pool_counts(i * ts, ts, win),
                                   halo / _pool_counts((i + 1) * ts, POOL_HALO, win)], axis=0)
            step = 1
            while step < win:
                acc = acc + pltpu.roll(acc, n_rows - step, 0)
                step *= 2
            dh_ref[:, cols] = acc[:ts, :] - cur

    row = pl.BlockSpec((ts, D), lambda i: (i, 0))
    return pl.pallas_call(
        body, grid=(n_i,), name="pool_bwd_window",
        in_specs=[row, pl.BlockSpec((POOL_HALO, D), lambda i: (jnp.minimum((i + 1) * hb, S // POOL_HALO - 1), 0))],
        out_specs=row,
        out_shape=jax.ShapeDtypeStruct((S, D), F32),
        compiler_params=_params(("arbitrary",)),
    )(dp, dp)


_HG_LEVELS = (32, 16, 8, 4, 2, 1)
_N_LEV = len(_HG_LEVELS) + 1


def _hgrn_constants():
    C = HG_CHUNK
    t = np.arange(C)
    tri = (t[None, :] <= t[:, None]).astype(np.float32)
    blocks = [tri]
    masks, upq, upk = [], [], []
    for m in _HG_LEVELS:
        p = (t // (2 * m)) * 2 * m + m - 1
        blocks.append(tri[p])
        masks.append(((t[:, None] // (2 * m)) == (t[None, :] // (2 * m))).astype(np.float32))
        upper = (t % (2 * m)) >= m
        upq.append(np.repeat(upper[:, None], HEAD, 1).astype(np.float32))
        upk.append(np.repeat(~upper[:, None], HEAD, 1).astype(np.float32))
    blocks.append(tri)
    masks.append(np.eye(C, dtype=np.float32))
    upq.append(np.ones((C, HEAD), np.float32))
    upk.append(np.ones((C, HEAD), np.float32))
    mstack = np.concatenate(blocks, axis=0)
    mstack3 = np.concatenate([mstack] * 3, axis=1)
    trirev3 = np.concatenate([tri.T] * 3, axis=1)
    return (jnp.asarray(mstack3, BF16), jnp.asarray(np.stack(masks)), jnp.asarray(np.stack(upq)),
            jnp.asarray(np.stack(upk)), jnp.asarray(trirev3, BF16))


def _split3(x):
    hi = x.astype(BF16)
    r1 = x - hi.astype(F32)
    mid = r1.astype(BF16)
    lo = (r1 - mid.astype(F32)).astype(BF16)
    return jnp.concatenate([hi, mid, lo], axis=0)


def _hgrn_chunk_common(qa, fa, lbv, mstack3, upq, upk):
    sq = _sigmoid(qa)
    q = qa * sq
    sf = _sigmoid(fa)
    f = lbv + (1.0 - lbv) * sf
    g = jnp.log(f)
    k = 1.0 - f
    gall = jnp.dot(mstack3, _split3(g), preferred_element_type=F32).reshape(_N_LEV + 1, HG_CHUNK, HEAD)
    G = gall[0]
    eq_exp = G[None] - gall[1:]
    eq = jnp.exp(jnp.minimum(eq_exp, 0.0)) * upq
    ek = jnp.exp(jnp.minimum(-eq_exp, 0.0)) * upk
    Qs = (q[None] * eq).astype(BF16)
    Ks = (k[None] * ek).astype(BF16)
    return sq, q, sf, f, k, G, eq, ek, Qs, Ks


def hgrn_forward(proj, lb, hg_norm, ts=512):
    S = proj.shape[0]
    nh = lb.shape[1] // HEAD
    C = HG_CHUNK
    ncs = ts // C
    mstack3, masks, upq, upk, _ = _hgrn_constants()

    def body(qa_ref, fa_ref, ia_ref, ga_ref, lb_ref, gn_ref, ms_ref, mk_ref, uq_ref, uk_ref,
             oa_ref, oraw_ref, st_ref, state):
        tt = pl.program_id(1)

        @pl.when(tt == 0)
        def _():
            state[...] = jnp.zeros_like(state)

        gn = gn_ref[...]

        def chunk(c, carry):
            sl = pl.ds(pl.multiple_of(c * C, C), C)
            for hh in range(HG_HEADS_PER_BLOCK):
                cols = slice(hh * HEAD, (hh + 1) * HEAD)
                qa, fa, v, ga = qa_ref[sl, cols], fa_ref[sl, cols], ia_ref[sl, cols], ga_ref[sl, cols]
                _, q, _, _, k, G, _, _, Qs, Ks = _hgrn_chunk_common(qa, fa, lb_ref[:, cols], ms_ref[...],
                                                                    uq_ref[...], uk_ref[...])
                att7 = lax.dot_general(Qs, Ks, (((2,), (2,)), ((0,), (0,))), preferred_element_type=F32)
                att = jnp.sum(att7 * mk_ref[...], axis=0)
                st = state[hh]
                st_ref[hh, c] = st
                vb = v.astype(BF16)
                qg = (q * jnp.exp(G)).astype(BF16)
                o = jnp.dot(att.astype(BF16), vb, preferred_element_type=F32)
                o = o + lax.dot_general(qg, st.astype(BF16), (((1,), (1,)), ((), ())),
                                        preferred_element_type=F32)
                g_last = G[C - 1:C, :]
                kh = (k * jnp.exp(g_last - G)).astype(BF16)
                state[hh] = st * jnp.exp(g_last) + lax.dot_general(vb, kh, (((0,), (0,)), ((), ())),
                                                                   preferred_element_type=F32)
                oraw_ref[sl, cols] = o
                r = lax.rsqrt(jnp.mean(o * o, axis=-1, keepdims=True) + RMS_EPS)
                oa_ref[sl, cols] = (((o * r) * gn) * (ga * _sigmoid(ga))).astype(BF16)
            return carry

        lax.fori_loop(0, ncs, chunk, 0)

    hpb = HG_HEADS_PER_BLOCK
    wide = hpb * HEAD

    def col(m0):
        return pl.BlockSpec((ts, wide), lambda h, t: (t, m0 // hpb + h))

    const3 = lambda shape: pl.BlockSpec(shape, lambda h, t: (0, 0, 0))
    return pl.pallas_call(
        body, grid=(nh // hpb, S // ts), name="hgrn_fwd",
        in_specs=[col(0), col(nh), col(2 * nh), col(3 * nh),
                  pl.BlockSpec((1, wide), lambda h, t: (0, h)), pl.BlockSpec((1, HEAD), lambda h, t: (0, 0)),
                  pl.BlockSpec(mstack3.shape, lambda h, t: (0, 0)), const3(masks.shape), const3(upq.shape),
                  const3(upk.shape)],
        out_specs=[pl.BlockSpec((ts, wide), lambda h, t: (t, h)), pl.BlockSpec((ts, wide), lambda h, t: (t, h)),
                   pl.BlockSpec((hpb, ncs, HEAD, HEAD), lambda h, t: (h, t, 0, 0))],
        out_shape=[jax.ShapeDtypeStruct((S, nh * HEAD), BF16), jax.ShapeDtypeStruct((S, nh * HEAD), F32),
                   jax.ShapeDtypeStruct((nh, S // C, HEAD, HEAD), F32)],
        scratch_shapes=[pltpu.VMEM((hpb, HEAD, HEAD), F32)],
        compiler_params=_params(("arbitrary", "arbitrary")),
    )(proj, proj, proj, proj, lb, hg_norm, mstack3, masks, upq, upk)


def hgrn_backward(dcat, proj, oraw, states, lb, hg_norm, ts=512):
    S = proj.shape[0]
    nh = lb.shape[1] // HEAD
    C = HG_CHUNK
    ncs = ts // C
    nt = S // ts
    mstack3, masks, upq, upk, trirev3 = _hgrn_constants()

    def body(do_ref, qa_ref, fa_ref, ia_ref, ga_ref, or_ref, st_ref, lb_ref, gn_ref, ms_ref, mk_ref, uq_ref,
             uk_ref, tr_ref, dqa_ref, dfa_ref, dia_ref, dga_ref, dlb_ref, dgn_ref, dstate):
        tt = pl.program_id(1)

        @pl.when(tt == 0)
        def _():
            dstate[...] = jnp.zeros_like(dstate)
            dlb_ref[...] = jnp.zeros_like(dlb_ref)
            dgn_ref[...] = jnp.zeros_like(dgn_ref)

        gn = gn_ref[...]

        def chunk(cc, carry):
            c = ncs - 1 - cc
            sl = pl.ds(pl.multiple_of(c * C, C), C)
            for hh in range(HG_HEADS_PER_BLOCK):
                cols = slice(hh * HEAD, (hh + 1) * HEAD)
                lbv = lb_ref[:, cols]
                qa, fa, v, ga = qa_ref[sl, cols], fa_ref[sl, cols], ia_ref[sl, cols], ga_ref[sl, cols]
                sq, q, sf, f, k, G, eq, ek, Qs, Ks = _hgrn_chunk_common(qa, fa, lbv, ms_ref[...], uq_ref[...],
                                                                        uk_ref[...])
                mk = mk_ref[...]
                att7 = lax.dot_general(Qs, Ks, (((2,), (2,)), ((0,), (0,))), preferred_element_type=F32)
                att = jnp.sum(att7 * mk, axis=0)
                o = or_ref[sl, cols]
                dO = do_ref[sl, cols]
                sg = _sigmoid(ga)
                r = lax.rsqrt(jnp.mean(o * o, axis=-1, keepdims=True) + RMS_EPS)
                xh = o * r
                dga_ref[sl, cols] = (dO * (xh * gn) * (sg * (1.0 + ga * (1.0 - sg)))).astype(BF16)
                don = dO * (ga * sg)
                dgn_ref[hh] += jnp.sum(don * xh, axis=0, keepdims=True)
                dxh = don * gn
                do = r * (dxh - xh * jnp.mean(dxh * xh, axis=-1, keepdims=True))
                dob = do.astype(BF16)
                st = st_ref[hh, c]
                dst = dstate[hh]
                dstb = dst.astype(BF16)
                vb = v.astype(BF16)
                eG = jnp.exp(G)
                g_last = G[C - 1:C, :]
                e_last = jnp.exp(g_last)
                e_tail = jnp.exp(g_last - G)
                qg = (q * eG).astype(BF16)
                kh = (k * e_tail).astype(BF16)
                dq_inter = jnp.dot(dob, st.astype(BF16), preferred_element_type=F32) * eG
                dk_inter = jnp.dot(vb, dstb, preferred_element_type=F32) * e_tail
                dv = lax.dot_general(kh, dstb, (((1,), (1,)), ((), ())), preferred_element_type=F32)
                dv = dv + lax.dot_general(att.astype(BF16), dob, (((0,), (0,)), ((), ())),
                                          preferred_element_type=F32)
                dA = lax.dot_general(dob, vb, (((1,), (1,)), ((), ())), preferred_element_type=F32)
                dA7 = (dA[None] * mk).astype(BF16)
                dAT7 = (dA.T[None] * mk).astype(BF16)
                dQs = lax.dot_general(dA7, Ks, (((2,), (1,)), ((0,), (0,))), preferred_element_type=F32)
                dKs = lax.dot_general(dAT7, Qs, (((2,), (1,)), ((0,), (0,))), preferred_element_type=F32)
                dq = dq_inter + jnp.sum(dQs * eq, axis=0)
                dk = dk_inter + jnp.sum(dKs * ek, axis=0)
                dG = (jnp.sum(Qs.astype(F32) * dQs - Ks.astype(F32) * dKs, axis=0)
                      + q * dq_inter - k * dk_inter)
                last_extra = (jnp.sum(k * dk_inter, axis=0, keepdims=True)
                              + e_last * jnp.sum(dst * st, axis=0, keepdims=True))
                is_last = lax.broadcasted_iota(jnp.int32, (C, 1), 0) == C - 1
                dG = dG + jnp.where(is_last, last_extra, 0.0)
                dg = jnp.dot(tr_ref[...], _split3(dG), preferred_element_type=F32)
                df = dg / f - dk
                dfa_ref[sl, cols] = (df * (1.0 - lbv) * (sf * (1.0 - sf))).astype(BF16)
                dlb_ref[:, cols] += jnp.sum(df * (1.0 - sf), axis=0, keepdims=True)
                dqa_ref[sl, cols] = (dq * (sq * (1.0 + qa * (1.0 - sq)))).astype(BF16)
                dia_ref[sl, cols] = dv.astype(BF16)
                dstate[hh] = dst * e_last + lax.dot_general(dob, qg, (((0,), (0,)), ((), ())),
                                                            preferred_element_type=F32)
            return carry

        lax.fori_loop(0, ncs, chunk, 0)

    hpb = HG_HEADS_PER_BLOCK
    wide = hpb * HEAD

    def col(m0):
        return pl.BlockSpec((ts, wide), lambda h, t: (nt - 1 - t, m0 // hpb + h))

    const3 = lambda shape: pl.BlockSpec(shape, lambda h, t: (0, 0, 0))
    const2 = lambda shape: pl.BlockSpec(shape, lambda h, t: (0, 0))
    ocol = pl.BlockSpec((ts, wide), lambda h, t: (nt - 1 - t, h))
    half = nh * HEAD
    return pl.pallas_call(
        body, grid=(nh // hpb, nt), name="hgrn_bwd",
        in_specs=[col(0), col(0), col(nh), col(2 * nh), col(3 * nh), col(0),
                  pl.BlockSpec((hpb, ncs, HEAD, HEAD), lambda h, t: (h, nt - 1 - t, 0, 0)),
                  pl.BlockSpec((1, wide), lambda h, t: (0, h)), const2((1, HEAD)),
                  const2(mstack3.shape), const3(masks.shape), const3(upq.shape), const3(upk.shape),
                  const2(trirev3.shape)],
        out_specs=[ocol, ocol, ocol, ocol, pl.BlockSpec((1, wide), lambda h, t: (0, h)),
                   pl.BlockSpec((hpb, 1, HEAD), lambda h, t: (h, 0, 0))],
        out_shape=[jax.ShapeDtypeStruct((S, half), BF16)] * 4
                  + [jax.ShapeDtypeStruct((1, half), F32), jax.ShapeDtypeStruct((nh, 1, HEAD), F32)],
        scratch_shapes=[pltpu.VMEM((hpb, HEAD, HEAD), F32)],
        compiler_params=_params(("arbitrary", "arbitrary")),
    )(dcat, proj, proj, proj, proj, oraw, states, lb, hg_norm, mstack3, masks, upq, upk, trirev3)


SB_SUB = 128
LOG2_E = 1.4426950408889634
SB_SCALE = 1.0 / math.sqrt(HEAD)
SB_QUERY_SCALE = SB_SCALE * LOG2_E


def _split2(x):
    hi = x.astype(BF16)
    lo = (x - hi.astype(F32)).astype(BF16)
    return jnp.concatenate([hi, lo], axis=1)


def _sb_constants():
    j = np.arange(SB_SUB)
    after = (j[:, None] > j[None, :]).astype(np.float32)
    before = (j[:, None] < j[None, :]).astype(np.float32)
    return (jnp.asarray(np.concatenate([after, after], axis=0), BF16),
            jnp.asarray(np.concatenate([before, before], axis=0), BF16))


def _sb_diag_mask(t):
    return lax.broadcasted_iota(jnp.int32, (t, t), 1) < lax.broadcasted_iota(jnp.int32, (t, t), 0)


def _sb_scores(q, k_ref, col0, t):
    ks = k_ref[pl.ds(pl.multiple_of(col0, t), t), :]
    return lax.dot_general(q, ks, (((1,), (1,)), ((), ())), preferred_element_type=F32)


def _sb_weights(z, mask, run, after2):
    nsub = z.shape[1] // SB_SUB
    nz = -z
    lk = jnp.minimum(nz, 0.0) - jnp.log(1.0 + jnp.exp2(jnp.minimum(z, nz))) * LOG2_E
    if mask is not None:
        lk = jnp.where(mask, lk, 0.0)
    locs, tots = [], []
    for b in range(nsub):
        lkb = lk[:, b * SB_SUB:(b + 1) * SB_SUB]
        loc = jnp.dot(_split2(lkb), after2, preferred_element_type=F32)
        locs.append(loc)
        tots.append(loc[:, 0:1] + lkb[:, 0:1])
    ws = [None] * nsub
    for b in reversed(range(nsub)):
        sl = slice(b * SB_SUB, (b + 1) * SB_SUB)
        ws[b] = jnp.exp2(z[:, sl] + lk[:, sl] + (locs[b] + run))
        run = run + tots[b]
    w = jnp.concatenate(ws, axis=1)
    if mask is not None:
        w = jnp.where(mask, w, 0.0)
    return w, run


def sb_forward(projb, nh, m0, t=512):
    S = projb.shape[0]
    after2, _ = _sb_constants()

    def body(q_ref, k_ref, v_ref, af_ref, o_ref):
        i = pl.program_id(1)
        q = q_ref[...]
        after = af_ref[...]

        def block(jb, run, mask):
            z = _sb_scores(q, k_ref, jb * t, t)
            w, run = _sb_weights(z, mask, run, after)
            vs = v_ref[pl.ds(pl.multiple_of(jb * t, t), t), :]
            return run, jnp.dot(w.astype(BF16), vs, preferred_element_type=F32)

        run, acc = block(i, jnp.zeros((t, 1), F32), _sb_diag_mask(t))

        def step(n, carry):
            run, acc = carry
            run, part = block(i - 1 - n, run, None)
            return run, acc + part

        _, acc = lax.fori_loop(0, i, step, (run, acc))
        o_ref[...] = acc.astype(BF16)

    return pl.pallas_call(
        body, grid=(nh, S // t), name="sb_fwd",
        in_specs=[pl.BlockSpec((t, HEAD), lambda h, i: (i, m0 + h)),
                  pl.BlockSpec((S, HEAD), lambda h, i: (0, m0 + nh + h)),
                  pl.BlockSpec((S, HEAD), lambda h, i: (0, m0 + 2 * nh + h)),
                  pl.BlockSpec(after2.shape, lambda h, i: (0, 0))],
        out_specs=pl.BlockSpec((t, HEAD), lambda h, i: (i, h)),
        out_shape=jax.ShapeDtypeStruct((S, nh * HEAD), BF16),
        compiler_params=_params(("arbitrary", "arbitrary")),
    )(projb, projb, projb, after2)


def sb_backward(dcat, projb, nh, m0, t=512):
    S = projb.shape[0]
    after2, before2 = _sb_constants()
    n_i = S // t
    nsub = t // SB_SUB

    def body(do_ref, q_ref, k_ref, v_ref, af_ref, bf_ref, dq_ref, dk_ref, dv_ref, dbuf, dk_acc, dv_acc):
        i = pl.program_id(1)

        @pl.when(i == 0)
        def _():
            dk_acc[...] = jnp.zeros_like(dk_acc)
            dv_acc[...] = jnp.zeros_like(dv_acc)

        q = q_ref[...]
        dob = do_ref[...].astype(BF16)
        after = af_ref[...]
        before = bf_ref[...]

        def right_to_left(jb, run, mask):
            ksl = pl.ds(pl.multiple_of(jb * t, t), t)
            z = _sb_scores(q, k_ref, jb * t, t)
            w, run = _sb_weights(z, mask, run, after)
            dw = lax.dot_general(dob, v_ref[ksl, :], (((1,), (1,)), ((), ())), preferred_element_type=F32)
            dbuf[jb] = dw * w
            dv_acc[ksl, :] += lax.dot_general(w.astype(BF16), dob, (((0,), (0,)), ((), ())),
                                              preferred_element_type=F32)
            return run

        run = right_to_left(i, jnp.zeros((t, 1), F32), _sb_diag_mask(t))
        lax.fori_loop(0, i, lambda n, run: right_to_left(i - 1 - n, run, None), run)

        def left_to_right(jb, run, dq, mask):
            ksl = pl.ds(pl.multiple_of(jb * t, t), t)
            z = _sb_scores(q, k_ref, jb * t, t)
            d = dbuf[jb]
            sig = 1.0 / (1.0 + jnp.exp2(-z))
            das = []
            for b in range(nsub):
                db = d[:, b * SB_SUB:(b + 1) * SB_SUB]
                prefix = run + jnp.dot(_split2(db), before, preferred_element_type=F32)
                das.append(db - sig[:, b * SB_SUB:(b + 1) * SB_SUB] * (db + prefix))
                run = prefix[:, SB_SUB - 1:SB_SUB] + db[:, SB_SUB - 1:SB_SUB]
            da = jnp.concatenate(das, axis=1)
            if mask is not None:
                da = jnp.where(mask, da, 0.0)
            dab = (da * SB_SCALE).astype(BF16)
            dq = dq + jnp.dot(dab, k_ref[ksl, :], preferred_element_type=F32)
            dk_acc[ksl, :] += lax.dot_general(dab, q, (((0,), (0,)), ((), ())), preferred_element_type=F32)
            return run, dq

        run, dq = lax.fori_loop(0, i, lambda jb, c: left_to_right(jb, c[0], c[1], None),
                                (jnp.zeros((t, 1), F32), jnp.zeros((t, HEAD), F32)))
        _, dq = left_to_right(i, run, dq, _sb_diag_mask(t))
        dq_ref[...] = dq.astype(BF16)

        @pl.when(i == n_i - 1)
        def _():
            dk_ref[...] = (dk_acc[...] * (1.0 / SB_QUERY_SCALE)).astype(BF16)
            dv_ref[...] = dv_acc[...].astype(BF16)

    half = nh * HEAD
    full = pl.BlockSpec((S, HEAD), lambda h, i: (0, h))
    return pl.pallas_call(
        body, grid=(nh, n_i), name="sb_bwd",
        in_specs=[pl.BlockSpec((t, HEAD), lambda h, i: (i, nh + h)),
                  pl.BlockSpec((t, HEAD), lambda h, i: (i, m0 + h)),
                  pl.BlockSpec((S, HEAD), lambda h, i: (0, m0 + nh + h)),
                  pl.BlockSpec((S, HEAD), lambda h, i: (0, m0 + 2 * nh + h)),
                  pl.BlockSpec(after2.shape, lambda h, i: (0, 0)), pl.BlockSpec(before2.shape, lambda h, i: (0, 0))],
        out_specs=[pl.BlockSpec((t, HEAD), lambda h, i: (i, h)), full, full],
        out_shape=[jax.ShapeDtypeStruct((S, half), BF16)] * 3,
        scratch_shapes=[pltpu.VMEM((n_i, t, t), F32), pltpu.VMEM((S, HEAD), F32), pltpu.VMEM((S, HEAD), F32)],
        compiler_params=_params(("arbitrary", "arbitrary")),
    )(dcat, projb, projb, projb, after2, before2)


def local_step(x, target, mix_norm, ffn_norm, final_norm, lb_logits, hg_norm, get_w_in, get_w_rest, send):
    S, D = x.shape
    half = D // 2
    nh = half // HEAD
    tm = 512
    tk = 1024
    row = lambda i, j: (i, 0)

    lb = jax.nn.softmax(lb_logits, axis=0)[0:1]

    h0, r0 = rms_fwd(x, mix_norm[0:1], BF16)
    w_in = get_w_in(h0)
    nbi = w_in.shape[2]
    col = jnp.arange(N_DEV * nbi) // half
    col_scale = jnp.where(col == 4, SB_QUERY_SCALE, 1.0).astype(F32)[None]
    proj, projb = matmul(
        "proj_in", [h0], [w_in], grid=(N_DEV, S // tm, 1),
        a_spec=pl.BlockSpec((tm, D), lambda j, i, k: (i, 0)),
        b_spec=pl.BlockSpec((None, D, nbi), lambda j, i, k: (j, 0, 0)),
        out_spec=pl.BlockSpec((tm, nbi), lambda j, i, k: (i, j)), out_shape=(S, N_DEV * nbi),
        out_dtypes=[F32, BF16], acc_shape=(8, 128),
        bf16_scale=col_scale, bf16_scale_spec=pl.BlockSpec((1, nbi), lambda j, i, k: (0, j)))
    oa, oraw, states = hgrn_forward(proj, lb, hg_norm)
    ob = sb_forward(projb, nh, 4 * nh)
    cat = jnp.concatenate([oa, ob], axis=1)
    w_out, pool_w, pool_scale, wg, wu, wd = get_w_rest(cat)
    (x1,) = matmul(
        "mix_out", [cat], [w_out], grid=(S // tm, 1),
        a_spec=pl.BlockSpec((tm, D), row), b_spec=pl.BlockSpec((D, D), lambda i, k: (0, 0)),
        out_spec=pl.BlockSpec((tm, D), row), out_shape=(S, D), out_dtypes=[F32], acc_shape=(8, 128),
        res=x, res_spec=pl.BlockSpec((tm, D), row))
    h1, r1 = rms_fwd(x1, ffn_norm[0:1], BF16)
    x2, ffn0 = ffn_forward(h1, x1, wg[0], wu[0], wd[0])

    h2, r2 = rms_fwd(x2, mix_norm[1:2], F32)
    x3, pooled = pool_forward(h2, x2, pool_w, pool_scale)
    h3, r3 = rms_fwd(x3, ffn_norm[1:2], BF16)
    x4, ffn1 = ffn_forward(h3, x3, wg[1], wu[1], wd[1])

    loss_blk, dx4, dx4b, d_final = loss_and_final_bwd(x4, final_norm, target)

    dh3, dwg1, dwu1, dwd1 = ffn_backward(dx4b, h3, ffn1, wg[1], wu[1], wd[1])
    dh3 = send("ffn1", dict(ffn_w_gate_1=dwg1, ffn_w_up_1=dwu1, ffn_w_down_1=dwd1), dh3)
    dx3, _, d_ffn1 = rms_bwd(dh3, x3, r3, ffn_norm[1:2], dx4)
    dmixed, dpooled, d_pscale = pool_backward_mix(dx3, pooled, pool_w, pool_scale)
    G = len(POOL_WINDOWS)
    P = D // G
    (d_pool_w,) = matmul(
        "pool_dw", [pooled], [dmixed], grid=(G, S // tk),
        a_spec=pl.BlockSpec((tk, P), lambda g, k: (k, g)), b_spec=pl.BlockSpec((tk, P), lambda g, k: (k, g)),
        out_spec=pl.BlockSpec((None, P, P), lambda g, k: (g, 0, 0)), out_shape=(G, P, P), out_dtypes=[BF16],
        acc_shape=(P, P), trans_a=True)
    dh2 = pool_backward_window(dpooled)
    dx2, dx2b, d_mix1 = rms_bwd(dh2, x2, r2, mix_norm[1:2], dx3)

    dh1, dwg0, dwu0, dwd0 = ffn_backward(dx2b, h1, ffn0, wg[0], wu[0], wd[0])
    dx1, dx1b, d_ffn0 = rms_bwd(dh1, x1, r1, ffn_norm[0:1], dx2)
    (dcat,) = matmul(
        "mix_out_dx", [dx1b], [w_out], grid=(S // tm, 1),
        a_spec=pl.BlockSpec((tm, D), row), b_spec=pl.BlockSpec((D, D), lambda i, k: (0, 0)),
        out_spec=pl.BlockSpec((tm, D), row), out_shape=(S, D), out_dtypes=[F32], acc_shape=(8, 128),
        trans_b=True)
    (d_w_out,) = matmul(
        "mix_out_dw", [cat], [dx1b], grid=(2, S // tk),
        a_spec=pl.BlockSpec((tk, half), lambda m, k: (k, m)), b_spec=pl.BlockSpec((tk, D), lambda m, k: (k, 0)),
        out_spec=pl.BlockSpec((half, D), lambda m, k: (m, 0)), out_shape=(D, D), out_dtypes=[BF16],
        acc_shape=(half, D), trans_a=True)
    dcat = send("layer0", dict(ffn_w_gate_0=dwg0, ffn_w_up_0=dwu0, ffn_w_down_0=dwd0, pool_w=d_pool_w,
                               ab_w_out=d_w_out), dcat)
    dqa, dfa, dia, dga, d_lb, d_hg = hgrn_backward(dcat, proj, oraw, states, lb, hg_norm)
    dqb, dkb, dvb = sb_backward(dcat, projb, nh, 4 * nh)
    dproj = jnp.concatenate([dqa, dfa, dia, dga, dqb, dkb, dvb], axis=1)
    (d_w_in,) = matmul(
        "proj_in_dw", [h0], [dproj], grid=(N_DEV, S // tk),
        a_spec=pl.BlockSpec((tk, D), lambda j, k: (k, 0)), b_spec=pl.BlockSpec((tk, nbi), lambda j, k: (k, j)),
        out_spec=pl.BlockSpec((None, D, nbi), lambda j, k: (j, 0, 0)), out_shape=(N_DEV, D, nbi),
        out_dtypes=[BF16], acc_shape=(D, nbi), trans_a=True)
    dproj = send("w_in", dict(ab_w_in=d_w_in), dproj)
    (dh0,) = matmul(
        "proj_in_dx", [dproj], [w_in], grid=(S // tm, N_DEV),
        a_spec=pl.BlockSpec((tm, nbi), lambda i, j: (i, j)),
        b_spec=pl.BlockSpec((None, D, nbi), lambda i, j: (j, 0, 0)),
        out_spec=pl.BlockSpec((tm, D), row), out_shape=(S, D), out_dtypes=[F32], acc_shape=(tm, D),
        trans_b=True)
    dx0, _, d_mix0 = rms_bwd(dh0, x, r0, mix_norm[0:1], dx1)

    d_l0 = d_lb * lb * (1.0 - lb)
    small = dict(
        loss=loss_blk[0:1, 0:1],
        mix_norm=jnp.concatenate([d_mix0, d_mix1], axis=0),
        ffn_norm=jnp.concatenate([d_ffn0, d_ffn1], axis=0),
        final_norm=d_final,
        lb_logits=jnp.concatenate([d_l0, -d_l0], axis=0),
        hg_out_norm=jnp.sum(d_hg, axis=0),
        pool_scale=d_pscale,
    )
    return dx0, small


def _my_index():
    return 4 * lax.axis_index("x") + 2 * lax.axis_index("y") + lax.axis_index("c")


def _peer(r):
    x, y, c = lax.axis_index("x"), lax.axis_index("y"), lax.axis_index("c")
    px = 1 - x if (r >> 2) & 1 else x
    py = 1 - y if (r >> 1) & 1 else y
    pc = 1 - c if r & 1 else c
    return (px, py, pc), 4 * px + 2 * py + pc


def exchange(name, arrays, gather):
    n = len(arrays)
    n_peers = N_DEV - 1

    def body(*refs):
        ins, outs = refs[:n], refs[n:2 * n]
        send_sems, recv_sems, local_sems = refs[2 * n:]
        me = _my_index()
        local = []
        for a in range(n):
            src = ins[a] if gather else ins[a].at[me]
            cp = pltpu.make_async_copy(src, outs[a].at[me], local_sems.at[a])
            cp.start()
            local.append(cp)
        remote = []
        for a in range(n):
            for r in range(1, N_DEV):
                peer, pidx = _peer(r)
                src = ins[a] if gather else ins[a].at[pidx]
                cp = pltpu.make_async_remote_copy(
                    src_ref=src, dst_ref=outs[a].at[me], send_sem=send_sems.at[a * n_peers + r - 1],
                    recv_sem=recv_sems.at[a * n_peers + r - 1], device_id=peer, device_id_type=MESH)
                cp.start()
                remote.append((cp, a, r))
        for cp, a, r in remote:
            _, pidx = _peer(r)
            src = ins[a] if gather else ins[a].at[pidx]
            pltpu.make_async_remote_copy(
                src_ref=src, dst_ref=outs[a].at[pidx], send_sem=send_sems.at[a * n_peers + r - 1],
                recv_sem=recv_sems.at[a * n_peers + r - 1], device_id=_peer(r)[0], device_id_type=MESH).wait_recv()
        for cp, a, r in remote:
            cp.wait_send()
        for cp in local:
            cp.wait()

    out_shape = [jax.ShapeDtypeStruct(((N_DEV,) + a.shape) if gather else a.shape, a.dtype) for a in arrays]
    any_spec = pl.BlockSpec(memory_space=pl.ANY)
    return pl.pallas_call(
        body, name=name, in_specs=[any_spec] * n, out_specs=[any_spec] * n, out_shape=out_shape,
        scratch_shapes=[pltpu.SemaphoreType.DMA((n * n_peers,)), pltpu.SemaphoreType.DMA((n * n_peers,)),
                        pltpu.SemaphoreType.DMA((n,))],
    )(*arrays)


_HBM = pl.BlockSpec(memory_space=pltpu.HBM)
_SEM = pl.BlockSpec(memory_space=pltpu.SEMAPHORE)
_EFFECT = pltpu.SideEffectType.DATAFLOW_SIDE_EFFECTING


def _landing(arrays, gather):
    me = _my_index()
    lands = []
    for a in arrays:
        own = a[None] if gather else lax.dynamic_slice_in_dim(a, me, 1, axis=0)
        shape = ((N_DEV,) + a.shape) if gather else a.shape
        lands.append(lax.dynamic_update_slice_in_dim(lax.empty(shape, a.dtype), own, me, axis=0))
    return lands


def exchange_start(name, arrays, gather, carry):
    n = len(arrays)
    n_peers = N_DEV - 1
    lands = _landing(arrays, gather)
    n_thru = 2 * n + 1

    def body(*refs):
        src, land = refs[:n], refs[n:2 * n]
        send_sems, recv_sems = refs[n_thru], refs[n_thru + 1]
        token = refs[-1]
        me = _my_index()
        for a in range(n):
            for r in range(1, N_DEV):
                peer, pidx = _peer(r)
                pltpu.make_async_remote_copy(
                    src_ref=src[a] if gather else src[a].at[pidx], dst_ref=land[a].at[me],
                    send_sem=send_sems.at[a * n_peers + r - 1], recv_sem=recv_sems.at[a * n_peers + r - 1],
                    device_id=peer, device_id_type=MESH).start()
        token[...] = jnp.zeros_like(token)

    operands = list(arrays) + lands + [carry]
    outs = pl.pallas_call(
        body, name=name,
        out_shape=(pltpu.SemaphoreType.DMA((n * n_peers,)), pltpu.SemaphoreType.DMA((n * n_peers,)),
                   *[pltpu.HBM(a.shape, a.dtype) for a in operands], jax.ShapeDtypeStruct((8, 128), F32)),
        in_specs=[_HBM] * n_thru,
        out_specs=(_SEM, _SEM, *([_HBM] * n_thru), pl.BlockSpec(memory_space=pltpu.VMEM)),
        input_output_aliases={i: 2 + i for i in range(n_thru)},
        compiler_params=pltpu.CompilerParams(has_side_effects=_EFFECT),
    )(*[pltpu.with_memory_space_constraint(a, pltpu.HBM) for a in operands])
    handle = (outs[0], outs[1], list(outs[2:2 + n]), list(outs[2 + n:2 + 2 * n]), gather)
    return handle, outs[2 + 2 * n]


def exchange_wait(name, handle, after):
    send_sems, recv_sems, srcs, lands, gather = handle
    n = len(srcs)
    n_peers = N_DEV - 1

    def body(*refs):
        src, land = refs[:n], refs[n:2 * n]
        send_s, recv_s = refs[2 * n], refs[2 * n + 1]
        for a in range(n):
            for r in range(1, N_DEV):
                peer, pidx = _peer(r)
                cp = pltpu.make_async_remote_copy(
                    src_ref=src[a] if gather else src[a].at[pidx], dst_ref=land[a].at[pidx],
                    send_sem=send_s.at[a * n_peers + r - 1], recv_sem=recv_s.at[a * n_peers + r - 1],
                    device_id=peer, device_id_type=MESH)
                cp.wait_send()
                cp.wait_recv()

    shapes = [pltpu.HBM(a.shape, a.dtype) for a in srcs] + [pltpu.HBM(l.shape, l.dtype) for l in lands]
    outs = pl.pallas_call(
        body, name=name, out_shape=tuple(shapes),
        in_specs=[_HBM] * (2 * n) + [_SEM, _SEM, pl.BlockSpec(memory_space=pl.ANY)],
        out_specs=tuple([_HBM] * (2 * n)),
        input_output_aliases={i: i for i in range(2 * n)},
        compiler_params=pltpu.CompilerParams(has_side_effects=_EFFECT),
    )(*srcs, *lands, send_sems, recv_sems, after)
    return list(outs[n:])


def _row_tile(rows, cap=256):
    best = None
    for t in range(16, min(rows, cap) + 1, 16):
        if rows % t == 0:
            best = t
    return best if best is not None else rows


def sum_slots(name, recv):
    n, R, C = recv.shape
    tr = _row_tile(R)

    def body(r_ref, o_ref):
        g = r_ref[0].astype(F32)
        for d in range(1, n):
            g = g + r_ref[d].astype(F32)
        o_ref[...] = g

    return pl.pallas_call(
        body, grid=(R // tr,), name=name,
        in_specs=[pl.BlockSpec((n, tr, C), lambda i: (0, i, 0))],
        out_specs=pl.BlockSpec((tr, C), lambda i: (i, 0)),
        out_shape=jax.ShapeDtypeStruct((R, C), F32),
        compiler_params=_params(("arbitrary",)),
    )(recv)


def adamw(name, recv, w, m, v, layer=None, prev=None):
    n, R, C = recv.shape
    tr = _row_tile(R)

    def body(r_ref, w_ref, m_ref, v_ref, *rest):
        g_ref, d_ref, nm_ref, nv_ref = rest[-4:]
        g = r_ref[0].astype(F32)
        for d in range(1, n):
            g = g + r_ref[d].astype(F32)
        mm = ADAM_B1 * m_ref[...] + (1.0 - ADAM_B1) * g
        vv = ADAM_B2 * v_ref[...] + (1.0 - ADAM_B2) * (g * g)
        m_hat = mm / (1.0 - ADAM_B1 ** ADAM_STEP)
        v_hat = vv / (1.0 - ADAM_B2 ** ADAM_STEP)
        g_ref[...] = g
        d_ref[...] = -ADAM_LR * (m_hat / (jnp.sqrt(v_hat) + ADAM_EPS) + ADAM_WD * w_ref[...])
        nm_ref[...] = mm
        nv_ref[...] = vv

    if layer is None:
        row = pl.BlockSpec((tr, C), lambda i: (i, 0))
        shape = (R, C)
    else:
        row = pl.BlockSpec((None, tr, C), lambda i: (layer, i, 0))
        shape = w.shape
    prev = [] if prev is None else list(prev)
    return pl.pallas_call(
        body, grid=(R // tr,), name=name,
        in_specs=[pl.BlockSpec((n, tr, C), lambda i: (0, i, 0)), row, row, row]
                 + [pl.BlockSpec(memory_space=pl.ANY)] * len(prev),
        out_specs=[row] * 4,
        out_shape=[jax.ShapeDtypeStruct(shape, F32)] * 4,
        input_output_aliases={4 + o: o for o in range(len(prev))},
        compiler_params=_params(("arbitrary",)),
    )(recv, w, m, v, *prev)


def _adamw_nd(name, recv, w, m, v):
    shp = w.shape
    C = shp[-1]
    flat = lambda a: a.reshape(-1, C)
    outs = adamw(name, recv.reshape(recv.shape[0], -1, C), flat(w), flat(m), flat(v))
    return [o.reshape(shp) for o in outs]


_SMALL_NAMES = ("loss", "mix_norm", "ffn_norm", "final_norm", "lb_logits", "hg_out_norm", "pool_scale")
_LANES = 128


def _pack_small(parts):
    rows, layout = [], {}
    at = 0
    for name in parts:
        flat = parts[name].reshape(-1).astype(F32)
        n_rows = -(-flat.shape[0] // (8 * _LANES)) * 8
        flat = jnp.pad(flat, (0, n_rows * _LANES - flat.shape[0]))
        rows.append(flat.reshape(n_rows, _LANES))
        layout[name] = (at, parts[name].shape)
        at += n_rows
    return jnp.concatenate(rows, axis=0), layout


def _unpack_small(pack, layout):
    out = {}
    for name, (at, shape) in layout.items():
        size = int(np.prod(shape))
        n_rows = -(-size // _LANES)
        out[name] = pack[at:at + n_rows].reshape(-1)[:size].reshape(shape)
    return out


def kernel(x, mix_norm, ffn_norm, final_norm, ab_w_in, lb_logits, hg_out_norm, ab_w_out, pool_w, pool_scale, ffn_w_gate, ffn_w_up, ffn_w_down, loss_target, m_mix_norm, m_ffn_norm, m_final_norm, m_ab_w_in, m_lb_logits, m_hg_out_norm, m_ab_w_out, m_pool_w, m_pool_scale, m_ffn_w_gate, m_ffn_w_up, m_ffn_w_down, v_mix_norm, v_ffn_norm, v_final_norm, v_ab_w_in, v_lb_logits, v_hg_out_norm, v_ab_w_out, v_pool_w, v_pool_scale, v_ffn_w_gate, v_ffn_w_up, v_ffn_w_down):
    D = x.shape[-1]
    n_layers = ffn_w_gate.shape[0]
    G = pool_w.shape[1]
    P = pool_w.shape[3]
    me = _my_index()

    in_handle, mix_norm_after = exchange_start("gather_w_in_start", [ab_w_in[0].astype(BF16)], True, mix_norm)
    rest = [ab_w_out[0], pool_w[0]]
    for l in range(n_layers):
        rest += [ffn_w_gate[l], ffn_w_up[l], ffn_w_down[l]]
    rest = [s.astype(BF16) for s in rest] + [pool_scale]
    rest_handle = []

    def get_w_in(after):
        w_in = exchange_wait("gather_w_in_wait", in_handle, after)[0]
        handle, w_in = exchange_start("gather_rest_start", rest, True, w_in)
        rest_handle.append(handle)
        return w_in

    def get_w_rest(after):
        got = exchange_wait("gather_rest_wait", rest_handle[0], after)
        w_out_g = got[0].reshape(D, D)
        pool_g = got[1].transpose(1, 0, 2, 3).reshape(G, P, P)
        wg = [got[2 + 3 * l] for l in range(n_layers)]
        wu = [got[3 + 3 * l] for l in range(n_layers)]
        wd = [got[4 + 3 * l] for l in range(n_layers)]
        return w_out_g, pool_g, got[-1].reshape(1, D), wg, wu, wd

    in_flight = []

    def send(tag, grads, carry):
        if "pool_w" in grads:
            grads = dict(grads, pool_w=grads["pool_w"].reshape(G, N_DEV, P // N_DEV, P).transpose(1, 0, 2, 3))
        if "ab_w_out" in grads:
            grads = dict(grads, ab_w_out=grads["ab_w_out"].reshape(N_DEV, D // N_DEV, D))
        handle, carry = exchange_start("grads_" + tag + "_start", list(grads.values()), False, carry)
        in_flight.append((tag, list(grads.keys()), handle))
        return carry

    dx0, small = local_step(x[0], loss_target[0], mix_norm_after, ffn_norm, final_norm[None],
                            lb_logits, hg_out_norm, get_w_in, get_w_rest, send)

    recv = {}
    for tag, names, handle in in_flight:
        recv.update(zip(names, exchange_wait("grads_" + tag + "_wait", handle, dx0)))
    small_pack, layout = _pack_small({k: small[k] for k in _SMALL_NAMES})
    (small_all,) = exchange("gather_small", [small_pack], gather=True)
    tot = _unpack_small(sum_slots("sum_small", small_all), layout)

    res = {}
    res["ab_w_in"] = _adamw_nd("adamw_w_in", recv["ab_w_in"], ab_w_in, m_ab_w_in, v_ab_w_in)
    res["ab_w_out"] = _adamw_nd("adamw_w_out", recv["ab_w_out"], ab_w_out, m_ab_w_out, v_ab_w_out)
    res["pool_w"] = _adamw_nd("adamw_pool_w", recv["pool_w"], pool_w, m_pool_w, v_pool_w)
    ffn_in = {"ffn_w_gate": (ffn_w_gate, m_ffn_w_gate, v_ffn_w_gate),
              "ffn_w_up": (ffn_w_up, m_ffn_w_up, v_ffn_w_up),
              "ffn_w_down": (ffn_w_down, m_ffn_w_down, v_ffn_w_down)}
    for name, (w, m, v) in ffn_in.items():
        outs = None
        for l in range(n_layers):
            outs = adamw("adamw_" + name, recv[name + "_" + str(l)], w, m, v, layer=l, prev=outs)
        res[name] = outs

    n_ps = pool_scale.shape[1]
    small_g = dict(tot)
    small_g["pool_scale"] = lax.dynamic_slice(tot["pool_scale"], (0, me * n_ps), (1, n_ps))
    small_w = dict(mix_norm=(mix_norm, m_mix_norm, v_mix_norm), ffn_norm=(ffn_norm, m_ffn_norm, v_ffn_norm),
                   final_norm=(final_norm, m_final_norm, v_final_norm),
                   lb_logits=(lb_logits, m_lb_logits, v_lb_logits),
                   hg_out_norm=(hg_out_norm, m_hg_out_norm, v_hg_out_norm),
                   pool_scale=(pool_scale, m_pool_scale, v_pool_scale))
    g_pack, lay2 = _pack_small({k: small_g[k].reshape(small_w[k][0].shape) for k in small_w})
    w_pack, _ = _pack_small({k: small_w[k][0] for k in small_w})
    m_pack, _ = _pack_small({k: small_w[k][1] for k in small_w})
    v_pack, _ = _pack_small({k: small_w[k][2] for k in small_w})
    small_out = [_unpack_small(o, lay2) for o in adamw("adamw_small", g_pack[None], w_pack, m_pack, v_pack)]
    for k in small_w:
        res[k] = [small_out[o][k] for o in range(4)]

    order = ("mix_norm", "ffn_norm", "final_norm", "ab_w_in", "lb_logits", "hg_out_norm", "ab_w_out", "pool_w",
             "pool_scale", "ffn_w_gate", "ffn_w_up", "ffn_w_down")
    outs = [tot["loss"].reshape(()), dx0[None]]
    for o in range(4):
        outs += [res[k][o] for k in order]
    return tuple(outs)
```

```python
import functools
import math

import numpy as np
import jax
import jax.numpy as jnp
from jax import lax
from jax.experimental import pallas as pl
from jax.experimental.pallas import tpu as pltpu

F32 = jnp.float32
BF16 = jnp.bfloat16

N_DEV = 8
RMS_EPS = 1e-6
HEAD = 128
HG_CHUNK = 64
HG_HEADS_PER_BLOCK = 4
POOL_WINDOWS = (2, 4, 8, 16)
POOL_HALO = 16
ADAM_LR, ADAM_B1, ADAM_B2, ADAM_EPS, ADAM_WD, ADAM_STEP = 0.001, 0.9, 0.999, 1e-08, 0.01, 10
VMEM_LIMIT_BYTES = 60 * 1024 * 1024
MESH = pl.DeviceIdType.MESH


def _params(sem):
    return pltpu.CompilerParams(dimension_semantics=sem, vmem_limit_bytes=VMEM_LIMIT_BYTES)


def _sigmoid(x):
    return 1.0 / (1.0 + jnp.exp(-x))


def rms_fwd(x, gain, out_dtype, ts=512):
    S, D = x.shape

    def body(x_ref, g_ref, h_ref, r_ref):
        xv = x_ref[...]
        r = lax.rsqrt(jnp.mean(xv * xv, axis=-1, keepdims=True) + RMS_EPS)
        h_ref[...] = ((xv * r) * g_ref[...]).astype(h_ref.dtype)
        r_ref[...] = r

    return pl.pallas_call(
        body, grid=(S // ts,), name="rms_fwd",
        in_specs=[pl.BlockSpec((ts, D), lambda i: (i, 0)), pl.BlockSpec((1, D), lambda i: (0, 0))],
        out_specs=[pl.BlockSpec((ts, D), lambda i: (i, 0)), pl.BlockSpec((ts, 1), lambda i: (i, 0))],
        out_shape=[jax.ShapeDtypeStruct((S, D), out_dtype), jax.ShapeDtypeStruct((S, 1), F32)],
        compiler_params=_params(("arbitrary",)),
    )(x, gain)


def rms_bwd(dh, x, r, gain, dres, ts=512):
    S, D = x.shape

    def body(dh_ref, x_ref, r_ref, g_ref, dres_ref, dx_ref, dxb_ref, dg_ref):
        i = pl.program_id(0)
        rr = r_ref[...]
        xh = x_ref[...] * rr
        dhv = dh_ref[...]
        dxh = dhv * g_ref[...]
        dx = dres_ref[...] + rr * (dxh - xh * jnp.mean(dxh * xh, axis=-1, keepdims=True))
        dx_ref[...] = dx
        dxb_ref[...] = dx.astype(BF16)
        part = jnp.sum(dhv * xh, axis=0, keepdims=True)

        @pl.when(i == 0)
        def _():
            dg_ref[...] = part

        @pl.when(i > 0)
        def _():
            dg_ref[...] += part

    row = pl.BlockSpec((ts, D), lambda i: (i, 0))
    vec = pl.BlockSpec((1, D), lambda i: (0, 0))
    return pl.pallas_call(
        body, grid=(S // ts,), name="rms_bwd",
        in_specs=[row, row, pl.BlockSpec((ts, 1), lambda i: (i, 0)), vec, row],
        out_specs=[row, row, vec],
        out_shape=[jax.ShapeDtypeStruct((S, D), F32), jax.ShapeDtypeStruct((S, D), BF16),
                   jax.ShapeDtypeStruct((1, D), F32)],
        compiler_params=_params(("arbitrary",)),
    )(dh, x, r, gain, dres)


def loss_and_final_bwd(x, gain, target, ts=512):
    S, D = x.shape

    def body(x_ref, g_ref, t_ref, loss_ref, dx_ref, dxb_ref, dg_ref):
        i = pl.program_id(0)
        xv = x_ref[...]
        rr = lax.rsqrt(jnp.mean(xv * xv, axis=-1, keepdims=True) + RMS_EPS)
        xh = xv * rr
        err = xh * g_ref[...] - t_ref[...]
        part_loss = 0.5 * jnp.sum(jnp.mean(err * err, axis=-1, keepdims=True))
        dy = err / D
        dxh = dy * g_ref[...]
        dx = rr * (dxh - xh * jnp.mean(dxh * xh, axis=-1, keepdims=True))
        dx_ref[...] = dx
        dxb_ref[...] = dx.astype(BF16)
        part = jnp.sum(dy * xh, axis=0, keepdims=True)

        @pl.when(i == 0)
        def _():
            dg_ref[...] = part
            loss_ref[...] = jnp.zeros_like(loss_ref) + part_loss

        @pl.when(i > 0)
        def _():
            dg_ref[...] += part
            loss_ref[...] += part_loss

    row = pl.BlockSpec((ts, D), lambda i: (i, 0))
    vec = pl.BlockSpec((1, D), lambda i: (0, 0))
    return pl.pallas_call(
        body, grid=(S // ts,), name="loss_final",
        in_specs=[row, vec, row],
        out_specs=[pl.BlockSpec((8, 128), lambda i: (0, 0)), row, row, vec],
        out_shape=[jax.ShapeDtypeStruct((8, 128), F32), jax.ShapeDtypeStruct((S, D), F32),
                   jax.ShapeDtypeStruct((S, D), BF16), jax.ShapeDtypeStruct((1, D), F32)],
        compiler_params=_params(("arbitrary",)),
    )(x, gain, target)


def matmul(name, a_ops, b_ops, *, grid, a_spec, b_spec, out_spec, out_shape, out_dtypes, acc_shape,
           trans_a=False, trans_b=False, res=None, res_spec=None, bf16_scale=None, bf16_scale_spec=None):
    n_pairs = len(a_ops)
    n_out = len(out_dtypes)
    nk = grid[-1]
    kaxis = len(grid) - 1
    dn = (((0,) if trans_a else (1,), (1,) if trans_b else (0,)), ((), ()))

    def body(*refs):
        a_refs = refs[:n_pairs]
        b_refs = refs[n_pairs:2 * n_pairs]
        pos = 2 * n_pairs
        res_ref = None
        if res is not None:
            res_ref = refs[pos]
            pos += 1
        scale_ref = None
        if bf16_scale is not None:
            scale_ref = refs[pos]
            pos += 1
        out_refs = refs[pos:pos + n_out]
        acc_ref = refs[pos + n_out]
        k = pl.program_id(kaxis)
        in_place = n_out == 1 and out_dtypes[0] == F32
        target = out_refs[0] if in_place else acc_ref

        def finish(val):
            if res_ref is not None:
                val = val + res_ref[...]
            for o in out_refs:
                if scale_ref is not None and o.dtype == BF16:
                    o[...] = (val * scale_ref[...]).astype(BF16)
                else:
                    o[...] = val.astype(o.dtype)

        if nk > 1:
            @pl.when(k == 0)
            def _():
                if in_place and res_ref is not None:
                    target[...] = res_ref[...]
                else:
                    target[...] = jnp.zeros_like(target)

        part = None
        for ar, br in zip(a_refs, b_refs):
            d = lax.dot_general(ar[...].astype(BF16), br[...].astype(BF16), dn, preferred_element_type=F32)
            part = d if part is None else part + d

        if nk == 1:
            finish(part)
        else:
            target[...] += part
            if not in_place:
                @pl.when(k == nk - 1)
                def _():
                    finish(acc_ref[...])

    in_specs = [a_spec] * n_pairs + [b_spec] * n_pairs
    operands = list(a_ops) + list(b_ops)
    if res is not None:
        in_specs.append(res_spec)
        operands.append(res)
    if bf16_scale is not None:
        in_specs.append(bf16_scale_spec)
        operands.append(bf16_scale)
    return pl.pallas_call(
        body, grid=grid, name=name, in_specs=in_specs,
        out_specs=[out_spec] * n_out,
        out_shape=[jax.ShapeDtypeStruct(out_shape, dt) for dt in out_dtypes],
        scratch_shapes=[pltpu.VMEM(acc_shape, F32)],
        compiler_params=_params(("arbitrary",) * len(grid)),
    )(*operands)


def ffn_gate_up(h, wg, wu, tm=1024):
    S, D = h.shape
    nb = wg.shape[2]

    def body(h_ref, wg_ref, wu_ref, p_ref, r_ref, a_ref):
        for c in range(2):
            rows = slice(c * (tm // 2), (c + 1) * (tm // 2))
            hv = h_ref[rows, :]
            g = jnp.dot(hv, wg_ref[...], preferred_element_type=F32)
            u = jnp.dot(hv, wu_ref[...], preferred_element_type=F32)
            s = _sigmoid(g)
            p = g * s
            p_ref[rows, :] = p
            r_ref[rows, :] = u * (s * (1.0 + g * (1.0 - s)))
            a_ref[rows, :] = (p * u).astype(BF16)

    wspec = pl.BlockSpec((None, D, nb), lambda j, i: (j, 0, 0))
    ospec = pl.BlockSpec((None, tm, nb), lambda j, i: (j, i, 0))
    return pl.pallas_call(
        body, grid=(N_DEV, S // tm), name="ffn_gate_up",
        in_specs=[pl.BlockSpec((tm, D), lambda j, i: (i, 0)), wspec, wspec],
        out_specs=[ospec, ospec, ospec],
        out_shape=[jax.ShapeDtypeStruct((N_DEV, S, nb), F32), jax.ShapeDtypeStruct((N_DEV, S, nb), F32),
                   jax.ShapeDtypeStruct((N_DEV, S, nb), BF16)],
        compiler_params=_params(("arbitrary", "arbitrary")),
    )(h, wg, wu)


def ffn_bwd_hidden(dy, wd, p, r, tm=1024):
    S, D = dy.shape
    nb = wd.shape[1]

    def body(dy_ref, wd_ref, p_ref, r_ref, dg_ref, du_ref):
        for c in range(2):
            rows = slice(c * (tm // 2), (c + 1) * (tm // 2))
            da = lax.dot_general(dy_ref[rows, :], wd_ref[...], (((1,), (1,)), ((), ())),
                                 preferred_element_type=F32)
            du_ref[rows, :] = (da * p_ref[rows, :]).astype(BF16)
            dg_ref[rows, :] = (da * r_ref[rows, :]).astype(BF16)

    hspec = pl.BlockSpec((None, tm, nb), lambda j, i: (j, i, 0))
    return pl.pallas_call(
        body, grid=(N_DEV, S // tm), name="ffn_bwd_hidden",
        in_specs=[pl.BlockSpec((tm, D), lambda j, i: (i, 0)), pl.BlockSpec((None, nb, D), lambda j, i: (j, 0, 0)),
                  hspec, hspec],
        out_specs=[hspec, hspec],
        out_shape=[jax.ShapeDtypeStruct((N_DEV, S, nb), BF16), jax.ShapeDtypeStruct((N_DEV, S, nb), BF16)],
        compiler_params=_params(("arbitrary", "arbitrary")),
    )(dy, wd, p, r)


def ffn_forward(h, xres, wg, wu, wd, tm=512):
    S, D = h.shape
    nb = wg.shape[2]
    g, u, a = ffn_gate_up(h, wg, wu)
    (xo,) = matmul(
        "ffn_down", [a], [wd], grid=(S // tm, N_DEV),
        a_spec=pl.BlockSpec((None, tm, nb), lambda i, j: (j, i, 0)),
        b_spec=pl.BlockSpec((None, nb, D), lambda i, j: (j, 0, 0)),
        out_spec=pl.BlockSpec((tm, D), lambda i, j: (i, 0)), out_shape=(S, D), out_dtypes=[F32],
        acc_shape=(tm, D), res=xres, res_spec=pl.BlockSpec((tm, D), lambda i, j: (i, 0)))
    return xo, (g, u, a)


def ffn_backward(dy_b, h, saved, wg, wu, wd, tm=512, tk=1024):
    S, D = h.shape
    nb = wg.shape[2]
    g, u, a = saved
    dg, du = ffn_bwd_hidden(dy_b, wd, g, u)
    (dh,) = matmul(
        "ffn_dh", [dg, du], [wg, wu], grid=(S // tm, N_DEV),
        a_spec=pl.BlockSpec((None, tm, nb), lambda i, j: (j, i, 0)),
        b_spec=pl.BlockSpec((None, D, nb), lambda i, j: (j, 0, 0)),
        out_spec=pl.BlockSpec((tm, D), lambda i, j: (i, 0)), out_shape=(S, D), out_dtypes=[F32],
        acc_shape=(tm, D), trans_b=True)

    def wgrad_in(name, dhid):
        (dw,) = matmul(
            name, [h], [dhid], grid=(N_DEV, S // tk),
            a_spec=pl.BlockSpec((tk, D), lambda j, k: (k, 0)),
            b_spec=pl.BlockSpec((None, tk, nb), lambda j, k: (j, k, 0)),
            out_spec=pl.BlockSpec((None, D, nb), lambda j, k: (j, 0, 0)), out_shape=(N_DEV, D, nb),
            out_dtypes=[BF16], acc_shape=(D, nb), trans_a=True)
        return dw

    dwg = wgrad_in("ffn_dwg", dg)
    dwu = wgrad_in("ffn_dwu", du)
    (dwd,) = matmul(
        "ffn_dwd", [a], [dy_b], grid=(N_DEV, S // tk),
        a_spec=pl.BlockSpec((None, tk, nb), lambda j, k: (j, k, 0)),
        b_spec=pl.BlockSpec((tk, D), lambda j, k: (k, 0)),
        out_spec=pl.BlockSpec((None, nb, D), lambda j, k: (j, 0, 0)), out_shape=(N_DEV, nb, D),
        out_dtypes=[BF16], acc_shape=(nb, D), trans_a=True)
    return dh, dwg, dwu, dwd


def _pool_counts(row0, n, w):
    pos = row0 + lax.broadcasted_iota(jnp.int32, (n, 1), 0)
    return jnp.minimum(pos + 1, w).astype(F32)


def pool_forward(h, xres, w, scale, ts=256):
    S, D = h.shape
    G = len(POOL_WINDOWS)
    P = D // G
    hb = ts // POOL_HALO

    def body(h_ref, halo_ref, x_ref, w_ref, s_ref, xo_ref, p_ref):
        i = pl.program_id(0)
        for gi, win in enumerate(POOL_WINDOWS):
            cols = slice(gi * P, (gi + 1) * P)
            cur = h_ref[:, cols]
            halo = jnp.where(i > 0, halo_ref[:, cols], 0.0)
            acc = jnp.concatenate([halo, cur], axis=0)
            step = 1
            while step < win:
                acc = acc + pltpu.roll(acc, step, 0)
                step *= 2
            wsum = acc[POOL_HALO:, :]
            pooled = wsum / _pool_counts(i * ts, ts, win) - cur
            pb = pooled.astype(BF16)
            p_ref[:, cols] = pb
            mixed = jnp.dot(pb, w_ref[gi], preferred_element_type=F32)
            xo_ref[:, cols] = x_ref[:, cols] + mixed * s_ref[:, cols]

    row = pl.BlockSpec((ts, D), lambda i: (i, 0))
    return pl.pallas_call(
        body, grid=(S // ts,), name="pool_fwd",
        in_specs=[row, pl.BlockSpec((POOL_HALO, D), lambda i: (jnp.maximum(i * hb - 1, 0), 0)), row,
                  pl.BlockSpec((G, P, P), lambda i: (0, 0, 0)), pl.BlockSpec((1, D), lambda i: (0, 0))],
        out_specs=[row, row],
        out_shape=[jax.ShapeDtypeStruct((S, D), F32), jax.ShapeDtypeStruct((S, D), BF16)],
        compiler_params=_params(("arbitrary",)),
    )(h, h, xres, w, scale)


def pool_backward_mix(dx, pooled, w, scale, ts=256):
    S, D = dx.shape
    G = len(POOL_WINDOWS)
    P = D // G

    def body(dx_ref, p_ref, w_ref, s_ref, dm_ref, dp_ref, ds_ref):
        i = pl.program_id(0)
        parts = []
        for gi in range(G):
            cols = slice(gi * P, (gi + 1) * P)
            dxv = dx_ref[:, cols]
            dmb = (dxv * s_ref[:, cols]).astype(BF16)
            dm_ref[:, cols] = dmb
            dp_ref[:, cols] = lax.dot_general(dmb, w_ref[gi], (((1,), (1,)), ((), ())),
                                              preferred_element_type=F32)
            mixed = jnp.dot(p_ref[:, cols], w_ref[gi], preferred_element_type=F32)
            parts.append(jnp.sum(dxv * mixed, axis=0, keepdims=True))
        part = jnp.concatenate(parts, axis=1)

        @pl.when(i == 0)
        def _():
            ds_ref[...] = part

        @pl.when(i > 0)
        def _():
            ds_ref[...] += part

    row = pl.BlockSpec((ts, D), lambda i: (i, 0))
    vec = pl.BlockSpec((1, D), lambda i: (0, 0))
    return pl.pallas_call(
        body, grid=(S // ts,), name="pool_bwd_mix",
        in_specs=[row, row, pl.BlockSpec((G, P, P), lambda i: (0, 0, 0)), vec],
        out_specs=[row, row, vec],
        out_shape=[jax.ShapeDtypeStruct((S, D), BF16), jax.ShapeDtypeStruct((S, D), F32),
                   jax.ShapeDtypeStruct((1, D), F32)],
        compiler_params=_params(("arbitrary",)),
    )(dx, pooled, w, scale)


def pool_backward_window(dp, ts=256):
    S, D = dp.shape
    G = len(POOL_WINDOWS)
    P = D // G
    hb = ts // POOL_HALO
    n_i = S // ts
    n_rows = ts + POOL_HALO

    def body(dp_ref, halo_ref, dh_ref):
        i = pl.program_id(0)
        for gi, win in enumerate(POOL_WINDOWS):
            cols = slice(gi * P, (gi + 1) * P)
            cur = dp_ref[:, cols]
            halo = jnp.where(i < n_i - 1, halo_ref[:, cols], 0.0)
            acc = jnp.concatenate([cur / _pool_counts(i * ts, ts, win),
                                   halo / _pool_counts((i + 1) * ts, POOL_HALO, win)], axis=0)
            step = 1
            while step < win:
                acc = acc + pltpu.roll(acc, n_rows - step, 0)
                step *= 2
            dh_ref[:, cols] = acc[:ts, :] - cur

    row = pl.BlockSpec((ts, D), lambda i: (i, 0))
    return pl.pallas_call(
        body, grid=(n_i,), name="pool_bwd_window",
        in_specs=[row, pl.BlockSpec((POOL_HALO, D), lambda i: (jnp.minimum((i + 1) * hb, S // POOL_HALO - 1), 0))],
        out_specs=row,
        out_shape=jax.ShapeDtypeStruct((S, D), F32),
        compiler_params=_params(("arbitrary",)),
    )(dp, dp)


_HG_LEVELS = (32, 16, 8, 4, 2, 1)
_N_LEV = len(_HG_LEVELS) + 1


def _hgrn_constants():
    C = HG_CHUNK
    t = np.arange(C)
    tri = (t[None, :] <= t[:, None]).astype(np.float32)
    blocks = [tri]
    masks, upq, upk = [], [], []
    for m in _HG_LEVELS:
        p = (t // (2 * m)) * 2 * m + m - 1
        blocks.append(tri[p])
        masks.append(((t[:, None] // (2 * m)) == (t[None, :] // (2 * m))).astype(np.float32))
        upper = (t % (2 * m)) >= m
        upq.append(np.repeat(upper[:, None], HEAD, 1).astype(np.float32))
        upk.append(np.repeat(~upper[:, None], HEAD, 1).astype(np.float32))
    blocks.append(tri)
    masks.append(np.eye(C, dtype=np.float32))
    upq.append(np.ones((C, HEAD), np.float32))
    upk.append(np.ones((C, HEAD), np.float32))
    mstack = np.concatenate(blocks, axis=0)
    mstack3 = np.concatenate([mstack] * 3, axis=1)
    trirev3 = np.concatenate([tri.T] * 3, axis=1)
    return (jnp.asarray(mstack3, BF16), jnp.asarray(np.stack(masks)), jnp.asarray(np.stack(upq)),
            jnp.asarray(np.stack(upk)), jnp.asarray(trirev3, BF16))


def _split3(x):
    hi = x.astype(BF16)
    r1 = x - hi.astype(F32)
    mid = r1.astype(BF16)
    lo = (r1 - mid.astype(F32)).astype(BF16)
    return jnp.concatenate([hi, mid, lo], axis=0)


def _hgrn_chunk_common(qa, fa, lbv, mstack3, upq, upk):
    sq = _sigmoid(qa)
    q = qa * sq
    sf = _sigmoid(fa)
    f = lbv + (1.0 - lbv) * sf
    g = jnp.log(f)
    k = 1.0 - f
    gall = jnp.dot(mstack3, _split3(g), preferred_element_type=F32).reshape(_N_LEV + 1, HG_CHUNK, HEAD)
    G = gall[0]
    eq_exp = G[None] - gall[1:]
    eq = jnp.exp(jnp.minimum(eq_exp, 0.0)) * upq
    ek = jnp.exp(jnp.minimum(-eq_exp, 0.0)) * upk
    Qs = (q[None] * eq).astype(BF16)
    Ks = (k[None] * ek).astype(BF16)
    return sq, q, sf, f, k, G, eq, ek, Qs, Ks


def hgrn_forward(proj, lb, hg_norm, ts=512):
    S = proj.shape[0]
    nh = lb.shape[1] // HEAD
    C = HG_CHUNK
    ncs = ts // C
    mstack3, masks, upq, upk, _ = _hgrn_constants()

    def body(qa_ref, fa_ref, ia_ref, ga_ref, lb_ref, gn_ref, ms_ref, mk_ref, uq_ref, uk_ref,
             oa_ref, oraw_ref, st_ref, state):
        tt = pl.program_id(1)

        @pl.when(tt == 0)
        def _():
            state[...] = jnp.zeros_like(state)

        gn = gn_ref[...]

        def chunk(c, carry):
            sl = pl.ds(pl.multiple_of(c * C, C), C)
            for hh in range(HG_HEADS_PER_BLOCK):
                cols = slice(hh * HEAD, (hh + 1) * HEAD)
                qa, fa, v, ga = qa_ref[sl, cols], fa_ref[sl, cols], ia_ref[sl, cols], ga_ref[sl, cols]
                _, q, _, _, k, G, _, _, Qs, Ks = _hgrn_chunk_common(qa, fa, lb_ref[:, cols], ms_ref[...],
                                                                    uq_ref[...], uk_ref[...])
                att7 = lax.dot_general(Qs, Ks, (((2,), (2,)), ((0,), (0,))), preferred_element_type=F32)
                att = jnp.sum(att7 * mk_ref[...], axis=0)
                st = state[hh]
                st_ref[hh, c] = st
                vb = v.astype(BF16)
                qg = (q * jnp.exp(G)).astype(BF16)
                o = jnp.dot(att.astype(BF16), vb, preferred_element_type=F32)
                o = o + lax.dot_general(qg, st.astype(BF16), (((1,), (1,)), ((), ())),
                                        preferred_element_type=F32)
                g_last = G[C - 1:C, :]
                kh = (k * jnp.exp(g_last - G)).astype(BF16)
                state[hh] = st * jnp.exp(g_last) + lax.dot_general(vb, kh, (((0,), (0,)), ((), ())),
                                                                   preferred_element_type=F32)
                oraw_ref[sl, cols] = o
                r = lax.rsqrt(jnp.mean(o * o, axis=-1, keepdims=True) + RMS_EPS)
                oa_ref[sl, cols] = (((o * r) * gn) * (ga * _sigmoid(ga))).astype(BF16)
            return carry

        lax.fori_loop(0, ncs, chunk, 0)

    hpb = HG_HEADS_PER_BLOCK
    wide = hpb * HEAD

    def col(m0):
        return pl.BlockSpec((ts, wide), lambda h, t: (t, m0 // hpb + h))

    const3 = lambda shape: pl.BlockSpec(shape, lambda h, t: (0, 0, 0))
    return pl.pallas_call(
        body, grid=(nh // hpb, S // ts), name="hgrn_fwd",
        in_specs=[col(0), col(nh), col(2 * nh), col(3 * nh),
                  pl.BlockSpec((1, wide), lambda h, t: (0, h)), pl.BlockSpec((1, HEAD), lambda h, t: (0, 0)),
                  pl.BlockSpec(mstack3.shape, lambda h, t: (0, 0)), const3(masks.shape), const3(upq.shape),
                  const3(upk.shape)],
        out_specs=[pl.BlockSpec((ts, wide), lambda h, t: (t, h)), pl.BlockSpec((ts, wide), lambda h, t: (t, h)),
                   pl.BlockSpec((hpb, ncs, HEAD, HEAD), lambda h, t: (h, t, 0, 0))],
        out_shape=[jax.ShapeDtypeStruct((S, nh * HEAD), BF16), jax.ShapeDtypeStruct((S, nh * HEAD), F32),
                   jax.ShapeDtypeStruct((nh, S // C, HEAD, HEAD), F32)],
        scratch_shapes=[pltpu.VMEM((hpb, HEAD, HEAD), F32)],
        compiler_params=_params(("arbitrary", "arbitrary")),
    )(proj, proj, proj, proj, lb, hg_norm, mstack3, masks, upq, upk)


def hgrn_backward(dcat, proj, oraw, states, lb, hg_norm, ts=512):
    S = proj.shape[0]
    nh = lb.shape[1] // HEAD
    C = HG_CHUNK
    ncs = ts // C
    nt = S // ts
    mstack3, masks, upq, upk, trirev3 = _hgrn_constants()

    def body(do_ref, qa_ref, fa_ref, ia_ref, ga_ref, or_ref, st_ref, lb_ref, gn_ref, ms_ref, mk_ref, uq_ref,
             uk_ref, tr_ref, dqa_ref, dfa_ref, dia_ref, dga_ref, dlb_ref, dgn_ref, dstate):
        tt = pl.program_id(1)

        @pl.when(tt == 0)
        def _():
            dstate[...] = jnp.zeros_like(dstate)
            dlb_ref[...] = jnp.zeros_like(dlb_ref)
            dgn_ref[...] = jnp.zeros_like(dgn_ref)

        gn = gn_ref[...]

        def chunk(cc, carry):
            c = ncs - 1 - cc
            sl = pl.ds(pl.multiple_of(c * C, C), C)
            for hh in range(HG_HEADS_PER_BLOCK):
                cols = slice(hh * HEAD, (hh + 1) * HEAD)
                lbv = lb_ref[:, cols]
                qa, fa, v, ga = qa_ref[sl, cols], fa_ref[sl, cols], ia_ref[sl, cols], ga_ref[sl, cols]
                sq, q, sf, f, k, G, eq, ek, Qs, Ks = _hgrn_chunk_common(qa, fa, lbv, ms_ref[...], uq_ref[...],
                                                                        uk_ref[...])
                mk = mk_ref[...]
                att7 = lax.dot_general(Qs, Ks, (((2,), (2,)), ((0,), (0,))), preferred_element_type=F32)
                att = jnp.sum(att7 * mk, axis=0)
                o = or_ref[sl, cols]
                dO = do_ref[sl, cols]
                sg = _sigmoid(ga)
                r = lax.rsqrt(jnp.mean(o * o, axis=-1, keepdims=True) + RMS_EPS)
                xh = o * r
                dga_ref[sl, cols] = (dO * (xh * gn) * (sg * (1.0 + ga * (1.0 - sg)))).astype(BF16)
                don = dO * (ga * sg)
                dgn_ref[hh] += jnp.sum(don * xh, axis=0, keepdims=True)
                dxh = don * gn
                do = r * (dxh - xh * jnp.mean(dxh * xh, axis=-1, keepdims=True))
                dob = do.astype(BF16)
                st = st_ref[hh, c]
                dst = dstate[hh]
                dstb = dst.astype(BF16)
                vb = v.astype(BF16)
                eG = jnp.exp(G)
                g_last = G[C - 1:C, :]
                e_last = jnp.exp(g_last)
                e_tail = jnp.exp(g_last - G)
                qg = (q * eG).astype(BF16)
                kh = (k * e_tail).astype(BF16)
                dq_inter = jnp.dot(dob, st.astype(BF16), preferred_element_type=F32) * eG
                dk_inter = jnp.dot(vb, dstb, preferred_element_type=F32) * e_tail
                dv = lax.dot_general(kh, dstb, (((1,), (1,)), ((), ())), preferred_element_type=F32)
                dv = dv + lax.dot_general(att.astype(BF16), dob, (((0,), (0,)), ((), ())),
                                          preferred_element_type=F32)
                dA = lax.dot_general(dob, vb, (((1,), (1,)), ((), ())), preferred_element_type=F32)
                dA7 = (dA[None] * mk).astype(BF16)
                dAT7 = (dA.T[None] * mk).astype(BF16)
                dQs = lax.dot_general(dA7, Ks, (((2,), (1,)), ((0,), (0,))), preferred_element_type=F32)
                dKs = lax.dot_general(dAT7, Qs, (((2,), (1,)), ((0,), (0,))), preferred_element_type=F32)
                dq = dq_inter + jnp.sum(dQs * eq, axis=0)
                dk = dk_inter + jnp.sum(dKs * ek, axis=0)
                dG = (jnp.sum(Qs.astype(F32) * dQs - Ks.astype(F32) * dKs, axis=0)
                      + q * dq_inter - k * dk_inter)
                last_extra = (jnp.sum(k * dk_inter, axis=0, keepdims=True)
                              + e_last * jnp.sum(dst * st, axis=0, keepdims=True))
                is_last = lax.broadcasted_iota(jnp.int32, (C, 1), 0) == C - 1
                dG = dG + jnp.where(is_last, last_extra, 0.0)
                dg = jnp.dot(tr_ref[...], _split3(dG), preferred_element_type=F32)
                df = dg / f - dk
                dfa_ref[sl, cols] = (df * (1.0 - lbv) * (sf * (1.0 - sf))).astype(BF16)
                dlb_ref[:, cols] += jnp.sum(df * (1.0 - sf), axis=0, keepdims=True)
                dqa_ref[sl, cols] = (dq * (sq * (1.0 + qa * (1.0 - sq)))).astype(BF16)
                dia_ref[sl, cols] = dv.astype(BF16)
                dstate[hh] = dst * e_last + lax.dot_general(dob, qg, (((0,), (0,)), ((), ())),
                                                            preferred_element_type=F32)
            return carry

        lax.fori_loop(0, ncs, chunk, 0)

    hpb = HG_HEADS_PER_BLOCK
    wide = hpb * HEAD

    def col(m0):
        return pl.BlockSpec((ts, wide), lambda h, t: (nt - 1 - t, m0 // hpb + h))

    const3 = lambda shape: pl.BlockSpec(shape, lambda h, t: (0, 0, 0))
    const2 = lambda shape: pl.BlockSpec(shape, lambda h, t: (0, 0))
    ocol = pl.BlockSpec((ts, wide), lambda h, t: (nt - 1 - t, h))
    half = nh * HEAD
    return pl.pallas_call(
        body, grid=(nh // hpb, nt), name="hgrn_bwd",
        in_specs=[col(0), col(0), col(nh), col(2 * nh), col(3 * nh), col(0),
                  pl.BlockSpec((hpb, ncs, HEAD, HEAD), lambda h, t: (h, nt - 1 - t, 0, 0)),
                  pl.BlockSpec((1, wide), lambda h, t: (0, h)), const2((1, HEAD)),
                  const2(mstack3.shape), const3(masks.shape), const3(upq.shape), const3(upk.shape),
                  const2(trirev3.shape)],
        out_specs=[ocol, ocol, ocol, ocol, pl.BlockSpec((1, wide), lambda h, t: (0, h)),
                   pl.BlockSpec((hpb, 1, HEAD), lambda h, t: (h, 0, 0))],
        out_shape=[jax.ShapeDtypeStruct((S, half), BF16)] * 4
                  + [jax.ShapeDtypeStruct((1, half), F32), jax.ShapeDtypeStruct((nh, 1, HEAD), F32)],
        scratch_shapes=[pltpu.VMEM((hpb, HEAD, HEAD), F32)],
        compiler_params=_params(("arbitrary", "arbitrary")),
    )(dcat, proj, proj, proj, proj, oraw, states, lb, hg_norm, mstack3, masks, upq, upk, trirev3)


SB_SUB = 128
LOG2_E = 1.4426950408889634
SB_SCALE = 1.0 / math.sqrt(HEAD)
SB_QUERY_SCALE = SB_SCALE * LOG2_E


def _split2(x):
    hi = x.astype(BF16)
    lo = (x - hi.astype(F32)).astype(BF16)
    return jnp.concatenate([hi, lo], axis=1)


def _sb_constants():
    j = np.arange(SB_SUB)
    after = (j[:, None] > j[None, :]).astype(np.float32)
    before = (j[:, None] < j[None, :]).astype(np.float32)
    return (jnp.asarray(np.concatenate([after, after], axis=0), BF16),
            jnp.asarray(np.concatenate([before, before], axis=0), BF16))


def _sb_diag_mask(t):
    return lax.broadcasted_iota(jnp.int32, (t, t), 1) < lax.broadcasted_iota(jnp.int32, (t, t), 0)


def _sb_scores(q, k_ref, col0, t):
    ks = k_ref[pl.ds(pl.multiple_of(col0, t), t), :]
    return lax.dot_general(q, ks, (((1,), (1,)), ((), ())), preferred_element_type=F32)


def _sb_weights(z, mask, run, after2):
    nsub = z.shape[1] // SB_SUB
    nz = -z
    lk = jnp.minimum(nz, 0.0) - jnp.log(1.0 + jnp.exp2(jnp.minimum(z, nz))) * LOG2_E
    if mask is not None:
        lk = jnp.where(mask, lk, 0.0)
    locs, tots = [], []
    for b in range(nsub):
        lkb = lk[:, b * SB_SUB:(b + 1) * SB_SUB]
        loc = jnp.dot(_split2(lkb), after2, preferred_element_type=F32)
        locs.append(loc)
        tots.append(loc[:, 0:1] + lkb[:, 0:1])
    ws = [None] * nsub
    for b in reversed(range(nsub)):
        sl = slice(b * SB_SUB, (b + 1) * SB_SUB)
        ws[b] = jnp.exp2(z[:, sl] + lk[:, sl] + (locs[b] + run))
        run = run + tots[b]
    w = jnp.concatenate(ws, axis=1)
    if mask is not None:
        w = jnp.where(mask, w, 0.0)
    return w, run


def sb_forward(projb, nh, m0, t=512):
    S = projb.shape[0]
    after2, _ = _sb_constants()

    def body(q_ref, k_ref, v_ref, af_ref, o_ref):
        i = pl.program_id(1)
        q = q_ref[...]
        after = af_ref[...]

        def block(jb, run, mask):
            z = _sb_scores(q, k_ref, jb * t, t)
            w, run = _sb_weights(z, mask, run, after)
            vs = v_ref[pl.ds(pl.multiple_of(jb * t, t), t), :]
            return run, jnp.dot(w.astype(BF16), vs, preferred_element_type=F32)

        run, acc = block(i, jnp.zeros((t, 1), F32), _sb_diag_mask(t))

        def step(n, carry):
            run, acc = carry
            run, part = block(i - 1 - n, run, None)
            return run, acc + part

        _, acc = lax.fori_loop(0, i, step, (run, acc))
        o_ref[...] = acc.astype(BF16)

    return pl.pallas_call(
        body, grid=(nh, S // t), name="sb_fwd",
        in_specs=[pl.BlockSpec((t, HEAD), lambda h, i: (i, m0 + h)),
                  pl.BlockSpec((S, HEAD), lambda h, i: (0, m0 + nh + h)),
                  pl.BlockSpec((S, HEAD), lambda h, i: (0, m0 + 2 * nh + h)),
                  pl.BlockSpec(after2.shape, lambda h, i: (0, 0))],
        out_specs=pl.BlockSpec((t, HEAD), lambda h, i: (i, h)),
        out_shape=jax.ShapeDtypeStruct((S, nh * HEAD), BF16),
        compiler_params=_params(("arbitrary", "arbitrary")),
    )(projb, projb, projb, after2)


def sb_backward(dcat, projb, nh, m0, t=512):
    S = projb.shape[0]
    after2, before2 = _sb_constants()
    n_i = S // t
    nsub = t // SB_SUB

    def body(do_ref, q_ref, k_ref, v_ref, af_ref, bf_ref, dq_ref, dk_ref, dv_ref, dbuf, dk_acc, dv_acc):
        i = pl.program_id(1)

        @pl.when(i == 0)
        def _():
            dk_acc[...] = jnp.zeros_like(dk_acc)
            dv_acc[...] = jnp.zeros_like(dv_acc)

        q = q_ref[...]
        dob = do_ref[...].astype(BF16)
        after = af_ref[...]
        before = bf_ref[...]

        def right_to_left(jb, run, mask):
            ksl = pl.ds(pl.multiple_of(jb * t, t), t)
            z = _sb_scores(q, k_ref, jb * t, t)
            w, run = _sb_weights(z, mask, run, after)
            dw = lax.dot_general(dob, v_ref[ksl, :], (((1,), (1,)), ((), ())), preferred_element_type=F32)
            dbuf[jb] = dw * w
            dv_acc[ksl, :] += lax.dot_general(w.astype(BF16), dob, (((0,), (0,)), ((), ())),
                                              preferred_element_type=F32)
            return run

        run = right_to_left(i, jnp.zeros((t, 1), F32), _sb_diag_mask(t))
        lax.fori_loop(0, i, lambda n, run: right_to_left(i - 1 - n, run, None), run)

        def left_to_right(jb, run, dq, mask):
            ksl = pl.ds(pl.multiple_of(jb * t, t), t)
            z = _sb_scores(q, k_ref, jb * t, t)
            d = dbuf[jb]
            sig = 1.0 / (1.0 + jnp.exp2(-z))
            das = []
            for b in range(nsub):
                db = d[:, b * SB_SUB:(b + 1) * SB_SUB]
                prefix = run + jnp.dot(_split2(db), before, preferred_element_type=F32)
                das.append(db - sig[:, b * SB_SUB:(b + 1) * SB_SUB] * (db + prefix))
                run = prefix[:, SB_SUB - 1:SB_SUB] + db[:, SB_SUB - 1:SB_SUB]
            da = jnp.concatenate(das, axis=1)
            if mask is not None:
                da = jnp.where(mask, da, 0.0)
            dab = (da * SB_SCALE).astype(BF16)
            dq = dq + jnp.dot(dab, k_ref[ksl, :], preferred_element_type=F32)
            dk_acc[ksl, :] += lax.dot_general(dab, q, (((0,), (0,)), ((), ())), preferred_element_type=F32)
            return run, dq

        run, dq = lax.fori_loop(0, i, lambda jb, c: left_to_right(jb, c[0], c[1], None),
                                (jnp.zeros((t, 1), F32), jnp.zeros((t, HEAD), F32)))
        _, dq = left_to_right(i, run, dq, _sb_diag_mask(t))
        dq_ref[...] = dq.astype(BF16)

        @pl.when(i == n_i - 1)
        def _():
            dk_ref[...] = (dk_acc[...] * (1.0 / SB_QUERY_SCALE)).astype(BF16)
            dv_ref[...] = dv_acc[...].astype(BF16)

    half = nh * HEAD
    full = pl.BlockSpec((S, HEAD), lambda h, i: (0, h))
    return pl.pallas_call(
        body, grid=(nh, n_i), name="sb_bwd",
        in_specs=[pl.BlockSpec((t, HEAD), lambda h, i: (i, nh + h)),
                  pl.BlockSpec((t, HEAD), lambda h, i: (i, m0 + h)),
                  pl.BlockSpec((S, HEAD), lambda h, i: (0, m0 + nh + h)),
                  pl.BlockSpec((S, HEAD), lambda h, i: (0, m0 + 2 * nh + h)),
                  pl.BlockSpec(after2.shape, lambda h, i: (0, 0)), pl.BlockSpec(before2.shape, lambda h, i: (0, 0))],
        out_specs=[pl.BlockSpec((t, HEAD), lambda h, i: (i, h)), full, full],
        out_shape=[jax.ShapeDtypeStruct((S, half), BF16)] * 3,
        scratch_shapes=[pltpu.VMEM((n_i, t, t), F32), pltpu.VMEM((S, HEAD), F32), pltpu.VMEM((S, HEAD), F32)],
        compiler_params=_params(("arbitrary", "arbitrary")),
    )(dcat, projb, projb, projb, after2, before2)


def local_step(x, target, mix_norm, ffn_norm, final_norm, lb_logits, hg_norm, get_w_in, get_w_rest, send):
    S, D = x.shape
    half = D // 2
    nh = half // HEAD
    tm = 512
    tk = 1024
    row = lambda i, j: (i, 0)

    lb = jax.nn.softmax(lb_logits, axis=0)[0:1]

    h0, r0 = rms_fwd(x, mix_norm[0:1], BF16)
    w_in = get_w_in(h0)
    nbi = w_in.shape[2]
    col = jnp.arange(N_DEV * nbi) // half
    col_scale = jnp.where(col == 4, SB_QUERY_SCALE, 1.0).astype(F32)[None]
    proj, projb = matmul(
        "proj_in", [h0], [w_in], grid=(N_DEV, S // tm, 1),
        a_spec=pl.BlockSpec((tm, D), lambda j, i, k: (i, 0)),
        b_spec=pl.BlockSpec((None, D, nbi), lambda j, i, k: (j, 0, 0)),
        out_spec=pl.BlockSpec((tm, nbi), lambda j, i, k: (i, j)), out_shape=(S, N_DEV * nbi),
        out_dtypes=[F32, BF16], acc_shape=(8, 128),
        bf16_scale=col_scale, bf16_scale_spec=pl.BlockSpec((1, nbi), lambda j, i, k: (0, j)))
    oa, oraw, states = hgrn_forward(proj, lb, hg_norm)
    ob = sb_forward(projb, nh, 4 * nh)
    cat = jnp.concatenate([oa, ob], axis=1)
    w_out, pool_w, pool_scale, wg, wu, wd = get_w_rest(cat)
    (x1,) = matmul(
        "mix_out", [cat], [w_out], grid=(S // tm, 1),
        a_spec=pl.BlockSpec((tm, D), row), b_spec=pl.BlockSpec((D, D), lambda i, k: (0, 0)),
        out_spec=pl.BlockSpec((tm, D), row), out_shape=(S, D), out_dtypes=[F32], acc_shape=(8, 128),
        res=x, res_spec=pl.BlockSpec((tm, D), row))
    h1, r1 = rms_fwd(x1, ffn_norm[0:1], BF16)
    x2, ffn0 = ffn_forward(h1, x1, wg[0], wu[0], wd[0])

    h2, r2 = rms_fwd(x2, mix_norm[1:2], F32)
    x3, pooled = pool_forward(h2, x2, pool_w, pool_scale)
    h3, r3 = rms_fwd(x3, ffn_norm[1:2], BF16)
    x4, ffn1 = ffn_forward(h3, x3, wg[1], wu[1], wd[1])

    loss_blk, dx4, dx4b, d_final = loss_and_final_bwd(x4, final_norm, target)

    dh3, dwg1, dwu1, dwd1 = ffn_backward(dx4b, h3, ffn1, wg[1], wu[1], wd[1])
    dh3 = send("ffn1", dict(ffn_w_gate_1=dwg1, ffn_w_up_1=dwu1, ffn_w_down_1=dwd1), dh3)
    dx3, _, d_ffn1 = rms_bwd(dh3, x3, r3, ffn_norm[1:2], dx4)
    dmixed, dpooled, d_pscale = pool_backward_mix(dx3, pooled, pool_w, pool_scale)
    G = len(POOL_WINDOWS)
    P = D // G
    (d_pool_w,) = matmul(
        "pool_dw", [pooled], [dmixed], grid=(G, S // tk),
        a_spec=pl.BlockSpec((tk, P), lambda g, k: (k, g)), b_spec=pl.BlockSpec((tk, P), lambda g, k: (k, g)),
        out_spec=pl.BlockSpec((None, P, P), lambda g, k: (g, 0, 0)), out_shape=(G, P, P), out_dtypes=[BF16],
        acc_shape=(P, P), trans_a=True)
    dh2 = pool_backward_window(dpooled)
    dx2, dx2b, d_mix1 = rms_bwd(dh2, x2, r2, mix_norm[1:2], dx3)

    dh1, dwg0, dwu0, dwd0 = ffn_backward(dx2b, h1, ffn0, wg[0], wu[0], wd[0])
    dx1, dx1b, d_ffn0 = rms_bwd(dh1, x1, r1, ffn_norm[0:1], dx2)
    (dcat,) = matmul(
        "mix_out_dx", [dx1b], [w_out], grid=(S // tm, 1),
        a_spec=pl.BlockSpec((tm, D), row), b_spec=pl.BlockSpec((D, D), lambda i, k: (0, 0)),
        out_spec=pl.BlockSpec((tm, D), row), out_shape=(S, D), out_dtypes=[F32], acc_shape=(8, 128),
        trans_b=True)
    (d_w_out,) = matmul(
        "mix_out_dw", [cat], [dx1b], grid=(2, S // tk),
        a_spec=pl.BlockSpec((tk, half), lambda m, k: (k, m)), b_spec=pl.BlockSpec((tk, D), lambda m, k: (k, 0)),
        out_spec=pl.BlockSpec((half, D), lambda m, k: (m, 0)), out_shape=(D, D), out_dtypes=[BF16],
        acc_shape=(half, D), trans_a=True)
    dcat = send("layer0", dict(ffn_w_gate_0=dwg0, ffn_w_up_0=dwu0, ffn_w_down_0=dwd0, pool_w=d_pool_w,
                               ab_w_out=d_w_out), dcat)
    dqa, dfa, dia, dga, d_lb, d_hg = hgrn_backward(dcat, proj, oraw, states, lb, hg_norm)
    dqb, dkb, dvb = sb_backward(dcat, projb, nh, 4 * nh)
    dproj = jnp.concatenate([dqa, dfa, dia, dga, dqb, dkb, dvb], axis=1)
    (d_w_in,) = matmul(
        "proj_in_dw", [h0], [dproj], grid=(N_DEV, S // tk),
        a_spec=pl.BlockSpec((tk, D), lambda j, k: (k, 0)), b_spec=pl.BlockSpec((tk, nbi), lambda j, k: (k, j)),
        out_spec=pl.BlockSpec((None, D, nbi), lambda j, k: (j, 0, 0)), out_shape=(N_DEV, D, nbi),
        out_dtypes=[BF16], acc_shape=(D, nbi), trans_a=True)
    dproj = send("w_in", dict(ab_w_in=d_w_in), dproj)
    (dh0,) = matmul(
        "proj_in_dx", [dproj], [w_in], grid=(S // tm, N_DEV),
        a_spec=pl.BlockSpec((tm, nbi), lambda i, j: (i, j)),
        b_spec=pl.BlockSpec((None, D, nbi), lambda i, j: (j, 0, 0)),
        out_spec=pl.BlockSpec((tm, D), row), out_shape=(S, D), out_dtypes=[F32], acc_shape=(tm, D),
        trans_b=True)
    dx0, _, d_mix0 = rms_bwd(dh0, x, r0, mix_norm[0:1], dx1)

    d_l0 = d_lb * lb * (1.0 - lb)
    small = dict(
        loss=loss_blk[0:1, 0:1],
        mix_norm=jnp.concatenate([d_mix0, d_mix1], axis=0),
        ffn_norm=jnp.concatenate([d_ffn0, d_ffn1], axis=0),
        final_norm=d_final,
        lb_logits=jnp.concatenate([d_l0, -d_l0], axis=0),
        hg_out_norm=jnp.sum(d_hg, axis=0),
        pool_scale=d_pscale,
    )
    return dx0, small


def _my_index():
    return 4 * lax.axis_index("x") + 2 * lax.axis_index("y") + lax.axis_index("c")


def _peer(r):
    x, y, c = lax.axis_index("x"), lax.axis_index("y"), lax.axis_index("c")
    px = 1 - x if (r >> 2) & 1 else x
    py = 1 - y if (r >> 1) & 1 else y
    pc = 1 - c if r & 1 else c
    return (px, py, pc), 4 * px + 2 * py + pc


def exchange(name, arrays, gather):
    n = len(arrays)
    n_peers = N_DEV - 1

    def body(*refs):
        ins, outs = refs[:n], refs[n:2 * n]
        send_sems, recv_sems, local_sems = refs[2 * n:]
        me = _my_index()
        local = []
        for a in range(n):
            src = ins[a] if gather else ins[a].at[me]
            cp = pltpu.make_async_copy(src, outs[a].at[me], local_sems.at[a])
            cp.start()
            local.append(cp)
        remote = []
        for a in range(n):
            for r in range(1, N_DEV):
                peer, pidx = _peer(r)
                src = ins[a] if gather else ins[a].at[pidx]
                cp = pltpu.make_async_remote_copy(
                    src_ref=src, dst_ref=outs[a].at[me], send_sem=send_sems.at[a * n_peers + r - 1],
                    recv_sem=recv_sems.at[a * n_peers + r - 1], device_id=peer, device_id_type=MESH)
                cp.start()
                remote.append((cp, a, r))
        for cp, a, r in remote:
            _, pidx = _peer(r)
            src = ins[a] if gather else ins[a].at[pidx]
            pltpu.make_async_remote_copy(
                src_ref=src, dst_ref=outs[a].at[pidx], send_sem=send_sems.at[a * n_peers + r - 1],
                recv_sem=recv_sems.at[a * n_peers + r - 1], device_id=_peer(r)[0], device_id_type=MESH).wait_recv()
        for cp, a, r in remote:
            cp.wait_send()
        for cp in local:
            cp.wait()

    out_shape = [jax.ShapeDtypeStruct(((N_DEV,) + a.shape) if gather else a.shape, a.dtype) for a in arrays]
    any_spec = pl.BlockSpec(memory_space=pl.ANY)
    return pl.pallas_call(
        body, name=name, in_specs=[any_spec] * n, out_specs=[any_spec] * n, out_shape=out_shape,
        scratch_shapes=[pltpu.SemaphoreType.DMA((n * n_peers,)), pltpu.SemaphoreType.DMA((n * n_peers,)),
                        pltpu.SemaphoreType.DMA((n,))],
    )(*arrays)


_HBM = pl.BlockSpec(memory_space=pltpu.HBM)
_SEM = pl.BlockSpec(memory_space=pltpu.SEMAPHORE)
_EFFECT = pltpu.SideEffectType.DATAFLOW_SIDE_EFFECTING


def _landing(arrays, gather):
    me = _my_index()
    lands = []
    for a in arrays:
        own = a[None] if gather else lax.dynamic_slice_in_dim(a, me, 1, axis=0)
        shape = ((N_DEV,) + a.shape) if gather else a.shape
        lands.append(lax.dynamic_update_slice_in_dim(lax.empty(shape, a.dtype), own, me, axis=0))
    return lands


def exchange_start(name, arrays, gather, carry):
    n = len(arrays)
    n_peers = N_DEV - 1
    lands = _landing(arrays, gather)
    n_thru = 2 * n + 1

    def body(*refs):
        src, land = refs[:n], refs[n:2 * n]
        send_sems, recv_sems = refs[n_thru], refs[n_thru + 1]
        token = refs[-1]
        me = _my_index()
        for a in range(n):
            for r in range(1, N_DEV):
                peer, pidx = _peer(r)
                pltpu.make_async_remote_copy(
                    src_ref=src[a] if gather else src[a].at[pidx], dst_ref=land[a].at[me],
                    send_sem=send_sems.at[a * n_peers + r - 1], recv_sem=recv_sems.at[a * n_peers + r - 1],
                    device_id=peer, device_id_type=MESH).start()
        token[...] = jnp.zeros_like(token)

    operands = list(arrays) + lands + [carry]
    outs = pl.pallas_call(
        body, name=name,
        out_shape=(pltpu.SemaphoreType.DMA((n * n_peers,)), pltpu.SemaphoreType.DMA((n * n_peers,)),
                   *[pltpu.HBM(a.shape, a.dtype) for a in operands], jax.ShapeDtypeStruct((8, 128), F32)),
        in_specs=[_HBM] * n_thru,
        out_specs=(_SEM, _SEM, *([_HBM] * n_thru), pl.BlockSpec(memory_space=pltpu.VMEM)),
        input_output_aliases={i: 2 + i for i in range(n_thru)},
        compiler_params=pltpu.CompilerParams(has_side_effects=_EFFECT),
    )(*[pltpu.with_memory_space_constraint(a, pltpu.HBM) for a in operands])
    handle = (outs[0], outs[1], list(outs[2:2 + n]), list(outs[2 + n:2 + 2 * n]), gather)
    return handle, outs[2 + 2 * n]


def exchange_wait(name, handle, after):
    send_sems, recv_sems, srcs, lands, gather = handle
    n = len(srcs)
    n_peers = N_DEV - 1

    def body(*refs):
        src, land = refs[:n], refs[n:2 * n]
        send_s, recv_s = refs[2 * n], refs[2 * n + 1]
        for a in range(n):
            for r in range(1, N_DEV):
                peer, pidx = _peer(r)
                cp = pltpu.make_async_remote_copy(
                    src_ref=src[a] if gather else src[a].at[pidx], dst_ref=land[a].at[pidx],
                    send_sem=send_s.at[a * n_peers + r - 1], recv_sem=recv_s.at[a * n_peers + r - 1],
                    device_id=peer, device_id_type=MESH)
                cp.wait_send()
                cp.wait_recv()

    shapes = [pltpu.HBM(a.shape, a.dtype) for a in srcs] + [pltpu.HBM(l.shape, l.dtype) for l in lands]
    outs = pl.pallas_call(
        body, name=name, out_shape=tuple(shapes),
        in_specs=[_HBM] * (2 * n) + [_SEM, _SEM, pl.BlockSpec(memory_space=pl.ANY)],
        out_specs=tuple([_HBM] * (2 * n)),
        input_output_aliases={i: i for i in range(2 * n)},
        compiler_params=pltpu.CompilerParams(has_side_effects=_EFFECT),
    )(*srcs, *lands, send_sems, recv_sems, after)
    return list(outs[n:])


def _row_tile(rows, cap=256):
    best = None
    for t in range(16, min(rows, cap) + 1, 16):
        if rows % t == 0:
            best = t
    return best if best is not None else rows


def sum_slots(name, recv):
    n, R, C = recv.shape
    tr = _row_tile(R)

    def body(r_ref, o_ref):
        g = r_ref[0].astype(F32)
        for d in range(1, n):
            g = g + r_ref[d].astype(F32)
        o_ref[...] = g

    return pl.pallas_call(
        body, grid=(R // tr,), name=name,
        in_specs=[pl.BlockSpec((n, tr, C), lambda i: (0, i, 0))],
        out_specs=pl.BlockSpec((tr, C), lambda i: (i, 0)),
        out_shape=jax.ShapeDtypeStruct((R, C), F32),
        compiler_params=_params(("arbitrary",)),
    )(recv)


def adamw(name, recv, w, m, v, layer=None, prev=None):
    n, R, C = recv.shape
    tr = _row_tile(R)

    def body(r_ref, w_ref, m_ref, v_ref, *rest):
        g_ref, d_ref, nm_ref, nv_ref = rest[-4:]
        g = r_ref[0].astype(F32)
        for d in range(1, n):
            g = g + r_ref[d].astype(F32)
        mm = ADAM_B1 * m_ref[...] + (1.0 - ADAM_B1) * g
        vv = ADAM_B2 * v_ref[...] + (1.0 - ADAM_B2) * (g * g)
        m_hat = mm / (1.0 - ADAM_B1 ** ADAM_STEP)
        v_hat = vv / (1.0 - ADAM_B2 ** ADAM_STEP)
        g_ref[...] = g
        d_ref[...] = -ADAM_LR * (m_hat / (jnp.sqrt(v_hat) + ADAM_EPS) + ADAM_WD * w_ref[...])
        nm_ref[...] = mm
        nv_ref[...] = vv

    if layer is None:
        row = pl.BlockSpec((tr, C), lambda i: (i, 0))
        shape = (R, C)
    else:
        row = pl.BlockSpec((None, tr, C), lambda i: (layer, i, 0))
        shape = w.shape
    prev = [] if prev is None else list(prev)
    return pl.pallas_call(
        body, grid=(R // tr,), name=name,
        in_specs=[pl.BlockSpec((n, tr, C), lambda i: (0, i, 0)), row, row, row]
                 + [pl.BlockSpec(memory_space=pl.ANY)] * len(prev),
        out_specs=[row] * 4,
        out_shape=[jax.ShapeDtypeStruct(shape, F32)] * 4,
        input_output_aliases={4 + o: o for o in range(len(prev))},
        compiler_params=_params(("arbitrary",)),
    )(recv, w, m, v, *prev)


def _adamw_nd(name, recv, w, m, v):
    shp = w.shape
    C = shp[-1]
    flat = lambda a: a.reshape(-1, C)
    outs = adamw(name, recv.reshape(recv.shape[0], -1, C), flat(w), flat(m), flat(v))
    return [o.reshape(shp) for o in outs]


_SMALL_NAMES = ("loss", "mix_norm", "ffn_norm", "final_norm", "lb_logits", "hg_out_norm", "pool_scale")
_LANES = 128


def _pack_small(parts):
    rows, layout = [], {}
    at = 0
    for name in parts:
        flat = parts[name].reshape(-1).astype(F32)
        n_rows = -(-flat.shape[0] // (8 * _LANES)) * 8
        flat = jnp.pad(flat, (0, n_rows * _LANES - flat.shape[0]))
        rows.append(flat.reshape(n_rows, _LANES))
        layout[name] = (at, parts[name].shape)
        at += n_rows
    return jnp.concatenate(rows, axis=0), layout


def _unpack_small(pack, layout):
    out = {}
    for name, (at, shape) in layout.items():
        size = int(np.prod(shape))
        n_rows = -(-size // _LANES)
        out[name] = pack[at:at + n_rows].reshape(-1)[:size].reshape(shape)
    return out


def kernel(x, mix_norm, ffn_norm, final_norm, ab_w_in, lb_logits, hg_out_norm, ab_w_out, pool_w, pool_scale, ffn_w_gate, ffn_w_up, ffn_w_down, loss_target, m_mix_norm, m_ffn_norm, m_final_norm, m_ab_w_in, m_lb_logits, m_hg_out_norm, m_ab_w_out, m_pool_w, m_pool_scale, m_ffn_w_gate, m_ffn_w_up, m_ffn_w_down, v_mix_norm, v_ffn_norm, v_final_norm, v_ab_w_in, v_lb_logits, v_hg_out_norm, v_ab_w_out, v_pool_w, v_pool_scale, v_ffn_w_gate, v_ffn_w_up, v_ffn_w_down):
    D = x.shape[-1]
    n_layers = ffn_w_gate.shape[0]
    G = pool_w.shape[1]
    P = pool_w.shape[3]
    me = _my_index()

    in_handle, mix_norm_after = exchange_start("gather_w_in_start", [ab_w_in[0].astype(BF16)], True, mix_norm)
    rest = [ab_w_out[0], pool_w[0]]
    for l in range(n_layers):
        rest += [ffn_w_gate[l], ffn_w_up[l], ffn_w_down[l]]
    rest = [s.astype(BF16) for s in rest] + [pool_scale]
    rest_handle = []

    def get_w_in(after):
        w_in = exchange_wait("gather_w_in_wait", in_handle, after)[0]
        handle, w_in = exchange_start("gather_rest_start", rest, True, w_in)
        rest_handle.append(handle)
        return w_in

    def get_w_rest(after):
        got = exchange_wait("gather_rest_wait", rest_handle[0], after)
        w_out_g = got[0].reshape(D, D)
        pool_g = got[1].transpose(1, 0, 2, 3).reshape(G, P, P)
        wg = [got[2 + 3 * l] for l in range(n_layers)]
        wu = [got[3 + 3 * l] for l in range(n_layers)]
        wd = [got[4 + 3 * l] for l in range(n_layers)]
        return w_out_g, pool_g, got[-1].reshape(1, D), wg, wu, wd

    in_flight = []

    def send(tag, grads, carry):
        if "pool_w" in grads:
            grads = dict(grads, pool_w=grads["pool_w"].reshape(G, N_DEV, P // N_DEV, P).transpose(1, 0, 2, 3))
        if "ab_w_out" in grads:
            grads = dict(grads, ab_w_out=grads["ab_w_out"].reshape(N_DEV, D // N_DEV, D))
        handle, carry = exchange_start("grads_" + tag + "_start", list(grads.values()), False, carry)
        in_flight.append((tag, list(grads.keys()), handle))
        return carry

    dx0, small = local_step(x[0], loss_target[0], mix_norm_after, ffn_norm, final_norm[None],
                            lb_logits, hg_out_norm, get_w_in, get_w_rest, send)

    recv = {}
    for tag, names, handle in in_flight:
        recv.update(zip(names, exchange_wait("grads_" + tag + "_wait", handle, dx0)))
    small_pack, layout = _pack_small({k: small[k] for k in _SMALL_NAMES})
    (small_all,) = exchange("gather_small", [small_pack], gather=True)
    tot = _unpack_small(sum_slots("sum_small", small_all), layout)

    res = {}
    res["ab_w_in"] = _adamw_nd("adamw_w_in", recv["ab_w_in"], ab_w_in, m_ab_w_in, v_ab_w_in)
    res["ab_w_out"] = _adamw_nd("adamw_w_out", recv["ab_w_out"], ab_w_out, m_ab_w_out, v_ab_w_out)
    res["pool_w"] = _adamw_nd("adamw_pool_w", recv["pool_w"], pool_w, m_pool_w, v_pool_w)
    ffn_in = {"ffn_w_gate": (ffn_w_gate, m_ffn_w_gate, v_ffn_w_gate),
              "ffn_w_up": (ffn_w_up, m_ffn_w_up, v_ffn_w_up),
              "ffn_w_down": (ffn_w_down, m_ffn_w_down, v_ffn_w_down)}
    for name, (w, m, v) in ffn_in.items():
        outs = None
        for l in range(n_layers):
            outs = adamw("adamw_" + name, recv[name + "_" + str(l)], w, m, v, layer=l, prev=outs)
        res[name] = outs

    n_ps = pool_scale.shape[1]
    small_g = dict(tot)
    small_g["pool_scale"] = lax.dynamic_slice(tot["pool_scale"], (0, me * n_ps), (1, n_ps))
    small_w = dict(mix_norm=(mix_norm, m_mix_norm, v_mix_norm), ffn_norm=(ffn_norm, m_ffn_norm, v_ffn_norm),
                   final_norm=(final_norm, m_final_norm, v_final_norm),
                   lb_logits=(lb_logits, m_lb_logits, v_lb_logits),
                   hg_out_norm=(hg_out_norm, m_hg_out_norm, v_hg_out_norm),
                   pool_scale=(pool_scale, m_pool_scale, v_pool_scale))
    g_pack, lay2 = _pack_small({k: small_g[k].reshape(small_w[k][0].shape) for k in small_w})
    w_pack, _ = _pack_small({k: small_w[k][0] for k in small_w})
    m_pack, _ = _pack_small({k: small_w[k][1] for k in small_w})
    v_pack, _ = _pack_small({k: small_w[k][2] for k in small_w})
    small_out = [_unpack_small(o, lay2) for o in adamw("adamw_small", g_pack[None], w_pack, m_pack, v_pack)]
    for k in small_w:
        res[k] = [small_out[o][k] for o in range(4)]

    order = ("mix_norm", "ffn_norm", "final_norm", "ab_w_in", "lb_logits", "hg_out_norm", "ab_w_out", "pool_w",
             "pool_scale", "ffn_w_gate", "ffn_w_up", "ffn_w_down")
    outs = [tot["loss"].reshape(()), dx0[None]]
    for o in range(4):
        outs += [res[k][o] for k in order]
    return tuple(outs)
```

```python
import functools
import math

import numpy as np
import jax
import jax.numpy as jnp
from jax import lax
from jax.experimental import pallas as pl
from jax.experimental.pallas import tpu as pltpu

F32 = jnp.float32
BF16 = jnp.bfloat16

N_DEV = 8
RMS_EPS = 1e-6
HEAD = 128
HG_CHUNK = 64
HG_HEADS_PER_BLOCK = 4
POOL_WINDOWS = (2, 4, 8, 16)
POOL_HALO = 16
ADAM_LR, ADAM_B1, ADAM_B2, ADAM_EPS, ADAM_WD, ADAM_STEP = 0.001, 0.9, 0.999, 1e-08, 0.01, 10
VMEM_LIMIT_BYTES = 60 * 1024 * 1024
MESH = pl.DeviceIdType.MESH


def _params(sem):
    return pltpu.CompilerParams(dimension_semantics=sem, vmem_limit_bytes=VMEM_LIMIT_BYTES)


def _sigmoid(x):
    return 1.0 / (1.0 + jnp.exp(-x))


def rms_fwd(x, gain, out_dtype, ts=512):
    S, D = x.shape

    def body(x_ref, g_ref, h_ref, r_ref):
        xv = x_ref[...]
        r = lax.rsqrt(jnp.mean(xv * xv, axis=-1, keepdims=True) + RMS_EPS)
        h_ref[...] = ((xv * r) * g_ref[...]).astype(h_ref.dtype)
        r_ref[...] = r

    return pl.pallas_call(
        body, grid=(S // ts,), name="rms_fwd",
        in_specs=[pl.BlockSpec((ts, D), lambda i: (i, 0)), pl.BlockSpec((1, D), lambda i: (0, 0))],
        out_specs=[pl.BlockSpec((ts, D), lambda i: (i, 0)), pl.BlockSpec((ts, 1), lambda i: (i, 0))],
        out_shape=[jax.ShapeDtypeStruct((S, D), out_dtype), jax.ShapeDtypeStruct((S, 1), F32)],
        compiler_params=_params(("arbitrary",)),
    )(x, gain)


def rms_bwd(dh, x, r, gain, dres, ts=512):
    S, D = x.shape

    def body(dh_ref, x_ref, r_ref, g_ref, dres_ref, dx_ref, dxb_ref, dg_ref):
        i = pl.program_id(0)
        rr = r_ref[...]
        xh = x_ref[...] * rr
        dhv = dh_ref[...]
        dxh = dhv * g_ref[...]
        dx = dres_ref[...] + rr * (dxh - xh * jnp.mean(dxh * xh, axis=-1, keepdims=True))
        dx_ref[...] = dx
        dxb_ref[...] = dx.astype(BF16)
        part = jnp.sum(dhv * xh, axis=0, keepdims=True)

        @pl.when(i == 0)
        def _():
            dg_ref[...] = part

        @pl.when(i > 0)
        def _():
            dg_ref[...] += part

    row = pl.BlockSpec((ts, D), lambda i: (i, 0))
    vec = pl.BlockSpec((1, D), lambda i: (0, 0))
    return pl.pallas_call(
        body, grid=(S // ts,), name="rms_bwd",
        in_specs=[row, row, pl.BlockSpec((ts, 1), lambda i: (i, 0)), vec, row],
        out_specs=[row, row, vec],
        out_shape=[jax.ShapeDtypeStruct((S, D), F32), jax.ShapeDtypeStruct((S, D), BF16),
                   jax.ShapeDtypeStruct((1, D), F32)],
        compiler_params=_params(("arbitrary",)),
    )(dh, x, r, gain, dres)


def loss_and_final_bwd(x, gain, target, ts=512):
    S, D = x.shape

    def body(x_ref, g_ref, t_ref, loss_ref, dx_ref, dxb_ref, dg_ref):
        i = pl.program_id(0)
        xv = x_ref[...]
        rr = lax.rsqrt(jnp.mean(xv * xv, axis=-1, keepdims=True) + RMS_EPS)
        xh = xv * rr
        err = xh * g_ref[...] - t_ref[...]
        part_loss = 0.5 * jnp.sum(jnp.mean(err * err, axis=-1, keepdims=True))
        dy = err / D
        dxh = dy * g_ref[...]
        dx = rr * (dxh - xh * jnp.mean(dxh * xh, axis=-1, keepdims=True))
        dx_ref[...] = dx
        dxb_ref[...] = dx.astype(BF16)
        part = jnp.sum(dy * xh, axis=0, keepdims=True)

        @pl.when(i == 0)
        def _():
            dg_ref[...] = part
            loss_ref[...] = jnp.zeros_like(loss_ref) + part_loss

        @pl.when(i > 0)
        def _():
            dg_ref[...] += part
            loss_ref[...] += part_loss

    row = pl.BlockSpec((ts, D), lambda i: (i, 0))
    vec = pl.BlockSpec((1, D), lambda i: (0, 0))
    return pl.pallas_call(
        body, grid=(S // ts,), name="loss_final",
        in_specs=[row, vec, row],
        out_specs=[pl.BlockSpec((8, 128), lambda i: (0, 0)), row, row, vec],
        out_shape=[jax.ShapeDtypeStruct((8, 128), F32), jax.ShapeDtypeStruct((S, D), F32),
                   jax.ShapeDtypeStruct((S, D), BF16), jax.ShapeDtypeStruct((1, D), F32)],
        compiler_params=_params(("arbitrary",)),
    )(x, gain, target)


def matmul(name, a_ops, b_ops, *, grid, a_spec, b_spec, out_spec, out_shape, out_dtypes, acc_shape,
           trans_a=False, trans_b=False, res=None, res_spec=None, bf16_scale=None, bf16_scale_spec=None):
    n_pairs = len(a_ops)
    n_out = len(out_dtypes)
    nk = grid[-1]
    kaxis = len(grid) - 1
    dn = (((0,) if trans_a else (1,), (1,) if trans_b else (0,)), ((), ()))

    def body(*refs):
        a_refs = refs[:n_pairs]
        b_refs = refs[n_pairs:2 * n_pairs]
        pos = 2 * n_pairs
        res_ref = None
        if res is not None:
            res_ref = refs[pos]
            pos += 1
        scale_ref = None
        if bf16_scale is not None:
            scale_ref = refs[pos]
            pos += 1
        out_refs = refs[pos:pos + n_out]
        acc_ref = refs[pos + n_out]
        k = pl.program_id(kaxis)
        in_place = n_out == 1 and out_dtypes[0] == F32
        target = out_refs[0] if in_place else acc_ref

        def finish(val):
            if res_ref is not None:
                val = val + res_ref[...]
            for o in out_refs:
                if scale_ref is not None and o.dtype == BF16:
                    o[...] = (val * scale_ref[...]).astype(BF16)
                else:
                    o[...] = val.astype(o.dtype)

        if nk > 1:
            @pl.when(k == 0)
            def _():
                if in_place and res_ref is not None:
                    target[...] = res_ref[...]
                else:
                    target[...] = jnp.zeros_like(target)

        part = None
        for ar, br in zip(a_refs, b_refs):
            d = lax.dot_general(ar[...].astype(BF16), br[...].astype(BF16), dn, preferred_element_type=F32)
            part = d if part is None else part + d

        if nk == 1:
            finish(part)
        else:
            target[...] += part
            if not in_place:
                @pl.when(k == nk - 1)
                def _():
                    finish(acc_ref[...])

    in_specs = [a_spec] * n_pairs + [b_spec] * n_pairs
    operands = list(a_ops) + list(b_ops)
    if res is not None:
        in_specs.append(res_spec)
        operands.append(res)
    if bf16_scale is not None:
        in_specs.append(bf16_scale_spec)
        operands.append(bf16_scale)
    return pl.pallas_call(
        body, grid=grid, name=name, in_specs=in_specs,
        out_specs=[out_spec] * n_out,
        out_shape=[jax.ShapeDtypeStruct(out_shape, dt) for dt in out_dtypes],
        scratch_shapes=[pltpu.VMEM(acc_shape, F32)],
        compiler_params=_params(("arbitrary",) * len(grid)),
    )(*operands)


def ffn_gate_up(h, wg, wu, tm=1024):
    S, D = h.shape
    nb = wg.shape[2]

    def body(h_ref, wg_ref, wu_ref, p_ref, r_ref, a_ref):
        for c in range(2):
            rows = slice(c * (tm // 2), (c + 1) * (tm // 2))
            hv = h_ref[rows, :]
            g = jnp.dot(hv, wg_ref[...], preferred_element_type=F32)
            u = jnp.dot(hv, wu_ref[...], preferred_element_type=F32)
            s = _sigmoid(g)
            p = g * s
            p_ref[rows, :] = p
            r_ref[rows, :] = u * (s * (1.0 + g * (1.0 - s)))
            a_ref[rows, :] = (p * u).astype(BF16)

    wspec = pl.BlockSpec((None, D, nb), lambda j, i: (j, 0, 0))
    ospec = pl.BlockSpec((None, tm, nb), lambda j, i: (j, i, 0))
    return pl.pallas_call(
        body, grid=(N_DEV, S // tm), name="ffn_gate_up",
        in_specs=[pl.BlockSpec((tm, D), lambda j, i: (i, 0)), wspec, wspec],
        out_specs=[ospec, ospec, ospec],
        out_shape=[jax.ShapeDtypeStruct((N_DEV, S, nb), F32), jax.ShapeDtypeStruct((N_DEV, S, nb), F32),
                   jax.ShapeDtypeStruct((N_DEV, S, nb), BF16)],
        compiler_params=_params(("arbitrary", "arbitrary")),
    )(h, wg, wu)


def ffn_bwd_hidden(dy, wd, p, r, tm=1024):
    S, D = dy.shape
    nb = wd.shape[1]

    def body(dy_ref, wd_ref, p_ref, r_ref, dg_ref, du_ref):
        for c in range(2):
            rows = slice(c * (tm // 2), (c + 1) * (tm // 2))
            da = lax.dot_general(dy_ref[rows, :], wd_ref[...], (((1,), (1,)), ((), ())),
                                 preferred_element_type=F32)
            du_ref[rows, :] = (da * p_ref[rows, :]).astype(BF16)
            dg_ref[rows, :] = (da * r_ref[rows, :]).astype(BF16)

    hspec = pl.BlockSpec((None, tm, nb), lambda j, i: (j, i, 0))
    return pl.pallas_call(
        body, grid=(N_DEV, S // tm), name="ffn_bwd_hidden",
        in_specs=[pl.BlockSpec((tm, D), lambda j, i: (i, 0)), pl.BlockSpec((None, nb, D), lambda j, i: (j, 0, 0)),
                  hspec, hspec],
        out_specs=[hspec, hspec],
        out_shape=[jax.ShapeDtypeStruct((N_DEV, S, nb), BF16), jax.ShapeDtypeStruct((N_DEV, S, nb), BF16)],
        compiler_params=_params(("arbitrary", "arbitrary")),
    )(dy, wd, p, r)


def ffn_forward(h, xres, wg, wu, wd, tm=1024):
    S, D = h.shape
    nb = wg.shape[2]
    g, u, a = ffn_gate_up(h, wg, wu)
    (xo,) = matmul(
        "ffn_down", [a], [wd], grid=(S // tm, N_DEV),
        a_spec=pl.BlockSpec((None, tm, nb), lambda i, j: (j, i, 0)),
        b_spec=pl.BlockSpec((None, nb, D), lambda i, j: (j, 0, 0)),
        out_spec=pl.BlockSpec((tm, D), lambda i, j: (i, 0)), out_shape=(S, D), out_dtypes=[F32],
        acc_shape=(tm, D), res=xres, res_spec=pl.BlockSpec((tm, D), lambda i, j: (i, 0)))
    return xo, (g, u, a)


def ffn_backward(dy_b, h, saved, wg, wu, wd, tm=1024, tk=2048):
    S, D = h.shape
    nb = wg.shape[2]
    g, u, a = saved
    dg, du = ffn_bwd_hidden(dy_b, wd, g, u)
    (dh,) = matmul(
        "ffn_dh", [dg, du], [wg, wu], grid=(S // tm, N_DEV),
        a_spec=pl.BlockSpec((None, tm, nb), lambda i, j: (j, i, 0)),
        b_spec=pl.BlockSpec((None, D, nb), lambda i, j: (j, 0, 0)),
        out_spec=pl.BlockSpec((tm, D), lambda i, j: (i, 0)), out_shape=(S, D), out_dtypes=[F32],
        acc_shape=(tm, D), trans_b=True)

    def wgrad_in(name, dhid):
        (dw,) = matmul(
            name, [h], [dhid], grid=(N_DEV, S // tk),
            a_spec=pl.BlockSpec((tk, D), lambda j, k: (k, 0)),
            b_spec=pl.BlockSpec((None, tk, nb), lambda j, k: (j, k, 0)),
            out_spec=pl.BlockSpec((None, D, nb), lambda j, k: (j, 0, 0)), out_shape=(N_DEV, D, nb),
            out_dtypes=[BF16], acc_shape=(D, nb), trans_a=True)
        return dw

    dwg = wgrad_in("ffn_dwg", dg)
    dwu = wgrad_in("ffn_dwu", du)
    (dwd,) = matmul(
        "ffn_dwd", [a], [dy_b], grid=(N_DEV, S // tk),
        a_spec=pl.BlockSpec((None, tk, nb), lambda j, k: (j, k, 0)),
        b_spec=pl.BlockSpec((tk, D), lambda j, k: (k, 0)),
        out_spec=pl.BlockSpec((None, nb, D), lambda j, k: (j, 0, 0)), out_shape=(N_DEV, nb, D),
        out_dtypes=[BF16], acc_shape=(nb, D), trans_a=True)
    return dh, dwg, dwu, dwd


def _pool_counts(row0, n, w):
    pos = row0 + lax.broadcasted_iota(jnp.int32, (n, 1), 0)
    return jnp.minimum(pos + 1, w).astype(F32)


def pool_forward(h, xres, w, scale, ts=256):
    S, D = h.shape
    G = len(POOL_WINDOWS)
    P = D // G
    hb = ts // POOL_HALO

    def body(h_ref, halo_ref, x_ref, w_ref, s_ref, xo_ref, p_ref):
        i = pl.program_id(0)
        for gi, win in enumerate(POOL_WINDOWS):
            cols = slice(gi * P, (gi + 1) * P)
            cur = h_ref[:, cols]
            halo = jnp.where(i > 0, halo_ref[:, cols], 0.0)
            acc = jnp.concatenate([halo, cur], axis=0)
            step = 1
            while step < win:
                acc = acc + pltpu.roll(acc, step, 0)
                step *= 2
            wsum = acc[POOL_HALO:, :]
            pooled = wsum / _pool_counts(i * ts, ts, win) - cur
            pb = pooled.astype(BF16)
            p_ref[:, cols] = pb
            mixed = jnp.dot(pb, w_ref[gi], preferred_element_type=F32)
            xo_ref[:, cols] = x_ref[:, cols] + mixed * s_ref[:, cols]

    row = pl.BlockSpec((ts, D), lambda i: (i, 0))
    return pl.pallas_call(
        body, grid=(S // ts,), name="pool_fwd",
        in_specs=[row, pl.BlockSpec((POOL_HALO, D), lambda i: (jnp.maximum(i * hb - 1, 0), 0)), row,
                  pl.BlockSpec((G, P, P), lambda i: (0, 0, 0)), pl.BlockSpec((1, D), lambda i: (0, 0))],
        out_specs=[row, row],
        out_shape=[jax.ShapeDtypeStruct((S, D), F32), jax.ShapeDtypeStruct((S, D), BF16)],
        compiler_params=_params(("arbitrary",)),
    )(h, h, xres, w, scale)


def pool_backward_mix(dx, pooled, w, scale, ts=256):
    S, D = dx.shape
    G = len(POOL_WINDOWS)
    P = D // G

    def body(dx_ref, p_ref, w_ref, s_ref, dm_ref, dp_ref, ds_ref):
        i = pl.program_id(0)
        parts = []
        for gi in range(G):
            cols = slice(gi * P, (gi + 1) * P)
            dxv = dx_ref[:, cols]
            dmb = (dxv * s_ref[:, cols]).astype(BF16)
            dm_ref[:, cols] = dmb
            dp_ref[:, cols] = lax.dot_general(dmb, w_ref[gi], (((1,), (1,)), ((), ())),
                                              preferred_element_type=F32)
            mixed = jnp.dot(p_ref[:, cols], w_ref[gi], preferred_element_type=F32)
            parts.append(jnp.sum(dxv * mixed, axis=0, keepdims=True))
        part = jnp.concatenate(parts, axis=1)

        @pl.when(i == 0)
        def _():
            ds_ref[...] = part

        @pl.when(i > 0)
        def _():
            ds_ref[...] += part

    row = pl.BlockSpec((ts, D), lambda i: (i, 0))
    vec = pl.BlockSpec((1, D), lambda i: (0, 0))
    return pl.pallas_call(
        body, grid=(S // ts,), name="pool_bwd_mix",
        in_specs=[row, row, pl.BlockSpec((G, P, P), lambda i: (0, 0, 0)), vec],
        out_specs=[row, row, vec],
        out_shape=[jax.ShapeDtypeStruct((S, D), BF16), jax.ShapeDtypeStruct((S, D), F32),
                   jax.ShapeDtypeStruct((1, D), F32)],
        compiler_params=_params(("arbitrary",)),
    )(dx, pooled, w, scale)


def pool_backward_window(dp, ts=256):
    S, D = dp.shape
    G = len(POOL_WINDOWS)
    P = D // G
    hb = ts // POOL_HALO
    n_i = S // ts
    n_rows = ts + POOL_HALO

    def body(dp_ref, halo_ref, dh_ref):
        i = pl.program_id(0)
        for gi, win in enumerate(POOL_WINDOWS):
            cols = slice(gi * P, (gi + 1) * P)
            cur = dp_ref[:, cols]
            halo = jnp.where(i < n_i - 1, halo_ref[:, cols], 0.0)
            acc = jnp.concatenate([cur / _pool_counts(i * ts, ts, win),
                                   halo / _pool_counts((i + 1) * ts, POOL_HALO, win)], axis=0)
            step = 1
            while step < win:
                acc = acc + pltpu.roll(acc, n_rows - step, 0)
                step *= 2
            dh_ref[:, cols] = acc[:ts, :] - cur

    row = pl.BlockSpec((ts, D), lambda i: (i, 0))
    return pl.pallas_call(
        body, grid=(n_i,), name="pool_bwd_window",
        in_specs=[row, pl.BlockSpec((POOL_HALO, D), lambda i: (jnp.minimum((i + 1) * hb, S // POOL_HALO - 1), 0))],
        out_specs=row,
        out_shape=jax.ShapeDtypeStruct((S, D), F32),
        compiler_params=_params(("arbitrary",)),
    )(dp, dp)


_HG_LEVELS = (32, 16, 8, 4, 2, 1)
_N_LEV = len(_HG_LEVELS) + 1


def _hgrn_constants():
    C = HG_CHUNK
    t = np.arange(C)
    tri = (t[None, :] <= t[:, None]).astype(np.float32)
    blocks = [tri]
    masks, upq, upk = [], [], []
    for m in _HG_LEVELS:
        p = (t // (2 * m)) * 2 * m + m - 1
        blocks.append(tri[p])
        masks.append(((t[:, None] // (2 * m)) == (t[None, :] // (2 * m))).astype(np.float32))
        upper = (t % (2 * m)) >= m
        upq.append(np.repeat(upper[:, None], HEAD, 1).astype(np.float32))
        upk.append(np.repeat(~upper[:, None], HEAD, 1).astype(np.float32))
    blocks.append(tri)
    masks.append(np.eye(C, dtype=np.float32))
    upq.append(np.ones((C, HEAD), np.float32))
    upk.append(np.ones((C, HEAD), np.float32))
    mstack = np.concatenate(blocks, axis=0)
    mstack3 = np.concatenate([mstack] * 3, axis=1)
    trirev3 = np.concatenate([tri.T] * 3, axis=1)
    return (jnp.asarray(mstack3, BF16), jnp.asarray(np.stack(masks)), jnp.asarray(np.stack(upq)),
            jnp.asarray(np.stack(upk)), jnp.asarray(trirev3, BF16))


def _split3(x):
    hi = x.astype(BF16)
    r1 = x - hi.astype(F32)
    mid = r1.astype(BF16)
    lo = (r1 - mid.astype(F32)).astype(BF16)
    return jnp.concatenate([hi, mid, lo], axis=0)


def _hgrn_chunk_common(qa, fa, lbv, mstack3, upq, upk):
    sq = _sigmoid(qa)
    q = qa * sq
    sf = _sigmoid(fa)
    f = lbv + (1.0 - lbv) * sf
    g = jnp.log(f)
    k = 1.0 - f
    gall = jnp.dot(mstack3, _split3(g), preferred_element_type=F32).reshape(_N_LEV + 1, HG_CHUNK, HEAD)
    G = gall[0]
    eq_exp = G[None] - gall[1:]
    eq = jnp.exp(jnp.minimum(eq_exp, 0.0)) * upq
    ek = jnp.exp(jnp.minimum(-eq_exp, 0.0)) * upk
    Qs = (q[None] * eq).astype(BF16)
    Ks = (k[None] * ek).astype(BF16)
    return sq, q, sf, f, k, G, eq, ek, Qs, Ks


def hgrn_forward(proj, lb, hg_norm, ts=512):
    S = proj.shape[0]
    nh = lb.shape[1] // HEAD
    C = HG_CHUNK
    ncs = ts // C
    mstack3, masks, upq, upk, _ = _hgrn_constants()

    def body(qa_ref, fa_ref, ia_ref, ga_ref, lb_ref, gn_ref, ms_ref, mk_ref, uq_ref, uk_ref,
             oa_ref, oraw_ref, st_ref, state):
        tt = pl.program_id(1)

        @pl.when(tt == 0)
        def _():
            state[...] = jnp.zeros_like(state)

        gn = gn_ref[...]

        def chunk(c, carry):
            sl = pl.ds(pl.multiple_of(c * C, C), C)
            for hh in range(HG_HEADS_PER_BLOCK):
                cols = slice(hh * HEAD, (hh + 1) * HEAD)
                qa, fa, v, ga = qa_ref[sl, cols], fa_ref[sl, cols], ia_ref[sl, cols], ga_ref[sl, cols]
                _, q, _, _, k, G, _, _, Qs, Ks = _hgrn_chunk_common(qa, fa, lb_ref[:, cols], ms_ref[...],
                                                                    uq_ref[...], uk_ref[...])
                att7 = lax.dot_general(Qs, Ks, (((2,), (2,)), ((0,), (0,))), preferred_element_type=F32)
                att = jnp.sum(att7 * mk_ref[...], axis=0)
                st = state[hh]
                st_ref[hh, c] = st
                vb = v.astype(BF16)
                qg = (q * jnp.exp(G)).astype(BF16)
                o = jnp.dot(att.astype(BF16), vb, preferred_element_type=F32)
                o = o + lax.dot_general(qg, st.astype(BF16), (((1,), (1,)), ((), ())),
                                        preferred_element_type=F32)
                g_last = G[C - 1:C, :]
                kh = (k * jnp.exp(g_last - G)).astype(BF16)
                state[hh] = st * jnp.exp(g_last) + lax.dot_general(vb, kh, (((0,), (0,)), ((), ())),
                                                                   preferred_element_type=F32)
                oraw_ref[sl, cols] = o
                r = lax.rsqrt(jnp.mean(o * o, axis=-1, keepdims=True) + RMS_EPS)
                oa_ref[sl, cols] = (((o * r) * gn) * (ga * _sigmoid(ga))).astype(BF16)
            return carry

        lax.fori_loop(0, ncs, chunk, 0)

    hpb = HG_HEADS_PER_BLOCK
    wide = hpb * HEAD

    def col(m0):
        return pl.BlockSpec((ts, wide), lambda h, t: (t, m0 // hpb + h))

    const3 = lambda shape: pl.BlockSpec(shape, lambda h, t: (0, 0, 0))
    return pl.pallas_call(
        body, grid=(nh // hpb, S // ts), name="hgrn_fwd",
        in_specs=[col(0), col(nh), col(2 * nh), col(3 * nh),
                  pl.BlockSpec((1, wide), lambda h, t: (0, h)), pl.BlockSpec((1, HEAD), lambda h, t: (0, 0)),
                  pl.BlockSpec(mstack3.shape, lambda h, t: (0, 0)), const3(masks.shape), const3(upq.shape),
                  const3(upk.shape)],
        out_specs=[pl.BlockSpec((ts, wide), lambda h, t: (t, h)), pl.BlockSpec((ts, wide), lambda h, t: (t, h)),
                   pl.BlockSpec((hpb, ncs, HEAD, HEAD), lambda h, t: (h, t, 0, 0))],
        out_shape=[jax.ShapeDtypeStruct((S, nh * HEAD), BF16), jax.ShapeDtypeStruct((S, nh * HEAD), F32),
                   jax.ShapeDtypeStruct((nh, S // C, HEAD, HEAD), F32)],
        scratch_shapes=[pltpu.VMEM((hpb, HEAD, HEAD), F32)],
        compiler_params=_params(("arbitrary", "arbitrary")),
    )(proj, proj, proj, proj, lb, hg_norm, mstack3, masks, upq, upk)


def hgrn_backward(dcat, proj, oraw, states, lb, hg_norm, ts=512):
    S = proj.shape[0]
    nh = lb.shape[1] // HEAD
    C = HG_CHUNK
    ncs = ts // C
    nt = S // ts
    mstack3, masks, upq, upk, trirev3 = _hgrn_constants()

    def body(do_ref, qa_ref, fa_ref, ia_ref, ga_ref, or_ref, st_ref, lb_ref, gn_ref, ms_ref, mk_ref, uq_ref,
             uk_ref, tr_ref, dqa_ref, dfa_ref, dia_ref, dga_ref, dlb_ref, dgn_ref, dstate):
        tt = pl.program_id(1)

        @pl.when(tt == 0)
        def _():
            dstate[...] = jnp.zeros_like(dstate)
            dlb_ref[...] = jnp.zeros_like(dlb_ref)
            dgn_ref[...] = jnp.zeros_like(dgn_ref)

        gn = gn_ref[...]

        def chunk(cc, carry):
            c = ncs - 1 - cc
            sl = pl.ds(pl.multiple_of(c * C, C), C)
            for hh in range(HG_HEADS_PER_BLOCK):
                cols = slice(hh * HEAD, (hh + 1) * HEAD)
                lbv = lb_ref[:, cols]
                qa, fa, v, ga = qa_ref[sl, cols], fa_ref[sl, cols], ia_ref[sl, cols], ga_ref[sl, cols]
                sq, q, sf, f, k, G, eq, ek, Qs, Ks = _hgrn_chunk_common(qa, fa, lbv, ms_ref[...], uq_ref[...],
                                                                        uk_ref[...])
                mk = mk_ref[...]
                att7 = lax.dot_general(Qs, Ks, (((2,), (2,)), ((0,), (0,))), preferred_element_type=F32)
                att = jnp.sum(att7 * mk, axis=0)
                o = or_ref[sl, cols]
                dO = do_ref[sl, cols]
                sg = _sigmoid(ga)
                r = lax.rsqrt(jnp.mean(o * o, axis=-1, keepdims=True) + RMS_EPS)
                xh = o * r
                dga_ref[sl, cols] = (dO * (xh * gn) * (sg * (1.0 + ga * (1.0 - sg)))).astype(BF16)
                don = dO * (ga * sg)
                dgn_ref[hh] += jnp.sum(don * xh, axis=0, keepdims=True)
                dxh = don * gn
                do = r * (dxh - xh * jnp.mean(dxh * xh, axis=-1, keepdims=True))
                dob = do.astype(BF16)
                st = st_ref[hh, c]
                dst = dstate[hh]
                dstb = dst.astype(BF16)
                vb = v.astype(BF16)
                eG = jnp.exp(G)
                g_last = G[C - 1:C, :]
                e_last = jnp.exp(g_last)
                e_tail = jnp.exp(g_last - G)
                qg = (q * eG).astype(BF16)
                kh = (k * e_tail).astype(BF16)
                dq_inter = jnp.dot(dob, st.astype(BF16), preferred_element_type=F32) * eG
                dk_inter = jnp.dot(vb, dstb, preferred_element_type=F32) * e_tail
                dv = lax.dot_general(kh, dstb, (((1,), (1,)), ((), ())), preferred_element_type=F32)
                dv = dv + lax.dot_general(att.astype(BF16), dob, (((0,), (0,)), ((), ())),
                                          preferred_element_type=F32)
                dA = lax.dot_general(dob, vb, (((1,), (1,)), ((), ())), preferred_element_type=F32)
                dA7 = (dA[None] * mk).astype(BF16)
                dAT7 = (dA.T[None] * mk).astype(BF16)
                dQs = lax.dot_general(dA7, Ks, (((2,), (1,)), ((0,), (0,))), preferred_element_type=F32)
                dKs = lax.dot_general(dAT7, Qs, (((2,), (1,)), ((0,), (0,))), preferred_element_type=F32)
                dq = dq_inter + jnp.sum(dQs * eq, axis=0)
                dk = dk_inter + jnp.sum(dKs * ek, axis=0)
                dG = (jnp.sum(Qs.astype(F32) * dQs - Ks.astype(F32) * dKs, axis=0)
                      + q * dq_inter - k * dk_inter)
                last_extra = (jnp.sum(k * dk_inter, axis=0, keepdims=True)
                              + e_last * jnp.sum(dst * st, axis=0, keepdims=True))
                is_last = lax.broadcasted_iota(jnp.int32, (C, 1), 0) == C - 1
                dG = dG + jnp.where(is_last, last_extra, 0.0)
                dg = jnp.dot(tr_ref[...], _split3(dG), preferred_element_type=F32)
                df = dg / f - dk
                dfa_ref[sl, cols] = (df * (1.0 - lbv) * (sf * (1.0 - sf))).astype(BF16)
                dlb_ref[:, cols] += jnp.sum(df * (1.0 - sf), axis=0, keepdims=True)
                dqa_ref[sl, cols] = (dq * (sq * (1.0 + qa * (1.0 - sq)))).astype(BF16)
                dia_ref[sl, cols] = dv.astype(BF16)
                dstate[hh] = dst * e_last + lax.dot_general(dob, qg, (((0,), (0,)), ((), ())),
                                                            preferred_element_type=F32)
            return carry

        lax.fori_loop(0, ncs, chunk, 0)

    hpb = HG_HEADS_PER_BLOCK
    wide = hpb * HEAD

    def col(m0):
        return pl.BlockSpec((ts, wide), lambda h, t: (nt - 1 - t, m0 // hpb + h))

    const3 = lambda shape: pl.BlockSpec(shape, lambda h, t: (0, 0, 0))
    const2 = lambda shape: pl.BlockSpec(shape, lambda h, t: (0, 0))
    ocol = pl.BlockSpec((ts, wide), lambda h, t: (nt - 1 - t, h))
    half = nh * HEAD
    return pl.pallas_call(
        body, grid=(nh // hpb, nt), name="hgrn_bwd",
        in_specs=[col(0), col(0), col(nh), col(2 * nh), col(3 * nh), col(0),
                  pl.BlockSpec((hpb, ncs, HEAD, HEAD), lambda h, t: (h, nt - 1 - t, 0, 0)),
                  pl.BlockSpec((1, wide), lambda h, t: (0, h)), const2((1, HEAD)),
                  const2(mstack3.shape), const3(masks.shape), const3(upq.shape), const3(upk.shape),
                  const2(trirev3.shape)],
        out_specs=[ocol, ocol, ocol, ocol, pl.BlockSpec((1, wide), lambda h, t: (0, h)),
                   pl.BlockSpec((hpb, 1, HEAD), lambda h, t: (h, 0, 0))],
        out_shape=[jax.ShapeDtypeStruct((S, half), BF16)] * 4
                  + [jax.ShapeDtypeStruct((1, half), F32), jax.ShapeDtypeStruct((nh, 1, HEAD), F32)],
        scratch_shapes=[pltpu.VMEM((hpb, HEAD, HEAD), F32)],
        compiler_params=_params(("arbitrary", "arbitrary")),
    )(dcat, proj, proj, proj, proj, oraw, states, lb, hg_norm, mstack3, masks, upq, upk, trirev3)


SB_SUB = 128
LOG2_E = 1.4426950408889634
SB_SCALE = 1.0 / math.sqrt(HEAD)
SB_QUERY_SCALE = SB_SCALE * LOG2_E


def _split2(x):
    hi = x.astype(BF16)
    lo = (x - hi.astype(F32)).astype(BF16)
    return jnp.concatenate([hi, lo], axis=1)


def _sb_constants():
    j = np.arange(SB_SUB)
    after = (j[:, None] > j[None, :]).astype(np.float32)
    before = (j[:, None] < j[None, :]).astype(np.float32)
    return (jnp.asarray(np.concatenate([after, after], axis=0), BF16),
            jnp.asarray(np.concatenate([before, before], axis=0), BF16))


def _sb_diag_mask(t):
    return lax.broadcasted_iota(jnp.int32, (t, t), 1) < lax.broadcasted_iota(jnp.int32, (t, t), 0)


def _sb_scores(q, k_ref, col0, t):
    ks = k_ref[pl.ds(pl.multiple_of(col0, t), t), :]
    return lax.dot_general(q, ks, (((1,), (1,)), ((), ())), preferred_element_type=F32)


def _sb_weights(z, mask, run, after2):
    nsub = z.shape[1] // SB_SUB
    nz = -z
    lk = jnp.minimum(nz, 0.0) - jnp.log(1.0 + jnp.exp2(jnp.minimum(z, nz))) * LOG2_E
    if mask is not None:
        lk = jnp.where(mask, lk, 0.0)
    locs, tots = [], []
    for b in range(nsub):
        lkb = lk[:, b * SB_SUB:(b + 1) * SB_SUB]
        loc = jnp.dot(_split2(lkb), after2, preferred_element_type=F32)
        locs.append(loc)
        tots.append(loc[:, 0:1] + lkb[:, 0:1])
    ws = [None] * nsub
    for b in reversed(range(nsub)):
        sl = slice(b * SB_SUB, (b + 1) * SB_SUB)
        ws[b] = jnp.exp2(z[:, sl] + lk[:, sl] + (locs[b] + run))
        run = run + tots[b]
    w = jnp.concatenate(ws, axis=1)
    if mask is not None:
        w = jnp.where(mask, w, 0.0)
    return w, run


def sb_forward(projb, nh, m0, t=512):
    S = projb.shape[0]
    after2, _ = _sb_constants()

    def body(q_ref, k_ref, v_ref, af_ref, o_ref):
        i = pl.program_id(1)
        q = q_ref[...]
        after = af_ref[...]

        def block(jb, run, mask):
            z = _sb_scores(q, k_ref, jb * t, t)
            w, run = _sb_weights(z, mask, run, after)
            vs = v_ref[pl.ds(pl.multiple_of(jb * t, t), t), :]
            return run, jnp.dot(w.astype(BF16), vs, preferred_element_type=F32)

        run, acc = block(i, jnp.zeros((t, 1), F32), _sb_diag_mask(t))

        def step(n, carry):
            run, acc = carry
            run, part = block(i - 1 - n, run, None)
            return run, acc + part

        _, acc = lax.fori_loop(0, i, step, (run, acc))
        o_ref[...] = acc.astype(BF16)

    return pl.pallas_call(
        body, grid=(nh, S // t), name="sb_fwd",
        in_specs=[pl.BlockSpec((t, HEAD), lambda h, i: (i, m0 + h)),
                  pl.BlockSpec((S, HEAD), lambda h, i: (0, m0 + nh + h)),
                  pl.BlockSpec((S, HEAD), lambda h, i: (0, m0 + 2 * nh + h)),
                  pl.BlockSpec(after2.shape, lambda h, i: (0, 0))],
        out_specs=pl.BlockSpec((t, HEAD), lambda h, i: (i, h)),
        out_shape=jax.ShapeDtypeStruct((S, nh * HEAD), BF16),
        compiler_params=_params(("arbitrary", "arbitrary")),
    )(projb, projb, projb, after2)


def sb_backward(dcat, projb, nh, m0, t=512):
    S = projb.shape[0]
    after2, before2 = _sb_constants()
    n_i = S // t
    nsub = t // SB_SUB

    def body(do_ref, q_ref, k_ref, v_ref, af_ref, bf_ref, dq_ref, dk_ref, dv_ref, dbuf, dk_acc, dv_acc):
        i = pl.program_id(1)

        @pl.when(i == 0)
        def _():
            dk_acc[...] = jnp.zeros_like(dk_acc)
            dv_acc[...] = jnp.zeros_like(dv_acc)

        q = q_ref[...]
        dob = do_ref[...].astype(BF16)
        after = af_ref[...]
        before = bf_ref[...]

        def right_to_left(jb, run, mask):
            ksl = pl.ds(pl.multiple_of(jb * t, t), t)
            z = _sb_scores(q, k_ref, jb * t, t)
            w, run = _sb_weights(z, mask, run, after)
            dw = lax.dot_general(dob, v_ref[ksl, :], (((1,), (1,)), ((), ())), preferred_element_type=F32)
            dbuf[jb] = dw * w
            dv_acc[ksl, :] += lax.dot_general(w.astype(BF16), dob, (((0,), (0,)), ((), ())),
                                              preferred_element_type=F32)
            return run

        run = right_to_left(i, jnp.zeros((t, 1), F32), _sb_diag_mask(t))
        lax.fori_loop(0, i, lambda n, run: right_to_left(i - 1 - n, run, None), run)

        def left_to_right(jb, run, dq, mask):
            ksl = pl.ds(pl.multiple_of(jb * t, t), t)
            z = _sb_scores(q, k_ref, jb * t, t)
            d = dbuf[jb]
            sig = 1.0 / (1.0 + jnp.exp2(-z))
            das = []
            for b in range(nsub):
                db = d[:, b * SB_SUB:(b + 1) * SB_SUB]
                prefix = run + jnp.dot(_split2(db), before, preferred_element_type=F32)
                das.append(db - sig[:, b * SB_SUB:(b + 1) * SB_SUB] * (db + prefix))
                run = prefix[:, SB_SUB - 1:SB_SUB] + db[:, SB_SUB - 1:SB_SUB]
            da = jnp.concatenate(das, axis=1)
            if mask is not None:
                da = jnp.where(mask, da, 0.0)
            dab = (da * SB_SCALE).astype(BF16)
            dq = dq + jnp.dot(dab, k_ref[ksl, :], preferred_element_type=F32)
            dk_acc[ksl, :] += lax.dot_general(dab, q, (((0,), (0,)), ((), ())), preferred_element_type=F32)
            return run, dq

        run, dq = lax.fori_loop(0, i, lambda jb, c: left_to_right(jb, c[0], c[1], None),
                                (jnp.zeros((t, 1), F32), jnp.zeros((t, HEAD), F32)))
        _, dq = left_to_right(i, run, dq, _sb_diag_mask(t))
        dq_ref[...] = dq.astype(BF16)

        @pl.when(i == n_i - 1)
        def _():
            dk_ref[...] = (dk_acc[...] * (1.0 / SB_QUERY_SCALE)).astype(BF16)
            dv_ref[...] = dv_acc[...].astype(BF16)

    half = nh * HEAD
    full = pl.BlockSpec((S, HEAD), lambda h, i: (0, h))
    return pl.pallas_call(
        body, grid=(nh, n_i), name="sb_bwd",
        in_specs=[pl.BlockSpec((t, HEAD), lambda h, i: (i, nh + h)),
                  pl.BlockSpec((t, HEAD), lambda h, i: (i, m0 + h)),
                  pl.BlockSpec((S, HEAD), lambda h, i: (0, m0 + nh + h)),
                  pl.BlockSpec((S, HEAD), lambda h, i: (0, m0 + 2 * nh + h)),
                  pl.BlockSpec(after2.shape, lambda h, i: (0, 0)), pl.BlockSpec(before2.shape, lambda h, i: (0, 0))],
        out_specs=[pl.BlockSpec((t, HEAD), lambda h, i: (i, h)), full, full],
        out_shape=[jax.ShapeDtypeStruct((S, half), BF16)] * 3,
        scratch_shapes=[pltpu.VMEM((n_i, t, t), F32), pltpu.VMEM((S, HEAD), F32), pltpu.VMEM((S, HEAD), F32)],
        compiler_params=_params(("arbitrary", "arbitrary")),
    )(dcat, projb, projb, projb, after2, before2)


def local_step(x, target, mix_norm, ffn_norm, final_norm, lb_logits, hg_norm, get_w_in, get_w_rest, send):
    S, D = x.shape
    half = D // 2
    nh = half // HEAD
    tm = 512
    tk = 2048
    row = lambda i, j: (i, 0)

    lb = jax.nn.softmax(lb_logits, axis=0)[0:1]

    h0, r0 = rms_fwd(x, mix_norm[0:1], BF16)
    w_in = get_w_in(h0)
    nbi = w_in.shape[2]
    col = jnp.arange(N_DEV * nbi) // half
    col_scale = jnp.where(col == 4, SB_QUERY_SCALE, 1.0).astype(F32)[None]
    proj, projb = matmul(
        "proj_in", [h0], [w_in], grid=(N_DEV, S // tm, 1),
        a_spec=pl.BlockSpec((tm, D), lambda j, i, k: (i, 0)),
        b_spec=pl.BlockSpec((None, D, nbi), lambda j, i, k: (j, 0, 0)),
        out_spec=pl.BlockSpec((tm, nbi), lambda j, i, k: (i, j)), out_shape=(S, N_DEV * nbi),
        out_dtypes=[F32, BF16], acc_shape=(8, 128),
        bf16_scale=col_scale, bf16_scale_spec=pl.BlockSpec((1, nbi), lambda j, i, k: (0, j)))
    oa, oraw, states = hgrn_forward(proj, lb, hg_norm)
    ob = sb_forward(projb, nh, 4 * nh)
    cat = jnp.concatenate([oa, ob], axis=1)
    w_out, pool_w, pool_scale, wg, wu, wd = get_w_rest(cat)
    (x1,) = matmul(
        "mix_out", [cat], [w_out], grid=(S // tm, 1),
        a_spec=pl.BlockSpec((tm, D), row), b_spec=pl.BlockSpec((D, D), lambda i, k: (0, 0)),
        out_spec=pl.BlockSpec((tm, D), row), out_shape=(S, D), out_dtypes=[F32], acc_shape=(8, 128),
        res=x, res_spec=pl.BlockSpec((tm, D), row))
    h1, r1 = rms_fwd(x1, ffn_norm[0:1], BF16)
    x2, ffn0 = ffn_forward(h1, x1, wg[0], wu[0], wd[0])

    h2, r2 = rms_fwd(x2, mix_norm[1:2], F32)
    x3, pooled = pool_forward(h2, x2, pool_w, pool_scale)
    h3, r3 = rms_fwd(x3, ffn_norm[1:2], BF16)
    x4, ffn1 = ffn_forward(h3, x3, wg[1], wu[1], wd[1])

    loss_blk, dx4, dx4b, d_final = loss_and_final_bwd(x4, final_norm, target)

    dh3, dwg1, dwu1, dwd1 = ffn_backward(dx4b, h3, ffn1, wg[1], wu[1], wd[1])
    dh3 = send("ffn1", dict(ffn_w_gate_1=dwg1, ffn_w_up_1=dwu1, ffn_w_down_1=dwd1), dh3)
    dx3, _, d_ffn1 = rms_bwd(dh3, x3, r3, ffn_norm[1:2], dx4)
    dmixed, dpooled, d_pscale = pool_backward_mix(dx3, pooled, pool_w, pool_scale)
    G = len(POOL_WINDOWS)
    P = D // G
    (d_pool_w,) = matmul(
        "pool_dw", [pooled], [dmixed], grid=(G, S // tk),
        a_spec=pl.BlockSpec((tk, P), lambda g, k: (k, g)), b_spec=pl.BlockSpec((tk, P), lambda g, k: (k, g)),
        out_spec=pl.BlockSpec((None, P, P), lambda g, k: (g, 0, 0)), out_shape=(G, P, P), out_dtypes=[BF16],
        acc_shape=(P, P), trans_a=True)
    dh2 = pool_backward_window(dpooled)
    dx2, dx2b, d_mix1 = rms_bwd(dh2, x2, r2, mix_norm[1:2], dx3)

    dh1, dwg0, dwu0, dwd0 = ffn_backward(dx2b, h1, ffn0, wg[0], wu[0], wd[0])
    dx1, dx1b, d_ffn0 = rms_bwd(dh1, x1, r1, ffn_norm[0:1], dx2)
    (dcat,) = matmul(
        "mix_out_dx", [dx1b], [w_out], grid=(S // tm, 1),
        a_spec=pl.BlockSpec((tm, D), row), b_spec=pl.BlockSpec((D, D), lambda i, k: (0, 0)),
        out_spec=pl.BlockSpec((tm, D), row), out_shape=(S, D), out_dtypes=[F32], acc_shape=(8, 128),
        trans_b=True)
    (d_w_out,) = matmul(
        "mix_out_dw", [cat], [dx1b], grid=(2, S // tk),
        a_spec=pl.BlockSpec((tk, half), lambda m, k: (k, m)), b_spec=pl.BlockSpec((tk, D), lambda m, k: (k, 0)),
        out_spec=pl.BlockSpec((half, D), lambda m, k: (m, 0)), out_shape=(D, D), out_dtypes=[BF16],
        acc_shape=(half, D), trans_a=True)
    dcat = send("layer0", dict(ffn_w_gate_0=dwg0, ffn_w_up_0=dwu0, ffn_w_down_0=dwd0, pool_w=d_pool_w,
                               ab_w_out=d_w_out), dcat)
    dqa, dfa, dia, dga, d_lb, d_hg = hgrn_backward(dcat, proj, oraw, states, lb, hg_norm)
    dqb, dkb, dvb = sb_backward(dcat, projb, nh, 4 * nh)
    dproj = jnp.concatenate([dqa, dfa, dia, dga, dqb, dkb, dvb], axis=1)
    (d_w_in,) = matmul(
        "proj_in_dw", [h0], [dproj], grid=(N_DEV, S // tk),
        a_spec=pl.BlockSpec((tk, D), lambda j, k: (k, 0)), b_spec=pl.BlockSpec((tk, nbi), lambda j, k: (k, j)),
        out_spec=pl.BlockSpec((None, D, nbi), lambda j, k: (j, 0, 0)), out_shape=(N_DEV, D, nbi),
        out_dtypes=[BF16], acc_shape=(D, nbi), trans_a=True)
    dproj = send("w_in", dict(ab_w_in=d_w_in), dproj)
    (dh0,) = matmul(
        "proj_in_dx", [dproj], [w_in], grid=(S // tm, N_DEV),
        a_spec=pl.BlockSpec((tm, nbi), lambda i, j: (i, j)),
        b_spec=pl.BlockSpec((None, D, nbi), lambda i, j: (j, 0, 0)),
        out_spec=pl.BlockSpec((tm, D), row), out_shape=(S, D), out_dtypes=[F32], acc_shape=(tm, D),
        trans_b=True)
    dx0, _, d_mix0 = rms_bwd(dh0, x, r0, mix_norm[0:1], dx1)

    d_l0 = d_lb * lb * (1.0 - lb)
    small = dict(
        loss=loss_blk[0:1, 0:1],
        mix_norm=jnp.concatenate([d_mix0, d_mix1], axis=0),
        ffn_norm=jnp.concatenate([d_ffn0, d_ffn1], axis=0),
        final_norm=d_final,
        lb_logits=jnp.concatenate([d_l0, -d_l0], axis=0),
        hg_out_norm=jnp.sum(d_hg, axis=0),
        pool_scale=d_pscale,
    )
    return dx0, small


def _my_index():
    return 4 * lax.axis_index("x") + 2 * lax.axis_index("y") + lax.axis_index("c")


def _peer(r):
    x, y, c = lax.axis_index("x"), lax.axis_index("y"), lax.axis_index("c")
    px = 1 - x if (r >> 2) & 1 else x
    py = 1 - y if (r >> 1) & 1 else y
    pc = 1 - c if r & 1 else c
    return (px, py, pc), 4 * px + 2 * py + pc


def exchange(name, arrays, gather):
    n = len(arrays)
    n_peers = N_DEV - 1

    def body(*refs):
        ins, outs = refs[:n], refs[n:2 * n]
        send_sems, recv_sems, local_sems = refs[2 * n:]
        me = _my_index()
        local = []
        for a in range(n):
            src = ins[a] if gather else ins[a].at[me]
            cp = pltpu.make_async_copy(src, outs[a].at[me], local_sems.at[a])
            cp.start()
            local.append(cp)
        remote = []
        for a in range(n):
            for r in range(1, N_DEV):
                peer, pidx = _peer(r)
                src = ins[a] if gather else ins[a].at[pidx]
                cp = pltpu.make_async_remote_copy(
                    src_ref=src, dst_ref=outs[a].at[me], send_sem=send_sems.at[a * n_peers + r - 1],
                    recv_sem=recv_sems.at[a * n_peers + r - 1], device_id=peer, device_id_type=MESH)
                cp.start()
                remote.append((cp, a, r))
        for cp, a, r in remote:
            _, pidx = _peer(r)
            src = ins[a] if gather else ins[a].at[pidx]
            pltpu.make_async_remote_copy(
                src_ref=src, dst_ref=outs[a].at[pidx], send_sem=send_sems.at[a * n_peers + r - 1],
                recv_sem=recv_sems.at[a * n_peers + r - 1], device_id=_peer(r)[0], device_id_type=MESH).wait_recv()
        for cp, a, r in remote:
            cp.wait_send()
        for cp in local:
            cp.wait()

    out_shape = [jax.ShapeDtypeStruct(((N_DEV,) + a.shape) if gather else a.shape, a.dtype) for a in arrays]
    any_spec = pl.BlockSpec(memory_space=pl.ANY)
    return pl.pallas_call(
        body, name=name, in_specs=[any_spec] * n, out_specs=[any_spec] * n, out_shape=out_shape,
        scratch_shapes=[pltpu.SemaphoreType.DMA((n * n_peers,)), pltpu.SemaphoreType.DMA((n * n_peers,)),
                        pltpu.SemaphoreType.DMA((n,))],
    )(*arrays)


_HBM = pl.BlockSpec(memory_space=pltpu.HBM)
_SEM = pl.BlockSpec(memory_space=pltpu.SEMAPHORE)
_EFFECT = pltpu.SideEffectType.DATAFLOW_SIDE_EFFECTING


def _landing(arrays, gather):
    me = _my_index()
    lands = []
    for a in arrays:
        own = a[None] if gather else lax.dynamic_slice_in_dim(a, me, 1, axis=0)
        shape = ((N_DEV,) + a.shape) if gather else a.shape
        lands.append(lax.dynamic_update_slice_in_dim(lax.empty(shape, a.dtype), own, me, axis=0))
    return lands


def exchange_start(name, arrays, gather, carry):
    n = len(arrays)
    n_peers = N_DEV - 1
    lands = _landing(arrays, gather)
    n_thru = 2 * n + 1

    def body(*refs):
        src, land = refs[:n], refs[n:2 * n]
        send_sems, recv_sems = refs[n_thru], refs[n_thru + 1]
        token = refs[-1]
        me = _my_index()
        for a in range(n):
            for r in range(1, N_DEV):
                peer, pidx = _peer(r)
                pltpu.make_async_remote_copy(
                    src_ref=src[a] if gather else src[a].at[pidx], dst_ref=land[a].at[me],
                    send_sem=send_sems.at[a * n_peers + r - 1], recv_sem=recv_sems.at[a * n_peers + r - 1],
                    device_id=peer, device_id_type=MESH).start()
        token[...] = jnp.zeros_like(token)

    operands = list(arrays) + lands + [carry]
    outs = pl.pallas_call(
        body, name=name,
        out_shape=(pltpu.SemaphoreType.DMA((n * n_peers,)), pltpu.SemaphoreType.DMA((n * n_peers,)),
                   *[pltpu.HBM(a.shape, a.dtype) for a in operands], jax.ShapeDtypeStruct((8, 128), F32)),
        in_specs=[_HBM] * n_thru,
        out_specs=(_SEM, _SEM, *([_HBM] * n_thru), pl.BlockSpec(memory_space=pltpu.VMEM)),
        input_output_aliases={i: 2 + i for i in range(n_thru)},
        compiler_params=pltpu.CompilerParams(has_side_effects=_EFFECT),
    )(*[pltpu.with_memory_space_constraint(a, pltpu.HBM) for a in operands])
    handle = (outs[0], outs[1], list(outs[2:2 + n]), list(outs[2 + n:2 + 2 * n]), gather)
    return handle, outs[2 + 2 * n]


def exchange_wait(name, handle, after):
    send_sems, recv_sems, srcs, lands, gather = handle
    n = len(srcs)
    n_peers = N_DEV - 1

    def body(*refs):
        src, land = refs[:n], refs[n:2 * n]
        send_s, recv_s = refs[2 * n], refs[2 * n + 1]
        for a in range(n):
            for r in range(1, N_DEV):
                peer, pidx = _peer(r)
                cp = pltpu.make_async_remote_copy(
                    src_ref=src[a] if gather else src[a].at[pidx], dst_ref=land[a].at[pidx],
                    send_sem=send_s.at[a * n_peers + r - 1], recv_sem=recv_s.at[a * n_peers + r - 1],
                    device_id=peer, device_id_type=MESH)
                cp.wait_send()
                cp.wait_recv()

    shapes = [pltpu.HBM(a.shape, a.dtype) for a in srcs] + [pltpu.HBM(l.shape, l.dtype) for l in lands]
    outs = pl.pallas_call(
        body, name=name, out_shape=tuple(shapes),
        in_specs=[_HBM] * (2 * n) + [_SEM, _SEM, pl.BlockSpec(memory_space=pl.ANY)],
        out_specs=tuple([_HBM] * (2 * n)),
        input_output_aliases={i: i for i in range(2 * n)},
        compiler_params=pltpu.CompilerParams(has_side_effects=_EFFECT),
    )(*srcs, *lands, send_sems, recv_sems, after)
    return list(outs[n:])


def _row_tile(rows, cap=256):
    best = None
    for t in range(16, min(rows, cap) + 1, 16):
        if rows % t == 0:
            best = t
    return best if best is not None else rows


def sum_slots(name, recv):
    n, R, C = recv.shape
    tr = _row_tile(R)

    def body(r_ref, o_ref):
        g = r_ref[0].astype(F32)
        for d in range(1, n):
            g = g + r_ref[d].astype(F32)
        o_ref[...] = g

    return pl.pallas_call(
        body, grid=(R // tr,), name=name,
        in_specs=[pl.BlockSpec((n, tr, C), lambda i: (0, i, 0))],
        out_specs=pl.BlockSpec((tr, C), lambda i: (i, 0)),
        out_shape=jax.ShapeDtypeStruct((R, C), F32),
        compiler_params=_params(("arbitrary",)),
    )(recv)


def adamw(name, recv, w, m, v, layer=None, prev=None):
    n, R, C = recv.shape
    tr = _row_tile(R)

    def body(r_ref, w_ref, m_ref, v_ref, *rest):
        g_ref, d_ref, nm_ref, nv_ref = rest[-4:]
        g = r_ref[0].astype(F32)
        for d in range(1, n):
            g = g + r_ref[d].astype(F32)
        mm = ADAM_B1 * m_ref[...] + (1.0 - ADAM_B1) * g
        vv = ADAM_B2 * v_ref[...] + (1.0 - ADAM_B2) * (g * g)
        m_hat = mm / (1.0 - ADAM_B1 ** ADAM_STEP)
        v_hat = vv / (1.0 - ADAM_B2 ** ADAM_STEP)
        g_ref[...] = g
        d_ref[...] = -ADAM_LR * (m_hat / (jnp.sqrt(v_hat) + ADAM_EPS) + ADAM_WD * w_ref[...])
        nm_ref[...] = mm
        nv_ref[...] = vv

    if layer is None:
        row = pl.BlockSpec((tr, C), lambda i: (i, 0))
        shape = (R, C)
    else:
        row = pl.BlockSpec((None, tr, C), lambda i: (layer, i, 0))
        shape = w.shape
    prev = [] if prev is None else list(prev)
    return pl.pallas_call(
        body, grid=(R // tr,), name=name,
        in_specs=[pl.BlockSpec((n, tr, C), lambda i: (0, i, 0)), row, row, row]
                 + [pl.BlockSpec(memory_space=pl.ANY)] * len(prev),
        out_specs=[row] * 4,
        out_shape=[jax.ShapeDtypeStruct(shape, F32)] * 4,
        input_output_aliases={4 + o: o for o in range(len(prev))},
        compiler_params=_params(("arbitrary",)),
    )(recv, w, m, v, *prev)


def _adamw_nd(name, recv, w, m, v):
    shp = w.shape
    C = shp[-1]
    flat = lambda a: a.reshape(-1, C)
    outs = adamw(name, recv.reshape(recv.shape[0], -1, C), flat(w), flat(m), flat(v))
    return [o.reshape(shp) for o in outs]


_SMALL_NAMES = ("loss", "mix_norm", "ffn_norm", "final_norm", "lb_logits", "hg_out_norm", "pool_scale")
_LANES = 128


def _pack_small(parts):
    rows, layout = [], {}
    at = 0
    for name in parts:
        flat = parts[name].reshape(-1).astype(F32)
        n_rows = -(-flat.shape[0] // (8 * _LANES)) * 8
        flat = jnp.pad(flat, (0, n_rows * _LANES - flat.shape[0]))
        rows.append(flat.reshape(n_rows, _LANES))
        layout[name] = (at, parts[name].shape)
        at += n_rows
    return jnp.concatenate(rows, axis=0), layout


def _unpack_small(pack, layout):
    out = {}
    for name, (at, shape) in layout.items():
        size = int(np.prod(shape))
        n_rows = -(-size // _LANES)
        out[name] = pack[at:at + n_rows].reshape(-1)[:size].reshape(shape)
    return out


def kernel(x, mix_norm, ffn_norm, final_norm, ab_w_in, lb_logits, hg_out_norm, ab_w_out, pool_w, pool_scale, ffn_w_gate, ffn_w_up, ffn_w_down, loss_target, m_mix_norm, m_ffn_norm, m_final_norm, m_ab_w_in, m_lb_logits, m_hg_out_norm, m_ab_w_out, m_pool_w, m_pool_scale, m_ffn_w_gate, m_ffn_w_up, m_ffn_w_down, v_mix_norm, v_ffn_norm, v_final_norm, v_ab_w_in, v_lb_logits, v_hg_out_norm, v_ab_w_out, v_pool_w, v_pool_scale, v_ffn_w_gate, v_ffn_w_up, v_ffn_w_down):
    D = x.shape[-1]
    n_layers = ffn_w_gate.shape[0]
    G = pool_w.shape[1]
    P = pool_w.shape[3]
    me = _my_index()

    in_handle, mix_norm_after = exchange_start("gather_w_in_start", [ab_w_in[0].astype(BF16)], True, mix_norm)
    rest = [ab_w_out[0], pool_w[0]]
    for l in range(n_layers):
        rest += [ffn_w_gate[l], ffn_w_up[l], ffn_w_down[l]]
    rest = [s.astype(BF16) for s in rest] + [pool_scale]
    rest_handle = []

    def get_w_in(after):
        w_in = exchange_wait("gather_w_in_wait", in_handle, after)[0]
        handle, w_in = exchange_start("gather_rest_start", rest, True, w_in)
        rest_handle.append(handle)
        return w_in

    def get_w_rest(after):
        got = exchange_wait("gather_rest_wait", rest_handle[0], after)
        w_out_g = got[0].reshape(D, D)
        pool_g = got[1].transpose(1, 0, 2, 3).reshape(G, P, P)
        wg = [got[2 + 3 * l] for l in range(n_layers)]
        wu = [got[3 + 3 * l] for l in range(n_layers)]
        wd = [got[4 + 3 * l] for l in range(n_layers)]
        return w_out_g, pool_g, got[-1].reshape(1, D), wg, wu, wd

    in_flight = []

    def send(tag, grads, carry):
        if "pool_w" in grads:
            grads = dict(grads, pool_w=grads["pool_w"].reshape(G, N_DEV, P // N_DEV, P).transpose(1, 0, 2, 3))
        if "ab_w_out" in grads:
            grads = dict(grads, ab_w_out=grads["ab_w_out"].reshape(N_DEV, D // N_DEV, D))
        handle, carry = exchange_start("grads_" + tag + "_start", list(grads.values()), False, carry)
        in_flight.append((tag, list(grads.keys()), handle))
        return carry

    dx0, small = local_step(x[0], loss_target[0], mix_norm_after, ffn_norm, final_norm[None],
                            lb_logits, hg_out_norm, get_w_in, get_w_rest, send)

    recv = {}
    for tag, names, handle in in_flight:
        recv.update(zip(names, exchange_wait("grads_" + tag + "_wait", handle, dx0)))
    small_pack, layout = _pack_small({k: small[k] for k in _SMALL_NAMES})
    (small_all,) = exchange("gather_small", [small_pack], gather=True)
    tot = _unpack_small(sum_slots("sum_small", small_all), layout)

    res = {}
    res["ab_w_in"] = _adamw_nd("adamw_w_in", recv["ab_w_in"], ab_w_in, m_ab_w_in, v_ab_w_in)
    res["ab_w_out"] = _adamw_nd("adamw_w_out", recv["ab_w_out"], ab_w_out, m_ab_w_out, v_ab_w_out)
    res["pool_w"] = _adamw_nd("adamw_pool_w", recv["pool_w"], pool_w, m_pool_w, v_pool_w)
    ffn_in = {"ffn_w_gate": (ffn_w_gate, m_ffn_w_gate, v_ffn_w_gate),
              "ffn_w_up": (ffn_w_up, m_ffn_w_up, v_ffn_w_up),
              "ffn_w_down": (ffn_w_down, m_ffn_w_down, v_ffn_w_down)}
    for name, (w, m, v) in ffn_in.items():
        outs = None
        for l in range(n_layers):
            outs = adamw("adamw_" + name, recv[name + "_" + str(l)], w, m, v, layer=l, prev=outs)
        res[name] = outs

    n_ps = pool_scale.shape[1]
    small_g = dict(tot)
    small_g["pool_scale"] = lax.dynamic_slice(tot["pool_scale"], (0, me * n_ps), (1, n_ps))
    small_w = dict(mix_norm=(mix_norm, m_mix_norm, v_mix_norm), ffn_norm=(ffn_norm, m_ffn_norm, v_ffn_norm),
                   final_norm=(final_norm, m_final_norm, v_final_norm),
                   lb_logits=(lb_logits, m_lb_logits, v_lb_logits),
                   hg_out_norm=(hg_out_norm, m_hg_out_norm, v_hg_out_norm),
                   pool_scale=(pool_scale, m_pool_scale, v_pool_scale))
    g_pack, lay2 = _pack_small({k: small_g[k].reshape(small_w[k][0].shape) for k in small_w})
    w_pack, _ = _pack_small({k: small_w[k][0] for k in small_w})
    m_pack, _ = _pack_small({k: small_w[k][1] for k in small_w})
    v_pack, _ = _pack_small({k: small_w[k][2] for k in small_w})
    small_out = [_unpack_small(o, lay2) for o in adamw("adamw_small", g_pack[None], w_pack, m_pack, v_pack)]
    for k in small_w:
        res[k] = [small_out[o][k] for o in range(4)]

    order = ("mix_norm", "ffn_norm", "final_norm", "ab_w_in", "lb_logits", "hg_out_norm", "ab_w_out", "pool_w",
             "pool_scale", "ffn_w_gate", "ffn_w_up", "ffn_w_down")
    outs = [tot["loss"].reshape(()), dx0[None]]
    for o in range(4):
        outs += [res[k][o] for k in order]
    return tuple(outs)
```

```python
import functools
import math

import numpy as np
import jax
import jax.numpy as jnp
from jax import lax
from jax.experimental import pallas as pl
from jax.experimental.pallas import tpu as pltpu

F32 = jnp.float32
BF16 = jnp.bfloat16

N_DEV = 8
RMS_EPS = 1e-6
HEAD = 128
HG_CHUNK = 64
HG_HEADS_PER_BLOCK = 4
POOL_WINDOWS = (2, 4, 8, 16)
POOL_HALO = 16
ADAM_LR, ADAM_B1, ADAM_B2, ADAM_EPS, ADAM_WD, ADAM_STEP = 0.001, 0.9, 0.999, 1e-08, 0.01, 10
VMEM_LIMIT_BYTES = 60 * 1024 * 1024
MESH = pl.DeviceIdType.MESH


def _params(sem):
    return pltpu.CompilerParams(dimension_semantics=sem, vmem_limit_bytes=VMEM_LIMIT_BYTES)


def _sigmoid(x):
    return 1.0 / (1.0 + jnp.exp(-x))


def rms_fwd(x, gain, out_dtype, ts=512):
    S, D = x.shape

    def body(x_ref, g_ref, h_ref, r_ref):
        xv = x_ref[...]
        r = lax.rsqrt(jnp.mean(xv * xv, axis=-1, keepdims=True) + RMS_EPS)
        h_ref[...] = ((xv * r) * g_ref[...]).astype(h_ref.dtype)
        r_ref[...] = r

    return pl.pallas_call(
        body, grid=(S // ts,), name="rms_fwd",
        in_specs=[pl.BlockSpec((ts, D), lambda i: (i, 0)), pl.BlockSpec((1, D), lambda i: (0, 0))],
        out_specs=[pl.BlockSpec((ts, D), lambda i: (i, 0)), pl.BlockSpec((ts, 1), lambda i: (i, 0))],
        out_shape=[jax.ShapeDtypeStruct((S, D), out_dtype), jax.ShapeDtypeStruct((S, 1), F32)],
        compiler_params=_params(("arbitrary",)),
    )(x, gain)


def rms_bwd(dh, x, r, gain, dres, ts=512):
    S, D = x.shape

    def body(dh_ref, x_ref, r_ref, g_ref, dres_ref, dx_ref, dxb_ref, dg_ref):
        i = pl.program_id(0)
        rr = r_ref[...]
        xh = x_ref[...] * rr
        dhv = dh_ref[...]
        dxh = dhv * g_ref[...]
        dx = dres_ref[...] + rr * (dxh - xh * jnp.mean(dxh * xh, axis=-1, keepdims=True))
        dx_ref[...] = dx
        dxb_ref[...] = dx.astype(BF16)
        part = jnp.sum(dhv * xh, axis=0, keepdims=True)

        @pl.when(i == 0)
        def _():
            dg_ref[...] = part

        @pl.when(i > 0)
        def _():
            dg_ref[...] += part

    row = pl.BlockSpec((ts, D), lambda i: (i, 0))
    vec = pl.BlockSpec((1, D), lambda i: (0, 0))
    return pl.pallas_call(
        body, grid=(S // ts,), name="rms_bwd",
        in_specs=[row, row, pl.BlockSpec((ts, 1), lambda i: (i, 0)), vec, row],
        out_specs=[row, row, vec],
        out_shape=[jax.ShapeDtypeStruct((S, D), F32), jax.ShapeDtypeStruct((S, D), BF16),
                   jax.ShapeDtypeStruct((1, D), F32)],
        compiler_params=_params(("arbitrary",)),
    )(dh, x, r, gain, dres)


def loss_and_final_bwd(x, gain, target, ts=512):
    S, D = x.shape

    def body(x_ref, g_ref, t_ref, loss_ref, dx_ref, dxb_ref, dg_ref):
        i = pl.program_id(0)
        xv = x_ref[...]
        rr = lax.rsqrt(jnp.mean(xv * xv, axis=-1, keepdims=True) + RMS_EPS)
        xh = xv * rr
        err = xh * g_ref[...] - t_ref[...]
        part_loss = 0.5 * jnp.sum(jnp.mean(err * err, axis=-1, keepdims=True))
        dy = err / D
        dxh = dy * g_ref[...]
        dx = rr * (dxh - xh * jnp.mean(dxh * xh, axis=-1, keepdims=True))
        dx_ref[...] = dx
        dxb_ref[...] = dx.astype(BF16)
        part = jnp.sum(dy * xh, axis=0, keepdims=True)

        @pl.when(i == 0)
        def _():
            dg_ref[...] = part
            loss_ref[...] = jnp.zeros_like(loss_ref) + part_loss

        @pl.when(i > 0)
        def _():
            dg_ref[...] += part
            loss_ref[...] += part_loss

    row = pl.BlockSpec((ts, D), lambda i: (i, 0))
    vec = pl.BlockSpec((1, D), lambda i: (0, 0))
    return pl.pallas_call(
        body, grid=(S // ts,), name="loss_final",
        in_specs=[row, vec, row],
        out_specs=[pl.BlockSpec((8, 128), lambda i: (0, 0)), row, row, vec],
        out_shape=[jax.ShapeDtypeStruct((8, 128), F32), jax.ShapeDtypeStruct((S, D), F32),
                   jax.ShapeDtypeStruct((S, D), BF16), jax.ShapeDtypeStruct((1, D), F32)],
        compiler_params=_params(("arbitrary",)),
    )(x, gain, target)


def matmul(name, a_ops, b_ops, *, grid, a_spec, b_spec, out_spec, out_shape, out_dtypes, acc_shape,
           trans_a=False, trans_b=False, res=None, res_spec=None, bf16_scale=None, bf16_scale_spec=None):
    n_pairs = len(a_ops)
    n_out = len(out_dtypes)
    nk = grid[-1]
    kaxis = len(grid) - 1
    dn = (((0,) if trans_a else (1,), (1,) if trans_b else (0,)), ((), ()))

    def body(*refs):
        a_refs = refs[:n_pairs]
        b_refs = refs[n_pairs:2 * n_pairs]
        pos = 2 * n_pairs
        res_ref = None
        if res is not None:
            res_ref = refs[pos]
            pos += 1
        scale_ref = None
        if bf16_scale is not None:
            scale_ref = refs[pos]
            pos += 1
        out_refs = refs[pos:pos + n_out]
        acc_ref = refs[pos + n_out]
        k = pl.program_id(kaxis)
        in_place = n_out == 1 and out_dtypes[0] == F32
        target = out_refs[0] if in_place else acc_ref

        def finish(val):
            if res_ref is not None:
                val = val + res_ref[...]
            for o in out_refs:
                if scale_ref is not None and o.dtype == BF16:
                    o[...] = (val * scale_ref[...]).astype(BF16)
                else:
                    o[...] = val.astype(o.dtype)

        if nk > 1:
            @pl.when(k == 0)
            def _():
                if in_place and res_ref is not None:
                    target[...] = res_ref[...]
                else:
                    target[...] = jnp.zeros_like(target)

        part = None
        for ar, br in zip(a_refs, b_refs):
            d = lax.dot_general(ar[...].astype(BF16), br[...].astype(BF16), dn, preferred_element_type=F32)
            part = d if part is None else part + d

        if nk == 1:
            finish(part)
        else:
            target[...] += part
            if not in_place:
                @pl.when(k == nk - 1)
                def _():
                    finish(acc_ref[...])

    in_specs = [a_spec] * n_pairs + [b_spec] * n_pairs
    operands = list(a_ops) + list(b_ops)
    if res is not None:
        in_specs.append(res_spec)
        operands.append(res)
    if bf16_scale is not None:
        in_specs.append(bf16_scale_spec)
        operands.append(bf16_scale)
    return pl.pallas_call(
        body, grid=grid, name=name, in_specs=in_specs,
        out_specs=[out_spec] * n_out,
        out_shape=[jax.ShapeDtypeStruct(out_shape, dt) for dt in out_dtypes],
        scratch_shapes=[pltpu.VMEM(acc_shape, F32)],
        compiler_params=_params(("arbitrary",) * len(grid)),
    )(*operands)


def ffn_gate_up(h, wg, wu, tm=1024):
    S, D = h.shape
    nb = wg.shape[2]

    def body(h_ref, wg_ref, wu_ref, p_ref, r_ref, a_ref):
        for c in range(2):
            rows = slice(c * (tm // 2), (c + 1) * (tm // 2))
            hv = h_ref[rows, :]
            g = jnp.dot(hv, wg_ref[...], preferred_element_type=F32)
            u = jnp.dot(hv, wu_ref[...], preferred_element_type=F32)
            s = _sigmoid(g)
            p = g * s
            p_ref[rows, :] = p
            r_ref[rows, :] = u * (s * (1.0 + g * (1.0 - s)))
            a_ref[rows, :] = (p * u).astype(BF16)

    wspec = pl.BlockSpec((None, D, nb), lambda j, i: (j, 0, 0))
    ospec = pl.BlockSpec((None, tm, nb), lambda j, i: (j, i, 0))
    return pl.pallas_call(
        body, grid=(N_DEV, S // tm), name="ffn_gate_up",
        in_specs=[pl.BlockSpec((tm, D), lambda j, i: (i, 0)), wspec, wspec],
        out_specs=[ospec, ospec, ospec],
        out_shape=[jax.ShapeDtypeStruct((N_DEV, S, nb), F32), jax.ShapeDtypeStruct((N_DEV, S, nb), F32),
                   jax.ShapeDtypeStruct((N_DEV, S, nb), BF16)],
        compiler_params=_params(("arbitrary", "arbitrary")),
    )(h, wg, wu)


def ffn_bwd_hidden(dy, wd, p, r, tm=1024):
    S, D = dy.shape
    nb = wd.shape[1]

    def body(dy_ref, wd_ref, p_ref, r_ref, dg_ref, du_ref):
        for c in range(2):
            rows = slice(c * (tm // 2), (c + 1) * (tm // 2))
            da = lax.dot_general(dy_ref[rows, :], wd_ref[...], (((1,), (1,)), ((), ())),
                                 preferred_element_type=F32)
            du_ref[rows, :] = (da * p_ref[rows, :]).astype(BF16)
            dg_ref[rows, :] = (da * r_ref[rows, :]).astype(BF16)

    hspec = pl.BlockSpec((None, tm, nb), lambda j, i: (j, i, 0))
    return pl.pallas_call(
        body, grid=(N_DEV, S // tm), name="ffn_bwd_hidden",
        in_specs=[pl.BlockSpec((tm, D), lambda j, i: (i, 0)), pl.BlockSpec((None, nb, D), lambda j, i: (j, 0, 0)),
                  hspec, hspec],
        out_specs=[hspec, hspec],
        out_shape=[jax.ShapeDtypeStruct((N_DEV, S, nb), BF16), jax.ShapeDtypeStruct((N_DEV, S, nb), BF16)],
        compiler_params=_params(("arbitrary", "arbitrary")),
    )(dy, wd, p, r)


def ffn_forward(h, xres, wg, wu, wd, tm=1024):
    S, D = h.shape
    nb = wg.shape[2]
    g, u, a = ffn_gate_up(h, wg, wu)
    (xo,) = matmul(
        "ffn_down", [a], [wd], grid=(S // tm, N_DEV),
        a_spec=pl.BlockSpec((None, tm, nb), lambda i, j: (j, i, 0)),
        b_spec=pl.BlockSpec((None, nb, D), lambda i, j: (j, 0, 0)),
        out_spec=pl.BlockSpec((tm, D), lambda i, j: (i, 0)), out_shape=(S, D), out_dtypes=[F32],
        acc_shape=(tm, D), res=xres, res_spec=pl.BlockSpec((tm, D), lambda i, j: (i, 0)))
    return xo, (g, u, a)


def ffn_backward(dy_b, h, saved, wg, wu, wd, tm=1024, tk=2048):
    S, D = h.shape
    nb = wg.shape[2]
    g, u, a = saved
    dg, du = ffn_bwd_hidden(dy_b, wd, g, u)
    (dh,) = matmul(
        "ffn_dh", [dg, du], [wg, wu], grid=(S // tm, N_DEV),
        a_spec=pl.BlockSpec((None, tm, nb), lambda i, j: (j, i, 0)),
        b_spec=pl.BlockSpec((None, D, nb), lambda i, j: (j, 0, 0)),
        out_spec=pl.BlockSpec((tm, D), lambda i, j: (i, 0)), out_shape=(S, D), out_dtypes=[F32],
        acc_shape=(tm, D), trans_b=True)

    def wgrad_in(name, dhid):
        (dw,) = matmul(
            name, [h], [dhid], grid=(N_DEV, S // tk),
            a_spec=pl.BlockSpec((tk, D), lambda j, k: (k, 0)),
            b_spec=pl.BlockSpec((None, tk, nb), lambda j, k: (j, k, 0)),
            out_spec=pl.BlockSpec((None, D, nb), lambda j, k: (j, 0, 0)), out_shape=(N_DEV, D, nb),
            out_dtypes=[BF16], acc_shape=(D, nb), trans_a=True)
        return dw

    dwg = wgrad_in("ffn_dwg", dg)
    dwu = wgrad_in("ffn_dwu", du)
    (dwd,) = matmul(
        "ffn_dwd", [a], [dy_b], grid=(N_DEV, S // tk),
        a_spec=pl.BlockSpec((None, tk, nb), lambda j, k: (j, k, 0)),
        b_spec=pl.BlockSpec((tk, D), lambda j, k: (k, 0)),
        out_spec=pl.BlockSpec((None, nb, D), lambda j, k: (j, 0, 0)), out_shape=(N_DEV, nb, D),
        out_dtypes=[BF16], acc_shape=(nb, D), trans_a=True)
    return dh, dwg, dwu, dwd


def _pool_counts(row0, n, w):
    pos = row0 + lax.broadcasted_iota(jnp.int32, (n, 1), 0)
    return jnp.minimum(pos + 1, w).astype(F32)


def pool_forward(h, xres, w, scale, ts=256):
    S, D = h.shape
    G = len(POOL_WINDOWS)
    P = D // G
    hb = ts // POOL_HALO

    def body(h_ref, halo_ref, x_ref, w_ref, s_ref, xo_ref, p_ref):
        i = pl.program_id(0)
        for gi, win in enumerate(POOL_WINDOWS):
            cols = slice(gi * P, (gi + 1) * P)
            cur = h_ref[:, cols]
            halo = jnp.where(i > 0, halo_ref[:, cols], 0.0)
            acc = jnp.concatenate([halo, cur], axis=0)
            step = 1
            while step < win:
                acc = acc + pltpu.roll(acc, step, 0)
                step *= 2
            wsum = acc[POOL_HALO:, :]
            pooled = wsum / _pool_counts(i * ts, ts, win) - cur
            pb = pooled.astype(BF16)
            p_ref[:, cols] = pb
            mixed = jnp.dot(pb, w_ref[gi], preferred_element_type=F32)
            xo_ref[:, cols] = x_ref[:, cols] + mixed * s_ref[:, cols]

    row = pl.BlockSpec((ts, D), lambda i: (i, 0))
    return pl.pallas_call(
        body, grid=(S // ts,), name="pool_fwd",
        in_specs=[row, pl.BlockSpec((POOL_HALO, D), lambda i: (jnp.maximum(i * hb - 1, 0), 0)), row,
                  pl.BlockSpec((G, P, P), lambda i: (0, 0, 0)), pl.BlockSpec((1, D), lambda i: (0, 0))],
        out_specs=[row, row],
        out_shape=[jax.ShapeDtypeStruct((S, D), F32), jax.ShapeDtypeStruct((S, D), BF16)],
        compiler_params=_params(("arbitrary",)),
    )(h, h, xres, w, scale)


def pool_backward_mix(dx, pooled, w, scale, ts=256):
    S, D = dx.shape
    G = len(POOL_WINDOWS)
    P = D // G

    def body(dx_ref, p_ref, w_ref, s_ref, dm_ref, dp_ref, ds_ref):
        i = pl.program_id(0)
        parts = []
        for gi in range(G):
            cols = slice(gi * P, (gi + 1) * P)
            dxv = dx_ref[:, cols]
            dmb = (dxv * s_ref[:, cols]).astype(BF16)
            dm_ref[:, cols] = dmb
            dp_ref[:, cols] = lax.dot_general(dmb, w_ref[gi], (((1,), (1,)), ((), ())),
                                              preferred_element_type=F32)
            mixed = jnp.dot(p_ref[:, cols], w_ref[gi], preferred_element_type=F32)
            parts.append(jnp.sum(dxv * mixed, axis=0, keepdims=True))
        part = jnp.concatenate(parts, axis=1)

        @pl.when(i == 0)
        def _():
            ds_ref[...] = part

        @pl.when(i > 0)
        def _():
            ds_ref[...] += part

    row = pl.BlockSpec((ts, D), lambda i: (i, 0))
    vec = pl.BlockSpec((1, D), lambda i: (0, 0))
    return pl.pallas_call(
        body, grid=(S // ts,), name="pool_bwd_mix",
        in_specs=[row, row, pl.BlockSpec((G, P, P), lambda i: (0, 0, 0)), vec],
        out_specs=[row, row, vec],
        out_shape=[jax.ShapeDtypeStruct((S, D), BF16), jax.ShapeDtypeStruct((S, D), F32),
                   jax.ShapeDtypeStruct((1, D), F32)],
        compiler_params=_params(("arbitrary",)),
    )(dx, pooled, w, scale)


def pool_backward_window(dp, ts=256):
    S, D = dp.shape
    G = len(POOL_WINDOWS)
    P = D // G
    hb = ts // POOL_HALO
    n_i = S // ts
    n_rows = ts + POOL_HALO

    def body(dp_ref, halo_ref, dh_ref):
        i = pl.program_id(0)
        for gi, win in enumerate(POOL_WINDOWS):
            cols = slice(gi * P, (gi + 1) * P)
            cur = dp_ref[:, cols]
            halo = jnp.where(i < n_i - 1, halo_ref[:, cols], 0.0)
            acc = jnp.concatenate([cur / _pool_counts(i * ts, ts, win),
                                   halo / _pool_counts((i + 1) * ts, POOL_HALO, win)], axis=0)
            step = 1
            while step < win:
                acc = acc + pltpu.roll(acc, n_rows - step, 0)
                step *= 2
            dh_ref[:, cols] = acc[:ts, :] - cur

    row = pl.BlockSpec((ts, D), lambda i: (i, 0))
    return pl.pallas_call(
        body, grid=(n_i,), name="pool_bwd_window",
        in_specs=[row, pl.BlockSpec((POOL_HALO, D), lambda i: (jnp.minimum((i + 1) * hb, S // POOL_HALO - 1), 0))],
        out_specs=row,
        out_shape=jax.ShapeDtypeStruct((S, D), F32),
        compiler_params=_params(("arbitrary",)),
    )(dp, dp)


_HG_LEVELS = (32, 16, 8, 4, 2, 1)
_N_LEV = len(_HG_LEVELS) + 1


def _hgrn_constants():
    C = HG_CHUNK
    t = np.arange(C)
    tri = (t[None, :] <= t[:, None]).astype(np.float32)
    blocks = [tri]
    masks, upq, upk = [], [], []
    for m in _HG_LEVELS:
        p = (t // (2 * m)) * 2 * m + m - 1
        blocks.append(tri[p])
        masks.append(((t[:, None] // (2 * m)) == (t[None, :] // (2 * m))).astype(np.float32))
        upper = (t % (2 * m)) >= m
        upq.append(np.repeat(upper[:, None], HEAD, 1).astype(np.float32))
        upk.append(np.repeat(~upper[:, None], HEAD, 1).astype(np.float32))
    blocks.append(tri)
    masks.append(np.eye(C, dtype=np.float32))
    upq.append(np.ones((C, HEAD), np.float32))
    upk.append(np.ones((C, HEAD), np.float32))
    mstack = np.concatenate(blocks, axis=0)
    mstack3 = np.concatenate([mstack] * 3, axis=1)
    trirev3 = np.concatenate([tri.T] * 3, axis=1)
    return (jnp.asarray(mstack3, BF16), jnp.asarray(np.stack(masks)), jnp.asarray(np.stack(upq)),
            jnp.asarray(np.stack(upk)), jnp.asarray(trirev3, BF16))


def _split3(x):
    hi = x.astype(BF16)
    r1 = x - hi.astype(F32)
    mid = r1.astype(BF16)
    lo = (r1 - mid.astype(F32)).astype(BF16)
    return jnp.concatenate([hi, mid, lo], axis=0)


def _hgrn_chunk_common(qa, fa, lbv, mstack3, upq, upk):
    sq = _sigmoid(qa)
    q = qa * sq
    sf = _sigmoid(fa)
    f = lbv + (1.0 - lbv) * sf
    g = jnp.log(f)
    k = 1.0 - f
    gall = jnp.dot(mstack3, _split3(g), preferred_element_type=F32).reshape(_N_LEV + 1, HG_CHUNK, HEAD)
    G = gall[0]
    eq_exp = G[None] - gall[1:]
    eq = jnp.exp(jnp.minimum(eq_exp, 0.0)) * upq
    ek = jnp.exp(jnp.minimum(-eq_exp, 0.0)) * upk
    Qs = (q[None] * eq).astype(BF16)
    Ks = (k[None] * ek).astype(BF16)
    return sq, q, sf, f, k, G, eq, ek, Qs, Ks


def hgrn_forward(proj, lb, hg_norm, ts=512):
    S = proj.shape[0]
    nh = lb.shape[1] // HEAD
    C = HG_CHUNK
    ncs = ts // C
    mstack3, masks, upq, upk, _ = _hgrn_constants()

    def body(qa_ref, fa_ref, ia_ref, ga_ref, lb_ref, gn_ref, ms_ref, mk_ref, uq_ref, uk_ref,
             oa_ref, oraw_ref, st_ref, state):
        tt = pl.program_id(1)

        @pl.when(tt == 0)
        def _():
            state[...] = jnp.zeros_like(state)

        gn = gn_ref[...]

        def chunk(c, carry):
            sl = pl.ds(pl.multiple_of(c * C, C), C)
            for hh in range(HG_HEADS_PER_BLOCK):
                cols = slice(hh * HEAD, (hh + 1) * HEAD)
                qa, fa, v, ga = qa_ref[sl, cols], fa_ref[sl, cols], ia_ref[sl, cols], ga_ref[sl, cols]
                _, q, _, _, k, G, _, _, Qs, Ks = _hgrn_chunk_common(qa, fa, lb_ref[:, cols], ms_ref[...],
                                                                    uq_ref[...], uk_ref[...])
                att7 = lax.dot_general(Qs, Ks, (((2,), (2,)), ((0,), (0,))), preferred_element_type=F32)
                att = jnp.sum(att7 * mk_ref[...], axis=0)
                st = state[hh]
                st_ref[hh, c] = st
                vb = v.astype(BF16)
                qg = (q * jnp.exp(G)).astype(BF16)
                o = jnp.dot(att.astype(BF16), vb, preferred_element_type=F32)
                o = o + lax.dot_general(qg, st.astype(BF16), (((1,), (1,)), ((), ())),
                                        preferred_element_type=F32)
                g_last = G[C - 1:C, :]
                kh = (k * jnp.exp(g_last - G)).astype(BF16)
                state[hh] = st * jnp.exp(g_last) + lax.dot_general(vb, kh, (((0,), (0,)), ((), ())),
                                                                   preferred_element_type=F32)
                oraw_ref[sl, cols] = o
                r = lax.rsqrt(jnp.mean(o * o, axis=-1, keepdims=True) + RMS_EPS)
                oa_ref[sl, cols] = (((o * r) * gn) * (ga * _sigmoid(ga))).astype(BF16)
            return carry

        lax.fori_loop(0, ncs, chunk, 0)

    hpb = HG_HEADS_PER_BLOCK
    wide = hpb * HEAD

    def col(m0):
        return pl.BlockSpec((ts, wide), lambda h, t: (t, m0 // hpb + h))

    const3 = lambda shape: pl.BlockSpec(shape, lambda h, t: (0, 0, 0))
    return pl.pallas_call(
        body, grid=(nh // hpb, S // ts), name="hgrn_fwd",
        in_specs=[col(0), col(nh), col(2 * nh), col(3 * nh),
                  pl.BlockSpec((1, wide), lambda h, t: (0, h)), pl.BlockSpec((1, HEAD), lambda h, t: (0, 0)),
                  pl.BlockSpec(mstack3.shape, lambda h, t: (0, 0)), const3(masks.shape), const3(upq.shape),
                  const3(upk.shape)],
        out_specs=[pl.BlockSpec((ts, wide), lambda h, t: (t, h)), pl.BlockSpec((ts, wide), lambda h, t: (t, h)),
                   pl.BlockSpec((hpb, ncs, HEAD, HEAD), lambda h, t: (h, t, 0, 0))],
        out_shape=[jax.ShapeDtypeStruct((S, nh * HEAD), BF16), jax.ShapeDtypeStruct((S, nh * HEAD), F32),
                   jax.ShapeDtypeStruct((nh, S // C, HEAD, HEAD), F32)],
        scratch_shapes=[pltpu.VMEM((hpb, HEAD, HEAD), F32)],
        compiler_params=_params(("arbitrary", "arbitrary")),
    )(proj, proj, proj, proj, lb, hg_norm, mstack3, masks, upq, upk)


def hgrn_backward(dcat, proj, oraw, states, lb, hg_norm, ts=512):
    S = proj.shape[0]
    nh = lb.shape[1] // HEAD
    C = HG_CHUNK
    ncs = ts // C
    nt = S // ts
    mstack3, masks, upq, upk, trirev3 = _hgrn_constants()

    def body(do_ref, qa_ref, fa_ref, ia_ref, ga_ref, or_ref, st_ref, lb_ref, gn_ref, ms_ref, mk_ref, uq_ref,
             uk_ref, tr_ref, dqa_ref, dfa_ref, dia_ref, dga_ref, dlb_ref, dgn_ref, dstate):
        tt = pl.program_id(1)

        @pl.when(tt == 0)
        def _():
            dstate[...] = jnp.zeros_like(dstate)
            dlb_ref[...] = jnp.zeros_like(dlb_ref)
            dgn_ref[...] = jnp.zeros_like(dgn_ref)

        gn = gn_ref[...]

        def chunk(cc, carry):
            c = ncs - 1 - cc
            sl = pl.ds(pl.multiple_of(c * C, C), C)
            for hh in range(HG_HEADS_PER_BLOCK):
                cols = slice(hh * HEAD, (hh + 1) * HEAD)
                lbv = lb_ref[:, cols]
                qa, fa, v, ga = qa_ref[sl, cols], fa_ref[sl, cols], ia_ref[sl, cols], ga_ref[sl, cols]
                sq, q, sf, f, k, G, eq, ek, Qs, Ks = _hgrn_chunk_common(qa, fa, lbv, ms_ref[...], uq_ref[...],
                                                                        uk_ref[...])
                mk = mk_ref[...]
                att7 = lax.dot_general(Qs, Ks, (((2,), (2,)), ((0,), (0,))), preferred_element_type=F32)
                att = jnp.sum(att7 * mk, axis=0)
                o = or_ref[sl, cols]
                dO = do_ref[sl, cols]
                sg = _sigmoid(ga)
                r = lax.rsqrt(jnp.mean(o * o, axis=-1, keepdims=True) + RMS_EPS)
                xh = o * r
                dga_ref[sl, cols] = (dO * (xh * gn) * (sg * (1.0 + ga * (1.0 - sg)))).astype(BF16)
                don = dO * (ga * sg)
                dgn_ref[hh] += jnp.sum(don * xh, axis=0, keepdims=True)
                dxh = don * gn
                do = r * (dxh - xh * jnp.mean(dxh * xh, axis=-1, keepdims=True))
                dob = do.astype(BF16)
                st = st_ref[hh, c]
                dst = dstate[hh]
                dstb = dst.astype(BF16)
                vb = v.astype(BF16)
                eG = jnp.exp(G)
                g_last = G[C - 1:C, :]
                e_last = jnp.exp(g_last)
                e_tail = jnp.exp(g_last - G)
                qg = (q * eG).astype(BF16)
                kh = (k * e_tail).astype(BF16)
                dq_inter = jnp.dot(dob, st.astype(BF16), preferred_element_type=F32) * eG
                dk_inter = jnp.dot(vb, dstb, preferred_element_type=F32) * e_tail
                dv = lax.dot_general(kh, dstb, (((1,), (1,)), ((), ())), preferred_element_type=F32)
                dv = dv + lax.dot_general(att.astype(BF16), dob, (((0,), (0,)), ((), ())),
                                          preferred_element_type=F32)
                dA = lax.dot_general(dob, vb, (((1,), (1,)), ((), ())), preferred_element_type=F32)
                dA7 = (dA[None] * mk).astype(BF16)
                dAT7 = (dA.T[None] * mk).astype(BF16)
                dQs = lax.dot_general(dA7, Ks, (((2,), (1,)), ((0,), (0,))), preferred_element_type=F32)
                dKs = lax.dot_general(dAT7, Qs, (((2,), (1,)), ((0,), (0,))), preferred_element_type=F32)
                dq = dq_inter + jnp.sum(dQs * eq, axis=0)
                dk = dk_inter + jnp.sum(dKs * ek, axis=0)
                dG = (jnp.sum(Qs.astype(F32) * dQs - Ks.astype(F32) * dKs, axis=0)
                      + q * dq_inter - k * dk_inter)
                last_extra = (jnp.sum(k * dk_inter, axis=0, keepdims=True)
                              + e_last * jnp.sum(dst * st, axis=0, keepdims=True))
                is_last = lax.broadcasted_iota(jnp.int32, (C, 1), 0) == C - 1
                dG = dG + jnp.where(is_last, last_extra, 0.0)
                dg = jnp.dot(tr_ref[...], _split3(dG), preferred_element_type=F32)
                df = dg / f - dk
                dfa_ref[sl, cols] = (df * (1.0 - lbv) * (sf * (1.0 - sf))).astype(BF16)
                dlb_ref[:, cols] += jnp.sum(df * (1.0 - sf), axis=0, keepdims=True)
                dqa_ref[sl, cols] = (dq * (sq * (1.0 + qa * (1.0 - sq)))).astype(BF16)
                dia_ref[sl, cols] = dv.astype(BF16)
                dstate[hh] = dst * e_last + lax.dot_general(dob, qg, (((0,), (0,)), ((), ())),
                                                            preferred_element_type=F32)
            return carry

        lax.fori_loop(0, ncs, chunk, 0)

    hpb = HG_HEADS_PER_BLOCK
    wide = hpb * HEAD

    def col(m0):
        return pl.BlockSpec((ts, wide), lambda h, t: (nt - 1 - t, m0 // hpb + h))

    const3 = lambda shape: pl.BlockSpec(shape, lambda h, t: (0, 0, 0))
    const2 = lambda shape: pl.BlockSpec(shape, lambda h, t: (0, 0))
    ocol = pl.BlockSpec((ts, wide), lambda h, t: (nt - 1 - t, h))
    half = nh * HEAD
    return pl.pallas_call(
        body, grid=(nh // hpb, nt), name="hgrn_bwd",
        in_specs=[col(0), col(0), col(nh), col(2 * nh), col(3 * nh), col(0),
                  pl.BlockSpec((hpb, ncs, HEAD, HEAD), lambda h, t: (h, nt - 1 - t, 0, 0)),
                  pl.BlockSpec((1, wide), lambda h, t: (0, h)), const2((1, HEAD)),
                  const2(mstack3.shape), const3(masks.shape), const3(upq.shape), const3(upk.shape),
                  const2(trirev3.shape)],
        out_specs=[ocol, ocol, ocol, ocol, pl.BlockSpec((1, wide), lambda h, t: (0, h)),
                   pl.BlockSpec((hpb, 1, HEAD), lambda h, t: (h, 0, 0))],
        out_shape=[jax.ShapeDtypeStruct((S, half), BF16)] * 4
                  + [jax.ShapeDtypeStruct((1, half), F32), jax.ShapeDtypeStruct((nh, 1, HEAD), F32)],
        scratch_shapes=[pltpu.VMEM((hpb, HEAD, HEAD), F32)],
        compiler_params=_params(("arbitrary", "arbitrary")),
    )(dcat, proj, proj, proj, proj, oraw, states, lb, hg_norm, mstack3, masks, upq, upk, trirev3)


SB_SUB = 128
LOG2_E = 1.4426950408889634
SB_SCALE = 1.0 / math.sqrt(HEAD)
SB_QUERY_SCALE = SB_SCALE * LOG2_E


def _split2(x):
    hi = x.astype(BF16)
    lo = (x - hi.astype(F32)).astype(BF16)
    return jnp.concatenate([hi, lo], axis=1)


def _sb_constants():
    j = np.arange(SB_SUB)
    after = (j[:, None] > j[None, :]).astype(np.float32)
    before = (j[:, None] < j[None, :]).astype(np.float32)
    return (jnp.asarray(np.concatenate([after, after], axis=0), BF16),
            jnp.asarray(np.concatenate([before, before], axis=0), BF16))


def _sb_tri(i):
    return (i * (i + 1)) // 2


def _sb_diag_mask(t):
    return lax.broadcasted_iota(jnp.int32, (t, t), 1) < lax.broadcasted_iota(jnp.int32, (t, t), 0)


def _sb_scores(q, k_ref, col0, t):
    ks = k_ref[pl.ds(pl.multiple_of(col0, t), t), :]
    return lax.dot_general(q, ks, (((1,), (1,)), ((), ())), preferred_element_type=F32)


def _sb_weights(z, mask, run, after2):
    nsub = z.shape[1] // SB_SUB
    nz = -z
    lk = jnp.minimum(nz, 0.0) - jnp.log(1.0 + jnp.exp2(jnp.minimum(z, nz))) * LOG2_E
    if mask is not None:
        lk = jnp.where(mask, lk, 0.0)
    locs, tots = [], []
    for b in range(nsub):
        lkb = lk[:, b * SB_SUB:(b + 1) * SB_SUB]
        loc = jnp.dot(_split2(lkb), after2, preferred_element_type=F32)
        locs.append(loc)
        tots.append(loc[:, 0:1] + lkb[:, 0:1])
    ws = [None] * nsub
    for b in reversed(range(nsub)):
        sl = slice(b * SB_SUB, (b + 1) * SB_SUB)
        ws[b] = jnp.exp2(z[:, sl] + lk[:, sl] + (locs[b] + run))
        run = run + tots[b]
    w = jnp.concatenate(ws, axis=1)
    if mask is not None:
        w = jnp.where(mask, w, 0.0)
    return w, run


def sb_forward(projb, nh, m0, t=512):
    S = projb.shape[0]
    after2, _ = _sb_constants()
    n_i = S // t

    def body(q_ref, k_ref, v_ref, af_ref, o_ref, w_hbm, wbuf, wsem):
        h = pl.program_id(0)
        i = pl.program_id(1)
        q = q_ref[...]
        after = af_ref[...]
        base = _sb_tri(i)

        def store(slot, jb):
            return pltpu.make_async_copy(wbuf.at[slot], w_hbm.at[h, base + jb], wsem.at[slot])

        def block(n, jb, run, mask):
            slot = n % 2

            @pl.when(n >= 2)
            def _():
                store(slot, jb).wait()

            z = _sb_scores(q, k_ref, jb * t, t)
            w, run = _sb_weights(z, mask, run, after)
            wb = w.astype(BF16)
            wbuf[slot] = wb
            store(slot, jb).start()
            vs = v_ref[pl.ds(pl.multiple_of(jb * t, t), t), :]
            return run, jnp.dot(wb, vs, preferred_element_type=F32)

        run, acc = block(0, i, jnp.zeros((t, 1), F32), _sb_diag_mask(t))

        def step(n, carry):
            run, acc = carry
            run, part = block(n + 1, i - 1 - n, run, None)
            return run, acc + part

        _, acc = lax.fori_loop(0, i, step, (run, acc))
        o_ref[...] = acc.astype(BF16)
        store(i % 2, 0).wait()

        @pl.when(i >= 1)
        def _():
            store((i + 1) % 2, 0).wait()

    return pl.pallas_call(
        body, grid=(nh, n_i), name="sb_fwd",
        in_specs=[pl.BlockSpec((t, HEAD), lambda h, i: (i, m0 + h)),
                  pl.BlockSpec((S, HEAD), lambda h, i: (0, m0 + nh + h)),
                  pl.BlockSpec((S, HEAD), lambda h, i: (0, m0 + 2 * nh + h)),
                  pl.BlockSpec(after2.shape, lambda h, i: (0, 0))],
        out_specs=[pl.BlockSpec((t, HEAD), lambda h, i: (i, h)), pl.BlockSpec(memory_space=pl.ANY)],
        out_shape=[jax.ShapeDtypeStruct((S, nh * HEAD), BF16),
                   jax.ShapeDtypeStruct((nh, _sb_tri(n_i), t, t), BF16)],
        scratch_shapes=[pltpu.VMEM((2, t, t), BF16), pltpu.SemaphoreType.DMA((2,))],
        compiler_params=_params(("arbitrary", "arbitrary")),
    )(projb, projb, projb, after2)


def sb_backward(dcat, projb, w_all, nh, m0, t=512):
    S = projb.shape[0]
    _, before2 = _sb_constants()
    n_i = S // t
    nsub = t // SB_SUB

    def body(do_ref, q_ref, k_ref, v_ref, bf_ref, w_hbm, dq_ref, dk_ref, dv_ref, dk_acc, dv_acc, wbuf, wsem):
        h = pl.program_id(0)
        i = pl.program_id(1)

        @pl.when(i == 0)
        def _():
            dk_acc[...] = jnp.zeros_like(dk_acc)
            dv_acc[...] = jnp.zeros_like(dv_acc)

        q = q_ref[...]
        dob = do_ref[...].astype(BF16)
        before = bf_ref[...]
        base = _sb_tri(i)

        def load(slot, jb):
            return pltpu.make_async_copy(w_hbm.at[h, base + jb], wbuf.at[slot], wsem.at[slot])

        load(0, 0).start()

        def left_to_right(jb, run, dq, mask):
            slot = jb % 2
            load(slot, jb).wait()

            @pl.when(jb < i)
            def _():
                load(1 - slot, jb + 1).start()

            ksl = pl.ds(pl.multiple_of(jb * t, t), t)
            wb = wbuf[slot]
            z = _sb_scores(q, k_ref, jb * t, t)
            dw = lax.dot_general(dob, v_ref[ksl, :], (((1,), (1,)), ((), ())), preferred_element_type=F32)
            d = dw * wb.astype(F32)
            dv_acc[ksl, :] += lax.dot_general(wb, dob, (((0,), (0,)), ((), ())), preferred_element_type=F32)
            sig = 1.0 / (1.0 + jnp.exp2(-z))
            das = []
            for b in range(nsub):
                db = d[:, b * SB_SUB:(b + 1) * SB_SUB]
                prefix = run + jnp.dot(_split2(db), before, preferred_element_type=F32)
                das.append(db - sig[:, b * SB_SUB:(b + 1) * SB_SUB] * (db + prefix))
                run = prefix[:, SB_SUB - 1:SB_SUB] + db[:, SB_SUB - 1:SB_SUB]
            da = jnp.concatenate(das, axis=1)
            if mask is not None:
                da = jnp.where(mask, da, 0.0)
            dab = (da * SB_SCALE).astype(BF16)
            dq = dq + jnp.dot(dab, k_ref[ksl, :], preferred_element_type=F32)
            dk_acc[ksl, :] += lax.dot_general(dab, q, (((0,), (0,)), ((), ())), preferred_element_type=F32)
            return run, dq

        run, dq = lax.fori_loop(0, i, lambda jb, c: left_to_right(jb, c[0], c[1], None),
                                (jnp.zeros((t, 1), F32), jnp.zeros((t, HEAD), F32)))
        _, dq = left_to_right(i, run, dq, _sb_diag_mask(t))
        dq_ref[...] = dq.astype(BF16)

        @pl.when(i == n_i - 1)
        def _():
            dk_ref[...] = (dk_acc[...] * (1.0 / SB_QUERY_SCALE)).astype(BF16)
            dv_ref[...] = dv_acc[...].astype(BF16)

    half = nh * HEAD
    full = pl.BlockSpec((S, HEAD), lambda h, i: (0, h))
    return pl.pallas_call(
        body, grid=(nh, n_i), name="sb_bwd",
        in_specs=[pl.BlockSpec((t, HEAD), lambda h, i: (i, nh + h)),
                  pl.BlockSpec((t, HEAD), lambda h, i: (i, m0 + h)),
                  pl.BlockSpec((S, HEAD), lambda h, i: (0, m0 + nh + h)),
                  pl.BlockSpec((S, HEAD), lambda h, i: (0, m0 + 2 * nh + h)),
                  pl.BlockSpec(before2.shape, lambda h, i: (0, 0)), pl.BlockSpec(memory_space=pl.ANY)],
        out_specs=[pl.BlockSpec((t, HEAD), lambda h, i: (i, h)), full, full],
        out_shape=[jax.ShapeDtypeStruct((S, half), BF16)] * 3,
        scratch_shapes=[pltpu.VMEM((S, HEAD), F32), pltpu.VMEM((S, HEAD), F32),
                        pltpu.VMEM((2, t, t), BF16), pltpu.SemaphoreType.DMA((2,))],
        compiler_params=_params(("arbitrary", "arbitrary")),
    )(dcat, projb, projb, projb, before2, w_all)


def local_step(x, target, mix_norm, ffn_norm, final_norm, lb_logits, hg_norm, get_w_in, get_w_rest, send):
    S, D = x.shape
    half = D // 2
    nh = half // HEAD
    tm = 512
    tk = 2048
    row = lambda i, j: (i, 0)

    lb = jax.nn.softmax(lb_logits, axis=0)[0:1]

    h0, r0 = rms_fwd(x, mix_norm[0:1], BF16)
    w_in = get_w_in(h0)
    nbi = w_in.shape[2]
    col = jnp.arange(N_DEV * nbi) // half
    col_scale = jnp.where(col == 4, SB_QUERY_SCALE, 1.0).astype(F32)[None]
    proj, projb = matmul(
        "proj_in", [h0], [w_in], grid=(N_DEV, S // tm, 1),
        a_spec=pl.BlockSpec((tm, D), lambda j, i, k: (i, 0)),
        b_spec=pl.BlockSpec((None, D, nbi), lambda j, i, k: (j, 0, 0)),
        out_spec=pl.BlockSpec((tm, nbi), lambda j, i, k: (i, j)), out_shape=(S, N_DEV * nbi),
        out_dtypes=[F32, BF16], acc_shape=(8, 128),
        bf16_scale=col_scale, bf16_scale_spec=pl.BlockSpec((1, nbi), lambda j, i, k: (0, j)))
    oa, oraw, states = hgrn_forward(proj, lb, hg_norm)
    ob, sb_weights = sb_forward(projb, nh, 4 * nh)
    cat = jnp.concatenate([oa, ob], axis=1)
    w_out, pool_w, pool_scale, wg, wu, wd = get_w_rest(cat)
    (x1,) = matmul(
        "mix_out", [cat], [w_out], grid=(S // tm, 1),
        a_spec=pl.BlockSpec((tm, D), row), b_spec=pl.BlockSpec((D, D), lambda i, k: (0, 0)),
        out_spec=pl.BlockSpec((tm, D), row), out_shape=(S, D), out_dtypes=[F32], acc_shape=(8, 128),
        res=x, res_spec=pl.BlockSpec((tm, D), row))
    h1, r1 = rms_fwd(x1, ffn_norm[0:1], BF16)
    x2, ffn0 = ffn_forward(h1, x1, wg[0], wu[0], wd[0])

    h2, r2 = rms_fwd(x2, mix_norm[1:2], F32)
    x3, pooled = pool_forward(h2, x2, pool_w, pool_scale)
    h3, r3 = rms_fwd(x3, ffn_norm[1:2], BF16)
    x4, ffn1 = ffn_forward(h3, x3, wg[1], wu[1], wd[1])

    loss_blk, dx4, dx4b, d_final = loss_and_final_bwd(x4, final_norm, target)

    dh3, dwg1, dwu1, dwd1 = ffn_backward(dx4b, h3, ffn1, wg[1], wu[1], wd[1])
    dh3 = send("ffn1", dict(ffn_w_gate_1=dwg1, ffn_w_up_1=dwu1, ffn_w_down_1=dwd1), dh3)
    dx3, _, d_ffn1 = rms_bwd(dh3, x3, r3, ffn_norm[1:2], dx4)
    dmixed, dpooled, d_pscale = pool_backward_mix(dx3, pooled, pool_w, pool_scale)
    G = len(POOL_WINDOWS)
    P = D // G
    (d_pool_w,) = matmul(
        "pool_dw", [pooled], [dmixed], grid=(G, S // tk),
        a_spec=pl.BlockSpec((tk, P), lambda g, k: (k, g)), b_spec=pl.BlockSpec((tk, P), lambda g, k: (k, g)),
        out_spec=pl.BlockSpec((None, P, P), lambda g, k: (g, 0, 0)), out_shape=(G, P, P), out_dtypes=[BF16],
        acc_shape=(P, P), trans_a=True)
    dh2 = pool_backward_window(dpooled)
    dx2, dx2b, d_mix1 = rms_bwd(dh2, x2, r2, mix_norm[1:2], dx3)

    dh1, dwg0, dwu0, dwd0 = ffn_backward(dx2b, h1, ffn0, wg[0], wu[0], wd[0])
    dx1, dx1b, d_ffn0 = rms_bwd(dh1, x1, r1, ffn_norm[0:1], dx2)
    (dcat,) = matmul(
        "mix_out_dx", [dx1b], [w_out], grid=(S // tm, 1),
        a_spec=pl.BlockSpec((tm, D), row), b_spec=pl.BlockSpec((D, D), lambda i, k: (0, 0)),
        out_spec=pl.BlockSpec((tm, D), row), out_shape=(S, D), out_dtypes=[F32], acc_shape=(8, 128),
        trans_b=True)
    (d_w_out,) = matmul(
        "mix_out_dw", [cat], [dx1b], grid=(2, S // tk),
        a_spec=pl.BlockSpec((tk, half), lambda m, k: (k, m)), b_spec=pl.BlockSpec((tk, D), lambda m, k: (k, 0)),
        out_spec=pl.BlockSpec((half, D), lambda m, k: (m, 0)), out_shape=(D, D), out_dtypes=[BF16],
        acc_shape=(half, D), trans_a=True)
    dcat = send("layer0", dict(ffn_w_gate_0=dwg0, ffn_w_up_0=dwu0, ffn_w_down_0=dwd0, pool_w=d_pool_w,
                               ab_w_out=d_w_out), dcat)
    dqa, dfa, dia, dga, d_lb, d_hg = hgrn_backward(dcat, proj, oraw, states, lb, hg_norm)
    dqb, dkb, dvb = sb_backward(dcat, projb, sb_weights, nh, 4 * nh)
    dproj = jnp.concatenate([dqa, dfa, dia, dga, dqb, dkb, dvb], axis=1)
    (d_w_in,) = matmul(
        "proj_in_dw", [h0], [dproj], grid=(N_DEV, S // tk),
        a_spec=pl.BlockSpec((tk, D), lambda j, k: (k, 0)), b_spec=pl.BlockSpec((tk, nbi), lambda j, k: (k, j)),
        out_spec=pl.BlockSpec((None, D, nbi), lambda j, k: (j, 0, 0)), out_shape=(N_DEV, D, nbi),
        out_dtypes=[BF16], acc_shape=(D, nbi), trans_a=True)
    dproj = send("w_in", dict(ab_w_in=d_w_in), dproj)
    (dh0,) = matmul(
        "proj_in_dx", [dproj], [w_in], grid=(S // tm, N_DEV),
        a_spec=pl.BlockSpec((tm, nbi), lambda i, j: (i, j)),
        b_spec=pl.BlockSpec((None, D, nbi), lambda i, j: (j, 0, 0)),
        out_spec=pl.BlockSpec((tm, D), row), out_shape=(S, D), out_dtypes=[F32], acc_shape=(tm, D),
        trans_b=True)
    dx0, _, d_mix0 = rms_bwd(dh0, x, r0, mix_norm[0:1], dx1)

    d_l0 = d_lb * lb * (1.0 - lb)
    small = dict(
        loss=loss_blk[0:1, 0:1],
        mix_norm=jnp.concatenate([d_mix0, d_mix1], axis=0),
        ffn_norm=jnp.concatenate([d_ffn0, d_ffn1], axis=0),
        final_norm=d_final,
        lb_logits=jnp.concatenate([d_l0, -d_l0], axis=0),
        hg_out_norm=jnp.sum(d_hg, axis=0),
        pool_scale=d_pscale,
    )
    return dx0, small


def _my_index():
    return 4 * lax.axis_index("x") + 2 * lax.axis_index("y") + lax.axis_index("c")


def _peer(r):
    x, y, c = lax.axis_index("x"), lax.axis_index("y"), lax.axis_index("c")
    px = 1 - x if (r >> 2) & 1 else x
    py = 1 - y if (r >> 1) & 1 else y
    pc = 1 - c if r & 1 else c
    return (px, py, pc), 4 * px + 2 * py + pc


def exchange(name, arrays, gather):
    n = len(arrays)
    n_peers = N_DEV - 1

    def body(*refs):
        ins, outs = refs[:n], refs[n:2 * n]
        send_sems, recv_sems, local_sems = refs[2 * n:]
        me = _my_index()
        local = []
        for a in range(n):
            src = ins[a] if gather else ins[a].at[me]
            cp = pltpu.make_async_copy(src, outs[a].at[me], local_sems.at[a])
            cp.start()
            local.append(cp)
        remote = []
        for a in range(n):
            for r in range(1, N_DEV):
                peer, pidx = _peer(r)
                src = ins[a] if gather else ins[a].at[pidx]
                cp = pltpu.make_async_remote_copy(
                    src_ref=src, dst_ref=outs[a].at[me], send_sem=send_sems.at[a * n_peers + r - 1],
                    recv_sem=recv_sems.at[a * n_peers + r - 1], device_id=peer, device_id_type=MESH)
                cp.start()
                remote.append((cp, a, r))
        for cp, a, r in remote:
            _, pidx = _peer(r)
            src = ins[a] if gather else ins[a].at[pidx]
            pltpu.make_async_remote_copy(
                src_ref=src, dst_ref=outs[a].at[pidx], send_sem=send_sems.at[a * n_peers + r - 1],
                recv_sem=recv_sems.at[a * n_peers + r - 1], device_id=_peer(r)[0], device_id_type=MESH).wait_recv()
        for cp, a, r in remote:
            cp.wait_send()
        for cp in local:
            cp.wait()

    out_shape = [jax.ShapeDtypeStruct(((N_DEV,) + a.shape) if gather else a.shape, a.dtype) for a in arrays]
    any_spec = pl.BlockSpec(memory_space=pl.ANY)
    return pl.pallas_call(
        body, name=name, in_specs=[any_spec] * n, out_specs=[any_spec] * n, out_shape=out_shape,
        scratch_shapes=[pltpu.SemaphoreType.DMA((n * n_peers,)), pltpu.SemaphoreType.DMA((n * n_peers,)),
                        pltpu.SemaphoreType.DMA((n,))],
    )(*arrays)


_HBM = pl.BlockSpec(memory_space=pltpu.HBM)
_SEM = pl.BlockSpec(memory_space=pltpu.SEMAPHORE)
_EFFECT = pltpu.SideEffectType.DATAFLOW_SIDE_EFFECTING


def _landing(arrays, gather):
    me = _my_index()
    lands = []
    for a in arrays:
        own = a[None] if gather else lax.dynamic_slice_in_dim(a, me, 1, axis=0)
        shape = ((N_DEV,) + a.shape) if gather else a.shape
        lands.append(lax.dynamic_update_slice_in_dim(lax.empty(shape, a.dtype), own, me, axis=0))
    return lands


def exchange_start(name, arrays, gather, carry):
    n = len(arrays)
    n_peers = N_DEV - 1
    lands = _landing(arrays, gather)
    n_thru = 2 * n + 1

    def body(*refs):
        src, land = refs[:n], refs[n:2 * n]
        send_sems, recv_sems = refs[n_thru], refs[n_thru + 1]
        token = refs[-1]
        me = _my_index()
        for a in range(n):
            for r in range(1, N_DEV):
                peer, pidx = _peer(r)
                pltpu.make_async_remote_copy(
                    src_ref=src[a] if gather else src[a].at[pidx], dst_ref=land[a].at[me],
                    send_sem=send_sems.at[a * n_peers + r - 1], recv_sem=recv_sems.at[a * n_peers + r - 1],
                    device_id=peer, device_id_type=MESH).start()
        token[...] = jnp.zeros_like(token)

    operands = list(arrays) + lands + [carry]
    outs = pl.pallas_call(
        body, name=name,
        out_shape=(pltpu.SemaphoreType.DMA((n * n_peers,)), pltpu.SemaphoreType.DMA((n * n_peers,)),
                   *[pltpu.HBM(a.shape, a.dtype) for a in operands], jax.ShapeDtypeStruct((8, 128), F32)),
        in_specs=[_HBM] * n_thru,
        out_specs=(_SEM, _SEM, *([_HBM] * n_thru), pl.BlockSpec(memory_space=pltpu.VMEM)),
        input_output_aliases={i: 2 + i for i in range(n_thru)},
        compiler_params=pltpu.CompilerParams(has_side_effects=_EFFECT),
    )(*[pltpu.with_memory_space_constraint(a, pltpu.HBM) for a in operands])
    handle = (outs[0], outs[1], list(outs[2:2 + n]), list(outs[2 + n:2 + 2 * n]), gather)
    return handle, outs[2 + 2 * n]


def exchange_wait(name, handle, after):
    send_sems, recv_sems, srcs, lands, gather = handle
    n = len(srcs)
    n_peers = N_DEV - 1

    def body(*refs):
        src, land = refs[:n], refs[n:2 * n]
        send_s, recv_s = refs[2 * n], refs[2 * n + 1]
        for a in range(n):
            for r in range(1, N_DEV):
                peer, pidx = _peer(r)
                cp = pltpu.make_async_remote_copy(
                    src_ref=src[a] if gather else src[a].at[pidx], dst_ref=land[a].at[pidx],
                    send_sem=send_s.at[a * n_peers + r - 1], recv_sem=recv_s.at[a * n_peers + r - 1],
                    device_id=peer, device_id_type=MESH)
                cp.wait_send()
                cp.wait_recv()

    shapes = [pltpu.HBM(a.shape, a.dtype) for a in srcs] + [pltpu.HBM(l.shape, l.dtype) for l in lands]
    outs = pl.pallas_call(
        body, name=name, out_shape=tuple(shapes),
        in_specs=[_HBM] * (2 * n) + [_SEM, _SEM, pl.BlockSpec(memory_space=pl.ANY)],
        out_specs=tuple([_HBM] * (2 * n)),
        input_output_aliases={i: i for i in range(2 * n)},
        compiler_params=pltpu.CompilerParams(has_side_effects=_EFFECT),
    )(*srcs, *lands, send_sems, recv_sems, after)
    return list(outs[n:])


def _row_tile(rows, cap=256):
    best = None
    for t in range(16, min(rows, cap) + 1, 16):
        if rows % t == 0:
            best = t
    return best if best is not None else rows


def sum_slots(name, recv):
    n, R, C = recv.shape
    tr = _row_tile(R)

    def body(r_ref, o_ref):
        g = r_ref[0].astype(F32)
        for d in range(1, n):
            g = g + r_ref[d].astype(F32)
        o_ref[...] = g

    return pl.pallas_call(
        body, grid=(R // tr,), name=name,
        in_specs=[pl.BlockSpec((n, tr, C), lambda i: (0, i, 0))],
        out_specs=pl.BlockSpec((tr, C), lambda i: (i, 0)),
        out_shape=jax.ShapeDtypeStruct((R, C), F32),
        compiler_params=_params(("arbitrary",)),
    )(recv)


def adamw(name, recv, w, m, v, layer=None, prev=None):
    n, R, C = recv.shape
    tr = _row_tile(R)

    def body(r_ref, w_ref, m_ref, v_ref, *rest):
        g_ref, d_ref, nm_ref, nv_ref = rest[-4:]
        g = r_ref[0].astype(F32)
        for d in range(1, n):
            g = g + r_ref[d].astype(F32)
        mm = ADAM_B1 * m_ref[...] + (1.0 - ADAM_B1) * g
        vv = ADAM_B2 * v_ref[...] + (1.0 - ADAM_B2) * (g * g)
        m_hat = mm / (1.0 - ADAM_B1 ** ADAM_STEP)
        v_hat = vv / (1.0 - ADAM_B2 ** ADAM_STEP)
        g_ref[...] = g
        d_ref[...] = -ADAM_LR * (m_hat / (jnp.sqrt(v_hat) + ADAM_EPS) + ADAM_WD * w_ref[...])
        nm_ref[...] = mm
        nv_ref[...] = vv

    if layer is None:
        row = pl.BlockSpec((tr, C), lambda i: (i, 0))
        shape = (R, C)
    else:
        row = pl.BlockSpec((None, tr, C), lambda i: (layer, i, 0))
        shape = w.shape
    prev = [] if prev is None else list(prev)
    return pl.pallas_call(
        body, grid=(R // tr,), name=name,
        in_specs=[pl.BlockSpec((n, tr, C), lambda i: (0, i, 0)), row, row, row]
                 + [pl.BlockSpec(memory_space=pl.ANY)] * len(prev),
        out_specs=[row] * 4,
        out_shape=[jax.ShapeDtypeStruct(shape, F32)] * 4,
        input_output_aliases={4 + o: o for o in range(len(prev))},
        compiler_params=_params(("arbitrary",)),
    )(recv, w, m, v, *prev)


def _adamw_nd(name, recv, w, m, v):
    shp = w.shape
    C = shp[-1]
    flat = lambda a: a.reshape(-1, C)
    outs = adamw(name, recv.reshape(recv.shape[0], -1, C), flat(w), flat(m), flat(v))
    return [o.reshape(shp) for o in outs]


_SMALL_NAMES = ("loss", "mix_norm", "ffn_norm", "final_norm", "lb_logits", "hg_out_norm", "pool_scale")
_LANES = 128


def _pack_small(parts):
    rows, layout = [], {}
    at = 0
    for name in parts:
        flat = parts[name].reshape(-1).astype(F32)
        n_rows = -(-flat.shape[0] // (8 * _LANES)) * 8
        flat = jnp.pad(flat, (0, n_rows * _LANES - flat.shape[0]))
        rows.append(flat.reshape(n_rows, _LANES))
        layout[name] = (at, parts[name].shape)
        at += n_rows
    return jnp.concatenate(rows, axis=0), layout


def _unpack_small(pack, layout):
    out = {}
    for name, (at, shape) in layout.items():
        size = int(np.prod(shape))
        n_rows = -(-size // _LANES)
        out[name] = pack[at:at + n_rows].reshape(-1)[:size].reshape(shape)
    return out


def kernel(x, mix_norm, ffn_norm, final_norm, ab_w_in, lb_logits, hg_out_norm, ab_w_out, pool_w, pool_scale, ffn_w_gate, ffn_w_up, ffn_w_down, loss_target, m_mix_norm, m_ffn_norm, m_final_norm, m_ab_w_in, m_lb_logits, m_hg_out_norm, m_ab_w_out, m_pool_w, m_pool_scale, m_ffn_w_gate, m_ffn_w_up, m_ffn_w_down, v_mix_norm, v_ffn_norm, v_final_norm, v_ab_w_in, v_lb_logits, v_hg_out_norm, v_ab_w_out, v_pool_w, v_pool_scale, v_ffn_w_gate, v_ffn_w_up, v_ffn_w_down):
    D = x.shape[-1]
    n_layers = ffn_w_gate.shape[0]
    G = pool_w.shape[1]
    P = pool_w.shape[3]
    me = _my_index()

    in_handle, mix_norm_after = exchange_start("gather_w_in_start", [ab_w_in[0].astype(BF16)], True, mix_norm)
    rest = [ab_w_out[0], pool_w[0]]
    for l in range(n_layers):
        rest += [ffn_w_gate[l], ffn_w_up[l], ffn_w_down[l]]
    rest = [s.astype(BF16) for s in rest] + [pool_scale]
    rest_handle = []

    def get_w_in(after):
        w_in = exchange_wait("gather_w_in_wait", in_handle, after)[0]
        handle, w_in = exchange_start("gather_rest_start", rest, True, w_in)
        rest_handle.append(handle)
        return w_in

    def get_w_rest(after):
        got = exchange_wait("gather_rest_wait", rest_handle[0], after)
        w_out_g = got[0].reshape(D, D)
        pool_g = got[1].transpose(1, 0, 2, 3).reshape(G, P, P)
        wg = [got[2 + 3 * l] for l in range(n_layers)]
        wu = [got[3 + 3 * l] for l in range(n_layers)]
        wd = [got[4 + 3 * l] for l in range(n_layers)]
        return w_out_g, pool_g, got[-1].reshape(1, D), wg, wu, wd

    in_flight = []

    def send(tag, grads, carry):
        if "pool_w" in grads:
            grads = dict(grads, pool_w=grads["pool_w"].reshape(G, N_DEV, P // N_DEV, P).transpose(1, 0, 2, 3))
        if "ab_w_out" in grads:
            grads = dict(grads, ab_w_out=grads["ab_w_out"].reshape(N_DEV, D // N_DEV, D))
        handle, carry = exchange_start("grads_" + tag + "_start", list(grads.values()), False, carry)
        in_flight.append((tag, list(grads.keys()), handle))
        return carry

    dx0, small = local_step(x[0], loss_target[0], mix_norm_after, ffn_norm, final_norm[None],
                            lb_logits, hg_out_norm, get_w_in, get_w_rest, send)

    recv = {}
    for tag, names, handle in in_flight:
        recv.update(zip(names, exchange_wait("grads_" + tag + "_wait", handle, dx0)))
    small_pack, layout = _pack_small({k: small[k] for k in _SMALL_NAMES})
    (small_all,) = exchange("gather_small", [small_pack], gather=True)
    tot = _unpack_small(sum_slots("sum_small", small_all), layout)

    res = {}
    res["ab_w_in"] = _adamw_nd("adamw_w_in", recv["ab_w_in"], ab_w_in, m_ab_w_in, v_ab_w_in)
    res["ab_w_out"] = _adamw_nd("adamw_w_out", recv["ab_w_out"], ab_w_out, m_ab_w_out, v_ab_w_out)
    res["pool_w"] = _adamw_nd("adamw_pool_w", recv["pool_w"], pool_w, m_pool_w, v_pool_w)
    ffn_in = {"ffn_w_gate": (ffn_w_gate, m_ffn_w_gate, v_ffn_w_gate),
              "ffn_w_up": (ffn_w_up, m_ffn_w_up, v_ffn_w_up),
              "ffn_w_down": (ffn_w_down, m_ffn_w_down, v_ffn_w_down)}
    for name, (w, m, v) in ffn_in.items():
        outs = None
        for l in range(n_layers):
            outs = adamw("adamw_" + name, recv[name + "_" + str(l)], w, m, v, layer=l, prev=outs)
        res[name] = outs

    n_ps = pool_scale.shape[1]
    small_g = dict(tot)
    small_g["pool_scale"] = lax.dynamic_slice(tot["pool_scale"], (0, me * n_ps), (1, n_ps))
    small_w = dict(mix_norm=(mix_norm, m_mix_norm, v_mix_norm), ffn_norm=(ffn_norm, m_ffn_norm, v_ffn_norm),
                   final_norm=(final_norm, m_final_norm, v_final_norm),
                   lb_logits=(lb_logits, m_lb_logits, v_lb_logits),
                   hg_out_norm=(hg_out_norm, m_hg_out_norm, v_hg_out_norm),
                   pool_scale=(pool_scale, m_pool_scale, v_pool_scale))
    g_pack, lay2 = _pack_small({k: small_g[k].reshape(small_w[k][0].shape) for k in small_w})
    w_pack, _ = _pack_small({k: small_w[k][0] for k in small_w})
    m_pack, _ = _pack_small({k: small_w[k][1] for k in small_w})
    v_pack, _ = _pack_small({k: small_w[k][2] for k in small_w})
    small_out = [_unpack_small(o, lay2) for o in adamw("adamw_small", g_pack[None], w_pack, m_pack, v_pack)]
    for k in small_w:
        res[k] = [small_out[o][k] for o in range(4)]

    order = ("mix_norm", "ffn_norm", "final_norm", "ab_w_in", "lb_logits", "hg_out_norm", "ab_w_out", "pool_w",
             "pool_scale", "ffn_w_gate", "ffn_w_up", "ffn_w_down")
    outs = [tot["loss"].reshape(()), dx0[None]]
    for o in range(4):
        outs += [res[k][o] for k in order]
    return tuple(outs)
```

```python
import functools
import math

import numpy as np
import jax
import jax.numpy as jnp
from jax import lax
from jax.experimental import pallas as pl
from jax.experimental.pallas import tpu as pltpu

F32 = jnp.float32
BF16 = jnp.bfloat16

N_DEV = 8
RMS_EPS = 1e-6
HEAD = 128
HG_CHUNK = 64
HG_HEADS_PER_BLOCK = 4
POOL_WINDOWS = (2, 4, 8, 16)
POOL_HALO = 16
ADAM_LR, ADAM_B1, ADAM_B2, ADAM_EPS, ADAM_WD, ADAM_STEP = 0.001, 0.9, 0.999, 1e-08, 0.01, 10
VMEM_LIMIT_BYTES = 60 * 1024 * 1024
MESH = pl.DeviceIdType.MESH


def _params(sem):
    return pltpu.CompilerParams(dimension_semantics=sem, vmem_limit_bytes=VMEM_LIMIT_BYTES)


def _sigmoid(x):
    return 1.0 / (1.0 + jnp.exp(-x))


def rms_fwd(x, gain, out_dtype, ts=512):
    S, D = x.shape

    def body(x_ref, g_ref, h_ref, r_ref):
        xv = x_ref[...]
        r = lax.rsqrt(jnp.mean(xv * xv, axis=-1, keepdims=True) + RMS_EPS)
        h_ref[...] = ((xv * r) * g_ref[...]).astype(h_ref.dtype)
        r_ref[...] = r

    return pl.pallas_call(
        body, grid=(S // ts,), name="rms_fwd",
        in_specs=[pl.BlockSpec((ts, D), lambda i: (i, 0)), pl.BlockSpec((1, D), lambda i: (0, 0))],
        out_specs=[pl.BlockSpec((ts, D), lambda i: (i, 0)), pl.BlockSpec((ts, 1), lambda i: (i, 0))],
        out_shape=[jax.ShapeDtypeStruct((S, D), out_dtype), jax.ShapeDtypeStruct((S, 1), F32)],
        compiler_params=_params(("arbitrary",)),
    )(x, gain)


def rms_bwd(dh, x, r, gain, dres, ts=512):
    S, D = x.shape

    def body(dh_ref, x_ref, r_ref, g_ref, dres_ref, dx_ref, dxb_ref, dg_ref):
        i = pl.program_id(0)
        rr = r_ref[...]
        xh = x_ref[...] * rr
        dhv = dh_ref[...]
        dxh = dhv * g_ref[...]
        dx = dres_ref[...] + rr * (dxh - xh * jnp.mean(dxh * xh, axis=-1, keepdims=True))
        dx_ref[...] = dx
        dxb_ref[...] = dx.astype(BF16)
        part = jnp.sum(dhv * xh, axis=0, keepdims=True)

        @pl.when(i == 0)
        def _():
            dg_ref[...] = part

        @pl.when(i > 0)
        def _():
            dg_ref[...] += part

    row = pl.BlockSpec((ts, D), lambda i: (i, 0))
    vec = pl.BlockSpec((1, D), lambda i: (0, 0))
    return pl.pallas_call(
        body, grid=(S // ts,), name="rms_bwd",
        in_specs=[row, row, pl.BlockSpec((ts, 1), lambda i: (i, 0)), vec, row],
        out_specs=[row, row, vec],
        out_shape=[jax.ShapeDtypeStruct((S, D), F32), jax.ShapeDtypeStruct((S, D), BF16),
                   jax.ShapeDtypeStruct((1, D), F32)],
        compiler_params=_params(("arbitrary",)),
    )(dh, x, r, gain, dres)


def loss_and_final_bwd(x, gain, target, ts=512):
    S, D = x.shape

    def body(x_ref, g_ref, t_ref, loss_ref, dx_ref, dxb_ref, dg_ref):
        i = pl.program_id(0)
        xv = x_ref[...]
        rr = lax.rsqrt(jnp.mean(xv * xv, axis=-1, keepdims=True) + RMS_EPS)
        xh = xv * rr
        err = xh * g_ref[...] - t_ref[...]
        part_loss = 0.5 * jnp.sum(jnp.mean(err * err, axis=-1, keepdims=True))
        dy = err / D
        dxh = dy * g_ref[...]
        dx = rr * (dxh - xh * jnp.mean(dxh * xh, axis=-1, keepdims=True))
        dx_ref[...] = dx
        dxb_ref[...] = dx.astype(BF16)
        part = jnp.sum(dy * xh, axis=0, keepdims=True)

        @pl.when(i == 0)
        def _():
            dg_ref[...] = part
            loss_ref[...] = jnp.zeros_like(loss_ref) + part_loss

        @pl.when(i > 0)
        def _():
            dg_ref[...] += part
            loss_ref[...] += part_loss

    row = pl.BlockSpec((ts, D), lambda i: (i, 0))
    vec = pl.BlockSpec((1, D), lambda i: (0, 0))
    return pl.pallas_call(
        body, grid=(S // ts,), name="loss_final",
        in_specs=[row, vec, row],
        out_specs=[pl.BlockSpec((8, 128), lambda i: (0, 0)), row, row, vec],
        out_shape=[jax.ShapeDtypeStruct((8, 128), F32), jax.ShapeDtypeStruct((S, D), F32),
                   jax.ShapeDtypeStruct((S, D), BF16), jax.ShapeDtypeStruct((1, D), F32)],
        compiler_params=_params(("arbitrary",)),
    )(x, gain, target)


def matmul(name, a_ops, b_ops, *, grid, a_spec, b_spec, out_spec, out_shape, out_dtypes, acc_shape,
           trans_a=False, trans_b=False, res=None, res_spec=None, bf16_scale=None, bf16_scale_spec=None):
    n_pairs = len(a_ops)
    n_out = len(out_dtypes)
    nk = grid[-1]
    kaxis = len(grid) - 1
    dn = (((0,) if trans_a else (1,), (1,) if trans_b else (0,)), ((), ()))

    def body(*refs):
        a_refs = refs[:n_pairs]
        b_refs = refs[n_pairs:2 * n_pairs]
        pos = 2 * n_pairs
        res_ref = None
        if res is not None:
            res_ref = refs[pos]
            pos += 1
        scale_ref = None
        if bf16_scale is not None:
            scale_ref = refs[pos]
            pos += 1
        out_refs = refs[pos:pos + n_out]
        acc_ref = refs[pos + n_out]
        k = pl.program_id(kaxis)
        in_place = n_out == 1 and out_dtypes[0] == F32
        target = out_refs[0] if in_place else acc_ref

        def finish(val):
            if res_ref is not None:
                val = val + res_ref[...]
            for o in out_refs:
                if scale_ref is not None and o.dtype == BF16:
                    o[...] = (val * scale_ref[...]).astype(BF16)
                else:
                    o[...] = val.astype(o.dtype)

        if nk > 1:
            @pl.when(k == 0)
            def _():
                if in_place and res_ref is not None:
                    target[...] = res_ref[...]
                else:
                    target[...] = jnp.zeros_like(target)

        part = None
        for ar, br in zip(a_refs, b_refs):
            d = lax.dot_general(ar[...].astype(BF16), br[...].astype(BF16), dn, preferred_element_type=F32)
            part = d if part is None else part + d

        if nk == 1:
            finish(part)
        else:
            target[...] += part
            if not in_place:
                @pl.when(k == nk - 1)
                def _():
                    finish(acc_ref[...])

    in_specs = [a_spec] * n_pairs + [b_spec] * n_pairs
    operands = list(a_ops) + list(b_ops)
    if res is not None:
        in_specs.append(res_spec)
        operands.append(res)
    if bf16_scale is not None:
        in_specs.append(bf16_scale_spec)
        operands.append(bf16_scale)
    return pl.pallas_call(
        body, grid=grid, name=name, in_specs=in_specs,
        out_specs=[out_spec] * n_out,
        out_shape=[jax.ShapeDtypeStruct(out_shape, dt) for dt in out_dtypes],
        scratch_shapes=[pltpu.VMEM(acc_shape, F32)],
        compiler_params=_params(("arbitrary",) * len(grid)),
    )(*operands)


def ffn_gate_up(h, wg, wu, tm=1024):
    S, D = h.shape
    nb = wg.shape[2]

    def body(h_ref, wg_ref, wu_ref, p_ref, r_ref, a_ref):
        for c in range(2):
            rows = slice(c * (tm // 2), (c + 1) * (tm // 2))
            hv = h_ref[rows, :]
            g = jnp.dot(hv, wg_ref[...], preferred_element_type=F32)
            u = jnp.dot(hv, wu_ref[...], preferred_element_type=F32)
            s = _sigmoid(g)
            p = g * s
            p_ref[rows, :] = p
            r_ref[rows, :] = u * (s * (1.0 + g * (1.0 - s)))
            a_ref[rows, :] = (p * u).astype(BF16)

    wspec = pl.BlockSpec((None, D, nb), lambda j, i: (j, 0, 0))
    ospec = pl.BlockSpec((None, tm, nb), lambda j, i: (j, i, 0))
    return pl.pallas_call(
        body, grid=(N_DEV, S // tm), name="ffn_gate_up",
        in_specs=[pl.BlockSpec((tm, D), lambda j, i: (i, 0)), wspec, wspec],
        out_specs=[ospec, ospec, ospec],
        out_shape=[jax.ShapeDtypeStruct((N_DEV, S, nb), F32), jax.ShapeDtypeStruct((N_DEV, S, nb), F32),
                   jax.ShapeDtypeStruct((N_DEV, S, nb), BF16)],
        compiler_params=_params(("arbitrary", "arbitrary")),
    )(h, wg, wu)


def ffn_bwd_hidden(dy, wd, p, r, tm=1024):
    S, D = dy.shape
    nb = wd.shape[1]

    def body(dy_ref, wd_ref, p_ref, r_ref, dg_ref, du_ref):
        for c in range(2):
            rows = slice(c * (tm // 2), (c + 1) * (tm // 2))
            da = lax.dot_general(dy_ref[rows, :], wd_ref[...], (((1,), (1,)), ((), ())),
                                 preferred_element_type=F32)
            du_ref[rows, :] = (da * p_ref[rows, :]).astype(BF16)
            dg_ref[rows, :] = (da * r_ref[rows, :]).astype(BF16)

    hspec = pl.BlockSpec((None, tm, nb), lambda j, i: (j, i, 0))
    return pl.pallas_call(
        body, grid=(N_DEV, S // tm), name="ffn_bwd_hidden",
        in_specs=[pl.BlockSpec((tm, D), lambda j, i: (i, 0)), pl.BlockSpec((None, nb, D), lambda j, i: (j, 0, 0)),
                  hspec, hspec],
        out_specs=[hspec, hspec],
        out_shape=[jax.ShapeDtypeStruct((N_DEV, S, nb), BF16), jax.ShapeDtypeStruct((N_DEV, S, nb), BF16)],
        compiler_params=_params(("arbitrary", "arbitrary")),
    )(dy, wd, p, r)


def ffn_forward(h, xres, wg, wu, wd, tm=1024):
    S, D = h.shape
    nb = wg.shape[2]
    g, u, a = ffn_gate_up(h, wg, wu)
    (xo,) = matmul(
        "ffn_down", [a], [wd], grid=(S // tm, N_DEV),
        a_spec=pl.BlockSpec((None, tm, nb), lambda i, j: (j, i, 0)),
        b_spec=pl.BlockSpec((None, nb, D), lambda i, j: (j, 0, 0)),
        out_spec=pl.BlockSpec((tm, D), lambda i, j: (i, 0)), out_shape=(S, D), out_dtypes=[F32],
        acc_shape=(tm, D), res=xres, res_spec=pl.BlockSpec((tm, D), lambda i, j: (i, 0)))
    return xo, (g, u, a)


def ffn_backward(dy_b, h, saved, wg, wu, wd, tm=1024, tk=2048):
    S, D = h.shape
    nb = wg.shape[2]
    g, u, a = saved
    dg, du = ffn_bwd_hidden(dy_b, wd, g, u)
    (dh,) = matmul(
        "ffn_dh", [dg, du], [wg, wu], grid=(S // tm, N_DEV),
        a_spec=pl.BlockSpec((None, tm, nb), lambda i, j: (j, i, 0)),
        b_spec=pl.BlockSpec((None, D, nb), lambda i, j: (j, 0, 0)),
        out_spec=pl.BlockSpec((tm, D), lambda i, j: (i, 0)), out_shape=(S, D), out_dtypes=[F32],
        acc_shape=(tm, D), trans_b=True)

    def wgrad_in(name, dhid):
        (dw,) = matmul(
            name, [dhid], [h], grid=(N_DEV, S // tk),
            a_spec=pl.BlockSpec((None, tk, nb), lambda j, k: (j, k, 0)),
            b_spec=pl.BlockSpec((tk, D), lambda j, k: (k, 0)),
            out_spec=pl.BlockSpec((None, nb, D), lambda j, k: (j, 0, 0)), out_shape=(N_DEV, nb, D),
            out_dtypes=[BF16], acc_shape=(nb, D), trans_a=True)
        return dw

    dwg = wgrad_in("ffn_dwg", dg)
    dwu = wgrad_in("ffn_dwu", du)
    (dwd,) = matmul(
        "ffn_dwd", [a], [dy_b], grid=(N_DEV, S // tk),
        a_spec=pl.BlockSpec((None, tk, nb), lambda j, k: (j, k, 0)),
        b_spec=pl.BlockSpec((tk, D), lambda j, k: (k, 0)),
        out_spec=pl.BlockSpec((None, nb, D), lambda j, k: (j, 0, 0)), out_shape=(N_DEV, nb, D),
        out_dtypes=[BF16], acc_shape=(nb, D), trans_a=True)
    return dh, dwg, dwu, dwd


def _pool_counts(row0, n, w):
    pos = row0 + lax.broadcasted_iota(jnp.int32, (n, 1), 0)
    return jnp.minimum(pos + 1, w).astype(F32)


def pool_forward(h, xres, w, scale, ts=256):
    S, D = h.shape
    G = len(POOL_WINDOWS)
    P = D // G
    hb = ts // POOL_HALO

    def body(h_ref, halo_ref, x_ref, w_ref, s_ref, xo_ref, p_ref):
        i = pl.program_id(0)
        for gi, win in enumerate(POOL_WINDOWS):
            cols = slice(gi * P, (gi + 1) * P)
            cur = h_ref[:, cols]
            halo = jnp.where(i > 0, halo_ref[:, cols], 0.0)
            acc = jnp.concatenate([halo, cur], axis=0)
            step = 1
            while step < win:
                acc = acc + pltpu.roll(acc, step, 0)
                step *= 2
            wsum = acc[POOL_HALO:, :]
            pooled = wsum / _pool_counts(i * ts, ts, win) - cur
            pb = pooled.astype(BF16)
            p_ref[:, cols] = pb
            mixed = jnp.dot(pb, w_ref[gi], preferred_element_type=F32)
            xo_ref[:, cols] = x_ref[:, cols] + mixed * s_ref[:, cols]

    row = pl.BlockSpec((ts, D), lambda i: (i, 0))
    return pl.pallas_call(
        body, grid=(S // ts,), name="pool_fwd",
        in_specs=[row, pl.BlockSpec((POOL_HALO, D), lambda i: (jnp.maximum(i * hb - 1, 0), 0)), row,
                  pl.BlockSpec((G, P, P), lambda i: (0, 0, 0)), pl.BlockSpec((1, D), lambda i: (0, 0))],
        out_specs=[row, row],
        out_shape=[jax.ShapeDtypeStruct((S, D), F32), jax.ShapeDtypeStruct((S, D), BF16)],
        compiler_params=_params(("arbitrary",)),
    )(h, h, xres, w, scale)


def pool_backward_mix(dx, pooled, w, scale, ts=256):
    S, D = dx.shape
    G = len(POOL_WINDOWS)
    P = D // G

    def body(dx_ref, p_ref, w_ref, s_ref, dm_ref, dp_ref, ds_ref):
        i = pl.program_id(0)
        parts = []
        for gi in range(G):
            cols = slice(gi * P, (gi + 1) * P)
            dxv = dx_ref[:, cols]
            dmb = (dxv * s_ref[:, cols]).astype(BF16)
            dm_ref[:, cols] = dmb
            dp_ref[:, cols] = lax.dot_general(dmb, w_ref[gi], (((1,), (1,)), ((), ())),
                                              preferred_element_type=F32)
            mixed = jnp.dot(p_ref[:, cols], w_ref[gi], preferred_element_type=F32)
            parts.append(jnp.sum(dxv * mixed, axis=0, keepdims=True))
        part = jnp.concatenate(parts, axis=1)

        @pl.when(i == 0)
        def _():
            ds_ref[...] = part

        @pl.when(i > 0)
        def _():
            ds_ref[...] += part

    row = pl.BlockSpec((ts, D), lambda i: (i, 0))
    vec = pl.BlockSpec((1, D), lambda i: (0, 0))
    return pl.pallas_call(
        body, grid=(S // ts,), name="pool_bwd_mix",
        in_specs=[row, row, pl.BlockSpec((G, P, P), lambda i: (0, 0, 0)), vec],
        out_specs=[row, row, vec],
        out_shape=[jax.ShapeDtypeStruct((S, D), BF16), jax.ShapeDtypeStruct((S, D), F32),
                   jax.ShapeDtypeStruct((1, D), F32)],
        compiler_params=_params(("arbitrary",)),
    )(dx, pooled, w, scale)


def pool_backward_window(dp, ts=256):
    S, D = dp.shape
    G = len(POOL_WINDOWS)
    P = D // G
    hb = ts // POOL_HALO
    n_i = S // ts
    n_rows = ts + POOL_HALO

    def body(dp_ref, halo_ref, dh_ref):
        i = pl.program_id(0)
        for gi, win in enumerate(POOL_WINDOWS):
            cols = slice(gi * P, (gi + 1) * P)
            cur = dp_ref[:, cols]
            halo = jnp.where(i < n_i - 1, halo_ref[:, cols], 0.0)
            acc = jnp.concatenate([cur / _pool_counts(i * ts, ts, win),
                                   halo / _pool_counts((i + 1) * ts, POOL_HALO, win)], axis=0)
            step = 1
            while step < win:
                acc = acc + pltpu.roll(acc, n_rows - step, 0)
                step *= 2
            dh_ref[:, cols] = acc[:ts, :] - cur

    row = pl.BlockSpec((ts, D), lambda i: (i, 0))
    return pl.pallas_call(
        body, grid=(n_i,), name="pool_bwd_window",
        in_specs=[row, pl.BlockSpec((POOL_HALO, D), lambda i: (jnp.minimum((i + 1) * hb, S // POOL_HALO - 1), 0))],
        out_specs=row,
        out_shape=jax.ShapeDtypeStruct((S, D), F32),
        compiler_params=_params(("arbitrary",)),
    )(dp, dp)


_HG_LEVELS = (32, 16, 8, 4, 2, 1)
_N_LEV = len(_HG_LEVELS) + 1


def _hgrn_constants():
    C = HG_CHUNK
    t = np.arange(C)
    tri = (t[None, :] <= t[:, None]).astype(np.float32)
    blocks = [tri]
    masks, upq, upk = [], [], []
    for m in _HG_LEVELS:
        p = (t // (2 * m)) * 2 * m + m - 1
        blocks.append(tri[p])
        masks.append(((t[:, None] // (2 * m)) == (t[None, :] // (2 * m))).astype(np.float32))
        upper = (t % (2 * m)) >= m
        upq.append(np.repeat(upper[:, None], HEAD, 1).astype(np.float32))
        upk.append(np.repeat(~upper[:, None], HEAD, 1).astype(np.float32))
    blocks.append(tri)
    masks.append(np.eye(C, dtype=np.float32))
    upq.append(np.ones((C, HEAD), np.float32))
    upk.append(np.ones((C, HEAD), np.float32))
    mstack = np.concatenate(blocks, axis=0)
    mstack3 = np.concatenate([mstack] * 3, axis=1)
    trirev3 = np.concatenate([tri.T] * 3, axis=1)
    return (jnp.asarray(mstack3, BF16), jnp.asarray(np.stack(masks)), jnp.asarray(np.stack(upq)),
            jnp.asarray(np.stack(upk)), jnp.asarray(trirev3, BF16))


def _split3(x):
    hi = x.astype(BF16)
    r1 = x - hi.astype(F32)
    mid = r1.astype(BF16)
    lo = (r1 - mid.astype(F32)).astype(BF16)
    return jnp.concatenate([hi, mid, lo], axis=0)


def _hgrn_chunk_common(qa, fa, lbv, mstack3, upq, upk):
    sq = _sigmoid(qa)
    q = qa * sq
    sf = _sigmoid(fa)
    f = lbv + (1.0 - lbv) * sf
    g = jnp.log(f)
    k = 1.0 - f
    gall = jnp.dot(mstack3, _split3(g), preferred_element_type=F32).reshape(_N_LEV + 1, HG_CHUNK, HEAD)
    G = gall[0]
    eq_exp = G[None] - gall[1:]
    eq = jnp.exp(jnp.minimum(eq_exp, 0.0)) * upq
    ek = jnp.exp(jnp.minimum(-eq_exp, 0.0)) * upk
    Qs = (q[None] * eq).astype(BF16)
    Ks = (k[None] * ek).astype(BF16)
    return sq, q, sf, f, k, G, eq, ek, Qs, Ks


def hgrn_forward(proj, lb, hg_norm, ts=512):
    S = proj.shape[0]
    nh = lb.shape[1] // HEAD
    C = HG_CHUNK
    ncs = ts // C
    mstack3, masks, upq, upk, _ = _hgrn_constants()

    def body(qa_ref, fa_ref, ia_ref, ga_ref, lb_ref, gn_ref, ms_ref, mk_ref, uq_ref, uk_ref,
             oa_ref, oraw_ref, st_ref, state):
        tt = pl.program_id(1)

        @pl.when(tt == 0)
        def _():
            state[...] = jnp.zeros_like(state)

        gn = gn_ref[...]

        def chunk(c, carry):
            sl = pl.ds(pl.multiple_of(c * C, C), C)
            for hh in range(HG_HEADS_PER_BLOCK):
                cols = slice(hh * HEAD, (hh + 1) * HEAD)
                qa, fa, v, ga = qa_ref[sl, cols], fa_ref[sl, cols], ia_ref[sl, cols], ga_ref[sl, cols]
                _, q, _, _, k, G, _, _, Qs, Ks = _hgrn_chunk_common(qa, fa, lb_ref[:, cols], ms_ref[...],
                                                                    uq_ref[...], uk_ref[...])
                att7 = lax.dot_general(Qs, Ks, (((2,), (2,)), ((0,), (0,))), preferred_element_type=F32)
                att = jnp.sum(att7 * mk_ref[...], axis=0)
                st = state[hh]
                st_ref[hh, c] = st
                vb = v.astype(BF16)
                qg = (q * jnp.exp(G)).astype(BF16)
                o = jnp.dot(att.astype(BF16), vb, preferred_element_type=F32)
                o = o + lax.dot_general(qg, st.astype(BF16), (((1,), (1,)), ((), ())),
                                        preferred_element_type=F32)
                g_last = G[C - 1:C, :]
                kh = (k * jnp.exp(g_last - G)).astype(BF16)
                state[hh] = st * jnp.exp(g_last) + lax.dot_general(vb, kh, (((0,), (0,)), ((), ())),
                                                                   preferred_element_type=F32)
                oraw_ref[sl, cols] = o
                r = lax.rsqrt(jnp.mean(o * o, axis=-1, keepdims=True) + RMS_EPS)
                oa_ref[sl, cols] = (((o * r) * gn) * (ga * _sigmoid(ga))).astype(BF16)
            return carry

        lax.fori_loop(0, ncs, chunk, 0)

    hpb = HG_HEADS_PER_BLOCK
    wide = hpb * HEAD

    def col(m0):
        return pl.BlockSpec((ts, wide), lambda h, t: (t, m0 // hpb + h))

    const3 = lambda shape: pl.BlockSpec(shape, lambda h, t: (0, 0, 0))
    return pl.pallas_call(
        body, grid=(nh // hpb, S // ts), name="hgrn_fwd",
        in_specs=[col(0), col(nh), col(2 * nh), col(3 * nh),
                  pl.BlockSpec((1, wide), lambda h, t: (0, h)), pl.BlockSpec((1, HEAD), lambda h, t: (0, 0)),
                  pl.BlockSpec(mstack3.shape, lambda h, t: (0, 0)), const3(masks.shape), const3(upq.shape),
                  const3(upk.shape)],
        out_specs=[pl.BlockSpec((ts, wide), lambda h, t: (t, h)), pl.BlockSpec((ts, wide), lambda h, t: (t, h)),
                   pl.BlockSpec((hpb, ncs, HEAD, HEAD), lambda h, t: (h, t, 0, 0))],
        out_shape=[jax.ShapeDtypeStruct((S, nh * HEAD), BF16), jax.ShapeDtypeStruct((S, nh * HEAD), F32),
                   jax.ShapeDtypeStruct((nh, S // C, HEAD, HEAD), F32)],
        scratch_shapes=[pltpu.VMEM((hpb, HEAD, HEAD), F32)],
        compiler_params=_params(("arbitrary", "arbitrary")),
    )(proj, proj, proj, proj, lb, hg_norm, mstack3, masks, upq, upk)


def hgrn_backward(dcat, proj, oraw, states, lb, hg_norm, ts=512):
    S = proj.shape[0]
    nh = lb.shape[1] // HEAD
    C = HG_CHUNK
    ncs = ts // C
    nt = S // ts
    mstack3, masks, upq, upk, trirev3 = _hgrn_constants()

    def body(do_ref, qa_ref, fa_ref, ia_ref, ga_ref, or_ref, st_ref, lb_ref, gn_ref, ms_ref, mk_ref, uq_ref,
             uk_ref, tr_ref, dqa_ref, dfa_ref, dia_ref, dga_ref, dlb_ref, dgn_ref, dstate):
        tt = pl.program_id(1)

        @pl.when(tt == 0)
        def _():
            dstate[...] = jnp.zeros_like(dstate)
            dlb_ref[...] = jnp.zeros_like(dlb_ref)
            dgn_ref[...] = jnp.zeros_like(dgn_ref)

        gn = gn_ref[...]

        def chunk(cc, carry):
            c = ncs - 1 - cc
            sl = pl.ds(pl.multiple_of(c * C, C), C)
            for hh in range(HG_HEADS_PER_BLOCK):
                cols = slice(hh * HEAD, (hh + 1) * HEAD)
                lbv = lb_ref[:, cols]
                qa, fa, v, ga = qa_ref[sl, cols], fa_ref[sl, cols], ia_ref[sl, cols], ga_ref[sl, cols]
                sq, q, sf, f, k, G, eq, ek, Qs, Ks = _hgrn_chunk_common(qa, fa, lbv, ms_ref[...], uq_ref[...],
                                                                        uk_ref[...])
                mk = mk_ref[...]
                att7 = lax.dot_general(Qs, Ks, (((2,), (2,)), ((0,), (0,))), preferred_element_type=F32)
                att = jnp.sum(att7 * mk, axis=0)
                o = or_ref[sl, cols]
                dO = do_ref[sl, cols]
                sg = _sigmoid(ga)
                r = lax.rsqrt(jnp.mean(o * o, axis=-1, keepdims=True) + RMS_EPS)
                xh = o * r
                dga_ref[sl, cols] = (dO * (xh * gn) * (sg * (1.0 + ga * (1.0 - sg)))).astype(BF16)
                don = dO * (ga * sg)
                dgn_ref[hh] += jnp.sum(don * xh, axis=0, keepdims=True)
                dxh = don * gn
                do = r * (dxh - xh * jnp.mean(dxh * xh, axis=-1, keepdims=True))
                dob = do.astype(BF16)
                st = st_ref[hh, c]
                dst = dstate[hh]
                dstb = dst.astype(BF16)
                vb = v.astype(BF16)
                eG = jnp.exp(G)
                g_last = G[C - 1:C, :]
                e_last = jnp.exp(g_last)
                e_tail = jnp.exp(g_last - G)
                qg = (q * eG).astype(BF16)
                kh = (k * e_tail).astype(BF16)
                dq_inter = jnp.dot(dob, st.astype(BF16), preferred_element_type=F32) * eG
                dk_inter = jnp.dot(vb, dstb, preferred_element_type=F32) * e_tail
                dv = lax.dot_general(kh, dstb, (((1,), (1,)), ((), ())), preferred_element_type=F32)
                dv = dv + lax.dot_general(att.astype(BF16), dob, (((0,), (0,)), ((), ())),
                                          preferred_element_type=F32)
                dA = lax.dot_general(dob, vb, (((1,), (1,)), ((), ())), preferred_element_type=F32)
                dA7 = (dA[None] * mk).astype(BF16)
                dAT7 = (dA.T[None] * mk).astype(BF16)
                dQs = lax.dot_general(dA7, Ks, (((2,), (1,)), ((0,), (0,))), preferred_element_type=F32)
                dKs = lax.dot_general(dAT7, Qs, (((2,), (1,)), ((0,), (0,))), preferred_element_type=F32)
                dq = dq_inter + jnp.sum(dQs * eq, axis=0)
                dk = dk_inter + jnp.sum(dKs * ek, axis=0)
                dG = (jnp.sum(Qs.astype(F32) * dQs - Ks.astype(F32) * dKs, axis=0)
                      + q * dq_inter - k * dk_inter)
                last_extra = (jnp.sum(k * dk_inter, axis=0, keepdims=True)
                              + e_last * jnp.sum(dst * st, axis=0, keepdims=True))
                is_last = lax.broadcasted_iota(jnp.int32, (C, 1), 0) == C - 1
                dG = dG + jnp.where(is_last, last_extra, 0.0)
                dg = jnp.dot(tr_ref[...], _split3(dG), preferred_element_type=F32)
                df = dg / f - dk
                dfa_ref[sl, cols] = (df * (1.0 - lbv) * (sf * (1.0 - sf))).astype(BF16)
                dlb_ref[:, cols] += jnp.sum(df * (1.0 - sf), axis=0, keepdims=True)
                dqa_ref[sl, cols] = (dq * (sq * (1.0 + qa * (1.0 - sq)))).astype(BF16)
                dia_ref[sl, cols] = dv.astype(BF16)
                dstate[hh] = dst * e_last + lax.dot_general(dob, qg, (((0,), (0,)), ((), ())),
                                                            preferred_element_type=F32)
            return carry

        lax.fori_loop(0, ncs, chunk, 0)

    hpb = HG_HEADS_PER_BLOCK
    wide = hpb * HEAD

    def col(m0):
        return pl.BlockSpec((ts, wide), lambda h, t: (nt - 1 - t, m0 // hpb + h))

    const3 = lambda shape: pl.BlockSpec(shape, lambda h, t: (0, 0, 0))
    const2 = lambda shape: pl.BlockSpec(shape, lambda h, t: (0, 0))
    ocol = pl.BlockSpec((ts, wide), lambda h, t: (nt - 1 - t, h))
    half = nh * HEAD
    return pl.pallas_call(
        body, grid=(nh // hpb, nt), name="hgrn_bwd",
        in_specs=[col(0), col(0), col(nh), col(2 * nh), col(3 * nh), col(0),
                  pl.BlockSpec((hpb, ncs, HEAD, HEAD), lambda h, t: (h, nt - 1 - t, 0, 0)),
                  pl.BlockSpec((1, wide), lambda h, t: (0, h)), const2((1, HEAD)),
                  const2(mstack3.shape), const3(masks.shape), const3(upq.shape), const3(upk.shape),
                  const2(trirev3.shape)],
        out_specs=[ocol, ocol, ocol, ocol, pl.BlockSpec((1, wide), lambda h, t: (0, h)),
                   pl.BlockSpec((hpb, 1, HEAD), lambda h, t: (h, 0, 0))],
        out_shape=[jax.ShapeDtypeStruct((S, half), BF16)] * 4
                  + [jax.ShapeDtypeStruct((1, half), F32), jax.ShapeDtypeStruct((nh, 1, HEAD), F32)],
        scratch_shapes=[pltpu.VMEM((hpb, HEAD, HEAD), F32)],
        compiler_params=_params(("arbitrary", "arbitrary")),
    )(dcat, proj, proj, proj, proj, oraw, states, lb, hg_norm, mstack3, masks, upq, upk, trirev3)


SB_SUB = 128
LOG2_E = 1.4426950408889634
SB_SCALE = 1.0 / math.sqrt(HEAD)
SB_QUERY_SCALE = SB_SCALE * LOG2_E


def _split2(x):
    hi = x.astype(BF16)
    lo = (x - hi.astype(F32)).astype(BF16)
    return jnp.concatenate([hi, lo], axis=1)


def _sb_constants():
    j = np.arange(SB_SUB)
    after = (j[:, None] > j[None, :]).astype(np.float32)
    before = (j[:, None] < j[None, :]).astype(np.float32)
    return (jnp.asarray(np.concatenate([after, after], axis=0), BF16),
            jnp.asarray(np.concatenate([before, before], axis=0), BF16))


def _sb_tri(i):
    return (i * (i + 1)) // 2


def _sb_diag_mask(t):
    return lax.broadcasted_iota(jnp.int32, (t, t), 1) < lax.broadcasted_iota(jnp.int32, (t, t), 0)


def _sb_scores(q, k_ref, col0, t):
    ks = k_ref[pl.ds(pl.multiple_of(col0, t), t), :]
    return lax.dot_general(q, ks, (((1,), (1,)), ((), ())), preferred_element_type=F32)


def _sb_weights(z, mask, run, after2):
    nsub = z.shape[1] // SB_SUB
    nz = -z
    lk = jnp.minimum(nz, 0.0) - jnp.log(1.0 + jnp.exp2(jnp.minimum(z, nz))) * LOG2_E
    if mask is not None:
        lk = jnp.where(mask, lk, 0.0)
    locs, tots = [], []
    for b in range(nsub):
        lkb = lk[:, b * SB_SUB:(b + 1) * SB_SUB]
        loc = jnp.dot(_split2(lkb), after2, preferred_element_type=F32)
        locs.append(loc)
        tots.append(loc[:, 0:1] + lkb[:, 0:1])
    ws = [None] * nsub
    for b in reversed(range(nsub)):
        sl = slice(b * SB_SUB, (b + 1) * SB_SUB)
        ws[b] = jnp.exp2(z[:, sl] + lk[:, sl] + (locs[b] + run))
        run = run + tots[b]
    w = jnp.concatenate(ws, axis=1)
    if mask is not None:
        w = jnp.where(mask, w, 0.0)
    return w, run


def sb_forward(projb, nh, m0, t=512):
    S = projb.shape[0]
    after2, _ = _sb_constants()
    n_i = S // t

    def body(q_ref, k_ref, v_ref, af_ref, o_ref, w_hbm, wbuf, wsem):
        h = pl.program_id(0)
        i = pl.program_id(1)
        q = q_ref[...]
        after = af_ref[...]
        base = _sb_tri(i)

        def store(slot, jb):
            return pltpu.make_async_copy(wbuf.at[slot], w_hbm.at[h, base + jb], wsem.at[slot])

        def block(n, jb, run, mask):
            slot = n % 2

            @pl.when(n >= 2)
            def _():
                store(slot, jb).wait()

            z = _sb_scores(q, k_ref, jb * t, t)
            w, run = _sb_weights(z, mask, run, after)
            wb = w.astype(BF16)
            wbuf[slot] = wb
            store(slot, jb).start()
            vs = v_ref[pl.ds(pl.multiple_of(jb * t, t), t), :]
            return run, jnp.dot(wb, vs, preferred_element_type=F32)

        run, acc = block(0, i, jnp.zeros((t, 1), F32), _sb_diag_mask(t))

        def step(n, carry):
            run, acc = carry
            run, part = block(n + 1, i - 1 - n, run, None)
            return run, acc + part

        _, acc = lax.fori_loop(0, i, step, (run, acc))
        o_ref[...] = acc.astype(BF16)
        store(i % 2, 0).wait()

        @pl.when(i >= 1)
        def _():
            store((i + 1) % 2, 0).wait()

    return pl.pallas_call(
        body, grid=(nh, n_i), name="sb_fwd",
        in_specs=[pl.BlockSpec((t, HEAD), lambda h, i: (i, m0 + h)),
                  pl.BlockSpec((S, HEAD), lambda h, i: (0, m0 + nh + h)),
                  pl.BlockSpec((S, HEAD), lambda h, i: (0, m0 + 2 * nh + h)),
                  pl.BlockSpec(after2.shape, lambda h, i: (0, 0))],
        out_specs=[pl.BlockSpec((t, HEAD), lambda h, i: (i, h)), pl.BlockSpec(memory_space=pl.ANY)],
        out_shape=[jax.ShapeDtypeStruct((S, nh * HEAD), BF16),
                   jax.ShapeDtypeStruct((nh, _sb_tri(n_i), t, t), BF16)],
        scratch_shapes=[pltpu.VMEM((2, t, t), BF16), pltpu.SemaphoreType.DMA((2,))],
        compiler_params=_params(("arbitrary", "arbitrary")),
    )(projb, projb, projb, after2)


def sb_backward(dcat, projb, w_all, nh, m0, t=512):
    S = projb.shape[0]
    _, before2 = _sb_constants()
    n_i = S // t
    nsub = t // SB_SUB

    def body(do_ref, q_ref, k_ref, v_ref, bf_ref, w_hbm, dq_ref, dk_ref, dv_ref, dk_acc, dv_acc, wbuf, wsem):
        h = pl.program_id(0)
        i = pl.program_id(1)

        @pl.when(i == 0)
        def _():
            dk_acc[...] = jnp.zeros_like(dk_acc)
            dv_acc[...] = jnp.zeros_like(dv_acc)

        q = q_ref[...]
        dob = do_ref[...].astype(BF16)
        before = bf_ref[...]
        base = _sb_tri(i)

        def load(slot, jb):
            return pltpu.make_async_copy(w_hbm.at[h, base + jb], wbuf.at[slot], wsem.at[slot])

        load(0, 0).start()

        def left_to_right(jb, run, dq, mask):
            slot = jb % 2
            load(slot, jb).wait()

            @pl.when(jb < i)
            def _():
                load(1 - slot, jb + 1).start()

            ksl = pl.ds(pl.multiple_of(jb * t, t), t)
            wb = wbuf[slot]
            z = _sb_scores(q, k_ref, jb * t, t)
            dw = lax.dot_general(dob, v_ref[ksl, :], (((1,), (1,)), ((), ())), preferred_element_type=F32)
            d = dw * wb.astype(F32)
            dv_acc[ksl, :] += lax.dot_general(wb, dob, (((0,), (0,)), ((), ())), preferred_element_type=F32)
            sig = 1.0 / (1.0 + jnp.exp2(-z))
            das = []
            for b in range(nsub):
                db = d[:, b * SB_SUB:(b + 1) * SB_SUB]
                prefix = run + jnp.dot(_split2(db), before, preferred_element_type=F32)
                das.append(db - sig[:, b * SB_SUB:(b + 1) * SB_SUB] * (db + prefix))
                run = prefix[:, SB_SUB - 1:SB_SUB] + db[:, SB_SUB - 1:SB_SUB]
            da = jnp.concatenate(das, axis=1)
            if mask is not None:
                da = jnp.where(mask, da, 0.0)
            dab = (da * SB_SCALE).astype(BF16)
            dq = dq + jnp.dot(dab, k_ref[ksl, :], preferred_element_type=F32)
            dk_acc[ksl, :] += lax.dot_general(dab, q, (((0,), (0,)), ((), ())), preferred_element_type=F32)
            return run, dq

        run, dq = lax.fori_loop(0, i, lambda jb, c: left_to_right(jb, c[0], c[1], None),
                                (jnp.zeros((t, 1), F32), jnp.zeros((t, HEAD), F32)))
        _, dq = left_to_right(i, run, dq, _sb_diag_mask(t))
        dq_ref[...] = dq.astype(BF16)

        @pl.when(i == n_i - 1)
        def _():
            dk_ref[...] = (dk_acc[...] * (1.0 / SB_QUERY_SCALE)).astype(BF16)
            dv_ref[...] = dv_acc[...].astype(BF16)

    half = nh * HEAD
    full = pl.BlockSpec((S, HEAD), lambda h, i: (0, h))
    return pl.pallas_call(
        body, grid=(nh, n_i), name="sb_bwd",
        in_specs=[pl.BlockSpec((t, HEAD), lambda h, i: (i, nh + h)),
                  pl.BlockSpec((t, HEAD), lambda h, i: (i, m0 + h)),
                  pl.BlockSpec((S, HEAD), lambda h, i: (0, m0 + nh + h)),
                  pl.BlockSpec((S, HEAD), lambda h, i: (0, m0 + 2 * nh + h)),
                  pl.BlockSpec(before2.shape, lambda h, i: (0, 0)), pl.BlockSpec(memory_space=pl.ANY)],
        out_specs=[pl.BlockSpec((t, HEAD), lambda h, i: (i, h)), full, full],
        out_shape=[jax.ShapeDtypeStruct((S, half), BF16)] * 3,
        scratch_shapes=[pltpu.VMEM((S, HEAD), F32), pltpu.VMEM((S, HEAD), F32),
                        pltpu.VMEM((2, t, t), BF16), pltpu.SemaphoreType.DMA((2,))],
        compiler_params=_params(("arbitrary", "arbitrary")),
    )(dcat, projb, projb, projb, before2, w_all)


def local_step(x, target, mix_norm, ffn_norm, final_norm, lb_logits, hg_norm, get_w_in, get_w_rest, send):
    S, D = x.shape
    half = D // 2
    nh = half // HEAD
    tm = 512
    tk = 2048
    row = lambda i, j: (i, 0)

    lb = jax.nn.softmax(lb_logits, axis=0)[0:1]

    h0, r0 = rms_fwd(x, mix_norm[0:1], BF16)
    w_in = get_w_in(h0)
    nbi = w_in.shape[2]
    col = jnp.arange(N_DEV * nbi) // half
    col_scale = jnp.where(col == 4, SB_QUERY_SCALE, 1.0).astype(F32)[None]
    proj, projb = matmul(
        "proj_in", [h0], [w_in], grid=(N_DEV, S // tm, 1),
        a_spec=pl.BlockSpec((tm, D), lambda j, i, k: (i, 0)),
        b_spec=pl.BlockSpec((None, D, nbi), lambda j, i, k: (j, 0, 0)),
        out_spec=pl.BlockSpec((tm, nbi), lambda j, i, k: (i, j)), out_shape=(S, N_DEV * nbi),
        out_dtypes=[F32, BF16], acc_shape=(8, 128),
        bf16_scale=col_scale, bf16_scale_spec=pl.BlockSpec((1, nbi), lambda j, i, k: (0, j)))
    oa, oraw, states = hgrn_forward(proj, lb, hg_norm)
    ob, sb_weights = sb_forward(projb, nh, 4 * nh)
    cat = jnp.concatenate([oa, ob], axis=1)
    w_out, pool_w, pool_scale, wg, wu, wd = get_w_rest(cat)
    (x1,) = matmul(
        "mix_out", [cat], [w_out], grid=(S // tm, 1),
        a_spec=pl.BlockSpec((tm, D), row), b_spec=pl.BlockSpec((D, D), lambda i, k: (0, 0)),
        out_spec=pl.BlockSpec((tm, D), row), out_shape=(S, D), out_dtypes=[F32], acc_shape=(8, 128),
        res=x, res_spec=pl.BlockSpec((tm, D), row))
    h1, r1 = rms_fwd(x1, ffn_norm[0:1], BF16)
    x2, ffn0 = ffn_forward(h1, x1, wg[0], wu[0], wd[0])

    h2, r2 = rms_fwd(x2, mix_norm[1:2], F32)
    x3, pooled = pool_forward(h2, x2, pool_w, pool_scale)
    h3, r3 = rms_fwd(x3, ffn_norm[1:2], BF16)
    x4, ffn1 = ffn_forward(h3, x3, wg[1], wu[1], wd[1])

    loss_blk, dx4, dx4b, d_final = loss_and_final_bwd(x4, final_norm, target)

    dh3, dwg1, dwu1, dwd1 = ffn_backward(dx4b, h3, ffn1, wg[1], wu[1], wd[1])
    dh3 = send("ffn1", dict(ffn_w_gate_1=dwg1, ffn_w_up_1=dwu1, ffn_w_down_1=dwd1), dh3)
    dx3, _, d_ffn1 = rms_bwd(dh3, x3, r3, ffn_norm[1:2], dx4)
    dmixed, dpooled, d_pscale = pool_backward_mix(dx3, pooled, pool_w, pool_scale)
    G = len(POOL_WINDOWS)
    P = D // G
    (d_pool_w,) = matmul(
        "pool_dw", [pooled], [dmixed], grid=(G, S // tk),
        a_spec=pl.BlockSpec((tk, P), lambda g, k: (k, g)), b_spec=pl.BlockSpec((tk, P), lambda g, k: (k, g)),
        out_spec=pl.BlockSpec((None, P, P), lambda g, k: (g, 0, 0)), out_shape=(G, P, P), out_dtypes=[BF16],
        acc_shape=(P, P), trans_a=True)
    dh2 = pool_backward_window(dpooled)
    dx2, dx2b, d_mix1 = rms_bwd(dh2, x2, r2, mix_norm[1:2], dx3)

    dh1, dwg0, dwu0, dwd0 = ffn_backward(dx2b, h1, ffn0, wg[0], wu[0], wd[0])
    dx1, dx1b, d_ffn0 = rms_bwd(dh1, x1, r1, ffn_norm[0:1], dx2)
    (dcat,) = matmul(
        "mix_out_dx", [dx1b], [w_out], grid=(S // tm, 1),
        a_spec=pl.BlockSpec((tm, D), row), b_spec=pl.BlockSpec((D, D), lambda i, k: (0, 0)),
        out_spec=pl.BlockSpec((tm, D), row), out_shape=(S, D), out_dtypes=[F32], acc_shape=(8, 128),
        trans_b=True)
    (d_w_out,) = matmul(
        "mix_out_dw", [cat], [dx1b], grid=(2, S // tk),
        a_spec=pl.BlockSpec((tk, half), lambda m, k: (k, m)), b_spec=pl.BlockSpec((tk, D), lambda m, k: (k, 0)),
        out_spec=pl.BlockSpec((half, D), lambda m, k: (m, 0)), out_shape=(D, D), out_dtypes=[BF16],
        acc_shape=(half, D), trans_a=True)
    dcat = send("layer0", dict(ffn_w_gate_0=dwg0, ffn_w_up_0=dwu0, ffn_w_down_0=dwd0, pool_w=d_pool_w,
                               ab_w_out=d_w_out), dcat)
    dqa, dfa, dia, dga, d_lb, d_hg = hgrn_backward(dcat, proj, oraw, states, lb, hg_norm)
    dqb, dkb, dvb = sb_backward(dcat, projb, sb_weights, nh, 4 * nh)
    dproj = jnp.concatenate([dqa, dfa, dia, dga, dqb, dkb, dvb], axis=1)
    (d_w_in,) = matmul(
        "proj_in_dw", [h0], [dproj], grid=(N_DEV, S // tk),
        a_spec=pl.BlockSpec((tk, D), lambda j, k: (k, 0)), b_spec=pl.BlockSpec((tk, nbi), lambda j, k: (k, j)),
        out_spec=pl.BlockSpec((None, D, nbi), lambda j, k: (j, 0, 0)), out_shape=(N_DEV, D, nbi),
        out_dtypes=[BF16], acc_shape=(D, nbi), trans_a=True)
    dproj = send("w_in", dict(ab_w_in=d_w_in), dproj)
    (dh0,) = matmul(
        "proj_in_dx", [dproj], [w_in], grid=(S // tm, N_DEV),
        a_spec=pl.BlockSpec((tm, nbi), lambda i, j: (i, j)),
        b_spec=pl.BlockSpec((None, D, nbi), lambda i, j: (j, 0, 0)),
        out_spec=pl.BlockSpec((tm, D), row), out_shape=(S, D), out_dtypes=[F32], acc_shape=(tm, D),
        trans_b=True)
    dx0, _, d_mix0 = rms_bwd(dh0, x, r0, mix_norm[0:1], dx1)

    d_l0 = d_lb * lb * (1.0 - lb)
    small = dict(
        loss=loss_blk[0:1, 0:1],
        mix_norm=jnp.concatenate([d_mix0, d_mix1], axis=0),
        ffn_norm=jnp.concatenate([d_ffn0, d_ffn1], axis=0),
        final_norm=d_final,
        lb_logits=jnp.concatenate([d_l0, -d_l0], axis=0),
        hg_out_norm=jnp.sum(d_hg, axis=0),
        pool_scale=d_pscale,
    )
    return dx0, small


def _my_index():
    return 4 * lax.axis_index("x") + 2 * lax.axis_index("y") + lax.axis_index("c")


def _peer(r):
    x, y, c = lax.axis_index("x"), lax.axis_index("y"), lax.axis_index("c")
    px = 1 - x if (r >> 2) & 1 else x
    py = 1 - y if (r >> 1) & 1 else y
    pc = 1 - c if r & 1 else c
    return (px, py, pc), 4 * px + 2 * py + pc


def exchange(name, arrays, gather):
    n = len(arrays)
    n_peers = N_DEV - 1

    def body(*refs):
        ins, outs = refs[:n], refs[n:2 * n]
        send_sems, recv_sems, local_sems = refs[2 * n:]
        me = _my_index()
        local = []
        for a in range(n):
            src = ins[a] if gather else ins[a].at[me]
            cp = pltpu.make_async_copy(src, outs[a].at[me], local_sems.at[a])
            cp.start()
            local.append(cp)
        remote = []
        for a in range(n):
            for r in range(1, N_DEV):
                peer, pidx = _peer(r)
                src = ins[a] if gather else ins[a].at[pidx]
                cp = pltpu.make_async_remote_copy(
                    src_ref=src, dst_ref=outs[a].at[me], send_sem=send_sems.at[a * n_peers + r - 1],
                    recv_sem=recv_sems.at[a * n_peers + r - 1], device_id=peer, device_id_type=MESH)
                cp.start()
                remote.append((cp, a, r))
        for cp, a, r in remote:
            _, pidx = _peer(r)
            src = ins[a] if gather else ins[a].at[pidx]
            pltpu.make_async_remote_copy(
                src_ref=src, dst_ref=outs[a].at[pidx], send_sem=send_sems.at[a * n_peers + r - 1],
                recv_sem=recv_sems.at[a * n_peers + r - 1], device_id=_peer(r)[0], device_id_type=MESH).wait_recv()
        for cp, a, r in remote:
            cp.wait_send()
        for cp in local:
            cp.wait()

    out_shape = [jax.ShapeDtypeStruct(((N_DEV,) + a.shape) if gather else a.shape, a.dtype) for a in arrays]
    any_spec = pl.BlockSpec(memory_space=pl.ANY)
    return pl.pallas_call(
        body, name=name, in_specs=[any_spec] * n, out_specs=[any_spec] * n, out_shape=out_shape,
        scratch_shapes=[pltpu.SemaphoreType.DMA((n * n_peers,)), pltpu.SemaphoreType.DMA((n * n_peers,)),
                        pltpu.SemaphoreType.DMA((n,))],
    )(*arrays)


_HBM = pl.BlockSpec(memory_space=pltpu.HBM)
_SEM = pl.BlockSpec(memory_space=pltpu.SEMAPHORE)
_EFFECT = pltpu.SideEffectType.DATAFLOW_SIDE_EFFECTING


def _landing(arrays, gather):
    me = _my_index()
    lands = []
    for a in arrays:
        own = a[None] if gather else lax.dynamic_slice_in_dim(a, me, 1, axis=0)
        shape = ((N_DEV,) + a.shape) if gather else a.shape
        lands.append(lax.dynamic_update_slice_in_dim(lax.empty(shape, a.dtype), own, me, axis=0))
    return lands


def exchange_start(name, arrays, gather, carry):
    n = len(arrays)
    n_peers = N_DEV - 1
    lands = _landing(arrays, gather)
    n_thru = 2 * n + 1

    def body(*refs):
        src, land = refs[:n], refs[n:2 * n]
        send_sems, recv_sems = refs[n_thru], refs[n_thru + 1]
        token = refs[-1]
        me = _my_index()
        for a in range(n):
            for r in range(1, N_DEV):
                peer, pidx = _peer(r)
                pltpu.make_async_remote_copy(
                    src_ref=src[a] if gather else src[a].at[pidx], dst_ref=land[a].at[me],
                    send_sem=send_sems.at[a * n_peers + r - 1], recv_sem=recv_sems.at[a * n_peers + r - 1],
                    device_id=peer, device_id_type=MESH).start()
        token[...] = jnp.zeros_like(token)

    operands = list(arrays) + lands + [carry]
    outs = pl.pallas_call(
        body, name=name,
        out_shape=(pltpu.SemaphoreType.DMA((n * n_peers,)), pltpu.SemaphoreType.DMA((n * n_peers,)),
                   *[pltpu.HBM(a.shape, a.dtype) for a in operands], jax.ShapeDtypeStruct((8, 128), F32)),
        in_specs=[_HBM] * n_thru,
        out_specs=(_SEM, _SEM, *([_HBM] * n_thru), pl.BlockSpec(memory_space=pltpu.VMEM)),
        input_output_aliases={i: 2 + i for i in range(n_thru)},
        compiler_params=pltpu.CompilerParams(has_side_effects=_EFFECT),
    )(*[pltpu.with_memory_space_constraint(a, pltpu.HBM) for a in operands])
    handle = (outs[0], outs[1], list(outs[2:2 + n]), list(outs[2 + n:2 + 2 * n]), gather)
    return handle, outs[2 + 2 * n]


def exchange_wait(name, handle, after):
    send_sems, recv_sems, srcs, lands, gather = handle
    n = len(srcs)
    n_peers = N_DEV - 1

    def body(*refs):
        src, land = refs[:n], refs[n:2 * n]
        send_s, recv_s = refs[2 * n], refs[2 * n + 1]
        for a in range(n):
            for r in range(1, N_DEV):
                peer, pidx = _peer(r)
                cp = pltpu.make_async_remote_copy(
                    src_ref=src[a] if gather else src[a].at[pidx], dst_ref=land[a].at[pidx],
                    send_sem=send_s.at[a * n_peers + r - 1], recv_sem=recv_s.at[a * n_peers + r - 1],
                    device_id=peer, device_id_type=MESH)
                cp.wait_send()
                cp.wait_recv()

    shapes = [pltpu.HBM(a.shape, a.dtype) for a in srcs] + [pltpu.HBM(l.shape, l.dtype) for l in lands]
    outs = pl.pallas_call(
        body, name=name, out_shape=tuple(shapes),
        in_specs=[_HBM] * (2 * n) + [_SEM, _SEM, pl.BlockSpec(memory_space=pl.ANY)],
        out_specs=tuple([_HBM] * (2 * n)),
        input_output_aliases={i: i for i in range(2 * n)},
        compiler_params=pltpu.CompilerParams(has_side_effects=_EFFECT),
    )(*srcs, *lands, send_sems, recv_sems, after)
    return list(outs[n:])


def _row_tile(rows, cap=256):
    best = None
    for t in range(16, min(rows, cap) + 1, 16):
        if rows % t == 0:
            best = t
    return best if best is not None else rows


def sum_slots(name, recv):
    n, R, C = recv.shape
    tr = _row_tile(R)

    def body(r_ref, o_ref):
        g = r_ref[0].astype(F32)
        for d in range(1, n):
            g = g + r_ref[d].astype(F32)
        o_ref[...] = g

    return pl.pallas_call(
        body, grid=(R // tr,), name=name,
        in_specs=[pl.BlockSpec((n, tr, C), lambda i: (0, i, 0))],
        out_specs=pl.BlockSpec((tr, C), lambda i: (i, 0)),
        out_shape=jax.ShapeDtypeStruct((R, C), F32),
        compiler_params=_params(("arbitrary",)),
    )(recv)


def adamw(name, recv, w, m, v, layer=None, prev=None):
    n, R, C = recv.shape
    tr = _row_tile(R)

    def body(r_ref, w_ref, m_ref, v_ref, *rest):
        g_ref, d_ref, nm_ref, nv_ref = rest[-4:]
        g = r_ref[0].astype(F32)
        for d in range(1, n):
            g = g + r_ref[d].astype(F32)
        mm = ADAM_B1 * m_ref[...] + (1.0 - ADAM_B1) * g
        vv = ADAM_B2 * v_ref[...] + (1.0 - ADAM_B2) * (g * g)
        m_hat = mm / (1.0 - ADAM_B1 ** ADAM_STEP)
        v_hat = vv / (1.0 - ADAM_B2 ** ADAM_STEP)
        g_ref[...] = g
        d_ref[...] = -ADAM_LR * (m_hat / (jnp.sqrt(v_hat) + ADAM_EPS) + ADAM_WD * w_ref[...])
        nm_ref[...] = mm
        nv_ref[...] = vv

    if layer is None:
        row = pl.BlockSpec((tr, C), lambda i: (i, 0))
        shape = (R, C)
    else:
        row = pl.BlockSpec((None, tr, C), lambda i: (layer, i, 0))
        shape = w.shape
    prev = [] if prev is None else list(prev)
    return pl.pallas_call(
        body, grid=(R // tr,), name=name,
        in_specs=[pl.BlockSpec((n, tr, C), lambda i: (0, i, 0)), row, row, row]
                 + [pl.BlockSpec(memory_space=pl.ANY)] * len(prev),
        out_specs=[row] * 4,
        out_shape=[jax.ShapeDtypeStruct(shape, F32)] * 4,
        input_output_aliases={4 + o: o for o in range(len(prev))},
        compiler_params=_params(("arbitrary",)),
    )(recv, w, m, v, *prev)


def _adamw_nd(name, recv, w, m, v):
    shp = w.shape
    C = shp[-1]
    flat = lambda a: a.reshape(-1, C)
    outs = adamw(name, recv.reshape(recv.shape[0], -1, C), flat(w), flat(m), flat(v))
    return [o.reshape(shp) for o in outs]


_SMALL_NAMES = ("loss", "mix_norm", "ffn_norm", "final_norm", "lb_logits", "hg_out_norm", "pool_scale")
_LANES = 128


def _pack_small(parts):
    rows, layout = [], {}
    at = 0
    for name in parts:
        flat = parts[name].reshape(-1).astype(F32)
        n_rows = -(-flat.shape[0] // (8 * _LANES)) * 8
        flat = jnp.pad(flat, (0, n_rows * _LANES - flat.shape[0]))
        rows.append(flat.reshape(n_rows, _LANES))
        layout[name] = (at, parts[name].shape)
        at += n_rows
    return jnp.concatenate(rows, axis=0), layout


def _unpack_small(pack, layout):
    out = {}
    for name, (at, shape) in layout.items():
        size = int(np.prod(shape))
        n_rows = -(-size // _LANES)
        out[name] = pack[at:at + n_rows].reshape(-1)[:size].reshape(shape)
    return out


def kernel(x, mix_norm, ffn_norm, final_norm, ab_w_in, lb_logits, hg_out_norm, ab_w_out, pool_w, pool_scale, ffn_w_gate, ffn_w_up, ffn_w_down, loss_target, m_mix_norm, m_ffn_norm, m_final_norm, m_ab_w_in, m_lb_logits, m_hg_out_norm, m_ab_w_out, m_pool_w, m_pool_scale, m_ffn_w_gate, m_ffn_w_up, m_ffn_w_down, v_mix_norm, v_ffn_norm, v_final_norm, v_ab_w_in, v_lb_logits, v_hg_out_norm, v_ab_w_out, v_pool_w, v_pool_scale, v_ffn_w_gate, v_ffn_w_up, v_ffn_w_down):
    D = x.shape[-1]
    n_layers = ffn_w_gate.shape[0]
    G = pool_w.shape[1]
    P = pool_w.shape[3]
    me = _my_index()

    in_handle, mix_norm_after = exchange_start("gather_w_in_start", [ab_w_in[0].astype(BF16)], True, mix_norm)
    rest = [ab_w_out[0], pool_w[0]]
    for l in range(n_layers):
        rest += [ffn_w_gate[l], ffn_w_up[l], ffn_w_down[l]]
    rest = [s.astype(BF16) for s in rest] + [pool_scale]
    rest_handle = []

    def get_w_in(after):
        w_in = exchange_wait("gather_w_in_wait", in_handle, after)[0]
        handle, w_in = exchange_start("gather_rest_start", rest, True, w_in)
        rest_handle.append(handle)
        return w_in

    def get_w_rest(after):
        got = exchange_wait("gather_rest_wait", rest_handle[0], after)
        w_out_g = got[0].reshape(D, D)
        pool_g = got[1].transpose(1, 0, 2, 3).reshape(G, P, P)
        wg = [got[2 + 3 * l] for l in range(n_layers)]
        wu = [got[3 + 3 * l] for l in range(n_layers)]
        wd = [got[4 + 3 * l] for l in range(n_layers)]
        return w_out_g, pool_g, got[-1].reshape(1, D), wg, wu, wd

    in_flight = []

    def send(tag, grads, carry):
        if "pool_w" in grads:
            grads = dict(grads, pool_w=grads["pool_w"].reshape(G, N_DEV, P // N_DEV, P).transpose(1, 0, 2, 3))
        if "ab_w_out" in grads:
            grads = dict(grads, ab_w_out=grads["ab_w_out"].reshape(N_DEV, D // N_DEV, D))
        handle, carry = exchange_start("grads_" + tag + "_start", list(grads.values()), False, carry)
        in_flight.append((tag, list(grads.keys()), handle))
        return carry

    dx0, small = local_step(x[0], loss_target[0], mix_norm_after, ffn_norm, final_norm[None],
                            lb_logits, hg_out_norm, get_w_in, get_w_rest, send)

    recv = {}
    for tag, names, handle in in_flight:
        recv.update(zip(names, exchange_wait("grads_" + tag + "_wait", handle, dx0)))
    small_pack, layout = _pack_small({k: small[k] for k in _SMALL_NAMES})
    (small_all,) = exchange("gather_small", [small_pack], gather=True)
    tot = _unpack_small(sum_slots("sum_small", small_all), layout)

    res = {}
    res["ab_w_in"] = _adamw_nd("adamw_w_in", recv["ab_w_in"], ab_w_in, m_ab_w_in, v_ab_w_in)
    res["ab_w_out"] = _adamw_nd("adamw_w_out", recv["ab_w_out"], ab_w_out, m_ab_w_out, v_ab_w_out)
    res["pool_w"] = _adamw_nd("adamw_pool_w", recv["pool_w"], pool_w, m_pool_w, v_pool_w)
    ffn_in = {"ffn_w_gate": (ffn_w_gate, m_ffn_w_gate, v_ffn_w_gate),
              "ffn_w_up": (ffn_w_up, m_ffn_w_up, v_ffn_w_up),
              "ffn_w_down": (ffn_w_down, m_ffn_w_down, v_ffn_w_down)}
    for name, (w, m, v) in ffn_in.items():
        flip = name != "ffn_w_down"
        if flip:
            w, m, v = (jnp.swapaxes(a, 1, 2) for a in (w, m, v))
        outs = None
        for l in range(n_layers):
            outs = adamw("adamw_" + name, recv[name + "_" + str(l)], w, m, v, layer=l, prev=outs)
        res[name] = [jnp.swapaxes(o, 1, 2) for o in outs] if flip else outs

    n_ps = pool_scale.shape[1]
    small_g = dict(tot)
    small_g["pool_scale"] = lax.dynamic_slice(tot["pool_scale"], (0, me * n_ps), (1, n_ps))
    small_w = dict(mix_norm=(mix_norm, m_mix_norm, v_mix_norm), ffn_norm=(ffn_norm, m_ffn_norm, v_ffn_norm),
                   final_norm=(final_norm, m_final_norm, v_final_norm),
                   lb_logits=(lb_logits, m_lb_logits, v_lb_logits),
                   hg_out_norm=(hg_out_norm, m_hg_out_norm, v_hg_out_norm),
                   pool_scale=(pool_scale, m_pool_scale, v_pool_scale))
    g_pack, lay2 = _pack_small({k: small_g[k].reshape(small_w[k][0].shape) for k in small_w})
    w_pack, _ = _pack_small({k: small_w[k][0] for k in small_w})
    m_pack, _ = _pack_small({k: small_w[k][1] for k in small_w})
    v_pack, _ = _pack_small({k: small_w[k][2] for k in small_w})
    small_out = [_unpack_small(o, lay2) for o in adamw("adamw_small", g_pack[None], w_pack, m_pack, v_pack)]
    for k in small_w:
        res[k] = [small_out[o][k] for o in range(4)]

    order = ("mix_norm", "ffn_norm", "final_norm", "ab_w_in", "lb_logits", "hg_out_norm", "ab_w_out", "pool_w",
             "pool_scale", "ffn_w_gate", "ffn_w_up", "ffn_w_down")
    outs = [tot["loss"].reshape(()), dx0[None]]
    for o in range(4):
        outs += [res[k][o] for k in order]
    return tuple(outs)
```

```python
import functools
import math

import numpy as np
import jax
import jax.numpy as jnp
from jax import lax
from jax.experimental import pallas as pl
from jax.experimental.pallas import tpu as pltpu

F32 = jnp.float32
BF16 = jnp.bfloat16

N_DEV = 8
RMS_EPS = 1e-6
HEAD = 128
HG_CHUNK = 64
HG_HEADS_PER_BLOCK = 4
POOL_WINDOWS = (2, 4, 8, 16)
POOL_HALO = 16
ADAM_LR, ADAM_B1, ADAM_B2, ADAM_EPS, ADAM_WD, ADAM_STEP = 0.001, 0.9, 0.999, 1e-08, 0.01, 10
VMEM_LIMIT_BYTES = 60 * 1024 * 1024
MESH = pl.DeviceIdType.MESH


def _params(sem):
    return pltpu.CompilerParams(dimension_semantics=sem, vmem_limit_bytes=VMEM_LIMIT_BYTES)


def _sigmoid(x):
    return 1.0 / (1.0 + jnp.exp(-x))


def rms_fwd(x, gain, out_dtype, ts=512):
    S, D = x.shape

    def body(x_ref, g_ref, h_ref, r_ref):
        xv = x_ref[...]
        r = lax.rsqrt(jnp.mean(xv * xv, axis=-1, keepdims=True) + RMS_EPS)
        h_ref[...] = ((xv * r) * g_ref[...]).astype(h_ref.dtype)
        r_ref[...] = r

    return pl.pallas_call(
        body, grid=(S // ts,), name="rms_fwd",
        in_specs=[pl.BlockSpec((ts, D), lambda i: (i, 0)), pl.BlockSpec((1, D), lambda i: (0, 0))],
        out_specs=[pl.BlockSpec((ts, D), lambda i: (i, 0)), pl.BlockSpec((ts, 1), lambda i: (i, 0))],
        out_shape=[jax.ShapeDtypeStruct((S, D), out_dtype), jax.ShapeDtypeStruct((S, 1), F32)],
        compiler_params=_params(("arbitrary",)),
    )(x, gain)


RMS_BWD_ROWS = 128


def _rms_bwd_tile(first, dh_of, x_ref, r_ref, g_ref, dres_ref, dx_ref, dxb_ref, dg_ref, rows):
    gv = g_ref[...]
    part = None
    for c in range(rows // RMS_BWD_ROWS):
        sl = slice(c * RMS_BWD_ROWS, (c + 1) * RMS_BWD_ROWS)
        rr = r_ref[sl, :]
        xh = x_ref[sl, :] * rr
        dhv = dh_of(sl)
        dxh = dhv * gv
        dx = dres_ref[sl, :] + rr * (dxh - xh * jnp.mean(dxh * xh, axis=-1, keepdims=True))
        dx_ref[sl, :] = dx
        dxb_ref[sl, :] = dx.astype(BF16)
        p = jnp.sum(dhv * xh, axis=0, keepdims=True)
        part = p if part is None else part + p

    @pl.when(first)
    def _():
        dg_ref[...] = part

    @pl.when(jnp.logical_not(first))
    def _():
        dg_ref[...] += part


def matmul_rms_bwd(name, a_ops, b_ops, *, grid, a_spec, b_spec, tm, x, r, gain, dres):
    S, D = x.shape
    n_pairs = len(a_ops)
    nk = grid[1]
    dn = (((1,), (1,)), ((), ()))

    def body(*refs):
        a_refs = refs[:n_pairs]
        b_refs = refs[n_pairs:2 * n_pairs]
        x_ref, r_ref, g_ref, dres_ref, dx_ref, dxb_ref, dg_ref, acc_ref = refs[2 * n_pairs:]
        i = pl.program_id(0)
        k = pl.program_id(1)

        @pl.when(k == 0)
        def _():
            acc_ref[...] = jnp.zeros_like(acc_ref)

        part = None
        for ar, br in zip(a_refs, b_refs):
            d = lax.dot_general(ar[...], br[...], dn, preferred_element_type=F32)
            part = d if part is None else part + d
        acc_ref[...] += part

        @pl.when(k == nk - 1)
        def _():
            _rms_bwd_tile(i == 0, lambda sl: acc_ref[sl, :], x_ref, r_ref, g_ref, dres_ref, dx_ref, dxb_ref,
                          dg_ref, tm)

    row = pl.BlockSpec((tm, D), lambda i, k: (i, 0))
    vec = pl.BlockSpec((1, D), lambda i, k: (0, 0))
    return pl.pallas_call(
        body, grid=grid, name=name,
        in_specs=[a_spec] * n_pairs + [b_spec] * n_pairs
                 + [row, pl.BlockSpec((tm, 1), lambda i, k: (i, 0)), vec, row],
        out_specs=[row, row, vec],
        out_shape=[jax.ShapeDtypeStruct((S, D), F32), jax.ShapeDtypeStruct((S, D), BF16),
                   jax.ShapeDtypeStruct((1, D), F32)],
        scratch_shapes=[pltpu.VMEM((tm, D), F32)],
        compiler_params=_params(("arbitrary", "arbitrary")),
    )(*a_ops, *b_ops, x, r, gain, dres)


def loss_and_final_bwd(x, gain, target, ts=512):
    S, D = x.shape

    def body(x_ref, g_ref, t_ref, loss_ref, dx_ref, dxb_ref, dg_ref):
        i = pl.program_id(0)
        xv = x_ref[...]
        rr = lax.rsqrt(jnp.mean(xv * xv, axis=-1, keepdims=True) + RMS_EPS)
        xh = xv * rr
        err = xh * g_ref[...] - t_ref[...]
        part_loss = 0.5 * jnp.sum(jnp.mean(err * err, axis=-1, keepdims=True))
        dy = err / D
        dxh = dy * g_ref[...]
        dx = rr * (dxh - xh * jnp.mean(dxh * xh, axis=-1, keepdims=True))
        dx_ref[...] = dx
        dxb_ref[...] = dx.astype(BF16)
        part = jnp.sum(dy * xh, axis=0, keepdims=True)

        @pl.when(i == 0)
        def _():
            dg_ref[...] = part
            loss_ref[...] = jnp.zeros_like(loss_ref) + part_loss

        @pl.when(i > 0)
        def _():
            dg_ref[...] += part
            loss_ref[...] += part_loss

    row = pl.BlockSpec((ts, D), lambda i: (i, 0))
    vec = pl.BlockSpec((1, D), lambda i: (0, 0))
    return pl.pallas_call(
        body, grid=(S // ts,), name="loss_final",
        in_specs=[row, vec, row],
        out_specs=[pl.BlockSpec((8, 128), lambda i: (0, 0)), row, row, vec],
        out_shape=[jax.ShapeDtypeStruct((8, 128), F32), jax.ShapeDtypeStruct((S, D), F32),
                   jax.ShapeDtypeStruct((S, D), BF16), jax.ShapeDtypeStruct((1, D), F32)],
        compiler_params=_params(("arbitrary",)),
    )(x, gain, target)


def matmul(name, a_ops, b_ops, *, grid, a_spec, b_spec, out_spec, out_shape, out_dtypes, acc_shape,
           trans_a=False, trans_b=False, res=None, res_spec=None, bf16_scale=None, bf16_scale_spec=None):
    n_pairs = len(a_ops)
    n_out = len(out_dtypes)
    nk = grid[-1]
    kaxis = len(grid) - 1
    dn = (((0,) if trans_a else (1,), (1,) if trans_b else (0,)), ((), ()))

    def body(*refs):
        a_refs = refs[:n_pairs]
        b_refs = refs[n_pairs:2 * n_pairs]
        pos = 2 * n_pairs
        res_ref = None
        if res is not None:
            res_ref = refs[pos]
            pos += 1
        scale_ref = None
        if bf16_scale is not None:
            scale_ref = refs[pos]
            pos += 1
        out_refs = refs[pos:pos + n_out]
        acc_ref = refs[pos + n_out]
        k = pl.program_id(kaxis)
        in_place = n_out == 1 and out_dtypes[0] == F32
        target = out_refs[0] if in_place else acc_ref

        def finish(val):
            if res_ref is not None:
                val = val + res_ref[...]
            for o in out_refs:
                if scale_ref is not None and o.dtype == BF16:
                    o[...] = (val * scale_ref[...]).astype(BF16)
                else:
                    o[...] = val.astype(o.dtype)

        if nk > 1:
            @pl.when(k == 0)
            def _():
                if in_place and res_ref is not None:
                    target[...] = res_ref[...]
                else:
                    target[...] = jnp.zeros_like(target)

        part = None
        for ar, br in zip(a_refs, b_refs):
            d = lax.dot_general(ar[...].astype(BF16), br[...].astype(BF16), dn, preferred_element_type=F32)
            part = d if part is None else part + d

        if nk == 1:
            finish(part)
        else:
            target[...] += part
            if not in_place:
                @pl.when(k == nk - 1)
                def _():
                    finish(acc_ref[...])

    in_specs = [a_spec] * n_pairs + [b_spec] * n_pairs
    operands = list(a_ops) + list(b_ops)
    if res is not None:
        in_specs.append(res_spec)
        operands.append(res)
    if bf16_scale is not None:
        in_specs.append(bf16_scale_spec)
        operands.append(bf16_scale)
    return pl.pallas_call(
        body, grid=grid, name=name, in_specs=in_specs,
        out_specs=[out_spec] * n_out,
        out_shape=[jax.ShapeDtypeStruct(out_shape, dt) for dt in out_dtypes],
        scratch_shapes=[pltpu.VMEM(acc_shape, F32)],
        compiler_params=_params(("arbitrary",) * len(grid)),
    )(*operands)


def ffn_gate_up(h, wg, wu, tm=1024):
    S, D = h.shape
    nb = wg.shape[2]

    def body(h_ref, wg_ref, wu_ref, p_ref, r_ref, a_ref):
        for c in range(2):
            rows = slice(c * (tm // 2), (c + 1) * (tm // 2))
            hv = h_ref[rows, :]
            g = jnp.dot(hv, wg_ref[...], preferred_element_type=F32)
            u = jnp.dot(hv, wu_ref[...], preferred_element_type=F32)
            s = _sigmoid(g)
            p = g * s
            p_ref[rows, :] = p
            r_ref[rows, :] = u * (s * (1.0 + g * (1.0 - s)))
            a_ref[rows, :] = (p * u).astype(BF16)

    wspec = pl.BlockSpec((None, D, nb), lambda j, i: (j, 0, 0))
    ospec = pl.BlockSpec((None, tm, nb), lambda j, i: (j, i, 0))
    return pl.pallas_call(
        body, grid=(N_DEV, S // tm), name="ffn_gate_up",
        in_specs=[pl.BlockSpec((tm, D), lambda j, i: (i, 0)), wspec, wspec],
        out_specs=[ospec, ospec, ospec],
        out_shape=[jax.ShapeDtypeStruct((N_DEV, S, nb), F32), jax.ShapeDtypeStruct((N_DEV, S, nb), F32),
                   jax.ShapeDtypeStruct((N_DEV, S, nb), BF16)],
        compiler_params=_params(("arbitrary", "arbitrary")),
    )(h, wg, wu)


def ffn_bwd_hidden(dy, wd, p, r, tm=1024):
    S, D = dy.shape
    nb = wd.shape[1]

    def body(dy_ref, wd_ref, p_ref, r_ref, dg_ref, du_ref):
        for c in range(2):
            rows = slice(c * (tm // 2), (c + 1) * (tm // 2))
            da = lax.dot_general(dy_ref[rows, :], wd_ref[...], (((1,), (1,)), ((), ())),
                                 preferred_element_type=F32)
            du_ref[rows, :] = (da * p_ref[rows, :]).astype(BF16)
            dg_ref[rows, :] = (da * r_ref[rows, :]).astype(BF16)

    hspec = pl.BlockSpec((None, tm, nb), lambda j, i: (j, i, 0))
    return pl.pallas_call(
        body, grid=(N_DEV, S // tm), name="ffn_bwd_hidden",
        in_specs=[pl.BlockSpec((tm, D), lambda j, i: (i, 0)), pl.BlockSpec((None, nb, D), lambda j, i: (j, 0, 0)),
                  hspec, hspec],
        out_specs=[hspec, hspec],
        out_shape=[jax.ShapeDtypeStruct((N_DEV, S, nb), BF16), jax.ShapeDtypeStruct((N_DEV, S, nb), BF16)],
        compiler_params=_params(("arbitrary", "arbitrary")),
    )(dy, wd, p, r)


def ffn_forward(h, xres, wg, wu, wd, tm=1024):
    S, D = h.shape
    nb = wg.shape[2]
    g, u, a = ffn_gate_up(h, wg, wu)
    (xo,) = matmul(
        "ffn_down", [a], [wd], grid=(S // tm, N_DEV),
        a_spec=pl.BlockSpec((None, tm, nb), lambda i, j: (j, i, 0)),
        b_spec=pl.BlockSpec((None, nb, D), lambda i, j: (j, 0, 0)),
        out_spec=pl.BlockSpec((tm, D), lambda i, j: (i, 0)), out_shape=(S, D), out_dtypes=[F32],
        acc_shape=(tm, D), res=xres, res_spec=pl.BlockSpec((tm, D), lambda i, j: (i, 0)))
    return xo, (g, u, a)


def ffn_backward(dy_b, h, saved, wg, wu, wd, x, r, gain, dres, tm=512, tk=2048):
    S, D = h.shape
    nb = wg.shape[2]
    g, u, a = saved
    dg, du = ffn_bwd_hidden(dy_b, wd, g, u)
    dx = matmul_rms_bwd(
        "ffn_dh", [dg, du], [wg, wu], grid=(S // tm, N_DEV),
        a_spec=pl.BlockSpec((None, tm, nb), lambda i, j: (j, i, 0)),
        b_spec=pl.BlockSpec((None, D, nb), lambda i, j: (j, 0, 0)),
        tm=tm, x=x, r=r, gain=gain, dres=dres)

    def wgrad_in(name, dhid):
        (dw,) = matmul(
            name, [dhid], [h], grid=(N_DEV, S // tk),
            a_spec=pl.BlockSpec((None, tk, nb), lambda j, k: (j, k, 0)),
            b_spec=pl.BlockSpec((tk, D), lambda j, k: (k, 0)),
            out_spec=pl.BlockSpec((None, nb, D), lambda j, k: (j, 0, 0)), out_shape=(N_DEV, nb, D),
            out_dtypes=[BF16], acc_shape=(nb, D), trans_a=True)
        return dw

    dwg = wgrad_in("ffn_dwg", dg)
    dwu = wgrad_in("ffn_dwu", du)
    (dwd,) = matmul(
        "ffn_dwd", [a], [dy_b], grid=(N_DEV, S // tk),
        a_spec=pl.BlockSpec((None, tk, nb), lambda j, k: (j, k, 0)),
        b_spec=pl.BlockSpec((tk, D), lambda j, k: (k, 0)),
        out_spec=pl.BlockSpec((None, nb, D), lambda j, k: (j, 0, 0)), out_shape=(N_DEV, nb, D),
        out_dtypes=[BF16], acc_shape=(nb, D), trans_a=True)
    return dx, dwg, dwu, dwd


def _pool_counts(row0, n, w):
    pos = row0 + lax.broadcasted_iota(jnp.int32, (n, 1), 0)
    return jnp.minimum(pos + 1, w).astype(F32)


def pool_forward(h, xres, w, scale, ts=256):
    S, D = h.shape
    G = len(POOL_WINDOWS)
    P = D // G
    hb = ts // POOL_HALO

    def body(h_ref, halo_ref, x_ref, w_ref, s_ref, xo_ref, p_ref):
        i = pl.program_id(0)
        for gi, win in enumerate(POOL_WINDOWS):
            cols = slice(gi * P, (gi + 1) * P)
            cur = h_ref[:, cols]
            halo = jnp.where(i > 0, halo_ref[:, cols], 0.0)
            acc = jnp.concatenate([halo, cur], axis=0)
            step = 1
            while step < win:
                acc = acc + pltpu.roll(acc, step, 0)
                step *= 2
            wsum = acc[POOL_HALO:, :]
            pooled = wsum / _pool_counts(i * ts, ts, win) - cur
            pb = pooled.astype(BF16)
            p_ref[:, cols] = pb
            mixed = jnp.dot(pb, w_ref[gi], preferred_element_type=F32)
            xo_ref[:, cols] = x_ref[:, cols] + mixed * s_ref[:, cols]

    row = pl.BlockSpec((ts, D), lambda i: (i, 0))
    return pl.pallas_call(
        body, grid=(S // ts,), name="pool_fwd",
        in_specs=[row, pl.BlockSpec((POOL_HALO, D), lambda i: (jnp.maximum(i * hb - 1, 0), 0)), row,
                  pl.BlockSpec((G, P, P), lambda i: (0, 0, 0)), pl.BlockSpec((1, D), lambda i: (0, 0))],
        out_specs=[row, row],
        out_shape=[jax.ShapeDtypeStruct((S, D), F32), jax.ShapeDtypeStruct((S, D), BF16)],
        compiler_params=_params(("arbitrary",)),
    )(h, h, xres, w, scale)


def pool_backward_mix(dx, pooled, w, scale, ts=256):
    S, D = dx.shape
    G = len(POOL_WINDOWS)
    P = D // G

    def body(dx_ref, p_ref, w_ref, s_ref, dm_ref, dp_ref, ds_ref):
        i = pl.program_id(0)
        parts = []
        for gi in range(G):
            cols = slice(gi * P, (gi + 1) * P)
            dxv = dx_ref[:, cols]
            dmb = (dxv * s_ref[:, cols]).astype(BF16)
            dm_ref[:, cols] = dmb
            dp_ref[:, cols] = lax.dot_general(dmb, w_ref[gi], (((1,), (1,)), ((), ())),
                                              preferred_element_type=F32)
            mixed = jnp.dot(p_ref[:, cols], w_ref[gi], preferred_element_type=F32)
            parts.append(jnp.sum(dxv * mixed, axis=0, keepdims=True))
        part = jnp.concatenate(parts, axis=1)

        @pl.when(i == 0)
        def _():
            ds_ref[...] = part

        @pl.when(i > 0)
        def _():
            ds_ref[...] += part

    row = pl.BlockSpec((ts, D), lambda i: (i, 0))
    vec = pl.BlockSpec((1, D), lambda i: (0, 0))
    return pl.pallas_call(
        body, grid=(S // ts,), name="pool_bwd_mix",
        in_specs=[row, row, pl.BlockSpec((G, P, P), lambda i: (0, 0, 0)), vec],
        out_specs=[row, row, vec],
        out_shape=[jax.ShapeDtypeStruct((S, D), BF16), jax.ShapeDtypeStruct((S, D), F32),
                   jax.ShapeDtypeStruct((1, D), F32)],
        compiler_params=_params(("arbitrary",)),
    )(dx, pooled, w, scale)


def pool_backward_window(dp, x, r, gain, dres, ts=256):
    S, D = dp.shape
    G = len(POOL_WINDOWS)
    P = D // G
    hb = ts // POOL_HALO
    n_i = S // ts
    n_rows = ts + POOL_HALO

    def body(dp_ref, halo_ref, x_ref, r_ref, g_ref, dres_ref, dx_ref, dxb_ref, dg_ref, dh_ref):
        i = pl.program_id(0)
        for gi, win in enumerate(POOL_WINDOWS):
            cols = slice(gi * P, (gi + 1) * P)
            cur = dp_ref[:, cols]
            halo = jnp.where(i < n_i - 1, halo_ref[:, cols], 0.0)
            acc = jnp.concatenate([cur / _pool_counts(i * ts, ts, win),
                                   halo / _pool_counts((i + 1) * ts, POOL_HALO, win)], axis=0)
            step = 1
            while step < win:
                acc = acc + pltpu.roll(acc, n_rows - step, 0)
                step *= 2
            dh_ref[:, cols] = acc[:ts, :] - cur
        _rms_bwd_tile(i == 0, lambda sl: dh_ref[sl, :], x_ref, r_ref, g_ref, dres_ref, dx_ref, dxb_ref, dg_ref, ts)

    row = pl.BlockSpec((ts, D), lambda i: (i, 0))
    vec = pl.BlockSpec((1, D), lambda i: (0, 0))
    return pl.pallas_call(
        body, grid=(n_i,), name="pool_bwd_window",
        in_specs=[row, pl.BlockSpec((POOL_HALO, D), lambda i: (jnp.minimum((i + 1) * hb, S // POOL_HALO - 1), 0)),
                  row, pl.BlockSpec((ts, 1), lambda i: (i, 0)), vec, row],
        out_specs=[row, row, vec],
        out_shape=[jax.ShapeDtypeStruct((S, D), F32), jax.ShapeDtypeStruct((S, D), BF16),
                   jax.ShapeDtypeStruct((1, D), F32)],
        scratch_shapes=[pltpu.VMEM((ts, D), F32)],
        compiler_params=_params(("arbitrary",)),
    )(dp, dp, x, r, gain, dres)


_HG_LEVELS = (32, 16, 8, 4, 2, 1)
_N_LEV = len(_HG_LEVELS) + 1


def _hgrn_constants():
    C = HG_CHUNK
    t = np.arange(C)
    tri = (t[None, :] <= t[:, None]).astype(np.float32)
    blocks = [tri]
    masks, upq, upk = [], [], []
    for m in _HG_LEVELS:
        p = (t // (2 * m)) * 2 * m + m - 1
        blocks.append(tri[p])
        masks.append(((t[:, None] // (2 * m)) == (t[None, :] // (2 * m))).astype(np.float32))
        upper = (t % (2 * m)) >= m
        upq.append(np.repeat(upper[:, None], HEAD, 1).astype(np.float32))
        upk.append(np.repeat(~upper[:, None], HEAD, 1).astype(np.float32))
    blocks.append(tri)
    masks.append(np.eye(C, dtype=np.float32))
    upq.append(np.ones((C, HEAD), np.float32))
    upk.append(np.ones((C, HEAD), np.float32))
    mstack = np.concatenate(blocks, axis=0)
    mstack3 = np.concatenate([mstack] * 3, axis=1)
    trirev3 = np.concatenate([tri.T] * 3, axis=1)
    return (jnp.asarray(mstack3, BF16), jnp.asarray(np.stack(masks)), jnp.asarray(np.stack(upq)),
            jnp.asarray(np.stack(upk)), jnp.asarray(trirev3, BF16))


def _split3(x):
    hi = x.astype(BF16)
    r1 = x - hi.astype(F32)
    mid = r1.astype(BF16)
    lo = (r1 - mid.astype(F32)).astype(BF16)
    return jnp.concatenate([hi, mid, lo], axis=0)


def _hgrn_chunk_common(qa, fa, lbv, mstack3, upq, upk):
    sq = _sigmoid(qa)
    q = qa * sq
    sf = _sigmoid(fa)
    f = lbv + (1.0 - lbv) * sf
    g = jnp.log(f)
    k = 1.0 - f
    gall = jnp.dot(mstack3, _split3(g), preferred_element_type=F32).reshape(_N_LEV + 1, HG_CHUNK, HEAD)
    G = gall[0]
    eq_exp = G[None] - gall[1:]
    eq = jnp.exp(jnp.minimum(eq_exp, 0.0)) * upq
    ek = jnp.exp(jnp.minimum(-eq_exp, 0.0)) * upk
    Qs = (q[None] * eq).astype(BF16)
    Ks = (k[None] * ek).astype(BF16)
    return sq, q, sf, f, k, G, eq, ek, Qs, Ks


def hgrn_forward(proj, lb, hg_norm, ts=512):
    S = proj.shape[0]
    nh = lb.shape[1] // HEAD
    C = HG_CHUNK
    ncs = ts // C
    mstack3, masks, upq, upk, _ = _hgrn_constants()

    def body(qa_ref, fa_ref, ia_ref, ga_ref, lb_ref, gn_ref, ms_ref, mk_ref, uq_ref, uk_ref,
             oa_ref, oraw_ref, st_ref, state):
        tt = pl.program_id(1)

        @pl.when(tt == 0)
        def _():
            state[...] = jnp.zeros_like(state)

        gn = gn_ref[...]

        def chunk(c, carry):
            sl = pl.ds(pl.multiple_of(c * C, C), C)
            for hh in range(HG_HEADS_PER_BLOCK):
                cols = slice(hh * HEAD, (hh + 1) * HEAD)
                qa, fa, v, ga = qa_ref[sl, cols], fa_ref[sl, cols], ia_ref[sl, cols], ga_ref[sl, cols]
                _, q, _, _, k, G, _, _, Qs, Ks = _hgrn_chunk_common(qa, fa, lb_ref[:, cols], ms_ref[...],
                                                                    uq_ref[...], uk_ref[...])
                att7 = lax.dot_general(Qs, Ks, (((2,), (2,)), ((0,), (0,))), preferred_element_type=F32)
                att = jnp.sum(att7 * mk_ref[...], axis=0)
                st = state[hh]
                st_ref[hh, c] = st
                vb = v.astype(BF16)
                qg = (q * jnp.exp(G)).astype(BF16)
                o = jnp.dot(att.astype(BF16), vb, preferred_element_type=F32)
                o = o + lax.dot_general(qg, st.astype(BF16), (((1,), (1,)), ((), ())),
                                        preferred_element_type=F32)
                g_last = G[C - 1:C, :]
                kh = (k * jnp.exp(g_last - G)).astype(BF16)
                state[hh] = st * jnp.exp(g_last) + lax.dot_general(vb, kh, (((0,), (0,)), ((), ())),
                                                                   preferred_element_type=F32)
                oraw_ref[sl, cols] = o
                r = lax.rsqrt(jnp.mean(o * o, axis=-1, keepdims=True) + RMS_EPS)
                oa_ref[sl, cols] = (((o * r) * gn) * (ga * _sigmoid(ga))).astype(BF16)
            return carry

        lax.fori_loop(0, ncs, chunk, 0)

    hpb = HG_HEADS_PER_BLOCK
    wide = hpb * HEAD

    def col(m0):
        return pl.BlockSpec((ts, wide), lambda h, t: (t, m0 // hpb + h))

    const3 = lambda shape: pl.BlockSpec(shape, lambda h, t: (0, 0, 0))
    return pl.pallas_call(
        body, grid=(nh // hpb, S // ts), name="hgrn_fwd",
        in_specs=[col(0), col(nh), col(2 * nh), col(3 * nh),
                  pl.BlockSpec((1, wide), lambda h, t: (0, h)), pl.BlockSpec((1, HEAD), lambda h, t: (0, 0)),
                  pl.BlockSpec(mstack3.shape, lambda h, t: (0, 0)), const3(masks.shape), const3(upq.shape),
                  const3(upk.shape)],
        out_specs=[pl.BlockSpec((ts, wide), lambda h, t: (t, h)), pl.BlockSpec((ts, wide), lambda h, t: (t, h)),
                   pl.BlockSpec((hpb, ncs, HEAD, HEAD), lambda h, t: (h, t, 0, 0))],
        out_shape=[jax.ShapeDtypeStruct((S, nh * HEAD), BF16), jax.ShapeDtypeStruct((S, nh * HEAD), F32),
                   jax.ShapeDtypeStruct((nh, S // C, HEAD, HEAD), F32)],
        scratch_shapes=[pltpu.VMEM((hpb, HEAD, HEAD), F32)],
        compiler_params=_params(("arbitrary", "arbitrary")),
    )(proj, proj, proj, proj, lb, hg_norm, mstack3, masks, upq, upk)


def hgrn_backward(dcat, proj, oraw, states, lb, hg_norm, ts=512):
    S = proj.shape[0]
    nh = lb.shape[1] // HEAD
    C = HG_CHUNK
    ncs = ts // C
    nt = S // ts
    mstack3, masks, upq, upk, trirev3 = _hgrn_constants()

    def body(do_ref, qa_ref, fa_ref, ia_ref, ga_ref, or_ref, st_ref, lb_ref, gn_ref, ms_ref, mk_ref, uq_ref,
             uk_ref, tr_ref, dqa_ref, dfa_ref, dia_ref, dga_ref, dlb_ref, dgn_ref, dstate):
        tt = pl.program_id(1)

        @pl.when(tt == 0)
        def _():
            dstate[...] = jnp.zeros_like(dstate)
            dlb_ref[...] = jnp.zeros_like(dlb_ref)
            dgn_ref[...] = jnp.zeros_like(dgn_ref)

        gn = gn_ref[...]

        def chunk(cc, carry):
            c = ncs - 1 - cc
            sl = pl.ds(pl.multiple_of(c * C, C), C)
            for hh in range(HG_HEADS_PER_BLOCK):
                cols = slice(hh * HEAD, (hh + 1) * HEAD)
                lbv = lb_ref[:, cols]
                qa, fa, v, ga = qa_ref[sl, cols], fa_ref[sl, cols], ia_ref[sl, cols], ga_ref[sl, cols]
                sq, q, sf, f, k, G, eq, ek, Qs, Ks = _hgrn_chunk_common(qa, fa, lbv, ms_ref[...], uq_ref[...],
                                                                        uk_ref[...])
                mk = mk_ref[...]
                att7 = lax.dot_general(Qs, Ks, (((2,), (2,)), ((0,), (0,))), preferred_element_type=F32)
                att = jnp.sum(att7 * mk, axis=0)
                o = or_ref[sl, cols]
                dO = do_ref[sl, cols]
                sg = _sigmoid(ga)
                r = lax.rsqrt(jnp.mean(o * o, axis=-1, keepdims=True) + RMS_EPS)
                xh = o * r
                dga_ref[sl, cols] = (dO * (xh * gn) * (sg * (1.0 + ga * (1.0 - sg)))).astype(BF16)
                don = dO * (ga * sg)
                dgn_ref[hh] += jnp.sum(don * xh, axis=0, keepdims=True)
                dxh = don * gn
                do = r * (dxh - xh * jnp.mean(dxh * xh, axis=-1, keepdims=True))
                dob = do.astype(BF16)
                st = st_ref[hh, c]
                dst = dstate[hh]
                dstb = dst.astype(BF16)
                vb = v.astype(BF16)
                eG = jnp.exp(G)
                g_last = G[C - 1:C, :]
                e_last = jnp.exp(g_last)
                e_tail = jnp.exp(g_last - G)
                qg = (q * eG).astype(BF16)
                kh = (k * e_tail).astype(BF16)
                dq_inter = jnp.dot(dob, st.astype(BF16), preferred_element_type=F32) * eG
                dk_inter = jnp.dot(vb, dstb, preferred_element_type=F32) * e_tail
                dv = lax.dot_general(kh, dstb, (((1,), (1,)), ((), ())), preferred_element_type=F32)
                dv = dv + lax.dot_general(att.astype(BF16), dob, (((0,), (0,)), ((), ())),
                                          preferred_element_type=F32)
                dA = lax.dot_general(dob, vb, (((1,), (1,)), ((), ())), preferred_element_type=F32)
                dA7 = (dA[None] * mk).astype(BF16)
                dAT7 = (dA.T[None] * mk).astype(BF16)
                dQs = lax.dot_general(dA7, Ks, (((2,), (1,)), ((0,), (0,))), preferred_element_type=F32)
                dKs = lax.dot_general(dAT7, Qs, (((2,), (1,)), ((0,), (0,))), preferred_element_type=F32)
                dq = dq_inter + jnp.sum(dQs * eq, axis=0)
                dk = dk_inter + jnp.sum(dKs * ek, axis=0)
                dG = (jnp.sum(Qs.astype(F32) * dQs - Ks.astype(F32) * dKs, axis=0)
                      + q * dq_inter - k * dk_inter)
                last_extra = (jnp.sum(k * dk_inter, axis=0, keepdims=True)
                              + e_last * jnp.sum(dst * st, axis=0, keepdims=True))
                is_last = lax.broadcasted_iota(jnp.int32, (C, 1), 0) == C - 1
                dG = dG + jnp.where(is_last, last_extra, 0.0)
                dg = jnp.dot(tr_ref[...], _split3(dG), preferred_element_type=F32)
                df = dg / f - dk
                dfa_ref[sl, cols] = (df * (1.0 - lbv) * (sf * (1.0 - sf))).astype(BF16)
                dlb_ref[:, cols] += jnp.sum(df * (1.0 - sf), axis=0, keepdims=True)
                dqa_ref[sl, cols] = (dq * (sq * (1.0 + qa * (1.0 - sq)))).astype(BF16)
                dia_ref[sl, cols] = dv.astype(BF16)
                dstate[hh] = dst * e_last + lax.dot_general(dob, qg, (((0,), (0,)), ((), ())),
                                                            preferred_element_type=F32)
            return carry

        lax.fori_loop(0, ncs, chunk, 0)

    hpb = HG_HEADS_PER_BLOCK
    wide = hpb * HEAD

    def col(m0):
        return pl.BlockSpec((ts, wide), lambda h, t: (nt - 1 - t, m0 // hpb + h))

    const3 = lambda shape: pl.BlockSpec(shape, lambda h, t: (0, 0, 0))
    const2 = lambda shape: pl.BlockSpec(shape, lambda h, t: (0, 0))
    ocol = pl.BlockSpec((ts, wide), lambda h, t: (nt - 1 - t, h))
    half = nh * HEAD
    return pl.pallas_call(
        body, grid=(nh // hpb, nt), name="hgrn_bwd",
        in_specs=[col(0), col(0), col(nh), col(2 * nh), col(3 * nh), col(0),
                  pl.BlockSpec((hpb, ncs, HEAD, HEAD), lambda h, t: (h, nt - 1 - t, 0, 0)),
                  pl.BlockSpec((1, wide), lambda h, t: (0, h)), const2((1, HEAD)),
                  const2(mstack3.shape), const3(masks.shape), const3(upq.shape), const3(upk.shape),
                  const2(trirev3.shape)],
        out_specs=[ocol, ocol, ocol, ocol, pl.BlockSpec((1, wide), lambda h, t: (0, h)),
                   pl.BlockSpec((hpb, 1, HEAD), lambda h, t: (h, 0, 0))],
        out_shape=[jax.ShapeDtypeStruct((S, half), BF16)] * 4
                  + [jax.ShapeDtypeStruct((1, half), F32), jax.ShapeDtypeStruct((nh, 1, HEAD), F32)],
        scratch_shapes=[pltpu.VMEM((hpb, HEAD, HEAD), F32)],
        compiler_params=_params(("arbitrary", "arbitrary")),
    )(dcat, proj, proj, proj, proj, oraw, states, lb, hg_norm, mstack3, masks, upq, upk, trirev3)


SB_SUB = 128
LOG2_E = 1.4426950408889634
SB_SCALE = 1.0 / math.sqrt(HEAD)
SB_QUERY_SCALE = SB_SCALE * LOG2_E


def _split2(x):
    hi = x.astype(BF16)
    lo = (x - hi.astype(F32)).astype(BF16)
    return jnp.concatenate([hi, lo], axis=1)


def _sb_constants():
    j = np.arange(SB_SUB)
    after = (j[:, None] > j[None, :]).astype(np.float32)
    before = (j[:, None] < j[None, :]).astype(np.float32)
    return (jnp.asarray(np.concatenate([after, after], axis=0), BF16),
            jnp.asarray(np.concatenate([before, before], axis=0), BF16))


def _sb_tri(i):
    return (i * (i + 1)) // 2


def _sb_diag_mask(t):
    return lax.broadcasted_iota(jnp.int32, (t, t), 1) < lax.broadcasted_iota(jnp.int32, (t, t), 0)


def _sb_scores(q, k_ref, col0, t):
    ks = k_ref[pl.ds(pl.multiple_of(col0, t), t), :]
    return lax.dot_general(q, ks, (((1,), (1,)), ((), ())), preferred_element_type=F32)


def _sb_weights(z, mask, run, after2):
    nsub = z.shape[1] // SB_SUB
    nz = -z
    lk = jnp.minimum(nz, 0.0) - jnp.log(1.0 + jnp.exp2(jnp.minimum(z, nz))) * LOG2_E
    if mask is not None:
        lk = jnp.where(mask, lk, 0.0)
    locs, tots = [], []
    for b in range(nsub):
        lkb = lk[:, b * SB_SUB:(b + 1) * SB_SUB]
        loc = jnp.dot(_split2(lkb), after2, preferred_element_type=F32)
        locs.append(loc)
        tots.append(loc[:, 0:1] + lkb[:, 0:1])
    ws = [None] * nsub
    for b in reversed(range(nsub)):
        sl = slice(b * SB_SUB, (b + 1) * SB_SUB)
        ws[b] = jnp.exp2(z[:, sl] + lk[:, sl] + (locs[b] + run))
        run = run + tots[b]
    w = jnp.concatenate(ws, axis=1)
    if mask is not None:
        w = jnp.where(mask, w, 0.0)
    return w, run


def sb_forward(projb, nh, m0, t=512):
    S = projb.shape[0]
    after2, _ = _sb_constants()
    n_i = S // t

    def body(q_ref, k_ref, v_ref, af_ref, o_ref, w_hbm, wbuf, wsem):
        h = pl.program_id(0)
        i = pl.program_id(1)
        q = q_ref[...]
        after = af_ref[...]
        base = _sb_tri(i)

        def store(slot, jb):
            return pltpu.make_async_copy(wbuf.at[slot], w_hbm.at[h, base + jb], wsem.at[slot])

        def block(n, jb, run, mask):
            slot = n % 2

            @pl.when(n >= 2)
            def _():
                store(slot, jb).wait()

            z = _sb_scores(q, k_ref, jb * t, t)
            w, run = _sb_weights(z, mask, run, after)
            wb = w.astype(BF16)
            wbuf[slot] = wb
            store(slot, jb).start()
            vs = v_ref[pl.ds(pl.multiple_of(jb * t, t), t), :]
            return run, jnp.dot(wb, vs, preferred_element_type=F32)

        run, acc = block(0, i, jnp.zeros((t, 1), F32), _sb_diag_mask(t))

        def step(n, carry):
            run, acc = carry
            run, part = block(n + 1, i - 1 - n, run, None)
            return run, acc + part

        _, acc = lax.fori_loop(0, i, step, (run, acc))
        o_ref[...] = acc.astype(BF16)
        store(i % 2, 0).wait()

        @pl.when(i >= 1)
        def _():
            store((i + 1) % 2, 0).wait()

    return pl.pallas_call(
        body, grid=(nh, n_i), name="sb_fwd",
        in_specs=[pl.BlockSpec((t, HEAD), lambda h, i: (i, m0 + h)),
                  pl.BlockSpec((S, HEAD), lambda h, i: (0, m0 + nh + h)),
                  pl.BlockSpec((S, HEAD), lambda h, i: (0, m0 + 2 * nh + h)),
                  pl.BlockSpec(after2.shape, lambda h, i: (0, 0))],
        out_specs=[pl.BlockSpec((t, HEAD), lambda h, i: (i, h)), pl.BlockSpec(memory_space=pl.ANY)],
        out_shape=[jax.ShapeDtypeStruct((S, nh * HEAD), BF16),
                   jax.ShapeDtypeStruct((nh, _sb_tri(n_i), t, t), BF16)],
        scratch_shapes=[pltpu.VMEM((2, t, t), BF16), pltpu.SemaphoreType.DMA((2,))],
        compiler_params=_params(("arbitrary", "arbitrary")),
    )(projb, projb, projb, after2)


def sb_backward(dcat, projb, w_all, nh, m0, t=512):
    S = projb.shape[0]
    _, before2 = _sb_constants()
    n_i = S // t
    nsub = t // SB_SUB

    def body(do_ref, q_ref, k_ref, v_ref, bf_ref, w_hbm, dq_ref, dk_ref, dv_ref, dk_acc, dv_acc, wbuf, wsem):
        h = pl.program_id(0)
        i = pl.program_id(1)

        @pl.when(i == 0)
        def _():
            dk_acc[...] = jnp.zeros_like(dk_acc)
            dv_acc[...] = jnp.zeros_like(dv_acc)

        q = q_ref[...]
        dob = do_ref[...].astype(BF16)
        before = bf_ref[...]
        base = _sb_tri(i)

        def load(slot, jb):
            return pltpu.make_async_copy(w_hbm.at[h, base + jb], wbuf.at[slot], wsem.at[slot])

        load(0, 0).start()

        def left_to_right(jb, run, dq, mask):
            slot = jb % 2
            load(slot, jb).wait()

            @pl.when(jb < i)
            def _():
                load(1 - slot, jb + 1).start()

            ksl = pl.ds(pl.multiple_of(jb * t, t), t)
            wb = wbuf[slot]
            z = _sb_scores(q, k_ref, jb * t, t)
            dw = lax.dot_general(dob, v_ref[ksl, :], (((1,), (1,)), ((), ())), preferred_element_type=F32)
            d = dw * wb.astype(F32)
            dv_acc[ksl, :] += lax.dot_general(wb, dob, (((0,), (0,)), ((), ())), preferred_element_type=F32)
            sig = 1.0 / (1.0 + jnp.exp2(-z))
            das = []
            for b in range(nsub):
                db = d[:, b * SB_SUB:(b + 1) * SB_SUB]
                prefix = run + jnp.dot(_split2(db), before, preferred_element_type=F32)
                das.append(db - sig[:, b * SB_SUB:(b + 1) * SB_SUB] * (db + prefix))
                run = prefix[:, SB_SUB - 1:SB_SUB] + db[:, SB_SUB - 1:SB_SUB]
            da = jnp.concatenate(das, axis=1)
            if mask is not None:
                da = jnp.where(mask, da, 0.0)
            dab = (da * SB_SCALE).astype(BF16)
            dq = dq + jnp.dot(dab, k_ref[ksl, :], preferred_element_type=F32)
            dk_acc[ksl, :] += lax.dot_general(dab, q, (((0,), (0,)), ((), ())), preferred_element_type=F32)
            return run, dq

        run, dq = lax.fori_loop(0, i, lambda jb, c: left_to_right(jb, c[0], c[1], None),
                                (jnp.zeros((t, 1), F32), jnp.zeros((t, HEAD), F32)))
        _, dq = left_to_right(i, run, dq, _sb_diag_mask(t))
        dq_ref[...] = dq.astype(BF16)

        @pl.when(i == n_i - 1)
        def _():
            dk_ref[...] = (dk_acc[...] * (1.0 / SB_QUERY_SCALE)).astype(BF16)
            dv_ref[...] = dv_acc[...].astype(BF16)

    half = nh * HEAD
    full = pl.BlockSpec((S, HEAD), lambda h, i: (0, h))
    return pl.pallas_call(
        body, grid=(nh, n_i), name="sb_bwd",
        in_specs=[pl.BlockSpec((t, HEAD), lambda h, i: (i, nh + h)),
                  pl.BlockSpec((t, HEAD), lambda h, i: (i, m0 + h)),
                  pl.BlockSpec((S, HEAD), lambda h, i: (0, m0 + nh + h)),
                  pl.BlockSpec((S, HEAD), lambda h, i: (0, m0 + 2 * nh + h)),
                  pl.BlockSpec(before2.shape, lambda h, i: (0, 0)), pl.BlockSpec(memory_space=pl.ANY)],
        out_specs=[pl.BlockSpec((t, HEAD), lambda h, i: (i, h)), full, full],
        out_shape=[jax.ShapeDtypeStruct((S, half), BF16)] * 3,
        scratch_shapes=[pltpu.VMEM((S, HEAD), F32), pltpu.VMEM((S, HEAD), F32),
                        pltpu.VMEM((2, t, t), BF16), pltpu.SemaphoreType.DMA((2,))],
        compiler_params=_params(("arbitrary", "arbitrary")),
    )(dcat, projb, projb, projb, before2, w_all)


def local_step(x, target, mix_norm, ffn_norm, final_norm, lb_logits, hg_norm, get_w_in, get_w_rest, send):
    S, D = x.shape
    half = D // 2
    nh = half // HEAD
    tm = 512
    tk = 2048
    row = lambda i, j: (i, 0)

    lb = jax.nn.softmax(lb_logits, axis=0)[0:1]

    h0, r0 = rms_fwd(x, mix_norm[0:1], BF16)
    w_in = get_w_in(h0)
    nbi = w_in.shape[2]
    col = jnp.arange(N_DEV * nbi) // half
    col_scale = jnp.where(col == 4, SB_QUERY_SCALE, 1.0).astype(F32)[None]
    proj, projb = matmul(
        "proj_in", [h0], [w_in], grid=(N_DEV, S // tm, 1),
        a_spec=pl.BlockSpec((tm, D), lambda j, i, k: (i, 0)),
        b_spec=pl.BlockSpec((None, D, nbi), lambda j, i, k: (j, 0, 0)),
        out_spec=pl.BlockSpec((tm, nbi), lambda j, i, k: (i, j)), out_shape=(S, N_DEV * nbi),
        out_dtypes=[F32, BF16], acc_shape=(8, 128),
        bf16_scale=col_scale, bf16_scale_spec=pl.BlockSpec((1, nbi), lambda j, i, k: (0, j)))
    oa, oraw, states = hgrn_forward(proj, lb, hg_norm)
    ob, sb_weights = sb_forward(projb, nh, 4 * nh)
    cat = jnp.concatenate([oa, ob], axis=1)
    w_out, pool_w, pool_scale, wg, wu, wd = get_w_rest(cat)
    (x1,) = matmul(
        "mix_out", [cat], [w_out], grid=(S // tm, 1),
        a_spec=pl.BlockSpec((tm, D), row), b_spec=pl.BlockSpec((D, D), lambda i, k: (0, 0)),
        out_spec=pl.BlockSpec((tm, D), row), out_shape=(S, D), out_dtypes=[F32], acc_shape=(8, 128),
        res=x, res_spec=pl.BlockSpec((tm, D), row))
    h1, r1 = rms_fwd(x1, ffn_norm[0:1], BF16)
    x2, ffn0 = ffn_forward(h1, x1, wg[0], wu[0], wd[0])

    h2, r2 = rms_fwd(x2, mix_norm[1:2], F32)
    x3, pooled = pool_forward(h2, x2, pool_w, pool_scale)
    h3, r3 = rms_fwd(x3, ffn_norm[1:2], BF16)
    x4, ffn1 = ffn_forward(h3, x3, wg[1], wu[1], wd[1])

    loss_blk, dx4, dx4b, d_final = loss_and_final_bwd(x4, final_norm, target)

    (dx3, _, d_ffn1), dwg1, dwu1, dwd1 = ffn_backward(dx4b, h3, ffn1, wg[1], wu[1], wd[1],
                                                      x3, r3, ffn_norm[1:2], dx4)
    dx3 = send("ffn1", dict(ffn_w_gate_1=dwg1, ffn_w_up_1=dwu1, ffn_w_down_1=dwd1), dx3)
    dmixed, dpooled, d_pscale = pool_backward_mix(dx3, pooled, pool_w, pool_scale)
    G = len(POOL_WINDOWS)
    P = D // G
    (d_pool_w,) = matmul(
        "pool_dw", [pooled], [dmixed], grid=(G, S // tk),
        a_spec=pl.BlockSpec((tk, P), lambda g, k: (k, g)), b_spec=pl.BlockSpec((tk, P), lambda g, k: (k, g)),
        out_spec=pl.BlockSpec((None, P, P), lambda g, k: (g, 0, 0)), out_shape=(G, P, P), out_dtypes=[BF16],
        acc_shape=(P, P), trans_a=True)
    dx2, dx2b, d_mix1 = pool_backward_window(dpooled, x2, r2, mix_norm[1:2], dx3)

    (dx1, dx1b, d_ffn0), dwg0, dwu0, dwd0 = ffn_backward(dx2b, h1, ffn0, wg[0], wu[0], wd[0],
                                                         x1, r1, ffn_norm[0:1], dx2)
    (dcat,) = matmul(
        "mix_out_dx", [dx1b], [w_out], grid=(S // tm, 1),
        a_spec=pl.BlockSpec((tm, D), row), b_spec=pl.BlockSpec((D, D), lambda i, k: (0, 0)),
        out_spec=pl.BlockSpec((tm, D), row), out_shape=(S, D), out_dtypes=[F32], acc_shape=(8, 128),
        trans_b=True)
    (d_w_out,) = matmul(
        "mix_out_dw", [cat], [dx1b], grid=(2, S // tk),
        a_spec=pl.BlockSpec((tk, half), lambda m, k: (k, m)), b_spec=pl.BlockSpec((tk, D), lambda m, k: (k, 0)),
        out_spec=pl.BlockSpec((half, D), lambda m, k: (m, 0)), out_shape=(D, D), out_dtypes=[BF16],
        acc_shape=(half, D), trans_a=True)
    dcat = send("layer0", dict(ffn_w_gate_0=dwg0, ffn_w_up_0=dwu0, ffn_w_down_0=dwd0, pool_w=d_pool_w,
                               ab_w_out=d_w_out), dcat)
    dqa, dfa, dia, dga, d_lb, d_hg = hgrn_backward(dcat, proj, oraw, states, lb, hg_norm)
    dqb, dkb, dvb = sb_backward(dcat, projb, sb_weights, nh, 4 * nh)
    dproj = jnp.concatenate([dqa, dfa, dia, dga, dqb, dkb, dvb], axis=1)
    (d_w_in,) = matmul(
        "proj_in_dw", [h0], [dproj], grid=(N_DEV, S // tk),
        a_spec=pl.BlockSpec((tk, D), lambda j, k: (k, 0)), b_spec=pl.BlockSpec((tk, nbi), lambda j, k: (k, j)),
        out_spec=pl.BlockSpec((None, D, nbi), lambda j, k: (j, 0, 0)), out_shape=(N_DEV, D, nbi),
        out_dtypes=[BF16], acc_shape=(D, nbi), trans_a=True)
    dproj = send("w_in", dict(ab_w_in=d_w_in), dproj)
    dx0, _, d_mix0 = matmul_rms_bwd(
        "proj_in_dx", [dproj], [w_in], grid=(S // tm, N_DEV),
        a_spec=pl.BlockSpec((tm, nbi), lambda i, j: (i, j)),
        b_spec=pl.BlockSpec((None, D, nbi), lambda i, j: (j, 0, 0)),
        tm=tm, x=x, r=r0, gain=mix_norm[0:1], dres=dx1)

    d_l0 = d_lb * lb * (1.0 - lb)
    small = dict(
        loss=loss_blk[0:1, 0:1],
        mix_norm=jnp.concatenate([d_mix0, d_mix1], axis=0),
        ffn_norm=jnp.concatenate([d_ffn0, d_ffn1], axis=0),
        final_norm=d_final,
        lb_logits=jnp.concatenate([d_l0, -d_l0], axis=0),
        hg_out_norm=jnp.sum(d_hg, axis=0),
        pool_scale=d_pscale,
    )
    return dx0, small


def _my_index():
    return 4 * lax.axis_index("x") + 2 * lax.axis_index("y") + lax.axis_index("c")


def _peer(r):
    x, y, c = lax.axis_index("x"), lax.axis_index("y"), lax.axis_index("c")
    px = 1 - x if (r >> 2) & 1 else x
    py = 1 - y if (r >> 1) & 1 else y
    pc = 1 - c if r & 1 else c
    return (px, py, pc), 4 * px + 2 * py + pc


def exchange(name, arrays, gather):
    n = len(arrays)
    n_peers = N_DEV - 1

    def body(*refs):
        ins, outs = refs[:n], refs[n:2 * n]
        send_sems, recv_sems, local_sems = refs[2 * n:]
        me = _my_index()
        local = []
        for a in range(n):
            src = ins[a] if gather else ins[a].at[me]
            cp = pltpu.make_async_copy(src, outs[a].at[me], local_sems.at[a])
            cp.start()
            local.append(cp)
        remote = []
        for a in range(n):
            for r in range(1, N_DEV):
                peer, pidx = _peer(r)
                src = ins[a] if gather else ins[a].at[pidx]
                cp = pltpu.make_async_remote_copy(
                    src_ref=src, dst_ref=outs[a].at[me], send_sem=send_sems.at[a * n_peers + r - 1],
                    recv_sem=recv_sems.at[a * n_peers + r - 1], device_id=peer, device_id_type=MESH)
                cp.start()
                remote.append((cp, a, r))
        for cp, a, r in remote:
            _, pidx = _peer(r)
            src = ins[a] if gather else ins[a].at[pidx]
            pltpu.make_async_remote_copy(
                src_ref=src, dst_ref=outs[a].at[pidx], send_sem=send_sems.at[a * n_peers + r - 1],
                recv_sem=recv_sems.at[a * n_peers + r - 1], device_id=_peer(r)[0], device_id_type=MESH).wait_recv()
        for cp, a, r in remote:
            cp.wait_send()
        for cp in local:
            cp.wait()

    out_shape = [jax.ShapeDtypeStruct(((N_DEV,) + a.shape) if gather else a.shape, a.dtype) for a in arrays]
    any_spec = pl.BlockSpec(memory_space=pl.ANY)
    return pl.pallas_call(
        body, name=name, in_specs=[any_spec] * n, out_specs=[any_spec] * n, out_shape=out_shape,
        scratch_shapes=[pltpu.SemaphoreType.DMA((n * n_peers,)), pltpu.SemaphoreType.DMA((n * n_peers,)),
                        pltpu.SemaphoreType.DMA((n,))],
    )(*arrays)


_HBM = pl.BlockSpec(memory_space=pltpu.HBM)
_SEM = pl.BlockSpec(memory_space=pltpu.SEMAPHORE)
_EFFECT = pltpu.SideEffectType.DATAFLOW_SIDE_EFFECTING


def _landing(arrays, gather):
    me = _my_index()
    lands = []
    for a in arrays:
        own = a[None] if gather else lax.dynamic_slice_in_dim(a, me, 1, axis=0)
        shape = ((N_DEV,) + a.shape) if gather else a.shape
        lands.append(lax.dynamic_update_slice_in_dim(lax.empty(shape, a.dtype), own, me, axis=0))
    return lands


def exchange_start(name, arrays, gather, carry):
    n = len(arrays)
    n_peers = N_DEV - 1
    lands = _landing(arrays, gather)
    n_thru = 2 * n + 1

    def body(*refs):
        src, land = refs[:n], refs[n:2 * n]
        send_sems, recv_sems = refs[n_thru], refs[n_thru + 1]
        token = refs[-1]
        me = _my_index()
        for a in range(n):
            for r in range(1, N_DEV):
                peer, pidx = _peer(r)
                pltpu.make_async_remote_copy(
                    src_ref=src[a] if gather else src[a].at[pidx], dst_ref=land[a].at[me],
                    send_sem=send_sems.at[a * n_peers + r - 1], recv_sem=recv_sems.at[a * n_peers + r - 1],
                    device_id=peer, device_id_type=MESH).start()
        token[...] = jnp.zeros_like(token)

    operands = list(arrays) + lands + [carry]
    outs = pl.pallas_call(
        body, name=name,
        out_shape=(pltpu.SemaphoreType.DMA((n * n_peers,)), pltpu.SemaphoreType.DMA((n * n_peers,)),
                   *[pltpu.HBM(a.shape, a.dtype) for a in operands], jax.ShapeDtypeStruct((8, 128), F32)),
        in_specs=[_HBM] * n_thru,
        out_specs=(_SEM, _SEM, *([_HBM] * n_thru), pl.BlockSpec(memory_space=pltpu.VMEM)),
        input_output_aliases={i: 2 + i for i in range(n_thru)},
        compiler_params=pltpu.CompilerParams(has_side_effects=_EFFECT),
    )(*[pltpu.with_memory_space_constraint(a, pltpu.HBM) for a in operands])
    handle = (outs[0], outs[1], list(outs[2:2 + n]), list(outs[2 + n:2 + 2 * n]), gather)
    return handle, outs[2 + 2 * n]


def exchange_wait(name, handle, after):
    send_sems, recv_sems, srcs, lands, gather = handle
    n = len(srcs)
    n_peers = N_DEV - 1

    def body(*refs):
        src, land = refs[:n], refs[n:2 * n]
        send_s, recv_s = refs[2 * n], refs[2 * n + 1]
        for a in range(n):
            for r in range(1, N_DEV):
                peer, pidx = _peer(r)
                cp = pltpu.make_async_remote_copy(
                    src_ref=src[a] if gather else src[a].at[pidx], dst_ref=land[a].at[pidx],
                    send_sem=send_s.at[a * n_peers + r - 1], recv_sem=recv_s.at[a * n_peers + r - 1],
                    device_id=peer, device_id_type=MESH)
                cp.wait_send()
                cp.wait_recv()

    shapes = [pltpu.HBM(a.shape, a.dtype) for a in srcs] + [pltpu.HBM(l.shape, l.dtype) for l in lands]
    outs = pl.pallas_call(
        body, name=name, out_shape=tuple(shapes),
        in_specs=[_HBM] * (2 * n) + [_SEM, _SEM, pl.BlockSpec(memory_space=pl.ANY)],
        out_specs=tuple([_HBM] * (2 * n)),
        input_output_aliases={i: i for i in range(2 * n)},
        compiler_params=pltpu.CompilerParams(has_side_effects=_EFFECT),
    )(*srcs, *lands, send_sems, recv_sems, after)
    return list(outs[n:])


def _row_tile(rows, cap=256):
    best = None
    for t in range(16, min(rows, cap) + 1, 16):
        if rows % t == 0:
            best = t
    return best if best is not None else rows


def sum_slots(name, recv):
    n, R, C = recv.shape
    tr = _row_tile(R)

    def body(r_ref, o_ref):
        g = r_ref[0].astype(F32)
        for d in range(1, n):
            g = g + r_ref[d].astype(F32)
        o_ref[...] = g

    return pl.pallas_call(
        body, grid=(R // tr,), name=name,
        in_specs=[pl.BlockSpec((n, tr, C), lambda i: (0, i, 0))],
        out_specs=pl.BlockSpec((tr, C), lambda i: (i, 0)),
        out_shape=jax.ShapeDtypeStruct((R, C), F32),
        compiler_params=_params(("arbitrary",)),
    )(recv)


def adamw(name, recv, w, m, v, layer=None, prev=None):
    n, R, C = recv.shape
    tr = _row_tile(R)

    def body(r_ref, w_ref, m_ref, v_ref, *rest):
        g_ref, d_ref, nm_ref, nv_ref = rest[-4:]
        g = r_ref[0].astype(F32)
        for d in range(1, n):
            g = g + r_ref[d].astype(F32)
        mm = ADAM_B1 * m_ref[...] + (1.0 - ADAM_B1) * g
        vv = ADAM_B2 * v_ref[...] + (1.0 - ADAM_B2) * (g * g)
        m_hat = mm / (1.0 - ADAM_B1 ** ADAM_STEP)
        v_hat = vv / (1.0 - ADAM_B2 ** ADAM_STEP)
        g_ref[...] = g
        d_ref[...] = -ADAM_LR * (m_hat / (jnp.sqrt(v_hat) + ADAM_EPS) + ADAM_WD * w_ref[...])
        nm_ref[...] = mm
        nv_ref[...] = vv

    if layer is None:
        row = pl.BlockSpec((tr, C), lambda i: (i, 0))
        shape = (R, C)
    else:
        row = pl.BlockSpec((None, tr, C), lambda i: (layer, i, 0))
        shape = w.shape
    prev = [] if prev is None else list(prev)
    return pl.pallas_call(
        body, grid=(R // tr,), name=name,
        in_specs=[pl.BlockSpec((n, tr, C), lambda i: (0, i, 0)), row, row, row]
                 + [pl.BlockSpec(memory_space=pl.ANY)] * len(prev),
        out_specs=[row] * 4,
        out_shape=[jax.ShapeDtypeStruct(shape, F32)] * 4,
        input_output_aliases={4 + o: o for o in range(len(prev))},
        compiler_params=_params(("arbitrary",)),
    )(recv, w, m, v, *prev)


def _adamw_nd(name, recv, w, m, v):
    shp = w.shape
    C = shp[-1]
    flat = lambda a: a.reshape(-1, C)
    outs = adamw(name, recv.reshape(recv.shape[0], -1, C), flat(w), flat(m), flat(v))
    return [o.reshape(shp) for o in outs]


_SMALL_NAMES = ("loss", "mix_norm", "ffn_norm", "final_norm", "lb_logits", "hg_out_norm", "pool_scale")
_LANES = 128


def _pack_small(parts):
    rows, layout = [], {}
    at = 0
    for name in parts:
        flat = parts[name].reshape(-1).astype(F32)
        n_rows = -(-flat.shape[0] // (8 * _LANES)) * 8
        flat = jnp.pad(flat, (0, n_rows * _LANES - flat.shape[0]))
        rows.append(flat.reshape(n_rows, _LANES))
        layout[name] = (at, parts[name].shape)
        at += n_rows
    return jnp.concatenate(rows, axis=0), layout


def _unpack_small(pack, layout):
    out = {}
    for name, (at, shape) in layout.items():
        size = int(np.prod(shape))
        n_rows = -(-size // _LANES)
        out[name] = pack[at:at + n_rows].reshape(-1)[:size].reshape(shape)
    return out


def kernel(x, mix_norm, ffn_norm, final_norm, ab_w_in, lb_logits, hg_out_norm, ab_w_out, pool_w, pool_scale, ffn_w_gate, ffn_w_up, ffn_w_down, loss_target, m_mix_norm, m_ffn_norm, m_final_norm, m_ab_w_in, m_lb_logits, m_hg_out_norm, m_ab_w_out, m_pool_w, m_pool_scale, m_ffn_w_gate, m_ffn_w_up, m_ffn_w_down, v_mix_norm, v_ffn_norm, v_final_norm, v_ab_w_in, v_lb_logits, v_hg_out_norm, v_ab_w_out, v_pool_w, v_pool_scale, v_ffn_w_gate, v_ffn_w_up, v_ffn_w_down):
    D = x.shape[-1]
    n_layers = ffn_w_gate.shape[0]
    G = pool_w.shape[1]
    P = pool_w.shape[3]
    me = _my_index()

    in_handle, mix_norm_after = exchange_start("gather_w_in_start", [ab_w_in[0].astype(BF16)], True, mix_norm)
    rest = [ab_w_out[0], pool_w[0]]
    for l in range(n_layers):
        rest += [ffn_w_gate[l], ffn_w_up[l], ffn_w_down[l]]
    rest = [s.astype(BF16) for s in rest] + [pool_scale]
    rest_handle = []

    def get_w_in(after):
        w_in = exchange_wait("gather_w_in_wait", in_handle, after)[0]
        handle, w_in = exchange_start("gather_rest_start", rest, True, w_in)
        rest_handle.append(handle)
        return w_in

    def get_w_rest(after):
        got = exchange_wait("gather_rest_wait", rest_handle[0], after)
        w_out_g = got[0].reshape(D, D)
        pool_g = got[1].transpose(1, 0, 2, 3).reshape(G, P, P)
        wg = [got[2 + 3 * l] for l in range(n_layers)]
        wu = [got[3 + 3 * l] for l in range(n_layers)]
        wd = [got[4 + 3 * l] for l in range(n_layers)]
        return w_out_g, pool_g, got[-1].reshape(1, D), wg, wu, wd

    in_flight = []

    def send(tag, grads, carry):
        if "pool_w" in grads:
            grads = dict(grads, pool_w=grads["pool_w"].reshape(G, N_DEV, P // N_DEV, P).transpose(1, 0, 2, 3))
        if "ab_w_out" in grads:
            grads = dict(grads, ab_w_out=grads["ab_w_out"].reshape(N_DEV, D // N_DEV, D))
        handle, carry = exchange_start("grads_" + tag + "_start", list(grads.values()), False, carry)
        in_flight.append((tag, list(grads.keys()), handle))
        return carry

    dx0, small = local_step(x[0], loss_target[0], mix_norm_after, ffn_norm, final_norm[None],
                            lb_logits, hg_out_norm, get_w_in, get_w_rest, send)

    recv = {}
    for tag, names, handle in in_flight:
        recv.update(zip(names, exchange_wait("grads_" + tag + "_wait", handle, dx0)))
    small_pack, layout = _pack_small({k: small[k] for k in _SMALL_NAMES})
    (small_all,) = exchange("gather_small", [small_pack], gather=True)
    tot = _unpack_small(sum_slots("sum_small", small_all), layout)

    res = {}
    res["ab_w_in"] = _adamw_nd("adamw_w_in", recv["ab_w_in"], ab_w_in, m_ab_w_in, v_ab_w_in)
    res["ab_w_out"] = _adamw_nd("adamw_w_out", recv["ab_w_out"], ab_w_out, m_ab_w_out, v_ab_w_out)
    res["pool_w"] = _adamw_nd("adamw_pool_w", recv["pool_w"], pool_w, m_pool_w, v_pool_w)
    ffn_in = {"ffn_w_gate": (ffn_w_gate, m_ffn_w_gate, v_ffn_w_gate),
              "ffn_w_up": (ffn_w_up, m_ffn_w_up, v_ffn_w_up),
              "ffn_w_down": (ffn_w_down, m_ffn_w_down, v_ffn_w_down)}
    for name, (w, m, v) in ffn_in.items():
        flip = name != "ffn_w_down"
        if flip:
            w, m, v = (jnp.swapaxes(a, 1, 2) for a in (w, m, v))
        outs = None
        for l in range(n_layers):
            outs = adamw("adamw_" + name, recv[name + "_" + str(l)], w, m, v, layer=l, prev=outs)
        res[name] = [jnp.swapaxes(o, 1, 2) for o in outs] if flip else outs

    n_ps = pool_scale.shape[1]
    small_g = dict(tot)
    small_g["pool_scale"] = lax.dynamic_slice(tot["pool_scale"], (0, me * n_ps), (1, n_ps))
    small_w = dict(mix_norm=(mix_norm, m_mix_norm, v_mix_norm), ffn_norm=(ffn_norm, m_ffn_norm, v_ffn_norm),
                   final_norm=(final_norm, m_final_norm, v_final_norm),
                   lb_logits=(lb_logits, m_lb_logits, v_lb_logits),
                   hg_out_norm=(hg_out_norm, m_hg_out_norm, v_hg_out_norm),
                   pool_scale=(pool_scale, m_pool_scale, v_pool_scale))
    g_pack, lay2 = _pack_small({k: small_g[k].reshape(small_w[k][0].shape) for k in small_w})
    w_pack, _ = _pack_small({k: small_w[k][0] for k in small_w})
    m_pack, _ = _pack_small({k: small_w[k][1] for k in small_w})
    v_pack, _ = _pack_small({k: small_w[k][2] for k in small_w})
    small_out = [_unpack_small(o, lay2) for o in adamw("adamw_small", g_pack[None], w_pack, m_pack, v_pack)]
    for k in small_w:
        res[k] = [small_out[o][k] for o in range(4)]

    order = ("mix_norm", "ffn_norm", "final_norm", "ab_w_in", "lb_logits", "hg_out_norm", "ab_w_out", "pool_w",
             "pool_scale", "ffn_w_gate", "ffn_w_up", "ffn_w_down")
    outs = [tot["loss"].reshape(()), dx0[None]]
    for o in range(4):
        outs += [res[k][o] for k in order]
    return tuple(outs)
```

```python
import functools
import math

import numpy as np
import jax
import jax.numpy as jnp
from jax import lax
from jax.experimental import pallas as pl
from jax.experimental.pallas import tpu as pltpu

F32 = jnp.float32
BF16 = jnp.bfloat16

N_DEV = 8
RMS_EPS = 1e-6
HEAD = 128
HG_CHUNK = 64
HG_HEADS_PER_BLOCK = 8
POOL_WINDOWS = (2, 4, 8, 16)
POOL_HALO = 16
ADAM_LR, ADAM_B1, ADAM_B2, ADAM_EPS, ADAM_WD, ADAM_STEP = 0.001, 0.9, 0.999, 1e-08, 0.01, 10
VMEM_LIMIT_BYTES = 60 * 1024 * 1024
MESH = pl.DeviceIdType.MESH


def _params(sem):
    return pltpu.CompilerParams(dimension_semantics=sem, vmem_limit_bytes=VMEM_LIMIT_BYTES)


def _sigmoid(x):
    return 1.0 / (1.0 + jnp.exp(-x))


def rms_fwd(x, gain, out_dtype, ts=512):
    S, D = x.shape

    def body(x_ref, g_ref, h_ref, r_ref):
        xv = x_ref[...]
        r = lax.rsqrt(jnp.mean(xv * xv, axis=-1, keepdims=True) + RMS_EPS)
        h_ref[...] = ((xv * r) * g_ref[...]).astype(h_ref.dtype)
        r_ref[...] = r

    return pl.pallas_call(
        body, grid=(S // ts,), name="rms_fwd",
        in_specs=[pl.BlockSpec((ts, D), lambda i: (i, 0)), pl.BlockSpec((1, D), lambda i: (0, 0))],
        out_specs=[pl.BlockSpec((ts, D), lambda i: (i, 0)), pl.BlockSpec((ts, 1), lambda i: (i, 0))],
        out_shape=[jax.ShapeDtypeStruct((S, D), out_dtype), jax.ShapeDtypeStruct((S, 1), F32)],
        compiler_params=_params(("arbitrary",)),
    )(x, gain)


RMS_BWD_ROWS = 128


def _rms_bwd_tile(first, dh_of, x_ref, r_ref, g_ref, dres_ref, dx_ref, dxb_ref, dg_ref, rows):
    gv = g_ref[...]
    part = None
    for c in range(rows // RMS_BWD_ROWS):
        sl = slice(c * RMS_BWD_ROWS, (c + 1) * RMS_BWD_ROWS)
        rr = r_ref[sl, :]
        xh = x_ref[sl, :] * rr
        dhv = dh_of(sl)
        dxh = dhv * gv
        dx = dres_ref[sl, :] + rr * (dxh - xh * jnp.mean(dxh * xh, axis=-1, keepdims=True))
        dx_ref[sl, :] = dx
        dxb_ref[sl, :] = dx.astype(BF16)
        p = jnp.sum(dhv * xh, axis=0, keepdims=True)
        part = p if part is None else part + p

    @pl.when(first)
    def _():
        dg_ref[...] = part

    @pl.when(jnp.logical_not(first))
    def _():
        dg_ref[...] += part


def matmul_rms_bwd(name, a_ops, b_ops, *, grid, a_spec, b_spec, tm, x, r, gain, dres):
    S, D = x.shape
    n_pairs = len(a_ops)
    nk = grid[1]
    dn = (((1,), (1,)), ((), ()))

    def body(*refs):
        a_refs = refs[:n_pairs]
        b_refs = refs[n_pairs:2 * n_pairs]
        x_ref, r_ref, g_ref, dres_ref, dx_ref, dxb_ref, dg_ref, acc_ref = refs[2 * n_pairs:]
        i = pl.program_id(0)
        k = pl.program_id(1)

        @pl.when(k == 0)
        def _():
            acc_ref[...] = jnp.zeros_like(acc_ref)

        part = None
        for ar, br in zip(a_refs, b_refs):
            d = lax.dot_general(ar[...], br[...], dn, preferred_element_type=F32)
            part = d if part is None else part + d
        acc_ref[...] += part

        @pl.when(k == nk - 1)
        def _():
            _rms_bwd_tile(i == 0, lambda sl: acc_ref[sl, :], x_ref, r_ref, g_ref, dres_ref, dx_ref, dxb_ref,
                          dg_ref, tm)

    row = pl.BlockSpec((tm, D), lambda i, k: (i, 0))
    vec = pl.BlockSpec((1, D), lambda i, k: (0, 0))
    return pl.pallas_call(
        body, grid=grid, name=name,
        in_specs=[a_spec] * n_pairs + [b_spec] * n_pairs
                 + [row, pl.BlockSpec((tm, 1), lambda i, k: (i, 0)), vec, row],
        out_specs=[row, row, vec],
        out_shape=[jax.ShapeDtypeStruct((S, D), F32), jax.ShapeDtypeStruct((S, D), BF16),
                   jax.ShapeDtypeStruct((1, D), F32)],
        scratch_shapes=[pltpu.VMEM((tm, D), F32)],
        compiler_params=_params(("arbitrary", "arbitrary")),
    )(*a_ops, *b_ops, x, r, gain, dres)


def loss_and_final_bwd(x, gain, target, ts=512):
    S, D = x.shape

    def body(x_ref, g_ref, t_ref, loss_ref, dx_ref, dxb_ref, dg_ref):
        i = pl.program_id(0)
        xv = x_ref[...]
        rr = lax.rsqrt(jnp.mean(xv * xv, axis=-1, keepdims=True) + RMS_EPS)
        xh = xv * rr
        err = xh * g_ref[...] - t_ref[...]
        part_loss = 0.5 * jnp.sum(jnp.mean(err * err, axis=-1, keepdims=True))
        dy = err / D
        dxh = dy * g_ref[...]
        dx = rr * (dxh - xh * jnp.mean(dxh * xh, axis=-1, keepdims=True))
        dx_ref[...] = dx
        dxb_ref[...] = dx.astype(BF16)
        part = jnp.sum(dy * xh, axis=0, keepdims=True)

        @pl.when(i == 0)
        def _():
            dg_ref[...] = part
            loss_ref[...] = jnp.zeros_like(loss_ref) + part_loss

        @pl.when(i > 0)
        def _():
            dg_ref[...] += part
            loss_ref[...] += part_loss

    row = pl.BlockSpec((ts, D), lambda i: (i, 0))
    vec = pl.BlockSpec((1, D), lambda i: (0, 0))
    return pl.pallas_call(
        body, grid=(S // ts,), name="loss_final",
        in_specs=[row, vec, row],
        out_specs=[pl.BlockSpec((8, 128), lambda i: (0, 0)), row, row, vec],
        out_shape=[jax.ShapeDtypeStruct((8, 128), F32), jax.ShapeDtypeStruct((S, D), F32),
                   jax.ShapeDtypeStruct((S, D), BF16), jax.ShapeDtypeStruct((1, D), F32)],
        compiler_params=_params(("arbitrary",)),
    )(x, gain, target)


def matmul(name, a_ops, b_ops, *, grid, a_spec, b_spec, out_spec, out_shape, out_dtypes, acc_shape,
           trans_a=False, trans_b=False, res=None, res_spec=None, bf16_scale=None, bf16_scale_spec=None):
    n_pairs = len(a_ops)
    n_out = len(out_dtypes)
    nk = grid[-1]
    kaxis = len(grid) - 1
    dn = (((0,) if trans_a else (1,), (1,) if trans_b else (0,)), ((), ()))

    def body(*refs):
        a_refs = refs[:n_pairs]
        b_refs = refs[n_pairs:2 * n_pairs]
        pos = 2 * n_pairs
        res_ref = None
        if res is not None:
            res_ref = refs[pos]
            pos += 1
        scale_ref = None
        if bf16_scale is not None:
            scale_ref = refs[pos]
            pos += 1
        out_refs = refs[pos:pos + n_out]
        acc_ref = refs[pos + n_out]
        k = pl.program_id(kaxis)
        in_place = n_out == 1 and out_dtypes[0] == F32
        target = out_refs[0] if in_place else acc_ref

        def finish(val):
            if res_ref is not None:
                val = val + res_ref[...]
            for o in out_refs:
                if scale_ref is not None and o.dtype == BF16:
                    o[...] = (val * scale_ref[...]).astype(BF16)
                else:
                    o[...] = val.astype(o.dtype)

        if nk > 1:
            @pl.when(k == 0)
            def _():
                if in_place and res_ref is not None:
                    target[...] = res_ref[...]
                else:
                    target[...] = jnp.zeros_like(target)

        part = None
        for ar, br in zip(a_refs, b_refs):
            d = lax.dot_general(ar[...].astype(BF16), br[...].astype(BF16), dn, preferred_element_type=F32)
            part = d if part is None else part + d

        if nk == 1:
            finish(part)
        else:
            target[...] += part
            if not in_place:
                @pl.when(k == nk - 1)
                def _():
                    finish(acc_ref[...])

    in_specs = [a_spec] * n_pairs + [b_spec] * n_pairs
    operands = list(a_ops) + list(b_ops)
    if res is not None:
        in_specs.append(res_spec)
        operands.append(res)
    if bf16_scale is not None:
        in_specs.append(bf16_scale_spec)
        operands.append(bf16_scale)
    return pl.pallas_call(
        body, grid=grid, name=name, in_specs=in_specs,
        out_specs=[out_spec] * n_out,
        out_shape=[jax.ShapeDtypeStruct(out_shape, dt) for dt in out_dtypes],
        scratch_shapes=[pltpu.VMEM(acc_shape, F32)],
        compiler_params=_params(("arbitrary",) * len(grid)),
    )(*operands)


def ffn_gate_up(h, wg, wu, tm=1024):
    S, D = h.shape
    nb = wg.shape[2]

    def body(h_ref, wg_ref, wu_ref, p_ref, r_ref, a_ref):
        for c in range(2):
            rows = slice(c * (tm // 2), (c + 1) * (tm // 2))
            hv = h_ref[rows, :]
            g = jnp.dot(hv, wg_ref[...], preferred_element_type=F32)
            u = jnp.dot(hv, wu_ref[...], preferred_element_type=F32)
            s = _sigmoid(g)
            p = g * s
            p_ref[rows, :] = p
            r_ref[rows, :] = u * (s * (1.0 + g * (1.0 - s)))
            a_ref[rows, :] = (p * u).astype(BF16)

    wspec = pl.BlockSpec((None, D, nb), lambda j, i: (j, 0, 0))
    ospec = pl.BlockSpec((None, tm, nb), lambda j, i: (j, i, 0))
    return pl.pallas_call(
        body, grid=(N_DEV, S // tm), name="ffn_gate_up",
        in_specs=[pl.BlockSpec((tm, D), lambda j, i: (i, 0)), wspec, wspec],
        out_specs=[ospec, ospec, ospec],
        out_shape=[jax.ShapeDtypeStruct((N_DEV, S, nb), F32), jax.ShapeDtypeStruct((N_DEV, S, nb), F32),
                   jax.ShapeDtypeStruct((N_DEV, S, nb), BF16)],
        compiler_params=_params(("arbitrary", "arbitrary")),
    )(h, wg, wu)


def ffn_bwd_hidden(dy, wd, p, r, tm=1024):
    S, D = dy.shape
    nb = wd.shape[1]

    def body(dy_ref, wd_ref, p_ref, r_ref, dg_ref, du_ref):
        for c in range(2):
            rows = slice(c * (tm // 2), (c + 1) * (tm // 2))
            da = lax.dot_general(dy_ref[rows, :], wd_ref[...], (((1,), (1,)), ((), ())),
                                 preferred_element_type=F32)
            du_ref[rows, :] = (da * p_ref[rows, :]).astype(BF16)
            dg_ref[rows, :] = (da * r_ref[rows, :]).astype(BF16)

    hspec = pl.BlockSpec((None, tm, nb), lambda j, i: (j, i, 0))
    return pl.pallas_call(
        body, grid=(N_DEV, S // tm), name="ffn_bwd_hidden",
        in_specs=[pl.BlockSpec((tm, D), lambda j, i: (i, 0)), pl.BlockSpec((None, nb, D), lambda j, i: (j, 0, 0)),
                  hspec, hspec],
        out_specs=[hspec, hspec],
        out_shape=[jax.ShapeDtypeStruct((N_DEV, S, nb), BF16), jax.ShapeDtypeStruct((N_DEV, S, nb), BF16)],
        compiler_params=_params(("arbitrary", "arbitrary")),
    )(dy, wd, p, r)


def ffn_forward(h, xres, wg, wu, wd, tm=1024):
    S, D = h.shape
    nb = wg.shape[2]
    g, u, a = ffn_gate_up(h, wg, wu)
    (xo,) = matmul(
        "ffn_down", [a], [wd], grid=(S // tm, N_DEV),
        a_spec=pl.BlockSpec((None, tm, nb), lambda i, j: (j, i, 0)),
        b_spec=pl.BlockSpec((None, nb, D), lambda i, j: (j, 0, 0)),
        out_spec=pl.BlockSpec((tm, D), lambda i, j: (i, 0)), out_shape=(S, D), out_dtypes=[F32],
        acc_shape=(tm, D), res=xres, res_spec=pl.BlockSpec((tm, D), lambda i, j: (i, 0)))
    return xo, (g, u, a)


def ffn_backward(dy_b, h, saved, wg, wu, wd, x, r, gain, dres, tm=512, tk=2048):
    S, D = h.shape
    nb = wg.shape[2]
    g, u, a = saved
    dg, du = ffn_bwd_hidden(dy_b, wd, g, u)
    dx = matmul_rms_bwd(
        "ffn_dh", [dg, du], [wg, wu], grid=(S // tm, N_DEV),
        a_spec=pl.BlockSpec((None, tm, nb), lambda i, j: (j, i, 0)),
        b_spec=pl.BlockSpec((None, D, nb), lambda i, j: (j, 0, 0)),
        tm=tm, x=x, r=r, gain=gain, dres=dres)

    def wgrad_in(name, dhid):
        (dw,) = matmul(
            name, [dhid], [h], grid=(N_DEV, S // tk),
            a_spec=pl.BlockSpec((None, tk, nb), lambda j, k: (j, k, 0)),
            b_spec=pl.BlockSpec((tk, D), lambda j, k: (k, 0)),
            out_spec=pl.BlockSpec((None, nb, D), lambda j, k: (j, 0, 0)), out_shape=(N_DEV, nb, D),
            out_dtypes=[BF16], acc_shape=(nb, D), trans_a=True)
        return dw

    dwg = wgrad_in("ffn_dwg", dg)
    dwu = wgrad_in("ffn_dwu", du)
    (dwd,) = matmul(
        "ffn_dwd", [a], [dy_b], grid=(N_DEV, S // tk),
        a_spec=pl.BlockSpec((None, tk, nb), lambda j, k: (j, k, 0)),
        b_spec=pl.BlockSpec((tk, D), lambda j, k: (k, 0)),
        out_spec=pl.BlockSpec((None, nb, D), lambda j, k: (j, 0, 0)), out_shape=(N_DEV, nb, D),
        out_dtypes=[BF16], acc_shape=(nb, D), trans_a=True)
    return dx, dwg, dwu, dwd


def _pool_counts(row0, n, w):
    pos = row0 + lax.broadcasted_iota(jnp.int32, (n, 1), 0)
    return jnp.minimum(pos + 1, w).astype(F32)


def pool_forward(h, xres, w, scale, ts=256):
    S, D = h.shape
    G = len(POOL_WINDOWS)
    P = D // G
    hb = ts // POOL_HALO

    def body(h_ref, halo_ref, x_ref, w_ref, s_ref, xo_ref, p_ref):
        i = pl.program_id(0)
        for gi, win in enumerate(POOL_WINDOWS):
            cols = slice(gi * P, (gi + 1) * P)
            cur = h_ref[:, cols]
            halo = jnp.where(i > 0, halo_ref[:, cols], 0.0)
            acc = jnp.concatenate([halo, cur], axis=0)
            step = 1
            while step < win:
                acc = acc + pltpu.roll(acc, step, 0)
                step *= 2
            wsum = acc[POOL_HALO:, :]
            pooled = wsum / _pool_counts(i * ts, ts, win) - cur
            pb = pooled.astype(BF16)
            p_ref[:, cols] = pb
            mixed = jnp.dot(pb, w_ref[gi], preferred_element_type=F32)
            xo_ref[:, cols] = x_ref[:, cols] + mixed * s_ref[:, cols]

    row = pl.BlockSpec((ts, D), lambda i: (i, 0))
    return pl.pallas_call(
        body, grid=(S // ts,), name="pool_fwd",
        in_specs=[row, pl.BlockSpec((POOL_HALO, D), lambda i: (jnp.maximum(i * hb - 1, 0), 0)), row,
                  pl.BlockSpec((G, P, P), lambda i: (0, 0, 0)), pl.BlockSpec((1, D), lambda i: (0, 0))],
        out_specs=[row, row],
        out_shape=[jax.ShapeDtypeStruct((S, D), F32), jax.ShapeDtypeStruct((S, D), BF16)],
        compiler_params=_params(("arbitrary",)),
    )(h, h, xres, w, scale)


def pool_backward_mix(dx, pooled, w, scale, ts=256):
    S, D = dx.shape
    G = len(POOL_WINDOWS)
    P = D // G

    def body(dx_ref, p_ref, w_ref, s_ref, dm_ref, dp_ref, ds_ref):
        i = pl.program_id(0)
        parts = []
        for gi in range(G):
            cols = slice(gi * P, (gi + 1) * P)
            dxv = dx_ref[:, cols]
            dmb = (dxv * s_ref[:, cols]).astype(BF16)
            dm_ref[:, cols] = dmb
            dp_ref[:, cols] = lax.dot_general(dmb, w_ref[gi], (((1,), (1,)), ((), ())),
                                              preferred_element_type=F32)
            mixed = jnp.dot(p_ref[:, cols], w_ref[gi], preferred_element_type=F32)
            parts.append(jnp.sum(dxv * mixed, axis=0, keepdims=True))
        part = jnp.concatenate(parts, axis=1)

        @pl.when(i == 0)
        def _():
            ds_ref[...] = part

        @pl.when(i > 0)
        def _():
            ds_ref[...] += part

    row = pl.BlockSpec((ts, D), lambda i: (i, 0))
    vec = pl.BlockSpec((1, D), lambda i: (0, 0))
    return pl.pallas_call(
        body, grid=(S // ts,), name="pool_bwd_mix",
        in_specs=[row, row, pl.BlockSpec((G, P, P), lambda i: (0, 0, 0)), vec],
        out_specs=[row, row, vec],
        out_shape=[jax.ShapeDtypeStruct((S, D), BF16), jax.ShapeDtypeStruct((S, D), F32),
                   jax.ShapeDtypeStruct((1, D), F32)],
        compiler_params=_params(("arbitrary",)),
    )(dx, pooled, w, scale)


def pool_backward_window(dp, x, r, gain, dres, ts=256):
    S, D = dp.shape
    G = len(POOL_WINDOWS)
    P = D // G
    hb = ts // POOL_HALO
    n_i = S // ts
    n_rows = ts + POOL_HALO

    def body(dp_ref, halo_ref, x_ref, r_ref, g_ref, dres_ref, dx_ref, dxb_ref, dg_ref, dh_ref):
        i = pl.program_id(0)
        for gi, win in enumerate(POOL_WINDOWS):
            cols = slice(gi * P, (gi + 1) * P)
            cur = dp_ref[:, cols]
            halo = jnp.where(i < n_i - 1, halo_ref[:, cols], 0.0)
            acc = jnp.concatenate([cur / _pool_counts(i * ts, ts, win),
                                   halo / _pool_counts((i + 1) * ts, POOL_HALO, win)], axis=0)
            step = 1
            while step < win:
                acc = acc + pltpu.roll(acc, n_rows - step, 0)
                step *= 2
            dh_ref[:, cols] = acc[:ts, :] - cur
        _rms_bwd_tile(i == 0, lambda sl: dh_ref[sl, :], x_ref, r_ref, g_ref, dres_ref, dx_ref, dxb_ref, dg_ref, ts)

    row = pl.BlockSpec((ts, D), lambda i: (i, 0))
    vec = pl.BlockSpec((1, D), lambda i: (0, 0))
    return pl.pallas_call(
        body, grid=(n_i,), name="pool_bwd_window",
        in_specs=[row, pl.BlockSpec((POOL_HALO, D), lambda i: (jnp.minimum((i + 1) * hb, S // POOL_HALO - 1), 0)),
                  row, pl.BlockSpec((ts, 1), lambda i: (i, 0)), vec, row],
        out_specs=[row, row, vec],
        out_shape=[jax.ShapeDtypeStruct((S, D), F32), jax.ShapeDtypeStruct((S, D), BF16),
                   jax.ShapeDtypeStruct((1, D), F32)],
        scratch_shapes=[pltpu.VMEM((ts, D), F32)],
        compiler_params=_params(("arbitrary",)),
    )(dp, dp, x, r, gain, dres)


_HG_LEVELS = (32, 16, 8, 4, 2, 1)
_N_LEV = len(_HG_LEVELS) + 1


def _hgrn_constants():
    C = HG_CHUNK
    t = np.arange(C)
    tri = (t[None, :] <= t[:, None]).astype(np.float32)
    blocks = [tri]
    masks, upq, upk = [], [], []
    for m in _HG_LEVELS:
        p = (t // (2 * m)) * 2 * m + m - 1
        blocks.append(tri[p])
        masks.append(((t[:, None] // (2 * m)) == (t[None, :] // (2 * m))).astype(np.float32))
        upper = (t % (2 * m)) >= m
        upq.append(np.repeat(upper[:, None], HEAD, 1).astype(np.float32))
        upk.append(np.repeat(~upper[:, None], HEAD, 1).astype(np.float32))
    blocks.append(tri)
    masks.append(np.eye(C, dtype=np.float32))
    upq.append(np.ones((C, HEAD), np.float32))
    upk.append(np.ones((C, HEAD), np.float32))
    mstack = np.concatenate(blocks, axis=0)
    mstack3 = np.concatenate([mstack] * 3, axis=1)
    trirev3 = np.concatenate([tri.T] * 3, axis=1)
    return (jnp.asarray(mstack3, BF16), jnp.asarray(np.stack(masks)), jnp.asarray(np.stack(upq)),
            jnp.asarray(np.stack(upk)), jnp.asarray(trirev3, BF16))


def _split3(x):
    hi = x.astype(BF16)
    r1 = x - hi.astype(F32)
    mid = r1.astype(BF16)
    lo = (r1 - mid.astype(F32)).astype(BF16)
    return jnp.concatenate([hi, mid, lo], axis=0)


def _hgrn_chunk_common(qa, fa, lbv, mstack3, upq, upk):
    sq = _sigmoid(qa)
    q = qa * sq
    sf = _sigmoid(fa)
    f = lbv + (1.0 - lbv) * sf
    g = jnp.log(f)
    k = 1.0 - f
    gall = jnp.dot(mstack3, _split3(g), preferred_element_type=F32).reshape(_N_LEV + 1, HG_CHUNK, HEAD)
    G = gall[0]
    eq_exp = G[None] - gall[1:]
    eq = jnp.exp(jnp.minimum(eq_exp, 0.0)) * upq
    ek = jnp.exp(jnp.minimum(-eq_exp, 0.0)) * upk
    Qs = (q[None] * eq).astype(BF16)
    Ks = (k[None] * ek).astype(BF16)
    return sq, q, sf, f, k, G, eq, ek, Qs, Ks


def hgrn_forward(proj, lb, hg_norm, ts=512):
    S = proj.shape[0]
    nh = lb.shape[1] // HEAD
    C = HG_CHUNK
    ncs = ts // C
    mstack3, masks, upq, upk, _ = _hgrn_constants()

    def body(qa_ref, fa_ref, ia_ref, ga_ref, lb_ref, gn_ref, ms_ref, mk_ref, uq_ref, uk_ref,
             oa_ref, oraw_ref, st_ref, state):
        tt = pl.program_id(1)

        @pl.when(tt == 0)
        def _():
            state[...] = jnp.zeros_like(state)

        gn = gn_ref[...]

        def chunk(c, carry):
            sl = pl.ds(pl.multiple_of(c * C, C), C)
            for hh in range(HG_HEADS_PER_BLOCK):
                cols = slice(hh * HEAD, (hh + 1) * HEAD)
                qa, fa, v, ga = qa_ref[sl, cols], fa_ref[sl, cols], ia_ref[sl, cols], ga_ref[sl, cols]
                _, q, _, _, k, G, _, _, Qs, Ks = _hgrn_chunk_common(qa, fa, lb_ref[:, cols], ms_ref[...],
                                                                    uq_ref[...], uk_ref[...])
                att7 = lax.dot_general(Qs, Ks, (((2,), (2,)), ((0,), (0,))), preferred_element_type=F32)
                att = jnp.sum(att7 * mk_ref[...], axis=0)
                st = state[hh]
                st_ref[hh, c] = st
                vb = v.astype(BF16)
                qg = (q * jnp.exp(G)).astype(BF16)
                o = jnp.dot(att.astype(BF16), vb, preferred_element_type=F32)
                o = o + lax.dot_general(qg, st.astype(BF16), (((1,), (1,)), ((), ())),
                                        preferred_element_type=F32)
                g_last = G[C - 1:C, :]
                kh = (k * jnp.exp(g_last - G)).astype(BF16)
                state[hh] = st * jnp.exp(g_last) + lax.dot_general(vb, kh, (((0,), (0,)), ((), ())),
                                                                   preferred_element_type=F32)
                oraw_ref[sl, cols] = o
                r = lax.rsqrt(jnp.mean(o * o, axis=-1, keepdims=True) + RMS_EPS)
                oa_ref[sl, cols] = (((o * r) * gn) * (ga * _sigmoid(ga))).astype(BF16)
            return carry

        lax.fori_loop(0, ncs, chunk, 0)

    hpb = HG_HEADS_PER_BLOCK
    wide = hpb * HEAD

    def col(m0):
        return pl.BlockSpec((ts, wide), lambda h, t: (t, m0 // hpb + h))

    const3 = lambda shape: pl.BlockSpec(shape, lambda h, t: (0, 0, 0))
    return pl.pallas_call(
        body, grid=(nh // hpb, S // ts), name="hgrn_fwd",
        in_specs=[col(0), col(nh), col(2 * nh), col(3 * nh),
                  pl.BlockSpec((1, wide), lambda h, t: (0, h)), pl.BlockSpec((1, HEAD), lambda h, t: (0, 0)),
                  pl.BlockSpec(mstack3.shape, lambda h, t: (0, 0)), const3(masks.shape), const3(upq.shape),
                  const3(upk.shape)],
        out_specs=[pl.BlockSpec((ts, wide), lambda h, t: (t, h)), pl.BlockSpec((ts, wide), lambda h, t: (t, h)),
                   pl.BlockSpec((hpb, ncs, HEAD, HEAD), lambda h, t: (h, t, 0, 0))],
        out_shape=[jax.ShapeDtypeStruct((S, nh * HEAD), BF16), jax.ShapeDtypeStruct((S, nh * HEAD), F32),
                   jax.ShapeDtypeStruct((nh, S // C, HEAD, HEAD), F32)],
        scratch_shapes=[pltpu.VMEM((hpb, HEAD, HEAD), F32)],
        compiler_params=_params(("arbitrary", "arbitrary")),
    )(proj, proj, proj, proj, lb, hg_norm, mstack3, masks, upq, upk)


def hgrn_backward(dcat, proj, oraw, states, lb, hg_norm, ts=512):
    S = proj.shape[0]
    nh = lb.shape[1] // HEAD
    C = HG_CHUNK
    ncs = ts // C
    nt = S // ts
    mstack3, masks, upq, upk, trirev3 = _hgrn_constants()

    def body(do_ref, qa_ref, fa_ref, ia_ref, ga_ref, or_ref, st_ref, lb_ref, gn_ref, ms_ref, mk_ref, uq_ref,
             uk_ref, tr_ref, dqa_ref, dfa_ref, dia_ref, dga_ref, dlb_ref, dgn_ref, dstate):
        tt = pl.program_id(1)

        @pl.when(tt == 0)
        def _():
            dstate[...] = jnp.zeros_like(dstate)
            dlb_ref[...] = jnp.zeros_like(dlb_ref)
            dgn_ref[...] = jnp.zeros_like(dgn_ref)

        gn = gn_ref[...]

        def chunk(cc, carry):
            c = ncs - 1 - cc
            sl = pl.ds(pl.multiple_of(c * C, C), C)
            for hh in range(HG_HEADS_PER_BLOCK):
                cols = slice(hh * HEAD, (hh + 1) * HEAD)
                lbv = lb_ref[:, cols]
                qa, fa, v, ga = qa_ref[sl, cols], fa_ref[sl, cols], ia_ref[sl, cols], ga_ref[sl, cols]
                sq, q, sf, f, k, G, eq, ek, Qs, Ks = _hgrn_chunk_common(qa, fa, lbv, ms_ref[...], uq_ref[...],
                                                                        uk_ref[...])
                mk = mk_ref[...]
                att7 = lax.dot_general(Qs, Ks, (((2,), (2,)), ((0,), (0,))), preferred_element_type=F32)
                att = jnp.sum(att7 * mk, axis=0)
                o = or_ref[sl, cols]
                dO = do_ref[sl, cols]
                sg = _sigmoid(ga)
                r = lax.rsqrt(jnp.mean(o * o, axis=-1, keepdims=True) + RMS_EPS)
                xh = o * r
                dga_ref[sl, cols] = (dO * (xh * gn) * (sg * (1.0 + ga * (1.0 - sg)))).astype(BF16)
                don = dO * (ga * sg)
                dgn_ref[hh] += jnp.sum(don * xh, axis=0, keepdims=True)
                dxh = don * gn
                do = r * (dxh - xh * jnp.mean(dxh * xh, axis=-1, keepdims=True))
                dob = do.astype(BF16)
                st = st_ref[hh, c]
                dst = dstate[hh]
                dstb = dst.astype(BF16)
                vb = v.astype(BF16)
                eG = jnp.exp(G)
                g_last = G[C - 1:C, :]
                e_last = jnp.exp(g_last)
                e_tail = jnp.exp(g_last - G)
                qg = (q * eG).astype(BF16)
                kh = (k * e_tail).astype(BF16)
                dq_inter = jnp.dot(dob, st.astype(BF16), preferred_element_type=F32) * eG
                dk_inter = jnp.dot(vb, dstb, preferred_element_type=F32) * e_tail
                dv = lax.dot_general(kh, dstb, (((1,), (1,)), ((), ())), preferred_element_type=F32)
                dv = dv + lax.dot_general(att.astype(BF16), dob, (((0,), (0,)), ((), ())),
                                          preferred_element_type=F32)
                dA = lax.dot_general(dob, vb, (((1,), (1,)), ((), ())), preferred_element_type=F32)
                dA7 = (dA[None] * mk).astype(BF16)
                dAT7 = (dA.T[None] * mk).astype(BF16)
                dQs = lax.dot_general(dA7, Ks, (((2,), (1,)), ((0,), (0,))), preferred_element_type=F32)
                dKs = lax.dot_general(dAT7, Qs, (((2,), (1,)), ((0,), (0,))), preferred_element_type=F32)
                dq = dq_inter + jnp.sum(dQs * eq, axis=0)
                dk = dk_inter + jnp.sum(dKs * ek, axis=0)
                dG = (jnp.sum(Qs.astype(F32) * dQs - Ks.astype(F32) * dKs, axis=0)
                      + q * dq_inter - k * dk_inter)
                last_extra = (jnp.sum(k * dk_inter, axis=0, keepdims=True)
                              + e_last * jnp.sum(dst * st, axis=0, keepdims=True))
                is_last = lax.broadcasted_iota(jnp.int32, (C, 1), 0) == C - 1
                dG = dG + jnp.where(is_last, last_extra, 0.0)
                dg = jnp.dot(tr_ref[...], _split3(dG), preferred_element_type=F32)
                df = dg / f - dk
                dfa_ref[sl, cols] = (df * (1.0 - lbv) * (sf * (1.0 - sf))).astype(BF16)
                dlb_ref[:, cols] += jnp.sum(df * (1.0 - sf), axis=0, keepdims=True)
                dqa_ref[sl, cols] = (dq * (sq * (1.0 + qa * (1.0 - sq)))).astype(BF16)
                dia_ref[sl, cols] = dv.astype(BF16)
                dstate[hh] = dst * e_last + lax.dot_general(dob, qg, (((0,), (0,)), ((), ())),
                                                            preferred_element_type=F32)
            return carry

        lax.fori_loop(0, ncs, chunk, 0)

    hpb = HG_HEADS_PER_BLOCK
    wide = hpb * HEAD

    def col(m0):
        return pl.BlockSpec((ts, wide), lambda h, t: (nt - 1 - t, m0 // hpb + h))

    const3 = lambda shape: pl.BlockSpec(shape, lambda h, t: (0, 0, 0))
    const2 = lambda shape: pl.BlockSpec(shape, lambda h, t: (0, 0))
    ocol = pl.BlockSpec((ts, wide), lambda h, t: (nt - 1 - t, h))
    half = nh * HEAD
    return pl.pallas_call(
        body, grid=(nh // hpb, nt), name="hgrn_bwd",
        in_specs=[col(0), col(0), col(nh), col(2 * nh), col(3 * nh), col(0),
                  pl.BlockSpec((hpb, ncs, HEAD, HEAD), lambda h, t: (h, nt - 1 - t, 0, 0)),
                  pl.BlockSpec((1, wide), lambda h, t: (0, h)), const2((1, HEAD)),
                  const2(mstack3.shape), const3(masks.shape), const3(upq.shape), const3(upk.shape),
                  const2(trirev3.shape)],
        out_specs=[ocol, ocol, ocol, ocol, pl.BlockSpec((1, wide), lambda h, t: (0, h)),
                   pl.BlockSpec((hpb, 1, HEAD), lambda h, t: (h, 0, 0))],
        out_shape=[jax.ShapeDtypeStruct((S, half), BF16)] * 4
                  + [jax.ShapeDtypeStruct((1, half), F32), jax.ShapeDtypeStruct((nh, 1, HEAD), F32)],
        scratch_shapes=[pltpu.VMEM((hpb, HEAD, HEAD), F32)],
        compiler_params=_params(("arbitrary", "arbitrary")),
    )(dcat, proj, proj, proj, proj, oraw, states, lb, hg_norm, mstack3, masks, upq, upk, trirev3)


SB_SUB = 128
LOG2_E = 1.4426950408889634
SB_SCALE = 1.0 / math.sqrt(HEAD)
SB_QUERY_SCALE = SB_SCALE * LOG2_E


def _split2(x):
    hi = x.astype(BF16)
    lo = (x - hi.astype(F32)).astype(BF16)
    return jnp.concatenate([hi, lo], axis=1)


def _sb_constants():
    j = np.arange(SB_SUB)
    after = (j[:, None] > j[None, :]).astype(np.float32)
    before = (j[:, None] < j[None, :]).astype(np.float32)
    return (jnp.asarray(np.concatenate([after, after], axis=0), BF16),
            jnp.asarray(np.concatenate([before, before], axis=0), BF16))


def _sb_tri(i):
    return (i * (i + 1)) // 2


def _sb_diag_mask(t):
    return lax.broadcasted_iota(jnp.int32, (t, t), 1) < lax.broadcasted_iota(jnp.int32, (t, t), 0)


def _sb_scores(q, k_ref, col0, t):
    ks = k_ref[pl.ds(pl.multiple_of(col0, t), t), :]
    return lax.dot_general(q, ks, (((1,), (1,)), ((), ())), preferred_element_type=F32)


def _sb_weights(z, mask, run, after2):
    nsub = z.shape[1] // SB_SUB
    nz = -z
    lk = jnp.minimum(nz, 0.0) - jnp.log(1.0 + jnp.exp2(jnp.minimum(z, nz))) * LOG2_E
    if mask is not None:
        lk = jnp.where(mask, lk, 0.0)
    locs, tots = [], []
    for b in range(nsub):
        lkb = lk[:, b * SB_SUB:(b + 1) * SB_SUB]
        loc = jnp.dot(_split2(lkb), after2, preferred_element_type=F32)
        locs.append(loc)
        tots.append(loc[:, 0:1] + lkb[:, 0:1])
    ws = [None] * nsub
    for b in reversed(range(nsub)):
        sl = slice(b * SB_SUB, (b + 1) * SB_SUB)
        ws[b] = jnp.exp2(z[:, sl] + lk[:, sl] + (locs[b] + run))
        run = run + tots[b]
    w = jnp.concatenate(ws, axis=1)
    if mask is not None:
        w = jnp.where(mask, w, 0.0)
    return w, run


def sb_forward(projb, nh, m0, t=512):
    S = projb.shape[0]
    after2, _ = _sb_constants()
    n_i = S // t

    def body(q_ref, k_ref, v_ref, af_ref, o_ref, w_hbm, wbuf, wsem):
        h = pl.program_id(0)
        i = pl.program_id(1)
        q = q_ref[...]
        after = af_ref[...]
        base = _sb_tri(i)

        def store(slot, jb):
            return pltpu.make_async_copy(wbuf.at[slot], w_hbm.at[h, base + jb], wsem.at[slot])

        def block(n, jb, run, mask):
            slot = n % 2

            @pl.when(n >= 2)
            def _():
                store(slot, jb).wait()

            z = _sb_scores(q, k_ref, jb * t, t)
            w, run = _sb_weights(z, mask, run, after)
            wb = w.astype(BF16)
            wbuf[slot] = wb
            store(slot, jb).start()
            vs = v_ref[pl.ds(pl.multiple_of(jb * t, t), t), :]
            return run, jnp.dot(wb, vs, preferred_element_type=F32)

        run, acc = block(0, i, jnp.zeros((t, 1), F32), _sb_diag_mask(t))

        def step(n, carry):
            run, acc = carry
            run, part = block(n + 1, i - 1 - n, run, None)
            return run, acc + part

        _, acc = lax.fori_loop(0, i, step, (run, acc))
        o_ref[...] = acc.astype(BF16)
        store(i % 2, 0).wait()

        @pl.when(i >= 1)
        def _():
            store((i + 1) % 2, 0).wait()

    return pl.pallas_call(
        body, grid=(nh, n_i), name="sb_fwd",
        in_specs=[pl.BlockSpec((t, HEAD), lambda h, i: (i, m0 + h)),
                  pl.BlockSpec((S, HEAD), lambda h, i: (0, m0 + nh + h)),
                  pl.BlockSpec((S, HEAD), lambda h, i: (0, m0 + 2 * nh + h)),
                  pl.BlockSpec(after2.shape, lambda h, i: (0, 0))],
        out_specs=[pl.BlockSpec((t, HEAD), lambda h, i: (i, h)), pl.BlockSpec(memory_space=pl.ANY)],
        out_shape=[jax.ShapeDtypeStruct((S, nh * HEAD), BF16),
                   jax.ShapeDtypeStruct((nh, _sb_tri(n_i), t, t), BF16)],
        scratch_shapes=[pltpu.VMEM((2, t, t), BF16), pltpu.SemaphoreType.DMA((2,))],
        compiler_params=_params(("arbitrary", "arbitrary")),
    )(projb, projb, projb, after2)


def sb_backward(dcat, projb, w_all, nh, m0, t=512):
    S = projb.shape[0]
    _, before2 = _sb_constants()
    n_i = S // t
    nsub = t // SB_SUB

    def body(do_ref, q_ref, k_ref, v_ref, bf_ref, w_hbm, dq_ref, dk_ref, dv_ref, dk_acc, dv_acc, wbuf, wsem):
        h = pl.program_id(0)
        i = pl.program_id(1)

        @pl.when(i == 0)
        def _():
            dk_acc[...] = jnp.zeros_like(dk_acc)
            dv_acc[...] = jnp.zeros_like(dv_acc)

        q = q_ref[...]
        dob = do_ref[...].astype(BF16)
        before = bf_ref[...]
        base = _sb_tri(i)

        def load(slot, jb):
            return pltpu.make_async_copy(w_hbm.at[h, base + jb], wbuf.at[slot], wsem.at[slot])

        load(0, 0).start()

        def left_to_right(jb, run, dq, mask):
            slot = jb % 2
            load(slot, jb).wait()

            @pl.when(jb < i)
            def _():
                load(1 - slot, jb + 1).start()

            ksl = pl.ds(pl.multiple_of(jb * t, t), t)
            wb = wbuf[slot]
            z = _sb_scores(q, k_ref, jb * t, t)
            dw = lax.dot_general(dob, v_ref[ksl, :], (((1,), (1,)), ((), ())), preferred_element_type=F32)
            d = dw * wb.astype(F32)
            dv_acc[ksl, :] += lax.dot_general(wb, dob, (((0,), (0,)), ((), ())), preferred_element_type=F32)
            sig = 1.0 / (1.0 + jnp.exp2(-z))
            das = []
            for b in range(nsub):
                db = d[:, b * SB_SUB:(b + 1) * SB_SUB]
                prefix = run + jnp.dot(_split2(db), before, preferred_element_type=F32)
                das.append(db - sig[:, b * SB_SUB:(b + 1) * SB_SUB] * (db + prefix))
                run = prefix[:, SB_SUB - 1:SB_SUB] + db[:, SB_SUB - 1:SB_SUB]
            da = jnp.concatenate(das, axis=1)
            if mask is not None:
                da = jnp.where(mask, da, 0.0)
            dab = (da * SB_SCALE).astype(BF16)
            dq = dq + jnp.dot(dab, k_ref[ksl, :], preferred_element_type=F32)
            dk_acc[ksl, :] += lax.dot_general(dab, q, (((0,), (0,)), ((), ())), preferred_element_type=F32)
            return run, dq

        run, dq = lax.fori_loop(0, i, lambda jb, c: left_to_right(jb, c[0], c[1], None),
                                (jnp.zeros((t, 1), F32), jnp.zeros((t, HEAD), F32)))
        _, dq = left_to_right(i, run, dq, _sb_diag_mask(t))
        dq_ref[...] = dq.astype(BF16)

        @pl.when(i == n_i - 1)
        def _():
            dk_ref[...] = (dk_acc[...] * (1.0 / SB_QUERY_SCALE)).astype(BF16)
            dv_ref[...] = dv_acc[...].astype(BF16)

    half = nh * HEAD
    full = pl.BlockSpec((S, HEAD), lambda h, i: (0, h))
    return pl.pallas_call(
        body, grid=(nh, n_i), name="sb_bwd",
        in_specs=[pl.BlockSpec((t, HEAD), lambda h, i: (i, nh + h)),
                  pl.BlockSpec((t, HEAD), lambda h, i: (i, m0 + h)),
                  pl.BlockSpec((S, HEAD), lambda h, i: (0, m0 + nh + h)),
                  pl.BlockSpec((S, HEAD), lambda h, i: (0, m0 + 2 * nh + h)),
                  pl.BlockSpec(before2.shape, lambda h, i: (0, 0)), pl.BlockSpec(memory_space=pl.ANY)],
        out_specs=[pl.BlockSpec((t, HEAD), lambda h, i: (i, h)), full, full],
        out_shape=[jax.ShapeDtypeStruct((S, half), BF16)] * 3,
        scratch_shapes=[pltpu.VMEM((S, HEAD), F32), pltpu.VMEM((S, HEAD), F32),
                        pltpu.VMEM((2, t, t), BF16), pltpu.SemaphoreType.DMA((2,))],
        compiler_params=_params(("arbitrary", "arbitrary")),
    )(dcat, projb, projb, projb, before2, w_all)


def local_step(x, target, mix_norm, ffn_norm, final_norm, lb_logits, hg_norm, get_w_in, get_w_rest, send):
    S, D = x.shape
    half = D // 2
    nh = half // HEAD
    tm = 512
    tk = 2048
    row = lambda i, j: (i, 0)

    lb = jax.nn.softmax(lb_logits, axis=0)[0:1]

    h0, r0 = rms_fwd(x, mix_norm[0:1], BF16)
    w_in = get_w_in(h0)
    nbi = w_in.shape[2]
    col = jnp.arange(N_DEV * nbi) // half
    col_scale = jnp.where(col == 4, SB_QUERY_SCALE, 1.0).astype(F32)[None]
    proj, projb = matmul(
        "proj_in", [h0], [w_in], grid=(N_DEV, S // (2 * tm), 1),
        a_spec=pl.BlockSpec((2 * tm, D), lambda j, i, k: (i, 0)),
        b_spec=pl.BlockSpec((None, D, nbi), lambda j, i, k: (j, 0, 0)),
        out_spec=pl.BlockSpec((2 * tm, nbi), lambda j, i, k: (i, j)), out_shape=(S, N_DEV * nbi),
        out_dtypes=[F32, BF16], acc_shape=(8, 128),
        bf16_scale=col_scale, bf16_scale_spec=pl.BlockSpec((1, nbi), lambda j, i, k: (0, j)))
    oa, oraw, states = hgrn_forward(proj, lb, hg_norm)
    ob, sb_weights = sb_forward(projb, nh, 4 * nh)
    cat = jnp.concatenate([oa, ob], axis=1)
    w_out, pool_w, pool_scale, wg, wu, wd = get_w_rest(cat)
    (x1,) = matmul(
        "mix_out", [cat], [w_out], grid=(S // tm, 1),
        a_spec=pl.BlockSpec((tm, D), row), b_spec=pl.BlockSpec((D, D), lambda i, k: (0, 0)),
        out_spec=pl.BlockSpec((tm, D), row), out_shape=(S, D), out_dtypes=[F32], acc_shape=(8, 128),
        res=x, res_spec=pl.BlockSpec((tm, D), row))
    h1, r1 = rms_fwd(x1, ffn_norm[0:1], BF16)
    x2, ffn0 = ffn_forward(h1, x1, wg[0], wu[0], wd[0])

    h2, r2 = rms_fwd(x2, mix_norm[1:2], F32)
    x3, pooled = pool_forward(h2, x2, pool_w, pool_scale)
    h3, r3 = rms_fwd(x3, ffn_norm[1:2], BF16)
    x4, ffn1 = ffn_forward(h3, x3, wg[1], wu[1], wd[1])

    loss_blk, dx4, dx4b, d_final = loss_and_final_bwd(x4, final_norm, target)

    (dx3, _, d_ffn1), dwg1, dwu1, dwd1 = ffn_backward(dx4b, h3, ffn1, wg[1], wu[1], wd[1],
                                                      x3, r3, ffn_norm[1:2], dx4)
    dx3 = send("ffn1", dict(ffn_w_gate_1=dwg1, ffn_w_up_1=dwu1, ffn_w_down_1=dwd1), dx3)
    dmixed, dpooled, d_pscale = pool_backward_mix(dx3, pooled, pool_w, pool_scale)
    G = len(POOL_WINDOWS)
    P = D // G
    (d_pool_w,) = matmul(
        "pool_dw", [pooled], [dmixed], grid=(G, S // tk),
        a_spec=pl.BlockSpec((tk, P), lambda g, k: (k, g)), b_spec=pl.BlockSpec((tk, P), lambda g, k: (k, g)),
        out_spec=pl.BlockSpec((None, P, P), lambda g, k: (g, 0, 0)), out_shape=(G, P, P), out_dtypes=[BF16],
        acc_shape=(P, P), trans_a=True)
    dx2, dx2b, d_mix1 = pool_backward_window(dpooled, x2, r2, mix_norm[1:2], dx3)

    (dx1, dx1b, d_ffn0), dwg0, dwu0, dwd0 = ffn_backward(dx2b, h1, ffn0, wg[0], wu[0], wd[0],
                                                         x1, r1, ffn_norm[0:1], dx2)
    (dcat,) = matmul(
        "mix_out_dx", [dx1b], [w_out], grid=(S // tm, 1),
        a_spec=pl.BlockSpec((tm, D), row), b_spec=pl.BlockSpec((D, D), lambda i, k: (0, 0)),
        out_spec=pl.BlockSpec((tm, D), row), out_shape=(S, D), out_dtypes=[F32], acc_shape=(8, 128),
        trans_b=True)
    (d_w_out,) = matmul(
        "mix_out_dw", [cat], [dx1b], grid=(2, S // tk),
        a_spec=pl.BlockSpec((tk, half), lambda m, k: (k, m)), b_spec=pl.BlockSpec((tk, D), lambda m, k: (k, 0)),
        out_spec=pl.BlockSpec((half, D), lambda m, k: (m, 0)), out_shape=(D, D), out_dtypes=[BF16],
        acc_shape=(half, D), trans_a=True)
    dcat = send("layer0", dict(ffn_w_gate_0=dwg0, ffn_w_up_0=dwu0, ffn_w_down_0=dwd0, pool_w=d_pool_w,
                               ab_w_out=d_w_out), dcat)
    dqa, dfa, dia, dga, d_lb, d_hg = hgrn_backward(dcat, proj, oraw, states, lb, hg_norm)
    dqb, dkb, dvb = sb_backward(dcat, projb, sb_weights, nh, 4 * nh)
    dproj = jnp.concatenate([dqa, dfa, dia, dga, dqb, dkb, dvb], axis=1)
    (d_w_in,) = matmul(
        "proj_in_dw", [h0], [dproj], grid=(N_DEV, S // tk),
        a_spec=pl.BlockSpec((tk, D), lambda j, k: (k, 0)), b_spec=pl.BlockSpec((tk, nbi), lambda j, k: (k, j)),
        out_spec=pl.BlockSpec((None, D, nbi), lambda j, k: (j, 0, 0)), out_shape=(N_DEV, D, nbi),
        out_dtypes=[BF16], acc_shape=(D, nbi), trans_a=True)
    dproj = send("w_in", dict(ab_w_in=d_w_in), dproj)
    dx0, _, d_mix0 = matmul_rms_bwd(
        "proj_in_dx", [dproj], [w_in], grid=(S // tm, N_DEV),
        a_spec=pl.BlockSpec((tm, nbi), lambda i, j: (i, j)),
        b_spec=pl.BlockSpec((None, D, nbi), lambda i, j: (j, 0, 0)),
        tm=tm, x=x, r=r0, gain=mix_norm[0:1], dres=dx1)

    d_l0 = d_lb * lb * (1.0 - lb)
    small = dict(
        loss=loss_blk[0:1, 0:1],
        mix_norm=jnp.concatenate([d_mix0, d_mix1], axis=0),
        ffn_norm=jnp.concatenate([d_ffn0, d_ffn1], axis=0),
        final_norm=d_final,
        lb_logits=jnp.concatenate([d_l0, -d_l0], axis=0),
        hg_out_norm=jnp.sum(d_hg, axis=0),
        pool_scale=d_pscale,
    )
    return dx0, small


def _my_index():
    return 4 * lax.axis_index("x") + 2 * lax.axis_index("y") + lax.axis_index("c")


def _peer(r):
    x, y, c = lax.axis_index("x"), lax.axis_index("y"), lax.axis_index("c")
    px = 1 - x if (r >> 2) & 1 else x
    py = 1 - y if (r >> 1) & 1 else y
    pc = 1 - c if r & 1 else c
    return (px, py, pc), 4 * px + 2 * py + pc


def gather_two_level(name, shard):
    def body(x_ref, out_ref, send_sems, recv_sems, local_sem):
        x, y, c = lax.axis_index("x"), lax.axis_index("y"), lax.axis_index("c")
        me, sibling = (x, y, c), (x, y, 1 - c)
        chips = [(1 - x, y), (x, 1 - y), (1 - x, 1 - y)]

        def slot(px, py, pc):
            return out_ref.at[4 * px + 2 * py + pc]

        def copy(k, block, to, src=None):
            return pltpu.make_async_remote_copy(
                src_ref=slot(*block) if src is None else src, dst_ref=slot(*block), send_sem=send_sems.at[k],
                recv_sem=recv_sems.at[k], device_id=to, device_id_type=MESH)

        mine = pltpu.make_async_copy(x_ref, slot(*me), local_sem)
        mine.start()
        first = [copy(0, me, sibling, src=x_ref)]
        first += [copy(1 + j, me, (*chip, c), src=x_ref) for j, chip in enumerate(chips)]
        for cp in first:
            cp.start()
        passed = [copy(4 + j, (*chip, c), sibling) for j, chip in enumerate(chips)]
        for j, chip in enumerate(chips):
            copy(1 + j, (*chip, c), me).wait_recv()
            passed[j].start()
        copy(0, sibling, me).wait_recv()
        for j, chip in enumerate(chips):
            copy(4 + j, (*chip, 1 - c), me).wait_recv()
        for cp in first + passed:
            cp.wait_send()
        mine.wait()

    any_spec = pl.BlockSpec(memory_space=pl.ANY)
    return pl.pallas_call(
        body, name=name, in_specs=[any_spec], out_specs=any_spec,
        out_shape=jax.ShapeDtypeStruct((N_DEV,) + shard.shape, shard.dtype),
        scratch_shapes=[pltpu.SemaphoreType.DMA((N_DEV - 1,)), pltpu.SemaphoreType.DMA((N_DEV - 1,)),
                        pltpu.SemaphoreType.DMA],
    )(shard)


def exchange(name, arrays, gather):
    n = len(arrays)
    n_peers = N_DEV - 1

    def body(*refs):
        ins, outs = refs[:n], refs[n:2 * n]
        send_sems, recv_sems, local_sems = refs[2 * n:]
        me = _my_index()
        local = []
        for a in range(n):
            src = ins[a] if gather else ins[a].at[me]
            cp = pltpu.make_async_copy(src, outs[a].at[me], local_sems.at[a])
            cp.start()
            local.append(cp)
        remote = []
        for a in range(n):
            for r in range(1, N_DEV):
                peer, pidx = _peer(r)
                src = ins[a] if gather else ins[a].at[pidx]
                cp = pltpu.make_async_remote_copy(
                    src_ref=src, dst_ref=outs[a].at[me], send_sem=send_sems.at[a * n_peers + r - 1],
                    recv_sem=recv_sems.at[a * n_peers + r - 1], device_id=peer, device_id_type=MESH)
                cp.start()
                remote.append((cp, a, r))
        for cp, a, r in remote:
            _, pidx = _peer(r)
            src = ins[a] if gather else ins[a].at[pidx]
            pltpu.make_async_remote_copy(
                src_ref=src, dst_ref=outs[a].at[pidx], send_sem=send_sems.at[a * n_peers + r - 1],
                recv_sem=recv_sems.at[a * n_peers + r - 1], device_id=_peer(r)[0], device_id_type=MESH).wait_recv()
        for cp, a, r in remote:
            cp.wait_send()
        for cp in local:
            cp.wait()

    out_shape = [jax.ShapeDtypeStruct(((N_DEV,) + a.shape) if gather else a.shape, a.dtype) for a in arrays]
    any_spec = pl.BlockSpec(memory_space=pl.ANY)
    return pl.pallas_call(
        body, name=name, in_specs=[any_spec] * n, out_specs=[any_spec] * n, out_shape=out_shape,
        scratch_shapes=[pltpu.SemaphoreType.DMA((n * n_peers,)), pltpu.SemaphoreType.DMA((n * n_peers,)),
                        pltpu.SemaphoreType.DMA((n,))],
    )(*arrays)


_HBM = pl.BlockSpec(memory_space=pltpu.HBM)
_SEM = pl.BlockSpec(memory_space=pltpu.SEMAPHORE)
_EFFECT = pltpu.SideEffectType.DATAFLOW_SIDE_EFFECTING


def _landing(arrays, gather):
    me = _my_index()
    lands = []
    for a in arrays:
        own = a[None] if gather else lax.dynamic_slice_in_dim(a, me, 1, axis=0)
        shape = ((N_DEV,) + a.shape) if gather else a.shape
        lands.append(lax.dynamic_update_slice_in_dim(lax.empty(shape, a.dtype), own, me, axis=0))
    return lands


def exchange_start(name, arrays, gather, carry):
    n = len(arrays)
    n_peers = N_DEV - 1
    lands = _landing(arrays, gather)
    n_thru = 2 * n + 1

    def body(*refs):
        src, land = refs[:n], refs[n:2 * n]
        send_sems, recv_sems = refs[n_thru], refs[n_thru + 1]
        token = refs[-1]
        me = _my_index()
        for a in range(n):
            for r in range(1, N_DEV):
                peer, pidx = _peer(r)
                pltpu.make_async_remote_copy(
                    src_ref=src[a] if gather else src[a].at[pidx], dst_ref=land[a].at[me],
                    send_sem=send_sems.at[a * n_peers + r - 1], recv_sem=recv_sems.at[a * n_peers + r - 1],
                    device_id=peer, device_id_type=MESH).start()
        token[...] = jnp.zeros_like(token)

    operands = list(arrays) + lands + [carry]
    outs = pl.pallas_call(
        body, name=name,
        out_shape=(pltpu.SemaphoreType.DMA((n * n_peers,)), pltpu.SemaphoreType.DMA((n * n_peers,)),
                   *[pltpu.HBM(a.shape, a.dtype) for a in operands], jax.ShapeDtypeStruct((8, 128), F32)),
        in_specs=[_HBM] * n_thru,
        out_specs=(_SEM, _SEM, *([_HBM] * n_thru), pl.BlockSpec(memory_space=pltpu.VMEM)),
        input_output_aliases={i: 2 + i for i in range(n_thru)},
        compiler_params=pltpu.CompilerParams(has_side_effects=_EFFECT),
    )(*[pltpu.with_memory_space_constraint(a, pltpu.HBM) for a in operands])
    handle = (outs[0], outs[1], list(outs[2:2 + n]), list(outs[2 + n:2 + 2 * n]), gather)
    return handle, outs[2 + 2 * n]


def exchange_wait(name, handle, after):
    send_sems, recv_sems, srcs, lands, gather = handle
    n = len(srcs)
    n_peers = N_DEV - 1

    def body(*refs):
        src, land = refs[:n], refs[n:2 * n]
        send_s, recv_s = refs[2 * n], refs[2 * n + 1]
        for a in range(n):
            for r in range(1, N_DEV):
                peer, pidx = _peer(r)
                cp = pltpu.make_async_remote_copy(
                    src_ref=src[a] if gather else src[a].at[pidx], dst_ref=land[a].at[pidx],
                    send_sem=send_s.at[a * n_peers + r - 1], recv_sem=recv_s.at[a * n_peers + r - 1],
                    device_id=peer, device_id_type=MESH)
                cp.wait_send()
                cp.wait_recv()

    shapes = [pltpu.HBM(a.shape, a.dtype) for a in srcs] + [pltpu.HBM(l.shape, l.dtype) for l in lands]
    outs = pl.pallas_call(
        body, name=name, out_shape=tuple(shapes),
        in_specs=[_HBM] * (2 * n) + [_SEM, _SEM, pl.BlockSpec(memory_space=pl.ANY)],
        out_specs=tuple([_HBM] * (2 * n)),
        input_output_aliases={i: i for i in range(2 * n)},
        compiler_params=pltpu.CompilerParams(has_side_effects=_EFFECT),
    )(*srcs, *lands, send_sems, recv_sems, after)
    return list(outs[n:])


def _row_tile(rows, cap=256):
    best = None
    for t in range(16, min(rows, cap) + 1, 16):
        if rows % t == 0:
            best = t
    return best if best is not None else rows


def sum_slots(name, recv):
    n, R, C = recv.shape
    tr = _row_tile(R)

    def body(r_ref, o_ref):
        g = r_ref[0].astype(F32)
        for d in range(1, n):
            g = g + r_ref[d].astype(F32)
        o_ref[...] = g

    return pl.pallas_call(
        body, grid=(R // tr,), name=name,
        in_specs=[pl.BlockSpec((n, tr, C), lambda i: (0, i, 0))],
        out_specs=pl.BlockSpec((tr, C), lambda i: (i, 0)),
        out_shape=jax.ShapeDtypeStruct((R, C), F32),
        compiler_params=_params(("arbitrary",)),
    )(recv)


def adamw(name, recv, w, m, v, layer=None, prev=None):
    n, R, C = recv.shape
    tr = _row_tile(R)

    def body(r_ref, w_ref, m_ref, v_ref, *rest):
        g_ref, d_ref, nm_ref, nv_ref = rest[-4:]
        g = r_ref[0].astype(F32)
        for d in range(1, n):
            g = g + r_ref[d].astype(F32)
        mm = ADAM_B1 * m_ref[...] + (1.0 - ADAM_B1) * g
        vv = ADAM_B2 * v_ref[...] + (1.0 - ADAM_B2) * (g * g)
        m_hat = mm / (1.0 - ADAM_B1 ** ADAM_STEP)
        v_hat = vv / (1.0 - ADAM_B2 ** ADAM_STEP)
        g_ref[...] = g
        d_ref[...] = -ADAM_LR * (m_hat / (jnp.sqrt(v_hat) + ADAM_EPS) + ADAM_WD * w_ref[...])
        nm_ref[...] = mm
        nv_ref[...] = vv

    if layer is None:
        row = pl.BlockSpec((tr, C), lambda i: (i, 0))
        shape = (R, C)
    else:
        row = pl.BlockSpec((None, tr, C), lambda i: (layer, i, 0))
        shape = w.shape
    prev = [] if prev is None else list(prev)
    return pl.pallas_call(
        body, grid=(R // tr,), name=name,
        in_specs=[pl.BlockSpec((n, tr, C), lambda i: (0, i, 0)), row, row, row]
                 + [pl.BlockSpec(memory_space=pl.ANY)] * len(prev),
        out_specs=[row] * 4,
        out_shape=[jax.ShapeDtypeStruct(shape, F32)] * 4,
        input_output_aliases={4 + o: o for o in range(len(prev))},
        compiler_params=_params(("arbitrary",)),
    )(recv, w, m, v, *prev)


def _adamw_nd(name, recv, w, m, v):
    shp = w.shape
    C = shp[-1]
    flat = lambda a: a.reshape(-1, C)
    outs = adamw(name, recv.reshape(recv.shape[0], -1, C), flat(w), flat(m), flat(v))
    return [o.reshape(shp) for o in outs]


_SMALL_NAMES = ("loss", "mix_norm", "ffn_norm", "final_norm", "lb_logits", "hg_out_norm", "pool_scale")
_LANES = 128


def _pack_small(parts):
    rows, layout = [], {}
    at = 0
    for name in parts:
        flat = parts[name].reshape(-1).astype(F32)
        n_rows = -(-flat.shape[0] // (8 * _LANES)) * 8
        flat = jnp.pad(flat, (0, n_rows * _LANES - flat.shape[0]))
        rows.append(flat.reshape(n_rows, _LANES))
        layout[name] = (at, parts[name].shape)
        at += n_rows
    return jnp.concatenate(rows, axis=0), layout


def _unpack_small(pack, layout):
    out = {}
    for name, (at, shape) in layout.items():
        size = int(np.prod(shape))
        n_rows = -(-size // _LANES)
        out[name] = pack[at:at + n_rows].reshape(-1)[:size].reshape(shape)
    return out


def kernel(x, mix_norm, ffn_norm, final_norm, ab_w_in, lb_logits, hg_out_norm, ab_w_out, pool_w, pool_scale, ffn_w_gate, ffn_w_up, ffn_w_down, loss_target, m_mix_norm, m_ffn_norm, m_final_norm, m_ab_w_in, m_lb_logits, m_hg_out_norm, m_ab_w_out, m_pool_w, m_pool_scale, m_ffn_w_gate, m_ffn_w_up, m_ffn_w_down, v_mix_norm, v_ffn_norm, v_final_norm, v_ab_w_in, v_lb_logits, v_hg_out_norm, v_ab_w_out, v_pool_w, v_pool_scale, v_ffn_w_gate, v_ffn_w_up, v_ffn_w_down):
    D = x.shape[-1]
    n_layers = ffn_w_gate.shape[0]
    G = pool_w.shape[1]
    P = pool_w.shape[3]
    me = _my_index()

    rest = [ab_w_out[0], pool_w[0]]
    for l in range(n_layers):
        rest += [ffn_w_gate[l], ffn_w_up[l], ffn_w_down[l]]
    rest = [s.astype(BF16) for s in rest] + [pool_scale]
    rest_handle = []

    def get_w_in(after):
        w_in = gather_two_level("gather_w_in", ab_w_in[0].astype(BF16))
        handle, w_in = exchange_start("gather_rest_start", rest, True, w_in)
        rest_handle.append(handle)
        return w_in

    def get_w_rest(after):
        got = exchange_wait("gather_rest_wait", rest_handle[0], after)
        w_out_g = got[0].reshape(D, D)
        pool_g = got[1].transpose(1, 0, 2, 3).reshape(G, P, P)
        wg = [got[2 + 3 * l] for l in range(n_layers)]
        wu = [got[3 + 3 * l] for l in range(n_layers)]
        wd = [got[4 + 3 * l] for l in range(n_layers)]
        return w_out_g, pool_g, got[-1].reshape(1, D), wg, wu, wd

    in_flight = []

    def send(tag, grads, carry):
        if "pool_w" in grads:
            grads = dict(grads, pool_w=grads["pool_w"].reshape(G, N_DEV, P // N_DEV, P).transpose(1, 0, 2, 3))
        if "ab_w_out" in grads:
            grads = dict(grads, ab_w_out=grads["ab_w_out"].reshape(N_DEV, D // N_DEV, D))
        handle, carry = exchange_start("grads_" + tag + "_start", list(grads.values()), False, carry)
        in_flight.append((tag, list(grads.keys()), handle))
        return carry

    dx0, small = local_step(x[0], loss_target[0], mix_norm, ffn_norm, final_norm[None],
                            lb_logits, hg_out_norm, get_w_in, get_w_rest, send)

    recv = {}
    for tag, names, handle in in_flight:
        recv.update(zip(names, exchange_wait("grads_" + tag + "_wait", handle, dx0)))
    small_pack, layout = _pack_small({k: small[k] for k in _SMALL_NAMES})
    (small_all,) = exchange("gather_small", [small_pack], gather=True)
    tot = _unpack_small(sum_slots("sum_small", small_all), layout)

    res = {}
    res["ab_w_in"] = _adamw_nd("adamw_w_in", recv["ab_w_in"], ab_w_in, m_ab_w_in, v_ab_w_in)
    res["ab_w_out"] = _adamw_nd("adamw_w_out", recv["ab_w_out"], ab_w_out, m_ab_w_out, v_ab_w_out)
    res["pool_w"] = _adamw_nd("adamw_pool_w", recv["pool_w"], pool_w, m_pool_w, v_pool_w)
    ffn_in = {"ffn_w_gate": (ffn_w_gate, m_ffn_w_gate, v_ffn_w_gate),
              "ffn_w_up": (ffn_w_up, m_ffn_w_up, v_ffn_w_up),
              "ffn_w_down": (ffn_w_down, m_ffn_w_down, v_ffn_w_down)}
    for name, (w, m, v) in ffn_in.items():
        flip = name != "ffn_w_down"
        if flip:
            w, m, v = (jnp.swapaxes(a, 1, 2) for a in (w, m, v))
        outs = None
        for l in range(n_layers):
            outs = adamw("adamw_" + name, recv[name + "_" + str(l)], w, m, v, layer=l, prev=outs)
        res[name] = [jnp.swapaxes(o, 1, 2) for o in outs] if flip else outs

    n_ps = pool_scale.shape[1]
    small_g = dict(tot)
    small_g["pool_scale"] = lax.dynamic_slice(tot["pool_scale"], (0, me * n_ps), (1, n_ps))
    small_w = dict(mix_norm=(mix_norm, m_mix_norm, v_mix_norm), ffn_norm=(ffn_norm, m_ffn_norm, v_ffn_norm),
                   final_norm=(final_norm, m_final_norm, v_final_norm),
                   lb_logits=(lb_logits, m_lb_logits, v_lb_logits),
                   hg_out_norm=(hg_out_norm, m_hg_out_norm, v_hg_out_norm),
                   pool_scale=(pool_scale, m_pool_scale, v_pool_scale))
    g_pack, lay2 = _pack_small({k: small_g[k].reshape(small_w[k][0].shape) for k in small_w})
    w_pack, _ = _pack_small({k: small_w[k][0] for k in small_w})
    m_pack, _ = _pack_small({k: small_w[k][1] for k in small_w})
    v_pack, _ = _pack_small({k: small_w[k][2] for k in small_w})
    small_out = [_unpack_small(o, lay2) for o in adamw("adamw_small", g_pack[None], w_pack, m_pack, v_pack)]
    for k in small_w:
        res[k] = [small_out[o][k] for o in range(4)]

    order = ("mix_norm", "ffn_norm", "final_norm", "ab_w_in", "lb_logits", "hg_out_norm", "ab_w_out", "pool_w",
             "pool_scale", "ffn_w_gate", "ffn_w_up", "ffn_w_down")
    outs = [tot["loss"].reshape(()), dx0[None]]
    for o in range(4):
        outs += [res[k][o] for k in order]
    return tuple(outs)
```

```python
import math

import numpy as np
import jax
import jax.numpy as jnp
from jax import lax
from jax.experimental import pallas as pl
from jax.experimental.pallas import tpu as pltpu

F32 = jnp.float32
BF16 = jnp.bfloat16

N_DEV = 8
RMS_EPS = 1e-6
HEAD = 128
HG_CHUNK = 64
HG_HEADS_PER_BLOCK = 8
POOL_WINDOWS = (2, 4, 8, 16)
POOL_HALO = 16
ADAM_LR, ADAM_B1, ADAM_B2, ADAM_EPS, ADAM_WD, ADAM_STEP = 0.001, 0.9, 0.999, 1e-08, 0.01, 10
VMEM_LIMIT_BYTES = 60 * 1024 * 1024
MESH = pl.DeviceIdType.MESH

ROW_TILE = 512
ROW_TILE_WIDE = 1024
REDUCE_TILE = 2048
POOL_TILE = 512
HG_TILE = 512
SB_TILE = 512


def _params(sem):
    return pltpu.CompilerParams(dimension_semantics=sem, vmem_limit_bytes=VMEM_LIMIT_BYTES)


def _sigmoid(x):
    return 1.0 / (1.0 + jnp.exp(-x))


def rms_fwd(x, gain, out_dtype, ts=ROW_TILE):
    S, D = x.shape

    def body(x_ref, g_ref, h_ref, r_ref):
        xv = x_ref[...]
        r = lax.rsqrt(jnp.mean(xv * xv, axis=-1, keepdims=True) + RMS_EPS)
        h_ref[...] = ((xv * r) * g_ref[...]).astype(h_ref.dtype)
        r_ref[...] = r

    return pl.pallas_call(
        body, grid=(S // ts,), name="rms_fwd",
        in_specs=[pl.BlockSpec((ts, D), lambda i: (i, 0)), pl.BlockSpec((1, D), lambda i: (0, 0))],
        out_specs=[pl.BlockSpec((ts, D), lambda i: (i, 0)), pl.BlockSpec((ts, 1), lambda i: (i, 0))],
        out_shape=[jax.ShapeDtypeStruct((S, D), out_dtype), jax.ShapeDtypeStruct((S, 1), F32)],
        compiler_params=_params(("arbitrary",)),
    )(x, gain)


RMS_BWD_ROWS = 128


def _rms_bwd_tile(first, dh_of, x_ref, r_ref, g_ref, dres_ref, dx_ref, dxb_ref, dg_ref, rows):
    gv = g_ref[...]
    part = None
    for c in range(rows // RMS_BWD_ROWS):
        sl = slice(c * RMS_BWD_ROWS, (c + 1) * RMS_BWD_ROWS)
        rr = r_ref[sl, :]
        xh = x_ref[sl, :] * rr
        dhv = dh_of(sl)
        dxh = dhv * gv
        dx = dres_ref[sl, :] + rr * (dxh - xh * jnp.mean(dxh * xh, axis=-1, keepdims=True))
        dx_ref[sl, :] = dx
        dxb_ref[sl, :] = dx.astype(BF16)
        p = jnp.sum(dhv * xh, axis=0, keepdims=True)
        part = p if part is None else part + p

    @pl.when(first)
    def _():
        dg_ref[...] = part

    @pl.when(jnp.logical_not(first))
    def _():
        dg_ref[...] += part


def matmul_rms_bwd(name, a_ops, b_ops, *, grid, a_spec, b_spec, tm, x, r, gain, dres):
    S, D = x.shape
    n_pairs = len(a_ops)
    nk = grid[1]
    dn = (((1,), (1,)), ((), ()))

    def body(*refs):
        a_refs = refs[:n_pairs]
        b_refs = refs[n_pairs:2 * n_pairs]
        x_ref, r_ref, g_ref, dres_ref, dx_ref, dxb_ref, dg_ref, acc_ref = refs[2 * n_pairs:]
        i = pl.program_id(0)
        k = pl.program_id(1)

        @pl.when(k == 0)
        def _():
            acc_ref[...] = jnp.zeros_like(acc_ref)

        part = None
        for ar, br in zip(a_refs, b_refs):
            d = lax.dot_general(ar[...], br[...], dn, preferred_element_type=F32)
            part = d if part is None else part + d
        acc_ref[...] += part

        @pl.when(k == nk - 1)
        def _():
            _rms_bwd_tile(i == 0, lambda sl: acc_ref[sl, :], x_ref, r_ref, g_ref, dres_ref, dx_ref, dxb_ref,
                          dg_ref, tm)

    row = pl.BlockSpec((tm, D), lambda i, k: (i, 0))
    vec = pl.BlockSpec((1, D), lambda i, k: (0, 0))
    return pl.pallas_call(
        body, grid=grid, name=name,
        in_specs=[a_spec] * n_pairs + [b_spec] * n_pairs
                 + [row, pl.BlockSpec((tm, 1), lambda i, k: (i, 0)), vec, row],
        out_specs=[row, row, vec],
        out_shape=[jax.ShapeDtypeStruct((S, D), F32), jax.ShapeDtypeStruct((S, D), BF16),
                   jax.ShapeDtypeStruct((1, D), F32)],
        scratch_shapes=[pltpu.VMEM((tm, D), F32)],
        compiler_params=_params(("arbitrary", "arbitrary")),
    )(*a_ops, *b_ops, x, r, gain, dres)


def loss_and_final_bwd(x, gain, target, ts=ROW_TILE):
    S, D = x.shape

    def body(x_ref, g_ref, t_ref, loss_ref, dx_ref, dxb_ref, dg_ref):
        i = pl.program_id(0)
        xv = x_ref[...]
        rr = lax.rsqrt(jnp.mean(xv * xv, axis=-1, keepdims=True) + RMS_EPS)
        xh = xv * rr
        err = xh * g_ref[...] - t_ref[...]
        part_loss = 0.5 * jnp.sum(jnp.mean(err * err, axis=-1, keepdims=True))
        dy = err / D
        dxh = dy * g_ref[...]
        dx = rr * (dxh - xh * jnp.mean(dxh * xh, axis=-1, keepdims=True))
        dx_ref[...] = dx
        dxb_ref[...] = dx.astype(BF16)
        part = jnp.sum(dy * xh, axis=0, keepdims=True)

        @pl.when(i == 0)
        def _():
            dg_ref[...] = part
            loss_ref[...] = jnp.zeros_like(loss_ref) + part_loss

        @pl.when(i > 0)
        def _():
            dg_ref[...] += part
            loss_ref[...] += part_loss

    row = pl.BlockSpec((ts, D), lambda i: (i, 0))
    vec = pl.BlockSpec((1, D), lambda i: (0, 0))
    return pl.pallas_call(
        body, grid=(S // ts,), name="loss_final",
        in_specs=[row, vec, row],
        out_specs=[pl.BlockSpec((8, 128), lambda i: (0, 0)), row, row, vec],
        out_shape=[jax.ShapeDtypeStruct((8, 128), F32), jax.ShapeDtypeStruct((S, D), F32),
                   jax.ShapeDtypeStruct((S, D), BF16), jax.ShapeDtypeStruct((1, D), F32)],
        compiler_params=_params(("arbitrary",)),
    )(x, gain, target)


def matmul(name, a_ops, b_ops, *, grid, a_spec, b_spec, out_spec, out_shape, out_dtypes, acc_shape,
           trans_a=False, trans_b=False, res=None, res_spec=None, bf16_scale=None, bf16_scale_spec=None):
    n_pairs = len(a_ops)
    n_out = len(out_dtypes)
    nk = grid[-1]
    kaxis = len(grid) - 1
    dn = (((0,) if trans_a else (1,), (1,) if trans_b else (0,)), ((), ()))

    def body(*refs):
        a_refs = refs[:n_pairs]
        b_refs = refs[n_pairs:2 * n_pairs]
        pos = 2 * n_pairs
        res_ref = None
        if res is not None:
            res_ref = refs[pos]
            pos += 1
        scale_ref = None
        if bf16_scale is not None:
            scale_ref = refs[pos]
            pos += 1
        out_refs = refs[pos:pos + n_out]
        acc_ref = refs[pos + n_out]
        k = pl.program_id(kaxis)
        in_place = n_out == 1 and out_dtypes[0] == F32
        target = out_refs[0] if in_place else acc_ref

        def finish(val):
            if res_ref is not None:
                val = val + res_ref[...]
            for o in out_refs:
                if scale_ref is not None and o.dtype == BF16:
                    o[...] = (val * scale_ref[...]).astype(BF16)
                else:
                    o[...] = val.astype(o.dtype)

        if nk > 1:
            @pl.when(k == 0)
            def _():
                if in_place and res_ref is not None:
                    target[...] = res_ref[...]
                else:
                    target[...] = jnp.zeros_like(target)

        part = None
        for ar, br in zip(a_refs, b_refs):
            d = lax.dot_general(ar[...].astype(BF16), br[...].astype(BF16), dn, preferred_element_type=F32)
            part = d if part is None else part + d

        if nk == 1:
            finish(part)
        else:
            target[...] += part
            if not in_place:
                @pl.when(k == nk - 1)
                def _():
                    finish(acc_ref[...])

    in_specs = [a_spec] * n_pairs + [b_spec] * n_pairs
    operands = list(a_ops) + list(b_ops)
    if res is not None:
        in_specs.append(res_spec)
        operands.append(res)
    if bf16_scale is not None:
        in_specs.append(bf16_scale_spec)
        operands.append(bf16_scale)
    return pl.pallas_call(
        body, grid=grid, name=name, in_specs=in_specs,
        out_specs=[out_spec] * n_out,
        out_shape=[jax.ShapeDtypeStruct(out_shape, dt) for dt in out_dtypes],
        scratch_shapes=[pltpu.VMEM(acc_shape, F32)],
        compiler_params=_params(("arbitrary",) * len(grid)),
    )(*operands)


def ffn_gate_up(h, wg, wu, tm=ROW_TILE_WIDE):
    S, D = h.shape
    nb = wg.shape[2]

    def body(h_ref, wg_ref, wu_ref, p_ref, r_ref, a_ref):
        for c in range(2):
            rows = slice(c * (tm // 2), (c + 1) * (tm // 2))
            hv = h_ref[rows, :]
            g = jnp.dot(hv, wg_ref[...], preferred_element_type=F32)
            u = jnp.dot(hv, wu_ref[...], preferred_element_type=F32)
            s = _sigmoid(g)
            p = g * s
            p_ref[rows, :] = p
            r_ref[rows, :] = u * (s * (1.0 + g * (1.0 - s)))
            a_ref[rows, :] = (p * u).astype(BF16)

    wspec = pl.BlockSpec((None, D, nb), lambda j, i: (j, 0, 0))
    ospec = pl.BlockSpec((None, tm, nb), lambda j, i: (j, i, 0))
    return pl.pallas_call(
        body, grid=(N_DEV, S // tm), name="ffn_gate_up",
        in_specs=[pl.BlockSpec((tm, D), lambda j, i: (i, 0)), wspec, wspec],
        out_specs=[ospec, ospec, ospec],
        out_shape=[jax.ShapeDtypeStruct((N_DEV, S, nb), F32), jax.ShapeDtypeStruct((N_DEV, S, nb), F32),
                   jax.ShapeDtypeStruct((N_DEV, S, nb), BF16)],
        compiler_params=_params(("arbitrary", "arbitrary")),
    )(h, wg, wu)


def ffn_bwd_hidden(dy, wd, p, r, tm=ROW_TILE_WIDE):
    S, D = dy.shape
    nb = wd.shape[1]

    def body(dy_ref, wd_ref, p_ref, r_ref, dg_ref, du_ref):
        for c in range(2):
            rows = slice(c * (tm // 2), (c + 1) * (tm // 2))
            da = lax.dot_general(dy_ref[rows, :], wd_ref[...], (((1,), (1,)), ((), ())),
                                 preferred_element_type=F32)
            du_ref[rows, :] = (da * p_ref[rows, :]).astype(BF16)
            dg_ref[rows, :] = (da * r_ref[rows, :]).astype(BF16)

    hspec = pl.BlockSpec((None, tm, nb), lambda j, i: (j, i, 0))
    return pl.pallas_call(
        body, grid=(N_DEV, S // tm), name="ffn_bwd_hidden",
        in_specs=[pl.BlockSpec((tm, D), lambda j, i: (i, 0)), pl.BlockSpec((None, nb, D), lambda j, i: (j, 0, 0)),
                  hspec, hspec],
        out_specs=[hspec, hspec],
        out_shape=[jax.ShapeDtypeStruct((N_DEV, S, nb), BF16), jax.ShapeDtypeStruct((N_DEV, S, nb), BF16)],
        compiler_params=_params(("arbitrary", "arbitrary")),
    )(dy, wd, p, r)


def ffn_forward(h, xres, wg, wu, wd, tm=ROW_TILE_WIDE):
    S, D = h.shape
    nb = wg.shape[2]
    g, u, a = ffn_gate_up(h, wg, wu)
    (xo,) = matmul(
        "ffn_down", [a], [wd], grid=(S // tm, N_DEV),
        a_spec=pl.BlockSpec((None, tm, nb), lambda i, j: (j, i, 0)),
        b_spec=pl.BlockSpec((None, nb, D), lambda i, j: (j, 0, 0)),
        out_spec=pl.BlockSpec((tm, D), lambda i, j: (i, 0)), out_shape=(S, D), out_dtypes=[F32],
        acc_shape=(tm, D), res=xres, res_spec=pl.BlockSpec((tm, D), lambda i, j: (i, 0)))
    return xo, (g, u, a)


def ffn_backward(dy_b, h, saved, wg, wu, wd, x, r, gain, dres, tm=ROW_TILE, tk=REDUCE_TILE):
    S, D = h.shape
    nb = wg.shape[2]
    g, u, a = saved
    dg, du = ffn_bwd_hidden(dy_b, wd, g, u)
    dx = matmul_rms_bwd(
        "ffn_dh", [dg, du], [wg, wu], grid=(S // tm, N_DEV),
        a_spec=pl.BlockSpec((None, tm, nb), lambda i, j: (j, i, 0)),
        b_spec=pl.BlockSpec((None, D, nb), lambda i, j: (j, 0, 0)),
        tm=tm, x=x, r=r, gain=gain, dres=dres)

    def wgrad_in(name, dhid):
        (dw,) = matmul(
            name, [dhid], [h], grid=(N_DEV, S // tk),
            a_spec=pl.BlockSpec((None, tk, nb), lambda j, k: (j, k, 0)),
            b_spec=pl.BlockSpec((tk, D), lambda j, k: (k, 0)),
            out_spec=pl.BlockSpec((None, nb, D), lambda j, k: (j, 0, 0)), out_shape=(N_DEV, nb, D),
            out_dtypes=[BF16], acc_shape=(nb, D), trans_a=True)
        return dw

    dwg = wgrad_in("ffn_dwg", dg)
    dwu = wgrad_in("ffn_dwu", du)
    (dwd,) = matmul(
        "ffn_dwd", [a], [dy_b], grid=(N_DEV, S // tk),
        a_spec=pl.BlockSpec((None, tk, nb), lambda j, k: (j, k, 0)),
        b_spec=pl.BlockSpec((tk, D), lambda j, k: (k, 0)),
        out_spec=pl.BlockSpec((None, nb, D), lambda j, k: (j, 0, 0)), out_shape=(N_DEV, nb, D),
        out_dtypes=[BF16], acc_shape=(nb, D), trans_a=True)
    return dx, dwg, dwu, dwd


def _pool_counts(row0, n, w):
    pos = row0 + lax.broadcasted_iota(jnp.int32, (n, 1), 0)
    return jnp.minimum(pos + 1, w).astype(F32)


def pool_forward(h, xres, w, scale, ts=POOL_TILE):
    S, D = h.shape
    G = len(POOL_WINDOWS)
    P = D // G
    hb = ts // POOL_HALO

    def body(h_ref, halo_ref, x_ref, w_ref, s_ref, xo_ref, p_ref):
        i = pl.program_id(0)
        for gi, win in enumerate(POOL_WINDOWS):
            cols = slice(gi * P, (gi + 1) * P)
            cur = h_ref[:, cols]
            halo = jnp.where(i > 0, halo_ref[:, cols], 0.0)
            acc = jnp.concatenate([halo, cur], axis=0)
            step = 1
            while step < win:
                acc = acc + pltpu.roll(acc, step, 0)
                step *= 2
            wsum = acc[POOL_HALO:, :]
            pooled = wsum / _pool_counts(i * ts, ts, win) - cur
            pb = pooled.astype(BF16)
            p_ref[:, cols] = pb
            mixed = jnp.dot(pb, w_ref[gi], preferred_element_type=F32)
            xo_ref[:, cols] = x_ref[:, cols] + mixed * s_ref[:, cols]

    row = pl.BlockSpec((ts, D), lambda i: (i, 0))
    return pl.pallas_call(
        body, grid=(S // ts,), name="pool_fwd",
        in_specs=[row, pl.BlockSpec((POOL_HALO, D), lambda i: (jnp.maximum(i * hb - 1, 0), 0)), row,
                  pl.BlockSpec((G, P, P), lambda i: (0, 0, 0)), pl.BlockSpec((1, D), lambda i: (0, 0))],
        out_specs=[row, row],
        out_shape=[jax.ShapeDtypeStruct((S, D), F32), jax.ShapeDtypeStruct((S, D), BF16)],
        compiler_params=_params(("arbitrary",)),
    )(h, h, xres, w, scale)


def pool_backward_mix(dx, pooled, w, scale, ts=POOL_TILE):
    S, D = dx.shape
    G = len(POOL_WINDOWS)
    P = D // G

    def body(dx_ref, p_ref, w_ref, s_ref, dm_ref, dp_ref, ds_ref):
        i = pl.program_id(0)
        parts = []
        for gi in range(G):
            cols = slice(gi * P, (gi + 1) * P)
            dxv = dx_ref[:, cols]
            dmb = (dxv * s_ref[:, cols]).astype(BF16)
            dm_ref[:, cols] = dmb
            dp_ref[:, cols] = lax.dot_general(dmb, w_ref[gi], (((1,), (1,)), ((), ())),
                                              preferred_element_type=F32)
            mixed = jnp.dot(p_ref[:, cols], w_ref[gi], preferred_element_type=F32)
            parts.append(jnp.sum(dxv * mixed, axis=0, keepdims=True))
        part = jnp.concatenate(parts, axis=1)

        @pl.when(i == 0)
        def _():
            ds_ref[...] = part

        @pl.when(i > 0)
        def _():
            ds_ref[...] += part

    row = pl.BlockSpec((ts, D), lambda i: (i, 0))
    vec = pl.BlockSpec((1, D), lambda i: (0, 0))
    return pl.pallas_call(
        body, grid=(S // ts,), name="pool_bwd_mix",
        in_specs=[row, row, pl.BlockSpec((G, P, P), lambda i: (0, 0, 0)), vec],
        out_specs=[row, row, vec],
        out_shape=[jax.ShapeDtypeStruct((S, D), BF16), jax.ShapeDtypeStruct((S, D), F32),
                   jax.ShapeDtypeStruct((1, D), F32)],
        compiler_params=_params(("arbitrary",)),
    )(dx, pooled, w, scale)


def pool_backward_window(dp, x, r, gain, dres, ts=POOL_TILE):
    S, D = dp.shape
    G = len(POOL_WINDOWS)
    P = D // G
    hb = ts // POOL_HALO
    n_i = S // ts
    n_rows = ts + POOL_HALO

    def body(dp_ref, halo_ref, x_ref, r_ref, g_ref, dres_ref, dx_ref, dxb_ref, dg_ref, dh_ref):
        i = pl.program_id(0)
        for gi, win in enumerate(POOL_WINDOWS):
            cols = slice(gi * P, (gi + 1) * P)
            cur = dp_ref[:, cols]
            halo = jnp.where(i < n_i - 1, halo_ref[:, cols], 0.0)
            acc = jnp.concatenate([cur / _pool_counts(i * ts, ts, win),
                                   halo / _pool_counts((i + 1) * ts, POOL_HALO, win)], axis=0)
            step = 1
            while step < win:
                acc = acc + pltpu.roll(acc, n_rows - step, 0)
                step *= 2
            dh_ref[:, cols] = acc[:ts, :] - cur
        _rms_bwd_tile(i == 0, lambda sl: dh_ref[sl, :], x_ref, r_ref, g_ref, dres_ref, dx_ref, dxb_ref, dg_ref, ts)

    row = pl.BlockSpec((ts, D), lambda i: (i, 0))
    vec = pl.BlockSpec((1, D), lambda i: (0, 0))
    return pl.pallas_call(
        body, grid=(n_i,), name="pool_bwd_window",
        in_specs=[row, pl.BlockSpec((POOL_HALO, D), lambda i: (jnp.minimum((i + 1) * hb, S // POOL_HALO - 1), 0)),
                  row, pl.BlockSpec((ts, 1), lambda i: (i, 0)), vec, row],
        out_specs=[row, row, vec],
        out_shape=[jax.ShapeDtypeStruct((S, D), F32), jax.ShapeDtypeStruct((S, D), BF16),
                   jax.ShapeDtypeStruct((1, D), F32)],
        scratch_shapes=[pltpu.VMEM((ts, D), F32)],
        compiler_params=_params(("arbitrary",)),
    )(dp, dp, x, r, gain, dres)


_HG_LEVELS = (32, 16, 8, 4, 2, 1)
_N_LEV = len(_HG_LEVELS) + 1


def _hgrn_constants():
    C = HG_CHUNK
    t = np.arange(C)
    tri = (t[None, :] <= t[:, None]).astype(np.float32)
    blocks = [tri]
    masks, upq, upk = [], [], []
    for m in _HG_LEVELS:
        p = (t // (2 * m)) * 2 * m + m - 1
        blocks.append(tri[p])
        masks.append(((t[:, None] // (2 * m)) == (t[None, :] // (2 * m))).astype(np.float32))
        upper = (t % (2 * m)) >= m
        upq.append(np.repeat(upper[:, None], HEAD, 1).astype(np.float32))
        upk.append(np.repeat(~upper[:, None], HEAD, 1).astype(np.float32))
    blocks.append(tri)
    masks.append(np.eye(C, dtype=np.float32))
    upq.append(np.ones((C, HEAD), np.float32))
    upk.append(np.ones((C, HEAD), np.float32))
    mstack = np.concatenate(blocks, axis=0)
    mstack3 = np.concatenate([mstack] * 3, axis=1)
    trirev3 = np.concatenate([tri.T] * 3, axis=1)
    return (jnp.asarray(mstack3, BF16), jnp.asarray(np.stack(masks)), jnp.asarray(np.stack(upq)),
            jnp.asarray(np.stack(upk)), jnp.asarray(trirev3, BF16))


def _split3(x):
    hi = x.astype(BF16)
    r1 = x - hi.astype(F32)
    mid = r1.astype(BF16)
    lo = (r1 - mid.astype(F32)).astype(BF16)
    return jnp.concatenate([hi, mid, lo], axis=0)


def _hgrn_chunk_common(qa, fa, lbv, mstack3, upq, upk):
    sq = _sigmoid(qa)
    q = qa * sq
    sf = _sigmoid(fa)
    f = lbv + (1.0 - lbv) * sf
    g = jnp.log(f)
    k = 1.0 - f
    gall = jnp.dot(mstack3, _split3(g), preferred_element_type=F32).reshape(_N_LEV + 1, HG_CHUNK, HEAD)
    G = gall[0]
    eq_exp = G[None] - gall[1:]
    eq = jnp.exp(jnp.minimum(eq_exp, 0.0)) * upq
    ek = jnp.exp(jnp.minimum(-eq_exp, 0.0)) * upk
    Qs = (q[None] * eq).astype(BF16)
    Ks = (k[None] * ek).astype(BF16)
    return sq, q, sf, f, k, G, eq, ek, Qs, Ks


def hgrn_forward(proj, lb, hg_norm, ts=HG_TILE):
    S = proj.shape[0]
    nh = lb.shape[1] // HEAD
    C = HG_CHUNK
    ncs = ts // C
    mstack3, masks, upq, upk, _ = _hgrn_constants()

    def body(qa_ref, fa_ref, ia_ref, ga_ref, lb_ref, gn_ref, ms_ref, mk_ref, uq_ref, uk_ref,
             oa_ref, oraw_ref, st_ref, state):
        tt = pl.program_id(1)

        @pl.when(tt == 0)
        def _():
            state[...] = jnp.zeros_like(state)

        gn = gn_ref[...]

        def chunk(c, carry):
            sl = pl.ds(pl.multiple_of(c * C, C), C)
            for hh in range(HG_HEADS_PER_BLOCK):
                cols = slice(hh * HEAD, (hh + 1) * HEAD)
                qa, fa, v, ga = qa_ref[sl, cols], fa_ref[sl, cols], ia_ref[sl, cols], ga_ref[sl, cols]
                _, q, _, _, k, G, _, _, Qs, Ks = _hgrn_chunk_common(qa, fa, lb_ref[:, cols], ms_ref[...],
                                                                    uq_ref[...], uk_ref[...])
                att7 = lax.dot_general(Qs, Ks, (((2,), (2,)), ((0,), (0,))), preferred_element_type=F32)
                att = jnp.sum(att7 * mk_ref[...], axis=0)
                st = state[hh]
                st_ref[hh, c] = st
                vb = v.astype(BF16)
                qg = (q * jnp.exp(G)).astype(BF16)
                o = jnp.dot(att.astype(BF16), vb, preferred_element_type=F32)
                o = o + lax.dot_general(qg, st.astype(BF16), (((1,), (1,)), ((), ())),
                                        preferred_element_type=F32)
                g_last = G[C - 1:C, :]
                kh = (k * jnp.exp(g_last - G)).astype(BF16)
                state[hh] = st * jnp.exp(g_last) + lax.dot_general(vb, kh, (((0,), (0,)), ((), ())),
                                                                   preferred_element_type=F32)
                oraw_ref[sl, cols] = o
                r = lax.rsqrt(jnp.mean(o * o, axis=-1, keepdims=True) + RMS_EPS)
                oa_ref[sl, cols] = (((o * r) * gn) * (ga * _sigmoid(ga))).astype(BF16)
            return carry

        lax.fori_loop(0, ncs, chunk, 0)

    hpb = HG_HEADS_PER_BLOCK
    wide = hpb * HEAD

    def col(m0):
        return pl.BlockSpec((ts, wide), lambda h, t: (t, m0 // hpb + h))

    const3 = lambda shape: pl.BlockSpec(shape, lambda h, t: (0, 0, 0))
    return pl.pallas_call(
        body, grid=(nh // hpb, S // ts), name="hgrn_fwd",
        in_specs=[col(0), col(nh), col(2 * nh), col(3 * nh),
                  pl.BlockSpec((1, wide), lambda h, t: (0, h)), pl.BlockSpec((1, HEAD), lambda h, t: (0, 0)),
                  pl.BlockSpec(mstack3.shape, lambda h, t: (0, 0)), const3(masks.shape), const3(upq.shape),
                  const3(upk.shape)],
        out_specs=[pl.BlockSpec((ts, wide), lambda h, t: (t, h)), pl.BlockSpec((ts, wide), lambda h, t: (t, h)),
                   pl.BlockSpec((hpb, ncs, HEAD, HEAD), lambda h, t: (h, t, 0, 0))],
        out_shape=[jax.ShapeDtypeStruct((S, nh * HEAD), BF16), jax.ShapeDtypeStruct((S, nh * HEAD), F32),
                   jax.ShapeDtypeStruct((nh, S // C, HEAD, HEAD), F32)],
        scratch_shapes=[pltpu.VMEM((hpb, HEAD, HEAD), F32)],
        compiler_params=_params(("arbitrary", "arbitrary")),
    )(proj, proj, proj, proj, lb, hg_norm, mstack3, masks, upq, upk)


def hgrn_backward(dcat, proj, oraw, states, lb, hg_norm, ts=HG_TILE):
    S = proj.shape[0]
    nh = lb.shape[1] // HEAD
    C = HG_CHUNK
    ncs = ts // C
    nt = S // ts
    mstack3, masks, upq, upk, trirev3 = _hgrn_constants()

    def body(do_ref, qa_ref, fa_ref, ia_ref, ga_ref, or_ref, st_ref, lb_ref, gn_ref, ms_ref, mk_ref, uq_ref,
             uk_ref, tr_ref, dqa_ref, dfa_ref, dia_ref, dga_ref, dlb_ref, dgn_ref, dstate):
        tt = pl.program_id(1)

        @pl.when(tt == 0)
        def _():
            dstate[...] = jnp.zeros_like(dstate)
            dlb_ref[...] = jnp.zeros_like(dlb_ref)
            dgn_ref[...] = jnp.zeros_like(dgn_ref)

        gn = gn_ref[...]

        def chunk(cc, carry):
            c = ncs - 1 - cc
            sl = pl.ds(pl.multiple_of(c * C, C), C)
            for hh in range(HG_HEADS_PER_BLOCK):
                cols = slice(hh * HEAD, (hh + 1) * HEAD)
                lbv = lb_ref[:, cols]
                qa, fa, v, ga = qa_ref[sl, cols], fa_ref[sl, cols], ia_ref[sl, cols], ga_ref[sl, cols]
                sq, q, sf, f, k, G, eq, ek, Qs, Ks = _hgrn_chunk_common(qa, fa, lbv, ms_ref[...], uq_ref[...],
                                                                        uk_ref[...])
                mk = mk_ref[...]
                att7 = lax.dot_general(Qs, Ks, (((2,), (2,)), ((0,), (0,))), preferred_element_type=F32)
                att = jnp.sum(att7 * mk, axis=0)
                o = or_ref[sl, cols]
                dO = do_ref[sl, cols]
                sg = _sigmoid(ga)
                r = lax.rsqrt(jnp.mean(o * o, axis=-1, keepdims=True) + RMS_EPS)
                xh = o * r
                dga_ref[sl, cols] = (dO * (xh * gn) * (sg * (1.0 + ga * (1.0 - sg)))).astype(BF16)
                don = dO * (ga * sg)
                dgn_ref[hh] += jnp.sum(don * xh, axis=0, keepdims=True)
                dxh = don * gn
                do = r * (dxh - xh * jnp.mean(dxh * xh, axis=-1, keepdims=True))
                dob = do.astype(BF16)
                st = st_ref[hh, c]
                dst = dstate[hh]
                dstb = dst.astype(BF16)
                vb = v.astype(BF16)
                eG = jnp.exp(G)
                g_last = G[C - 1:C, :]
                e_last = jnp.exp(g_last)
                e_tail = jnp.exp(g_last - G)
                qg = (q * eG).astype(BF16)
                kh = (k * e_tail).astype(BF16)
                dq_inter = jnp.dot(dob, st.astype(BF16), preferred_element_type=F32) * eG
                dk_inter = jnp.dot(vb, dstb, preferred_element_type=F32) * e_tail
                dv = lax.dot_general(kh, dstb, (((1,), (1,)), ((), ())), preferred_element_type=F32)
                dv = dv + lax.dot_general(att.astype(BF16), dob, (((0,), (0,)), ((), ())),
                                          preferred_element_type=F32)
                dA = lax.dot_general(dob, vb, (((1,), (1,)), ((), ())), preferred_element_type=F32)
                dA7 = (dA[None] * mk).astype(BF16)
                dAT7 = (dA.T[None] * mk).astype(BF16)
                dQs = lax.dot_general(dA7, Ks, (((2,), (1,)), ((0,), (0,))), preferred_element_type=F32)
                dKs = lax.dot_general(dAT7, Qs, (((2,), (1,)), ((0,), (0,))), preferred_element_type=F32)
                dq = dq_inter + jnp.sum(dQs * eq, axis=0)
                dk = dk_inter + jnp.sum(dKs * ek, axis=0)
                dG = (jnp.sum(Qs.astype(F32) * dQs - Ks.astype(F32) * dKs, axis=0)
                      + q * dq_inter - k * dk_inter)
                last_extra = (jnp.sum(k * dk_inter, axis=0, keepdims=True)
                              + e_last * jnp.sum(dst * st, axis=0, keepdims=True))
                is_last = lax.broadcasted_iota(jnp.int32, (C, 1), 0) == C - 1
                dG = dG + jnp.where(is_last, last_extra, 0.0)
                dg = jnp.dot(tr_ref[...], _split3(dG), preferred_element_type=F32)
                df = dg / f - dk
                dfa_ref[sl, cols] = (df * (1.0 - lbv) * (sf * (1.0 - sf))).astype(BF16)
                dlb_ref[:, cols] += jnp.sum(df * (1.0 - sf), axis=0, keepdims=True)
                dqa_ref[sl, cols] = (dq * (sq * (1.0 + qa * (1.0 - sq)))).astype(BF16)
                dia_ref[sl, cols] = dv.astype(BF16)
                dstate[hh] = dst * e_last + lax.dot_general(dob, qg, (((0,), (0,)), ((), ())),
                                                            preferred_element_type=F32)
            return carry

        lax.fori_loop(0, ncs, chunk, 0)

    hpb = HG_HEADS_PER_BLOCK
    wide = hpb * HEAD

    def col(m0):
        return pl.BlockSpec((ts, wide), lambda h, t: (nt - 1 - t, m0 // hpb + h))

    const3 = lambda shape: pl.BlockSpec(shape, lambda h, t: (0, 0, 0))
    const2 = lambda shape: pl.BlockSpec(shape, lambda h, t: (0, 0))
    ocol = pl.BlockSpec((ts, wide), lambda h, t: (nt - 1 - t, h))
    half = nh * HEAD
    return pl.pallas_call(
        body, grid=(nh // hpb, nt), name="hgrn_bwd",
        in_specs=[col(0), col(0), col(nh), col(2 * nh), col(3 * nh), col(0),
                  pl.BlockSpec((hpb, ncs, HEAD, HEAD), lambda h, t: (h, nt - 1 - t, 0, 0)),
                  pl.BlockSpec((1, wide), lambda h, t: (0, h)), const2((1, HEAD)),
                  const2(mstack3.shape), const3(masks.shape), const3(upq.shape), const3(upk.shape),
                  const2(trirev3.shape)],
        out_specs=[ocol, ocol, ocol, ocol, pl.BlockSpec((1, wide), lambda h, t: (0, h)),
                   pl.BlockSpec((hpb, 1, HEAD), lambda h, t: (h, 0, 0))],
        out_shape=[jax.ShapeDtypeStruct((S, half), BF16)] * 4
                  + [jax.ShapeDtypeStruct((1, half), F32), jax.ShapeDtypeStruct((nh, 1, HEAD), F32)],
        scratch_shapes=[pltpu.VMEM((hpb, HEAD, HEAD), F32)],
        compiler_params=_params(("arbitrary", "arbitrary")),
    )(dcat, proj, proj, proj, proj, oraw, states, lb, hg_norm, mstack3, masks, upq, upk, trirev3)


SB_SUB = 128
LOG2_E = 1.4426950408889634
SB_SCALE = 1.0 / math.sqrt(HEAD)
SB_QUERY_SCALE = SB_SCALE * LOG2_E


def _split2(x):
    hi = x.astype(BF16)
    lo = (x - hi.astype(F32)).astype(BF16)
    return jnp.concatenate([hi, lo], axis=1)


def _sb_constants():
    j = np.arange(SB_SUB)
    after = (j[:, None] > j[None, :]).astype(np.float32)
    before = (j[:, None] < j[None, :]).astype(np.float32)
    return (jnp.asarray(np.concatenate([after, after], axis=0), BF16),
            jnp.asarray(np.concatenate([before, before], axis=0), BF16))


def _sb_tri(i):
    return (i * (i + 1)) // 2


def _sb_diag_mask(t):
    return lax.broadcasted_iota(jnp.int32, (t, t), 1) < lax.broadcasted_iota(jnp.int32, (t, t), 0)


def _sb_scores(q, k_ref, col0, t):
    ks = k_ref[pl.ds(pl.multiple_of(col0, t), t), :]
    return lax.dot_general(q, ks, (((1,), (1,)), ((), ())), preferred_element_type=F32)


def _sb_weights(z, mask, run, after2):
    nsub = z.shape[1] // SB_SUB
    nz = -z
    lk = jnp.minimum(nz, 0.0) - jnp.log(1.0 + jnp.exp2(jnp.minimum(z, nz))) * LOG2_E
    if mask is not None:
        lk = jnp.where(mask, lk, 0.0)
    locs, tots = [], []
    for b in range(nsub):
        lkb = lk[:, b * SB_SUB:(b + 1) * SB_SUB]
        loc = jnp.dot(_split2(lkb), after2, preferred_element_type=F32)
        locs.append(loc)
        tots.append(loc[:, 0:1] + lkb[:, 0:1])
    ws = [None] * nsub
    for b in reversed(range(nsub)):
        sl = slice(b * SB_SUB, (b + 1) * SB_SUB)
        ws[b] = jnp.exp2(z[:, sl] + lk[:, sl] + (locs[b] + run))
        run = run + tots[b]
    w = jnp.concatenate(ws, axis=1)
    if mask is not None:
        w = jnp.where(mask, w, 0.0)
    return w, run


def sb_forward(projb, nh, m0, t=SB_TILE):
    S = projb.shape[0]
    after2, _ = _sb_constants()
    n_i = S // t

    def body(q_ref, k_ref, v_ref, af_ref, o_ref, w_hbm, wbuf, wsem):
        h = pl.program_id(0)
        i = pl.program_id(1)
        q = q_ref[...]
        after = af_ref[...]
        base = _sb_tri(i)

        def store(slot, jb):
            return pltpu.make_async_copy(wbuf.at[slot], w_hbm.at[h, base + jb], wsem.at[slot])

        def block(n, jb, run, mask):
            slot = n % 2

            @pl.when(n >= 2)
            def _():
                store(slot, jb).wait()

            z = _sb_scores(q, k_ref, jb * t, t)
            w, run = _sb_weights(z, mask, run, after)
            wb = w.astype(BF16)
            wbuf[slot] = wb
            store(slot, jb).start()
            vs = v_ref[pl.ds(pl.multiple_of(jb * t, t), t), :]
            return run, jnp.dot(wb, vs, preferred_element_type=F32)

        run, acc = block(0, i, jnp.zeros((t, 1), F32), _sb_diag_mask(t))

        def step(n, carry):
            run, acc = carry
            run, part = block(n + 1, i - 1 - n, run, None)
            return run, acc + part

        _, acc = lax.fori_loop(0, i, step, (run, acc))
        o_ref[...] = acc.astype(BF16)
        store(i % 2, 0).wait()

        @pl.when(i >= 1)
        def _():
            store((i + 1) % 2, 0).wait()

    return pl.pallas_call(
        body, grid=(nh, n_i), name="sb_fwd",
        in_specs=[pl.BlockSpec((t, HEAD), lambda h, i: (i, m0 + h)),
                  pl.BlockSpec((S, HEAD), lambda h, i: (0, m0 + nh + h)),
                  pl.BlockSpec((S, HEAD), lambda h, i: (0, m0 + 2 * nh + h)),
                  pl.BlockSpec(after2.shape, lambda h, i: (0, 0))],
        out_specs=[pl.BlockSpec((t, HEAD), lambda h, i: (i, h)), pl.BlockSpec(memory_space=pl.ANY)],
        out_shape=[jax.ShapeDtypeStruct((S, nh * HEAD), BF16),
                   jax.ShapeDtypeStruct((nh, _sb_tri(n_i), t, t), BF16)],
        scratch_shapes=[pltpu.VMEM((2, t, t), BF16), pltpu.SemaphoreType.DMA((2,))],
        compiler_params=_params(("arbitrary", "arbitrary")),
    )(projb, projb, projb, after2)


def sb_backward(dcat, projb, w_all, nh, m0, t=SB_TILE):
    S = projb.shape[0]
    _, before2 = _sb_constants()
    n_i = S // t
    nsub = t // SB_SUB

    def body(do_ref, q_ref, k_ref, v_ref, bf_ref, w_hbm, dq_ref, dk_ref, dv_ref, dk_acc, dv_acc, wbuf, wsem):
        h = pl.program_id(0)
        i = pl.program_id(1)

        @pl.when(i == 0)
        def _():
            dk_acc[...] = jnp.zeros_like(dk_acc)
            dv_acc[...] = jnp.zeros_like(dv_acc)

        q = q_ref[...]
        dob = do_ref[...].astype(BF16)
        before = bf_ref[...]
        base = _sb_tri(i)

        def load(slot, jb):
            return pltpu.make_async_copy(w_hbm.at[h, base + jb], wbuf.at[slot], wsem.at[slot])

        load(0, 0).start()

        def left_to_right(jb, run, dq, mask):
            slot = jb % 2
            load(slot, jb).wait()

            @pl.when(jb < i)
            def _():
                load(1 - slot, jb + 1).start()

            ksl = pl.ds(pl.multiple_of(jb * t, t), t)
            wb = wbuf[slot]
            z = _sb_scores(q, k_ref, jb * t, t)
            dw = lax.dot_general(dob, v_ref[ksl, :], (((1,), (1,)), ((), ())), preferred_element_type=F32)
            d = dw * wb.astype(F32)
            dv_acc[ksl, :] += lax.dot_general(wb, dob, (((0,), (0,)), ((), ())), preferred_element_type=F32)
            sig = 1.0 / (1.0 + jnp.exp2(-z))
            das = []
            for b in range(nsub):
                db = d[:, b * SB_SUB:(b + 1) * SB_SUB]
                prefix = run + jnp.dot(_split2(db), before, preferred_element_type=F32)
                das.append(db - sig[:, b * SB_SUB:(b + 1) * SB_SUB] * (db + prefix))
                run = prefix[:, SB_SUB - 1:SB_SUB] + db[:, SB_SUB - 1:SB_SUB]
            da = jnp.concatenate(das, axis=1)
            if mask is not None:
                da = jnp.where(mask, da, 0.0)
            dab = (da * SB_SCALE).astype(BF16)
            dq = dq + jnp.dot(dab, k_ref[ksl, :], preferred_element_type=F32)
            dk_acc[ksl, :] += lax.dot_general(dab, q, (((0,), (0,)), ((), ())), preferred_element_type=F32)
            return run, dq

        run, dq = lax.fori_loop(0, i, lambda jb, c: left_to_right(jb, c[0], c[1], None),
                                (jnp.zeros((t, 1), F32), jnp.zeros((t, HEAD), F32)))
        _, dq = left_to_right(i, run, dq, _sb_diag_mask(t))
        dq_ref[...] = dq.astype(BF16)

        @pl.when(i == n_i - 1)
        def _():
            dk_ref[...] = (dk_acc[...] * (1.0 / SB_QUERY_SCALE)).astype(BF16)
            dv_ref[...] = dv_acc[...].astype(BF16)

    half = nh * HEAD
    full = pl.BlockSpec((S, HEAD), lambda h, i: (0, h))
    return pl.pallas_call(
        body, grid=(nh, n_i), name="sb_bwd",
        in_specs=[pl.BlockSpec((t, HEAD), lambda h, i: (i, nh + h)),
                  pl.BlockSpec((t, HEAD), lambda h, i: (i, m0 + h)),
                  pl.BlockSpec((S, HEAD), lambda h, i: (0, m0 + nh + h)),
                  pl.BlockSpec((S, HEAD), lambda h, i: (0, m0 + 2 * nh + h)),
                  pl.BlockSpec(before2.shape, lambda h, i: (0, 0)), pl.BlockSpec(memory_space=pl.ANY)],
        out_specs=[pl.BlockSpec((t, HEAD), lambda h, i: (i, h)), full, full],
        out_shape=[jax.ShapeDtypeStruct((S, half), BF16)] * 3,
        scratch_shapes=[pltpu.VMEM((S, HEAD), F32), pltpu.VMEM((S, HEAD), F32),
                        pltpu.VMEM((2, t, t), BF16), pltpu.SemaphoreType.DMA((2,))],
        compiler_params=_params(("arbitrary", "arbitrary")),
    )(dcat, projb, projb, projb, before2, w_all)


def local_step(x, target, mix_norm, ffn_norm, final_norm, lb_logits, hg_norm, get_w_in, get_w_rest, send):
    S, D = x.shape
    half = D // 2
    nh = half // HEAD
    tm = ROW_TILE
    tk = REDUCE_TILE
    row = lambda i, j: (i, 0)

    lb = jax.nn.softmax(lb_logits, axis=0)[0:1]

    h0, r0 = rms_fwd(x, mix_norm[0:1], BF16)
    w_in = get_w_in(h0)
    nbi = w_in.shape[2]
    col = jnp.arange(N_DEV * nbi) // half
    col_scale = jnp.where(col == 4, SB_QUERY_SCALE, 1.0).astype(F32)[None]
    proj, projb = matmul(
        "proj_in", [h0], [w_in], grid=(N_DEV, S // ROW_TILE_WIDE, 1),
        a_spec=pl.BlockSpec((ROW_TILE_WIDE, D), lambda j, i, k: (i, 0)),
        b_spec=pl.BlockSpec((None, D, nbi), lambda j, i, k: (j, 0, 0)),
        out_spec=pl.BlockSpec((ROW_TILE_WIDE, nbi), lambda j, i, k: (i, j)), out_shape=(S, N_DEV * nbi),
        out_dtypes=[F32, BF16], acc_shape=(8, 128),
        bf16_scale=col_scale, bf16_scale_spec=pl.BlockSpec((1, nbi), lambda j, i, k: (0, j)))
    oa, oraw, states = hgrn_forward(proj, lb, hg_norm)
    ob, sb_weights = sb_forward(projb, nh, 4 * nh)
    cat = jnp.concatenate([oa, ob], axis=1)
    w_out, pool_w, pool_scale, wg, wu, wd = get_w_rest(cat)
    (x1,) = matmul(
        "mix_out", [cat], [w_out], grid=(S // tm, 1),
        a_spec=pl.BlockSpec((tm, D), row), b_spec=pl.BlockSpec((D, D), lambda i, k: (0, 0)),
        out_spec=pl.BlockSpec((tm, D), row), out_shape=(S, D), out_dtypes=[F32], acc_shape=(8, 128),
        res=x, res_spec=pl.BlockSpec((tm, D), row))
    h1, r1 = rms_fwd(x1, ffn_norm[0:1], BF16)
    x2, ffn0 = ffn_forward(h1, x1, wg[0], wu[0], wd[0])

    h2, r2 = rms_fwd(x2, mix_norm[1:2], F32)
    x3, pooled = pool_forward(h2, x2, pool_w, pool_scale)
    h3, r3 = rms_fwd(x3, ffn_norm[1:2], BF16)
    x4, ffn1 = ffn_forward(h3, x3, wg[1], wu[1], wd[1])

    loss_blk, dx4, dx4b, d_final = loss_and_final_bwd(x4, final_norm, target)

    (dx3, _, d_ffn1), dwg1, dwu1, dwd1 = ffn_backward(dx4b, h3, ffn1, wg[1], wu[1], wd[1],
                                                      x3, r3, ffn_norm[1:2], dx4)
    dx3 = send("ffn1", dict(ffn_w_gate_1=dwg1, ffn_w_up_1=dwu1, ffn_w_down_1=dwd1), dx3)
    dmixed, dpooled, d_pscale = pool_backward_mix(dx3, pooled, pool_w, pool_scale)
    G = len(POOL_WINDOWS)
    P = D // G
    (d_pool_w,) = matmul(
        "pool_dw", [pooled], [dmixed], grid=(G, S // tk),
        a_spec=pl.BlockSpec((tk, P), lambda g, k: (k, g)), b_spec=pl.BlockSpec((tk, P), lambda g, k: (k, g)),
        out_spec=pl.BlockSpec((None, P, P), lambda g, k: (g, 0, 0)), out_shape=(G, P, P), out_dtypes=[BF16],
        acc_shape=(P, P), trans_a=True)
    dx2, dx2b, d_mix1 = pool_backward_window(dpooled, x2, r2, mix_norm[1:2], dx3)

    (dx1, dx1b, d_ffn0), dwg0, dwu0, dwd0 = ffn_backward(dx2b, h1, ffn0, wg[0], wu[0], wd[0],
                                                         x1, r1, ffn_norm[0:1], dx2)
    (dcat,) = matmul(
        "mix_out_dx", [dx1b], [w_out], grid=(S // tm, 1),
        a_spec=pl.BlockSpec((tm, D), row), b_spec=pl.BlockSpec((D, D), lambda i, k: (0, 0)),
        out_spec=pl.BlockSpec((tm, D), row), out_shape=(S, D), out_dtypes=[F32], acc_shape=(8, 128),
        trans_b=True)
    (d_w_out,) = matmul(
        "mix_out_dw", [cat], [dx1b], grid=(2, S // tk),
        a_spec=pl.BlockSpec((tk, half), lambda m, k: (k, m)), b_spec=pl.BlockSpec((tk, D), lambda m, k: (k, 0)),
        out_spec=pl.BlockSpec((half, D), lambda m, k: (m, 0)), out_shape=(D, D), out_dtypes=[BF16],
        acc_shape=(half, D), trans_a=True)
    dcat = send("layer0", dict(ffn_w_gate_0=dwg0, ffn_w_up_0=dwu0, ffn_w_down_0=dwd0, pool_w=d_pool_w,
                               ab_w_out=d_w_out), dcat)
    dqa, dfa, dia, dga, d_lb, d_hg = hgrn_backward(dcat, proj, oraw, states, lb, hg_norm)
    dqb, dkb, dvb = sb_backward(dcat, projb, sb_weights, nh, 4 * nh)
    dproj = jnp.concatenate([dqa, dfa, dia, dga, dqb, dkb, dvb], axis=1)
    (d_w_in,) = matmul(
        "proj_in_dw", [h0], [dproj], grid=(N_DEV, S // tk),
        a_spec=pl.BlockSpec((tk, D), lambda j, k: (k, 0)), b_spec=pl.BlockSpec((tk, nbi), lambda j, k: (k, j)),
        out_spec=pl.BlockSpec((None, D, nbi), lambda j, k: (j, 0, 0)), out_shape=(N_DEV, D, nbi),
        out_dtypes=[BF16], acc_shape=(D, nbi), trans_a=True)
    dproj = send("w_in", dict(ab_w_in=d_w_in), dproj)
    dx0, _, d_mix0 = matmul_rms_bwd(
        "proj_in_dx", [dproj], [w_in], grid=(S // tm, N_DEV),
        a_spec=pl.BlockSpec((tm, nbi), lambda i, j: (i, j)),
        b_spec=pl.BlockSpec((None, D, nbi), lambda i, j: (j, 0, 0)),
        tm=tm, x=x, r=r0, gain=mix_norm[0:1], dres=dx1)

    d_l0 = d_lb * lb * (1.0 - lb)
    small = dict(
        loss=loss_blk[0:1, 0:1],
        mix_norm=jnp.concatenate([d_mix0, d_mix1], axis=0),
        ffn_norm=jnp.concatenate([d_ffn0, d_ffn1], axis=0),
        final_norm=d_final,
        lb_logits=jnp.concatenate([d_l0, -d_l0], axis=0),
        hg_out_norm=jnp.sum(d_hg, axis=0),
        pool_scale=d_pscale,
    )
    return dx0, small


def _my_index():
    return 4 * lax.axis_index("x") + 2 * lax.axis_index("y") + lax.axis_index("c")


def _peer(r):
    x, y, c = lax.axis_index("x"), lax.axis_index("y"), lax.axis_index("c")
    px = 1 - x if (r >> 2) & 1 else x
    py = 1 - y if (r >> 1) & 1 else y
    pc = 1 - c if r & 1 else c
    return (px, py, pc), 4 * px + 2 * py + pc


def gather_two_level(name, shard):
    def body(x_ref, out_ref, send_sems, recv_sems, local_sem):
        x, y, c = lax.axis_index("x"), lax.axis_index("y"), lax.axis_index("c")
        me, sibling = (x, y, c), (x, y, 1 - c)
        chips = [(1 - x, y), (x, 1 - y), (1 - x, 1 - y)]

        def slot(px, py, pc):
            return out_ref.at[4 * px + 2 * py + pc]

        def copy(k, block, to, src=None):
            return pltpu.make_async_remote_copy(
                src_ref=slot(*block) if src is None else src, dst_ref=slot(*block), send_sem=send_sems.at[k],
                recv_sem=recv_sems.at[k], device_id=to, device_id_type=MESH)

        mine = pltpu.make_async_copy(x_ref, slot(*me), local_sem)
        mine.start()
        first = [copy(0, me, sibling, src=x_ref)]
        first += [copy(1 + j, me, (*chip, c), src=x_ref) for j, chip in enumerate(chips)]
        for cp in first:
            cp.start()
        passed = [copy(4 + j, (*chip, c), sibling) for j, chip in enumerate(chips)]
        for j, chip in enumerate(chips):
            copy(1 + j, (*chip, c), me).wait_recv()
            passed[j].start()
        copy(0, sibling, me).wait_recv()
        for j, chip in enumerate(chips):
            copy(4 + j, (*chip, 1 - c), me).wait_recv()
        for cp in first + passed:
            cp.wait_send()
        mine.wait()

    any_spec = pl.BlockSpec(memory_space=pl.ANY)
    return pl.pallas_call(
        body, name=name, in_specs=[any_spec], out_specs=any_spec,
        out_shape=jax.ShapeDtypeStruct((N_DEV,) + shard.shape, shard.dtype),
        scratch_shapes=[pltpu.SemaphoreType.DMA((N_DEV - 1,)), pltpu.SemaphoreType.DMA((N_DEV - 1,)),
                        pltpu.SemaphoreType.DMA],
    )(shard)


def exchange(name, arrays, gather):
    n = len(arrays)
    n_peers = N_DEV - 1

    def body(*refs):
        ins, outs = refs[:n], refs[n:2 * n]
        send_sems, recv_sems, local_sems = refs[2 * n:]
        me = _my_index()
        local = []
        for a in range(n):
            src = ins[a] if gather else ins[a].at[me]
            cp = pltpu.make_async_copy(src, outs[a].at[me], local_sems.at[a])
            cp.start()
            local.append(cp)
        remote = []
        for a in range(n):
            for r in range(1, N_DEV):
                peer, pidx = _peer(r)
                src = ins[a] if gather else ins[a].at[pidx]
                cp = pltpu.make_async_remote_copy(
                    src_ref=src, dst_ref=outs[a].at[me], send_sem=send_sems.at[a * n_peers + r - 1],
                    recv_sem=recv_sems.at[a * n_peers + r - 1], device_id=peer, device_id_type=MESH)
                cp.start()
                remote.append((cp, a, r))
        for cp, a, r in remote:
            _, pidx = _peer(r)
            src = ins[a] if gather else ins[a].at[pidx]
            pltpu.make_async_remote_copy(
                src_ref=src, dst_ref=outs[a].at[pidx], send_sem=send_sems.at[a * n_peers + r - 1],
                recv_sem=recv_sems.at[a * n_peers + r - 1], device_id=_peer(r)[0], device_id_type=MESH).wait_recv()
        for cp, a, r in remote:
            cp.wait_send()
        for cp in local:
            cp.wait()

    out_shape = [jax.ShapeDtypeStruct(((N_DEV,) + a.shape) if gather else a.shape, a.dtype) for a in arrays]
    any_spec = pl.BlockSpec(memory_space=pl.ANY)
    return pl.pallas_call(
        body, name=name, in_specs=[any_spec] * n, out_specs=[any_spec] * n, out_shape=out_shape,
        scratch_shapes=[pltpu.SemaphoreType.DMA((n * n_peers,)), pltpu.SemaphoreType.DMA((n * n_peers,)),
                        pltpu.SemaphoreType.DMA((n,))],
    )(*arrays)


_HBM = pl.BlockSpec(memory_space=pltpu.HBM)
_SEM = pl.BlockSpec(memory_space=pltpu.SEMAPHORE)
_EFFECT = pltpu.SideEffectType.DATAFLOW_SIDE_EFFECTING


def _landing(arrays, gather):
    me = _my_index()
    lands = []
    for a in arrays:
        own = a[None] if gather else lax.dynamic_slice_in_dim(a, me, 1, axis=0)
        shape = ((N_DEV,) + a.shape) if gather else a.shape
        lands.append(lax.dynamic_update_slice_in_dim(lax.empty(shape, a.dtype), own, me, axis=0))
    return lands


def exchange_start(name, arrays, gather, carry):
    n = len(arrays)
    n_peers = N_DEV - 1
    lands = _landing(arrays, gather)
    n_thru = 2 * n + 1

    def body(*refs):
        src, land = refs[:n], refs[n:2 * n]
        send_sems, recv_sems = refs[n_thru], refs[n_thru + 1]
        token = refs[-1]
        me = _my_index()
        for a in range(n):
            for r in range(1, N_DEV):
                peer, pidx = _peer(r)
                pltpu.make_async_remote_copy(
                    src_ref=src[a] if gather else src[a].at[pidx], dst_ref=land[a].at[me],
                    send_sem=send_sems.at[a * n_peers + r - 1], recv_sem=recv_sems.at[a * n_peers + r - 1],
                    device_id=peer, device_id_type=MESH).start()
        token[...] = jnp.zeros_like(token)

    operands = list(arrays) + lands + [carry]
    outs = pl.pallas_call(
        body, name=name,
        out_shape=(pltpu.SemaphoreType.DMA((n * n_peers,)), pltpu.SemaphoreType.DMA((n * n_peers,)),
                   *[pltpu.HBM(a.shape, a.dtype) for a in operands], jax.ShapeDtypeStruct((8, 128), F32)),
        in_specs=[_HBM] * n_thru,
        out_specs=(_SEM, _SEM, *([_HBM] * n_thru), pl.BlockSpec(memory_space=pltpu.VMEM)),
        input_output_aliases={i: 2 + i for i in range(n_thru)},
        compiler_params=pltpu.CompilerParams(has_side_effects=_EFFECT),
    )(*[pltpu.with_memory_space_constraint(a, pltpu.HBM) for a in operands])
    handle = (outs[0], outs[1], list(outs[2:2 + n]), list(outs[2 + n:2 + 2 * n]), gather)
    return handle, outs[2 + 2 * n]


def exchange_wait(name, handle, after):
    send_sems, recv_sems, srcs, lands, gather = handle
    n = len(srcs)
    n_peers = N_DEV - 1

    def body(*refs):
        src, land = refs[:n], refs[n:2 * n]
        send_s, recv_s = refs[2 * n], refs[2 * n + 1]
        for a in range(n):
            for r in range(1, N_DEV):
                peer, pidx = _peer(r)
                cp = pltpu.make_async_remote_copy(
                    src_ref=src[a] if gather else src[a].at[pidx], dst_ref=land[a].at[pidx],
                    send_sem=send_s.at[a * n_peers + r - 1], recv_sem=recv_s.at[a * n_peers + r - 1],
                    device_id=peer, device_id_type=MESH)
                cp.wait_send()
                cp.wait_recv()

    shapes = [pltpu.HBM(a.shape, a.dtype) for a in srcs] + [pltpu.HBM(l.shape, l.dtype) for l in lands]
    outs = pl.pallas_call(
        body, name=name, out_shape=tuple(shapes),
        in_specs=[_HBM] * (2 * n) + [_SEM, _SEM, pl.BlockSpec(memory_space=pl.ANY)],
        out_specs=tuple([_HBM] * (2 * n)),
        input_output_aliases={i: i for i in range(2 * n)},
        compiler_params=pltpu.CompilerParams(has_side_effects=_EFFECT),
    )(*srcs, *lands, send_sems, recv_sems, after)
    return list(outs[n:])


def _row_tile(rows, cap=256):
    best = None
    for t in range(16, min(rows, cap) + 1, 16):
        if rows % t == 0:
            best = t
    return best if best is not None else rows


def sum_slots(name, recv):
    n, R, C = recv.shape
    tr = _row_tile(R)

    def body(r_ref, o_ref):
        g = r_ref[0].astype(F32)
        for d in range(1, n):
            g = g + r_ref[d].astype(F32)
        o_ref[...] = g

    return pl.pallas_call(
        body, grid=(R // tr,), name=name,
        in_specs=[pl.BlockSpec((n, tr, C), lambda i: (0, i, 0))],
        out_specs=pl.BlockSpec((tr, C), lambda i: (i, 0)),
        out_shape=jax.ShapeDtypeStruct((R, C), F32),
        compiler_params=_params(("arbitrary",)),
    )(recv)


def adamw(name, recv, w, m, v, layer=None, prev=None):
    n, R, C = recv.shape
    tr = _row_tile(R)

    def body(r_ref, w_ref, m_ref, v_ref, *rest):
        g_ref, d_ref, nm_ref, nv_ref = rest[-4:]
        g = r_ref[0].astype(F32)
        for d in range(1, n):
            g = g + r_ref[d].astype(F32)
        mm = ADAM_B1 * m_ref[...] + (1.0 - ADAM_B1) * g
        vv = ADAM_B2 * v_ref[...] + (1.0 - ADAM_B2) * (g * g)
        m_hat = mm / (1.0 - ADAM_B1 ** ADAM_STEP)
        v_hat = vv / (1.0 - ADAM_B2 ** ADAM_STEP)
        g_ref[...] = g
        d_ref[...] = -ADAM_LR * (m_hat / (jnp.sqrt(v_hat) + ADAM_EPS) + ADAM_WD * w_ref[...])
        nm_ref[...] = mm
        nv_ref[...] = vv

    if layer is None:
        row = pl.BlockSpec((tr, C), lambda i: (i, 0))
        shape = (R, C)
    else:
        row = pl.BlockSpec((None, tr, C), lambda i: (layer, i, 0))
        shape = w.shape
    prev = [] if prev is None else list(prev)
    return pl.pallas_call(
        body, grid=(R // tr,), name=name,
        in_specs=[pl.BlockSpec((n, tr, C), lambda i: (0, i, 0)), row, row, row]
                 + [pl.BlockSpec(memory_space=pl.ANY)] * len(prev),
        out_specs=[row] * 4,
        out_shape=[jax.ShapeDtypeStruct(shape, F32)] * 4,
        input_output_aliases={4 + o: o for o in range(len(prev))},
        compiler_params=_params(("arbitrary",)),
    )(recv, w, m, v, *prev)


def _adamw_nd(name, recv, w, m, v):
    shp = w.shape
    C = shp[-1]
    flat = lambda a: a.reshape(-1, C)
    outs = adamw(name, recv.reshape(recv.shape[0], -1, C), flat(w), flat(m), flat(v))
    return [o.reshape(shp) for o in outs]


_SMALL_NAMES = ("loss", "mix_norm", "ffn_norm", "final_norm", "lb_logits", "hg_out_norm", "pool_scale")
_LANES = 128


def _pack_small(parts):
    rows, layout = [], {}
    at = 0
    for name in parts:
        flat = parts[name].reshape(-1).astype(F32)
        n_rows = -(-flat.shape[0] // (8 * _LANES)) * 8
        flat = jnp.pad(flat, (0, n_rows * _LANES - flat.shape[0]))
        rows.append(flat.reshape(n_rows, _LANES))
        layout[name] = (at, parts[name].shape)
        at += n_rows
    return jnp.concatenate(rows, axis=0), layout


def _unpack_small(pack, layout):
    out = {}
    for name, (at, shape) in layout.items():
        size = int(np.prod(shape))
        n_rows = -(-size // _LANES)
        out[name] = pack[at:at + n_rows].reshape(-1)[:size].reshape(shape)
    return out


def kernel(x, mix_norm, ffn_norm, final_norm, ab_w_in, lb_logits, hg_out_norm, ab_w_out, pool_w, pool_scale, ffn_w_gate, ffn_w_up, ffn_w_down, loss_target, m_mix_norm, m_ffn_norm, m_final_norm, m_ab_w_in, m_lb_logits, m_hg_out_norm, m_ab_w_out, m_pool_w, m_pool_scale, m_ffn_w_gate, m_ffn_w_up, m_ffn_w_down, v_mix_norm, v_ffn_norm, v_final_norm, v_ab_w_in, v_lb_logits, v_hg_out_norm, v_ab_w_out, v_pool_w, v_pool_scale, v_ffn_w_gate, v_ffn_w_up, v_ffn_w_down):
    D = x.shape[-1]
    n_layers = ffn_w_gate.shape[0]
    G = pool_w.shape[1]
    P = pool_w.shape[3]
    me = _my_index()

    rest = [ab_w_out[0], pool_w[0]]
    for l in range(n_layers):
        rest += [ffn_w_gate[l], ffn_w_up[l], ffn_w_down[l]]
    rest = [s.astype(BF16) for s in rest] + [pool_scale]
    rest_handle = []

    def get_w_in(after):
        w_in = gather_two_level("gather_w_in", ab_w_in[0].astype(BF16))
        handle, w_in = exchange_start("gather_rest_start", rest, True, w_in)
        rest_handle.append(handle)
        return w_in

    def get_w_rest(after):
        got = exchange_wait("gather_rest_wait", rest_handle[0], after)
        w_out_g = got[0].reshape(D, D)
        pool_g = got[1].transpose(1, 0, 2, 3).reshape(G, P, P)
        wg = [got[2 + 3 * l] for l in range(n_layers)]
        wu = [got[3 + 3 * l] for l in range(n_layers)]
        wd = [got[4 + 3 * l] for l in range(n_layers)]
        return w_out_g, pool_g, got[-1].reshape(1, D), wg, wu, wd

    in_flight = []

    def send(tag, grads, carry):
        if "pool_w" in grads:
            grads = dict(grads, pool_w=grads["pool_w"].reshape(G, N_DEV, P // N_DEV, P).transpose(1, 0, 2, 3))
        if "ab_w_out" in grads:
            grads = dict(grads, ab_w_out=grads["ab_w_out"].reshape(N_DEV, D // N_DEV, D))
        handle, carry = exchange_start("grads_" + tag + "_start", list(grads.values()), False, carry)
        in_flight.append((tag, list(grads.keys()), handle))
        return carry

    dx0, small = local_step(x[0], loss_target[0], mix_norm, ffn_norm, final_norm[None],
                            lb_logits, hg_out_norm, get_w_in, get_w_rest, send)

    recv = {}
    for tag, names, handle in in_flight:
        recv.update(zip(names, exchange_wait("grads_" + tag + "_wait", handle, dx0)))
    small_pack, layout = _pack_small({k: small[k] for k in _SMALL_NAMES})
    (small_all,) = exchange("gather_small", [small_pack], gather=True)
    tot = _unpack_small(sum_slots("sum_small", small_all), layout)

    res = {}
    res["ab_w_in"] = _adamw_nd("adamw_w_in", recv["ab_w_in"], ab_w_in, m_ab_w_in, v_ab_w_in)
    res["ab_w_out"] = _adamw_nd("adamw_w_out", recv["ab_w_out"], ab_w_out, m_ab_w_out, v_ab_w_out)
    res["pool_w"] = _adamw_nd("adamw_pool_w", recv["pool_w"], pool_w, m_pool_w, v_pool_w)
    ffn_in = {"ffn_w_gate": (ffn_w_gate, m_ffn_w_gate, v_ffn_w_gate),
              "ffn_w_up": (ffn_w_up, m_ffn_w_up, v_ffn_w_up),
              "ffn_w_down": (ffn_w_down, m_ffn_w_down, v_ffn_w_down)}
    for name, (w, m, v) in ffn_in.items():
        flip = name != "ffn_w_down"
        if flip:
            w, m, v = (jnp.swapaxes(a, 1, 2) for a in (w, m, v))
        outs = None
        for l in range(n_layers):
            outs = adamw("adamw_" + name, recv[name + "_" + str(l)], w, m, v, layer=l, prev=outs)
        res[name] = [jnp.swapaxes(o, 1, 2) for o in outs] if flip else outs

    n_ps = pool_scale.shape[1]
    small_g = dict(tot)
    small_g["pool_scale"] = lax.dynamic_slice(tot["pool_scale"], (0, me * n_ps), (1, n_ps))
    small_w = dict(mix_norm=(mix_norm, m_mix_norm, v_mix_norm), ffn_norm=(ffn_norm, m_ffn_norm, v_ffn_norm),
                   final_norm=(final_norm, m_final_norm, v_final_norm),
                   lb_logits=(lb_logits, m_lb_logits, v_lb_logits),
                   hg_out_norm=(hg_out_norm, m_hg_out_norm, v_hg_out_norm),
                   pool_scale=(pool_scale, m_pool_scale, v_pool_scale))
    g_pack, lay2 = _pack_small({k: small_g[k].reshape(small_w[k][0].shape) for k in small_w})
    w_pack, _ = _pack_small({k: small_w[k][0] for k in small_w})
    m_pack, _ = _pack_small({k: small_w[k][1] for k in small_w})
    v_pack, _ = _pack_small({k: small_w[k][2] for k in small_w})
    small_out = [_unpack_small(o, lay2) for o in adamw("adamw_small", g_pack[None], w_pack, m_pack, v_pack)]
    for k in small_w:
        res[k] = [small_out[o][k] for o in range(4)]

    order = ("mix_norm", "ffn_norm", "final_norm", "ab_w_in", "lb_logits", "hg_out_norm", "ab_w_out", "pool_w",
             "pool_scale", "ffn_w_gate", "ffn_w_up", "ffn_w_down")
    outs = [tot["loss"].reshape(()), dx0[None]]
    for o in range(4):
        outs += [res[k][o] for k in order]
    return tuple(outs)
```

```python
import math

import numpy as np
import jax
import jax.numpy as jnp
from jax import lax
from jax.experimental import pallas as pl
from jax.experimental.pallas import tpu as pltpu

F32 = jnp.float32
BF16 = jnp.bfloat16

N_DEV = 8
RMS_EPS = 1e-6
HEAD = 128
HG_CHUNK = 64
HG_HEADS_PER_BLOCK = 8
POOL_WINDOWS = (2, 4, 8, 16)
POOL_HALO = 16
ADAM_LR, ADAM_B1, ADAM_B2, ADAM_EPS, ADAM_WD, ADAM_STEP = 0.001, 0.9, 0.999, 1e-08, 0.01, 10
VMEM_LIMIT_BYTES = 60 * 1024 * 1024
MESH = pl.DeviceIdType.MESH

ROW_TILE = 512
ROW_TILE_WIDE = 1024
REDUCE_TILE = 2048
POOL_TILE = 512
HG_TILE = 512
SB_TILE = 512


def _params(sem):
    return pltpu.CompilerParams(dimension_semantics=sem, vmem_limit_bytes=VMEM_LIMIT_BYTES)


def _sigmoid(x):
    return 1.0 / (1.0 + jnp.exp(-x))


def rms_fwd(x, gain, out_dtype, ts=ROW_TILE):
    S, D = x.shape

    def body(x_ref, g_ref, h_ref, r_ref):
        xv = x_ref[...]
        r = lax.rsqrt(jnp.mean(xv * xv, axis=-1, keepdims=True) + RMS_EPS)
        h_ref[...] = ((xv * r) * g_ref[...]).astype(h_ref.dtype)
        r_ref[...] = r

    return pl.pallas_call(
        body, grid=(S // ts,), name="rms_fwd",
        in_specs=[pl.BlockSpec((ts, D), lambda i: (i, 0)), pl.BlockSpec((1, D), lambda i: (0, 0))],
        out_specs=[pl.BlockSpec((ts, D), lambda i: (i, 0)), pl.BlockSpec((ts, 1), lambda i: (i, 0))],
        out_shape=[jax.ShapeDtypeStruct((S, D), out_dtype), jax.ShapeDtypeStruct((S, 1), F32)],
        compiler_params=_params(("arbitrary",)),
    )(x, gain)


RMS_BWD_ROWS = 128


def _rms_bwd_tile(first, dh_of, x_ref, r_ref, g_ref, dres_ref, dx_ref, dxb_ref, dg_ref, rows):
    gv = g_ref[...]
    part = None
    for c in range(rows // RMS_BWD_ROWS):
        sl = slice(c * RMS_BWD_ROWS, (c + 1) * RMS_BWD_ROWS)
        rr = r_ref[sl, :]
        xh = x_ref[sl, :] * rr
        dhv = dh_of(sl)
        dxh = dhv * gv
        dx = dres_ref[sl, :] + rr * (dxh - xh * jnp.mean(dxh * xh, axis=-1, keepdims=True))
        dx_ref[sl, :] = dx
        dxb_ref[sl, :] = dx.astype(BF16)
        p = jnp.sum(dhv * xh, axis=0, keepdims=True)
        part = p if part is None else part + p

    @pl.when(first)
    def _():
        dg_ref[...] = part

    @pl.when(jnp.logical_not(first))
    def _():
        dg_ref[...] += part


def matmul_rms_bwd(name, a_ops, b_ops, *, grid, a_spec, b_spec, tm, x, r, gain, dres):
    S, D = x.shape
    n_pairs = len(a_ops)
    nk = grid[1]
    dn = (((1,), (1,)), ((), ()))

    def body(*refs):
        a_refs = refs[:n_pairs]
        b_refs = refs[n_pairs:2 * n_pairs]
        x_ref, r_ref, g_ref, dres_ref, dx_ref, dxb_ref, dg_ref, acc_ref = refs[2 * n_pairs:]
        i = pl.program_id(0)
        k = pl.program_id(1)

        @pl.when(k == 0)
        def _():
            acc_ref[...] = jnp.zeros_like(acc_ref)

        part = None
        for ar, br in zip(a_refs, b_refs):
            d = lax.dot_general(ar[...], br[...], dn, preferred_element_type=F32)
            part = d if part is None else part + d
        acc_ref[...] += part

        @pl.when(k == nk - 1)
        def _():
            _rms_bwd_tile(i == 0, lambda sl: acc_ref[sl, :], x_ref, r_ref, g_ref, dres_ref, dx_ref, dxb_ref,
                          dg_ref, tm)

    row = pl.BlockSpec((tm, D), lambda i, k: (i, 0))
    vec = pl.BlockSpec((1, D), lambda i, k: (0, 0))
    return pl.pallas_call(
        body, grid=grid, name=name,
        in_specs=[a_spec] * n_pairs + [b_spec] * n_pairs
                 + [row, pl.BlockSpec((tm, 1), lambda i, k: (i, 0)), vec, row],
        out_specs=[row, row, vec],
        out_shape=[jax.ShapeDtypeStruct((S, D), F32), jax.ShapeDtypeStruct((S, D), BF16),
                   jax.ShapeDtypeStruct((1, D), F32)],
        scratch_shapes=[pltpu.VMEM((tm, D), F32)],
        compiler_params=_params(("arbitrary", "arbitrary")),
    )(*a_ops, *b_ops, x, r, gain, dres)


def matmul_residual_rms(name, a, b, res, gain, tm=ROW_TILE):
    S, K = a.shape
    N = b.shape[1]

    def body(a_ref, b_ref, res_ref, g_ref, xo_ref, h_ref, r_ref):
        xo = res_ref[...] + jnp.dot(a_ref[...], b_ref[...], preferred_element_type=F32)
        xo_ref[...] = xo
        r = lax.rsqrt(jnp.mean(xo * xo, axis=-1, keepdims=True) + RMS_EPS)
        h_ref[...] = ((xo * r) * g_ref[...]).astype(BF16)
        r_ref[...] = r

    row = pl.BlockSpec((tm, N), lambda i: (i, 0))
    return pl.pallas_call(
        body, grid=(S // tm,), name=name,
        in_specs=[pl.BlockSpec((tm, K), lambda i: (i, 0)), pl.BlockSpec((K, N), lambda i: (0, 0)), row,
                  pl.BlockSpec((1, N), lambda i: (0, 0))],
        out_specs=[row, row, pl.BlockSpec((tm, 1), lambda i: (i, 0))],
        out_shape=[jax.ShapeDtypeStruct((S, N), F32), jax.ShapeDtypeStruct((S, N), BF16),
                   jax.ShapeDtypeStruct((S, 1), F32)],
        compiler_params=_params(("arbitrary",)),
    )(a, b, res, gain)


def loss_and_final_bwd(x, gain, target, ts=ROW_TILE):
    S, D = x.shape

    def body(x_ref, g_ref, t_ref, loss_ref, dx_ref, dxb_ref, dg_ref):
        i = pl.program_id(0)
        xv = x_ref[...]
        rr = lax.rsqrt(jnp.mean(xv * xv, axis=-1, keepdims=True) + RMS_EPS)
        xh = xv * rr
        err = xh * g_ref[...] - t_ref[...]
        part_loss = 0.5 * jnp.sum(jnp.mean(err * err, axis=-1, keepdims=True))
        dy = err / D
        dxh = dy * g_ref[...]
        dx = rr * (dxh - xh * jnp.mean(dxh * xh, axis=-1, keepdims=True))
        dx_ref[...] = dx
        dxb_ref[...] = dx.astype(BF16)
        part = jnp.sum(dy * xh, axis=0, keepdims=True)

        @pl.when(i == 0)
        def _():
            dg_ref[...] = part
            loss_ref[...] = jnp.zeros_like(loss_ref) + part_loss

        @pl.when(i > 0)
        def _():
            dg_ref[...] += part
            loss_ref[...] += part_loss

    row = pl.BlockSpec((ts, D), lambda i: (i, 0))
    vec = pl.BlockSpec((1, D), lambda i: (0, 0))
    return pl.pallas_call(
        body, grid=(S // ts,), name="loss_final",
        in_specs=[row, vec, row],
        out_specs=[pl.BlockSpec((8, 128), lambda i: (0, 0)), row, row, vec],
        out_shape=[jax.ShapeDtypeStruct((8, 128), F32), jax.ShapeDtypeStruct((S, D), F32),
                   jax.ShapeDtypeStruct((S, D), BF16), jax.ShapeDtypeStruct((1, D), F32)],
        compiler_params=_params(("arbitrary",)),
    )(x, gain, target)


def matmul(name, a_ops, b_ops, *, grid, a_spec, b_spec, out_spec, out_shape, out_dtypes, acc_shape,
           trans_a=False, trans_b=False, res=None, res_spec=None, bf16_scale=None, bf16_scale_spec=None):
    n_pairs = len(a_ops)
    n_out = len(out_dtypes)
    nk = grid[-1]
    kaxis = len(grid) - 1
    dn = (((0,) if trans_a else (1,), (1,) if trans_b else (0,)), ((), ()))

    def body(*refs):
        a_refs = refs[:n_pairs]
        b_refs = refs[n_pairs:2 * n_pairs]
        pos = 2 * n_pairs
        res_ref = None
        if res is not None:
            res_ref = refs[pos]
            pos += 1
        scale_ref = None
        if bf16_scale is not None:
            scale_ref = refs[pos]
            pos += 1
        out_refs = refs[pos:pos + n_out]
        acc_ref = refs[pos + n_out]
        k = pl.program_id(kaxis)
        in_place = n_out == 1 and out_dtypes[0] == F32
        target = out_refs[0] if in_place else acc_ref

        def finish(val):
            if res_ref is not None:
                val = val + res_ref[...]
            for o in out_refs:
                if scale_ref is not None and o.dtype == BF16:
                    o[...] = (val * scale_ref[...]).astype(BF16)
                else:
                    o[...] = val.astype(o.dtype)

        if nk > 1:
            @pl.when(k == 0)
            def _():
                if in_place and res_ref is not None:
                    target[...] = res_ref[...]
                else:
                    target[...] = jnp.zeros_like(target)

        part = None
        for ar, br in zip(a_refs, b_refs):
            d = lax.dot_general(ar[...].astype(BF16), br[...].astype(BF16), dn, preferred_element_type=F32)
            part = d if part is None else part + d

        if nk == 1:
            finish(part)
        else:
            target[...] += part
            if not in_place:
                @pl.when(k == nk - 1)
                def _():
                    finish(acc_ref[...])

    in_specs = [a_spec] * n_pairs + [b_spec] * n_pairs
    operands = list(a_ops) + list(b_ops)
    if res is not None:
        in_specs.append(res_spec)
        operands.append(res)
    if bf16_scale is not None:
        in_specs.append(bf16_scale_spec)
        operands.append(bf16_scale)
    return pl.pallas_call(
        body, grid=grid, name=name, in_specs=in_specs,
        out_specs=[out_spec] * n_out,
        out_shape=[jax.ShapeDtypeStruct(out_shape, dt) for dt in out_dtypes],
        scratch_shapes=[pltpu.VMEM(acc_shape, F32)],
        compiler_params=_params(("arbitrary",) * len(grid)),
    )(*operands)


def ffn_gate_up(h, wg, wu, tm=ROW_TILE_WIDE):
    S, D = h.shape
    nb = wg.shape[2]

    def body(h_ref, wg_ref, wu_ref, p_ref, r_ref, a_ref):
        for c in range(2):
            rows = slice(c * (tm // 2), (c + 1) * (tm // 2))
            hv = h_ref[rows, :]
            g = jnp.dot(hv, wg_ref[...], preferred_element_type=F32)
            u = jnp.dot(hv, wu_ref[...], preferred_element_type=F32)
            s = _sigmoid(g)
            p = g * s
            p_ref[rows, :] = p
            r_ref[rows, :] = u * (s * (1.0 + g * (1.0 - s)))
            a_ref[rows, :] = (p * u).astype(BF16)

    wspec = pl.BlockSpec((None, D, nb), lambda j, i: (j, 0, 0))
    ospec = pl.BlockSpec((None, tm, nb), lambda j, i: (j, i, 0))
    return pl.pallas_call(
        body, grid=(N_DEV, S // tm), name="ffn_gate_up",
        in_specs=[pl.BlockSpec((tm, D), lambda j, i: (i, 0)), wspec, wspec],
        out_specs=[ospec, ospec, ospec],
        out_shape=[jax.ShapeDtypeStruct((N_DEV, S, nb), F32), jax.ShapeDtypeStruct((N_DEV, S, nb), F32),
                   jax.ShapeDtypeStruct((N_DEV, S, nb), BF16)],
        compiler_params=_params(("arbitrary", "arbitrary")),
    )(h, wg, wu)


def ffn_bwd_hidden(dy, wd, p, r, tm=ROW_TILE_WIDE):
    S, D = dy.shape
    nb = wd.shape[1]

    def body(dy_ref, wd_ref, p_ref, r_ref, dg_ref, du_ref):
        for c in range(2):
            rows = slice(c * (tm // 2), (c + 1) * (tm // 2))
            da = lax.dot_general(dy_ref[rows, :], wd_ref[...], (((1,), (1,)), ((), ())),
                                 preferred_element_type=F32)
            du_ref[rows, :] = (da * p_ref[rows, :]).astype(BF16)
            dg_ref[rows, :] = (da * r_ref[rows, :]).astype(BF16)

    hspec = pl.BlockSpec((None, tm, nb), lambda j, i: (j, i, 0))
    return pl.pallas_call(
        body, grid=(N_DEV, S // tm), name="ffn_bwd_hidden",
        in_specs=[pl.BlockSpec((tm, D), lambda j, i: (i, 0)), pl.BlockSpec((None, nb, D), lambda j, i: (j, 0, 0)),
                  hspec, hspec],
        out_specs=[hspec, hspec],
        out_shape=[jax.ShapeDtypeStruct((N_DEV, S, nb), BF16), jax.ShapeDtypeStruct((N_DEV, S, nb), BF16)],
        compiler_params=_params(("arbitrary", "arbitrary")),
    )(dy, wd, p, r)


def ffn_forward(h, xres, wg, wu, wd, tm=ROW_TILE_WIDE):
    S, D = h.shape
    nb = wg.shape[2]
    g, u, a = ffn_gate_up(h, wg, wu)
    (xo,) = matmul(
        "ffn_down", [a], [wd], grid=(S // tm, N_DEV),
        a_spec=pl.BlockSpec((None, tm, nb), lambda i, j: (j, i, 0)),
        b_spec=pl.BlockSpec((None, nb, D), lambda i, j: (j, 0, 0)),
        out_spec=pl.BlockSpec((tm, D), lambda i, j: (i, 0)), out_shape=(S, D), out_dtypes=[F32],
        acc_shape=(tm, D), res=xres, res_spec=pl.BlockSpec((tm, D), lambda i, j: (i, 0)))
    return xo, (g, u, a)


def ffn_backward(dy_b, h, saved, wg, wu, wd, x, r, gain, dres, tm=ROW_TILE, tk=REDUCE_TILE):
    S, D = h.shape
    nb = wg.shape[2]
    g, u, a = saved
    dg, du = ffn_bwd_hidden(dy_b, wd, g, u)
    dx = matmul_rms_bwd(
        "ffn_dh", [dg, du], [wg, wu], grid=(S // tm, N_DEV),
        a_spec=pl.BlockSpec((None, tm, nb), lambda i, j: (j, i, 0)),
        b_spec=pl.BlockSpec((None, D, nb), lambda i, j: (j, 0, 0)),
        tm=tm, x=x, r=r, gain=gain, dres=dres)

    def wgrad_in(name, dhid):
        (dw,) = matmul(
            name, [dhid], [h], grid=(N_DEV, S // tk),
            a_spec=pl.BlockSpec((None, tk, nb), lambda j, k: (j, k, 0)),
            b_spec=pl.BlockSpec((tk, D), lambda j, k: (k, 0)),
            out_spec=pl.BlockSpec((None, nb, D), lambda j, k: (j, 0, 0)), out_shape=(N_DEV, nb, D),
            out_dtypes=[BF16], acc_shape=(nb, D), trans_a=True)
        return dw

    dwg = wgrad_in("ffn_dwg", dg)
    dwu = wgrad_in("ffn_dwu", du)
    (dwd,) = matmul(
        "ffn_dwd", [a], [dy_b], grid=(N_DEV, S // tk),
        a_spec=pl.BlockSpec((None, tk, nb), lambda j, k: (j, k, 0)),
        b_spec=pl.BlockSpec((tk, D), lambda j, k: (k, 0)),
        out_spec=pl.BlockSpec((None, nb, D), lambda j, k: (j, 0, 0)), out_shape=(N_DEV, nb, D),
        out_dtypes=[BF16], acc_shape=(nb, D), trans_a=True)
    return dx, dwg, dwu, dwd


def _pool_counts(row0, n, w):
    pos = row0 + lax.broadcasted_iota(jnp.int32, (n, 1), 0)
    return jnp.minimum(pos + 1, w).astype(F32)


def pool_forward(h, xres, w, scale, gain_next, ts=POOL_TILE):
    S, D = h.shape
    G = len(POOL_WINDOWS)
    P = D // G
    hb = ts // POOL_HALO

    def body(h_ref, halo_ref, x_ref, w_ref, s_ref, gn_ref, xo_ref, p_ref, hn_ref, rn_ref):
        i = pl.program_id(0)
        for gi, win in enumerate(POOL_WINDOWS):
            cols = slice(gi * P, (gi + 1) * P)
            cur = h_ref[:, cols]
            halo = jnp.where(i > 0, halo_ref[:, cols], 0.0)
            acc = jnp.concatenate([halo, cur], axis=0)
            step = 1
            while step < win:
                acc = acc + pltpu.roll(acc, step, 0)
                step *= 2
            wsum = acc[POOL_HALO:, :]
            pooled = wsum / _pool_counts(i * ts, ts, win) - cur
            pb = pooled.astype(BF16)
            p_ref[:, cols] = pb
            mixed = jnp.dot(pb, w_ref[gi], preferred_element_type=F32)
            xo_ref[:, cols] = x_ref[:, cols] + mixed * s_ref[:, cols]
        xo = xo_ref[...]
        r = lax.rsqrt(jnp.mean(xo * xo, axis=-1, keepdims=True) + RMS_EPS)
        hn_ref[...] = ((xo * r) * gn_ref[...]).astype(BF16)
        rn_ref[...] = r

    row = pl.BlockSpec((ts, D), lambda i: (i, 0))
    vec = pl.BlockSpec((1, D), lambda i: (0, 0))
    return pl.pallas_call(
        body, grid=(S // ts,), name="pool_fwd",
        in_specs=[row, pl.BlockSpec((POOL_HALO, D), lambda i: (jnp.maximum(i * hb - 1, 0), 0)), row,
                  pl.BlockSpec((G, P, P), lambda i: (0, 0, 0)), vec, vec],
        out_specs=[row, row, row, pl.BlockSpec((ts, 1), lambda i: (i, 0))],
        out_shape=[jax.ShapeDtypeStruct((S, D), F32), jax.ShapeDtypeStruct((S, D), BF16),
                   jax.ShapeDtypeStruct((S, D), BF16), jax.ShapeDtypeStruct((S, 1), F32)],
        compiler_params=_params(("arbitrary",)),
    )(h, h, xres, w, scale, gain_next)


def pool_backward_mix(dx, pooled, w, scale, ts=POOL_TILE):
    S, D = dx.shape
    G = len(POOL_WINDOWS)
    P = D // G

    def body(dx_ref, p_ref, w_ref, s_ref, dm_ref, dp_ref, ds_ref):
        i = pl.program_id(0)
        parts = []
        for gi in range(G):
            cols = slice(gi * P, (gi + 1) * P)
            dxv = dx_ref[:, cols]
            dmb = (dxv * s_ref[:, cols]).astype(BF16)
            dm_ref[:, cols] = dmb
            dp_ref[:, cols] = lax.dot_general(dmb, w_ref[gi], (((1,), (1,)), ((), ())),
                                              preferred_element_type=F32)
            mixed = jnp.dot(p_ref[:, cols], w_ref[gi], preferred_element_type=F32)
            parts.append(jnp.sum(dxv * mixed, axis=0, keepdims=True))
        part = jnp.concatenate(parts, axis=1)

        @pl.when(i == 0)
        def _():
            ds_ref[...] = part

        @pl.when(i > 0)
        def _():
            ds_ref[...] += part

    row = pl.BlockSpec((ts, D), lambda i: (i, 0))
    vec = pl.BlockSpec((1, D), lambda i: (0, 0))
    return pl.pallas_call(
        body, grid=(S // ts,), name="pool_bwd_mix",
        in_specs=[row, row, pl.BlockSpec((G, P, P), lambda i: (0, 0, 0)), vec],
        out_specs=[row, row, vec],
        out_shape=[jax.ShapeDtypeStruct((S, D), BF16), jax.ShapeDtypeStruct((S, D), F32),
                   jax.ShapeDtypeStruct((1, D), F32)],
        compiler_params=_params(("arbitrary",)),
    )(dx, pooled, w, scale)


def pool_backward_window(dp, x, r, gain, dres, ts=POOL_TILE):
    S, D = dp.shape
    G = len(POOL_WINDOWS)
    P = D // G
    hb = ts // POOL_HALO
    n_i = S // ts
    n_rows = ts + POOL_HALO

    def body(dp_ref, halo_ref, x_ref, r_ref, g_ref, dres_ref, dx_ref, dxb_ref, dg_ref, dh_ref):
        i = pl.program_id(0)
        for gi, win in enumerate(POOL_WINDOWS):
            cols = slice(gi * P, (gi + 1) * P)
            cur = dp_ref[:, cols]
            halo = jnp.where(i < n_i - 1, halo_ref[:, cols], 0.0)
            acc = jnp.concatenate([cur / _pool_counts(i * ts, ts, win),
                                   halo / _pool_counts((i + 1) * ts, POOL_HALO, win)], axis=0)
            step = 1
            while step < win:
                acc = acc + pltpu.roll(acc, n_rows - step, 0)
                step *= 2
            dh_ref[:, cols] = acc[:ts, :] - cur
        _rms_bwd_tile(i == 0, lambda sl: dh_ref[sl, :], x_ref, r_ref, g_ref, dres_ref, dx_ref, dxb_ref, dg_ref, ts)

    row = pl.BlockSpec((ts, D), lambda i: (i, 0))
    vec = pl.BlockSpec((1, D), lambda i: (0, 0))
    return pl.pallas_call(
        body, grid=(n_i,), name="pool_bwd_window",
        in_specs=[row, pl.BlockSpec((POOL_HALO, D), lambda i: (jnp.minimum((i + 1) * hb, S // POOL_HALO - 1), 0)),
                  row, pl.BlockSpec((ts, 1), lambda i: (i, 0)), vec, row],
        out_specs=[row, row, vec],
        out_shape=[jax.ShapeDtypeStruct((S, D), F32), jax.ShapeDtypeStruct((S, D), BF16),
                   jax.ShapeDtypeStruct((1, D), F32)],
        scratch_shapes=[pltpu.VMEM((ts, D), F32)],
        compiler_params=_params(("arbitrary",)),
    )(dp, dp, x, r, gain, dres)


_HG_LEVELS = (32, 16, 8, 4, 2, 1)
_N_LEV = len(_HG_LEVELS) + 1


def _hgrn_constants():
    C = HG_CHUNK
    t = np.arange(C)
    tri = (t[None, :] <= t[:, None]).astype(np.float32)
    blocks = [tri]
    masks, upq, upk = [], [], []
    for m in _HG_LEVELS:
        p = (t // (2 * m)) * 2 * m + m - 1
        blocks.append(tri[p])
        masks.append(((t[:, None] // (2 * m)) == (t[None, :] // (2 * m))).astype(np.float32))
        upper = (t % (2 * m)) >= m
        upq.append(np.repeat(upper[:, None], HEAD, 1).astype(np.float32))
        upk.append(np.repeat(~upper[:, None], HEAD, 1).astype(np.float32))
    blocks.append(tri)
    masks.append(np.eye(C, dtype=np.float32))
    upq.append(np.ones((C, HEAD), np.float32))
    upk.append(np.ones((C, HEAD), np.float32))
    mstack = np.concatenate(blocks, axis=0)
    mstack3 = np.concatenate([mstack] * 3, axis=1)
    trirev3 = np.concatenate([tri.T] * 3, axis=1)
    return (jnp.asarray(mstack3, BF16), jnp.asarray(np.stack(masks)), jnp.asarray(np.stack(upq)),
            jnp.asarray(np.stack(upk)), jnp.asarray(trirev3, BF16))


def _split3(x):
    hi = x.astype(BF16)
    r1 = x - hi.astype(F32)
    mid = r1.astype(BF16)
    lo = (r1 - mid.astype(F32)).astype(BF16)
    return jnp.concatenate([hi, mid, lo], axis=0)


def _hgrn_chunk_common(qa, fa, lbv, mstack3, upq, upk):
    sq = _sigmoid(qa)
    q = qa * sq
    sf = _sigmoid(fa)
    f = lbv + (1.0 - lbv) * sf
    g = jnp.log(f)
    k = 1.0 - f
    gall = jnp.dot(mstack3, _split3(g), preferred_element_type=F32).reshape(_N_LEV + 1, HG_CHUNK, HEAD)
    G = gall[0]
    eq_exp = G[None] - gall[1:]
    eq = jnp.exp(jnp.minimum(eq_exp, 0.0)) * upq
    ek = jnp.exp(jnp.minimum(-eq_exp, 0.0)) * upk
    Qs = (q[None] * eq).astype(BF16)
    Ks = (k[None] * ek).astype(BF16)
    return sq, q, sf, f, k, G, eq, ek, Qs, Ks


def hgrn_forward(proj, lb, hg_norm, ts=HG_TILE):
    S = proj.shape[0]
    nh = lb.shape[1] // HEAD
    C = HG_CHUNK
    ncs = ts // C
    mstack3, masks, upq, upk, _ = _hgrn_constants()

    def body(qa_ref, fa_ref, ia_ref, ga_ref, lb_ref, gn_ref, ms_ref, mk_ref, uq_ref, uk_ref,
             oa_ref, oraw_ref, st_ref, state):
        tt = pl.program_id(1)

        @pl.when(tt == 0)
        def _():
            state[...] = jnp.zeros_like(state)

        gn = gn_ref[...]

        def chunk(c, carry):
            sl = pl.ds(pl.multiple_of(c * C, C), C)
            for hh in range(HG_HEADS_PER_BLOCK):
                cols = slice(hh * HEAD, (hh + 1) * HEAD)
                qa, fa, v, ga = qa_ref[sl, cols], fa_ref[sl, cols], ia_ref[sl, cols], ga_ref[sl, cols]
                _, q, _, _, k, G, _, _, Qs, Ks = _hgrn_chunk_common(qa, fa, lb_ref[:, cols], ms_ref[...],
                                                                    uq_ref[...], uk_ref[...])
                att7 = lax.dot_general(Qs, Ks, (((2,), (2,)), ((0,), (0,))), preferred_element_type=F32)
                att = jnp.sum(att7 * mk_ref[...], axis=0)
                st = state[hh]
                st_ref[hh, c] = st
                vb = v.astype(BF16)
                qg = (q * jnp.exp(G)).astype(BF16)
                o = jnp.dot(att.astype(BF16), vb, preferred_element_type=F32)
                o = o + lax.dot_general(qg, st.astype(BF16), (((1,), (1,)), ((), ())),
                                        preferred_element_type=F32)
                g_last = G[C - 1:C, :]
                kh = (k * jnp.exp(g_last - G)).astype(BF16)
                state[hh] = st * jnp.exp(g_last) + lax.dot_general(vb, kh, (((0,), (0,)), ((), ())),
                                                                   preferred_element_type=F32)
                oraw_ref[sl, cols] = o
                r = lax.rsqrt(jnp.mean(o * o, axis=-1, keepdims=True) + RMS_EPS)
                oa_ref[sl, cols] = (((o * r) * gn) * (ga * _sigmoid(ga))).astype(BF16)
            return carry

        lax.fori_loop(0, ncs, chunk, 0)

    hpb = HG_HEADS_PER_BLOCK
    wide = hpb * HEAD

    def col(m0):
        return pl.BlockSpec((ts, wide), lambda h, t: (t, m0 // hpb + h))

    const3 = lambda shape: pl.BlockSpec(shape, lambda h, t: (0, 0, 0))
    return pl.pallas_call(
        body, grid=(nh // hpb, S // ts), name="hgrn_fwd",
        in_specs=[col(0), col(nh), col(2 * nh), col(3 * nh),
                  pl.BlockSpec((1, wide), lambda h, t: (0, h)), pl.BlockSpec((1, HEAD), lambda h, t: (0, 0)),
                  pl.BlockSpec(mstack3.shape, lambda h, t: (0, 0)), const3(masks.shape), const3(upq.shape),
                  const3(upk.shape)],
        out_specs=[pl.BlockSpec((ts, wide), lambda h, t: (t, h)), pl.BlockSpec((ts, wide), lambda h, t: (t, h)),
                   pl.BlockSpec((hpb, ncs, HEAD, HEAD), lambda h, t: (h, t, 0, 0))],
        out_shape=[jax.ShapeDtypeStruct((S, nh * HEAD), BF16), jax.ShapeDtypeStruct((S, nh * HEAD), F32),
                   jax.ShapeDtypeStruct((nh, S // C, HEAD, HEAD), F32)],
        scratch_shapes=[pltpu.VMEM((hpb, HEAD, HEAD), F32)],
        compiler_params=_params(("arbitrary", "arbitrary")),
    )(proj, proj, proj, proj, lb, hg_norm, mstack3, masks, upq, upk)


def hgrn_backward(dcat, proj, oraw, states, lb, hg_norm, ts=HG_TILE):
    S = proj.shape[0]
    nh = lb.shape[1] // HEAD
    C = HG_CHUNK
    ncs = ts // C
    nt = S // ts
    mstack3, masks, upq, upk, trirev3 = _hgrn_constants()

    def body(do_ref, qa_ref, fa_ref, ia_ref, ga_ref, or_ref, st_ref, lb_ref, gn_ref, ms_ref, mk_ref, uq_ref,
             uk_ref, tr_ref, dqa_ref, dfa_ref, dia_ref, dga_ref, dlb_ref, dgn_ref, dstate):
        tt = pl.program_id(1)

        @pl.when(tt == 0)
        def _():
            dstate[...] = jnp.zeros_like(dstate)
            dlb_ref[...] = jnp.zeros_like(dlb_ref)
            dgn_ref[...] = jnp.zeros_like(dgn_ref)

        gn = gn_ref[...]

        def chunk(cc, carry):
            c = ncs - 1 - cc
            sl = pl.ds(pl.multiple_of(c * C, C), C)
            for hh in range(HG_HEADS_PER_BLOCK):
                cols = slice(hh * HEAD, (hh + 1) * HEAD)
                lbv = lb_ref[:, cols]
                qa, fa, v, ga = qa_ref[sl, cols], fa_ref[sl, cols], ia_ref[sl, cols], ga_ref[sl, cols]
                sq, q, sf, f, k, G, eq, ek, Qs, Ks = _hgrn_chunk_common(qa, fa, lbv, ms_ref[...], uq_ref[...],
                                                                        uk_ref[...])
                mk = mk_ref[...]
                att7 = lax.dot_general(Qs, Ks, (((2,), (2,)), ((0,), (0,))), preferred_element_type=F32)
                att = jnp.sum(att7 * mk, axis=0)
                o = or_ref[sl, cols]
                dO = do_ref[sl, cols]
                sg = _sigmoid(ga)
                r = lax.rsqrt(jnp.mean(o * o, axis=-1, keepdims=True) + RMS_EPS)
                xh = o * r
                dga_ref[sl, cols] = (dO * (xh * gn) * (sg * (1.0 + ga * (1.0 - sg)))).astype(BF16)
                don = dO * (ga * sg)
                dgn_ref[hh] += jnp.sum(don * xh, axis=0, keepdims=True)
                dxh = don * gn
                do = r * (dxh - xh * jnp.mean(dxh * xh, axis=-1, keepdims=True))
                dob = do.astype(BF16)
                st = st_ref[hh, c]
                dst = dstate[hh]
                dstb = dst.astype(BF16)
                vb = v.astype(BF16)
                eG = jnp.exp(G)
                g_last = G[C - 1:C, :]
                e_last = jnp.exp(g_last)
                e_tail = jnp.exp(g_last - G)
                qg = (q * eG).astype(BF16)
                kh = (k * e_tail).astype(BF16)
                dq_inter = jnp.dot(dob, st.astype(BF16), preferred_element_type=F32) * eG
                dk_inter = jnp.dot(vb, dstb, preferred_element_type=F32) * e_tail
                dv = lax.dot_general(kh, dstb, (((1,), (1,)), ((), ())), preferred_element_type=F32)
                dv = dv + lax.dot_general(att.astype(BF16), dob, (((0,), (0,)), ((), ())),
                                          preferred_element_type=F32)
                dA = lax.dot_general(dob, vb, (((1,), (1,)), ((), ())), preferred_element_type=F32)
                dA7 = (dA[None] * mk).astype(BF16)
                dAT7 = (dA.T[None] * mk).astype(BF16)
                dQs = lax.dot_general(dA7, Ks, (((2,), (1,)), ((0,), (0,))), preferred_element_type=F32)
                dKs = lax.dot_general(dAT7, Qs, (((2,), (1,)), ((0,), (0,))), preferred_element_type=F32)
                dq = dq_inter + jnp.sum(dQs * eq, axis=0)
                dk = dk_inter + jnp.sum(dKs * ek, axis=0)
                dG = (jnp.sum(Qs.astype(F32) * dQs - Ks.astype(F32) * dKs, axis=0)
                      + q * dq_inter - k * dk_inter)
                last_extra = (jnp.sum(k * dk_inter, axis=0, keepdims=True)
                              + e_last * jnp.sum(dst * st, axis=0, keepdims=True))
                is_last = lax.broadcasted_iota(jnp.int32, (C, 1), 0) == C - 1
                dG = dG + jnp.where(is_last, last_extra, 0.0)
                dg = jnp.dot(tr_ref[...], _split3(dG), preferred_element_type=F32)
                df = dg / f - dk
                dfa_ref[sl, cols] = (df * (1.0 - lbv) * (sf * (1.0 - sf))).astype(BF16)
                dlb_ref[:, cols] += jnp.sum(df * (1.0 - sf), axis=0, keepdims=True)
                dqa_ref[sl, cols] = (dq * (sq * (1.0 + qa * (1.0 - sq)))).astype(BF16)
                dia_ref[sl, cols] = dv.astype(BF16)
                dstate[hh] = dst * e_last + lax.dot_general(dob, qg, (((0,), (0,)), ((), ())),
                                                            preferred_element_type=F32)
            return carry

        lax.fori_loop(0, ncs, chunk, 0)

    hpb = HG_HEADS_PER_BLOCK
    wide = hpb * HEAD

    def col(m0):
        return pl.BlockSpec((ts, wide), lambda h, t: (nt - 1 - t, m0 // hpb + h))

    const3 = lambda shape: pl.BlockSpec(shape, lambda h, t: (0, 0, 0))
    const2 = lambda shape: pl.BlockSpec(shape, lambda h, t: (0, 0))
    ocol = pl.BlockSpec((ts, wide), lambda h, t: (nt - 1 - t, h))
    half = nh * HEAD
    return pl.pallas_call(
        body, grid=(nh // hpb, nt), name="hgrn_bwd",
        in_specs=[col(0), col(0), col(nh), col(2 * nh), col(3 * nh), col(0),
                  pl.BlockSpec((hpb, ncs, HEAD, HEAD), lambda h, t: (h, nt - 1 - t, 0, 0)),
                  pl.BlockSpec((1, wide), lambda h, t: (0, h)), const2((1, HEAD)),
                  const2(mstack3.shape), const3(masks.shape), const3(upq.shape), const3(upk.shape),
                  const2(trirev3.shape)],
        out_specs=[ocol, ocol, ocol, ocol, pl.BlockSpec((1, wide), lambda h, t: (0, h)),
                   pl.BlockSpec((hpb, 1, HEAD), lambda h, t: (h, 0, 0))],
        out_shape=[jax.ShapeDtypeStruct((S, half), BF16)] * 4
                  + [jax.ShapeDtypeStruct((1, half), F32), jax.ShapeDtypeStruct((nh, 1, HEAD), F32)],
        scratch_shapes=[pltpu.VMEM((hpb, HEAD, HEAD), F32)],
        compiler_params=_params(("arbitrary", "arbitrary")),
    )(dcat, proj, proj, proj, proj, oraw, states, lb, hg_norm, mstack3, masks, upq, upk, trirev3)


SB_SUB = 128
LOG2_E = 1.4426950408889634
SB_SCALE = 1.0 / math.sqrt(HEAD)
SB_QUERY_SCALE = SB_SCALE * LOG2_E


def _split2(x):
    hi = x.astype(BF16)
    lo = (x - hi.astype(F32)).astype(BF16)
    return jnp.concatenate([hi, lo], axis=1)


def _sb_constants():
    j = np.arange(SB_SUB)
    after = (j[:, None] > j[None, :]).astype(np.float32)
    before = (j[:, None] < j[None, :]).astype(np.float32)
    return (jnp.asarray(np.concatenate([after, after], axis=0), BF16),
            jnp.asarray(np.concatenate([before, before], axis=0), BF16))


def _sb_tri(i):
    return (i * (i + 1)) // 2


def _sb_diag_mask(t):
    return lax.broadcasted_iota(jnp.int32, (t, t), 1) < lax.broadcasted_iota(jnp.int32, (t, t), 0)


def _sb_scores(q, k_ref, col0, t):
    ks = k_ref[pl.ds(pl.multiple_of(col0, t), t), :]
    return lax.dot_general(q, ks, (((1,), (1,)), ((), ())), preferred_element_type=F32)


def _sb_weights(z, mask, run, after2):
    nsub = z.shape[1] // SB_SUB
    nz = -z
    lk = jnp.minimum(nz, 0.0) - jnp.log(1.0 + jnp.exp2(jnp.minimum(z, nz))) * LOG2_E
    if mask is not None:
        lk = jnp.where(mask, lk, 0.0)
    locs, tots = [], []
    for b in range(nsub):
        lkb = lk[:, b * SB_SUB:(b + 1) * SB_SUB]
        loc = jnp.dot(_split2(lkb), after2, preferred_element_type=F32)
        locs.append(loc)
        tots.append(loc[:, 0:1] + lkb[:, 0:1])
    ws = [None] * nsub
    for b in reversed(range(nsub)):
        sl = slice(b * SB_SUB, (b + 1) * SB_SUB)
        ws[b] = jnp.exp2(z[:, sl] + lk[:, sl] + (locs[b] + run))
        run = run + tots[b]
    w = jnp.concatenate(ws, axis=1)
    if mask is not None:
        w = jnp.where(mask, w, 0.0)
    return w, run


def sb_forward(projb, nh, m0, t=SB_TILE):
    S = projb.shape[0]
    after2, _ = _sb_constants()
    n_i = S // t

    def body(q_ref, k_ref, v_ref, af_ref, o_ref, w_hbm, wbuf, wsem):
        h = pl.program_id(0)
        i = pl.program_id(1)
        q = q_ref[...]
        after = af_ref[...]
        base = _sb_tri(i)

        def store(slot, jb):
            return pltpu.make_async_copy(wbuf.at[slot], w_hbm.at[h, base + jb], wsem.at[slot])

        def block(n, jb, run, mask):
            slot = n % 2

            @pl.when(n >= 2)
            def _():
                store(slot, jb).wait()

            z = _sb_scores(q, k_ref, jb * t, t)
            w, run = _sb_weights(z, mask, run, after)
            wb = w.astype(BF16)
            wbuf[slot] = wb
            store(slot, jb).start()
            vs = v_ref[pl.ds(pl.multiple_of(jb * t, t), t), :]
            return run, jnp.dot(wb, vs, preferred_element_type=F32)

        run, acc = block(0, i, jnp.zeros((t, 1), F32), _sb_diag_mask(t))

        def step(n, carry):
            run, acc = carry
            run, part = block(n + 1, i - 1 - n, run, None)
            return run, acc + part

        _, acc = lax.fori_loop(0, i, step, (run, acc))
        o_ref[...] = acc.astype(BF16)
        store(i % 2, 0).wait()

        @pl.when(i >= 1)
        def _():
            store((i + 1) % 2, 0).wait()

    return pl.pallas_call(
        body, grid=(nh, n_i), name="sb_fwd",
        in_specs=[pl.BlockSpec((t, HEAD), lambda h, i: (i, m0 + h)),
                  pl.BlockSpec((S, HEAD), lambda h, i: (0, m0 + nh + h)),
                  pl.BlockSpec((S, HEAD), lambda h, i: (0, m0 + 2 * nh + h)),
                  pl.BlockSpec(after2.shape, lambda h, i: (0, 0))],
        out_specs=[pl.BlockSpec((t, HEAD), lambda h, i: (i, h)), pl.BlockSpec(memory_space=pl.ANY)],
        out_shape=[jax.ShapeDtypeStruct((S, nh * HEAD), BF16),
                   jax.ShapeDtypeStruct((nh, _sb_tri(n_i), t, t), BF16)],
        scratch_shapes=[pltpu.VMEM((2, t, t), BF16), pltpu.SemaphoreType.DMA((2,))],
        compiler_params=_params(("arbitrary", "arbitrary")),
    )(projb, projb, projb, after2)


def sb_backward(dcat, projb, w_all, nh, m0, t=SB_TILE):
    S = projb.shape[0]
    _, before2 = _sb_constants()
    n_i = S // t
    nsub = t // SB_SUB

    def body(do_ref, q_ref, k_ref, v_ref, bf_ref, w_hbm, dq_ref, dk_ref, dv_ref, dk_acc, dv_acc, wbuf, wsem):
        h = pl.program_id(0)
        i = pl.program_id(1)

        @pl.when(i == 0)
        def _():
            dk_acc[...] = jnp.zeros_like(dk_acc)
            dv_acc[...] = jnp.zeros_like(dv_acc)

        q = q_ref[...]
        dob = do_ref[...].astype(BF16)
        before = bf_ref[...]
        base = _sb_tri(i)

        def load(slot, jb):
            return pltpu.make_async_copy(w_hbm.at[h, base + jb], wbuf.at[slot], wsem.at[slot])

        load(0, 0).start()

        def left_to_right(jb, run, dq, mask):
            slot = jb % 2
            load(slot, jb).wait()

            @pl.when(jb < i)
            def _():
                load(1 - slot, jb + 1).start()

            ksl = pl.ds(pl.multiple_of(jb * t, t), t)
            wb = wbuf[slot]
            z = _sb_scores(q, k_ref, jb * t, t)
            dw = lax.dot_general(dob, v_ref[ksl, :], (((1,), (1,)), ((), ())), preferred_element_type=F32)
            d = dw * wb.astype(F32)
            dv_acc[ksl, :] += lax.dot_general(wb, dob, (((0,), (0,)), ((), ())), preferred_element_type=F32)
            sig = 1.0 / (1.0 + jnp.exp2(-z))
            das = []
            for b in range(nsub):
                db = d[:, b * SB_SUB:(b + 1) * SB_SUB]
                prefix = run + jnp.dot(_split2(db), before, preferred_element_type=F32)
                das.append(db - sig[:, b * SB_SUB:(b + 1) * SB_SUB] * (db + prefix))
                run = prefix[:, SB_SUB - 1:SB_SUB] + db[:, SB_SUB - 1:SB_SUB]
            da = jnp.concatenate(das, axis=1)
            if mask is not None:
                da = jnp.where(mask, da, 0.0)
            dab = (da * SB_SCALE).astype(BF16)
            dq = dq + jnp.dot(dab, k_ref[ksl, :], preferred_element_type=F32)
            dk_acc[ksl, :] += lax.dot_general(dab, q, (((0,), (0,)), ((), ())), preferred_element_type=F32)
            return run, dq

        run, dq = lax.fori_loop(0, i, lambda jb, c: left_to_right(jb, c[0], c[1], None),
                                (jnp.zeros((t, 1), F32), jnp.zeros((t, HEAD), F32)))
        _, dq = left_to_right(i, run, dq, _sb_diag_mask(t))
        dq_ref[...] = dq.astype(BF16)

        @pl.when(i == n_i - 1)
        def _():
            dk_ref[...] = (dk_acc[...] * (1.0 / SB_QUERY_SCALE)).astype(BF16)
            dv_ref[...] = dv_acc[...].astype(BF16)

    half = nh * HEAD
    full = pl.BlockSpec((S, HEAD), lambda h, i: (0, h))
    return pl.pallas_call(
        body, grid=(nh, n_i), name="sb_bwd",
        in_specs=[pl.BlockSpec((t, HEAD), lambda h, i: (i, nh + h)),
                  pl.BlockSpec((t, HEAD), lambda h, i: (i, m0 + h)),
                  pl.BlockSpec((S, HEAD), lambda h, i: (0, m0 + nh + h)),
                  pl.BlockSpec((S, HEAD), lambda h, i: (0, m0 + 2 * nh + h)),
                  pl.BlockSpec(before2.shape, lambda h, i: (0, 0)), pl.BlockSpec(memory_space=pl.ANY)],
        out_specs=[pl.BlockSpec((t, HEAD), lambda h, i: (i, h)), full, full],
        out_shape=[jax.ShapeDtypeStruct((S, half), BF16)] * 3,
        scratch_shapes=[pltpu.VMEM((S, HEAD), F32), pltpu.VMEM((S, HEAD), F32),
                        pltpu.VMEM((2, t, t), BF16), pltpu.SemaphoreType.DMA((2,))],
        compiler_params=_params(("arbitrary", "arbitrary")),
    )(dcat, projb, projb, projb, before2, w_all)


def local_step(x, target, mix_norm, ffn_norm, final_norm, lb_logits, hg_norm, get_w_in, get_w_rest, send):
    S, D = x.shape
    half = D // 2
    nh = half // HEAD
    tm = ROW_TILE
    tk = REDUCE_TILE
    row = lambda i, j: (i, 0)

    lb = jax.nn.softmax(lb_logits, axis=0)[0:1]

    h0, r0 = rms_fwd(x, mix_norm[0:1], BF16)
    w_in = get_w_in(h0)
    nbi = w_in.shape[2]
    col = jnp.arange(N_DEV * nbi) // half
    col_scale = jnp.where(col == 4, SB_QUERY_SCALE, 1.0).astype(F32)[None]
    proj, projb = matmul(
        "proj_in", [h0], [w_in], grid=(N_DEV, S // ROW_TILE_WIDE, 1),
        a_spec=pl.BlockSpec((ROW_TILE_WIDE, D), lambda j, i, k: (i, 0)),
        b_spec=pl.BlockSpec((None, D, nbi), lambda j, i, k: (j, 0, 0)),
        out_spec=pl.BlockSpec((ROW_TILE_WIDE, nbi), lambda j, i, k: (i, j)), out_shape=(S, N_DEV * nbi),
        out_dtypes=[F32, BF16], acc_shape=(8, 128),
        bf16_scale=col_scale, bf16_scale_spec=pl.BlockSpec((1, nbi), lambda j, i, k: (0, j)))
    oa, oraw, states = hgrn_forward(proj, lb, hg_norm)
    ob, sb_weights = sb_forward(projb, nh, 4 * nh)
    cat = jnp.concatenate([oa, ob], axis=1)
    w_out, pool_w, pool_scale, wg, wu, wd = get_w_rest(cat)
    x1, h1, r1 = matmul_residual_rms("mix_out", cat, w_out, x, ffn_norm[0:1])
    x2, ffn0 = ffn_forward(h1, x1, wg[0], wu[0], wd[0])

    h2, r2 = rms_fwd(x2, mix_norm[1:2], F32)
    x3, pooled, h3, r3 = pool_forward(h2, x2, pool_w, pool_scale, ffn_norm[1:2])
    x4, ffn1 = ffn_forward(h3, x3, wg[1], wu[1], wd[1])

    loss_blk, dx4, dx4b, d_final = loss_and_final_bwd(x4, final_norm, target)

    (dx3, _, d_ffn1), dwg1, dwu1, dwd1 = ffn_backward(dx4b, h3, ffn1, wg[1], wu[1], wd[1],
                                                      x3, r3, ffn_norm[1:2], dx4)
    dx3 = send("ffn1", dict(ffn_w_gate_1=dwg1, ffn_w_up_1=dwu1, ffn_w_down_1=dwd1), dx3)
    dmixed, dpooled, d_pscale = pool_backward_mix(dx3, pooled, pool_w, pool_scale)
    G = len(POOL_WINDOWS)
    P = D // G
    (d_pool_w,) = matmul(
        "pool_dw", [pooled], [dmixed], grid=(G, S // tk),
        a_spec=pl.BlockSpec((tk, P), lambda g, k: (k, g)), b_spec=pl.BlockSpec((tk, P), lambda g, k: (k, g)),
        out_spec=pl.BlockSpec((None, P, P), lambda g, k: (g, 0, 0)), out_shape=(G, P, P), out_dtypes=[BF16],
        acc_shape=(P, P), trans_a=True)
    dx2, dx2b, d_mix1 = pool_backward_window(dpooled, x2, r2, mix_norm[1:2], dx3)

    (dx1, dx1b, d_ffn0), dwg0, dwu0, dwd0 = ffn_backward(dx2b, h1, ffn0, wg[0], wu[0], wd[0],
                                                         x1, r1, ffn_norm[0:1], dx2)
    (dcat,) = matmul(
        "mix_out_dx", [dx1b], [w_out], grid=(S // tm, 1),
        a_spec=pl.BlockSpec((tm, D), row), b_spec=pl.BlockSpec((D, D), lambda i, k: (0, 0)),
        out_spec=pl.BlockSpec((tm, D), row), out_shape=(S, D), out_dtypes=[F32], acc_shape=(8, 128),
        trans_b=True)
    (d_w_out,) = matmul(
        "mix_out_dw", [cat], [dx1b], grid=(2, S // tk),
        a_spec=pl.BlockSpec((tk, half), lambda m, k: (k, m)), b_spec=pl.BlockSpec((tk, D), lambda m, k: (k, 0)),
        out_spec=pl.BlockSpec((half, D), lambda m, k: (m, 0)), out_shape=(D, D), out_dtypes=[BF16],
        acc_shape=(half, D), trans_a=True)
    dcat = send("layer0", dict(ffn_w_gate_0=dwg0, ffn_w_up_0=dwu0, ffn_w_down_0=dwd0, pool_w=d_pool_w,
                               ab_w_out=d_w_out), dcat)
    dqa, dfa, dia, dga, d_lb, d_hg = hgrn_backward(dcat, proj, oraw, states, lb, hg_norm)
    dqb, dkb, dvb = sb_backward(dcat, projb, sb_weights, nh, 4 * nh)
    dproj = jnp.concatenate([dqa, dfa, dia, dga, dqb, dkb, dvb], axis=1)
    (d_w_in,) = matmul(
        "proj_in_dw", [h0], [dproj], grid=(N_DEV, S // tk),
        a_spec=pl.BlockSpec((tk, D), lambda j, k: (k, 0)), b_spec=pl.BlockSpec((tk, nbi), lambda j, k: (k, j)),
        out_spec=pl.BlockSpec((None, D, nbi), lambda j, k: (j, 0, 0)), out_shape=(N_DEV, D, nbi),
        out_dtypes=[BF16], acc_shape=(D, nbi), trans_a=True)
    dproj = send("w_in", dict(ab_w_in=d_w_in), dproj)
    dx0, _, d_mix0 = matmul_rms_bwd(
        "proj_in_dx", [dproj], [w_in], grid=(S // tm, N_DEV),
        a_spec=pl.BlockSpec((tm, nbi), lambda i, j: (i, j)),
        b_spec=pl.BlockSpec((None, D, nbi), lambda i, j: (j, 0, 0)),
        tm=tm, x=x, r=r0, gain=mix_norm[0:1], dres=dx1)

    d_l0 = d_lb * lb * (1.0 - lb)
    small = dict(
        loss=loss_blk[0:1, 0:1],
        mix_norm=jnp.concatenate([d_mix0, d_mix1], axis=0),
        ffn_norm=jnp.concatenate([d_ffn0, d_ffn1], axis=0),
        final_norm=d_final,
        lb_logits=jnp.concatenate([d_l0, -d_l0], axis=0),
        hg_out_norm=jnp.sum(d_hg, axis=0),
        pool_scale=d_pscale,
    )
    return dx0, small


def _my_index():
    return 4 * lax.axis_index("x") + 2 * lax.axis_index("y") + lax.axis_index("c")


def _peer(r):
    x, y, c = lax.axis_index("x"), lax.axis_index("y"), lax.axis_index("c")
    px = 1 - x if (r >> 2) & 1 else x
    py = 1 - y if (r >> 1) & 1 else y
    pc = 1 - c if r & 1 else c
    return (px, py, pc), 4 * px + 2 * py + pc


def gather_two_level(name, shard):
    def body(x_ref, out_ref, send_sems, recv_sems, local_sem):
        x, y, c = lax.axis_index("x"), lax.axis_index("y"), lax.axis_index("c")
        me, sibling = (x, y, c), (x, y, 1 - c)
        chips = [(1 - x, y), (x, 1 - y), (1 - x, 1 - y)]

        def slot(px, py, pc):
            return out_ref.at[4 * px + 2 * py + pc]

        def copy(k, block, to, src=None):
            return pltpu.make_async_remote_copy(
                src_ref=slot(*block) if src is None else src, dst_ref=slot(*block), send_sem=send_sems.at[k],
                recv_sem=recv_sems.at[k], device_id=to, device_id_type=MESH)

        mine = pltpu.make_async_copy(x_ref, slot(*me), local_sem)
        mine.start()
        first = [copy(0, me, sibling, src=x_ref)]
        first += [copy(1 + j, me, (*chip, c), src=x_ref) for j, chip in enumerate(chips)]
        for cp in first:
            cp.start()
        passed = [copy(4 + j, (*chip, c), sibling) for j, chip in enumerate(chips)]
        for j, chip in enumerate(chips):
            copy(1 + j, (*chip, c), me).wait_recv()
            passed[j].start()
        copy(0, sibling, me).wait_recv()
        for j, chip in enumerate(chips):
            copy(4 + j, (*chip, 1 - c), me).wait_recv()
        for cp in first + passed:
            cp.wait_send()
        mine.wait()

    any_spec = pl.BlockSpec(memory_space=pl.ANY)
    return pl.pallas_call(
        body, name=name, in_specs=[any_spec], out_specs=any_spec,
        out_shape=jax.ShapeDtypeStruct((N_DEV,) + shard.shape, shard.dtype),
        scratch_shapes=[pltpu.SemaphoreType.DMA((N_DEV - 1,)), pltpu.SemaphoreType.DMA((N_DEV - 1,)),
                        pltpu.SemaphoreType.DMA],
    )(shard)


def exchange(name, arrays, gather):
    n = len(arrays)
    n_peers = N_DEV - 1

    def body(*refs):
        ins, outs = refs[:n], refs[n:2 * n]
        send_sems, recv_sems, local_sems = refs[2 * n:]
        me = _my_index()
        local = []
        for a in range(n):
            src = ins[a] if gather else ins[a].at[me]
            cp = pltpu.make_async_copy(src, outs[a].at[me], local_sems.at[a])
            cp.start()
            local.append(cp)
        remote = []
        for a in range(n):
            for r in range(1, N_DEV):
                peer, pidx = _peer(r)
                src = ins[a] if gather else ins[a].at[pidx]
                cp = pltpu.make_async_remote_copy(
                    src_ref=src, dst_ref=outs[a].at[me], send_sem=send_sems.at[a * n_peers + r - 1],
                    recv_sem=recv_sems.at[a * n_peers + r - 1], device_id=peer, device_id_type=MESH)
                cp.start()
                remote.append((cp, a, r))
        for cp, a, r in remote:
            _, pidx = _peer(r)
            src = ins[a] if gather else ins[a].at[pidx]
            pltpu.make_async_remote_copy(
                src_ref=src, dst_ref=outs[a].at[pidx], send_sem=send_sems.at[a * n_peers + r - 1],
                recv_sem=recv_sems.at[a * n_peers + r - 1], device_id=_peer(r)[0], device_id_type=MESH).wait_recv()
        for cp, a, r in remote:
            cp.wait_send()
        for cp in local:
            cp.wait()

    out_shape = [jax.ShapeDtypeStruct(((N_DEV,) + a.shape) if gather else a.shape, a.dtype) for a in arrays]
    any_spec = pl.BlockSpec(memory_space=pl.ANY)
    return pl.pallas_call(
        body, name=name, in_specs=[any_spec] * n, out_specs=[any_spec] * n, out_shape=out_shape,
        scratch_shapes=[pltpu.SemaphoreType.DMA((n * n_peers,)), pltpu.SemaphoreType.DMA((n * n_peers,)),
                        pltpu.SemaphoreType.DMA((n,))],
    )(*arrays)


_HBM = pl.BlockSpec(memory_space=pltpu.HBM)
_SEM = pl.BlockSpec(memory_space=pltpu.SEMAPHORE)
_EFFECT = pltpu.SideEffectType.DATAFLOW_SIDE_EFFECTING


def _landing(arrays, gather):
    me = _my_index()
    lands = []
    for a in arrays:
        own = a[None] if gather else lax.dynamic_slice_in_dim(a, me, 1, axis=0)
        shape = ((N_DEV,) + a.shape) if gather else a.shape
        lands.append(lax.dynamic_update_slice_in_dim(lax.empty(shape, a.dtype), own, me, axis=0))
    return lands


def exchange_start(name, arrays, gather, carry):
    n = len(arrays)
    n_peers = N_DEV - 1
    lands = _landing(arrays, gather)
    n_thru = 2 * n + 1

    def body(*refs):
        src, land = refs[:n], refs[n:2 * n]
        send_sems, recv_sems = refs[n_thru], refs[n_thru + 1]
        token = refs[-1]
        me = _my_index()
        for a in range(n):
            for r in range(1, N_DEV):
                peer, pidx = _peer(r)
                pltpu.make_async_remote_copy(
                    src_ref=src[a] if gather else src[a].at[pidx], dst_ref=land[a].at[me],
                    send_sem=send_sems.at[a * n_peers + r - 1], recv_sem=recv_sems.at[a * n_peers + r - 1],
                    device_id=peer, device_id_type=MESH).start()
        token[...] = jnp.zeros_like(token)

    operands = list(arrays) + lands + [carry]
    outs = pl.pallas_call(
        body, name=name,
        out_shape=(pltpu.SemaphoreType.DMA((n * n_peers,)), pltpu.SemaphoreType.DMA((n * n_peers,)),
                   *[pltpu.HBM(a.shape, a.dtype) for a in operands], jax.ShapeDtypeStruct((8, 128), F32)),
        in_specs=[_HBM] * n_thru,
        out_specs=(_SEM, _SEM, *([_HBM] * n_thru), pl.BlockSpec(memory_space=pltpu.VMEM)),
        input_output_aliases={i: 2 + i for i in range(n_thru)},
        compiler_params=pltpu.CompilerParams(has_side_effects=_EFFECT),
    )(*[pltpu.with_memory_space_constraint(a, pltpu.HBM) for a in operands])
    handle = (outs[0], outs[1], list(outs[2:2 + n]), list(outs[2 + n:2 + 2 * n]), gather)
    return handle, outs[2 + 2 * n]


def exchange_wait(name, handle, after):
    send_sems, recv_sems, srcs, lands, gather = handle
    n = len(srcs)
    n_peers = N_DEV - 1

    def body(*refs):
        src, land = refs[:n], refs[n:2 * n]
        send_s, recv_s = refs[2 * n], refs[2 * n + 1]
        for a in range(n):
            for r in range(1, N_DEV):
                peer, pidx = _peer(r)
                cp = pltpu.make_async_remote_copy(
                    src_ref=src[a] if gather else src[a].at[pidx], dst_ref=land[a].at[pidx],
                    send_sem=send_s.at[a * n_peers + r - 1], recv_sem=recv_s.at[a * n_peers + r - 1],
                    device_id=peer, device_id_type=MESH)
                cp.wait_send()
                cp.wait_recv()

    shapes = [pltpu.HBM(a.shape, a.dtype) for a in srcs] + [pltpu.HBM(l.shape, l.dtype) for l in lands]
    outs = pl.pallas_call(
        body, name=name, out_shape=tuple(shapes),
        in_specs=[_HBM] * (2 * n) + [_SEM, _SEM, pl.BlockSpec(memory_space=pl.ANY)],
        out_specs=tuple([_HBM] * (2 * n)),
        input_output_aliases={i: i for i in range(2 * n)},
        compiler_params=pltpu.CompilerParams(has_side_effects=_EFFECT),
    )(*srcs, *lands, send_sems, recv_sems, after)
    return list(outs[n:])


def _row_tile(rows, cap=256):
    best = None
    for t in range(16, min(rows, cap) + 1, 16):
        if rows % t == 0:
            best = t
    return best if best is not None else rows


def sum_slots(name, recv):
    n, R, C = recv.shape
    tr = _row_tile(R)

    def body(r_ref, o_ref):
        g = r_ref[0].astype(F32)
        for d in range(1, n):
            g = g + r_ref[d].astype(F32)
        o_ref[...] = g

    return pl.pallas_call(
        body, grid=(R // tr,), name=name,
        in_specs=[pl.BlockSpec((n, tr, C), lambda i: (0, i, 0))],
        out_specs=pl.BlockSpec((tr, C), lambda i: (i, 0)),
        out_shape=jax.ShapeDtypeStruct((R, C), F32),
        compiler_params=_params(("arbitrary",)),
    )(recv)


def adamw(name, recv, w, m, v, layer=None, prev=None):
    n, R, C = recv.shape
    tr = _row_tile(R)

    def body(r_ref, w_ref, m_ref, v_ref, *rest):
        g_ref, d_ref, nm_ref, nv_ref = rest[-4:]
        g = r_ref[0].astype(F32)
        for d in range(1, n):
            g = g + r_ref[d].astype(F32)
        mm = ADAM_B1 * m_ref[...] + (1.0 - ADAM_B1) * g
        vv = ADAM_B2 * v_ref[...] + (1.0 - ADAM_B2) * (g * g)
        m_hat = mm / (1.0 - ADAM_B1 ** ADAM_STEP)
        v_hat = vv / (1.0 - ADAM_B2 ** ADAM_STEP)
        g_ref[...] = g
        d_ref[...] = -ADAM_LR * (m_hat / (jnp.sqrt(v_hat) + ADAM_EPS) + ADAM_WD * w_ref[...])
        nm_ref[...] = mm
        nv_ref[...] = vv

    if layer is None:
        row = pl.BlockSpec((tr, C), lambda i: (i, 0))
        shape = (R, C)
    else:
        row = pl.BlockSpec((None, tr, C), lambda i: (layer, i, 0))
        shape = w.shape
    prev = [] if prev is None else list(prev)
    return pl.pallas_call(
        body, grid=(R // tr,), name=name,
        in_specs=[pl.BlockSpec((n, tr, C), lambda i: (0, i, 0)), row, row, row]
                 + [pl.BlockSpec(memory_space=pl.ANY)] * len(prev),
        out_specs=[row] * 4,
        out_shape=[jax.ShapeDtypeStruct(shape, F32)] * 4,
        input_output_aliases={4 + o: o for o in range(len(prev))},
        compiler_params=_params(("arbitrary",)),
    )(recv, w, m, v, *prev)


def _adamw_nd(name, recv, w, m, v):
    shp = w.shape
    C = shp[-1]
    flat = lambda a: a.reshape(-1, C)
    outs = adamw(name, recv.reshape(recv.shape[0], -1, C), flat(w), flat(m), flat(v))
    return [o.reshape(shp) for o in outs]


_SMALL_NAMES = ("loss", "mix_norm", "ffn_norm", "final_norm", "lb_logits", "hg_out_norm", "pool_scale")
_LANES = 128


def _pack_small(parts):
    rows, layout = [], {}
    at = 0
    for name in parts:
        flat = parts[name].reshape(-1).astype(F32)
        n_rows = -(-flat.shape[0] // (8 * _LANES)) * 8
        flat = jnp.pad(flat, (0, n_rows * _LANES - flat.shape[0]))
        rows.append(flat.reshape(n_rows, _LANES))
        layout[name] = (at, parts[name].shape)
        at += n_rows
    return jnp.concatenate(rows, axis=0), layout


def _unpack_small(pack, layout):
    out = {}
    for name, (at, shape) in layout.items():
        size = int(np.prod(shape))
        n_rows = -(-size // _LANES)
        out[name] = pack[at:at + n_rows].reshape(-1)[:size].reshape(shape)
    return out


def kernel(x, mix_norm, ffn_norm, final_norm, ab_w_in, lb_logits, hg_out_norm, ab_w_out, pool_w, pool_scale, ffn_w_gate, ffn_w_up, ffn_w_down, loss_target, m_mix_norm, m_ffn_norm, m_final_norm, m_ab_w_in, m_lb_logits, m_hg_out_norm, m_ab_w_out, m_pool_w, m_pool_scale, m_ffn_w_gate, m_ffn_w_up, m_ffn_w_down, v_mix_norm, v_ffn_norm, v_final_norm, v_ab_w_in, v_lb_logits, v_hg_out_norm, v_ab_w_out, v_pool_w, v_pool_scale, v_ffn_w_gate, v_ffn_w_up, v_ffn_w_down):
    D = x.shape[-1]
    n_layers = ffn_w_gate.shape[0]
    G = pool_w.shape[1]
    P = pool_w.shape[3]
    me = _my_index()

    rest = [ab_w_out[0], pool_w[0]]
    for l in range(n_layers):
        rest += [ffn_w_gate[l], ffn_w_up[l], ffn_w_down[l]]
    rest = [s.astype(BF16) for s in rest] + [pool_scale]
    rest_handle = []

    def get_w_in(after):
        w_in = gather_two_level("gather_w_in", ab_w_in[0].astype(BF16))
        handle, w_in = exchange_start("gather_rest_start", rest, True, w_in)
        rest_handle.append(handle)
        return w_in

    def get_w_rest(after):
        got = exchange_wait("gather_rest_wait", rest_handle[0], after)
        w_out_g = got[0].reshape(D, D)
        pool_g = got[1].transpose(1, 0, 2, 3).reshape(G, P, P)
        wg = [got[2 + 3 * l] for l in range(n_layers)]
        wu = [got[3 + 3 * l] for l in range(n_layers)]
        wd = [got[4 + 3 * l] for l in range(n_layers)]
        return w_out_g, pool_g, got[-1].reshape(1, D), wg, wu, wd

    in_flight = []

    def send(tag, grads, carry):
        if "pool_w" in grads:
            grads = dict(grads, pool_w=grads["pool_w"].reshape(G, N_DEV, P // N_DEV, P).transpose(1, 0, 2, 3))
        if "ab_w_out" in grads:
            grads = dict(grads, ab_w_out=grads["ab_w_out"].reshape(N_DEV, D // N_DEV, D))
        handle, carry = exchange_start("grads_" + tag + "_start", list(grads.values()), False, carry)
        in_flight.append((tag, list(grads.keys()), handle))
        return carry

    dx0, small = local_step(x[0], loss_target[0], mix_norm, ffn_norm, final_norm[None],
                            lb_logits, hg_out_norm, get_w_in, get_w_rest, send)

    recv = {}
    for tag, names, handle in in_flight:
        recv.update(zip(names, exchange_wait("grads_" + tag + "_wait", handle, dx0)))
    small_pack, layout = _pack_small({k: small[k] for k in _SMALL_NAMES})
    (small_all,) = exchange("gather_small", [small_pack], gather=True)
    tot = _unpack_small(sum_slots("sum_small", small_all), layout)

    res = {}
    res["ab_w_in"] = _adamw_nd("adamw_w_in", recv["ab_w_in"], ab_w_in, m_ab_w_in, v_ab_w_in)
    res["ab_w_out"] = _adamw_nd("adamw_w_out", recv["ab_w_out"], ab_w_out, m_ab_w_out, v_ab_w_out)
    res["pool_w"] = _adamw_nd("adamw_pool_w", recv["pool_w"], pool_w, m_pool_w, v_pool_w)
    ffn_in = {"ffn_w_gate": (ffn_w_gate, m_ffn_w_gate, v_ffn_w_gate),
              "ffn_w_up": (ffn_w_up, m_ffn_w_up, v_ffn_w_up),
              "ffn_w_down": (ffn_w_down, m_ffn_w_down, v_ffn_w_down)}
    for name, (w, m, v) in ffn_in.items():
        flip = name != "ffn_w_down"
        if flip:
            w, m, v = (jnp.swapaxes(a, 1, 2) for a in (w, m, v))
        outs = None
        for l in range(n_layers):
            outs = adamw("adamw_" + name, recv[name + "_" + str(l)], w, m, v, layer=l, prev=outs)
        res[name] = [jnp.swapaxes(o, 1, 2) for o in outs] if flip else outs

    n_ps = pool_scale.shape[1]
    small_g = dict(tot)
    small_g["pool_scale"] = lax.dynamic_slice(tot["pool_scale"], (0, me * n_ps), (1, n_ps))
    small_w = dict(mix_norm=(mix_norm, m_mix_norm, v_mix_norm), ffn_norm=(ffn_norm, m_ffn_norm, v_ffn_norm),
                   final_norm=(final_norm, m_final_norm, v_final_norm),
                   lb_logits=(lb_logits, m_lb_logits, v_lb_logits),
                   hg_out_norm=(hg_out_norm, m_hg_out_norm, v_hg_out_norm),
                   pool_scale=(pool_scale, m_pool_scale, v_pool_scale))
    g_pack, lay2 = _pack_small({k: small_g[k].reshape(small_w[k][0].shape) for k in small_w})
    w_pack, _ = _pack_small({k: small_w[k][0] for k in small_w})
    m_pack, _ = _pack_small({k: small_w[k][1] for k in small_w})
    v_pack, _ = _pack_small({k: small_w[k][2] for k in small_w})
    small_out = [_unpack_small(o, lay2) for o in adamw("adamw_small", g_pack[None], w_pack, m_pack, v_pack)]
    for k in small_w:
        res[k] = [small_out[o][k] for o in range(4)]

    order = ("mix_norm", "ffn_norm", "final_norm", "ab_w_in", "lb_logits", "hg_out_norm", "ab_w_out", "pool_w",
             "pool_scale", "ffn_w_gate", "ffn_w_up", "ffn_w_down")
    outs = [tot["loss"].reshape(()), dx0[None]]
    for o in range(4):
        outs += [res[k][o] for k in order]
    return tuple(outs)
```

```python
import math

import numpy as np
import jax
import jax.numpy as jnp
from jax import lax
from jax.experimental import pallas as pl
from jax.experimental.pallas import tpu as pltpu

F32 = jnp.float32
BF16 = jnp.bfloat16

N_DEV = 8
RMS_EPS = 1e-6
HEAD = 128
HG_CHUNK = 64
HG_HEADS_PER_BLOCK = 8
POOL_WINDOWS = (2, 4, 8, 16)
POOL_HALO = 16
ADAM_LR, ADAM_B1, ADAM_B2, ADAM_EPS, ADAM_WD, ADAM_STEP = 0.001, 0.9, 0.999, 1e-08, 0.01, 10
VMEM_LIMIT_BYTES = 60 * 1024 * 1024
MESH = pl.DeviceIdType.MESH

ROW_TILE = 512
ROW_TILE_WIDE = 1024
REDUCE_TILE = 2048
POOL_TILE = 512
HG_TILE = 512
SB_TILE = 512


def _params(sem):
    return pltpu.CompilerParams(dimension_semantics=sem, vmem_limit_bytes=VMEM_LIMIT_BYTES)


def _sigmoid(x):
    return 1.0 / (1.0 + jnp.exp(-x))


def rms_fwd(x, gain, out_dtype, ts=ROW_TILE):
    S, D = x.shape

    def body(x_ref, g_ref, h_ref, r_ref):
        xv = x_ref[...]
        r = lax.rsqrt(jnp.mean(xv * xv, axis=-1, keepdims=True) + RMS_EPS)
        h_ref[...] = ((xv * r) * g_ref[...]).astype(h_ref.dtype)
        r_ref[...] = r

    return pl.pallas_call(
        body, grid=(S // ts,), name="rms_fwd",
        in_specs=[pl.BlockSpec((ts, D), lambda i: (i, 0)), pl.BlockSpec((1, D), lambda i: (0, 0))],
        out_specs=[pl.BlockSpec((ts, D), lambda i: (i, 0)), pl.BlockSpec((ts, 1), lambda i: (i, 0))],
        out_shape=[jax.ShapeDtypeStruct((S, D), out_dtype), jax.ShapeDtypeStruct((S, 1), F32)],
        compiler_params=_params(("arbitrary",)),
    )(x, gain)


RMS_BWD_ROWS = 128


def _rms_bwd_tile(first, dh_of, x_ref, r_ref, g_ref, dres_ref, dx_ref, dxb_ref, dg_ref, rows):
    gv = g_ref[...]
    part = None
    for c in range(rows // RMS_BWD_ROWS):
        sl = slice(c * RMS_BWD_ROWS, (c + 1) * RMS_BWD_ROWS)
        rr = r_ref[sl, :]
        xh = x_ref[sl, :] * rr
        dhv = dh_of(sl)
        dxh = dhv * gv
        dx = dres_ref[sl, :] + rr * (dxh - xh * jnp.mean(dxh * xh, axis=-1, keepdims=True))
        dx_ref[sl, :] = dx
        dxb_ref[sl, :] = dx.astype(BF16)
        p = jnp.sum(dhv * xh, axis=0, keepdims=True)
        part = p if part is None else part + p

    @pl.when(first)
    def _():
        dg_ref[...] = part

    @pl.when(jnp.logical_not(first))
    def _():
        dg_ref[...] += part


def matmul_rms_bwd(name, a_ops, b_ops, *, grid, a_spec, b_spec, tm, x, r, gain, dres):
    S, D = x.shape
    n_pairs = len(a_ops)
    nk = grid[1]
    dn = (((1,), (1,)), ((), ()))

    def body(*refs):
        a_refs = refs[:n_pairs]
        b_refs = refs[n_pairs:2 * n_pairs]
        x_ref, r_ref, g_ref, dres_ref, dx_ref, dxb_ref, dg_ref, acc_ref = refs[2 * n_pairs:]
        i = pl.program_id(0)
        k = pl.program_id(1)

        @pl.when(k == 0)
        def _():
            acc_ref[...] = jnp.zeros_like(acc_ref)

        part = None
        for ar, br in zip(a_refs, b_refs):
            d = lax.dot_general(ar[...], br[...], dn, preferred_element_type=F32)
            part = d if part is None else part + d
        acc_ref[...] += part

        @pl.when(k == nk - 1)
        def _():
            _rms_bwd_tile(i == 0, lambda sl: acc_ref[sl, :], x_ref, r_ref, g_ref, dres_ref, dx_ref, dxb_ref,
                          dg_ref, tm)

    row = pl.BlockSpec((tm, D), lambda i, k: (i, 0))
    vec = pl.BlockSpec((1, D), lambda i, k: (0, 0))
    return pl.pallas_call(
        body, grid=grid, name=name,
        in_specs=[a_spec] * n_pairs + [b_spec] * n_pairs
                 + [row, pl.BlockSpec((tm, 1), lambda i, k: (i, 0)), vec, row],
        out_specs=[row, row, vec],
        out_shape=[jax.ShapeDtypeStruct((S, D), F32), jax.ShapeDtypeStruct((S, D), BF16),
                   jax.ShapeDtypeStruct((1, D), F32)],
        scratch_shapes=[pltpu.VMEM((tm, D), F32)],
        compiler_params=_params(("arbitrary", "arbitrary")),
    )(*a_ops, *b_ops, x, r, gain, dres)


def matmul_residual_rms(name, a, b, res, gain, tm=ROW_TILE):
    S, K = a.shape
    N = b.shape[1]

    def body(a_ref, b_ref, res_ref, g_ref, xo_ref, h_ref, r_ref):
        xo = res_ref[...] + jnp.dot(a_ref[...], b_ref[...], preferred_element_type=F32)
        xo_ref[...] = xo
        r = lax.rsqrt(jnp.mean(xo * xo, axis=-1, keepdims=True) + RMS_EPS)
        h_ref[...] = ((xo * r) * g_ref[...]).astype(BF16)
        r_ref[...] = r

    row = pl.BlockSpec((tm, N), lambda i: (i, 0))
    return pl.pallas_call(
        body, grid=(S // tm,), name=name,
        in_specs=[pl.BlockSpec((tm, K), lambda i: (i, 0)), pl.BlockSpec((K, N), lambda i: (0, 0)), row,
                  pl.BlockSpec((1, N), lambda i: (0, 0))],
        out_specs=[row, row, pl.BlockSpec((tm, 1), lambda i: (i, 0))],
        out_shape=[jax.ShapeDtypeStruct((S, N), F32), jax.ShapeDtypeStruct((S, N), BF16),
                   jax.ShapeDtypeStruct((S, 1), F32)],
        compiler_params=_params(("arbitrary",)),
    )(a, b, res, gain)


def loss_and_final_bwd(x, gain, target, ts=ROW_TILE):
    S, D = x.shape

    def body(x_ref, g_ref, t_ref, loss_ref, dx_ref, dxb_ref, dg_ref):
        i = pl.program_id(0)
        xv = x_ref[...]
        rr = lax.rsqrt(jnp.mean(xv * xv, axis=-1, keepdims=True) + RMS_EPS)
        xh = xv * rr
        err = xh * g_ref[...] - t_ref[...]
        part_loss = 0.5 * jnp.sum(jnp.mean(err * err, axis=-1, keepdims=True))
        dy = err / D
        dxh = dy * g_ref[...]
        dx = rr * (dxh - xh * jnp.mean(dxh * xh, axis=-1, keepdims=True))
        dx_ref[...] = dx
        dxb_ref[...] = dx.astype(BF16)
        part = jnp.sum(dy * xh, axis=0, keepdims=True)

        @pl.when(i == 0)
        def _():
            dg_ref[...] = part
            loss_ref[...] = jnp.zeros_like(loss_ref) + part_loss

        @pl.when(i > 0)
        def _():
            dg_ref[...] += part
            loss_ref[...] += part_loss

    row = pl.BlockSpec((ts, D), lambda i: (i, 0))
    vec = pl.BlockSpec((1, D), lambda i: (0, 0))
    return pl.pallas_call(
        body, grid=(S // ts,), name="loss_final",
        in_specs=[row, vec, row],
        out_specs=[pl.BlockSpec((8, 128), lambda i: (0, 0)), row, row, vec],
        out_shape=[jax.ShapeDtypeStruct((8, 128), F32), jax.ShapeDtypeStruct((S, D), F32),
                   jax.ShapeDtypeStruct((S, D), BF16), jax.ShapeDtypeStruct((1, D), F32)],
        compiler_params=_params(("arbitrary",)),
    )(x, gain, target)


def matmul(name, a_ops, b_ops, *, grid, a_spec, b_spec, out_spec, out_shape, out_dtypes, acc_shape,
           trans_a=False, trans_b=False, res=None, res_spec=None, bf16_scale=None, bf16_scale_spec=None):
    n_pairs = len(a_ops)
    n_out = len(out_dtypes)
    nk = grid[-1]
    kaxis = len(grid) - 1
    dn = (((0,) if trans_a else (1,), (1,) if trans_b else (0,)), ((), ()))

    def body(*refs):
        a_refs = refs[:n_pairs]
        b_refs = refs[n_pairs:2 * n_pairs]
        pos = 2 * n_pairs
        res_ref = None
        if res is not None:
            res_ref = refs[pos]
            pos += 1
        scale_ref = None
        if bf16_scale is not None:
            scale_ref = refs[pos]
            pos += 1
        out_refs = refs[pos:pos + n_out]
        acc_ref = refs[pos + n_out]
        k = pl.program_id(kaxis)
        in_place = n_out == 1 and out_dtypes[0] == F32
        target = out_refs[0] if in_place else acc_ref

        def finish(val):
            if res_ref is not None:
                val = val + res_ref[...]
            for o in out_refs:
                if scale_ref is not None and o.dtype == BF16:
                    o[...] = (val * scale_ref[...]).astype(BF16)
                else:
                    o[...] = val.astype(o.dtype)

        if nk > 1:
            @pl.when(k == 0)
            def _():
                if in_place and res_ref is not None:
                    target[...] = res_ref[...]
                else:
                    target[...] = jnp.zeros_like(target)

        part = None
        for ar, br in zip(a_refs, b_refs):
            d = lax.dot_general(ar[...].astype(BF16), br[...].astype(BF16), dn, preferred_element_type=F32)
            part = d if part is None else part + d

        if nk == 1:
            finish(part)
        else:
            target[...] += part
            if not in_place:
                @pl.when(k == nk - 1)
                def _():
                    finish(acc_ref[...])

    in_specs = [a_spec] * n_pairs + [b_spec] * n_pairs
    operands = list(a_ops) + list(b_ops)
    if res is not None:
        in_specs.append(res_spec)
        operands.append(res)
    if bf16_scale is not None:
        in_specs.append(bf16_scale_spec)
        operands.append(bf16_scale)
    return pl.pallas_call(
        body, grid=grid, name=name, in_specs=in_specs,
        out_specs=[out_spec] * n_out,
        out_shape=[jax.ShapeDtypeStruct(out_shape, dt) for dt in out_dtypes],
        scratch_shapes=[pltpu.VMEM(acc_shape, F32)],
        compiler_params=_params(("arbitrary",) * len(grid)),
    )(*operands)


def ffn_gate_up(h, wg, wu, tm=ROW_TILE_WIDE):
    S, D = h.shape
    nb = wg.shape[2]

    def body(h_ref, wg_ref, wu_ref, p_ref, r_ref, a_ref):
        for c in range(2):
            rows = slice(c * (tm // 2), (c + 1) * (tm // 2))
            hv = h_ref[rows, :]
            g = jnp.dot(hv, wg_ref[...], preferred_element_type=F32)
            u = jnp.dot(hv, wu_ref[...], preferred_element_type=F32)
            s = _sigmoid(g)
            p = g * s
            p_ref[rows, :] = p
            r_ref[rows, :] = u * (s * (1.0 + g * (1.0 - s)))
            a_ref[rows, :] = (p * u).astype(BF16)

    wspec = pl.BlockSpec((None, D, nb), lambda j, i: (j, 0, 0))
    ospec = pl.BlockSpec((None, tm, nb), lambda j, i: (j, i, 0))
    return pl.pallas_call(
        body, grid=(N_DEV, S // tm), name="ffn_gate_up",
        in_specs=[pl.BlockSpec((tm, D), lambda j, i: (i, 0)), wspec, wspec],
        out_specs=[ospec, ospec, ospec],
        out_shape=[jax.ShapeDtypeStruct((N_DEV, S, nb), F32), jax.ShapeDtypeStruct((N_DEV, S, nb), F32),
                   jax.ShapeDtypeStruct((N_DEV, S, nb), BF16)],
        compiler_params=_params(("arbitrary", "arbitrary")),
    )(h, wg, wu)


def ffn_bwd_hidden(dy, wd, p, r, tm=ROW_TILE_WIDE):
    S, D = dy.shape
    nb = wd.shape[1]

    def body(dy_ref, wd_ref, p_ref, r_ref, dg_ref, du_ref):
        for c in range(2):
            rows = slice(c * (tm // 2), (c + 1) * (tm // 2))
            da = lax.dot_general(dy_ref[rows, :], wd_ref[...], (((1,), (1,)), ((), ())),
                                 preferred_element_type=F32)
            du_ref[rows, :] = (da * p_ref[rows, :]).astype(BF16)
            dg_ref[rows, :] = (da * r_ref[rows, :]).astype(BF16)

    hspec = pl.BlockSpec((None, tm, nb), lambda j, i: (j, i, 0))
    return pl.pallas_call(
        body, grid=(N_DEV, S // tm), name="ffn_bwd_hidden",
        in_specs=[pl.BlockSpec((tm, D), lambda j, i: (i, 0)), pl.BlockSpec((None, nb, D), lambda j, i: (j, 0, 0)),
                  hspec, hspec],
        out_specs=[hspec, hspec],
        out_shape=[jax.ShapeDtypeStruct((N_DEV, S, nb), BF16), jax.ShapeDtypeStruct((N_DEV, S, nb), BF16)],
        compiler_params=_params(("arbitrary", "arbitrary")),
    )(dy, wd, p, r)


def ffn_forward(h, xres, wg, wu, wd, tm=ROW_TILE_WIDE):
    S, D = h.shape
    nb = wg.shape[2]
    g, u, a = ffn_gate_up(h, wg, wu)
    (xo,) = matmul(
        "ffn_down", [a], [wd], grid=(S // tm, N_DEV),
        a_spec=pl.BlockSpec((None, tm, nb), lambda i, j: (j, i, 0)),
        b_spec=pl.BlockSpec((None, nb, D), lambda i, j: (j, 0, 0)),
        out_spec=pl.BlockSpec((tm, D), lambda i, j: (i, 0)), out_shape=(S, D), out_dtypes=[F32],
        acc_shape=(tm, D), res=xres, res_spec=pl.BlockSpec((tm, D), lambda i, j: (i, 0)))
    return xo, (g, u, a)


def ffn_backward(dy_b, h, saved, wg, wu, wd, x, r, gain, dres, tm=ROW_TILE, tk=REDUCE_TILE):
    S, D = h.shape
    nb = wg.shape[2]
    g, u, a = saved
    dg, du = ffn_bwd_hidden(dy_b, wd, g, u)
    dx = matmul_rms_bwd(
        "ffn_dh", [dg, du], [wg, wu], grid=(S // tm, N_DEV),
        a_spec=pl.BlockSpec((None, tm, nb), lambda i, j: (j, i, 0)),
        b_spec=pl.BlockSpec((None, D, nb), lambda i, j: (j, 0, 0)),
        tm=tm, x=x, r=r, gain=gain, dres=dres)

    def wgrad_in(name, dhid):
        (dw,) = matmul(
            name, [dhid], [h], grid=(N_DEV, S // tk),
            a_spec=pl.BlockSpec((None, tk, nb), lambda j, k: (j, k, 0)),
            b_spec=pl.BlockSpec((tk, D), lambda j, k: (k, 0)),
            out_spec=pl.BlockSpec((None, nb, D), lambda j, k: (j, 0, 0)), out_shape=(N_DEV, nb, D),
            out_dtypes=[BF16], acc_shape=(nb, D), trans_a=True)
        return dw

    dwg = wgrad_in("ffn_dwg", dg)
    dwu = wgrad_in("ffn_dwu", du)
    (dwd,) = matmul(
        "ffn_dwd", [a], [dy_b], grid=(N_DEV, S // tk),
        a_spec=pl.BlockSpec((None, tk, nb), lambda j, k: (j, k, 0)),
        b_spec=pl.BlockSpec((tk, D), lambda j, k: (k, 0)),
        out_spec=pl.BlockSpec((None, nb, D), lambda j, k: (j, 0, 0)), out_shape=(N_DEV, nb, D),
        out_dtypes=[BF16], acc_shape=(nb, D), trans_a=True)
    return dx, dwg, dwu, dwd


def _pool_counts(row0, n, w):
    pos = row0 + lax.broadcasted_iota(jnp.int32, (n, 1), 0)
    return jnp.minimum(pos + 1, w).astype(F32)


def pool_forward(h, xres, w, scale, gain_next, ts=POOL_TILE):
    S, D = h.shape
    G = len(POOL_WINDOWS)
    P = D // G
    hb = ts // POOL_HALO

    def body(h_ref, halo_ref, x_ref, w_ref, s_ref, gn_ref, xo_ref, p_ref, hn_ref, rn_ref):
        i = pl.program_id(0)
        for gi, win in enumerate(POOL_WINDOWS):
            cols = slice(gi * P, (gi + 1) * P)
            cur = h_ref[:, cols]
            halo = jnp.where(i > 0, halo_ref[:, cols], 0.0)
            acc = jnp.concatenate([halo, cur], axis=0)
            step = 1
            while step < win:
                acc = acc + pltpu.roll(acc, step, 0)
                step *= 2
            wsum = acc[POOL_HALO:, :]
            pooled = wsum / _pool_counts(i * ts, ts, win) - cur
            pb = pooled.astype(BF16)
            p_ref[:, cols] = pb
            mixed = jnp.dot(pb, w_ref[gi], preferred_element_type=F32)
            xo_ref[:, cols] = x_ref[:, cols] + mixed * s_ref[:, cols]
        xo = xo_ref[...]
        r = lax.rsqrt(jnp.mean(xo * xo, axis=-1, keepdims=True) + RMS_EPS)
        hn_ref[...] = ((xo * r) * gn_ref[...]).astype(BF16)
        rn_ref[...] = r

    row = pl.BlockSpec((ts, D), lambda i: (i, 0))
    vec = pl.BlockSpec((1, D), lambda i: (0, 0))
    return pl.pallas_call(
        body, grid=(S // ts,), name="pool_fwd",
        in_specs=[row, pl.BlockSpec((POOL_HALO, D), lambda i: (jnp.maximum(i * hb - 1, 0), 0)), row,
                  pl.BlockSpec((G, P, P), lambda i: (0, 0, 0)), vec, vec],
        out_specs=[row, row, row, pl.BlockSpec((ts, 1), lambda i: (i, 0))],
        out_shape=[jax.ShapeDtypeStruct((S, D), F32), jax.ShapeDtypeStruct((S, D), BF16),
                   jax.ShapeDtypeStruct((S, D), BF16), jax.ShapeDtypeStruct((S, 1), F32)],
        compiler_params=_params(("arbitrary",)),
    )(h, h, xres, w, scale, gain_next)


def pool_backward_mix(dx, pooled, w, scale, ts=POOL_TILE):
    S, D = dx.shape
    G = len(POOL_WINDOWS)
    P = D // G

    def body(dx_ref, p_ref, w_ref, s_ref, dm_ref, dp_ref, ds_ref):
        i = pl.program_id(0)
        parts = []
        for gi in range(G):
            cols = slice(gi * P, (gi + 1) * P)
            dxv = dx_ref[:, cols]
            dmb = (dxv * s_ref[:, cols]).astype(BF16)
            dm_ref[:, cols] = dmb
            dp_ref[:, cols] = lax.dot_general(dmb, w_ref[gi], (((1,), (1,)), ((), ())),
                                              preferred_element_type=F32)
            mixed = jnp.dot(p_ref[:, cols], w_ref[gi], preferred_element_type=F32)
            parts.append(jnp.sum(dxv * mixed, axis=0, keepdims=True))
        part = jnp.concatenate(parts, axis=1)

        @pl.when(i == 0)
        def _():
            ds_ref[...] = part

        @pl.when(i > 0)
        def _():
            ds_ref[...] += part

    row = pl.BlockSpec((ts, D), lambda i: (i, 0))
    vec = pl.BlockSpec((1, D), lambda i: (0, 0))
    return pl.pallas_call(
        body, grid=(S // ts,), name="pool_bwd_mix",
        in_specs=[row, row, pl.BlockSpec((G, P, P), lambda i: (0, 0, 0)), vec],
        out_specs=[row, row, vec],
        out_shape=[jax.ShapeDtypeStruct((S, D), BF16), jax.ShapeDtypeStruct((S, D), F32),
                   jax.ShapeDtypeStruct((1, D), F32)],
        compiler_params=_params(("arbitrary",)),
    )(dx, pooled, w, scale)


def pool_backward_window(dp, x, r, gain, dres, ts=POOL_TILE):
    S, D = dp.shape
    G = len(POOL_WINDOWS)
    P = D // G
    hb = ts // POOL_HALO
    n_i = S // ts
    n_rows = ts + POOL_HALO

    def body(dp_ref, halo_ref, x_ref, r_ref, g_ref, dres_ref, dx_ref, dxb_ref, dg_ref, dh_ref):
        i = pl.program_id(0)
        for gi, win in enumerate(POOL_WINDOWS):
            cols = slice(gi * P, (gi + 1) * P)
            cur = dp_ref[:, cols]
            halo = jnp.where(i < n_i - 1, halo_ref[:, cols], 0.0)
            acc = jnp.concatenate([cur / _pool_counts(i * ts, ts, win),
                                   halo / _pool_counts((i + 1) * ts, POOL_HALO, win)], axis=0)
            step = 1
            while step < win:
                acc = acc + pltpu.roll(acc, n_rows - step, 0)
                step *= 2
            dh_ref[:, cols] = acc[:ts, :] - cur
        _rms_bwd_tile(i == 0, lambda sl: dh_ref[sl, :], x_ref, r_ref, g_ref, dres_ref, dx_ref, dxb_ref, dg_ref, ts)

    row = pl.BlockSpec((ts, D), lambda i: (i, 0))
    vec = pl.BlockSpec((1, D), lambda i: (0, 0))
    return pl.pallas_call(
        body, grid=(n_i,), name="pool_bwd_window",
        in_specs=[row, pl.BlockSpec((POOL_HALO, D), lambda i: (jnp.minimum((i + 1) * hb, S // POOL_HALO - 1), 0)),
                  row, pl.BlockSpec((ts, 1), lambda i: (i, 0)), vec, row],
        out_specs=[row, row, vec],
        out_shape=[jax.ShapeDtypeStruct((S, D), F32), jax.ShapeDtypeStruct((S, D), BF16),
                   jax.ShapeDtypeStruct((1, D), F32)],
        scratch_shapes=[pltpu.VMEM((ts, D), F32)],
        compiler_params=_params(("arbitrary",)),
    )(dp, dp, x, r, gain, dres)


_HG_LEVELS = (32, 16, 8, 4, 2, 1)
_N_LEV = len(_HG_LEVELS) + 1


def _hgrn_constants():
    C = HG_CHUNK
    t = np.arange(C)
    tri = (t[None, :] <= t[:, None]).astype(np.float32)
    blocks = [tri]
    masks, upq, upk = [], [], []
    for m in _HG_LEVELS:
        p = (t // (2 * m)) * 2 * m + m - 1
        blocks.append(tri[p])
        masks.append(((t[:, None] // (2 * m)) == (t[None, :] // (2 * m))).astype(np.float32))
        upper = (t % (2 * m)) >= m
        upq.append(np.repeat(upper[:, None], HEAD, 1).astype(np.float32))
        upk.append(np.repeat(~upper[:, None], HEAD, 1).astype(np.float32))
    blocks.append(tri)
    masks.append(np.eye(C, dtype=np.float32))
    upq.append(np.ones((C, HEAD), np.float32))
    upk.append(np.ones((C, HEAD), np.float32))
    mstack = np.concatenate(blocks, axis=0)
    mstack3 = np.concatenate([mstack] * 3, axis=1)
    trirev3 = np.concatenate([tri.T] * 3, axis=1)
    return (jnp.asarray(mstack3, BF16), jnp.asarray(np.stack(masks)), jnp.asarray(np.stack(upq)),
            jnp.asarray(np.stack(upk)), jnp.asarray(trirev3, BF16))


def _split3(x):
    hi = x.astype(BF16)
    r1 = x - hi.astype(F32)
    mid = r1.astype(BF16)
    lo = (r1 - mid.astype(F32)).astype(BF16)
    return jnp.concatenate([hi, mid, lo], axis=0)


def _hgrn_chunk_common(qa, fa, lbv, mstack3, upq, upk):
    sq = _sigmoid(qa)
    q = qa * sq
    sf = _sigmoid(fa)
    f = lbv + (1.0 - lbv) * sf
    g = jnp.log(f)
    k = 1.0 - f
    gall = jnp.dot(mstack3, _split3(g), preferred_element_type=F32).reshape(_N_LEV + 1, HG_CHUNK, HEAD)
    G = gall[0]
    eq_exp = G[None] - gall[1:]
    eq = jnp.exp(jnp.minimum(eq_exp, 0.0)) * upq
    ek = jnp.exp(jnp.minimum(-eq_exp, 0.0)) * upk
    Qs = (q[None] * eq).astype(BF16)
    Ks = (k[None] * ek).astype(BF16)
    return sq, q, sf, f, k, G, eq, ek, Qs, Ks


def hgrn_forward(proj, lb, hg_norm, ts=HG_TILE):
    S = proj.shape[0]
    nh = lb.shape[1] // HEAD
    C = HG_CHUNK
    ncs = ts // C
    mstack3, masks, upq, upk, _ = _hgrn_constants()

    def body(qa_ref, fa_ref, ia_ref, ga_ref, lb_ref, gn_ref, ms_ref, mk_ref, uq_ref, uk_ref,
             oa_ref, oraw_ref, st_ref, state):
        tt = pl.program_id(1)

        @pl.when(tt == 0)
        def _():
            state[...] = jnp.zeros_like(state)

        gn = gn_ref[...]

        def chunk(c, carry):
            sl = pl.ds(pl.multiple_of(c * C, C), C)
            for hh in range(HG_HEADS_PER_BLOCK):
                cols = slice(hh * HEAD, (hh + 1) * HEAD)
                qa, fa, v, ga = qa_ref[sl, cols], fa_ref[sl, cols], ia_ref[sl, cols], ga_ref[sl, cols]
                _, q, _, _, k, G, _, _, Qs, Ks = _hgrn_chunk_common(qa, fa, lb_ref[:, cols], ms_ref[...],
                                                                    uq_ref[...], uk_ref[...])
                att7 = lax.dot_general(Qs, Ks, (((2,), (2,)), ((0,), (0,))), preferred_element_type=F32)
                att = jnp.sum(att7 * mk_ref[...], axis=0)
                st = state[hh]
                st_ref[hh, c] = st
                vb = v.astype(BF16)
                qg = (q * jnp.exp(G)).astype(BF16)
                o = jnp.dot(att.astype(BF16), vb, preferred_element_type=F32)
                o = o + lax.dot_general(qg, st.astype(BF16), (((1,), (1,)), ((), ())),
                                        preferred_element_type=F32)
                g_last = G[C - 1:C, :]
                kh = (k * jnp.exp(g_last - G)).astype(BF16)
                state[hh] = st * jnp.exp(g_last) + lax.dot_general(vb, kh, (((0,), (0,)), ((), ())),
                                                                   preferred_element_type=F32)
                oraw_ref[sl, cols] = o
                r = lax.rsqrt(jnp.mean(o * o, axis=-1, keepdims=True) + RMS_EPS)
                oa_ref[sl, cols] = (((o * r) * gn) * (ga * _sigmoid(ga))).astype(BF16)
            return carry

        lax.fori_loop(0, ncs, chunk, 0)

    hpb = HG_HEADS_PER_BLOCK
    wide = hpb * HEAD

    def col(m0):
        return pl.BlockSpec((ts, wide), lambda h, t: (t, m0 // hpb + h))

    const3 = lambda shape: pl.BlockSpec(shape, lambda h, t: (0, 0, 0))
    return pl.pallas_call(
        body, grid=(nh // hpb, S // ts), name="hgrn_fwd",
        in_specs=[col(0), col(nh), col(2 * nh), col(3 * nh),
                  pl.BlockSpec((1, wide), lambda h, t: (0, h)), pl.BlockSpec((1, HEAD), lambda h, t: (0, 0)),
                  pl.BlockSpec(mstack3.shape, lambda h, t: (0, 0)), const3(masks.shape), const3(upq.shape),
                  const3(upk.shape)],
        out_specs=[pl.BlockSpec((ts, wide), lambda h, t: (t, h)), pl.BlockSpec((ts, wide), lambda h, t: (t, h)),
                   pl.BlockSpec((hpb, ncs, HEAD, HEAD), lambda h, t: (h, t, 0, 0))],
        out_shape=[jax.ShapeDtypeStruct((S, nh * HEAD), BF16), jax.ShapeDtypeStruct((S, nh * HEAD), F32),
                   jax.ShapeDtypeStruct((nh, S // C, HEAD, HEAD), F32)],
        scratch_shapes=[pltpu.VMEM((hpb, HEAD, HEAD), F32)],
        compiler_params=_params(("arbitrary", "arbitrary")),
    )(proj, proj, proj, proj, lb, hg_norm, mstack3, masks, upq, upk)


def hgrn_backward(dcat, proj, oraw, states, lb, hg_norm, ts=HG_TILE):
    S = proj.shape[0]
    nh = lb.shape[1] // HEAD
    C = HG_CHUNK
    ncs = ts // C
    nt = S // ts
    mstack3, masks, upq, upk, trirev3 = _hgrn_constants()

    def body(do_ref, qa_ref, fa_ref, ia_ref, ga_ref, or_ref, st_ref, lb_ref, gn_ref, ms_ref, mk_ref, uq_ref,
             uk_ref, tr_ref, dqa_ref, dfa_ref, dia_ref, dga_ref, dlb_ref, dgn_ref, dstate):
        tt = pl.program_id(1)

        @pl.when(tt == 0)
        def _():
            dstate[...] = jnp.zeros_like(dstate)
            dlb_ref[...] = jnp.zeros_like(dlb_ref)
            dgn_ref[...] = jnp.zeros_like(dgn_ref)

        gn = gn_ref[...]

        def chunk(cc, carry):
            c = ncs - 1 - cc
            sl = pl.ds(pl.multiple_of(c * C, C), C)
            for hh in range(HG_HEADS_PER_BLOCK):
                cols = slice(hh * HEAD, (hh + 1) * HEAD)
                lbv = lb_ref[:, cols]
                qa, fa, v, ga = qa_ref[sl, cols], fa_ref[sl, cols], ia_ref[sl, cols], ga_ref[sl, cols]
                sq, q, sf, f, k, G, eq, ek, Qs, Ks = _hgrn_chunk_common(qa, fa, lbv, ms_ref[...], uq_ref[...],
                                                                        uk_ref[...])
                mk = mk_ref[...]
                att7 = lax.dot_general(Qs, Ks, (((2,), (2,)), ((0,), (0,))), preferred_element_type=F32)
                att = jnp.sum(att7 * mk, axis=0)
                o = or_ref[sl, cols]
                dO = do_ref[sl, cols]
                sg = _sigmoid(ga)
                r = lax.rsqrt(jnp.mean(o * o, axis=-1, keepdims=True) + RMS_EPS)
                xh = o * r
                dga_ref[sl, cols] = (dO * (xh * gn) * (sg * (1.0 + ga * (1.0 - sg)))).astype(BF16)
                don = dO * (ga * sg)
                dgn_ref[hh] += jnp.sum(don * xh, axis=0, keepdims=True)
                dxh = don * gn
                do = r * (dxh - xh * jnp.mean(dxh * xh, axis=-1, keepdims=True))
                dob = do.astype(BF16)
                st = st_ref[hh, c]
                dst = dstate[hh]
                dstb = dst.astype(BF16)
                vb = v.astype(BF16)
                eG = jnp.exp(G)
                g_last = G[C - 1:C, :]
                e_last = jnp.exp(g_last)
                e_tail = jnp.exp(g_last - G)
                qg = (q * eG).astype(BF16)
                kh = (k * e_tail).astype(BF16)
                dq_inter = jnp.dot(dob, st.astype(BF16), preferred_element_type=F32) * eG
                dk_inter = jnp.dot(vb, dstb, preferred_element_type=F32) * e_tail
                dv = lax.dot_general(kh, dstb, (((1,), (1,)), ((), ())), preferred_element_type=F32)
                dv = dv + lax.dot_general(att.astype(BF16), dob, (((0,), (0,)), ((), ())),
                                          preferred_element_type=F32)
                dA = lax.dot_general(dob, vb, (((1,), (1,)), ((), ())), preferred_element_type=F32)
                dA7 = (dA[None] * mk).astype(BF16)
                dAT7 = (dA.T[None] * mk).astype(BF16)
                dQs = lax.dot_general(dA7, Ks, (((2,), (1,)), ((0,), (0,))), preferred_element_type=F32)
                dKs = lax.dot_general(dAT7, Qs, (((2,), (1,)), ((0,), (0,))), preferred_element_type=F32)
                dq = dq_inter + jnp.sum(dQs * eq, axis=0)
                dk = dk_inter + jnp.sum(dKs * ek, axis=0)
                dG = (jnp.sum(Qs.astype(F32) * dQs - Ks.astype(F32) * dKs, axis=0)
                      + q * dq_inter - k * dk_inter)
                last_extra = (jnp.sum(k * dk_inter, axis=0, keepdims=True)
                              + e_last * jnp.sum(dst * st, axis=0, keepdims=True))
                is_last = lax.broadcasted_iota(jnp.int32, (C, 1), 0) == C - 1
                dG = dG + jnp.where(is_last, last_extra, 0.0)
                dg = jnp.dot(tr_ref[...], _split3(dG), preferred_element_type=F32)
                df = dg / f - dk
                dfa_ref[sl, cols] = (df * (1.0 - lbv) * (sf * (1.0 - sf))).astype(BF16)
                dlb_ref[:, cols] += jnp.sum(df * (1.0 - sf), axis=0, keepdims=True)
                dqa_ref[sl, cols] = (dq * (sq * (1.0 + qa * (1.0 - sq)))).astype(BF16)
                dia_ref[sl, cols] = dv.astype(BF16)
                dstate[hh] = dst * e_last + lax.dot_general(dob, qg, (((0,), (0,)), ((), ())),
                                                            preferred_element_type=F32)
            return carry

        lax.fori_loop(0, ncs, chunk, 0)

    hpb = HG_HEADS_PER_BLOCK
    wide = hpb * HEAD

    def col(m0):
        return pl.BlockSpec((ts, wide), lambda h, t: (nt - 1 - t, m0 // hpb + h))

    const3 = lambda shape: pl.BlockSpec(shape, lambda h, t: (0, 0, 0))
    const2 = lambda shape: pl.BlockSpec(shape, lambda h, t: (0, 0))
    ocol = pl.BlockSpec((ts, wide), lambda h, t: (nt - 1 - t, h))
    half = nh * HEAD
    return pl.pallas_call(
        body, grid=(nh // hpb, nt), name="hgrn_bwd",
        in_specs=[col(0), col(0), col(nh), col(2 * nh), col(3 * nh), col(0),
                  pl.BlockSpec((hpb, ncs, HEAD, HEAD), lambda h, t: (h, nt - 1 - t, 0, 0)),
                  pl.BlockSpec((1, wide), lambda h, t: (0, h)), const2((1, HEAD)),
                  const2(mstack3.shape), const3(masks.shape), const3(upq.shape), const3(upk.shape),
                  const2(trirev3.shape)],
        out_specs=[ocol, ocol, ocol, ocol, pl.BlockSpec((1, wide), lambda h, t: (0, h)),
                   pl.BlockSpec((hpb, 1, HEAD), lambda h, t: (h, 0, 0))],
        out_shape=[jax.ShapeDtypeStruct((S, half), BF16)] * 4
                  + [jax.ShapeDtypeStruct((1, half), F32), jax.ShapeDtypeStruct((nh, 1, HEAD), F32)],
        scratch_shapes=[pltpu.VMEM((hpb, HEAD, HEAD), F32)],
        compiler_params=_params(("arbitrary", "arbitrary")),
    )(dcat, proj, proj, proj, proj, oraw, states, lb, hg_norm, mstack3, masks, upq, upk, trirev3)


SB_SUB = 128
LOG2_E = 1.4426950408889634
SB_SCALE = 1.0 / math.sqrt(HEAD)
SB_QUERY_SCALE = SB_SCALE * LOG2_E


def _split2(x):
    hi = x.astype(BF16)
    lo = (x - hi.astype(F32)).astype(BF16)
    return jnp.concatenate([hi, lo], axis=1)


def _sb_constants():
    j = np.arange(SB_SUB)
    after = (j[:, None] > j[None, :]).astype(np.float32)
    before = (j[:, None] < j[None, :]).astype(np.float32)
    return (jnp.asarray(np.concatenate([after, after], axis=0), BF16),
            jnp.asarray(np.concatenate([before, before], axis=0), BF16))


def _sb_tri(i):
    return (i * (i + 1)) // 2


def _sb_diag_mask(t):
    return lax.broadcasted_iota(jnp.int32, (t, t), 1) < lax.broadcasted_iota(jnp.int32, (t, t), 0)


def _sb_scores(q, k_ref, col0, t):
    ks = k_ref[pl.ds(pl.multiple_of(col0, t), t), :]
    return lax.dot_general(q, ks, (((1,), (1,)), ((), ())), preferred_element_type=F32)


def _sb_weights(z, diagonal, run, after2):
    t = z.shape[0]
    nsub = z.shape[1] // SB_SUB
    lks, locs, tots = [], [], []
    for b in range(nsub):
        r0 = b * SB_SUB if diagonal else 0
        zb = z[r0:, b * SB_SUB:(b + 1) * SB_SUB]
        nzb = -zb
        lkb = jnp.minimum(nzb, 0.0) - jnp.log(1.0 + jnp.exp2(jnp.minimum(zb, nzb))) * LOG2_E
        if diagonal:
            rows = lax.broadcasted_iota(jnp.int32, zb.shape, 0)
            visible = lax.broadcasted_iota(jnp.int32, zb.shape, 1) < rows
            lkb = jnp.where(visible, lkb, 0.0)
        loc = jnp.dot(_split2(lkb), after2, preferred_element_type=F32)
        lks.append(lkb)
        locs.append(loc)
        tots.append(loc[:, 0:1] + lkb[:, 0:1])
    ws = [None] * nsub
    for b in reversed(range(nsub)):
        r0 = b * SB_SUB if diagonal else 0
        zb = z[r0:, b * SB_SUB:(b + 1) * SB_SUB]
        wb = jnp.exp2(zb + lks[b] + (locs[b] + run[r0:]))
        tot = tots[b]
        if diagonal:
            rows = lax.broadcasted_iota(jnp.int32, zb.shape, 0)
            wb = jnp.where(lax.broadcasted_iota(jnp.int32, zb.shape, 1) < rows, wb, 0.0)
            if r0:
                wb = jnp.concatenate([jnp.zeros((r0, SB_SUB), F32), wb], axis=0)
                tot = jnp.concatenate([jnp.zeros((r0, 1), F32), tot], axis=0)
        ws[b] = wb
        run = run + tot
    return jnp.concatenate(ws, axis=1), run


def sb_forward(projb, nh, m0, t=SB_TILE):
    S = projb.shape[0]
    after2, _ = _sb_constants()
    n_i = S // t

    def body(q_ref, k_ref, v_ref, af_ref, o_ref, w_hbm, wbuf, wsem):
        h = pl.program_id(0)
        i = pl.program_id(1)
        q = q_ref[...]
        after = af_ref[...]
        base = _sb_tri(i)

        def store(slot, jb):
            return pltpu.make_async_copy(wbuf.at[slot], w_hbm.at[h, base + jb], wsem.at[slot])

        def block(n, jb, run, diagonal):
            slot = n % 2

            @pl.when(n >= 2)
            def _():
                store(slot, jb).wait()

            z = _sb_scores(q, k_ref, jb * t, t)
            w, run = _sb_weights(z, diagonal, run, after)
            wb = w.astype(BF16)
            wbuf[slot] = wb
            store(slot, jb).start()
            vs = v_ref[pl.ds(pl.multiple_of(jb * t, t), t), :]
            return run, jnp.dot(wb, vs, preferred_element_type=F32)

        run, acc = block(0, i, jnp.zeros((t, 1), F32), True)

        def step(n, carry):
            run, acc = carry
            run, part = block(n + 1, i - 1 - n, run, False)
            return run, acc + part

        _, acc = lax.fori_loop(0, i, step, (run, acc))
        o_ref[...] = acc.astype(BF16)
        store(i % 2, 0).wait()

        @pl.when(i >= 1)
        def _():
            store((i + 1) % 2, 0).wait()

    return pl.pallas_call(
        body, grid=(nh, n_i), name="sb_fwd",
        in_specs=[pl.BlockSpec((t, HEAD), lambda h, i: (i, m0 + h)),
                  pl.BlockSpec((S, HEAD), lambda h, i: (0, m0 + nh + h)),
                  pl.BlockSpec((S, HEAD), lambda h, i: (0, m0 + 2 * nh + h)),
                  pl.BlockSpec(after2.shape, lambda h, i: (0, 0))],
        out_specs=[pl.BlockSpec((t, HEAD), lambda h, i: (i, h)), pl.BlockSpec(memory_space=pl.ANY)],
        out_shape=[jax.ShapeDtypeStruct((S, nh * HEAD), BF16),
                   jax.ShapeDtypeStruct((nh, _sb_tri(n_i), t, t), BF16)],
        scratch_shapes=[pltpu.VMEM((2, t, t), BF16), pltpu.SemaphoreType.DMA((2,))],
        compiler_params=_params(("arbitrary", "arbitrary")),
    )(projb, projb, projb, after2)


def sb_backward(dcat, projb, w_all, nh, m0, t=SB_TILE):
    S = projb.shape[0]
    _, before2 = _sb_constants()
    n_i = S // t
    nsub = t // SB_SUB

    def body(do_ref, q_ref, k_ref, v_ref, bf_ref, w_hbm, dq_ref, dk_ref, dv_ref, dk_acc, dv_acc, wbuf, wsem):
        h = pl.program_id(0)
        i = pl.program_id(1)

        @pl.when(i == 0)
        def _():
            dk_acc[...] = jnp.zeros_like(dk_acc)
            dv_acc[...] = jnp.zeros_like(dv_acc)

        q = q_ref[...]
        dob = do_ref[...].astype(BF16)
        before = bf_ref[...]
        base = _sb_tri(i)

        def load(slot, jb):
            return pltpu.make_async_copy(w_hbm.at[h, base + jb], wbuf.at[slot], wsem.at[slot])

        load(0, 0).start()

        def left_to_right(jb, run, dq, mask):
            slot = jb % 2
            load(slot, jb).wait()

            @pl.when(jb < i)
            def _():
                load(1 - slot, jb + 1).start()

            ksl = pl.ds(pl.multiple_of(jb * t, t), t)
            wb = wbuf[slot]
            z = _sb_scores(q, k_ref, jb * t, t)
            dw = lax.dot_general(dob, v_ref[ksl, :], (((1,), (1,)), ((), ())), preferred_element_type=F32)
            d = dw * wb.astype(F32)
            dv_acc[ksl, :] += lax.dot_general(wb, dob, (((0,), (0,)), ((), ())), preferred_element_type=F32)
            sig = 1.0 / (1.0 + jnp.exp2(-z))
            das = []
            for b in range(nsub):
                db = d[:, b * SB_SUB:(b + 1) * SB_SUB]
                prefix = run + jnp.dot(_split2(db), before, preferred_element_type=F32)
                das.append(db - sig[:, b * SB_SUB:(b + 1) * SB_SUB] * (db + prefix))
                run = prefix[:, SB_SUB - 1:SB_SUB] + db[:, SB_SUB - 1:SB_SUB]
            da = jnp.concatenate(das, axis=1)
            if mask is not None:
                da = jnp.where(mask, da, 0.0)
            dab = (da * SB_SCALE).astype(BF16)
            dq = dq + jnp.dot(dab, k_ref[ksl, :], preferred_element_type=F32)
            dk_acc[ksl, :] += lax.dot_general(dab, q, (((0,), (0,)), ((), ())), preferred_element_type=F32)
            return run, dq

        run, dq = lax.fori_loop(0, i, lambda jb, c: left_to_right(jb, c[0], c[1], None),
                                (jnp.zeros((t, 1), F32), jnp.zeros((t, HEAD), F32)))
        _, dq = left_to_right(i, run, dq, _sb_diag_mask(t))
        dq_ref[...] = dq.astype(BF16)

        @pl.when(i == n_i - 1)
        def _():
            dk_ref[...] = (dk_acc[...] * (1.0 / SB_QUERY_SCALE)).astype(BF16)
            dv_ref[...] = dv_acc[...].astype(BF16)

    half = nh * HEAD
    full = pl.BlockSpec((S, HEAD), lambda h, i: (0, h))
    return pl.pallas_call(
        body, grid=(nh, n_i), name="sb_bwd",
        in_specs=[pl.BlockSpec((t, HEAD), lambda h, i: (i, nh + h)),
                  pl.BlockSpec((t, HEAD), lambda h, i: (i, m0 + h)),
                  pl.BlockSpec((S, HEAD), lambda h, i: (0, m0 + nh + h)),
                  pl.BlockSpec((S, HEAD), lambda h, i: (0, m0 + 2 * nh + h)),
                  pl.BlockSpec(before2.shape, lambda h, i: (0, 0)), pl.BlockSpec(memory_space=pl.ANY)],
        out_specs=[pl.BlockSpec((t, HEAD), lambda h, i: (i, h)), full, full],
        out_shape=[jax.ShapeDtypeStruct((S, half), BF16)] * 3,
        scratch_shapes=[pltpu.VMEM((S, HEAD), F32), pltpu.VMEM((S, HEAD), F32),
                        pltpu.VMEM((2, t, t), BF16), pltpu.SemaphoreType.DMA((2,))],
        compiler_params=_params(("arbitrary", "arbitrary")),
    )(dcat, projb, projb, projb, before2, w_all)


def local_step(x, target, mix_norm, ffn_norm, final_norm, lb_logits, hg_norm, get_w_in, get_w_rest, send):
    S, D = x.shape
    half = D // 2
    nh = half // HEAD
    tm = ROW_TILE
    tk = REDUCE_TILE
    row = lambda i, j: (i, 0)

    lb = jax.nn.softmax(lb_logits, axis=0)[0:1]

    h0, r0 = rms_fwd(x, mix_norm[0:1], BF16)
    w_in = get_w_in(h0)
    nbi = w_in.shape[2]
    col = jnp.arange(N_DEV * nbi) // half
    col_scale = jnp.where(col == 4, SB_QUERY_SCALE, 1.0).astype(F32)[None]
    proj, projb = matmul(
        "proj_in", [h0], [w_in], grid=(N_DEV, S // ROW_TILE_WIDE, 1),
        a_spec=pl.BlockSpec((ROW_TILE_WIDE, D), lambda j, i, k: (i, 0)),
        b_spec=pl.BlockSpec((None, D, nbi), lambda j, i, k: (j, 0, 0)),
        out_spec=pl.BlockSpec((ROW_TILE_WIDE, nbi), lambda j, i, k: (i, j)), out_shape=(S, N_DEV * nbi),
        out_dtypes=[F32, BF16], acc_shape=(8, 128),
        bf16_scale=col_scale, bf16_scale_spec=pl.BlockSpec((1, nbi), lambda j, i, k: (0, j)))
    oa, oraw, states = hgrn_forward(proj, lb, hg_norm)
    ob, sb_weights = sb_forward(projb, nh, 4 * nh)
    cat = jnp.concatenate([oa, ob], axis=1)
    w_out, pool_w, pool_scale, wg, wu, wd = get_w_rest(cat)
    x1, h1, r1 = matmul_residual_rms("mix_out", cat, w_out, x, ffn_norm[0:1])
    x2, ffn0 = ffn_forward(h1, x1, wg[0], wu[0], wd[0])

    h2, r2 = rms_fwd(x2, mix_norm[1:2], F32)
    x3, pooled, h3, r3 = pool_forward(h2, x2, pool_w, pool_scale, ffn_norm[1:2])
    x4, ffn1 = ffn_forward(h3, x3, wg[1], wu[1], wd[1])

    loss_blk, dx4, dx4b, d_final = loss_and_final_bwd(x4, final_norm, target)

    (dx3, _, d_ffn1), dwg1, dwu1, dwd1 = ffn_backward(dx4b, h3, ffn1, wg[1], wu[1], wd[1],
                                                      x3, r3, ffn_norm[1:2], dx4)
    dx3 = send("ffn1", dict(ffn_w_gate_1=dwg1, ffn_w_up_1=dwu1, ffn_w_down_1=dwd1), dx3)
    dmixed, dpooled, d_pscale = pool_backward_mix(dx3, pooled, pool_w, pool_scale)
    G = len(POOL_WINDOWS)
    P = D // G
    (d_pool_w,) = matmul(
        "pool_dw", [pooled], [dmixed], grid=(G, S // tk),
        a_spec=pl.BlockSpec((tk, P), lambda g, k: (k, g)), b_spec=pl.BlockSpec((tk, P), lambda g, k: (k, g)),
        out_spec=pl.BlockSpec((None, P, P), lambda g, k: (g, 0, 0)), out_shape=(G, P, P), out_dtypes=[BF16],
        acc_shape=(P, P), trans_a=True)
    dx2, dx2b, d_mix1 = pool_backward_window(dpooled, x2, r2, mix_norm[1:2], dx3)

    (dx1, dx1b, d_ffn0), dwg0, dwu0, dwd0 = ffn_backward(dx2b, h1, ffn0, wg[0], wu[0], wd[0],
                                                         x1, r1, ffn_norm[0:1], dx2)
    (dcat,) = matmul(
        "mix_out_dx", [dx1b], [w_out], grid=(S // tm, 1),
        a_spec=pl.BlockSpec((tm, D), row), b_spec=pl.BlockSpec((D, D), lambda i, k: (0, 0)),
        out_spec=pl.BlockSpec((tm, D), row), out_shape=(S, D), out_dtypes=[F32], acc_shape=(8, 128),
        trans_b=True)
    (d_w_out,) = matmul(
        "mix_out_dw", [cat], [dx1b], grid=(2, S // tk),
        a_spec=pl.BlockSpec((tk, half), lambda m, k: (k, m)), b_spec=pl.BlockSpec((tk, D), lambda m, k: (k, 0)),
        out_spec=pl.BlockSpec((half, D), lambda m, k: (m, 0)), out_shape=(D, D), out_dtypes=[BF16],
        acc_shape=(half, D), trans_a=True)
    dcat = send("layer0", dict(ffn_w_gate_0=dwg0, ffn_w_up_0=dwu0, ffn_w_down_0=dwd0, pool_w=d_pool_w,
                               ab_w_out=d_w_out), dcat)
    dqa, dfa, dia, dga, d_lb, d_hg = hgrn_backward(dcat, proj, oraw, states, lb, hg_norm)
    dqb, dkb, dvb = sb_backward(dcat, projb, sb_weights, nh, 4 * nh)
    dproj = jnp.concatenate([dqa, dfa, dia, dga, dqb, dkb, dvb], axis=1)
    (d_w_in,) = matmul(
        "proj_in_dw", [h0], [dproj], grid=(N_DEV, S // tk),
        a_spec=pl.BlockSpec((tk, D), lambda j, k: (k, 0)), b_spec=pl.BlockSpec((tk, nbi), lambda j, k: (k, j)),
        out_spec=pl.BlockSpec((None, D, nbi), lambda j, k: (j, 0, 0)), out_shape=(N_DEV, D, nbi),
        out_dtypes=[BF16], acc_shape=(D, nbi), trans_a=True)
    dproj = send("w_in", dict(ab_w_in=d_w_in), dproj)
    dx0, _, d_mix0 = matmul_rms_bwd(
        "proj_in_dx", [dproj], [w_in], grid=(S // tm, N_DEV),
        a_spec=pl.BlockSpec((tm, nbi), lambda i, j: (i, j)),
        b_spec=pl.BlockSpec((None, D, nbi), lambda i, j: (j, 0, 0)),
        tm=tm, x=x, r=r0, gain=mix_norm[0:1], dres=dx1)

    d_l0 = d_lb * lb * (1.0 - lb)
    small = dict(
        loss=loss_blk[0:1, 0:1],
        mix_norm=jnp.concatenate([d_mix0, d_mix1], axis=0),
        ffn_norm=jnp.concatenate([d_ffn0, d_ffn1], axis=0),
        final_norm=d_final,
        lb_logits=jnp.concatenate([d_l0, -d_l0], axis=0),
        hg_out_norm=jnp.sum(d_hg, axis=0),
        pool_scale=d_pscale,
    )
    return dx0, small


def _my_index():
    return 4 * lax.axis_index("x") + 2 * lax.axis_index("y") + lax.axis_index("c")


def _peer(r):
    x, y, c = lax.axis_index("x"), lax.axis_index("y"), lax.axis_index("c")
    px = 1 - x if (r >> 2) & 1 else x
    py = 1 - y if (r >> 1) & 1 else y
    pc = 1 - c if r & 1 else c
    return (px, py, pc), 4 * px + 2 * py + pc


def gather_two_level(name, shard):
    def body(x_ref, out_ref, send_sems, recv_sems, local_sem):
        x, y, c = lax.axis_index("x"), lax.axis_index("y"), lax.axis_index("c")
        me, sibling = (x, y, c), (x, y, 1 - c)
        chips = [(1 - x, y), (x, 1 - y), (1 - x, 1 - y)]

        def slot(px, py, pc):
            return out_ref.at[4 * px + 2 * py + pc]

        def copy(k, block, to, src=None):
            return pltpu.make_async_remote_copy(
                src_ref=slot(*block) if src is None else src, dst_ref=slot(*block), send_sem=send_sems.at[k],
                recv_sem=recv_sems.at[k], device_id=to, device_id_type=MESH)

        mine = pltpu.make_async_copy(x_ref, slot(*me), local_sem)
        mine.start()
        first = [copy(0, me, sibling, src=x_ref)]
        first += [copy(1 + j, me, (*chip, c), src=x_ref) for j, chip in enumerate(chips)]
        for cp in first:
            cp.start()
        passed = [copy(4 + j, (*chip, c), sibling) for j, chip in enumerate(chips)]
        for j, chip in enumerate(chips):
            copy(1 + j, (*chip, c), me).wait_recv()
            passed[j].start()
        copy(0, sibling, me).wait_recv()
        for j, chip in enumerate(chips):
            copy(4 + j, (*chip, 1 - c), me).wait_recv()
        for cp in first + passed:
            cp.wait_send()
        mine.wait()

    any_spec = pl.BlockSpec(memory_space=pl.ANY)
    return pl.pallas_call(
        body, name=name, in_specs=[any_spec], out_specs=any_spec,
        out_shape=jax.ShapeDtypeStruct((N_DEV,) + shard.shape, shard.dtype),
        scratch_shapes=[pltpu.SemaphoreType.DMA((N_DEV - 1,)), pltpu.SemaphoreType.DMA((N_DEV - 1,)),
                        pltpu.SemaphoreType.DMA],
    )(shard)


def exchange(name, arrays, gather):
    n = len(arrays)
    n_peers = N_DEV - 1

    def body(*refs):
        ins, outs = refs[:n], refs[n:2 * n]
        send_sems, recv_sems, local_sems = refs[2 * n:]
        me = _my_index()
        local = []
        for a in range(n):
            src = ins[a] if gather else ins[a].at[me]
            cp = pltpu.make_async_copy(src, outs[a].at[me], local_sems.at[a])
            cp.start()
            local.append(cp)
        remote = []
        for a in range(n):
            for r in range(1, N_DEV):
                peer, pidx = _peer(r)
                src = ins[a] if gather else ins[a].at[pidx]
                cp = pltpu.make_async_remote_copy(
                    src_ref=src, dst_ref=outs[a].at[me], send_sem=send_sems.at[a * n_peers + r - 1],
                    recv_sem=recv_sems.at[a * n_peers + r - 1], device_id=peer, device_id_type=MESH)
                cp.start()
                remote.append((cp, a, r))
        for cp, a, r in remote:
            _, pidx = _peer(r)
            src = ins[a] if gather else ins[a].at[pidx]
            pltpu.make_async_remote_copy(
                src_ref=src, dst_ref=outs[a].at[pidx], send_sem=send_sems.at[a * n_peers + r - 1],
                recv_sem=recv_sems.at[a * n_peers + r - 1], device_id=_peer(r)[0], device_id_type=MESH).wait_recv()
        for cp, a, r in remote:
            cp.wait_send()
        for cp in local:
            cp.wait()

    out_shape = [jax.ShapeDtypeStruct(((N_DEV,) + a.shape) if gather else a.shape, a.dtype) for a in arrays]
    any_spec = pl.BlockSpec(memory_space=pl.ANY)
    return pl.pallas_call(
        body, name=name, in_specs=[any_spec] * n, out_specs=[any_spec] * n, out_shape=out_shape,
        scratch_shapes=[pltpu.SemaphoreType.DMA((n * n_peers,)), pltpu.SemaphoreType.DMA((n * n_peers,)),
                        pltpu.SemaphoreType.DMA((n,))],
    )(*arrays)


_HBM = pl.BlockSpec(memory_space=pltpu.HBM)
_SEM = pl.BlockSpec(memory_space=pltpu.SEMAPHORE)
_EFFECT = pltpu.SideEffectType.DATAFLOW_SIDE_EFFECTING


def _landing(arrays, gather):
    me = _my_index()
    lands = []
    for a in arrays:
        own = a[None] if gather else lax.dynamic_slice_in_dim(a, me, 1, axis=0)
        shape = ((N_DEV,) + a.shape) if gather else a.shape
        lands.append(lax.dynamic_update_slice_in_dim(lax.empty(shape, a.dtype), own, me, axis=0))
    return lands


def exchange_start(name, arrays, gather, carry):
    n = len(arrays)
    n_peers = N_DEV - 1
    lands = _landing(arrays, gather)
    n_thru = 2 * n + 1

    def body(*refs):
        src, land = refs[:n], refs[n:2 * n]
        send_sems, recv_sems = refs[n_thru], refs[n_thru + 1]
        token = refs[-1]
        me = _my_index()
        for a in range(n):
            for r in range(1, N_DEV):
                peer, pidx = _peer(r)
                pltpu.make_async_remote_copy(
                    src_ref=src[a] if gather else src[a].at[pidx], dst_ref=land[a].at[me],
                    send_sem=send_sems.at[a * n_peers + r - 1], recv_sem=recv_sems.at[a * n_peers + r - 1],
                    device_id=peer, device_id_type=MESH).start()
        token[...] = jnp.zeros_like(token)

    operands = list(arrays) + lands + [carry]
    outs = pl.pallas_call(
        body, name=name,
        out_shape=(pltpu.SemaphoreType.DMA((n * n_peers,)), pltpu.SemaphoreType.DMA((n * n_peers,)),
                   *[pltpu.HBM(a.shape, a.dtype) for a in operands], jax.ShapeDtypeStruct((8, 128), F32)),
        in_specs=[_HBM] * n_thru,
        out_specs=(_SEM, _SEM, *([_HBM] * n_thru), pl.BlockSpec(memory_space=pltpu.VMEM)),
        input_output_aliases={i: 2 + i for i in range(n_thru)},
        compiler_params=pltpu.CompilerParams(has_side_effects=_EFFECT),
    )(*[pltpu.with_memory_space_constraint(a, pltpu.HBM) for a in operands])
    handle = (outs[0], outs[1], list(outs[2:2 + n]), list(outs[2 + n:2 + 2 * n]), gather)
    return handle, outs[2 + 2 * n]


def exchange_wait(name, handle, after):
    send_sems, recv_sems, srcs, lands, gather = handle
    n = len(srcs)
    n_peers = N_DEV - 1

    def body(*refs):
        src, land = refs[:n], refs[n:2 * n]
        send_s, recv_s = refs[2 * n], refs[2 * n + 1]
        for a in range(n):
            for r in range(1, N_DEV):
                peer, pidx = _peer(r)
                cp = pltpu.make_async_remote_copy(
                    src_ref=src[a] if gather else src[a].at[pidx], dst_ref=land[a].at[pidx],
                    send_sem=send_s.at[a * n_peers + r - 1], recv_sem=recv_s.at[a * n_peers + r - 1],
                    device_id=peer, device_id_type=MESH)
                cp.wait_send()
                cp.wait_recv()

    shapes = [pltpu.HBM(a.shape, a.dtype) for a in srcs] + [pltpu.HBM(l.shape, l.dtype) for l in lands]
    outs = pl.pallas_call(
        body, name=name, out_shape=tuple(shapes),
        in_specs=[_HBM] * (2 * n) + [_SEM, _SEM, pl.BlockSpec(memory_space=pl.ANY)],
        out_specs=tuple([_HBM] * (2 * n)),
        input_output_aliases={i: i for i in range(2 * n)},
        compiler_params=pltpu.CompilerParams(has_side_effects=_EFFECT),
    )(*srcs, *lands, send_sems, recv_sems, after)
    return list(outs[n:])


def _row_tile(rows, cap=256):
    best = None
    for t in range(16, min(rows, cap) + 1, 16):
        if rows % t == 0:
            best = t
    return best if best is not None else rows


def sum_slots(name, recv):
    n, R, C = recv.shape
    tr = _row_tile(R)

    def body(r_ref, o_ref):
        g = r_ref[0].astype(F32)
        for d in range(1, n):
            g = g + r_ref[d].astype(F32)
        o_ref[...] = g

    return pl.pallas_call(
        body, grid=(R // tr,), name=name,
        in_specs=[pl.BlockSpec((n, tr, C), lambda i: (0, i, 0))],
        out_specs=pl.BlockSpec((tr, C), lambda i: (i, 0)),
        out_shape=jax.ShapeDtypeStruct((R, C), F32),
        compiler_params=_params(("arbitrary",)),
    )(recv)


def adamw(name, recv, w, m, v, layer=None, prev=None):
    n, R, C = recv.shape
    tr = _row_tile(R)

    def body(r_ref, w_ref, m_ref, v_ref, *rest):
        g_ref, d_ref, nm_ref, nv_ref = rest[-4:]
        g = r_ref[0].astype(F32)
        for d in range(1, n):
            g = g + r_ref[d].astype(F32)
        mm = ADAM_B1 * m_ref[...] + (1.0 - ADAM_B1) * g
        vv = ADAM_B2 * v_ref[...] + (1.0 - ADAM_B2) * (g * g)
        m_hat = mm / (1.0 - ADAM_B1 ** ADAM_STEP)
        v_hat = vv / (1.0 - ADAM_B2 ** ADAM_STEP)
        g_ref[...] = g
        d_ref[...] = -ADAM_LR * (m_hat / (jnp.sqrt(v_hat) + ADAM_EPS) + ADAM_WD * w_ref[...])
        nm_ref[...] = mm
        nv_ref[...] = vv

    if layer is None:
        row = pl.BlockSpec((tr, C), lambda i: (i, 0))
        shape = (R, C)
    else:
        row = pl.BlockSpec((None, tr, C), lambda i: (layer, i, 0))
        shape = w.shape
    prev = [] if prev is None else list(prev)
    return pl.pallas_call(
        body, grid=(R // tr,), name=name,
        in_specs=[pl.BlockSpec((n, tr, C), lambda i: (0, i, 0)), row, row, row]
                 + [pl.BlockSpec(memory_space=pl.ANY)] * len(prev),
        out_specs=[row] * 4,
        out_shape=[jax.ShapeDtypeStruct(shape, F32)] * 4,
        input_output_aliases={4 + o: o for o in range(len(prev))},
        compiler_params=_params(("arbitrary",)),
    )(recv, w, m, v, *prev)


def _adamw_nd(name, recv, w, m, v):
    shp = w.shape
    C = shp[-1]
    flat = lambda a: a.reshape(-1, C)
    outs = adamw(name, recv.reshape(recv.shape[0], -1, C), flat(w), flat(m), flat(v))
    return [o.reshape(shp) for o in outs]


_SMALL_NAMES = ("loss", "mix_norm", "ffn_norm", "final_norm", "lb_logits", "hg_out_norm", "pool_scale")
_LANES = 128


def _pack_small(parts):
    rows, layout = [], {}
    at = 0
    for name in parts:
        flat = parts[name].reshape(-1).astype(F32)
        n_rows = -(-flat.shape[0] // (8 * _LANES)) * 8
        flat = jnp.pad(flat, (0, n_rows * _LANES - flat.shape[0]))
        rows.append(flat.reshape(n_rows, _LANES))
        layout[name] = (at, parts[name].shape)
        at += n_rows
    return jnp.concatenate(rows, axis=0), layout


def _unpack_small(pack, layout):
    out = {}
    for name, (at, shape) in layout.items():
        size = int(np.prod(shape))
        n_rows = -(-size // _LANES)
        out[name] = pack[at:at + n_rows].reshape(-1)[:size].reshape(shape)
    return out


def kernel(x, mix_norm, ffn_norm, final_norm, ab_w_in, lb_logits, hg_out_norm, ab_w_out, pool_w, pool_scale, ffn_w_gate, ffn_w_up, ffn_w_down, loss_target, m_mix_norm, m_ffn_norm, m_final_norm, m_ab_w_in, m_lb_logits, m_hg_out_norm, m_ab_w_out, m_pool_w, m_pool_scale, m_ffn_w_gate, m_ffn_w_up, m_ffn_w_down, v_mix_norm, v_ffn_norm, v_final_norm, v_ab_w_in, v_lb_logits, v_hg_out_norm, v_ab_w_out, v_pool_w, v_pool_scale, v_ffn_w_gate, v_ffn_w_up, v_ffn_w_down):
    D = x.shape[-1]
    n_layers = ffn_w_gate.shape[0]
    G = pool_w.shape[1]
    P = pool_w.shape[3]
    me = _my_index()

    rest = [ab_w_out[0], pool_w[0]]
    for l in range(n_layers):
        rest += [ffn_w_gate[l], ffn_w_up[l], ffn_w_down[l]]
    rest = [s.astype(BF16) for s in rest] + [pool_scale]
    rest_handle = []

    def get_w_in(after):
        w_in = gather_two_level("gather_w_in", ab_w_in[0].astype(BF16))
        handle, w_in = exchange_start("gather_rest_start", rest, True, w_in)
        rest_handle.append(handle)
        return w_in

    def get_w_rest(after):
        got = exchange_wait("gather_rest_wait", rest_handle[0], after)
        w_out_g = got[0].reshape(D, D)
        pool_g = got[1].transpose(1, 0, 2, 3).reshape(G, P, P)
        wg = [got[2 + 3 * l] for l in range(n_layers)]
        wu = [got[3 + 3 * l] for l in range(n_layers)]
        wd = [got[4 + 3 * l] for l in range(n_layers)]
        return w_out_g, pool_g, got[-1].reshape(1, D), wg, wu, wd

    in_flight = []

    def send(tag, grads, carry):
        if "pool_w" in grads:
            grads = dict(grads, pool_w=grads["pool_w"].reshape(G, N_DEV, P // N_DEV, P).transpose(1, 0, 2, 3))
        if "ab_w_out" in grads:
            grads = dict(grads, ab_w_out=grads["ab_w_out"].reshape(N_DEV, D // N_DEV, D))
        handle, carry = exchange_start("grads_" + tag + "_start", list(grads.values()), False, carry)
        in_flight.append((tag, list(grads.keys()), handle))
        return carry

    dx0, small = local_step(x[0], loss_target[0], mix_norm, ffn_norm, final_norm[None],
                            lb_logits, hg_out_norm, get_w_in, get_w_rest, send)

    recv = {}
    for tag, names, handle in in_flight:
        recv.update(zip(names, exchange_wait("grads_" + tag + "_wait", handle, dx0)))
    small_pack, layout = _pack_small({k: small[k] for k in _SMALL_NAMES})
    (small_all,) = exchange("gather_small", [small_pack], gather=True)
    tot = _unpack_small(sum_slots("sum_small", small_all), layout)

    res = {}
    res["ab_w_in"] = _adamw_nd("adamw_w_in", recv["ab_w_in"], ab_w_in, m_ab_w_in, v_ab_w_in)
    res["ab_w_out"] = _adamw_nd("adamw_w_out", recv["ab_w_out"], ab_w_out, m_ab_w_out, v_ab_w_out)
    res["pool_w"] = _adamw_nd("adamw_pool_w", recv["pool_w"], pool_w, m_pool_w, v_pool_w)
    ffn_in = {"ffn_w_gate": (ffn_w_gate, m_ffn_w_gate, v_ffn_w_gate),
              "ffn_w_up": (ffn_w_up, m_ffn_w_up, v_ffn_w_up),
              "ffn_w_down": (ffn_w_down, m_ffn_w_down, v_ffn_w_down)}
    for name, (w, m, v) in ffn_in.items():
        flip = name != "ffn_w_down"
        if flip:
            w, m, v = (jnp.swapaxes(a, 1, 2) for a in (w, m, v))
        outs = None
        for l in range(n_layers):
            outs = adamw("adamw_" + name, recv[name + "_" + str(l)], w, m, v, layer=l, prev=outs)
        res[name] = [jnp.swapaxes(o, 1, 2) for o in outs] if flip else outs

    n_ps = pool_scale.shape[1]
    small_g = dict(tot)
    small_g["pool_scale"] = lax.dynamic_slice(tot["pool_scale"], (0, me * n_ps), (1, n_ps))
    small_w = dict(mix_norm=(mix_norm, m_mix_norm, v_mix_norm), ffn_norm=(ffn_norm, m_ffn_norm, v_ffn_norm),
                   final_norm=(final_norm, m_final_norm, v_final_norm),
                   lb_logits=(lb_logits, m_lb_logits, v_lb_logits),
                   hg_out_norm=(hg_out_norm, m_hg_out_norm, v_hg_out_norm),
                   pool_scale=(pool_scale, m_pool_scale, v_pool_scale))
    g_pack, lay2 = _pack_small({k: small_g[k].reshape(small_w[k][0].shape) for k in small_w})
    w_pack, _ = _pack_small({k: small_w[k][0] for k in small_w})
    m_pack, _ = _pack_small({k: small_w[k][1] for k in small_w})
    v_pack, _ = _pack_small({k: small_w[k][2] for k in small_w})
    small_out = [_unpack_small(o, lay2) for o in adamw("adamw_small", g_pack[None], w_pack, m_pack, v_pack)]
    for k in small_w:
        res[k] = [small_out[o][k] for o in range(4)]

    order = ("mix_norm", "ffn_norm", "final_norm", "ab_w_in", "lb_logits", "hg_out_norm", "ab_w_out", "pool_w",
             "pool_scale", "ffn_w_gate", "ffn_w_up", "ffn_w_down")
    outs = [tot["loss"].reshape(()), dx0[None]]
    for o in range(4):
        outs += [res[k][o] for k in order]
    return tuple(outs)
```

```python
import math

import numpy as np
import jax
import jax.numpy as jnp
from jax import lax
from jax.experimental import pallas as pl
from jax.experimental.pallas import tpu as pltpu

F32 = jnp.float32
BF16 = jnp.bfloat16

N_DEV = 8
RMS_EPS = 1e-6
HEAD = 128
HG_CHUNK = 64
HG_HEADS_PER_BLOCK = 8
POOL_WINDOWS = (2, 4, 8, 16)
POOL_HALO = 16
ADAM_LR, ADAM_B1, ADAM_B2, ADAM_EPS, ADAM_WD, ADAM_STEP = 0.001, 0.9, 0.999, 1e-08, 0.01, 10
VMEM_LIMIT_BYTES = 60 * 1024 * 1024
MESH = pl.DeviceIdType.MESH

ROW_TILE = 512
ROW_TILE_WIDE = 1024
REDUCE_TILE = 2048
POOL_TILE = 512
HG_TILE = 512
SB_TILE = 512


def _params(sem):
    return pltpu.CompilerParams(dimension_semantics=sem, vmem_limit_bytes=VMEM_LIMIT_BYTES)


def _sigmoid(x):
    return 1.0 / (1.0 + jnp.exp(-x))


def rms_fwd(x, gain, out_dtype, ts=ROW_TILE):
    S, D = x.shape

    def body(x_ref, g_ref, h_ref, r_ref):
        xv = x_ref[...]
        r = lax.rsqrt(jnp.mean(xv * xv, axis=-1, keepdims=True) + RMS_EPS)
        h_ref[...] = ((xv * r) * g_ref[...]).astype(h_ref.dtype)
        r_ref[...] = r

    return pl.pallas_call(
        body, grid=(S // ts,), name="rms_fwd",
        in_specs=[pl.BlockSpec((ts, D), lambda i: (i, 0)), pl.BlockSpec((1, D), lambda i: (0, 0))],
        out_specs=[pl.BlockSpec((ts, D), lambda i: (i, 0)), pl.BlockSpec((ts, 1), lambda i: (i, 0))],
        out_shape=[jax.ShapeDtypeStruct((S, D), out_dtype), jax.ShapeDtypeStruct((S, 1), F32)],
        compiler_params=_params(("arbitrary",)),
    )(x, gain)


RMS_BWD_ROWS = 128


def _rms_bwd_tile(first, dh_of, x_ref, r_ref, g_ref, dres_ref, dx_ref, dxb_ref, dg_ref, rows):
    gv = g_ref[...]
    part = None
    for c in range(rows // RMS_BWD_ROWS):
        sl = slice(c * RMS_BWD_ROWS, (c + 1) * RMS_BWD_ROWS)
        rr = r_ref[sl, :]
        xh = x_ref[sl, :] * rr
        dhv = dh_of(sl)
        dxh = dhv * gv
        dx = dres_ref[sl, :] + rr * (dxh - xh * jnp.mean(dxh * xh, axis=-1, keepdims=True))
        dx_ref[sl, :] = dx
        dxb_ref[sl, :] = dx.astype(BF16)
        p = jnp.sum(dhv * xh, axis=0, keepdims=True)
        part = p if part is None else part + p

    @pl.when(first)
    def _():
        dg_ref[...] = part

    @pl.when(jnp.logical_not(first))
    def _():
        dg_ref[...] += part


def matmul_rms_bwd(name, a_ops, b_ops, *, grid, a_spec, b_spec, tm, x, r, gain, dres):
    S, D = x.shape
    n_pairs = len(a_ops)
    nk = grid[1]
    dn = (((1,), (1,)), ((), ()))

    def body(*refs):
        a_refs = refs[:n_pairs]
        b_refs = refs[n_pairs:2 * n_pairs]
        x_ref, r_ref, g_ref, dres_ref, dx_ref, dxb_ref, dg_ref, acc_ref = refs[2 * n_pairs:]
        i = pl.program_id(0)
        k = pl.program_id(1)

        @pl.when(k == 0)
        def _():
            acc_ref[...] = jnp.zeros_like(acc_ref)

        part = None
        for ar, br in zip(a_refs, b_refs):
            d = lax.dot_general(ar[...], br[...], dn, preferred_element_type=F32)
            part = d if part is None else part + d
        acc_ref[...] += part

        @pl.when(k == nk - 1)
        def _():
            _rms_bwd_tile(i == 0, lambda sl: acc_ref[sl, :], x_ref, r_ref, g_ref, dres_ref, dx_ref, dxb_ref,
                          dg_ref, tm)

    row = pl.BlockSpec((tm, D), lambda i, k: (i, 0))
    vec = pl.BlockSpec((1, D), lambda i, k: (0, 0))
    return pl.pallas_call(
        body, grid=grid, name=name,
        in_specs=[a_spec] * n_pairs + [b_spec] * n_pairs
                 + [row, pl.BlockSpec((tm, 1), lambda i, k: (i, 0)), vec, row],
        out_specs=[row, row, vec],
        out_shape=[jax.ShapeDtypeStruct((S, D), F32), jax.ShapeDtypeStruct((S, D), BF16),
                   jax.ShapeDtypeStruct((1, D), F32)],
        scratch_shapes=[pltpu.VMEM((tm, D), F32)],
        compiler_params=_params(("arbitrary", "arbitrary")),
    )(*a_ops, *b_ops, x, r, gain, dres)


def matmul_residual_rms(name, a, b, res, gain, tm=ROW_TILE):
    S, K = a.shape
    N = b.shape[1]

    def body(a_ref, b_ref, res_ref, g_ref, xo_ref, h_ref, r_ref):
        xo = res_ref[...] + jnp.dot(a_ref[...], b_ref[...], preferred_element_type=F32)
        xo_ref[...] = xo
        r = lax.rsqrt(jnp.mean(xo * xo, axis=-1, keepdims=True) + RMS_EPS)
        h_ref[...] = ((xo * r) * g_ref[...]).astype(BF16)
        r_ref[...] = r

    row = pl.BlockSpec((tm, N), lambda i: (i, 0))
    return pl.pallas_call(
        body, grid=(S // tm,), name=name,
        in_specs=[pl.BlockSpec((tm, K), lambda i: (i, 0)), pl.BlockSpec((K, N), lambda i: (0, 0)), row,
                  pl.BlockSpec((1, N), lambda i: (0, 0))],
        out_specs=[row, row, pl.BlockSpec((tm, 1), lambda i: (i, 0))],
        out_shape=[jax.ShapeDtypeStruct((S, N), F32), jax.ShapeDtypeStruct((S, N), BF16),
                   jax.ShapeDtypeStruct((S, 1), F32)],
        compiler_params=_params(("arbitrary",)),
    )(a, b, res, gain)


def loss_and_final_bwd(x, gain, target, ts=ROW_TILE):
    S, D = x.shape

    def body(x_ref, g_ref, t_ref, loss_ref, dx_ref, dxb_ref, dg_ref):
        i = pl.program_id(0)
        xv = x_ref[...]
        rr = lax.rsqrt(jnp.mean(xv * xv, axis=-1, keepdims=True) + RMS_EPS)
        xh = xv * rr
        err = xh * g_ref[...] - t_ref[...]
        part_loss = 0.5 * jnp.sum(jnp.mean(err * err, axis=-1, keepdims=True))
        dy = err / D
        dxh = dy * g_ref[...]
        dx = rr * (dxh - xh * jnp.mean(dxh * xh, axis=-1, keepdims=True))
        dx_ref[...] = dx
        dxb_ref[...] = dx.astype(BF16)
        part = jnp.sum(dy * xh, axis=0, keepdims=True)

        @pl.when(i == 0)
        def _():
            dg_ref[...] = part
            loss_ref[...] = jnp.zeros_like(loss_ref) + part_loss

        @pl.when(i > 0)
        def _():
            dg_ref[...] += part
            loss_ref[...] += part_loss

    row = pl.BlockSpec((ts, D), lambda i: (i, 0))
    vec = pl.BlockSpec((1, D), lambda i: (0, 0))
    return pl.pallas_call(
        body, grid=(S // ts,), name="loss_final",
        in_specs=[row, vec, row],
        out_specs=[pl.BlockSpec((8, 128), lambda i: (0, 0)), row, row, vec],
        out_shape=[jax.ShapeDtypeStruct((8, 128), F32), jax.ShapeDtypeStruct((S, D), F32),
                   jax.ShapeDtypeStruct((S, D), BF16), jax.ShapeDtypeStruct((1, D), F32)],
        compiler_params=_params(("arbitrary",)),
    )(x, gain, target)


def matmul(name, a_ops, b_ops, *, grid, a_spec, b_spec, out_spec, out_shape, out_dtypes, acc_shape,
           trans_a=False, trans_b=False, res=None, res_spec=None, bf16_scale=None, bf16_scale_spec=None):
    n_pairs = len(a_ops)
    n_out = len(out_dtypes)
    nk = grid[-1]
    kaxis = len(grid) - 1
    dn = (((0,) if trans_a else (1,), (1,) if trans_b else (0,)), ((), ()))

    def body(*refs):
        a_refs = refs[:n_pairs]
        b_refs = refs[n_pairs:2 * n_pairs]
        pos = 2 * n_pairs
        res_ref = None
        if res is not None:
            res_ref = refs[pos]
            pos += 1
        scale_ref = None
        if bf16_scale is not None:
            scale_ref = refs[pos]
            pos += 1
        out_refs = refs[pos:pos + n_out]
        acc_ref = refs[pos + n_out]
        k = pl.program_id(kaxis)
        in_place = n_out == 1 and out_dtypes[0] == F32
        target = out_refs[0] if in_place else acc_ref

        def finish(val):
            if res_ref is not None:
                val = val + res_ref[...]
            for o in out_refs:
                if scale_ref is not None and o.dtype == BF16:
                    o[...] = (val * scale_ref[...]).astype(BF16)
                else:
                    o[...] = val.astype(o.dtype)

        if nk > 1:
            @pl.when(k == 0)
            def _():
                if in_place and res_ref is not None:
                    target[...] = res_ref[...]
                else:
                    target[...] = jnp.zeros_like(target)

        part = None
        for ar, br in zip(a_refs, b_refs):
            d = lax.dot_general(ar[...].astype(BF16), br[...].astype(BF16), dn, preferred_element_type=F32)
            part = d if part is None else part + d

        if nk == 1:
            finish(part)
        else:
            target[...] += part
            if not in_place:
                @pl.when(k == nk - 1)
                def _():
                    finish(acc_ref[...])

    in_specs = [a_spec] * n_pairs + [b_spec] * n_pairs
    operands = list(a_ops) + list(b_ops)
    if res is not None:
        in_specs.append(res_spec)
        operands.append(res)
    if bf16_scale is not None:
        in_specs.append(bf16_scale_spec)
        operands.append(bf16_scale)
    return pl.pallas_call(
        body, grid=grid, name=name, in_specs=in_specs,
        out_specs=[out_spec] * n_out,
        out_shape=[jax.ShapeDtypeStruct(out_shape, dt) for dt in out_dtypes],
        scratch_shapes=[pltpu.VMEM(acc_shape, F32)],
        compiler_params=_params(("arbitrary",) * len(grid)),
    )(*operands)


def ffn_gate_up(h, wg, wu, tm=ROW_TILE_WIDE):
    S, D = h.shape
    nb = wg.shape[2]

    def body(h_ref, wg_ref, wu_ref, p_ref, r_ref, a_ref):
        for c in range(2):
            rows = slice(c * (tm // 2), (c + 1) * (tm // 2))
            hv = h_ref[rows, :]
            g = jnp.dot(hv, wg_ref[...], preferred_element_type=F32)
            u = jnp.dot(hv, wu_ref[...], preferred_element_type=F32)
            s = _sigmoid(g)
            p = g * s
            p_ref[rows, :] = p
            r_ref[rows, :] = u * (s * (1.0 + g * (1.0 - s)))
            a_ref[rows, :] = (p * u).astype(BF16)

    wspec = pl.BlockSpec((None, D, nb), lambda j, i: (j, 0, 0))
    ospec = pl.BlockSpec((None, tm, nb), lambda j, i: (j, i, 0))
    return pl.pallas_call(
        body, grid=(N_DEV, S // tm), name="ffn_gate_up",
        in_specs=[pl.BlockSpec((tm, D), lambda j, i: (i, 0)), wspec, wspec],
        out_specs=[ospec, ospec, ospec],
        out_shape=[jax.ShapeDtypeStruct((N_DEV, S, nb), F32), jax.ShapeDtypeStruct((N_DEV, S, nb), F32),
                   jax.ShapeDtypeStruct((N_DEV, S, nb), BF16)],
        compiler_params=_params(("arbitrary", "arbitrary")),
    )(h, wg, wu)


def ffn_bwd_hidden(dy, wd, p, r, tm=ROW_TILE_WIDE):
    S, D = dy.shape
    nb = wd.shape[1]

    def body(dy_ref, wd_ref, p_ref, r_ref, dg_ref, du_ref):
        for c in range(2):
            rows = slice(c * (tm // 2), (c + 1) * (tm // 2))
            da = lax.dot_general(dy_ref[rows, :], wd_ref[...], (((1,), (1,)), ((), ())),
                                 preferred_element_type=F32)
            du_ref[rows, :] = (da * p_ref[rows, :]).astype(BF16)
            dg_ref[rows, :] = (da * r_ref[rows, :]).astype(BF16)

    hspec = pl.BlockSpec((None, tm, nb), lambda j, i: (j, i, 0))
    return pl.pallas_call(
        body, grid=(N_DEV, S // tm), name="ffn_bwd_hidden",
        in_specs=[pl.BlockSpec((tm, D), lambda j, i: (i, 0)), pl.BlockSpec((None, nb, D), lambda j, i: (j, 0, 0)),
                  hspec, hspec],
        out_specs=[hspec, hspec],
        out_shape=[jax.ShapeDtypeStruct((N_DEV, S, nb), BF16), jax.ShapeDtypeStruct((N_DEV, S, nb), BF16)],
        compiler_params=_params(("arbitrary", "arbitrary")),
    )(dy, wd, p, r)


def ffn_forward(h, xres, wg, wu, wd, tm=ROW_TILE_WIDE):
    S, D = h.shape
    nb = wg.shape[2]
    g, u, a = ffn_gate_up(h, wg, wu)
    (xo,) = matmul(
        "ffn_down", [a], [wd], grid=(S // tm, N_DEV),
        a_spec=pl.BlockSpec((None, tm, nb), lambda i, j: (j, i, 0)),
        b_spec=pl.BlockSpec((None, nb, D), lambda i, j: (j, 0, 0)),
        out_spec=pl.BlockSpec((tm, D), lambda i, j: (i, 0)), out_shape=(S, D), out_dtypes=[F32],
        acc_shape=(tm, D), res=xres, res_spec=pl.BlockSpec((tm, D), lambda i, j: (i, 0)))
    return xo, (g, u, a)


def ffn_backward(dy_b, h, saved, wg, wu, wd, x, r, gain, dres, tm=ROW_TILE, tk=REDUCE_TILE):
    S, D = h.shape
    nb = wg.shape[2]
    g, u, a = saved
    dg, du = ffn_bwd_hidden(dy_b, wd, g, u)
    dx = matmul_rms_bwd(
        "ffn_dh", [dg, du], [wg, wu], grid=(S // tm, N_DEV),
        a_spec=pl.BlockSpec((None, tm, nb), lambda i, j: (j, i, 0)),
        b_spec=pl.BlockSpec((None, D, nb), lambda i, j: (j, 0, 0)),
        tm=tm, x=x, r=r, gain=gain, dres=dres)

    def wgrad_in(name, dhid):
        (dw,) = matmul(
            name, [dhid], [h], grid=(N_DEV, S // tk),
            a_spec=pl.BlockSpec((None, tk, nb), lambda j, k: (j, k, 0)),
            b_spec=pl.BlockSpec((tk, D), lambda j, k: (k, 0)),
            out_spec=pl.BlockSpec((None, nb, D), lambda j, k: (j, 0, 0)), out_shape=(N_DEV, nb, D),
            out_dtypes=[BF16], acc_shape=(nb, D), trans_a=True)
        return dw

    dwg = wgrad_in("ffn_dwg", dg)
    dwu = wgrad_in("ffn_dwu", du)
    (dwd,) = matmul(
        "ffn_dwd", [a], [dy_b], grid=(N_DEV, S // tk),
        a_spec=pl.BlockSpec((None, tk, nb), lambda j, k: (j, k, 0)),
        b_spec=pl.BlockSpec((tk, D), lambda j, k: (k, 0)),
        out_spec=pl.BlockSpec((None, nb, D), lambda j, k: (j, 0, 0)), out_shape=(N_DEV, nb, D),
        out_dtypes=[BF16], acc_shape=(nb, D), trans_a=True)
    return dx, dwg, dwu, dwd


def _pool_counts(row0, n, w):
    pos = row0 + lax.broadcasted_iota(jnp.int32, (n, 1), 0)
    return jnp.minimum(pos + 1, w).astype(F32)


def pool_forward(h, xres, w, scale, gain_next, ts=POOL_TILE):
    S, D = h.shape
    G = len(POOL_WINDOWS)
    P = D // G
    hb = ts // POOL_HALO

    def body(h_ref, halo_ref, x_ref, w_ref, s_ref, gn_ref, xo_ref, p_ref, hn_ref, rn_ref):
        i = pl.program_id(0)
        for gi, win in enumerate(POOL_WINDOWS):
            cols = slice(gi * P, (gi + 1) * P)
            cur = h_ref[:, cols]
            halo = jnp.where(i > 0, halo_ref[:, cols], 0.0)
            acc = jnp.concatenate([halo, cur], axis=0)
            step = 1
            while step < win:
                acc = acc + pltpu.roll(acc, step, 0)
                step *= 2
            wsum = acc[POOL_HALO:, :]
            pooled = wsum / _pool_counts(i * ts, ts, win) - cur
            pb = pooled.astype(BF16)
            p_ref[:, cols] = pb
            mixed = jnp.dot(pb, w_ref[gi], preferred_element_type=F32)
            xo_ref[:, cols] = x_ref[:, cols] + mixed * s_ref[:, cols]
        xo = xo_ref[...]
        r = lax.rsqrt(jnp.mean(xo * xo, axis=-1, keepdims=True) + RMS_EPS)
        hn_ref[...] = ((xo * r) * gn_ref[...]).astype(BF16)
        rn_ref[...] = r

    row = pl.BlockSpec((ts, D), lambda i: (i, 0))
    vec = pl.BlockSpec((1, D), lambda i: (0, 0))
    return pl.pallas_call(
        body, grid=(S // ts,), name="pool_fwd",
        in_specs=[row, pl.BlockSpec((POOL_HALO, D), lambda i: (jnp.maximum(i * hb - 1, 0), 0)), row,
                  pl.BlockSpec((G, P, P), lambda i: (0, 0, 0)), vec, vec],
        out_specs=[row, row, row, pl.BlockSpec((ts, 1), lambda i: (i, 0))],
        out_shape=[jax.ShapeDtypeStruct((S, D), F32), jax.ShapeDtypeStruct((S, D), BF16),
                   jax.ShapeDtypeStruct((S, D), BF16), jax.ShapeDtypeStruct((S, 1), F32)],
        compiler_params=_params(("arbitrary",)),
    )(h, h, xres, w, scale, gain_next)


def pool_backward_mix(dx, pooled, w, scale, ts=POOL_TILE):
    S, D = dx.shape
    G = len(POOL_WINDOWS)
    P = D // G

    def body(dx_ref, p_ref, w_ref, s_ref, dm_ref, dp_ref, ds_ref):
        i = pl.program_id(0)
        parts = []
        for gi in range(G):
            cols = slice(gi * P, (gi + 1) * P)
            dxv = dx_ref[:, cols]
            dmb = (dxv * s_ref[:, cols]).astype(BF16)
            dm_ref[:, cols] = dmb
            dp_ref[:, cols] = lax.dot_general(dmb, w_ref[gi], (((1,), (1,)), ((), ())),
                                              preferred_element_type=F32)
            mixed = jnp.dot(p_ref[:, cols], w_ref[gi], preferred_element_type=F32)
            parts.append(jnp.sum(dxv * mixed, axis=0, keepdims=True))
        part = jnp.concatenate(parts, axis=1)

        @pl.when(i == 0)
        def _():
            ds_ref[...] = part

        @pl.when(i > 0)
        def _():
            ds_ref[...] += part

    row = pl.BlockSpec((ts, D), lambda i: (i, 0))
    vec = pl.BlockSpec((1, D), lambda i: (0, 0))
    return pl.pallas_call(
        body, grid=(S // ts,), name="pool_bwd_mix",
        in_specs=[row, row, pl.BlockSpec((G, P, P), lambda i: (0, 0, 0)), vec],
        out_specs=[row, row, vec],
        out_shape=[jax.ShapeDtypeStruct((S, D), BF16), jax.ShapeDtypeStruct((S, D), F32),
                   jax.ShapeDtypeStruct((1, D), F32)],
        compiler_params=_params(("arbitrary",)),
    )(dx, pooled, w, scale)


def pool_backward_window(dp, x, r, gain, dres, ts=POOL_TILE):
    S, D = dp.shape
    G = len(POOL_WINDOWS)
    P = D // G
    hb = ts // POOL_HALO
    n_i = S // ts
    n_rows = ts + POOL_HALO

    def body(dp_ref, halo_ref, x_ref, r_ref, g_ref, dres_ref, dx_ref, dxb_ref, dg_ref, dh_ref):
        i = pl.program_id(0)
        for gi, win in enumerate(POOL_WINDOWS):
            cols = slice(gi * P, (gi + 1) * P)
            cur = dp_ref[:, cols]
            halo = jnp.where(i < n_i - 1, halo_ref[:, cols], 0.0)
            acc = jnp.concatenate([cur / _pool_counts(i * ts, ts, win),
                                   halo / _pool_counts((i + 1) * ts, POOL_HALO, win)], axis=0)
            step = 1
            while step < win:
                acc = acc + pltpu.roll(acc, n_rows - step, 0)
                step *= 2
            dh_ref[:, cols] = acc[:ts, :] - cur
        _rms_bwd_tile(i == 0, lambda sl: dh_ref[sl, :], x_ref, r_ref, g_ref, dres_ref, dx_ref, dxb_ref, dg_ref, ts)

    row = pl.BlockSpec((ts, D), lambda i: (i, 0))
    vec = pl.BlockSpec((1, D), lambda i: (0, 0))
    return pl.pallas_call(
        body, grid=(n_i,), name="pool_bwd_window",
        in_specs=[row, pl.BlockSpec((POOL_HALO, D), lambda i: (jnp.minimum((i + 1) * hb, S // POOL_HALO - 1), 0)),
                  row, pl.BlockSpec((ts, 1), lambda i: (i, 0)), vec, row],
        out_specs=[row, row, vec],
        out_shape=[jax.ShapeDtypeStruct((S, D), F32), jax.ShapeDtypeStruct((S, D), BF16),
                   jax.ShapeDtypeStruct((1, D), F32)],
        scratch_shapes=[pltpu.VMEM((ts, D), F32)],
        compiler_params=_params(("arbitrary",)),
    )(dp, dp, x, r, gain, dres)


_HG_LEVELS = (32, 16, 8, 4, 2, 1)
_N_LEV = len(_HG_LEVELS) + 1


def _hgrn_constants():
    C = HG_CHUNK
    t = np.arange(C)
    tri = (t[None, :] <= t[:, None]).astype(np.float32)
    blocks = [tri]
    masks, upq, upk = [], [], []
    for m in _HG_LEVELS:
        p = (t // (2 * m)) * 2 * m + m - 1
        blocks.append(tri[p])
        masks.append(((t[:, None] // (2 * m)) == (t[None, :] // (2 * m))).astype(np.float32))
        upper = (t % (2 * m)) >= m
        upq.append(np.repeat(upper[:, None], HEAD, 1).astype(np.float32))
        upk.append(np.repeat(~upper[:, None], HEAD, 1).astype(np.float32))
    blocks.append(tri)
    masks.append(np.eye(C, dtype=np.float32))
    upq.append(np.ones((C, HEAD), np.float32))
    upk.append(np.ones((C, HEAD), np.float32))
    mstack = np.concatenate(blocks, axis=0)
    mstack3 = np.concatenate([mstack] * 3, axis=1)
    trirev3 = np.concatenate([tri.T] * 3, axis=1)
    return (jnp.asarray(mstack3, BF16), jnp.asarray(np.stack(masks)), jnp.asarray(np.stack(upq)),
            jnp.asarray(np.stack(upk)), jnp.asarray(trirev3, BF16))


def _split3(x):
    hi = x.astype(BF16)
    r1 = x - hi.astype(F32)
    mid = r1.astype(BF16)
    lo = (r1 - mid.astype(F32)).astype(BF16)
    return jnp.concatenate([hi, mid, lo], axis=0)


def _hgrn_chunk_common(qa, fa, lbv, mstack3, upq, upk):
    sq = _sigmoid(qa)
    q = qa * sq
    sf = _sigmoid(fa)
    f = lbv + (1.0 - lbv) * sf
    g = jnp.log(f)
    k = 1.0 - f
    gall = jnp.dot(mstack3, _split3(g), preferred_element_type=F32).reshape(_N_LEV + 1, HG_CHUNK, HEAD)
    G = gall[0]
    eq_exp = G[None] - gall[1:]
    eq = jnp.exp(jnp.minimum(eq_exp, 0.0)) * upq
    ek = jnp.exp(jnp.minimum(-eq_exp, 0.0)) * upk
    Qs = (q[None] * eq).astype(BF16)
    Ks = (k[None] * ek).astype(BF16)
    return sq, q, sf, f, k, G, eq, ek, Qs, Ks


def hgrn_forward(proj, lb, hg_norm, ts=HG_TILE):
    S = proj.shape[0]
    nh = lb.shape[1] // HEAD
    C = HG_CHUNK
    ncs = ts // C
    mstack3, masks, upq, upk, _ = _hgrn_constants()

    def body(qa_ref, fa_ref, ia_ref, ga_ref, lb_ref, gn_ref, ms_ref, mk_ref, uq_ref, uk_ref,
             oa_ref, oraw_ref, st_ref, state):
        tt = pl.program_id(1)

        @pl.when(tt == 0)
        def _():
            state[...] = jnp.zeros_like(state)

        gn = gn_ref[...]

        def chunk(c, carry):
            sl = pl.ds(pl.multiple_of(c * C, C), C)
            for hh in range(HG_HEADS_PER_BLOCK):
                cols = slice(hh * HEAD, (hh + 1) * HEAD)
                qa, fa, v, ga = qa_ref[sl, cols], fa_ref[sl, cols], ia_ref[sl, cols], ga_ref[sl, cols]
                _, q, _, _, k, G, _, _, Qs, Ks = _hgrn_chunk_common(qa, fa, lb_ref[:, cols], ms_ref[...],
                                                                    uq_ref[...], uk_ref[...])
                att7 = lax.dot_general(Qs, Ks, (((2,), (2,)), ((0,), (0,))), preferred_element_type=F32)
                att = jnp.sum(att7 * mk_ref[...], axis=0)
                st = state[hh]
                st_ref[hh, c] = st
                vb = v.astype(BF16)
                qg = (q * jnp.exp(G)).astype(BF16)
                o = jnp.dot(att.astype(BF16), vb, preferred_element_type=F32)
                o = o + lax.dot_general(qg, st.astype(BF16), (((1,), (1,)), ((), ())),
                                        preferred_element_type=F32)
                g_last = G[C - 1:C, :]
                kh = (k * jnp.exp(g_last - G)).astype(BF16)
                state[hh] = st * jnp.exp(g_last) + lax.dot_general(vb, kh, (((0,), (0,)), ((), ())),
                                                                   preferred_element_type=F32)
                oraw_ref[sl, cols] = o
                r = lax.rsqrt(jnp.mean(o * o, axis=-1, keepdims=True) + RMS_EPS)
                oa_ref[sl, cols] = (((o * r) * gn) * (ga * _sigmoid(ga))).astype(BF16)
            return carry

        lax.fori_loop(0, ncs, chunk, 0)

    hpb = HG_HEADS_PER_BLOCK
    wide = hpb * HEAD

    def col(m0):
        return pl.BlockSpec((ts, wide), lambda h, t: (t, m0 // hpb + h))

    const3 = lambda shape: pl.BlockSpec(shape, lambda h, t: (0, 0, 0))
    return pl.pallas_call(
        body, grid=(nh // hpb, S // ts), name="hgrn_fwd",
        in_specs=[col(0), col(nh), col(2 * nh), col(3 * nh),
                  pl.BlockSpec((1, wide), lambda h, t: (0, h)), pl.BlockSpec((1, HEAD), lambda h, t: (0, 0)),
                  pl.BlockSpec(mstack3.shape, lambda h, t: (0, 0)), const3(masks.shape), const3(upq.shape),
                  const3(upk.shape)],
        out_specs=[pl.BlockSpec((ts, wide), lambda h, t: (t, h)), pl.BlockSpec((ts, wide), lambda h, t: (t, h)),
                   pl.BlockSpec((hpb, ncs, HEAD, HEAD), lambda h, t: (h, t, 0, 0))],
        out_shape=[jax.ShapeDtypeStruct((S, nh * HEAD), BF16), jax.ShapeDtypeStruct((S, nh * HEAD), F32),
                   jax.ShapeDtypeStruct((nh, S // C, HEAD, HEAD), F32)],
        scratch_shapes=[pltpu.VMEM((hpb, HEAD, HEAD), F32)],
        compiler_params=_params(("arbitrary", "arbitrary")),
    )(proj, proj, proj, proj, lb, hg_norm, mstack3, masks, upq, upk)


def hgrn_backward(dcat, proj, oraw, states, lb, hg_norm, ts=HG_TILE):
    S = proj.shape[0]
    nh = lb.shape[1] // HEAD
    C = HG_CHUNK
    ncs = ts // C
    nt = S // ts
    mstack3, masks, upq, upk, trirev3 = _hgrn_constants()

    def body(do_ref, qa_ref, fa_ref, ia_ref, ga_ref, or_ref, st_ref, lb_ref, gn_ref, ms_ref, mk_ref, uq_ref,
             uk_ref, tr_ref, dqa_ref, dfa_ref, dia_ref, dga_ref, dlb_ref, dgn_ref, dstate):
        tt = pl.program_id(1)

        @pl.when(tt == 0)
        def _():
            dstate[...] = jnp.zeros_like(dstate)
            dlb_ref[...] = jnp.zeros_like(dlb_ref)
            dgn_ref[...] = jnp.zeros_like(dgn_ref)

        gn = gn_ref[...]

        def chunk(cc, carry):
            c = ncs - 1 - cc
            sl = pl.ds(pl.multiple_of(c * C, C), C)
            for hh in range(HG_HEADS_PER_BLOCK):
                cols = slice(hh * HEAD, (hh + 1) * HEAD)
                lbv = lb_ref[:, cols]
                qa, fa, v, ga = qa_ref[sl, cols], fa_ref[sl, cols], ia_ref[sl, cols], ga_ref[sl, cols]
                sq, q, sf, f, k, G, eq, ek, Qs, Ks = _hgrn_chunk_common(qa, fa, lbv, ms_ref[...], uq_ref[...],
                                                                        uk_ref[...])
                mk = mk_ref[...]
                att7 = lax.dot_general(Qs, Ks, (((2,), (2,)), ((0,), (0,))), preferred_element_type=F32)
                att = jnp.sum(att7 * mk, axis=0)
                o = or_ref[sl, cols]
                dO = do_ref[sl, cols]
                sg = _sigmoid(ga)
                r = lax.rsqrt(jnp.mean(o * o, axis=-1, keepdims=True) + RMS_EPS)
                xh = o * r
                dga_ref[sl, cols] = (dO * (xh * gn) * (sg * (1.0 + ga * (1.0 - sg)))).astype(BF16)
                don = dO * (ga * sg)
                dgn_ref[hh] += jnp.sum(don * xh, axis=0, keepdims=True)
                dxh = don * gn
                do = r * (dxh - xh * jnp.mean(dxh * xh, axis=-1, keepdims=True))
                dob = do.astype(BF16)
                st = st_ref[hh, c]
                dst = dstate[hh]
                dstb = dst.astype(BF16)
                vb = v.astype(BF16)
                eG = jnp.exp(G)
                g_last = G[C - 1:C, :]
                e_last = jnp.exp(g_last)
                e_tail = jnp.exp(g_last - G)
                qg = (q * eG).astype(BF16)
                kh = (k * e_tail).astype(BF16)
                dq_inter = jnp.dot(dob, st.astype(BF16), preferred_element_type=F32) * eG
                dk_inter = jnp.dot(vb, dstb, preferred_element_type=F32) * e_tail
                dv = lax.dot_general(kh, dstb, (((1,), (1,)), ((), ())), preferred_element_type=F32)
                dv = dv + lax.dot_general(att.astype(BF16), dob, (((0,), (0,)), ((), ())),
                                          preferred_element_type=F32)
                dA = lax.dot_general(dob, vb, (((1,), (1,)), ((), ())), preferred_element_type=F32)
                dA7 = (dA[None] * mk).astype(BF16)
                dAT7 = (dA.T[None] * mk).astype(BF16)
                dQs = lax.dot_general(dA7, Ks, (((2,), (1,)), ((0,), (0,))), preferred_element_type=F32)
                dKs = lax.dot_general(dAT7, Qs, (((2,), (1,)), ((0,), (0,))), preferred_element_type=F32)
                dq = dq_inter + jnp.sum(dQs * eq, axis=0)
                dk = dk_inter + jnp.sum(dKs * ek, axis=0)
                dG = (jnp.sum(Qs.astype(F32) * dQs - Ks.astype(F32) * dKs, axis=0)
                      + q * dq_inter - k * dk_inter)
                last_extra = (jnp.sum(k * dk_inter, axis=0, keepdims=True)
                              + e_last * jnp.sum(dst * st, axis=0, keepdims=True))
                is_last = lax.broadcasted_iota(jnp.int32, (C, 1), 0) == C - 1
                dG = dG + jnp.where(is_last, last_extra, 0.0)
                dg = jnp.dot(tr_ref[...], _split3(dG), preferred_element_type=F32)
                df = dg / f - dk
                dfa_ref[sl, cols] = (df * (1.0 - lbv) * (sf * (1.0 - sf))).astype(BF16)
                dlb_ref[:, cols] += jnp.sum(df * (1.0 - sf), axis=0, keepdims=True)
                dqa_ref[sl, cols] = (dq * (sq * (1.0 + qa * (1.0 - sq)))).astype(BF16)
                dia_ref[sl, cols] = dv.astype(BF16)
                dstate[hh] = dst * e_last + lax.dot_general(dob, qg, (((0,), (0,)), ((), ())),
                                                            preferred_element_type=F32)
            return carry

        lax.fori_loop(0, ncs, chunk, 0)

    hpb = HG_HEADS_PER_BLOCK
    wide = hpb * HEAD

    def col(m0):
        return pl.BlockSpec((ts, wide), lambda h, t: (nt - 1 - t, m0 // hpb + h))

    const3 = lambda shape: pl.BlockSpec(shape, lambda h, t: (0, 0, 0))
    const2 = lambda shape: pl.BlockSpec(shape, lambda h, t: (0, 0))
    ocol = pl.BlockSpec((ts, wide), lambda h, t: (nt - 1 - t, h))
    half = nh * HEAD
    return pl.pallas_call(
        body, grid=(nh // hpb, nt), name="hgrn_bwd",
        in_specs=[col(0), col(0), col(nh), col(2 * nh), col(3 * nh), col(0),
                  pl.BlockSpec((hpb, ncs, HEAD, HEAD), lambda h, t: (h, nt - 1 - t, 0, 0)),
                  pl.BlockSpec((1, wide), lambda h, t: (0, h)), const2((1, HEAD)),
                  const2(mstack3.shape), const3(masks.shape), const3(upq.shape), const3(upk.shape),
                  const2(trirev3.shape)],
        out_specs=[ocol, ocol, ocol, ocol, pl.BlockSpec((1, wide), lambda h, t: (0, h)),
                   pl.BlockSpec((hpb, 1, HEAD), lambda h, t: (h, 0, 0))],
        out_shape=[jax.ShapeDtypeStruct((S, half), BF16)] * 4
                  + [jax.ShapeDtypeStruct((1, half), F32), jax.ShapeDtypeStruct((nh, 1, HEAD), F32)],
        scratch_shapes=[pltpu.VMEM((hpb, HEAD, HEAD), F32)],
        compiler_params=_params(("arbitrary", "arbitrary")),
    )(dcat, proj, proj, proj, proj, oraw, states, lb, hg_norm, mstack3, masks, upq, upk, trirev3)


SB_SUB = 128
LOG2_E = 1.4426950408889634
SB_SCALE = 1.0 / math.sqrt(HEAD)
SB_QUERY_SCALE = SB_SCALE * LOG2_E


def _split2(x):
    hi = x.astype(BF16)
    lo = (x - hi.astype(F32)).astype(BF16)
    return jnp.concatenate([hi, lo], axis=1)


def _sb_constants():
    j = np.arange(SB_SUB)
    after = (j[:, None] > j[None, :]).astype(np.float32)
    before = (j[:, None] < j[None, :]).astype(np.float32)
    return (jnp.asarray(np.concatenate([after, after], axis=0), BF16),
            jnp.asarray(np.concatenate([before, before], axis=0), BF16))


def _sb_tri(i):
    return (i * (i + 1)) // 2


def _sb_scores(q, k_ref, col0, t):
    ks = k_ref[pl.ds(pl.multiple_of(col0, t), t), :]
    return lax.dot_general(q, ks, (((1,), (1,)), ((), ())), preferred_element_type=F32)


def _sb_weights(z, diagonal, run, after2):
    t = z.shape[0]
    nsub = z.shape[1] // SB_SUB
    lks, locs, tots = [], [], []
    for b in range(nsub):
        r0 = b * SB_SUB if diagonal else 0
        zb = z[r0:, b * SB_SUB:(b + 1) * SB_SUB]
        nzb = -zb
        lkb = jnp.minimum(nzb, 0.0) - jnp.log(1.0 + jnp.exp2(jnp.minimum(zb, nzb))) * LOG2_E
        if diagonal:
            rows = lax.broadcasted_iota(jnp.int32, zb.shape, 0)
            visible = lax.broadcasted_iota(jnp.int32, zb.shape, 1) < rows
            lkb = jnp.where(visible, lkb, 0.0)
        loc = jnp.dot(_split2(lkb), after2, preferred_element_type=F32)
        lks.append(lkb)
        locs.append(loc)
        tots.append(loc[:, 0:1] + lkb[:, 0:1])
    ws = [None] * nsub
    for b in reversed(range(nsub)):
        r0 = b * SB_SUB if diagonal else 0
        zb = z[r0:, b * SB_SUB:(b + 1) * SB_SUB]
        wb = jnp.exp2(zb + lks[b] + (locs[b] + run[r0:]))
        tot = tots[b]
        if diagonal:
            rows = lax.broadcasted_iota(jnp.int32, zb.shape, 0)
            wb = jnp.where(lax.broadcasted_iota(jnp.int32, zb.shape, 1) < rows, wb, 0.0)
            if r0:
                wb = jnp.concatenate([jnp.zeros((r0, SB_SUB), F32), wb], axis=0)
                tot = jnp.concatenate([jnp.zeros((r0, 1), F32), tot], axis=0)
        ws[b] = wb
        run = run + tot
    return jnp.concatenate(ws, axis=1), run


def sb_forward(projb, nh, m0, t=SB_TILE):
    S = projb.shape[0]
    after2, _ = _sb_constants()
    n_i = S // t

    def body(q_ref, k_ref, v_ref, af_ref, o_ref, w_hbm, wbuf, wsem):
        h = pl.program_id(0)
        i = pl.program_id(1)
        q = q_ref[...]
        after = af_ref[...]
        base = _sb_tri(i)

        def store(slot, jb):
            return pltpu.make_async_copy(wbuf.at[slot], w_hbm.at[h, base + jb], wsem.at[slot])

        def block(n, jb, run, diagonal):
            slot = n % 2

            @pl.when(n >= 2)
            def _():
                store(slot, jb).wait()

            z = _sb_scores(q, k_ref, jb * t, t)
            w, run = _sb_weights(z, diagonal, run, after)
            wb = w.astype(BF16)
            wbuf[slot] = wb
            store(slot, jb).start()
            vs = v_ref[pl.ds(pl.multiple_of(jb * t, t), t), :]
            return run, jnp.dot(wb, vs, preferred_element_type=F32)

        run, acc = block(0, i, jnp.zeros((t, 1), F32), True)

        def step(n, carry):
            run, acc = carry
            run, part = block(n + 1, i - 1 - n, run, False)
            return run, acc + part

        _, acc = lax.fori_loop(0, i, step, (run, acc))
        o_ref[...] = acc.astype(BF16)
        store(i % 2, 0).wait()

        @pl.when(i >= 1)
        def _():
            store((i + 1) % 2, 0).wait()

    return pl.pallas_call(
        body, grid=(nh, n_i), name="sb_fwd",
        in_specs=[pl.BlockSpec((t, HEAD), lambda h, i: (i, m0 + h)),
                  pl.BlockSpec((S, HEAD), lambda h, i: (0, m0 + nh + h)),
                  pl.BlockSpec((S, HEAD), lambda h, i: (0, m0 + 2 * nh + h)),
                  pl.BlockSpec(after2.shape, lambda h, i: (0, 0))],
        out_specs=[pl.BlockSpec((t, HEAD), lambda h, i: (i, h)), pl.BlockSpec(memory_space=pl.ANY)],
        out_shape=[jax.ShapeDtypeStruct((S, nh * HEAD), BF16),
                   jax.ShapeDtypeStruct((nh, _sb_tri(n_i), t, t), BF16)],
        scratch_shapes=[pltpu.VMEM((2, t, t), BF16), pltpu.SemaphoreType.DMA((2,))],
        compiler_params=_params(("arbitrary", "arbitrary")),
    )(projb, projb, projb, after2)


def sb_backward(dcat, projb, w_all, nh, m0, t=SB_TILE):
    S = projb.shape[0]
    _, before2 = _sb_constants()
    n_i = S // t
    nsub = t // SB_SUB

    def body(do_ref, q_ref, k_ref, v_ref, bf_ref, w_hbm, dq_ref, dk_ref, dv_ref, dk_acc, dv_acc, wbuf, wsem):
        h = pl.program_id(0)
        i = pl.program_id(1)

        @pl.when(i == 0)
        def _():
            dk_acc[...] = jnp.zeros_like(dk_acc)
            dv_acc[...] = jnp.zeros_like(dv_acc)

        q = q_ref[...]
        dob = do_ref[...].astype(BF16)
        before = bf_ref[...]
        base = _sb_tri(i)

        def load(slot, jb):
            return pltpu.make_async_copy(w_hbm.at[h, base + jb], wbuf.at[slot], wsem.at[slot])

        load(0, 0).start()

        def left_to_right(jb, run, dq, diagonal):
            slot = jb % 2
            load(slot, jb).wait()

            @pl.when(jb < i)
            def _():
                load(1 - slot, jb + 1).start()

            ksl = pl.ds(pl.multiple_of(jb * t, t), t)
            wb = wbuf[slot]
            z = _sb_scores(q, k_ref, jb * t, t)
            dw = lax.dot_general(dob, v_ref[ksl, :], (((1,), (1,)), ((), ())), preferred_element_type=F32)
            d = dw * wb.astype(F32)
            dv_acc[ksl, :] += lax.dot_general(wb, dob, (((0,), (0,)), ((), ())), preferred_element_type=F32)
            das = []
            for b in range(nsub):
                r0 = b * SB_SUB if diagonal else 0
                cols = slice(b * SB_SUB, (b + 1) * SB_SUB)
                db = d[r0:, cols]
                prefix = run[r0:] + jnp.dot(_split2(db), before, preferred_element_type=F32)
                sig = 1.0 / (1.0 + jnp.exp2(-z[r0:, cols]))
                dab_ = db - sig * (db + prefix)
                new_run = prefix[:, SB_SUB - 1:SB_SUB] + db[:, SB_SUB - 1:SB_SUB]
                if diagonal:
                    rows = lax.broadcasted_iota(jnp.int32, db.shape, 0)
                    dab_ = jnp.where(lax.broadcasted_iota(jnp.int32, db.shape, 1) < rows, dab_, 0.0)
                    if r0:
                        dab_ = jnp.concatenate([jnp.zeros((r0, SB_SUB), F32), dab_], axis=0)
                        new_run = jnp.concatenate([run[:r0], new_run], axis=0)
                das.append(dab_)
                run = new_run
            da = jnp.concatenate(das, axis=1)
            dab = (da * SB_SCALE).astype(BF16)
            dq = dq + jnp.dot(dab, k_ref[ksl, :], preferred_element_type=F32)
            dk_acc[ksl, :] += lax.dot_general(dab, q, (((0,), (0,)), ((), ())), preferred_element_type=F32)
            return run, dq

        run, dq = lax.fori_loop(0, i, lambda jb, c: left_to_right(jb, c[0], c[1], False),
                                (jnp.zeros((t, 1), F32), jnp.zeros((t, HEAD), F32)))
        _, dq = left_to_right(i, run, dq, True)
        dq_ref[...] = dq.astype(BF16)

        @pl.when(i == n_i - 1)
        def _():
            dk_ref[...] = (dk_acc[...] * (1.0 / SB_QUERY_SCALE)).astype(BF16)
            dv_ref[...] = dv_acc[...].astype(BF16)

    half = nh * HEAD
    full = pl.BlockSpec((S, HEAD), lambda h, i: (0, h))
    return pl.pallas_call(
        body, grid=(nh, n_i), name="sb_bwd",
        in_specs=[pl.BlockSpec((t, HEAD), lambda h, i: (i, nh + h)),
                  pl.BlockSpec((t, HEAD), lambda h, i: (i, m0 + h)),
                  pl.BlockSpec((S, HEAD), lambda h, i: (0, m0 + nh + h)),
                  pl.BlockSpec((S, HEAD), lambda h, i: (0, m0 + 2 * nh + h)),
                  pl.BlockSpec(before2.shape, lambda h, i: (0, 0)), pl.BlockSpec(memory_space=pl.ANY)],
        out_specs=[pl.BlockSpec((t, HEAD), lambda h, i: (i, h)), full, full],
        out_shape=[jax.ShapeDtypeStruct((S, half), BF16)] * 3,
        scratch_shapes=[pltpu.VMEM((S, HEAD), F32), pltpu.VMEM((S, HEAD), F32),
                        pltpu.VMEM((2, t, t), BF16), pltpu.SemaphoreType.DMA((2,))],
        compiler_params=_params(("arbitrary", "arbitrary")),
    )(dcat, projb, projb, projb, before2, w_all)


def local_step(x, target, mix_norm, ffn_norm, final_norm, lb_logits, hg_norm, get_w_in, get_w_rest, send):
    S, D = x.shape
    half = D // 2
    nh = half // HEAD
    tm = ROW_TILE
    tk = REDUCE_TILE
    row = lambda i, j: (i, 0)

    lb = jax.nn.softmax(lb_logits, axis=0)[0:1]

    h0, r0 = rms_fwd(x, mix_norm[0:1], BF16)
    w_in = get_w_in(h0)
    nbi = w_in.shape[2]
    col = jnp.arange(N_DEV * nbi) // half
    col_scale = jnp.where(col == 4, SB_QUERY_SCALE, 1.0).astype(F32)[None]
    proj, projb = matmul(
        "proj_in", [h0], [w_in], grid=(N_DEV, S // ROW_TILE_WIDE, 1),
        a_spec=pl.BlockSpec((ROW_TILE_WIDE, D), lambda j, i, k: (i, 0)),
        b_spec=pl.BlockSpec((None, D, nbi), lambda j, i, k: (j, 0, 0)),
        out_spec=pl.BlockSpec((ROW_TILE_WIDE, nbi), lambda j, i, k: (i, j)), out_shape=(S, N_DEV * nbi),
        out_dtypes=[F32, BF16], acc_shape=(8, 128),
        bf16_scale=col_scale, bf16_scale_spec=pl.BlockSpec((1, nbi), lambda j, i, k: (0, j)))
    oa, oraw, states = hgrn_forward(proj, lb, hg_norm)
    ob, sb_weights = sb_forward(projb, nh, 4 * nh)
    cat = jnp.concatenate([oa, ob], axis=1)
    w_out, pool_w, pool_scale, wg, wu, wd = get_w_rest(cat)
    x1, h1, r1 = matmul_residual_rms("mix_out", cat, w_out, x, ffn_norm[0:1])
    x2, ffn0 = ffn_forward(h1, x1, wg[0], wu[0], wd[0])

    h2, r2 = rms_fwd(x2, mix_norm[1:2], F32)
    x3, pooled, h3, r3 = pool_forward(h2, x2, pool_w, pool_scale, ffn_norm[1:2])
    x4, ffn1 = ffn_forward(h3, x3, wg[1], wu[1], wd[1])

    loss_blk, dx4, dx4b, d_final = loss_and_final_bwd(x4, final_norm, target)

    (dx3, _, d_ffn1), dwg1, dwu1, dwd1 = ffn_backward(dx4b, h3, ffn1, wg[1], wu[1], wd[1],
                                                      x3, r3, ffn_norm[1:2], dx4)
    dx3 = send("ffn1", dict(ffn_w_gate_1=dwg1, ffn_w_up_1=dwu1, ffn_w_down_1=dwd1), dx3)
    dmixed, dpooled, d_pscale = pool_backward_mix(dx3, pooled, pool_w, pool_scale)
    G = len(POOL_WINDOWS)
    P = D // G
    (d_pool_w,) = matmul(
        "pool_dw", [pooled], [dmixed], grid=(G, S // tk),
        a_spec=pl.BlockSpec((tk, P), lambda g, k: (k, g)), b_spec=pl.BlockSpec((tk, P), lambda g, k: (k, g)),
        out_spec=pl.BlockSpec((None, P, P), lambda g, k: (g, 0, 0)), out_shape=(G, P, P), out_dtypes=[BF16],
        acc_shape=(P, P), trans_a=True)
    dx2, dx2b, d_mix1 = pool_backward_window(dpooled, x2, r2, mix_norm[1:2], dx3)

    (dx1, dx1b, d_ffn0), dwg0, dwu0, dwd0 = ffn_backward(dx2b, h1, ffn0, wg[0], wu[0], wd[0],
                                                         x1, r1, ffn_norm[0:1], dx2)
    (dcat,) = matmul(
        "mix_out_dx", [dx1b], [w_out], grid=(S // tm, 1),
        a_spec=pl.BlockSpec((tm, D), row), b_spec=pl.BlockSpec((D, D), lambda i, k: (0, 0)),
        out_spec=pl.BlockSpec((tm, D), row), out_shape=(S, D), out_dtypes=[F32], acc_shape=(8, 128),
        trans_b=True)
    (d_w_out,) = matmul(
        "mix_out_dw", [cat], [dx1b], grid=(2, S // tk),
        a_spec=pl.BlockSpec((tk, half), lambda m, k: (k, m)), b_spec=pl.BlockSpec((tk, D), lambda m, k: (k, 0)),
        out_spec=pl.BlockSpec((half, D), lambda m, k: (m, 0)), out_shape=(D, D), out_dtypes=[BF16],
        acc_shape=(half, D), trans_a=True)
    dcat = send("layer0", dict(ffn_w_gate_0=dwg0, ffn_w_up_0=dwu0, ffn_w_down_0=dwd0, pool_w=d_pool_w,
                               ab_w_out=d_w_out), dcat)
    dqa, dfa, dia, dga, d_lb, d_hg = hgrn_backward(dcat, proj, oraw, states, lb, hg_norm)
    dqb, dkb, dvb = sb_backward(dcat, projb, sb_weights, nh, 4 * nh)
    dproj = jnp.concatenate([dqa, dfa, dia, dga, dqb, dkb, dvb], axis=1)
    (d_w_in,) = matmul(
        "proj_in_dw", [h0], [dproj], grid=(N_DEV, S // tk),
        a_spec=pl.BlockSpec((tk, D), lambda j, k: (k, 0)), b_spec=pl.BlockSpec((tk, nbi), lambda j, k: (k, j)),
        out_spec=pl.BlockSpec((None, D, nbi), lambda j, k: (j, 0, 0)), out_shape=(N_DEV, D, nbi),
        out_dtypes=[BF16], acc_shape=(D, nbi), trans_a=True)
    dproj = send("w_in", dict(ab_w_in=d_w_in), dproj)
    dx0, _, d_mix0 = matmul_rms_bwd(
        "proj_in_dx", [dproj], [w_in], grid=(S // tm, N_DEV),
        a_spec=pl.BlockSpec((tm, nbi), lambda i, j: (i, j)),
        b_spec=pl.BlockSpec((None, D, nbi), lambda i, j: (j, 0, 0)),
        tm=tm, x=x, r=r0, gain=mix_norm[0:1], dres=dx1)

    d_l0 = d_lb * lb * (1.0 - lb)
    small = dict(
        loss=loss_blk[0:1, 0:1],
        mix_norm=jnp.concatenate([d_mix0, d_mix1], axis=0),
        ffn_norm=jnp.concatenate([d_ffn0, d_ffn1], axis=0),
        final_norm=d_final,
        lb_logits=jnp.concatenate([d_l0, -d_l0], axis=0),
        hg_out_norm=jnp.sum(d_hg, axis=0),
        pool_scale=d_pscale,
    )
    return dx0, small


def _my_index():
    return 4 * lax.axis_index("x") + 2 * lax.axis_index("y") + lax.axis_index("c")


def _peer(r):
    x, y, c = lax.axis_index("x"), lax.axis_index("y"), lax.axis_index("c")
    px = 1 - x if (r >> 2) & 1 else x
    py = 1 - y if (r >> 1) & 1 else y
    pc = 1 - c if r & 1 else c
    return (px, py, pc), 4 * px + 2 * py + pc


def gather_two_level(name, shard):
    def body(x_ref, out_ref, send_sems, recv_sems, local_sem):
        x, y, c = lax.axis_index("x"), lax.axis_index("y"), lax.axis_index("c")
        me, sibling = (x, y, c), (x, y, 1 - c)
        chips = [(1 - x, y), (x, 1 - y), (1 - x, 1 - y)]

        def slot(px, py, pc):
            return out_ref.at[4 * px + 2 * py + pc]

        def copy(k, block, to, src=None):
            return pltpu.make_async_remote_copy(
                src_ref=slot(*block) if src is None else src, dst_ref=slot(*block), send_sem=send_sems.at[k],
                recv_sem=recv_sems.at[k], device_id=to, device_id_type=MESH)

        mine = pltpu.make_async_copy(x_ref, slot(*me), local_sem)
        mine.start()
        first = [copy(0, me, sibling, src=x_ref)]
        first += [copy(1 + j, me, (*chip, c), src=x_ref) for j, chip in enumerate(chips)]
        for cp in first:
            cp.start()
        passed = [copy(4 + j, (*chip, c), sibling) for j, chip in enumerate(chips)]
        for j, chip in enumerate(chips):
            copy(1 + j, (*chip, c), me).wait_recv()
            passed[j].start()
        copy(0, sibling, me).wait_recv()
        for j, chip in enumerate(chips):
            copy(4 + j, (*chip, 1 - c), me).wait_recv()
        for cp in first + passed:
            cp.wait_send()
        mine.wait()

    any_spec = pl.BlockSpec(memory_space=pl.ANY)
    return pl.pallas_call(
        body, name=name, in_specs=[any_spec], out_specs=any_spec,
        out_shape=jax.ShapeDtypeStruct((N_DEV,) + shard.shape, shard.dtype),
        scratch_shapes=[pltpu.SemaphoreType.DMA((N_DEV - 1,)), pltpu.SemaphoreType.DMA((N_DEV - 1,)),
                        pltpu.SemaphoreType.DMA],
    )(shard)


def exchange(name, arrays, gather):
    n = len(arrays)
    n_peers = N_DEV - 1

    def body(*refs):
        ins, outs = refs[:n], refs[n:2 * n]
        send_sems, recv_sems, local_sems = refs[2 * n:]
        me = _my_index()
        local = []
        for a in range(n):
            src = ins[a] if gather else ins[a].at[me]
            cp = pltpu.make_async_copy(src, outs[a].at[me], local_sems.at[a])
            cp.start()
            local.append(cp)
        remote = []
        for a in range(n):
            for r in range(1, N_DEV):
                peer, pidx = _peer(r)
                src = ins[a] if gather else ins[a].at[pidx]
                cp = pltpu.make_async_remote_copy(
                    src_ref=src, dst_ref=outs[a].at[me], send_sem=send_sems.at[a * n_peers + r - 1],
                    recv_sem=recv_sems.at[a * n_peers + r - 1], device_id=peer, device_id_type=MESH)
                cp.start()
                remote.append((cp, a, r))
        for cp, a, r in remote:
            _, pidx = _peer(r)
            src = ins[a] if gather else ins[a].at[pidx]
            pltpu.make_async_remote_copy(
                src_ref=src, dst_ref=outs[a].at[pidx], send_sem=send_sems.at[a * n_peers + r - 1],
                recv_sem=recv_sems.at[a * n_peers + r - 1], device_id=_peer(r)[0], device_id_type=MESH).wait_recv()
        for cp, a, r in remote:
            cp.wait_send()
        for cp in local:
            cp.wait()

    out_shape = [jax.ShapeDtypeStruct(((N_DEV,) + a.shape) if gather else a.shape, a.dtype) for a in arrays]
    any_spec = pl.BlockSpec(memory_space=pl.ANY)
    return pl.pallas_call(
        body, name=name, in_specs=[any_spec] * n, out_specs=[any_spec] * n, out_shape=out_shape,
        scratch_shapes=[pltpu.SemaphoreType.DMA((n * n_peers,)), pltpu.SemaphoreType.DMA((n * n_peers,)),
                        pltpu.SemaphoreType.DMA((n,))],
    )(*arrays)


_HBM = pl.BlockSpec(memory_space=pltpu.HBM)
_SEM = pl.BlockSpec(memory_space=pltpu.SEMAPHORE)
_EFFECT = pltpu.SideEffectType.DATAFLOW_SIDE_EFFECTING


def _landing(arrays, gather):
    me = _my_index()
    lands = []
    for a in arrays:
        own = a[None] if gather else lax.dynamic_slice_in_dim(a, me, 1, axis=0)
        shape = ((N_DEV,) + a.shape) if gather else a.shape
        lands.append(lax.dynamic_update_slice_in_dim(lax.empty(shape, a.dtype), own, me, axis=0))
    return lands


def exchange_start(name, arrays, gather, carry):
    n = len(arrays)
    n_peers = N_DEV - 1
    lands = _landing(arrays, gather)
    n_thru = 2 * n + 1

    def body(*refs):
        src, land = refs[:n], refs[n:2 * n]
        send_sems, recv_sems = refs[n_thru], refs[n_thru + 1]
        token = refs[-1]
        me = _my_index()
        for a in range(n):
            for r in range(1, N_DEV):
                peer, pidx = _peer(r)
                pltpu.make_async_remote_copy(
                    src_ref=src[a] if gather else src[a].at[pidx], dst_ref=land[a].at[me],
                    send_sem=send_sems.at[a * n_peers + r - 1], recv_sem=recv_sems.at[a * n_peers + r - 1],
                    device_id=peer, device_id_type=MESH).start()
        token[...] = jnp.zeros_like(token)

    operands = list(arrays) + lands + [carry]
    outs = pl.pallas_call(
        body, name=name,
        out_shape=(pltpu.SemaphoreType.DMA((n * n_peers,)), pltpu.SemaphoreType.DMA((n * n_peers,)),
                   *[pltpu.HBM(a.shape, a.dtype) for a in operands], jax.ShapeDtypeStruct((8, 128), F32)),
        in_specs=[_HBM] * n_thru,
        out_specs=(_SEM, _SEM, *([_HBM] * n_thru), pl.BlockSpec(memory_space=pltpu.VMEM)),
        input_output_aliases={i: 2 + i for i in range(n_thru)},
        compiler_params=pltpu.CompilerParams(has_side_effects=_EFFECT),
    )(*[pltpu.with_memory_space_constraint(a, pltpu.HBM) for a in operands])
    handle = (outs[0], outs[1], list(outs[2:2 + n]), list(outs[2 + n:2 + 2 * n]), gather)
    return handle, outs[2 + 2 * n]


def exchange_wait(name, handle, after):
    send_sems, recv_sems, srcs, lands, gather = handle
    n = len(srcs)
    n_peers = N_DEV - 1

    def body(*refs):
        src, land = refs[:n], refs[n:2 * n]
        send_s, recv_s = refs[2 * n], refs[2 * n + 1]
        for a in range(n):
            for r in range(1, N_DEV):
                peer, pidx = _peer(r)
                cp = pltpu.make_async_remote_copy(
                    src_ref=src[a] if gather else src[a].at[pidx], dst_ref=land[a].at[pidx],
                    send_sem=send_s.at[a * n_peers + r - 1], recv_sem=recv_s.at[a * n_peers + r - 1],
                    device_id=peer, device_id_type=MESH)
                cp.wait_send()
                cp.wait_recv()

    shapes = [pltpu.HBM(a.shape, a.dtype) for a in srcs] + [pltpu.HBM(l.shape, l.dtype) for l in lands]
    outs = pl.pallas_call(
        body, name=name, out_shape=tuple(shapes),
        in_specs=[_HBM] * (2 * n) + [_SEM, _SEM, pl.BlockSpec(memory_space=pl.ANY)],
        out_specs=tuple([_HBM] * (2 * n)),
        input_output_aliases={i: i for i in range(2 * n)},
        compiler_params=pltpu.CompilerParams(has_side_effects=_EFFECT),
    )(*srcs, *lands, send_sems, recv_sems, after)
    return list(outs[n:])


def _row_tile(rows, cap=256):
    best = None
    for t in range(16, min(rows, cap) + 1, 16):
        if rows % t == 0:
            best = t
    return best if best is not None else rows


def sum_slots(name, recv):
    n, R, C = recv.shape
    tr = _row_tile(R)

    def body(r_ref, o_ref):
        g = r_ref[0].astype(F32)
        for d in range(1, n):
            g = g + r_ref[d].astype(F32)
        o_ref[...] = g

    return pl.pallas_call(
        body, grid=(R // tr,), name=name,
        in_specs=[pl.BlockSpec((n, tr, C), lambda i: (0, i, 0))],
        out_specs=pl.BlockSpec((tr, C), lambda i: (i, 0)),
        out_shape=jax.ShapeDtypeStruct((R, C), F32),
        compiler_params=_params(("arbitrary",)),
    )(recv)


def adamw(name, recv, w, m, v, layer=None, prev=None):
    n, R, C = recv.shape
    tr = _row_tile(R)

    def body(r_ref, w_ref, m_ref, v_ref, *rest):
        g_ref, d_ref, nm_ref, nv_ref = rest[-4:]
        g = r_ref[0].astype(F32)
        for d in range(1, n):
            g = g + r_ref[d].astype(F32)
        mm = ADAM_B1 * m_ref[...] + (1.0 - ADAM_B1) * g
        vv = ADAM_B2 * v_ref[...] + (1.0 - ADAM_B2) * (g * g)
        m_hat = mm / (1.0 - ADAM_B1 ** ADAM_STEP)
        v_hat = vv / (1.0 - ADAM_B2 ** ADAM_STEP)
        g_ref[...] = g
        d_ref[...] = -ADAM_LR * (m_hat / (jnp.sqrt(v_hat) + ADAM_EPS) + ADAM_WD * w_ref[...])
        nm_ref[...] = mm
        nv_ref[...] = vv

    if layer is None:
        row = pl.BlockSpec((tr, C), lambda i: (i, 0))
        shape = (R, C)
    else:
        row = pl.BlockSpec((None, tr, C), lambda i: (layer, i, 0))
        shape = w.shape
    prev = [] if prev is None else list(prev)
    return pl.pallas_call(
        body, grid=(R // tr,), name=name,
        in_specs=[pl.BlockSpec((n, tr, C), lambda i: (0, i, 0)), row, row, row]
                 + [pl.BlockSpec(memory_space=pl.ANY)] * len(prev),
        out_specs=[row] * 4,
        out_shape=[jax.ShapeDtypeStruct(shape, F32)] * 4,
        input_output_aliases={4 + o: o for o in range(len(prev))},
        compiler_params=_params(("arbitrary",)),
    )(recv, w, m, v, *prev)


def _adamw_nd(name, recv, w, m, v):
    shp = w.shape
    C = shp[-1]
    flat = lambda a: a.reshape(-1, C)
    outs = adamw(name, recv.reshape(recv.shape[0], -1, C), flat(w), flat(m), flat(v))
    return [o.reshape(shp) for o in outs]


_SMALL_NAMES = ("loss", "mix_norm", "ffn_norm", "final_norm", "lb_logits", "hg_out_norm", "pool_scale")
_LANES = 128


def _pack_small(parts):
    rows, layout = [], {}
    at = 0
    for name in parts:
        flat = parts[name].reshape(-1).astype(F32)
        n_rows = -(-flat.shape[0] // (8 * _LANES)) * 8
        flat = jnp.pad(flat, (0, n_rows * _LANES - flat.shape[0]))
        rows.append(flat.reshape(n_rows, _LANES))
        layout[name] = (at, parts[name].shape)
        at += n_rows
    return jnp.concatenate(rows, axis=0), layout


def _unpack_small(pack, layout):
    out = {}
    for name, (at, shape) in layout.items():
        size = int(np.prod(shape))
        n_rows = -(-size // _LANES)
        out[name] = pack[at:at + n_rows].reshape(-1)[:size].reshape(shape)
    return out


def kernel(x, mix_norm, ffn_norm, final_norm, ab_w_in, lb_logits, hg_out_norm, ab_w_out, pool_w, pool_scale, ffn_w_gate, ffn_w_up, ffn_w_down, loss_target, m_mix_norm, m_ffn_norm, m_final_norm, m_ab_w_in, m_lb_logits, m_hg_out_norm, m_ab_w_out, m_pool_w, m_pool_scale, m_ffn_w_gate, m_ffn_w_up, m_ffn_w_down, v_mix_norm, v_ffn_norm, v_final_norm, v_ab_w_in, v_lb_logits, v_hg_out_norm, v_ab_w_out, v_pool_w, v_pool_scale, v_ffn_w_gate, v_ffn_w_up, v_ffn_w_down):
    D = x.shape[-1]
    n_layers = ffn_w_gate.shape[0]
    G = pool_w.shape[1]
    P = pool_w.shape[3]
    me = _my_index()

    rest = [ab_w_out[0], pool_w[0]]
    for l in range(n_layers):
        rest += [ffn_w_gate[l], ffn_w_up[l], ffn_w_down[l]]
    rest = [s.astype(BF16) for s in rest] + [pool_scale]
    rest_handle = []

    def get_w_in(after):
        w_in = gather_two_level("gather_w_in", ab_w_in[0].astype(BF16))
        handle, w_in = exchange_start("gather_rest_start", rest, True, w_in)
        rest_handle.append(handle)
        return w_in

    def get_w_rest(after):
        got = exchange_wait("gather_rest_wait", rest_handle[0], after)
        w_out_g = got[0].reshape(D, D)
        pool_g = got[1].transpose(1, 0, 2, 3).reshape(G, P, P)
        wg = [got[2 + 3 * l] for l in range(n_layers)]
        wu = [got[3 + 3 * l] for l in range(n_layers)]
        wd = [got[4 + 3 * l] for l in range(n_layers)]
        return w_out_g, pool_g, got[-1].reshape(1, D), wg, wu, wd

    in_flight = []

    def send(tag, grads, carry):
        if "pool_w" in grads:
            grads = dict(grads, pool_w=grads["pool_w"].reshape(G, N_DEV, P // N_DEV, P).transpose(1, 0, 2, 3))
        if "ab_w_out" in grads:
            grads = dict(grads, ab_w_out=grads["ab_w_out"].reshape(N_DEV, D // N_DEV, D))
        handle, carry = exchange_start("grads_" + tag + "_start", list(grads.values()), False, carry)
        in_flight.append((tag, list(grads.keys()), handle))
        return carry

    dx0, small = local_step(x[0], loss_target[0], mix_norm, ffn_norm, final_norm[None],
                            lb_logits, hg_out_norm, get_w_in, get_w_rest, send)

    recv = {}
    for tag, names, handle in in_flight:
        recv.update(zip(names, exchange_wait("grads_" + tag + "_wait", handle, dx0)))
    small_pack, layout = _pack_small({k: small[k] for k in _SMALL_NAMES})
    (small_all,) = exchange("gather_small", [small_pack], gather=True)
    tot = _unpack_small(sum_slots("sum_small", small_all), layout)

    res = {}
    res["ab_w_in"] = _adamw_nd("adamw_w_in", recv["ab_w_in"], ab_w_in, m_ab_w_in, v_ab_w_in)
    res["ab_w_out"] = _adamw_nd("adamw_w_out", recv["ab_w_out"], ab_w_out, m_ab_w_out, v_ab_w_out)
    res["pool_w"] = _adamw_nd("adamw_pool_w", recv["pool_w"], pool_w, m_pool_w, v_pool_w)
    ffn_in = {"ffn_w_gate": (ffn_w_gate, m_ffn_w_gate, v_ffn_w_gate),
              "ffn_w_up": (ffn_w_up, m_ffn_w_up, v_ffn_w_up),
              "ffn_w_down": (ffn_w_down, m_ffn_w_down, v_ffn_w_down)}
    for name, (w, m, v) in ffn_in.items():
        flip = name != "ffn_w_down"
        if flip:
            w, m, v = (jnp.swapaxes(a, 1, 2) for a in (w, m, v))
        outs = None
        for l in range(n_layers):
            outs = adamw("adamw_" + name, recv[name + "_" + str(l)], w, m, v, layer=l, prev=outs)
        res[name] = [jnp.swapaxes(o, 1, 2) for o in outs] if flip else outs

    n_ps = pool_scale.shape[1]
    small_g = dict(tot)
    small_g["pool_scale"] = lax.dynamic_slice(tot["pool_scale"], (0, me * n_ps), (1, n_ps))
    small_w = dict(mix_norm=(mix_norm, m_mix_norm, v_mix_norm), ffn_norm=(ffn_norm, m_ffn_norm, v_ffn_norm),
                   final_norm=(final_norm, m_final_norm, v_final_norm),
                   lb_logits=(lb_logits, m_lb_logits, v_lb_logits),
                   hg_out_norm=(hg_out_norm, m_hg_out_norm, v_hg_out_norm),
                   pool_scale=(pool_scale, m_pool_scale, v_pool_scale))
    g_pack, lay2 = _pack_small({k: small_g[k].reshape(small_w[k][0].shape) for k in small_w})
    w_pack, _ = _pack_small({k: small_w[k][0] for k in small_w})
    m_pack, _ = _pack_small({k: small_w[k][1] for k in small_w})
    v_pack, _ = _pack_small({k: small_w[k][2] for k in small_w})
    small_out = [_unpack_small(o, lay2) for o in adamw("adamw_small", g_pack[None], w_pack, m_pack, v_pack)]
    for k in small_w:
        res[k] = [small_out[o][k] for o in range(4)]

    order = ("mix_norm", "ffn_norm", "final_norm", "ab_w_in", "lb_logits", "hg_out_norm", "ab_w_out", "pool_w",
             "pool_scale", "ffn_w_gate", "ffn_w_up", "ffn_w_down")
    outs = [tot["loss"].reshape(()), dx0[None]]
    for o in range(4):
        outs += [res[k][o] for k in order]
    return tuple(outs)
```

```python
import math

import numpy as np
import jax
import jax.numpy as jnp
from jax import lax
from jax.experimental import pallas as pl
from jax.experimental.pallas import tpu as pltpu

F32 = jnp.float32
BF16 = jnp.bfloat16

N_DEV = 8
RMS_EPS = 1e-6
HEAD = 128
HG_CHUNK = 64
HG_HEADS_PER_BLOCK = 8
POOL_WINDOWS = (2, 4, 8, 16)
POOL_HALO = 16
ADAM_LR, ADAM_B1, ADAM_B2, ADAM_EPS, ADAM_WD, ADAM_STEP = 0.001, 0.9, 0.999, 1e-08, 0.01, 10
VMEM_LIMIT_BYTES = 60 * 1024 * 1024
MESH = pl.DeviceIdType.MESH

ROW_TILE = 512
ROW_TILE_WIDE = 1024
REDUCE_TILE = 2048
POOL_TILE = 512
HG_TILE = 512
SB_TILE = 512


def _params(sem):
    return pltpu.CompilerParams(dimension_semantics=sem, vmem_limit_bytes=VMEM_LIMIT_BYTES)


def _sigmoid(x):
    return 1.0 / (1.0 + jnp.exp(-x))


def rms_fwd(x, gain, out_dtype, ts=ROW_TILE):
    S, D = x.shape

    def body(x_ref, g_ref, h_ref, r_ref):
        xv = x_ref[...]
        r = lax.rsqrt(jnp.mean(xv * xv, axis=-1, keepdims=True) + RMS_EPS)
        h_ref[...] = ((xv * r) * g_ref[...]).astype(h_ref.dtype)
        r_ref[...] = r

    return pl.pallas_call(
        body, grid=(S // ts,), name="rms_fwd",
        in_specs=[pl.BlockSpec((ts, D), lambda i: (i, 0)), pl.BlockSpec((1, D), lambda i: (0, 0))],
        out_specs=[pl.BlockSpec((ts, D), lambda i: (i, 0)), pl.BlockSpec((ts, 1), lambda i: (i, 0))],
        out_shape=[jax.ShapeDtypeStruct((S, D), out_dtype), jax.ShapeDtypeStruct((S, 1), F32)],
        compiler_params=_params(("arbitrary",)),
    )(x, gain)


RMS_BWD_ROWS = 128


def _rms_bwd_tile(first, dh_of, x_ref, r_ref, g_ref, dres_ref, dx_ref, dxb_ref, dg_ref, rows):
    gv = g_ref[...]
    part = None
    for c in range(rows // RMS_BWD_ROWS):
        sl = slice(c * RMS_BWD_ROWS, (c + 1) * RMS_BWD_ROWS)
        rr = r_ref[sl, :]
        xh = x_ref[sl, :] * rr
        dhv = dh_of(sl)
        dxh = dhv * gv
        dx = dres_ref[sl, :] + rr * (dxh - xh * jnp.mean(dxh * xh, axis=-1, keepdims=True))
        dx_ref[sl, :] = dx
        dxb_ref[sl, :] = dx.astype(BF16)
        p = jnp.sum(dhv * xh, axis=0, keepdims=True)
        part = p if part is None else part + p

    @pl.when(first)
    def _():
        dg_ref[...] = part

    @pl.when(jnp.logical_not(first))
    def _():
        dg_ref[...] += part


def matmul_rms_bwd(name, a_ops, b_ops, *, grid, a_spec, b_spec, tm, x, r, gain, dres):
    S, D = x.shape
    n_pairs = len(a_ops)
    nk = grid[1]
    dn = (((1,), (1,)), ((), ()))

    def body(*refs):
        a_refs = refs[:n_pairs]
        b_refs = refs[n_pairs:2 * n_pairs]
        x_ref, r_ref, g_ref, dres_ref, dx_ref, dxb_ref, dg_ref, acc_ref = refs[2 * n_pairs:]
        i = pl.program_id(0)
        k = pl.program_id(1)

        @pl.when(k == 0)
        def _():
            acc_ref[...] = jnp.zeros_like(acc_ref)

        part = None
        for ar, br in zip(a_refs, b_refs):
            d = lax.dot_general(ar[...], br[...], dn, preferred_element_type=F32)
            part = d if part is None else part + d
        acc_ref[...] += part

        @pl.when(k == nk - 1)
        def _():
            _rms_bwd_tile(i == 0, lambda sl: acc_ref[sl, :], x_ref, r_ref, g_ref, dres_ref, dx_ref, dxb_ref,
                          dg_ref, tm)

    row = pl.BlockSpec((tm, D), lambda i, k: (i, 0))
    vec = pl.BlockSpec((1, D), lambda i, k: (0, 0))
    return pl.pallas_call(
        body, grid=grid, name=name,
        in_specs=[a_spec] * n_pairs + [b_spec] * n_pairs
                 + [row, pl.BlockSpec((tm, 1), lambda i, k: (i, 0)), vec, row],
        out_specs=[row, row, vec],
        out_shape=[jax.ShapeDtypeStruct((S, D), F32), jax.ShapeDtypeStruct((S, D), BF16),
                   jax.ShapeDtypeStruct((1, D), F32)],
        scratch_shapes=[pltpu.VMEM((tm, D), F32)],
        compiler_params=_params(("arbitrary", "arbitrary")),
    )(*a_ops, *b_ops, x, r, gain, dres)


def matmul_residual_rms(name, a, b, res, gain, tm=ROW_TILE):
    S, K = a.shape
    N = b.shape[1]

    def body(a_ref, b_ref, res_ref, g_ref, xo_ref, h_ref, r_ref):
        xo = res_ref[...] + jnp.dot(a_ref[...], b_ref[...], preferred_element_type=F32)
        xo_ref[...] = xo
        r = lax.rsqrt(jnp.mean(xo * xo, axis=-1, keepdims=True) + RMS_EPS)
        h_ref[...] = ((xo * r) * g_ref[...]).astype(BF16)
        r_ref[...] = r

    row = pl.BlockSpec((tm, N), lambda i: (i, 0))
    return pl.pallas_call(
        body, grid=(S // tm,), name=name,
        in_specs=[pl.BlockSpec((tm, K), lambda i: (i, 0)), pl.BlockSpec((K, N), lambda i: (0, 0)), row,
                  pl.BlockSpec((1, N), lambda i: (0, 0))],
        out_specs=[row, row, pl.BlockSpec((tm, 1), lambda i: (i, 0))],
        out_shape=[jax.ShapeDtypeStruct((S, N), F32), jax.ShapeDtypeStruct((S, N), BF16),
                   jax.ShapeDtypeStruct((S, 1), F32)],
        compiler_params=_params(("arbitrary",)),
    )(a, b, res, gain)


def loss_and_final_bwd(x, gain, target, ts=ROW_TILE):
    S, D = x.shape

    def body(x_ref, g_ref, t_ref, loss_ref, dx_ref, dxb_ref, dg_ref):
        i = pl.program_id(0)
        xv = x_ref[...]
        rr = lax.rsqrt(jnp.mean(xv * xv, axis=-1, keepdims=True) + RMS_EPS)
        xh = xv * rr
        err = xh * g_ref[...] - t_ref[...]
        part_loss = 0.5 * jnp.sum(jnp.mean(err * err, axis=-1, keepdims=True))
        dy = err / D
        dxh = dy * g_ref[...]
        dx = rr * (dxh - xh * jnp.mean(dxh * xh, axis=-1, keepdims=True))
        dx_ref[...] = dx
        dxb_ref[...] = dx.astype(BF16)
        part = jnp.sum(dy * xh, axis=0, keepdims=True)

        @pl.when(i == 0)
        def _():
            dg_ref[...] = part
            loss_ref[...] = jnp.zeros_like(loss_ref) + part_loss

        @pl.when(i > 0)
        def _():
            dg_ref[...] += part
            loss_ref[...] += part_loss

    row = pl.BlockSpec((ts, D), lambda i: (i, 0))
    vec = pl.BlockSpec((1, D), lambda i: (0, 0))
    return pl.pallas_call(
        body, grid=(S // ts,), name="loss_final",
        in_specs=[row, vec, row],
        out_specs=[pl.BlockSpec((8, 128), lambda i: (0, 0)), row, row, vec],
        out_shape=[jax.ShapeDtypeStruct((8, 128), F32), jax.ShapeDtypeStruct((S, D), F32),
                   jax.ShapeDtypeStruct((S, D), BF16), jax.ShapeDtypeStruct((1, D), F32)],
        compiler_params=_params(("arbitrary",)),
    )(x, gain, target)


def matmul(name, a_ops, b_ops, *, grid, a_spec, b_spec, out_spec, out_shape, out_dtypes, acc_shape,
           trans_a=False, trans_b=False, res=None, res_spec=None, bf16_scale=None, bf16_scale_spec=None):
    n_pairs = len(a_ops)
    n_out = len(out_dtypes)
    nk = grid[-1]
    kaxis = len(grid) - 1
    dn = (((0,) if trans_a else (1,), (1,) if trans_b else (0,)), ((), ()))

    def body(*refs):
        a_refs = refs[:n_pairs]
        b_refs = refs[n_pairs:2 * n_pairs]
        pos = 2 * n_pairs
        res_ref = None
        if res is not None:
            res_ref = refs[pos]
            pos += 1
        scale_ref = None
        if bf16_scale is not None:
            scale_ref = refs[pos]
            pos += 1
        out_refs = refs[pos:pos + n_out]
        acc_ref = refs[pos + n_out]
        k = pl.program_id(kaxis)
        in_place = n_out == 1 and out_dtypes[0] == F32
        target = out_refs[0] if in_place else acc_ref

        def finish(val):
            if res_ref is not None:
                val = val + res_ref[...]
            for o in out_refs:
                if scale_ref is not None and o.dtype == BF16:
                    o[...] = (val * scale_ref[...]).astype(BF16)
                else:
                    o[...] = val.astype(o.dtype)

        if nk > 1:
            @pl.when(k == 0)
            def _():
                if in_place and res_ref is not None:
                    target[...] = res_ref[...]
                else:
                    target[...] = jnp.zeros_like(target)

        part = None
        for ar, br in zip(a_refs, b_refs):
            d = lax.dot_general(ar[...].astype(BF16), br[...].astype(BF16), dn, preferred_element_type=F32)
            part = d if part is None else part + d

        if nk == 1:
            finish(part)
        else:
            target[...] += part
            if not in_place:
                @pl.when(k == nk - 1)
                def _():
                    finish(acc_ref[...])

    in_specs = [a_spec] * n_pairs + [b_spec] * n_pairs
    operands = list(a_ops) + list(b_ops)
    if res is not None:
        in_specs.append(res_spec)
        operands.append(res)
    if bf16_scale is not None:
        in_specs.append(bf16_scale_spec)
        operands.append(bf16_scale)
    return pl.pallas_call(
        body, grid=grid, name=name, in_specs=in_specs,
        out_specs=[out_spec] * n_out,
        out_shape=[jax.ShapeDtypeStruct(out_shape, dt) for dt in out_dtypes],
        scratch_shapes=[pltpu.VMEM(acc_shape, F32)],
        compiler_params=_params(("arbitrary",) * len(grid)),
    )(*operands)


def ffn_gate_up(h, wg, wu, tm=ROW_TILE_WIDE):
    S, D = h.shape
    nb = wg.shape[2]

    def body(h_ref, wg_ref, wu_ref, p_ref, r_ref, a_ref):
        for c in range(2):
            rows = slice(c * (tm // 2), (c + 1) * (tm // 2))
            hv = h_ref[rows, :]
            g = jnp.dot(hv, wg_ref[...], preferred_element_type=F32)
            u = jnp.dot(hv, wu_ref[...], preferred_element_type=F32)
            s = _sigmoid(g)
            p = g * s
            p_ref[rows, :] = p
            r_ref[rows, :] = u * (s * (1.0 + g * (1.0 - s)))
            a_ref[rows, :] = (p * u).astype(BF16)

    wspec = pl.BlockSpec((None, D, nb), lambda j, i: (j, 0, 0))
    ospec = pl.BlockSpec((None, tm, nb), lambda j, i: (j, i, 0))
    return pl.pallas_call(
        body, grid=(N_DEV, S // tm), name="ffn_gate_up",
        in_specs=[pl.BlockSpec((tm, D), lambda j, i: (i, 0)), wspec, wspec],
        out_specs=[ospec, ospec, ospec],
        out_shape=[jax.ShapeDtypeStruct((N_DEV, S, nb), F32), jax.ShapeDtypeStruct((N_DEV, S, nb), F32),
                   jax.ShapeDtypeStruct((N_DEV, S, nb), BF16)],
        compiler_params=_params(("arbitrary", "arbitrary")),
    )(h, wg, wu)


def ffn_bwd_hidden(dy, wd, p, r, tm=ROW_TILE_WIDE):
    S, D = dy.shape
    nb = wd.shape[1]

    def body(dy_ref, wd_ref, p_ref, r_ref, dg_ref, du_ref):
        for c in range(2):
            rows = slice(c * (tm // 2), (c + 1) * (tm // 2))
            da = lax.dot_general(dy_ref[rows, :], wd_ref[...], (((1,), (1,)), ((), ())),
                                 preferred_element_type=F32)
            du_ref[rows, :] = (da * p_ref[rows, :]).astype(BF16)
            dg_ref[rows, :] = (da * r_ref[rows, :]).astype(BF16)

    hspec = pl.BlockSpec((None, tm, nb), lambda j, i: (j, i, 0))
    return pl.pallas_call(
        body, grid=(N_DEV, S // tm), name="ffn_bwd_hidden",
        in_specs=[pl.BlockSpec((tm, D), lambda j, i: (i, 0)), pl.BlockSpec((None, nb, D), lambda j, i: (j, 0, 0)),
                  hspec, hspec],
        out_specs=[hspec, hspec],
        out_shape=[jax.ShapeDtypeStruct((N_DEV, S, nb), BF16), jax.ShapeDtypeStruct((N_DEV, S, nb), BF16)],
        compiler_params=_params(("arbitrary", "arbitrary")),
    )(dy, wd, p, r)


def ffn_forward(h, xres, wg, wu, wd, tm=ROW_TILE_WIDE):
    S, D = h.shape
    nb = wg.shape[2]
    g, u, a = ffn_gate_up(h, wg, wu)
    (xo,) = matmul(
        "ffn_down", [a], [wd], grid=(S // tm, N_DEV),
        a_spec=pl.BlockSpec((None, tm, nb), lambda i, j: (j, i, 0)),
        b_spec=pl.BlockSpec((None, nb, D), lambda i, j: (j, 0, 0)),
        out_spec=pl.BlockSpec((tm, D), lambda i, j: (i, 0)), out_shape=(S, D), out_dtypes=[F32],
        acc_shape=(tm, D), res=xres, res_spec=pl.BlockSpec((tm, D), lambda i, j: (i, 0)))
    return xo, (g, u, a)


def ffn_backward(dy_b, h, saved, wg, wu, wd, x, r, gain, dres, tm=ROW_TILE, tk=REDUCE_TILE):
    S, D = h.shape
    nb = wg.shape[2]
    g, u, a = saved
    dg, du = ffn_bwd_hidden(dy_b, wd, g, u)
    dx = matmul_rms_bwd(
        "ffn_dh", [dg, du], [wg, wu], grid=(S // tm, N_DEV),
        a_spec=pl.BlockSpec((None, tm, nb), lambda i, j: (j, i, 0)),
        b_spec=pl.BlockSpec((None, D, nb), lambda i, j: (j, 0, 0)),
        tm=tm, x=x, r=r, gain=gain, dres=dres)

    def wgrad_in(name, dhid):
        (dw,) = matmul(
            name, [dhid], [h], grid=(N_DEV, S // tk),
            a_spec=pl.BlockSpec((None, tk, nb), lambda j, k: (j, k, 0)),
            b_spec=pl.BlockSpec((tk, D), lambda j, k: (k, 0)),
            out_spec=pl.BlockSpec((None, nb, D), lambda j, k: (j, 0, 0)), out_shape=(N_DEV, nb, D),
            out_dtypes=[BF16], acc_shape=(nb, D), trans_a=True)
        return dw

    dwg = wgrad_in("ffn_dwg", dg)
    dwu = wgrad_in("ffn_dwu", du)
    (dwd,) = matmul(
        "ffn_dwd", [a], [dy_b], grid=(N_DEV, S // tk),
        a_spec=pl.BlockSpec((None, tk, nb), lambda j, k: (j, k, 0)),
        b_spec=pl.BlockSpec((tk, D), lambda j, k: (k, 0)),
        out_spec=pl.BlockSpec((None, nb, D), lambda j, k: (j, 0, 0)), out_shape=(N_DEV, nb, D),
        out_dtypes=[BF16], acc_shape=(nb, D), trans_a=True)
    return dx, dwg, dwu, dwd


def _pool_counts(row0, n, w):
    pos = row0 + lax.broadcasted_iota(jnp.int32, (n, 1), 0)
    return jnp.minimum(pos + 1, w).astype(F32)


def pool_forward(h, xres, w, scale, gain_next, ts=POOL_TILE):
    S, D = h.shape
    G = len(POOL_WINDOWS)
    P = D // G
    hb = ts // POOL_HALO

    def body(h_ref, halo_ref, x_ref, w_ref, s_ref, gn_ref, xo_ref, p_ref, hn_ref, rn_ref):
        i = pl.program_id(0)
        for gi, win in enumerate(POOL_WINDOWS):
            cols = slice(gi * P, (gi + 1) * P)
            cur = h_ref[:, cols]
            halo = jnp.where(i > 0, halo_ref[:, cols], 0.0)
            acc = jnp.concatenate([halo, cur], axis=0)
            step = 1
            while step < win:
                acc = acc + pltpu.roll(acc, step, 0)
                step *= 2
            wsum = acc[POOL_HALO:, :]
            pooled = wsum / _pool_counts(i * ts, ts, win) - cur
            pb = pooled.astype(BF16)
            p_ref[:, cols] = pb
            mixed = jnp.dot(pb, w_ref[gi], preferred_element_type=F32)
            xo_ref[:, cols] = x_ref[:, cols] + mixed * s_ref[:, cols]
        xo = xo_ref[...]
        r = lax.rsqrt(jnp.mean(xo * xo, axis=-1, keepdims=True) + RMS_EPS)
        hn_ref[...] = ((xo * r) * gn_ref[...]).astype(BF16)
        rn_ref[...] = r

    row = pl.BlockSpec((ts, D), lambda i: (i, 0))
    vec = pl.BlockSpec((1, D), lambda i: (0, 0))
    return pl.pallas_call(
        body, grid=(S // ts,), name="pool_fwd",
        in_specs=[row, pl.BlockSpec((POOL_HALO, D), lambda i: (jnp.maximum(i * hb - 1, 0), 0)), row,
                  pl.BlockSpec((G, P, P), lambda i: (0, 0, 0)), vec, vec],
        out_specs=[row, row, row, pl.BlockSpec((ts, 1), lambda i: (i, 0))],
        out_shape=[jax.ShapeDtypeStruct((S, D), F32), jax.ShapeDtypeStruct((S, D), BF16),
                   jax.ShapeDtypeStruct((S, D), BF16), jax.ShapeDtypeStruct((S, 1), F32)],
        compiler_params=_params(("arbitrary",)),
    )(h, h, xres, w, scale, gain_next)


def pool_backward_mix(dx, pooled, w, scale, ts=POOL_TILE):
    S, D = dx.shape
    G = len(POOL_WINDOWS)
    P = D // G

    def body(dx_ref, p_ref, w_ref, s_ref, dm_ref, dp_ref, ds_ref):
        i = pl.program_id(0)
        parts = []
        for gi in range(G):
            cols = slice(gi * P, (gi + 1) * P)
            dxv = dx_ref[:, cols]
            dmb = (dxv * s_ref[:, cols]).astype(BF16)
            dm_ref[:, cols] = dmb
            dp_ref[:, cols] = lax.dot_general(dmb, w_ref[gi], (((1,), (1,)), ((), ())),
                                              preferred_element_type=F32)
            mixed = jnp.dot(p_ref[:, cols], w_ref[gi], preferred_element_type=F32)
            parts.append(jnp.sum(dxv * mixed, axis=0, keepdims=True))
        part = jnp.concatenate(parts, axis=1)

        @pl.when(i == 0)
        def _():
            ds_ref[...] = part

        @pl.when(i > 0)
        def _():
            ds_ref[...] += part

    row = pl.BlockSpec((ts, D), lambda i: (i, 0))
    vec = pl.BlockSpec((1, D), lambda i: (0, 0))
    return pl.pallas_call(
        body, grid=(S // ts,), name="pool_bwd_mix",
        in_specs=[row, row, pl.BlockSpec((G, P, P), lambda i: (0, 0, 0)), vec],
        out_specs=[row, row, vec],
        out_shape=[jax.ShapeDtypeStruct((S, D), BF16), jax.ShapeDtypeStruct((S, D), F32),
                   jax.ShapeDtypeStruct((1, D), F32)],
        compiler_params=_params(("arbitrary",)),
    )(dx, pooled, w, scale)


def pool_backward_window(dp, x, r, gain, dres, ts=POOL_TILE):
    S, D = dp.shape
    G = len(POOL_WINDOWS)
    P = D // G
    hb = ts // POOL_HALO
    n_i = S // ts
    n_rows = ts + POOL_HALO

    def body(dp_ref, halo_ref, x_ref, r_ref, g_ref, dres_ref, dx_ref, dxb_ref, dg_ref, dh_ref):
        i = pl.program_id(0)
        for gi, win in enumerate(POOL_WINDOWS):
            cols = slice(gi * P, (gi + 1) * P)
            cur = dp_ref[:, cols]
            halo = jnp.where(i < n_i - 1, halo_ref[:, cols], 0.0)
            acc = jnp.concatenate([cur / _pool_counts(i * ts, ts, win),
                                   halo / _pool_counts((i + 1) * ts, POOL_HALO, win)], axis=0)
            step = 1
            while step < win:
                acc = acc + pltpu.roll(acc, n_rows - step, 0)
                step *= 2
            dh_ref[:, cols] = acc[:ts, :] - cur
        _rms_bwd_tile(i == 0, lambda sl: dh_ref[sl, :], x_ref, r_ref, g_ref, dres_ref, dx_ref, dxb_ref, dg_ref, ts)

    row = pl.BlockSpec((ts, D), lambda i: (i, 0))
    vec = pl.BlockSpec((1, D), lambda i: (0, 0))
    return pl.pallas_call(
        body, grid=(n_i,), name="pool_bwd_window",
        in_specs=[row, pl.BlockSpec((POOL_HALO, D), lambda i: (jnp.minimum((i + 1) * hb, S // POOL_HALO - 1), 0)),
                  row, pl.BlockSpec((ts, 1), lambda i: (i, 0)), vec, row],
        out_specs=[row, row, vec],
        out_shape=[jax.ShapeDtypeStruct((S, D), F32), jax.ShapeDtypeStruct((S, D), BF16),
                   jax.ShapeDtypeStruct((1, D), F32)],
        scratch_shapes=[pltpu.VMEM((ts, D), F32)],
        compiler_params=_params(("arbitrary",)),
    )(dp, dp, x, r, gain, dres)


_HG_LEVELS = (32, 16, 8, 4, 2, 1)
_N_LEV = len(_HG_LEVELS) + 1


def _hgrn_constants():
    C = HG_CHUNK
    t = np.arange(C)
    tri = (t[None, :] <= t[:, None]).astype(np.float32)
    blocks = [tri]
    masks, upq, upk = [], [], []
    for m in _HG_LEVELS:
        p = (t // (2 * m)) * 2 * m + m - 1
        blocks.append(tri[p])
        masks.append(((t[:, None] // (2 * m)) == (t[None, :] // (2 * m))).astype(np.float32))
        upper = (t % (2 * m)) >= m
        upq.append(np.repeat(upper[:, None], HEAD, 1).astype(np.float32))
        upk.append(np.repeat(~upper[:, None], HEAD, 1).astype(np.float32))
    blocks.append(tri)
    masks.append(np.eye(C, dtype=np.float32))
    upq.append(np.ones((C, HEAD), np.float32))
    upk.append(np.ones((C, HEAD), np.float32))
    mstack = np.concatenate(blocks, axis=0)
    mstack3 = np.concatenate([mstack] * 3, axis=1)
    trirev3 = np.concatenate([tri.T] * 3, axis=1)
    return (jnp.asarray(mstack3, BF16), jnp.asarray(np.stack(masks)), jnp.asarray(np.stack(upq)),
            jnp.asarray(np.stack(upk)), jnp.asarray(trirev3, BF16))


def _split3(x):
    hi = x.astype(BF16)
    r1 = x - hi.astype(F32)
    mid = r1.astype(BF16)
    lo = (r1 - mid.astype(F32)).astype(BF16)
    return jnp.concatenate([hi, mid, lo], axis=0)


def _hgrn_chunk_common(qa, fa, lbv, mstack3, upq, upk):
    sq = _sigmoid(qa)
    q = qa * sq
    sf = _sigmoid(fa)
    f = lbv + (1.0 - lbv) * sf
    g = jnp.log(f)
    k = 1.0 - f
    gall = jnp.dot(mstack3, _split3(g), preferred_element_type=F32).reshape(_N_LEV + 1, HG_CHUNK, HEAD)
    G = gall[0]
    eq_exp = G[None] - gall[1:]
    eq = jnp.exp(jnp.minimum(eq_exp, 0.0)) * upq
    ek = jnp.exp(jnp.minimum(-eq_exp, 0.0)) * upk
    Qs = (q[None] * eq).astype(BF16)
    Ks = (k[None] * ek).astype(BF16)
    return sq, q, sf, f, k, G, eq, ek, Qs, Ks


def hgrn_forward(proj, lb, hg_norm, ts=HG_TILE):
    S = proj.shape[0]
    nh = lb.shape[1] // HEAD
    C = HG_CHUNK
    ncs = ts // C
    mstack3, masks, upq, upk, _ = _hgrn_constants()

    def body(qa_ref, fa_ref, ia_ref, ga_ref, lb_ref, gn_ref, ms_ref, mk_ref, uq_ref, uk_ref,
             oa_ref, oraw_ref, st_ref, state):
        tt = pl.program_id(1)

        @pl.when(tt == 0)
        def _():
            state[...] = jnp.zeros_like(state)

        gn = gn_ref[...]

        def chunk(c, carry):
            sl = pl.ds(pl.multiple_of(c * C, C), C)
            for hh in range(HG_HEADS_PER_BLOCK):
                cols = slice(hh * HEAD, (hh + 1) * HEAD)
                qa, fa, v, ga = qa_ref[sl, cols], fa_ref[sl, cols], ia_ref[sl, cols], ga_ref[sl, cols]
                _, q, _, _, k, G, _, _, Qs, Ks = _hgrn_chunk_common(qa, fa, lb_ref[:, cols], ms_ref[...],
                                                                    uq_ref[...], uk_ref[...])
                att7 = lax.dot_general(Qs, Ks, (((2,), (2,)), ((0,), (0,))), preferred_element_type=F32)
                att = jnp.sum(att7 * mk_ref[...], axis=0)
                st = state[hh]
                st_ref[hh, c] = st
                vb = v.astype(BF16)
                qg = (q * jnp.exp(G)).astype(BF16)
                o = jnp.dot(att.astype(BF16), vb, preferred_element_type=F32)
                o = o + lax.dot_general(qg, st.astype(BF16), (((1,), (1,)), ((), ())),
                                        preferred_element_type=F32)
                g_last = G[C - 1:C, :]
                kh = (k * jnp.exp(g_last - G)).astype(BF16)
                state[hh] = st * jnp.exp(g_last) + lax.dot_general(vb, kh, (((0,), (0,)), ((), ())),
                                                                   preferred_element_type=F32)
                oraw_ref[sl, cols] = o
                r = lax.rsqrt(jnp.mean(o * o, axis=-1, keepdims=True) + RMS_EPS)
                oa_ref[sl, cols] = (((o * r) * gn) * (ga * _sigmoid(ga))).astype(BF16)
            return carry

        lax.fori_loop(0, ncs, chunk, 0)

    hpb = HG_HEADS_PER_BLOCK
    wide = hpb * HEAD

    def col(m0):
        return pl.BlockSpec((ts, wide), lambda h, t: (t, m0 // hpb + h))

    const3 = lambda shape: pl.BlockSpec(shape, lambda h, t: (0, 0, 0))
    return pl.pallas_call(
        body, grid=(nh // hpb, S // ts), name="hgrn_fwd",
        in_specs=[col(0), col(nh), col(2 * nh), col(3 * nh),
                  pl.BlockSpec((1, wide), lambda h, t: (0, h)), pl.BlockSpec((1, HEAD), lambda h, t: (0, 0)),
                  pl.BlockSpec(mstack3.shape, lambda h, t: (0, 0)), const3(masks.shape), const3(upq.shape),
                  const3(upk.shape)],
        out_specs=[pl.BlockSpec((ts, wide), lambda h, t: (t, h)), pl.BlockSpec((ts, wide), lambda h, t: (t, h)),
                   pl.BlockSpec((hpb, ncs, HEAD, HEAD), lambda h, t: (h, t, 0, 0))],
        out_shape=[jax.ShapeDtypeStruct((S, nh * HEAD), BF16), jax.ShapeDtypeStruct((S, nh * HEAD), F32),
                   jax.ShapeDtypeStruct((nh, S // C, HEAD, HEAD), F32)],
        scratch_shapes=[pltpu.VMEM((hpb, HEAD, HEAD), F32)],
        compiler_params=_params(("arbitrary", "arbitrary")),
    )(proj, proj, proj, proj, lb, hg_norm, mstack3, masks, upq, upk)


def hgrn_backward(dcat, proj, oraw, states, lb, hg_norm, ts=HG_TILE):
    S = proj.shape[0]
    nh = lb.shape[1] // HEAD
    C = HG_CHUNK
    ncs = ts // C
    nt = S // ts
    mstack3, masks, upq, upk, trirev3 = _hgrn_constants()

    def body(do_ref, qa_ref, fa_ref, ia_ref, ga_ref, or_ref, st_ref, lb_ref, gn_ref, ms_ref, mk_ref, uq_ref,
             uk_ref, tr_ref, dqa_ref, dfa_ref, dia_ref, dga_ref, dlb_ref, dgn_ref, dstate):
        tt = pl.program_id(1)

        @pl.when(tt == 0)
        def _():
            dstate[...] = jnp.zeros_like(dstate)
            dlb_ref[...] = jnp.zeros_like(dlb_ref)
            dgn_ref[...] = jnp.zeros_like(dgn_ref)

        gn = gn_ref[...]

        def chunk(cc, carry):
            c = ncs - 1 - cc
            sl = pl.ds(pl.multiple_of(c * C, C), C)
            for hh in range(HG_HEADS_PER_BLOCK):
                cols = slice(hh * HEAD, (hh + 1) * HEAD)
                lbv = lb_ref[:, cols]
                qa, fa, v, ga = qa_ref[sl, cols], fa_ref[sl, cols], ia_ref[sl, cols], ga_ref[sl, cols]
                sq, q, sf, f, k, G, eq, ek, Qs, Ks = _hgrn_chunk_common(qa, fa, lbv, ms_ref[...], uq_ref[...],
                                                                        uk_ref[...])
                mk = mk_ref[...]
                att7 = lax.dot_general(Qs, Ks, (((2,), (2,)), ((0,), (0,))), preferred_element_type=F32)
                att = jnp.sum(att7 * mk, axis=0)
                o = or_ref[sl, cols]
                dO = do_ref[sl, cols]
                sg = _sigmoid(ga)
                r = lax.rsqrt(jnp.mean(o * o, axis=-1, keepdims=True) + RMS_EPS)
                xh = o * r
                dga_ref[sl, cols] = (dO * (xh * gn) * (sg * (1.0 + ga * (1.0 - sg)))).astype(BF16)
                don = dO * (ga * sg)
                dgn_ref[hh] += jnp.sum(don * xh, axis=0, keepdims=True)
                dxh = don * gn
                do = r * (dxh - xh * jnp.mean(dxh * xh, axis=-1, keepdims=True))
                dob = do.astype(BF16)
                st = st_ref[hh, c]
                dst = dstate[hh]
                dstb = dst.astype(BF16)
                vb = v.astype(BF16)
                eG = jnp.exp(G)
                g_last = G[C - 1:C, :]
                e_last = jnp.exp(g_last)
                e_tail = jnp.exp(g_last - G)
                qg = (q * eG).astype(BF16)
                kh = (k * e_tail).astype(BF16)
                dq_inter = jnp.dot(dob, st.astype(BF16), preferred_element_type=F32) * eG
                dk_inter = jnp.dot(vb, dstb, preferred_element_type=F32) * e_tail
                dv = lax.dot_general(kh, dstb, (((1,), (1,)), ((), ())), preferred_element_type=F32)
                dv = dv + lax.dot_general(att.astype(BF16), dob, (((0,), (0,)), ((), ())),
                                          preferred_element_type=F32)
                dA = lax.dot_general(dob, vb, (((1,), (1,)), ((), ())), preferred_element_type=F32)
                dA7 = (dA[None] * mk).astype(BF16)
                dAT7 = (dA.T[None] * mk).astype(BF16)
                dQs = lax.dot_general(dA7, Ks, (((2,), (1,)), ((0,), (0,))), preferred_element_type=F32)
                dKs = lax.dot_general(dAT7, Qs, (((2,), (1,)), ((0,), (0,))), preferred_element_type=F32)
                dq = dq_inter + jnp.sum(dQs * eq, axis=0)
                dk = dk_inter + jnp.sum(dKs * ek, axis=0)
                dG = (jnp.sum(Qs.astype(F32) * dQs - Ks.astype(F32) * dKs, axis=0)
                      + q * dq_inter - k * dk_inter)
                last_extra = (jnp.sum(k * dk_inter, axis=0, keepdims=True)
                              + e_last * jnp.sum(dst * st, axis=0, keepdims=True))
                is_last = lax.broadcasted_iota(jnp.int32, (C, 1), 0) == C - 1
                dG = dG + jnp.where(is_last, last_extra, 0.0)
                dg = jnp.dot(tr_ref[...], _split3(dG), preferred_element_type=F32)
                df = dg / f - dk
                dfa_ref[sl, cols] = (df * (1.0 - lbv) * (sf * (1.0 - sf))).astype(BF16)
                dlb_ref[:, cols] += jnp.sum(df * (1.0 - sf), axis=0, keepdims=True)
                dqa_ref[sl, cols] = (dq * (sq * (1.0 + qa * (1.0 - sq)))).astype(BF16)
                dia_ref[sl, cols] = dv.astype(BF16)
                dstate[hh] = dst * e_last + lax.dot_general(dob, qg, (((0,), (0,)), ((), ())),
                                                            preferred_element_type=F32)
            return carry

        lax.fori_loop(0, ncs, chunk, 0)

    hpb = HG_HEADS_PER_BLOCK
    wide = hpb * HEAD

    def col(m0):
        return pl.BlockSpec((ts, wide), lambda h, t: (nt - 1 - t, m0 // hpb + h))

    const3 = lambda shape: pl.BlockSpec(shape, lambda h, t: (0, 0, 0))
    const2 = lambda shape: pl.BlockSpec(shape, lambda h, t: (0, 0))
    ocol = pl.BlockSpec((ts, wide), lambda h, t: (nt - 1 - t, h))
    half = nh * HEAD
    return pl.pallas_call(
        body, grid=(nh // hpb, nt), name="hgrn_bwd",
        in_specs=[col(0), col(0), col(nh), col(2 * nh), col(3 * nh), col(0),
                  pl.BlockSpec((hpb, ncs, HEAD, HEAD), lambda h, t: (h, nt - 1 - t, 0, 0)),
                  pl.BlockSpec((1, wide), lambda h, t: (0, h)), const2((1, HEAD)),
                  const2(mstack3.shape), const3(masks.shape), const3(upq.shape), const3(upk.shape),
                  const2(trirev3.shape)],
        out_specs=[ocol, ocol, ocol, ocol, pl.BlockSpec((1, wide), lambda h, t: (0, h)),
                   pl.BlockSpec((hpb, 1, HEAD), lambda h, t: (h, 0, 0))],
        out_shape=[jax.ShapeDtypeStruct((S, half), BF16)] * 4
                  + [jax.ShapeDtypeStruct((1, half), F32), jax.ShapeDtypeStruct((nh, 1, HEAD), F32)],
        scratch_shapes=[pltpu.VMEM((hpb, HEAD, HEAD), F32)],
        compiler_params=_params(("arbitrary", "arbitrary")),
    )(dcat, proj, proj, proj, proj, oraw, states, lb, hg_norm, mstack3, masks, upq, upk, trirev3)


SB_SUB = 128
LOG2_E = 1.4426950408889634
SB_SCALE = 1.0 / math.sqrt(HEAD)
SB_QUERY_SCALE = SB_SCALE * LOG2_E


def _split2(x):
    hi = x.astype(BF16)
    lo = (x - hi.astype(F32)).astype(BF16)
    return jnp.concatenate([hi, lo], axis=1)


def _sb_constants():
    j = np.arange(SB_SUB)
    after = (j[:, None] > j[None, :]).astype(np.float32)
    before = (j[:, None] < j[None, :]).astype(np.float32)
    return (jnp.asarray(np.concatenate([after, after], axis=0), BF16),
            jnp.asarray(np.concatenate([before, before], axis=0), BF16))


def _sb_tri(i):
    return (i * (i + 1)) // 2


def _sb_diag_mask(t):
    return lax.broadcasted_iota(jnp.int32, (t, t), 1) < lax.broadcasted_iota(jnp.int32, (t, t), 0)


def _sb_scores(q, k_ref, col0, t):
    ks = k_ref[pl.ds(pl.multiple_of(col0, t), t), :]
    return lax.dot_general(q, ks, (((1,), (1,)), ((), ())), preferred_element_type=F32)


def _sb_weights(z, diagonal, run, after2):
    t = z.shape[0]
    nsub = z.shape[1] // SB_SUB
    lks, locs, tots = [], [], []
    for b in range(nsub):
        r0 = b * SB_SUB if diagonal else 0
        zb = z[r0:, b * SB_SUB:(b + 1) * SB_SUB]
        nzb = -zb
        lkb = jnp.minimum(nzb, 0.0) - jnp.log(1.0 + jnp.exp2(jnp.minimum(zb, nzb))) * LOG2_E
        if diagonal:
            rows = lax.broadcasted_iota(jnp.int32, zb.shape, 0)
            visible = lax.broadcasted_iota(jnp.int32, zb.shape, 1) < rows
            lkb = jnp.where(visible, lkb, 0.0)
        loc = jnp.dot(_split2(lkb), after2, preferred_element_type=F32)
        lks.append(lkb)
        locs.append(loc)
        tots.append(loc[:, 0:1] + lkb[:, 0:1])
    ws = [None] * nsub
    for b in reversed(range(nsub)):
        r0 = b * SB_SUB if diagonal else 0
        zb = z[r0:, b * SB_SUB:(b + 1) * SB_SUB]
        wb = jnp.exp2(zb + lks[b] + (locs[b] + run[r0:]))
        tot = tots[b]
        if diagonal:
            rows = lax.broadcasted_iota(jnp.int32, zb.shape, 0)
            wb = jnp.where(lax.broadcasted_iota(jnp.int32, zb.shape, 1) < rows, wb, 0.0)
            if r0:
                wb = jnp.concatenate([jnp.zeros((r0, SB_SUB), F32), wb], axis=0)
                tot = jnp.concatenate([jnp.zeros((r0, 1), F32), tot], axis=0)
        ws[b] = wb
        run = run + tot
    return jnp.concatenate(ws, axis=1), run


def sb_forward(projb, nh, m0, t=SB_TILE):
    S = projb.shape[0]
    after2, _ = _sb_constants()
    n_i = S // t

    def body(q_ref, k_ref, v_ref, af_ref, o_ref, w_hbm, wbuf, wsem):
        h = pl.program_id(0)
        i = pl.program_id(1)
        q = q_ref[...]
        after = af_ref[...]
        base = _sb_tri(i)

        def store(slot, jb):
            return pltpu.make_async_copy(wbuf.at[slot], w_hbm.at[h, base + jb], wsem.at[slot])

        def block(n, jb, run, diagonal):
            slot = n % 2

            @pl.when(n >= 2)
            def _():
                store(slot, jb).wait()

            z = _sb_scores(q, k_ref, jb * t, t)
            w, run = _sb_weights(z, diagonal, run, after)
            wb = w.astype(BF16)
            wbuf[slot] = wb
            store(slot, jb).start()
            vs = v_ref[pl.ds(pl.multiple_of(jb * t, t), t), :]
            return run, jnp.dot(wb, vs, preferred_element_type=F32)

        run, acc = block(0, i, jnp.zeros((t, 1), F32), True)

        def step(n, carry):
            run, acc = carry
            run, part = block(n + 1, i - 1 - n, run, False)
            return run, acc + part

        _, acc = lax.fori_loop(0, i, step, (run, acc))
        o_ref[...] = acc.astype(BF16)
        store(i % 2, 0).wait()

        @pl.when(i >= 1)
        def _():
            store((i + 1) % 2, 0).wait()

    return pl.pallas_call(
        body, grid=(nh, n_i), name="sb_fwd",
        in_specs=[pl.BlockSpec((t, HEAD), lambda h, i: (i, m0 + h)),
                  pl.BlockSpec((S, HEAD), lambda h, i: (0, m0 + nh + h)),
                  pl.BlockSpec((S, HEAD), lambda h, i: (0, m0 + 2 * nh + h)),
                  pl.BlockSpec(after2.shape, lambda h, i: (0, 0))],
        out_specs=[pl.BlockSpec((t, HEAD), lambda h, i: (i, h)), pl.BlockSpec(memory_space=pl.ANY)],
        out_shape=[jax.ShapeDtypeStruct((S, nh * HEAD), BF16),
                   jax.ShapeDtypeStruct((nh, _sb_tri(n_i), t, t), BF16)],
        scratch_shapes=[pltpu.VMEM((2, t, t), BF16), pltpu.SemaphoreType.DMA((2,))],
        compiler_params=_params(("arbitrary", "arbitrary")),
    )(projb, projb, projb, after2)


def sb_backward(dcat, projb, w_all, nh, m0, t=SB_TILE):
    S = projb.shape[0]
    _, before2 = _sb_constants()
    n_i = S // t
    nsub = t // SB_SUB

    def body(do_ref, q_ref, k_ref, v_ref, bf_ref, w_hbm, dq_ref, dk_ref, dv_ref, dk_acc, dv_acc, wbuf, wsem):
        h = pl.program_id(0)
        i = pl.program_id(1)

        @pl.when(i == 0)
        def _():
            dk_acc[...] = jnp.zeros_like(dk_acc)
            dv_acc[...] = jnp.zeros_like(dv_acc)

        q = q_ref[...]
        dob = do_ref[...].astype(BF16)
        before = bf_ref[...]
        base = _sb_tri(i)

        def load(slot, jb):
            return pltpu.make_async_copy(w_hbm.at[h, base + jb], wbuf.at[slot], wsem.at[slot])

        load(0, 0).start()

        def left_to_right(jb, run, dq, mask):
            slot = jb % 2
            load(slot, jb).wait()

            @pl.when(jb < i)
            def _():
                load(1 - slot, jb + 1).start()

            ksl = pl.ds(pl.multiple_of(jb * t, t), t)
            wb = wbuf[slot]
            z = _sb_scores(q, k_ref, jb * t, t)
            dw = lax.dot_general(dob, v_ref[ksl, :], (((1,), (1,)), ((), ())), preferred_element_type=F32)
            d = dw * wb.astype(F32)
            dv_acc[ksl, :] += lax.dot_general(wb, dob, (((0,), (0,)), ((), ())), preferred_element_type=F32)
            sig = 1.0 / (1.0 + jnp.exp2(-z))
            das = []
            for b in range(nsub):
                db = d[:, b * SB_SUB:(b + 1) * SB_SUB]
                prefix = run + jnp.dot(_split2(db), before, preferred_element_type=F32)
                das.append(db - sig[:, b * SB_SUB:(b + 1) * SB_SUB] * (db + prefix))
                run = prefix[:, SB_SUB - 1:SB_SUB] + db[:, SB_SUB - 1:SB_SUB]
            da = jnp.concatenate(das, axis=1)
            if mask is not None:
                da = jnp.where(mask, da, 0.0)
            dab = (da * SB_SCALE).astype(BF16)
            dq = dq + jnp.dot(dab, k_ref[ksl, :], preferred_element_type=F32)
            dk_acc[ksl, :] += lax.dot_general(dab, q, (((0,), (0,)), ((), ())), preferred_element_type=F32)
            return run, dq

        run, dq = lax.fori_loop(0, i, lambda jb, c: left_to_right(jb, c[0], c[1], None),
                                (jnp.zeros((t, 1), F32), jnp.zeros((t, HEAD), F32)))
        _, dq = left_to_right(i, run, dq, _sb_diag_mask(t))
        dq_ref[...] = dq.astype(BF16)

        @pl.when(i == n_i - 1)
        def _():
            dk_ref[...] = (dk_acc[...] * (1.0 / SB_QUERY_SCALE)).astype(BF16)
            dv_ref[...] = dv_acc[...].astype(BF16)

    half = nh * HEAD
    full = pl.BlockSpec((S, HEAD), lambda h, i: (0, h))
    return pl.pallas_call(
        body, grid=(nh, n_i), name="sb_bwd",
        in_specs=[pl.BlockSpec((t, HEAD), lambda h, i: (i, nh + h)),
                  pl.BlockSpec((t, HEAD), lambda h, i: (i, m0 + h)),
                  pl.BlockSpec((S, HEAD), lambda h, i: (0, m0 + nh + h)),
                  pl.BlockSpec((S, HEAD), lambda h, i: (0, m0 + 2 * nh + h)),
                  pl.BlockSpec(before2.shape, lambda h, i: (0, 0)), pl.BlockSpec(memory_space=pl.ANY)],
        out_specs=[pl.BlockSpec((t, HEAD), lambda h, i: (i, h)), full, full],
        out_shape=[jax.ShapeDtypeStruct((S, half), BF16)] * 3,
        scratch_shapes=[pltpu.VMEM((S, HEAD), F32), pltpu.VMEM((S, HEAD), F32),
                        pltpu.VMEM((2, t, t), BF16), pltpu.SemaphoreType.DMA((2,))],
        compiler_params=_params(("arbitrary", "arbitrary")),
    )(dcat, projb, projb, projb, before2, w_all)


def local_step(x, target, mix_norm, ffn_norm, final_norm, lb_logits, hg_norm, get_w_in, get_w_rest, send,
               first_norm=None):
    S, D = x.shape
    half = D // 2
    nh = half // HEAD
    tm = ROW_TILE
    tk = REDUCE_TILE
    row = lambda i, j: (i, 0)

    lb = jax.nn.softmax(lb_logits, axis=0)[0:1]

    h0, r0 = rms_fwd(x, mix_norm[0:1], BF16) if first_norm is None else first_norm(x, mix_norm[0:1])
    w_in = get_w_in(h0)
    nbi = w_in.shape[2]
    col = jnp.arange(N_DEV * nbi) // half
    col_scale = jnp.where(col == 4, SB_QUERY_SCALE, 1.0).astype(F32)[None]
    proj, projb = matmul(
        "proj_in", [h0], [w_in], grid=(N_DEV, S // ROW_TILE_WIDE, 1),
        a_spec=pl.BlockSpec((ROW_TILE_WIDE, D), lambda j, i, k: (i, 0)),
        b_spec=pl.BlockSpec((None, D, nbi), lambda j, i, k: (j, 0, 0)),
        out_spec=pl.BlockSpec((ROW_TILE_WIDE, nbi), lambda j, i, k: (i, j)), out_shape=(S, N_DEV * nbi),
        out_dtypes=[F32, BF16], acc_shape=(8, 128),
        bf16_scale=col_scale, bf16_scale_spec=pl.BlockSpec((1, nbi), lambda j, i, k: (0, j)))
    oa, oraw, states = hgrn_forward(proj, lb, hg_norm)
    ob, sb_weights = sb_forward(projb, nh, 4 * nh)
    cat = jnp.concatenate([oa, ob], axis=1)
    w_out, pool_w, pool_scale, wg, wu, wd = get_w_rest(cat)
    x1, h1, r1 = matmul_residual_rms("mix_out", cat, w_out, x, ffn_norm[0:1])
    x2, ffn0 = ffn_forward(h1, x1, wg[0], wu[0], wd[0])

    h2, r2 = rms_fwd(x2, mix_norm[1:2], F32)
    x3, pooled, h3, r3 = pool_forward(h2, x2, pool_w, pool_scale, ffn_norm[1:2])
    x4, ffn1 = ffn_forward(h3, x3, wg[1], wu[1], wd[1])

    loss_blk, dx4, dx4b, d_final = loss_and_final_bwd(x4, final_norm, target)

    (dx3, _, d_ffn1), dwg1, dwu1, dwd1 = ffn_backward(dx4b, h3, ffn1, wg[1], wu[1], wd[1],
                                                      x3, r3, ffn_norm[1:2], dx4)
    dx3 = send("ffn1", dict(ffn_w_gate_1=dwg1, ffn_w_up_1=dwu1, ffn_w_down_1=dwd1), dx3)
    dmixed, dpooled, d_pscale = pool_backward_mix(dx3, pooled, pool_w, pool_scale)
    G = len(POOL_WINDOWS)
    P = D // G
    (d_pool_w,) = matmul(
        "pool_dw", [pooled], [dmixed], grid=(G, S // tk),
        a_spec=pl.BlockSpec((tk, P), lambda g, k: (k, g)), b_spec=pl.BlockSpec((tk, P), lambda g, k: (k, g)),
        out_spec=pl.BlockSpec((None, P, P), lambda g, k: (g, 0, 0)), out_shape=(G, P, P), out_dtypes=[BF16],
        acc_shape=(P, P), trans_a=True)
    dx2, dx2b, d_mix1 = pool_backward_window(dpooled, x2, r2, mix_norm[1:2], dx3)

    (dx1, dx1b, d_ffn0), dwg0, dwu0, dwd0 = ffn_backward(dx2b, h1, ffn0, wg[0], wu[0], wd[0],
                                                         x1, r1, ffn_norm[0:1], dx2)
    (dcat,) = matmul(
        "mix_out_dx", [dx1b], [w_out], grid=(S // tm, 1),
        a_spec=pl.BlockSpec((tm, D), row), b_spec=pl.BlockSpec((D, D), lambda i, k: (0, 0)),
        out_spec=pl.BlockSpec((tm, D), row), out_shape=(S, D), out_dtypes=[F32], acc_shape=(8, 128),
        trans_b=True)
    (d_w_out,) = matmul(
        "mix_out_dw", [cat], [dx1b], grid=(2, S // tk),
        a_spec=pl.BlockSpec((tk, half), lambda m, k: (k, m)), b_spec=pl.BlockSpec((tk, D), lambda m, k: (k, 0)),
        out_spec=pl.BlockSpec((half, D), lambda m, k: (m, 0)), out_shape=(D, D), out_dtypes=[BF16],
        acc_shape=(half, D), trans_a=True)
    dcat = send("layer0", dict(ffn_w_gate_0=dwg0, ffn_w_up_0=dwu0, ffn_w_down_0=dwd0, pool_w=d_pool_w,
                               ab_w_out=d_w_out), dcat)
    dqa, dfa, dia, dga, d_lb, d_hg = hgrn_backward(dcat, proj, oraw, states, lb, hg_norm)
    dqb, dkb, dvb = sb_backward(dcat, projb, sb_weights, nh, 4 * nh)
    dproj = jnp.concatenate([dqa, dfa, dia, dga, dqb, dkb, dvb], axis=1)
    (d_w_in,) = matmul(
        "proj_in_dw", [h0], [dproj], grid=(N_DEV, S // tk),
        a_spec=pl.BlockSpec((tk, D), lambda j, k: (k, 0)), b_spec=pl.BlockSpec((tk, nbi), lambda j, k: (k, j)),
        out_spec=pl.BlockSpec((None, D, nbi), lambda j, k: (j, 0, 0)), out_shape=(N_DEV, D, nbi),
        out_dtypes=[BF16], acc_shape=(D, nbi), trans_a=True)
    dproj = send("w_in", dict(ab_w_in=d_w_in), dproj)
    dx0, _, d_mix0 = matmul_rms_bwd(
        "proj_in_dx", [dproj], [w_in], grid=(S // tm, N_DEV),
        a_spec=pl.BlockSpec((tm, nbi), lambda i, j: (i, j)),
        b_spec=pl.BlockSpec((None, D, nbi), lambda i, j: (j, 0, 0)),
        tm=tm, x=x, r=r0, gain=mix_norm[0:1], dres=dx1)

    d_l0 = d_lb * lb * (1.0 - lb)
    small = dict(
        loss=loss_blk[0:1, 0:1],
        mix_norm=jnp.concatenate([d_mix0, d_mix1], axis=0),
        ffn_norm=jnp.concatenate([d_ffn0, d_ffn1], axis=0),
        final_norm=d_final,
        lb_logits=jnp.concatenate([d_l0, -d_l0], axis=0),
        hg_out_norm=jnp.sum(d_hg, axis=0),
        pool_scale=d_pscale,
    )
    return dx0, small


def _my_index():
    return 4 * lax.axis_index("x") + 2 * lax.axis_index("y") + lax.axis_index("c")


def _peer(r):
    x, y, c = lax.axis_index("x"), lax.axis_index("y"), lax.axis_index("c")
    px = 1 - x if (r >> 2) & 1 else x
    py = 1 - y if (r >> 1) & 1 else y
    pc = 1 - c if r & 1 else c
    return (px, py, pc), 4 * px + 2 * py + pc


def gather_two_level(name, shard):
    def body(x_ref, out_ref, send_sems, recv_sems, local_sem):
        x, y, c = lax.axis_index("x"), lax.axis_index("y"), lax.axis_index("c")
        me, sibling = (x, y, c), (x, y, 1 - c)
        chips = [(1 - x, y), (x, 1 - y), (1 - x, 1 - y)]

        def slot(px, py, pc):
            return out_ref.at[4 * px + 2 * py + pc]

        def copy(k, block, to, src=None):
            return pltpu.make_async_remote_copy(
                src_ref=slot(*block) if src is None else src, dst_ref=slot(*block), send_sem=send_sems.at[k],
                recv_sem=recv_sems.at[k], device_id=to, device_id_type=MESH)

        mine = pltpu.make_async_copy(x_ref, slot(*me), local_sem)
        mine.start()
        first = [copy(0, me, sibling, src=x_ref)]
        first += [copy(1 + j, me, (*chip, c), src=x_ref) for j, chip in enumerate(chips)]
        for cp in first:
            cp.start()
        passed = [copy(4 + j, (*chip, c), sibling) for j, chip in enumerate(chips)]
        for j, chip in enumerate(chips):
            copy(1 + j, (*chip, c), me).wait_recv()
            passed[j].start()
        copy(0, sibling, me).wait_recv()
        for j, chip in enumerate(chips):
            copy(4 + j, (*chip, 1 - c), me).wait_recv()
        for cp in first + passed:
            cp.wait_send()
        mine.wait()

    any_spec = pl.BlockSpec(memory_space=pl.ANY)
    return pl.pallas_call(
        body, name=name, in_specs=[any_spec], out_specs=any_spec,
        out_shape=jax.ShapeDtypeStruct((N_DEV,) + shard.shape, shard.dtype),
        scratch_shapes=[pltpu.SemaphoreType.DMA((N_DEV - 1,)), pltpu.SemaphoreType.DMA((N_DEV - 1,)),
                        pltpu.SemaphoreType.DMA],
    )(shard)


def gather_two_level_norm(name, shard, x, gain, ts=ROW_TILE):
    S, D = x.shape
    n_t = S // ts

    def body(s_ref, x_ref, g_ref, out_ref, h_ref, r_ref, send_sems, recv_sems, local_sem, xbuf, hbuf, in_sem,
             out_sem):
        x_, y_, c_ = lax.axis_index("x"), lax.axis_index("y"), lax.axis_index("c")
        me, sibling = (x_, y_, c_), (x_, y_, 1 - c_)
        chips = [(1 - x_, y_), (x_, 1 - y_), (1 - x_, 1 - y_)]

        def slot(px, py, pc):
            return out_ref.at[4 * px + 2 * py + pc]

        def copy(k, block, to, src=None):
            return pltpu.make_async_remote_copy(
                src_ref=slot(*block) if src is None else src, dst_ref=slot(*block), send_sem=send_sems.at[k],
                recv_sem=recv_sems.at[k], device_id=to, device_id_type=MESH)

        mine = pltpu.make_async_copy(s_ref, slot(*me), local_sem)
        mine.start()
        first = [copy(0, me, sibling, src=s_ref)]
        first += [copy(1 + j, me, (*chip, c_), src=s_ref) for j, chip in enumerate(chips)]
        for cp in first:
            cp.start()

        def load(i):
            return pltpu.make_async_copy(x_ref.at[pl.ds(i * ts, ts), :], xbuf.at[i % 2], in_sem.at[i % 2])

        def store(i):
            return pltpu.make_async_copy(hbuf.at[i % 2], h_ref.at[pl.ds(i * ts, ts), :], out_sem.at[i % 2])

        gv = g_ref[...]
        load(0).start()
        for i in range(n_t):
            load(i).wait()
            if i + 1 < n_t:
                load(i + 1).start()
            if i >= 2:
                store(i - 2).wait()
            xv = xbuf[i % 2]
            r = lax.rsqrt(jnp.mean(xv * xv, axis=-1, keepdims=True) + RMS_EPS)
            hbuf[i % 2] = ((xv * r) * gv).astype(BF16)
            r_ref[pl.ds(i * ts, ts), :] = r
            store(i).start()
        for i in range(max(n_t - 2, 0), n_t):
            store(i).wait()

        passed = [copy(4 + j, (*chip, c_), sibling) for j, chip in enumerate(chips)]
        for j, chip in enumerate(chips):
            copy(1 + j, (*chip, c_), me).wait_recv()
            passed[j].start()
        copy(0, sibling, me).wait_recv()
        for j, chip in enumerate(chips):
            copy(4 + j, (*chip, 1 - c_), me).wait_recv()
        for cp in first + passed:
            cp.wait_send()
        mine.wait()

    any_spec = pl.BlockSpec(memory_space=pl.ANY)
    vmem = pl.BlockSpec(memory_space=pltpu.VMEM)
    return pl.pallas_call(
        body, name=name, in_specs=[any_spec, any_spec, vmem], out_specs=[any_spec, any_spec, vmem],
        out_shape=[jax.ShapeDtypeStruct((N_DEV,) + shard.shape, shard.dtype), jax.ShapeDtypeStruct((S, D), BF16),
                   jax.ShapeDtypeStruct((S, 1), F32)],
        scratch_shapes=[pltpu.SemaphoreType.DMA((N_DEV - 1,)), pltpu.SemaphoreType.DMA((N_DEV - 1,)),
                        pltpu.SemaphoreType.DMA, pltpu.VMEM((2, ts, D), F32), pltpu.VMEM((2, ts, D), BF16),
                        pltpu.SemaphoreType.DMA((2,)), pltpu.SemaphoreType.DMA((2,))],
        compiler_params=pltpu.CompilerParams(vmem_limit_bytes=VMEM_LIMIT_BYTES),
    )(shard, x, gain)


def exchange(name, arrays, gather):
    n = len(arrays)
    n_peers = N_DEV - 1

    def body(*refs):
        ins, outs = refs[:n], refs[n:2 * n]
        send_sems, recv_sems, local_sems = refs[2 * n:]
        me = _my_index()
        local = []
        for a in range(n):
            src = ins[a] if gather else ins[a].at[me]
            cp = pltpu.make_async_copy(src, outs[a].at[me], local_sems.at[a])
            cp.start()
            local.append(cp)
        remote = []
        for a in range(n):
            for r in range(1, N_DEV):
                peer, pidx = _peer(r)
                src = ins[a] if gather else ins[a].at[pidx]
                cp = pltpu.make_async_remote_copy(
                    src_ref=src, dst_ref=outs[a].at[me], send_sem=send_sems.at[a * n_peers + r - 1],
                    recv_sem=recv_sems.at[a * n_peers + r - 1], device_id=peer, device_id_type=MESH)
                cp.start()
                remote.append((cp, a, r))
        for cp, a, r in remote:
            _, pidx = _peer(r)
            src = ins[a] if gather else ins[a].at[pidx]
            pltpu.make_async_remote_copy(
                src_ref=src, dst_ref=outs[a].at[pidx], send_sem=send_sems.at[a * n_peers + r - 1],
                recv_sem=recv_sems.at[a * n_peers + r - 1], device_id=_peer(r)[0], device_id_type=MESH).wait_recv()
        for cp, a, r in remote:
            cp.wait_send()
        for cp in local:
            cp.wait()

    out_shape = [jax.ShapeDtypeStruct(((N_DEV,) + a.shape) if gather else a.shape, a.dtype) for a in arrays]
    any_spec = pl.BlockSpec(memory_space=pl.ANY)
    return pl.pallas_call(
        body, name=name, in_specs=[any_spec] * n, out_specs=[any_spec] * n, out_shape=out_shape,
        scratch_shapes=[pltpu.SemaphoreType.DMA((n * n_peers,)), pltpu.SemaphoreType.DMA((n * n_peers,)),
                        pltpu.SemaphoreType.DMA((n,))],
    )(*arrays)


_HBM = pl.BlockSpec(memory_space=pltpu.HBM)
_SEM = pl.BlockSpec(memory_space=pltpu.SEMAPHORE)
_EFFECT = pltpu.SideEffectType.DATAFLOW_SIDE_EFFECTING


def _landing(arrays, gather):
    me = _my_index()
    lands = []
    for a in arrays:
        own = a[None] if gather else lax.dynamic_slice_in_dim(a, me, 1, axis=0)
        shape = ((N_DEV,) + a.shape) if gather else a.shape
        lands.append(lax.dynamic_update_slice_in_dim(lax.empty(shape, a.dtype), own, me, axis=0))
    return lands


def exchange_start(name, arrays, gather, carry):
    n = len(arrays)
    n_peers = N_DEV - 1
    lands = _landing(arrays, gather)
    n_thru = 2 * n + 1

    def body(*refs):
        src, land = refs[:n], refs[n:2 * n]
        send_sems, recv_sems = refs[n_thru], refs[n_thru + 1]
        token = refs[-1]
        me = _my_index()
        for a in range(n):
            for r in range(1, N_DEV):
                peer, pidx = _peer(r)
                pltpu.make_async_remote_copy(
                    src_ref=src[a] if gather else src[a].at[pidx], dst_ref=land[a].at[me],
                    send_sem=send_sems.at[a * n_peers + r - 1], recv_sem=recv_sems.at[a * n_peers + r - 1],
                    device_id=peer, device_id_type=MESH).start()
        token[...] = jnp.zeros_like(token)

    operands = list(arrays) + lands + [carry]
    outs = pl.pallas_call(
        body, name=name,
        out_shape=(pltpu.SemaphoreType.DMA((n * n_peers,)), pltpu.SemaphoreType.DMA((n * n_peers,)),
                   *[pltpu.HBM(a.shape, a.dtype) for a in operands], jax.ShapeDtypeStruct((8, 128), F32)),
        in_specs=[_HBM] * n_thru,
        out_specs=(_SEM, _SEM, *([_HBM] * n_thru), pl.BlockSpec(memory_space=pltpu.VMEM)),
        input_output_aliases={i: 2 + i for i in range(n_thru)},
        compiler_params=pltpu.CompilerParams(has_side_effects=_EFFECT),
    )(*[pltpu.with_memory_space_constraint(a, pltpu.HBM) for a in operands])
    handle = (outs[0], outs[1], list(outs[2:2 + n]), list(outs[2 + n:2 + 2 * n]), gather)
    return handle, outs[2 + 2 * n]


def exchange_wait(name, handle, after):
    send_sems, recv_sems, srcs, lands, gather = handle
    n = len(srcs)
    n_peers = N_DEV - 1

    def body(*refs):
        src, land = refs[:n], refs[n:2 * n]
        send_s, recv_s = refs[2 * n], refs[2 * n + 1]
        for a in range(n):
            for r in range(1, N_DEV):
                peer, pidx = _peer(r)
                cp = pltpu.make_async_remote_copy(
                    src_ref=src[a] if gather else src[a].at[pidx], dst_ref=land[a].at[pidx],
                    send_sem=send_s.at[a * n_peers + r - 1], recv_sem=recv_s.at[a * n_peers + r - 1],
                    device_id=peer, device_id_type=MESH)
                cp.wait_send()
                cp.wait_recv()

    shapes = [pltpu.HBM(a.shape, a.dtype) for a in srcs] + [pltpu.HBM(l.shape, l.dtype) for l in lands]
    outs = pl.pallas_call(
        body, name=name, out_shape=tuple(shapes),
        in_specs=[_HBM] * (2 * n) + [_SEM, _SEM, pl.BlockSpec(memory_space=pl.ANY)],
        out_specs=tuple([_HBM] * (2 * n)),
        input_output_aliases={i: i for i in range(2 * n)},
        compiler_params=pltpu.CompilerParams(has_side_effects=_EFFECT),
    )(*srcs, *lands, send_sems, recv_sems, after)
    return list(outs[n:])


def _row_tile(rows, cap=256):
    best = None
    for t in range(16, min(rows, cap) + 1, 16):
        if rows % t == 0:
            best = t
    return best if best is not None else rows


def sum_slots(name, recv):
    n, R, C = recv.shape
    tr = _row_tile(R)

    def body(r_ref, o_ref):
        g = r_ref[0].astype(F32)
        for d in range(1, n):
            g = g + r_ref[d].astype(F32)
        o_ref[...] = g

    return pl.pallas_call(
        body, grid=(R // tr,), name=name,
        in_specs=[pl.BlockSpec((n, tr, C), lambda i: (0, i, 0))],
        out_specs=pl.BlockSpec((tr, C), lambda i: (i, 0)),
        out_shape=jax.ShapeDtypeStruct((R, C), F32),
        compiler_params=_params(("arbitrary",)),
    )(recv)


def adamw(name, recv, w, m, v, layer=None, prev=None):
    n, R, C = recv.shape
    tr = _row_tile(R)

    def body(r_ref, w_ref, m_ref, v_ref, *rest):
        g_ref, d_ref, nm_ref, nv_ref = rest[-4:]
        g = r_ref[0].astype(F32)
        for d in range(1, n):
            g = g + r_ref[d].astype(F32)
        mm = ADAM_B1 * m_ref[...] + (1.0 - ADAM_B1) * g
        vv = ADAM_B2 * v_ref[...] + (1.0 - ADAM_B2) * (g * g)
        m_hat = mm / (1.0 - ADAM_B1 ** ADAM_STEP)
        v_hat = vv / (1.0 - ADAM_B2 ** ADAM_STEP)
        g_ref[...] = g
        d_ref[...] = -ADAM_LR * (m_hat / (jnp.sqrt(v_hat) + ADAM_EPS) + ADAM_WD * w_ref[...])
        nm_ref[...] = mm
        nv_ref[...] = vv

    if layer is None:
        row = pl.BlockSpec((tr, C), lambda i: (i, 0))
        shape = (R, C)
    else:
        row = pl.BlockSpec((None, tr, C), lambda i: (layer, i, 0))
        shape = w.shape
    prev = [] if prev is None else list(prev)
    return pl.pallas_call(
        body, grid=(R // tr,), name=name,
        in_specs=[pl.BlockSpec((n, tr, C), lambda i: (0, i, 0)), row, row, row]
                 + [pl.BlockSpec(memory_space=pl.ANY)] * len(prev),
        out_specs=[row] * 4,
        out_shape=[jax.ShapeDtypeStruct(shape, F32)] * 4,
        input_output_aliases={4 + o: o for o in range(len(prev))},
        compiler_params=_params(("arbitrary",)),
    )(recv, w, m, v, *prev)


def _adamw_nd(name, recv, w, m, v):
    shp = w.shape
    C = shp[-1]
    flat = lambda a: a.reshape(-1, C)
    outs = adamw(name, recv.reshape(recv.shape[0], -1, C), flat(w), flat(m), flat(v))
    return [o.reshape(shp) for o in outs]


_SMALL_NAMES = ("loss", "mix_norm", "ffn_norm", "final_norm", "lb_logits", "hg_out_norm", "pool_scale")
_LANES = 128


def _pack_small(parts):
    rows, layout = [], {}
    at = 0
    for name in parts:
        flat = parts[name].reshape(-1).astype(F32)
        n_rows = -(-flat.shape[0] // (8 * _LANES)) * 8
        flat = jnp.pad(flat, (0, n_rows * _LANES - flat.shape[0]))
        rows.append(flat.reshape(n_rows, _LANES))
        layout[name] = (at, parts[name].shape)
        at += n_rows
    return jnp.concatenate(rows, axis=0), layout


def _unpack_small(pack, layout):
    out = {}
    for name, (at, shape) in layout.items():
        size = int(np.prod(shape))
        n_rows = -(-size // _LANES)
        out[name] = pack[at:at + n_rows].reshape(-1)[:size].reshape(shape)
    return out


def kernel(x, mix_norm, ffn_norm, final_norm, ab_w_in, lb_logits, hg_out_norm, ab_w_out, pool_w, pool_scale, ffn_w_gate, ffn_w_up, ffn_w_down, loss_target, m_mix_norm, m_ffn_norm, m_final_norm, m_ab_w_in, m_lb_logits, m_hg_out_norm, m_ab_w_out, m_pool_w, m_pool_scale, m_ffn_w_gate, m_ffn_w_up, m_ffn_w_down, v_mix_norm, v_ffn_norm, v_final_norm, v_ab_w_in, v_lb_logits, v_hg_out_norm, v_ab_w_out, v_pool_w, v_pool_scale, v_ffn_w_gate, v_ffn_w_up, v_ffn_w_down):
    D = x.shape[-1]
    n_layers = ffn_w_gate.shape[0]
    G = pool_w.shape[1]
    P = pool_w.shape[3]
    me = _my_index()

    rest = [ab_w_out[0], pool_w[0]]
    for l in range(n_layers):
        rest += [ffn_w_gate[l], ffn_w_up[l], ffn_w_down[l]]
    rest = [s.astype(BF16) for s in rest] + [pool_scale]
    rest_handle = []

    gathered_in = []

    def first_norm(xs, gain):
        w_in, h0, r0 = gather_two_level_norm("gather_w_in", ab_w_in[0].astype(BF16), xs, gain)
        gathered_in.append(w_in)
        return h0, r0

    def get_w_in(after):
        handle, w_in = exchange_start("gather_rest_start", rest, True, gathered_in[0])
        rest_handle.append(handle)
        return w_in

    def get_w_rest(after):
        got = exchange_wait("gather_rest_wait", rest_handle[0], after)
        w_out_g = got[0].reshape(D, D)
        pool_g = got[1].transpose(1, 0, 2, 3).reshape(G, P, P)
        wg = [got[2 + 3 * l] for l in range(n_layers)]
        wu = [got[3 + 3 * l] for l in range(n_layers)]
        wd = [got[4 + 3 * l] for l in range(n_layers)]
        return w_out_g, pool_g, got[-1].reshape(1, D), wg, wu, wd

    in_flight = []

    def send(tag, grads, carry):
        if "pool_w" in grads:
            grads = dict(grads, pool_w=grads["pool_w"].reshape(G, N_DEV, P // N_DEV, P).transpose(1, 0, 2, 3))
        if "ab_w_out" in grads:
            grads = dict(grads, ab_w_out=grads["ab_w_out"].reshape(N_DEV, D // N_DEV, D))
        handle, carry = exchange_start("grads_" + tag + "_start", list(grads.values()), False, carry)
        in_flight.append((tag, list(grads.keys()), handle))
        return carry

    dx0, small = local_step(x[0], loss_target[0], mix_norm, ffn_norm, final_norm[None],
                            lb_logits, hg_out_norm, get_w_in, get_w_rest, send, first_norm)

    recv = {}
    for tag, names, handle in in_flight:
        recv.update(zip(names, exchange_wait("grads_" + tag + "_wait", handle, dx0)))
    small_pack, layout = _pack_small({k: small[k] for k in _SMALL_NAMES})
    (small_all,) = exchange("gather_small", [small_pack], gather=True)
    tot = _unpack_small(sum_slots("sum_small", small_all), layout)

    res = {}
    res["ab_w_in"] = _adamw_nd("adamw_w_in", recv["ab_w_in"], ab_w_in, m_ab_w_in, v_ab_w_in)
    res["ab_w_out"] = _adamw_nd("adamw_w_out", recv["ab_w_out"], ab_w_out, m_ab_w_out, v_ab_w_out)
    res["pool_w"] = _adamw_nd("adamw_pool_w", recv["pool_w"], pool_w, m_pool_w, v_pool_w)
    ffn_in = {"ffn_w_gate": (ffn_w_gate, m_ffn_w_gate, v_ffn_w_gate),
              "ffn_w_up": (ffn_w_up, m_ffn_w_up, v_ffn_w_up),
              "ffn_w_down": (ffn_w_down, m_ffn_w_down, v_ffn_w_down)}
    for name, (w, m, v) in ffn_in.items():
        flip = name != "ffn_w_down"
        if flip:
            w, m, v = (jnp.swapaxes(a, 1, 2) for a in (w, m, v))
        outs = None
        for l in range(n_layers):
            outs = adamw("adamw_" + name, recv[name + "_" + str(l)], w, m, v, layer=l, prev=outs)
        res[name] = [jnp.swapaxes(o, 1, 2) for o in outs] if flip else outs

    n_ps = pool_scale.shape[1]
    small_g = dict(tot)
    small_g["pool_scale"] = lax.dynamic_slice(tot["pool_scale"], (0, me * n_ps), (1, n_ps))
    small_w = dict(mix_norm=(mix_norm, m_mix_norm, v_mix_norm), ffn_norm=(ffn_norm, m_ffn_norm, v_ffn_norm),
                   final_norm=(final_norm, m_final_norm, v_final_norm),
                   lb_logits=(lb_logits, m_lb_logits, v_lb_logits),
                   hg_out_norm=(hg_out_norm, m_hg_out_norm, v_hg_out_norm),
                   pool_scale=(pool_scale, m_pool_scale, v_pool_scale))
    g_pack, lay2 = _pack_small({k: small_g[k].reshape(small_w[k][0].shape) for k in small_w})
    w_pack, _ = _pack_small({k: small_w[k][0] for k in small_w})
    m_pack, _ = _pack_small({k: small_w[k][1] for k in small_w})
    v_pack, _ = _pack_small({k: small_w[k][2] for k in small_w})
    small_out = [_unpack_small(o, lay2) for o in adamw("adamw_small", g_pack[None], w_pack, m_pack, v_pack)]
    for k in small_w:
        res[k] = [small_out[o][k] for o in range(4)]

    order = ("mix_norm", "ffn_norm", "final_norm", "ab_w_in", "lb_logits", "hg_out_norm", "ab_w_out", "pool_w",
             "pool_scale", "ffn_w_gate", "ffn_w_up", "ffn_w_down")
    outs = [tot["loss"].reshape(()), dx0[None]]
    for o in range(4):
        outs += [res[k][o] for k in order]
    return tuple(outs)
```
